```python
import math
import jax, jax.numpy as jnp
from jax import lax
import numpy as np

D_MODEL = 1024
BATCH = 2
SEQ = 16384
DEPTH = 2
DEC_BATCH = 32
DEC_SEQ = 16
PAST_LEN = 1024

CHUNK = 64
Q_BLOCK = 128
HEAD_DIM = 64
ROPE_THETA = 10000.0
H_A = 8
H_B = 4
DIFF_V = 2 * HEAD_DIM
H_C = 16
Q_RANK = 256
KV_RANK = 128
NOPE_DIM = 64
ROPE_DIM = 32
V_DIM_C = 64
N_GROUPS = 4
EXPERTS_PER_GROUP = 4
N_EXPERTS = N_GROUPS * EXPERTS_PER_GROUP
TOP_K = 2
D_EXPERT = 256
N_EVEN = (DEPTH + 1) // 2
N_ODD = DEPTH // 2
A_WIDTH = H_A * HEAD_DIM
B_QK_WIDTH = H_B * 2 * HEAD_DIM
B_V_WIDTH = H_B * DIFF_V
IN_AB = 3 * A_WIDTH + H_A + 2 * B_QK_WIDTH + B_V_WIDTH
MIX_AB = A_WIDTH + B_V_WIDTH
IN_C = Q_RANK + KV_RANK + ROPE_DIM
MIX_C = H_C * V_DIM_C
ALPHA = (2 * DEPTH) ** 0.25
BETA = (8 * DEPTH) ** -0.25
FGATE_BIAS = 3.0
LN_EPS = 1e-5
RMS_EPS = 1e-6
NEG_INF = -1e30

kernel_name = 'hybrid_fox_diff_mla_hmoe_stream_step'


def layer_norm(x, g, b):
    xf = x.astype(jnp.float32)
    mu = jnp.mean(xf, -1, keepdims=True)
    var = jnp.mean(jnp.square(xf - mu), -1, keepdims=True)
    return ((xf - mu) * lax.rsqrt(var + LN_EPS) * g + b).astype(x.dtype)


def rms_norm(x, g):
    xf = x.astype(jnp.float32)
    return (xf * lax.rsqrt(jnp.mean(jnp.square(xf), -1, keepdims=True) + RMS_EPS) * g).astype(x.dtype)


def rope(x, pos):
    d = x.shape[-1]
    inv = ROPE_THETA ** (-jnp.arange(0, d, 2, dtype=jnp.float32) / d)
    ang = pos.astype(jnp.float32)[:, None] * inv[None, :]
    shape = (1, x.shape[1]) + (1,) * (x.ndim - 3) + (d // 2,)
    cos = jnp.cos(ang).reshape(shape)
    sin = jnp.sin(ang).reshape(shape)
    xf = x.astype(jnp.float32)
    x1, x2 = xf[..., : d // 2], xf[..., d // 2:]
    return jnp.concatenate([x1 * cos - x2 * sin, x2 * cos + x1 * sin], -1).astype(x.dtype)


def causal_mask(qpos, kpos):
    return kpos[None, :] <= qpos[:, None]


def chunk_causal_mask(qpos, kpos):
    return (kpos[None, :] // CHUNK) <= (qpos[:, None] // CHUNK)


def masked_softmax(logits, mask):
    return jax.nn.softmax(jnp.where(mask, logits, NEG_INF), axis=-1)


def sweep_queries(fn, q_args, t):
    if t % Q_BLOCK != 0 or t <= Q_BLOCK:
        return fn(*q_args)
    nb = t // Q_BLOCK

    def body(i):
        s = i * Q_BLOCK
        return fn(*[lax.dynamic_slice_in_dim(a, s, Q_BLOCK, axis=1) for a in q_args])

    out = jnp.moveaxis(lax.map(body, jnp.arange(nb)), 0, 1)
    return out.reshape((out.shape[0], t) + out.shape[3:])


def fox_attend(q, cq, qpos, k, v, ck, kpos):
    s = jnp.einsum('bqhd,bkhd->bhqk', q, k, preferred_element_type=jnp.float32) * HEAD_DIM ** -0.5
    s = s + jnp.swapaxes(cq, 1, 2)[..., :, None] - jnp.swapaxes(ck, 1, 2)[..., None, :]
    p = masked_softmax(s, causal_mask(qpos[0], kpos))
    return jnp.einsum('bhqk,bkhd->bqhd', p.astype(v.dtype), v)


def diff_attend(q, qpos, k, v, kpos, lam):
    s = jnp.einsum('bqhnd,bkhnd->bnhqk', q, k, preferred_element_type=jnp.float32) * HEAD_DIM ** -0.5
    p = masked_softmax(s, chunk_causal_mask(qpos[0], kpos))
    a = p[:, 0] - lam * p[:, 1]
    return jnp.einsum('bhqk,bkhd->bqhd', a.astype(v.dtype), v)


def mla_attend(qn, qr, qpos, kn, kr, v, kpos):
    s = (jnp.einsum('bqhd,bkhd->bhqk', qn, kn, preferred_element_type=jnp.float32)
         + jnp.einsum('bqhr,bkr->bhqk', qr, kr, preferred_element_type=jnp.float32)) * (NOPE_DIM + ROPE_DIM) ** -0.5
    p = masked_softmax(s, chunk_causal_mask(qpos[0], kpos))
    return jnp.einsum('bhqk,bkhd->bqhd', p.astype(v.dtype), v)


def ab_mixer(x, cache, w_in, b_f, lq1, lk1, lq2, lk2, subln, w_out, lam_init):
    bsz, t, _ = x.shape
    pos0 = 0 if cache is None else cache[0].shape[1]
    pos = pos0 + jnp.arange(t)
    h = x @ w_in
    cuts = [A_WIDTH, 2 * A_WIDTH, 3 * A_WIDTH, 3 * A_WIDTH + H_A,
            3 * A_WIDTH + H_A + B_QK_WIDTH, 3 * A_WIDTH + H_A + 2 * B_QK_WIDTH]
    qa, ka, va, fa, qb, kb, vb = jnp.split(h, cuts, axis=-1)
    qa = qa.reshape(bsz, t, H_A, HEAD_DIM)
    ka = ka.reshape(bsz, t, H_A, HEAD_DIM)
    va = va.reshape(bsz, t, H_A, HEAD_DIM)
    logf = jax.nn.log_sigmoid(fa.astype(jnp.float32) + b_f.astype(jnp.float32))
    qb = rope(qb.reshape(bsz, t, H_B, 2, HEAD_DIM), pos)
    kb = rope(kb.reshape(bsz, t, H_B, 2, HEAD_DIM), pos)
    vb = vb.reshape(bsz, t, H_B, DIFF_V)
    if cache is None:
        kpos = pos
        k_a, v_a, lf, k_b, v_b = ka, va, logf, kb, vb
    else:
        ck_a, cv_a, cl, ck_b, cv_b = cache
        kpos = jnp.arange(pos0 + t)
        k_a = jnp.concatenate([ck_a, ka], 1)
        v_a = jnp.concatenate([cv_a, va], 1)
        lf = jnp.concatenate([cl.astype(jnp.float32), logf], 1)
        k_b = jnp.concatenate([ck_b, kb], 1)
        v_b = jnp.concatenate([cv_b, vb], 1)
    cum = lax.cumsum(lf, axis=1)
    cq = cum[:, -t:]
    qpos2 = pos[None, :]
    out_a = sweep_queries(lambda q, c, qp: fox_attend(q, c, qp, k_a, v_a, cum, kpos), (qa, cq, qpos2), t)
    f32 = jnp.float32
    lam = (jnp.exp(jnp.sum(lq1.astype(f32) * lk1.astype(f32)))
           - jnp.exp(jnp.sum(lq2.astype(f32) * lk2.astype(f32))) + lam_init)
    out_b = sweep_queries(lambda q, qp: diff_attend(q, qp, k_b, v_b, kpos, lam), (qb, qpos2), t)
    out_b = rms_norm(out_b, subln) * (1.0 - lam_init)
    mix = jnp.concatenate([out_a.reshape(bsz, t, A_WIDTH), out_b.reshape(bsz, t, B_V_WIDTH)], -1) @ w_out
    return mix, (ka, va, logf, kb, vb)


def mla_mixer(x, cache, w_in, g_q, g_kv, w_uq, w_ukv, w_out):
    bsz, t, _ = x.shape
    pos0 = 0 if cache is None else cache[0].shape[1]
    pos = pos0 + jnp.arange(t)
    h = x @ w_in
    qa = h[..., :Q_RANK]
    kva = h[..., Q_RANK:Q_RANK + KV_RANK]
    kra = h[..., Q_RANK + KV_RANK:]
    q = (rms_norm(qa, g_q) @ w_uq).reshape(bsz, t, H_C, NOPE_DIM + ROPE_DIM)
    qn = q[..., :NOPE_DIM]
    qr = rope(q[..., NOPE_DIM:], pos)
    ckv = rms_norm(kva, g_kv)
    kr = rope(kra[:, :, None, :], pos)[:, :, 0]
    if cache is None:
        kpos, c_all, kr_all = pos, ckv, kr
    else:
        kpos = jnp.arange(pos0 + t)
        c_all = jnp.concatenate([cache[0], ckv], 1)
        kr_all = jnp.concatenate([cache[1], kr], 1)
    kv = (c_all @ w_ukv).reshape(bsz, -1, H_C, NOPE_DIM + V_DIM_C)
    kn, v = kv[..., :NOPE_DIM], kv[..., NOPE_DIM:]
    out = sweep_queries(lambda a, b, qp: mla_attend(a, b, qp, kn, kr_all, v, kpos), (qn, qr, pos[None, :]), t)
    return out.reshape(bsz, t, MIX_C) @ w_out, (ckv, kr)


def hmoe(x, wg, bg, we, be, w1, w3, w2):
    bsz, t, d = x.shape
    xt = x.reshape(bsz * t, d)
    pg = jax.nn.softmax(jnp.dot(xt, wg, preferred_element_type=jnp.float32) + bg, axis=-1)
    g = jnp.argmax(pg, -1)
    p_g = jnp.take_along_axis(pg, g[:, None], -1)
    le = jnp.einsum('nd,gde->nge', xt, we, preferred_element_type=jnp.float32) + be
    le = jnp.take_along_axis(le, g[:, None, None], 1)[:, 0]
    w_top, i_top = lax.top_k(jax.nn.softmax(le, axis=-1), TOP_K)
    w_top = w_top / jnp.sum(w_top, -1, keepdims=True) * p_g
    eid = g[:, None] * EXPERTS_PER_GROUP + i_top
    gate = jnp.einsum('nk,nke->ne', w_top, jax.nn.one_hot(eid, N_EXPERTS, dtype=jnp.float32)).astype(x.dtype)
    out = jnp.zeros_like(xt)
    for e in range(N_EXPERTS):
        hdn = jax.nn.silu(xt @ w1[e]) * (xt @ w3[e])
        out = out + gate[:, e:e + 1] * (hdn @ w2[e])
    return out.reshape(bsz, t, d)


def setup_inputs(seed: int = 0) -> dict:
    key = jax.random.key(seed)
    keys = iter(jax.random.split(key, 64))

    def nrm(shape, scale=1.0):
        return scale * jax.random.normal(next(keys), shape, jnp.float32)

    def gain(shape):
        return 1.0 + 0.01 * nrm(shape)

    dh = HEAD_DIM
    return {
        'x_prompt': nrm((BATCH, SEQ, D_MODEL)),
        'x_sample': nrm((DEC_BATCH, DEC_SEQ, D_MODEL)),
        'cache_fox_k': nrm((N_EVEN, DEC_BATCH, PAST_LEN, H_A, dh)),
        'cache_fox_v': nrm((N_EVEN, DEC_BATCH, PAST_LEN, H_A, dh)),
        'cache_fox_logf': jax.nn.log_sigmoid(FGATE_BIAS + nrm((N_EVEN, DEC_BATCH, PAST_LEN, H_A))),
        'cache_diff_k': nrm((N_EVEN, DEC_BATCH, PAST_LEN, H_B, 2, dh)),
        'cache_diff_v': nrm((N_EVEN, DEC_BATCH, PAST_LEN, H_B, DIFF_V)),
        'cache_mla_ckv': nrm((N_ODD, DEC_BATCH, PAST_LEN, KV_RANK)),
        'cache_mla_krope': nrm((N_ODD, DEC_BATCH, PAST_LEN, ROPE_DIM)),
        'w_in_ab': nrm((N_EVEN, D_MODEL, IN_AB), D_MODEL ** -0.5),
        'b_fgate': FGATE_BIAS + 0.1 * nrm((N_EVEN, H_A)),
        'diff_lq1': nrm((N_EVEN, dh), 0.1),
        'diff_lk1': nrm((N_EVEN, dh), 0.1),
        'diff_lq2': nrm((N_EVEN, dh), 0.1),
        'diff_lk2': nrm((N_EVEN, dh), 0.1),
        'diff_subln': gain((N_EVEN, DIFF_V)),
        'w_out_ab': nrm((N_EVEN, MIX_AB, D_MODEL), BETA * MIX_AB ** -0.5),
        'w_in_c': nrm((N_ODD, D_MODEL, IN_C), D_MODEL ** -0.5),
        'mla_q_norm': gain((N_ODD, Q_RANK)),
        'mla_kv_norm': gain((N_ODD, KV_RANK)),
        'mla_w_uq': nrm((N_ODD, Q_RANK, H_C * (NOPE_DIM + ROPE_DIM)), Q_RANK ** -0.5),
        'mla_w_ukv': nrm((N_ODD, KV_RANK, H_C * (NOPE_DIM + V_DIM_C)), KV_RANK ** -0.5),
        'w_out_c': nrm((N_ODD, MIX_C, D_MODEL), BETA * MIX_C ** -0.5),
        'ln1_g': gain((DEPTH, D_MODEL)),
        'ln1_b': nrm((DEPTH, D_MODEL), 0.01),
        'ln2_g': gain((DEPTH, D_MODEL)),
        'ln2_b': nrm((DEPTH, D_MODEL), 0.01),
        'moe_wg': nrm((DEPTH, D_MODEL, N_GROUPS), D_MODEL ** -0.5),
        'moe_bg': nrm((DEPTH, N_GROUPS), 0.01),
        'moe_we': nrm((DEPTH, N_GROUPS, D_MODEL, EXPERTS_PER_GROUP), D_MODEL ** -0.5),
        'moe_be': nrm((DEPTH, N_GROUPS, EXPERTS_PER_GROUP), 0.01),
        'moe_w1': nrm((DEPTH, N_EXPERTS, D_MODEL, D_EXPERT), D_MODEL ** -0.5),
        'moe_w3': nrm((DEPTH, N_EXPERTS, D_MODEL, D_EXPERT), D_MODEL ** -0.5),
        'moe_w2': nrm((DEPTH, N_EXPERTS, D_EXPERT, D_MODEL), BETA * D_EXPERT ** -0.5),
    }


def reference(x_prompt, x_sample, cache_fox_k, cache_fox_v, cache_fox_logf, cache_diff_k, cache_diff_v,
              cache_mla_ckv, cache_mla_krope, w_in_ab, b_fgate, diff_lq1, diff_lk1, diff_lq2, diff_lk2,
              diff_subln, w_out_ab, w_in_c, mla_q_norm, mla_kv_norm, mla_w_uq, mla_w_ukv, w_out_c,
              ln1_g, ln1_b, ln2_g, ln2_b, moe_wg, moe_bg, moe_we, moe_be, moe_w1, moe_w3, moe_w2):
    xp, xs = x_prompt, x_sample
    ab_rows_p, ab_rows_s, c_rows_p, c_rows_s = [], [], [], []
    for i in range(DEPTH):
        j = i // 2
        if i % 2 == 0:
            lam_init = 0.8 - 0.6 * math.exp(-0.3 * i)
            wts = (w_in_ab[j], b_fgate[j], diff_lq1[j], diff_lk1[j], diff_lq2[j], diff_lk2[j],
                   diff_subln[j], w_out_ab[j], lam_init)
            mp, rp = ab_mixer(xp, None, *wts)
            ms, rs = ab_mixer(xs, (cache_fox_k[j], cache_fox_v[j], cache_fox_logf[j],
                                   cache_diff_k[j], cache_diff_v[j]), *wts)
            ab_rows_p.append(rp)
            ab_rows_s.append(rs)
        else:
            wts = (w_in_c[j], mla_q_norm[j], mla_kv_norm[j], mla_w_uq[j], mla_w_ukv[j], w_out_c[j])
            mp, rp = mla_mixer(xp, None, *wts)
            ms, rs = mla_mixer(xs, (cache_mla_ckv[j], cache_mla_krope[j]), *wts)
            c_rows_p.append(rp)
            c_rows_s.append(rs)
        xp = layer_norm(ALPHA * xp + mp, ln1_g[i], ln1_b[i])
        xs = layer_norm(ALPHA * xs + ms, ln1_g[i], ln1_b[i])
        moe_w = (moe_wg[i], moe_bg[i], moe_we[i], moe_be[i], moe_w1[i], moe_w3[i], moe_w2[i])
        xp = layer_norm(ALPHA * xp + hmoe(xp, *moe_w), ln2_g[i], ln2_b[i])
        xs = layer_norm(ALPHA * xs + hmoe(xs, *moe_w), ln2_g[i], ln2_b[i])

    def stack(rows, n):
        return jnp.stack([r[n] for r in rows])

    new_fox_k_p, new_fox_v_p, new_fox_logf_p = stack(ab_rows_p, 0), stack(ab_rows_p, 1), stack(ab_rows_p, 2)
    new_diff_k_p, new_diff_v_p = stack(ab_rows_p, 3), stack(ab_rows_p, 4)
    new_mla_ckv_p, new_mla_krope_p = stack(c_rows_p, 0), stack(c_rows_p, 1)
    new_fox_k_s, new_fox_v_s, new_fox_logf_s = stack(ab_rows_s, 0), stack(ab_rows_s, 1), stack(ab_rows_s, 2)
    new_diff_k_s, new_diff_v_s = stack(ab_rows_s, 3), stack(ab_rows_s, 4)
    new_mla_ckv_s, new_mla_krope_s = stack(c_rows_s, 0), stack(c_rows_s, 1)
    return (xp, xs,
            new_fox_k_p, new_fox_v_p, new_fox_logf_p, new_diff_k_p, new_diff_v_p, new_mla_ckv_p, new_mla_krope_p,
            new_fox_k_s, new_fox_v_s, new_fox_logf_s, new_diff_k_s, new_diff_v_s, new_mla_ckv_s, new_mla_krope_s)
```

```python
import functools
import math

import jax
import jax.numpy as jnp
from jax import lax
from jax.experimental import pallas as pl
from jax.experimental.pallas import tpu as pltpu

F32 = jnp.float32
BF16 = jnp.bfloat16

D_MODEL = 1024
CHUNK = 64
HEAD_DIM = 64
ROPE_THETA = 10000.0
H_A = 8
H_B = 4
H_C = 16
Q_RANK = 256
KV_RANK = 128
NOPE_DIM = 64
ROPE_DIM = 32
V_DIM_C = 64
N_GROUPS = 4
EXPERTS_PER_GROUP = 4
N_EXPERTS = N_GROUPS * EXPERTS_PER_GROUP
D_EXPERT = 256
A_WIDTH = H_A * HEAD_DIM
B_QK_WIDTH = H_B * 2 * HEAD_DIM
B_V_WIDTH = H_B * 2 * HEAD_DIM
FGATE_BIAS = 3.0
LN_EPS = 1e-5
RMS_EPS = 1e-6
NEG_INF = -1e30

LANES = 128
VMEM_LIMIT = 48 * 1024 * 1024
ATTN_BLOCK = 512
ROW_TILE = 512
MOE_TILE = 1024
GATE_LANE0 = N_GROUPS


def _cparams(sem):
    return pltpu.CompilerParams(dimension_semantics=sem, vmem_limit_bytes=VMEM_LIMIT)


def _rope3(x, c, s1, s2, shift_up, shift_down):
    return x * c + pltpu.roll(x, shift_up, 1) * s1 + pltpu.roll(x, shift_down, 1) * s2


def _layer_norm(y, g, b):
    mu = jnp.mean(y, axis=-1, keepdims=True)
    d = y - mu
    var = jnp.mean(d * d, axis=-1, keepdims=True)
    return d * lax.rsqrt(var + LN_EPS) * g + b


def _proj_ab_kernel(x_ref, w_ref, wf_ref, bf_ref, c_ref, s1_ref, s2_ref,
                    qa_ref, ka_ref, kab_ref, va_ref, vab_ref, lf_ref,
                    qb_ref, kb_ref, kbb_ref, vb_ref, vbb_ref):
    xb = x_ref[0].astype(BF16)

    def mm(i):
        return jnp.dot(xb, w_ref[i], preferred_element_type=F32)

    qa_ref[0] = (mm(0) * HEAD_DIM ** -0.5).astype(BF16)
    ka = mm(1)
    ka_ref[0] = ka
    kab_ref[0] = ka.astype(BF16)
    va = mm(2)
    va_ref[0] = va
    vab_ref[0] = va.astype(BF16)

    z = jnp.dot(xb, wf_ref[...], preferred_element_type=F32) + bf_ref[...]
    lf = jnp.minimum(z, 0.0) - jnp.log1p(jnp.exp(-jnp.abs(z)))
    lf_ref[0] = lf[:, :H_A]

    c, s1, s2 = c_ref[...], s1_ref[...], s2_ref[...]
    qb = mm(3)
    kb = mm(4)
    for s in range(B_QK_WIDTH // LANES):
        sl = slice(s * LANES, (s + 1) * LANES)
        qs = _rope3(qb[:, sl], c, s1, s2, LANES - HEAD_DIM // 2, HEAD_DIM // 2)
        qb_ref[0, :, sl] = (qs * HEAD_DIM ** -0.5).astype(BF16)
        ks = _rope3(kb[:, sl], c, s1, s2, LANES - HEAD_DIM // 2, HEAD_DIM // 2)
        kb_ref[0, :, sl] = ks
        kbb_ref[0, :, sl] = ks.astype(BF16)
    vb = mm(5)
    vb_ref[0] = vb
    vbb_ref[0] = vb.astype(BF16)


def _proj_ab(x, w6, wf, bf, tabs):
    nb, t, _ = x.shape
    tm = min(ROW_TILE, t)
    assert t % tm == 0
    w = A_WIDTH
    tok = lambda width: pl.BlockSpec((1, tm, width), lambda b, i: (b, i, 0))
    tab = pl.BlockSpec((tm, LANES), lambda b, i: (i, 0))
    full = lambda a: pl.BlockSpec(a.shape, lambda b, i: (0,) * a.ndim)
    sds = lambda width, dt: jax.ShapeDtypeStruct((nb, t, width), dt)
    return pl.pallas_call(
        _proj_ab_kernel,
        grid=(nb, t // tm),
        in_specs=[tok(D_MODEL), full(w6), full(wf), full(bf), tab, tab, tab],
        out_specs=[tok(w), tok(w), tok(w), tok(w), tok(w), tok(H_A), tok(w), tok(w), tok(w), tok(w), tok(w)],
        out_shape=[sds(w, BF16), sds(w, F32), sds(w, BF16), sds(w, F32), sds(w, BF16), sds(H_A, F32),
                   sds(w, BF16), sds(w, F32), sds(w, BF16), sds(w, F32), sds(w, BF16)],
        compiler_params=_cparams(("parallel", "parallel")),
        name="proj_ab",
    )(x, w6, wf, bf, *tabs)


def _cumsum_kernel(x_ref, o_ref, carry_ref):
    @pl.when(pl.program_id(0) == 0)
    def _():
        carry_ref[...] = jnp.zeros_like(carry_ref)

    x = x_ref[...]
    tc = x.shape[1]
    hi = x.astype(BF16)
    r1 = x - hi.astype(F32)
    mid = r1.astype(BF16)
    lo = (r1 - mid.astype(F32)).astype(BF16)
    row = lax.broadcasted_iota(jnp.int32, (tc, tc), 0)
    col = lax.broadcasted_iota(jnp.int32, (tc, tc), 1)
    upper = (row <= col).astype(BF16)
    dot = lambda a: jnp.dot(a, upper, preferred_element_type=F32)
    c = dot(hi) + dot(mid) + dot(lo) + carry_ref[...]
    o_ref[...] = c
    carry_ref[...] = c[:, tc - 1:tc]


def _cumsum_lanes(x):
    r, t = x.shape
    tc = min(ATTN_BLOCK, t)
    assert t % tc == 0
    return pl.pallas_call(
        _cumsum_kernel,
        grid=(t // tc,),
        in_specs=[pl.BlockSpec((r, tc), lambda i: (0, i))],
        out_specs=pl.BlockSpec((r, tc), lambda i: (0, i)),
        out_shape=jax.ShapeDtypeStruct((r, t), F32),
        scratch_shapes=[pltpu.VMEM((r, 1), F32)],
        compiler_params=_cparams(("arbitrary",)),
        name="cumsum",
    )(x)


def _attn_kernel(*refs, mode, tq, tk, mask_shift, nfull_static, lam_init):
    if mode == "fox":
        q_ref, k_ref, v_ref, c_ref, o_ref, m_sc, l_sc, acc_sc = refs
    elif mode == "diff":
        q_ref, k_ref, v_ref, lq1_ref, lk1_ref, lq2_ref, lk2_ref, sub_ref, o_ref, m_sc, l_sc, acc_sc = refs
    else:
        q_ref, k_ref, v_ref, o_ref, m_sc, l_sc, acc_sc = refs

    qi = pl.program_id(2)
    q = q_ref[0]
    lane = lax.broadcasted_iota(jnp.int32, (1, LANES), 1)
    if mode == "mla":
        qs = (q[:, :LANES], q[:, LANES:])
    else:
        zero = jnp.zeros_like(q)
        qs = (jnp.where(lane < HEAD_DIM, q, zero), jnp.where(lane >= HEAD_DIM, q, zero))

    m_sc[...] = jnp.full(m_sc.shape, NEG_INF, F32)
    l_sc[...] = jnp.zeros(l_sc.shape, F32)
    acc_sc[...] = jnp.zeros(acc_sc.shape, F32)

    def step(j, masked):
        k = k_ref[0, j]
        v = v_ref[0, j]
        for i in range(2):
            ki = k[:, i * LANES:(i + 1) * LANES] if mode == "mla" else k
            s = lax.dot_general(qs[i], ki, (((1,), (1,)), ((), ())), preferred_element_type=F32)
            if mode == "fox":
                s = s - c_ref[0, 0, j, i:i + 1, :]
            if masked:
                row = lax.broadcasted_iota(jnp.int32, (tq, tk), 0)
                col = lax.broadcasted_iota(jnp.int32, (tq, tk), 1)
                vis = lax.shift_right_logical(col, mask_shift) <= lax.shift_right_logical(row, mask_shift)
                s = jnp.where(vis, s, NEG_INF)
            m_prev = m_sc[i]
            m_new = jnp.maximum(m_prev, jnp.max(s, axis=1, keepdims=True))
            alpha = jnp.exp(m_prev - m_new)
            p = jnp.exp(s - m_new)
            l_sc[i] = alpha * l_sc[i] + jnp.sum(p, axis=1, keepdims=True)
            acc_sc[i] = alpha * acc_sc[i] + jnp.dot(p.astype(BF16), v, preferred_element_type=F32)
            m_sc[i] = m_new

    nfull = qi if nfull_static is None else nfull_static

    def body(j, carry):
        step(j, False)
        return carry

    lax.fori_loop(0, nfull, body, 0)
    step(nfull, True)

    o0 = acc_sc[0] / l_sc[0]
    o1 = acc_sc[1] / l_sc[1]
    if mode == "diff":
        lam = (jnp.exp(jnp.sum(lq1_ref[...] * lk1_ref[...], axis=1, keepdims=True))
               - jnp.exp(jnp.sum(lq2_ref[...] * lk2_ref[...], axis=1, keepdims=True)) + lam_init)
        o = o0 - lam * o1
        ms = jnp.mean(o * o, axis=1, keepdims=True)
        o = o * lax.rsqrt(ms + RMS_EPS) * sub_ref[...] * (1.0 - lam_init)
    else:
        o = jnp.where(lane < HEAD_DIM, o0, o1)
    o_ref[0] = o.astype(o_ref.dtype)


def _attention(mode, q, k, v, extra, *, n_pairs, mask_shift, nfull_static=None, lam_init=0.0):
    nb, t_q, _ = q.shape
    _, nkb, tk, _ = k.shape
    tq = min(ATTN_BLOCK, t_q)
    assert t_q % tq == 0
    if nfull_static is None:
        assert tq == tk and nkb == t_q // tq
    qw = 2 * LANES if mode == "mla" else LANES
    in_specs = [
        pl.BlockSpec((1, tq, qw), lambda b, p, i: (b, i, p)),
        pl.BlockSpec((1, nkb, tk, qw), lambda b, p, i: (b, 0, 0, p)),
        pl.BlockSpec((1, nkb, tk, LANES), lambda b, p, i: (b, 0, 0, p)),
    ]
    if mode == "fox":
        in_specs.append(pl.BlockSpec((1, 1, nkb, 2, tk), lambda b, p, i: (b, p, 0, 0, 0)))
    elif mode == "diff":
        in_specs += [pl.BlockSpec(a.shape, lambda b, p, i: (0, 0)) for a in extra]
    kern = functools.partial(_attn_kernel, mode=mode, tq=tq, tk=tk, mask_shift=mask_shift,
                             nfull_static=nfull_static, lam_init=lam_init)
    return pl.pallas_call(
        kern,
        grid=(nb, n_pairs, t_q // tq),
        in_specs=in_specs,
        out_specs=pl.BlockSpec((1, tq, LANES), lambda b, p, i: (b, i, p)),
        out_shape=jax.ShapeDtypeStruct((nb, t_q, n_pairs * LANES), BF16),
        scratch_shapes=[pltpu.VMEM((2, tq, 1), F32), pltpu.VMEM((2, tq, 1), F32),
                        pltpu.VMEM((2, tq, LANES), F32)],
        compiler_params=_cparams(("parallel", "parallel", "arbitrary")),
        name="attn_" + mode,
    )(q, k, v, *extra)


def _outproj_ln_kernel(*refs, n_in, alpha):
    x_ref = refs[0]
    o_refs = refs[1:1 + n_in]
    w_refs = refs[1 + n_in:1 + 2 * n_in]
    g_ref, b_ref, y_ref = refs[1 + 2 * n_in:]
    mix = jnp.dot(o_refs[0][...], w_refs[0][...], preferred_element_type=F32)
    for o_r, w_r in zip(o_refs[1:], w_refs[1:]):
        mix = mix + jnp.dot(o_r[...], w_r[...], preferred_element_type=F32)
    y_ref[...] = _layer_norm(alpha * x_ref[...] + mix, g_ref[...], b_ref[...])


def _outproj_ln(x, outs, ws, g, b, alpha):
    n, d = x.shape
    tm = min(ROW_TILE, n)
    assert n % tm == 0
    row = lambda width: pl.BlockSpec((tm, width), lambda i: (i, 0))
    full = lambda a: pl.BlockSpec(a.shape, lambda i: (0, 0))
    return pl.pallas_call(
        functools.partial(_outproj_ln_kernel, n_in=len(outs), alpha=alpha),
        grid=(n // tm,),
        in_specs=[row(d)] + [row(o.shape[1]) for o in outs] + [full(w) for w in ws] + [full(g), full(b)],
        out_specs=row(d),
        out_shape=jax.ShapeDtypeStruct((n, d), F32),
        compiler_params=_cparams(("parallel",)),
        name="outproj_ln",
    )(x, *outs, *ws, g, b)


def _route(logits):
    lane = lax.broadcasted_iota(jnp.int32, logits.shape, 1).astype(F32)
    big = float(1 << 20)
    is_g = lane < N_GROUPS
    lg = jnp.where(is_g, logits, NEG_INF)
    eg = jnp.where(is_g, jnp.exp(lg - jnp.max(lg, axis=1, keepdims=True)), 0.0)
    pg = eg / jnp.sum(eg, axis=1, keepdims=True)
    p_g = jnp.max(pg, axis=1, keepdims=True)
    gidx = jnp.min(jnp.where(is_g & (pg == p_g), lane, big), axis=1, keepdims=True)
    lo = GATE_LANE0 + EXPERTS_PER_GROUP * gidx
    sel = (lane >= lo) & (lane < lo + EXPERTS_PER_GROUP)
    le = jnp.where(sel, logits, NEG_INF)
    ee = jnp.where(sel, jnp.exp(le - jnp.max(le, axis=1, keepdims=True)), 0.0)
    pe = ee / jnp.sum(ee, axis=1, keepdims=True)
    v1 = jnp.max(jnp.where(sel, pe, -1.0), axis=1, keepdims=True)
    i1 = jnp.min(jnp.where(sel & (pe == v1), lane, big), axis=1, keepdims=True)
    rest = sel & (lane != i1)
    v2 = jnp.max(jnp.where(rest, pe, -1.0), axis=1, keepdims=True)
    i2 = jnp.min(jnp.where(rest & (pe == v2), lane, big), axis=1, keepdims=True)
    tot = v1 + v2
    w1 = v1 / tot * p_g
    w2 = v2 / tot * p_g
    return jnp.where(lane == i1, w1, jnp.where(lane == i2, w2, 0.0))


def _moe_ln_kernel(x_ref, wrh_ref, wrl_ref, br_ref, w13_ref, w2_ref, g_ref, b_ref, y_ref,
                   xb_sc, gate_sc, acc_sc, *, alpha):
    e = pl.program_id(1)

    @pl.when(e == 0)
    def _():
        x = x_ref[...]
        xh = x.astype(BF16)
        xl = (x - xh.astype(F32)).astype(BF16)
        xb_sc[...] = xh
        logits = (jnp.dot(xh, wrh_ref[...], preferred_element_type=F32)
                  + jnp.dot(xl, wrh_ref[...], preferred_element_type=F32)
                  + jnp.dot(xh, wrl_ref[...], preferred_element_type=F32) + br_ref[...])
        gate_sc[...] = _route(logits)
        acc_sc[...] = jnp.zeros_like(acc_sc)

    h = jnp.dot(xb_sc[...], w13_ref[0], preferred_element_type=F32)
    h1 = h[:, :D_EXPERT]
    h3 = h[:, D_EXPERT:]
    hdn = (h1 * jax.nn.sigmoid(h1)) * h3
    y = jnp.dot(hdn.astype(BF16), w2_ref[0], preferred_element_type=F32)
    lane = lax.broadcasted_iota(jnp.int32, (1, LANES), 1)
    ge = jnp.sum(jnp.where(lane == e + GATE_LANE0, gate_sc[...], 0.0), axis=1, keepdims=True)
    acc_sc[...] += ge * y

    @pl.when(e == pl.num_programs(1) - 1)
    def _():
        y_ref[...] = _layer_norm(alpha * x_ref[...] + acc_sc[...], g_ref[...], b_ref[...])


def _moe_ln(x, wrh, wrl, br, w13, w2, g, b, alpha):
    n, d = x.shape
    tm = min(MOE_TILE, n)
    assert n % tm == 0
    ne = w13.shape[0]
    full = lambda a: pl.BlockSpec(a.shape, lambda i, e: (0, 0))
    return pl.pallas_call(
        functools.partial(_moe_ln_kernel, alpha=alpha),
        grid=(n // tm, ne),
        in_specs=[pl.BlockSpec((tm, d), lambda i, e: (i, 0)), full(wrh), full(wrl), full(br),
                  pl.BlockSpec((1,) + w13.shape[1:], lambda i, e: (e, 0, 0)),
                  pl.BlockSpec((1,) + w2.shape[1:], lambda i, e: (e, 0, 0)),
                  full(g), full(b)],
        out_specs=pl.BlockSpec((tm, d), lambda i, e: (i, 0)),
        out_shape=jax.ShapeDtypeStruct((n, d), F32),
        scratch_shapes=[pltpu.VMEM((tm, d), BF16), pltpu.VMEM((tm, LANES), F32), pltpu.VMEM((tm, d), F32)],
        compiler_params=_cparams(("parallel", "arbitrary")),
        name="moe_ln",
    )(x, wrh, wrl, br, w13, w2, g, b)


def _proj_c_kernel(x_ref, win_ref, gq_ref, gkv_ref, wuq_ref, cq_ref, s1q_ref, s2q_ref,
                   ck_ref, s1k_ref, s2k_ref, q_ref, ckv_ref, kr_ref):
    xb = x_ref[0].astype(BF16)
    h = jnp.dot(xb, win_ref[...], preferred_element_type=F32)
    qa = h[:, :Q_RANK]
    kva = h[:, Q_RANK:Q_RANK + KV_RANK]
    krw = h[:, Q_RANK + KV_RANK:]
    qn = qa * lax.rsqrt(jnp.mean(qa * qa, axis=1, keepdims=True) + RMS_EPS) * gq_ref[...]
    ckv_ref[0] = kva * lax.rsqrt(jnp.mean(kva * kva, axis=1, keepdims=True) + RMS_EPS) * gkv_ref[...]
    half = ROPE_DIM // 2
    kr = _rope3(krw, ck_ref[...], s1k_ref[...], s2k_ref[...], LANES - half, half)
    kr_ref[0] = kr[:, :ROPE_DIM]
    q = jnp.dot(qn.astype(BF16), wuq_ref[...], preferred_element_type=F32)
    cq, s1q, s2q = cq_ref[...], s1q_ref[...], s2q_ref[...]
    scale = (NOPE_DIM + ROPE_DIM) ** -0.5
    for hd in range(H_C):
        sl = slice(hd * LANES, (hd + 1) * LANES)
        q_ref[0, :, sl] = (_rope3(q[:, sl], cq, s1q, s2q, LANES - half, half) * scale).astype(BF16)


def _proj_c(x, win, gq, gkv, wuq, tabs_q, tabs_k):
    nb, t, _ = x.shape
    tm = min(ROW_TILE, t)
    assert t % tm == 0
    tok = lambda width: pl.BlockSpec((1, tm, width), lambda b, i: (b, i, 0))
    tab = pl.BlockSpec((tm, LANES), lambda b, i: (i, 0))
    full = lambda a: pl.BlockSpec(a.shape, lambda b, i: (0, 0))
    return pl.pallas_call(
        _proj_c_kernel,
        grid=(nb, t // tm),
        in_specs=[tok(D_MODEL), full(win), full(gq), full(gkv), full(wuq)] + [tab] * 6,
        out_specs=[tok(H_C * LANES), tok(KV_RANK), tok(ROPE_DIM)],
        out_shape=[jax.ShapeDtypeStruct((nb, t, H_C * LANES), BF16),
                   jax.ShapeDtypeStruct((nb, t, KV_RANK), F32),
                   jax.ShapeDtypeStruct((nb, t, ROPE_DIM), F32)],
        compiler_params=_cparams(("parallel", "parallel")),
        name="proj_c",
    )(x, win, gq, gkv, wuq, *tabs_q, *tabs_k)


def _kv_up_kernel(ckv_ref, kr_ref, wk_ref, place_ref, wv_ref, k_ref, v_ref):
    cb = ckv_ref[...].astype(BF16)
    k = (jnp.dot(cb, wk_ref[...], preferred_element_type=F32)
         + jnp.dot(kr_ref[...].astype(BF16), place_ref[...], preferred_element_type=F32))
    k_ref[...] = k.astype(BF16)
    v_ref[...] = jnp.dot(cb, wv_ref[...], preferred_element_type=F32).astype(BF16)


def _kv_up(ckv, kr, wk, place, wv):
    n = ckv.shape[0]
    tm = min(ROW_TILE, n)
    assert n % tm == 0
    row = lambda width: pl.BlockSpec((tm, width), lambda i: (i, 0))
    full = lambda a: pl.BlockSpec(a.shape, lambda i: (0, 0))
    return pl.pallas_call(
        _kv_up_kernel,
        grid=(n // tm,),
        in_specs=[row(KV_RANK), row(ROPE_DIM), full(wk), full(place), full(wv)],
        out_specs=[row(H_C * LANES), row(H_C * V_DIM_C)],
        out_shape=[jax.ShapeDtypeStruct((n, H_C * LANES), BF16),
                   jax.ShapeDtypeStruct((n, H_C * V_DIM_C), BF16)],
        compiler_params=_cparams(("parallel",)),
        name="kv_up",
    )(ckv, kr, wk, place, wv)


def _rope_tables(pos, dim, lane0):
    half = dim // 2
    inv = ROPE_THETA ** (-jnp.arange(0, dim, 2, dtype=F32) / dim)
    ang = pos.astype(F32)[:, None] * inv[None, :]
    cos, sin = jnp.cos(ang), jnp.sin(ang)
    zero = jnp.zeros_like(sin)
    c = jnp.concatenate([cos, cos], axis=1)
    s1 = jnp.concatenate([-sin, zero], axis=1)
    s2 = jnp.concatenate([zero, sin], axis=1)
    if lane0 < 0:
        reps = LANES // dim
        return tuple(jnp.tile(a, (1, reps)) for a in (c, s1, s2))
    t = pos.shape[0]
    pad = lambda a, fill: jnp.concatenate(
        [jnp.full((t, lane0), fill, F32), a, jnp.full((t, LANES - lane0 - dim), fill, F32)], axis=1)
    return pad(c, 1.0), pad(s1, 0.0), pad(s2, 0.0)


def _pad_cols(a, width):
    return jnp.pad(a, ((0, 0), (0, width - a.shape[1])))


def _blocks(a, tk):
    nb, t, l = a.shape
    return a.reshape(nb, t // tk, tk, l)


def _cat_pad_time(cache, new, t_pad):
    nb, t0, l = cache.shape
    t1 = new.shape[1]
    return jnp.concatenate([cache, new, jnp.zeros((nb, t_pad - t0 - t1, l), cache.dtype)], axis=1)


def kernel(x_prompt, x_sample, cache_fox_k, cache_fox_v, cache_fox_logf, cache_diff_k, cache_diff_v, cache_mla_ckv, cache_mla_krope, w_in_ab, b_fgate, diff_lq1, diff_lk1, diff_lq2, diff_lk2, diff_subln, w_out_ab, w_in_c, mla_q_norm, mla_kv_norm, mla_w_uq, mla_w_ukv, w_out_c, ln1_g, ln1_b, ln2_g, ln2_b, moe_wg, moe_bg, moe_we, moe_be, moe_w1, moe_w3, moe_w2):
    bp, tp, d = x_prompt.shape
    bs, ts, _ = x_sample.shape
    past = cache_fox_k.shape[2]
    depth = ln1_g.shape[0]
    alpha = (2 * depth) ** 0.25
    tk = ATTN_BLOCK
    assert past % tk == 0 and ts == 16 and past % CHUNK == 0
    ns = bs * ts
    t_dec = past + tk
    nfull_dec = past // tk
    dec_shift = 4

    pos_p = jnp.arange(tp)
    pos_s = jnp.tile(past + jnp.arange(ts), bs)

    xp = x_prompt
    xs = x_sample.reshape(1, ns, d)
    out_ab_p, out_ab_s, out_c_p, out_c_s = [], [], [], []

    for i in range(depth):
        j = i // 2
        if i % 2 == 0:
            lam_init = 0.8 - 0.6 * math.exp(-0.3 * i)
            cuts = [0, A_WIDTH, 2 * A_WIDTH, 3 * A_WIDTH, 3 * A_WIDTH + H_A,
                    3 * A_WIDTH + H_A + B_QK_WIDTH, 3 * A_WIDTH + H_A + 2 * B_QK_WIDTH,
                    3 * A_WIDTH + H_A + 2 * B_QK_WIDTH + B_V_WIDTH]
            w = w_in_ab[j]
            piece = lambda a: w[:, cuts[a]:cuts[a + 1]]
            w6 = jnp.stack([piece(0), piece(1), piece(2), piece(4), piece(5), piece(6)]).astype(BF16)
            wf = _pad_cols(piece(3), LANES).astype(BF16)
            bf = _pad_cols(b_fgate[j][None, :], LANES)
            wout = w_out_ab[j].astype(BF16)
            diff_extra = (diff_lq1[j][None, :], diff_lk1[j][None, :], diff_lq2[j][None, :],
                          diff_lk2[j][None, :], diff_subln[j][None, :])

            tabs = _rope_tables(pos_p, HEAD_DIM, -1)
            (qa, ka, kab, va, vab, lf, qb, kb, kbb, vb, vbb) = _proj_ab(xp, w6, wf, bf, tabs)
            nkb = tp // tk
            cum = _cumsum_lanes(jnp.swapaxes(lf, 1, 2).reshape(bp * H_A, tp))
            cum = cum.reshape(bp, H_A // 2, 2, nkb, tk).transpose(0, 1, 3, 2, 4)
            oa = _attention("fox", qa, _blocks(kab, tk), _blocks(vab, tk), (cum,),
                            n_pairs=H_A // 2, mask_shift=0)
            ob = _attention("diff", qb, _blocks(kbb, tk), _blocks(vbb, tk), diff_extra,
                            n_pairs=H_B, mask_shift=int(math.log2(CHUNK)), lam_init=lam_init)
            out_ab_p.append((ka.reshape(bp, tp, H_A, HEAD_DIM), va.reshape(bp, tp, H_A, HEAD_DIM), lf,
                             kb.reshape(bp, tp, H_B, 2, HEAD_DIM), vb.reshape(bp, tp, H_B, 2 * HEAD_DIM)))
            xp2 = _outproj_ln(xp.reshape(bp * tp, d), [oa.reshape(bp * tp, -1), ob.reshape(bp * tp, -1)],
                              [wout[:A_WIDTH], wout[A_WIDTH:]], ln1_g[i][None, :], ln1_b[i][None, :], alpha)

            tabs = _rope_tables(pos_s, HEAD_DIM, -1)
            (qa, ka, kab, va, vab, lf, qb, kb, kbb, vb, vbb) = _proj_ab(xs, w6, wf, bf, tabs)
            rs = lambda a: a.reshape(bs, ts, a.shape[-1])
            cache_lf = jnp.swapaxes(cache_fox_logf[j].astype(F32), 1, 2)
            lf_all = jnp.concatenate([cache_lf, jnp.swapaxes(rs(lf), 1, 2),
                                      jnp.zeros((bs, H_A, tk - ts), F32)], axis=2)
            cum = _cumsum_lanes(lf_all.reshape(bs * H_A, t_dec))
            cum = cum.reshape(bs, H_A // 2, 2, t_dec // tk, tk).transpose(0, 1, 3, 2, 4)
            flat = lambda c: c.reshape(bs, past, -1).astype(BF16)
            k_all = _blocks(_cat_pad_time(flat(cache_fox_k[j]), rs(kab), t_dec), tk)
            v_all = _blocks(_cat_pad_time(flat(cache_fox_v[j]), rs(vab), t_dec), tk)
            oa = _attention("fox", rs(qa), k_all, v_all, (cum,), n_pairs=H_A // 2, mask_shift=0,
                            nfull_static=nfull_dec)
            k_all = _blocks(_cat_pad_time(flat(cache_diff_k[j]), rs(kbb), t_dec), tk)
            v_all = _blocks(_cat_pad_time(flat(cache_diff_v[j]), rs(vbb), t_dec), tk)
            ob = _attention("diff", rs(qb), k_all, v_all, diff_extra, n_pairs=H_B, mask_shift=dec_shift,
                            nfull_static=nfull_dec, lam_init=lam_init)
            out_ab_s.append((ka.reshape(bs, ts, H_A, HEAD_DIM), va.reshape(bs, ts, H_A, HEAD_DIM),
                             lf.reshape(bs, ts, H_A), kb.reshape(bs, ts, H_B, 2, HEAD_DIM),
                             vb.reshape(bs, ts, H_B, 2 * HEAD_DIM)))
            xs2 = _outproj_ln(xs.reshape(ns, d), [oa.reshape(ns, -1), ob.reshape(ns, -1)],
                              [wout[:A_WIDTH], wout[A_WIDTH:]], ln1_g[i][None, :], ln1_b[i][None, :], alpha)
        else:
            wc = w_in_c[j]
            kr_cols = _pad_cols(wc[:, Q_RANK + KV_RANK:], LANES)
            win = jnp.concatenate([wc[:, :Q_RANK + KV_RANK], kr_cols], axis=1).astype(BF16)
            wuq = jnp.pad(mla_w_uq[j].reshape(Q_RANK, H_C, NOPE_DIM + ROPE_DIM),
                          ((0, 0), (0, 0), (0, LANES - NOPE_DIM - ROPE_DIM))).reshape(Q_RANK, H_C * LANES)
            wuq = wuq.astype(BF16)
            wukv = mla_w_ukv[j].reshape(KV_RANK, H_C, NOPE_DIM + V_DIM_C)
            wk = jnp.pad(wukv[:, :, :NOPE_DIM], ((0, 0), (0, 0), (0, LANES - NOPE_DIM)))
            wk = wk.reshape(KV_RANK, H_C * LANES).astype(BF16)
            wv = wukv[:, :, NOPE_DIM:].reshape(KV_RANK, H_C * V_DIM_C).astype(BF16)
            place = jnp.tile(_pad_cols(jnp.concatenate(
                [jnp.zeros((ROPE_DIM, NOPE_DIM), F32), jnp.eye(ROPE_DIM, dtype=F32)], axis=1), LANES),
                (1, H_C)).astype(BF16)
            gq = mla_q_norm[j][None, :]
            gkv = mla_kv_norm[j][None, :]
            wout = w_out_c[j].astype(BF16)

            q, ckv, kr = _proj_c(xp, win, gq, gkv, wuq, _rope_tables(pos_p, ROPE_DIM, NOPE_DIM),
                                 _rope_tables(pos_p, ROPE_DIM, 0))
            kc, vc = _kv_up(ckv.reshape(bp * tp, KV_RANK), kr.reshape(bp * tp, ROPE_DIM), wk, place, wv)
            oc = _attention("mla", q, _blocks(kc.reshape(bp, tp, -1), tk), _blocks(vc.reshape(bp, tp, -1), tk),
                            (), n_pairs=H_C // 2, mask_shift=int(math.log2(CHUNK)))
            out_c_p.append((ckv, kr))
            xp2 = _outproj_ln(xp.reshape(bp * tp, d), [oc.reshape(bp * tp, -1)], [wout],
                              ln1_g[i][None, :], ln1_b[i][None, :], alpha)

            q, ckv, kr = _proj_c(xs, win, gq, gkv, wuq, _rope_tables(pos_s, ROPE_DIM, NOPE_DIM),
                                 _rope_tables(pos_s, ROPE_DIM, 0))
            ckv_all = _cat_pad_time(cache_mla_ckv[j].astype(F32), ckv.reshape(bs, ts, KV_RANK), t_dec)
            kr_all = _cat_pad_time(cache_mla_krope[j].astype(F32), kr.reshape(bs, ts, ROPE_DIM), t_dec)
            kc, vc = _kv_up(ckv_all.reshape(bs * t_dec, KV_RANK), kr_all.reshape(bs * t_dec, ROPE_DIM),
                            wk, place, wv)
            oc = _attention("mla", q.reshape(bs, ts, -1), _blocks(kc.reshape(bs, t_dec, -1), tk),
                            _blocks(vc.reshape(bs, t_dec, -1), tk), (), n_pairs=H_C // 2,
                            mask_shift=dec_shift, nfull_static=nfull_dec)
            out_c_s.append((ckv.reshape(bs, ts, KV_RANK), kr.reshape(bs, ts, ROPE_DIM)))
            xs2 = _outproj_ln(xs.reshape(ns, d), [oc.reshape(ns, -1)], [wout],
                              ln1_g[i][None, :], ln1_b[i][None, :], alpha)

        wr = _pad_cols(jnp.concatenate(
            [moe_wg[i]] + [moe_we[i][gi] for gi in range(N_GROUPS)], axis=1), LANES)
        wrh = wr.astype(BF16)
        wrl = (wr - wrh.astype(F32)).astype(BF16)
        br = _pad_cols(jnp.concatenate([moe_bg[i], moe_be[i].reshape(-1)])[None, :], LANES)
        w13 = jnp.concatenate([moe_w1[i], moe_w3[i]], axis=2).astype(BF16)
        w2 = moe_w2[i].astype(BF16)
        g2, b2 = ln2_g[i][None, :], ln2_b[i][None, :]
        xp = _moe_ln(xp2, wrh, wrl, br, w13, w2, g2, b2, alpha).reshape(bp, tp, d)
        xs = _moe_ln(xs2, wrh, wrl, br, w13, w2, g2, b2, alpha).reshape(1, ns, d)

    stack = lambda rows, n: jnp.stack([r[n] for r in rows])
    return (xp, xs.reshape(bs, ts, d),
            stack(out_ab_p, 0), stack(out_ab_p, 1), stack(out_ab_p, 2), stack(out_ab_p, 3), stack(out_ab_p, 4),
            stack(out_c_p, 0), stack(out_c_p, 1),
            stack(out_ab_s, 0), stack(out_ab_s, 1), stack(out_ab_s, 2), stack(out_ab_s, 3), stack(out_ab_s, 4),
            stack(out_c_s, 0), stack(out_c_s, 1))
```

```python
import functools
import math

import jax
import jax.numpy as jnp
from jax import lax
from jax.experimental import pallas as pl
from jax.experimental.pallas import tpu as pltpu

F32 = jnp.float32
BF16 = jnp.bfloat16

D_MODEL = 1024
CHUNK = 64
HEAD_DIM = 64
ROPE_THETA = 10000.0
H_A = 8
H_B = 4
H_C = 16
Q_RANK = 256
KV_RANK = 128
NOPE_DIM = 64
ROPE_DIM = 32
V_DIM_C = 64
N_GROUPS = 4
EXPERTS_PER_GROUP = 4
N_EXPERTS = N_GROUPS * EXPERTS_PER_GROUP
D_EXPERT = 256
A_WIDTH = H_A * HEAD_DIM
B_QK_WIDTH = H_B * 2 * HEAD_DIM
B_V_WIDTH = H_B * 2 * HEAD_DIM
FGATE_BIAS = 3.0
LN_EPS = 1e-5
RMS_EPS = 1e-6
NEG_INF = -1e30
LOG2E = math.log2(math.e)

LANES = 128
VMEM_LIMIT = 48 * 1024 * 1024
ATTN_BLOCK = 512
ROW_TILE = 512
MOE_TILE = 1024
GATE_LANE0 = N_GROUPS


def _cparams(sem):
    return pltpu.CompilerParams(dimension_semantics=sem, vmem_limit_bytes=VMEM_LIMIT)


def _rope3(x, c, s1, s2, shift_up, shift_down):
    return x * c + pltpu.roll(x, shift_up, 1) * s1 + pltpu.roll(x, shift_down, 1) * s2


def _layer_norm(y, g, b):
    mu = jnp.mean(y, axis=-1, keepdims=True)
    d = y - mu
    var = jnp.mean(d * d, axis=-1, keepdims=True)
    return d * lax.rsqrt(var + LN_EPS) * g + b


def _proj_ab_kernel(x_ref, w_ref, wf_ref, bf_ref, c_ref, s1_ref, s2_ref,
                    qa_ref, ka_ref, kab_ref, va_ref, vab_ref, lf_ref,
                    qb_ref, kb_ref, kbb_ref, vb_ref, vbb_ref):
    xb = x_ref[0].astype(BF16)

    def mm(i):
        return jnp.dot(xb, w_ref[i], preferred_element_type=F32)

    qa_ref[0] = (mm(0) * (HEAD_DIM ** -0.5 * LOG2E)).astype(BF16)
    ka = mm(1)
    ka_ref[0] = ka
    kab_ref[0] = ka.astype(BF16)
    va = mm(2)
    va_ref[0] = va
    vab_ref[0] = va.astype(BF16)

    z = jnp.dot(xb, wf_ref[...], preferred_element_type=F32) + bf_ref[...]
    lf = jnp.minimum(z, 0.0) - jnp.log1p(jnp.exp(-jnp.abs(z)))
    lf_ref[0] = lf[:, :H_A]

    c, s1, s2 = c_ref[...], s1_ref[...], s2_ref[...]
    qb = mm(3)
    kb = mm(4)
    for s in range(B_QK_WIDTH // LANES):
        sl = slice(s * LANES, (s + 1) * LANES)
        qs = _rope3(qb[:, sl], c, s1, s2, LANES - HEAD_DIM // 2, HEAD_DIM // 2)
        qb_ref[0, :, sl] = (qs * (HEAD_DIM ** -0.5 * LOG2E)).astype(BF16)
        ks = _rope3(kb[:, sl], c, s1, s2, LANES - HEAD_DIM // 2, HEAD_DIM // 2)
        kb_ref[0, :, sl] = ks
        kbb_ref[0, :, sl] = ks.astype(BF16)
    vb = mm(5)
    vb_ref[0] = vb
    vbb_ref[0] = vb.astype(BF16)


def _proj_ab(x, w6, wf, bf, tabs):
    nb, t, _ = x.shape
    tm = min(ROW_TILE, t)
    assert t % tm == 0
    w = A_WIDTH
    tok = lambda width: pl.BlockSpec((1, tm, width), lambda b, i: (b, i, 0))
    tab = pl.BlockSpec((tm, LANES), lambda b, i: (i, 0))
    full = lambda a: pl.BlockSpec(a.shape, lambda b, i: (0,) * a.ndim)
    sds = lambda width, dt: jax.ShapeDtypeStruct((nb, t, width), dt)
    return pl.pallas_call(
        _proj_ab_kernel,
        grid=(nb, t // tm),
        in_specs=[tok(D_MODEL), full(w6), full(wf), full(bf), tab, tab, tab],
        out_specs=[tok(w), tok(w), tok(w), tok(w), tok(w), tok(H_A), tok(w), tok(w), tok(w), tok(w), tok(w)],
        out_shape=[sds(w, BF16), sds(w, F32), sds(w, BF16), sds(w, F32), sds(w, BF16), sds(H_A, F32),
                   sds(w, BF16), sds(w, F32), sds(w, BF16), sds(w, F32), sds(w, BF16)],
        compiler_params=_cparams(("parallel", "parallel")),
        name="proj_ab",
    )(x, w6, wf, bf, *tabs)


def _cumsum_kernel(x_ref, o_ref, carry_ref):
    @pl.when(pl.program_id(0) == 0)
    def _():
        carry_ref[...] = jnp.zeros_like(carry_ref)

    x = x_ref[...]
    tc = x.shape[1]
    hi = x.astype(BF16)
    r1 = x - hi.astype(F32)
    mid = r1.astype(BF16)
    lo = (r1 - mid.astype(F32)).astype(BF16)
    row = lax.broadcasted_iota(jnp.int32, (tc, tc), 0)
    col = lax.broadcasted_iota(jnp.int32, (tc, tc), 1)
    upper = (row <= col).astype(BF16)
    dot = lambda a: jnp.dot(a, upper, preferred_element_type=F32)
    c = dot(hi) + dot(mid) + dot(lo) + carry_ref[...]
    o_ref[...] = c
    carry_ref[...] = c[:, tc - 1:tc]


def _cumsum_lanes(x):
    r, t = x.shape
    tc = min(ATTN_BLOCK, t)
    assert t % tc == 0
    return pl.pallas_call(
        _cumsum_kernel,
        grid=(t // tc,),
        in_specs=[pl.BlockSpec((r, tc), lambda i: (0, i))],
        out_specs=pl.BlockSpec((r, tc), lambda i: (0, i)),
        out_shape=jax.ShapeDtypeStruct((r, t), F32),
        scratch_shapes=[pltpu.VMEM((r, 1), F32)],
        compiler_params=_cparams(("arbitrary",)),
        name="cumsum",
    )(x)


def _attn_kernel(*refs, mode, tq, tk, mask_shift, nfull_static, lam_init):
    if mode == "fox":
        q_ref, k_ref, v_ref, c_ref, o_ref, m_sc, l_sc, acc_sc = refs
    elif mode == "diff":
        q_ref, k_ref, v_ref, lq1_ref, lk1_ref, lq2_ref, lk2_ref, sub_ref, o_ref, m_sc, l_sc, acc_sc = refs
    else:
        q_ref, k_ref, v_ref, o_ref, m_sc, l_sc, acc_sc = refs

    qi = pl.program_id(2)
    q = q_ref[0]
    lane = lax.broadcasted_iota(jnp.int32, (1, LANES), 1)
    if mode == "mla":
        qs = (q[:, :LANES], q[:, LANES:])
    else:
        zero = jnp.zeros_like(q)
        qs = (jnp.where(lane < HEAD_DIM, q, zero), jnp.where(lane >= HEAD_DIM, q, zero))

    m_sc[...] = jnp.full(m_sc.shape, NEG_INF, F32)
    l_sc[...] = jnp.zeros(l_sc.shape, F32)
    acc_sc[...] = jnp.zeros(acc_sc.shape, F32)

    def step(j, masked):
        k = k_ref[0, j]
        v = v_ref[0, j]
        for i in range(2):
            ki = k[:, i * LANES:(i + 1) * LANES] if mode == "mla" else k
            s = lax.dot_general(qs[i], ki, (((1,), (1,)), ((), ())), preferred_element_type=F32)
            if mode == "fox":
                s = s - c_ref[0, 0, j, i:i + 1, :] * LOG2E
            if masked:
                row = lax.broadcasted_iota(jnp.int32, (tq, tk), 0)
                col = lax.broadcasted_iota(jnp.int32, (tq, tk), 1)
                vis = lax.shift_right_logical(col, mask_shift) <= lax.shift_right_logical(row, mask_shift)
                s = jnp.where(vis, s, NEG_INF)
            m_prev = m_sc[i]
            m_new = jnp.maximum(m_prev, jnp.max(s, axis=1, keepdims=True))
            alpha = jnp.exp2(m_prev - m_new)
            ps = [jnp.exp2(s[:, c * LANES:(c + 1) * LANES] - m_new) for c in range(tk // LANES)]
            l_sc[i] = alpha * l_sc[i] + functools.reduce(lambda a, b: a + b, ps)
            p = jnp.concatenate(ps, axis=1).astype(BF16)
            acc_sc[i] = alpha * acc_sc[i] + jnp.dot(p, v, preferred_element_type=F32)
            m_sc[i] = m_new

    nfull = qi if nfull_static is None else nfull_static

    def body(j, carry):
        step(j, False)
        return carry

    lax.fori_loop(0, nfull, body, 0)
    step(nfull, True)

    o0 = acc_sc[0] / jnp.sum(l_sc[0], axis=1, keepdims=True)
    o1 = acc_sc[1] / jnp.sum(l_sc[1], axis=1, keepdims=True)
    if mode == "diff":
        lam = (jnp.exp(jnp.sum(lq1_ref[...] * lk1_ref[...], axis=1, keepdims=True))
               - jnp.exp(jnp.sum(lq2_ref[...] * lk2_ref[...], axis=1, keepdims=True)) + lam_init)
        o = o0 - lam * o1
        ms = jnp.mean(o * o, axis=1, keepdims=True)
        o = o * lax.rsqrt(ms + RMS_EPS) * sub_ref[...] * (1.0 - lam_init)
    else:
        o = jnp.where(lane < HEAD_DIM, o0, o1)
    o_ref[0] = o.astype(o_ref.dtype)


def _attention(mode, q, k, v, extra, *, n_pairs, mask_shift, nfull_static=None, lam_init=0.0):
    nb, t_q, _ = q.shape
    _, nkb, tk, _ = k.shape
    tq = min(ATTN_BLOCK, t_q)
    assert t_q % tq == 0
    if nfull_static is None:
        assert tq == tk and nkb == t_q // tq
    qw = 2 * LANES if mode == "mla" else LANES
    in_specs = [
        pl.BlockSpec((1, tq, qw), lambda b, p, i: (b, i, p)),
        pl.BlockSpec((1, nkb, tk, qw), lambda b, p, i: (b, 0, 0, p)),
        pl.BlockSpec((1, nkb, tk, LANES), lambda b, p, i: (b, 0, 0, p)),
    ]
    if mode == "fox":
        in_specs.append(pl.BlockSpec((1, 1, nkb, 2, tk), lambda b, p, i: (b, p, 0, 0, 0)))
    elif mode == "diff":
        in_specs += [pl.BlockSpec(a.shape, lambda b, p, i: (0, 0)) for a in extra]
    kern = functools.partial(_attn_kernel, mode=mode, tq=tq, tk=tk, mask_shift=mask_shift,
                             nfull_static=nfull_static, lam_init=lam_init)
    return pl.pallas_call(
        kern,
        grid=(nb, n_pairs, t_q // tq),
        in_specs=in_specs,
        out_specs=pl.BlockSpec((1, tq, LANES), lambda b, p, i: (b, i, p)),
        out_shape=jax.ShapeDtypeStruct((nb, t_q, n_pairs * LANES), BF16),
        scratch_shapes=[pltpu.VMEM((2, tq, LANES), F32), pltpu.VMEM((2, tq, LANES), F32),
                        pltpu.VMEM((2, tq, LANES), F32)],
        compiler_params=_cparams(("parallel", "parallel", "arbitrary")),
        name="attn_" + mode,
    )(q, k, v, *extra)


def _outproj_ln_kernel(*refs, n_in, alpha):
    x_ref = refs[0]
    o_refs = refs[1:1 + n_in]
    w_refs = refs[1 + n_in:1 + 2 * n_in]
    g_ref, b_ref, y_ref = refs[1 + 2 * n_in:]
    mix = jnp.dot(o_refs[0][...], w_refs[0][...], preferred_element_type=F32)
    for o_r, w_r in zip(o_refs[1:], w_refs[1:]):
        mix = mix + jnp.dot(o_r[...], w_r[...], preferred_element_type=F32)
    y_ref[...] = _layer_norm(alpha * x_ref[...] + mix, g_ref[...], b_ref[...])


def _outproj_ln(x, outs, ws, g, b, alpha):
    n, d = x.shape
    tm = min(ROW_TILE, n)
    assert n % tm == 0
    row = lambda width: pl.BlockSpec((tm, width), lambda i: (i, 0))
    full = lambda a: pl.BlockSpec(a.shape, lambda i: (0, 0))
    return pl.pallas_call(
        functools.partial(_outproj_ln_kernel, n_in=len(outs), alpha=alpha),
        grid=(n // tm,),
        in_specs=[row(d)] + [row(o.shape[1]) for o in outs] + [full(w) for w in ws] + [full(g), full(b)],
        out_specs=row(d),
        out_shape=jax.ShapeDtypeStruct((n, d), F32),
        compiler_params=_cparams(("parallel",)),
        name="outproj_ln",
    )(x, *outs, *ws, g, b)


def _route(logits):
    lane = lax.broadcasted_iota(jnp.int32, logits.shape, 1).astype(F32)
    big = float(1 << 20)
    is_g = lane < N_GROUPS
    lg = jnp.where(is_g, logits, NEG_INF)
    eg = jnp.where(is_g, jnp.exp(lg - jnp.max(lg, axis=1, keepdims=True)), 0.0)
    pg = eg / jnp.sum(eg, axis=1, keepdims=True)
    p_g = jnp.max(pg, axis=1, keepdims=True)
    gidx = jnp.min(jnp.where(is_g & (pg == p_g), lane, big), axis=1, keepdims=True)
    lo = GATE_LANE0 + EXPERTS_PER_GROUP * gidx
    sel = (lane >= lo) & (lane < lo + EXPERTS_PER_GROUP)
    le = jnp.where(sel, logits, NEG_INF)
    ee = jnp.where(sel, jnp.exp(le - jnp.max(le, axis=1, keepdims=True)), 0.0)
    pe = ee / jnp.sum(ee, axis=1, keepdims=True)
    v1 = jnp.max(jnp.where(sel, pe, -1.0), axis=1, keepdims=True)
    i1 = jnp.min(jnp.where(sel & (pe == v1), lane, big), axis=1, keepdims=True)
    rest = sel & (lane != i1)
    v2 = jnp.max(jnp.where(rest, pe, -1.0), axis=1, keepdims=True)
    i2 = jnp.min(jnp.where(rest & (pe == v2), lane, big), axis=1, keepdims=True)
    tot = v1 + v2
    w1 = v1 / tot * p_g
    w2 = v2 / tot * p_g
    return jnp.where(lane == i1, w1, jnp.where(lane == i2, w2, 0.0))


def _moe_ln_kernel(x_ref, wrh_ref, wrl_ref, br_ref, w13_ref, w2_ref, g_ref, b_ref, y_ref,
                   xb_sc, gate_sc, acc_sc, *, alpha):
    e = pl.program_id(1)

    @pl.when(e == 0)
    def _():
        x = x_ref[...]
        xh = x.astype(BF16)
        xl = (x - xh.astype(F32)).astype(BF16)
        xb_sc[...] = xh
        logits = (jnp.dot(xh, wrh_ref[...], preferred_element_type=F32)
                  + jnp.dot(xl, wrh_ref[...], preferred_element_type=F32)
                  + jnp.dot(xh, wrl_ref[...], preferred_element_type=F32) + br_ref[...])
        gate_sc[...] = _route(logits)
        acc_sc[...] = jnp.zeros_like(acc_sc)

    h = jnp.dot(xb_sc[...], w13_ref[0], preferred_element_type=F32)
    h1 = h[:, :D_EXPERT]
    h3 = h[:, D_EXPERT:]
    hdn = (h1 * jax.nn.sigmoid(h1)) * h3
    y = jnp.dot(hdn.astype(BF16), w2_ref[0], preferred_element_type=F32)
    lane = lax.broadcasted_iota(jnp.int32, (1, LANES), 1)
    ge = jnp.sum(jnp.where(lane == e + GATE_LANE0, gate_sc[...], 0.0), axis=1, keepdims=True)
    acc_sc[...] += ge * y

    @pl.when(e == pl.num_programs(1) - 1)
    def _():
        y_ref[...] = _layer_norm(alpha * x_ref[...] + acc_sc[...], g_ref[...], b_ref[...])


def _moe_ln(x, wrh, wrl, br, w13, w2, g, b, alpha):
    n, d = x.shape
    tm = min(MOE_TILE, n)
    assert n % tm == 0
    ne = w13.shape[0]
    full = lambda a: pl.BlockSpec(a.shape, lambda i, e: (0, 0))
    return pl.pallas_call(
        functools.partial(_moe_ln_kernel, alpha=alpha),
        grid=(n // tm, ne),
        in_specs=[pl.BlockSpec((tm, d), lambda i, e: (i, 0)), full(wrh), full(wrl), full(br),
                  pl.BlockSpec((1,) + w13.shape[1:], lambda i, e: (e, 0, 0)),
                  pl.BlockSpec((1,) + w2.shape[1:], lambda i, e: (e, 0, 0)),
                  full(g), full(b)],
        out_specs=pl.BlockSpec((tm, d), lambda i, e: (i, 0)),
        out_shape=jax.ShapeDtypeStruct((n, d), F32),
        scratch_shapes=[pltpu.VMEM((tm, d), BF16), pltpu.VMEM((tm, LANES), F32), pltpu.VMEM((tm, d), F32)],
        compiler_params=_cparams(("parallel", "arbitrary")),
        name="moe_ln",
    )(x, wrh, wrl, br, w13, w2, g, b)


def _proj_c_kernel(x_ref, win_ref, gq_ref, gkv_ref, wuq_ref, cq_ref, s1q_ref, s2q_ref,
                   ck_ref, s1k_ref, s2k_ref, q_ref, ckv_ref, kr_ref):
    xb = x_ref[0].astype(BF16)
    h = jnp.dot(xb, win_ref[...], preferred_element_type=F32)
    qa = h[:, :Q_RANK]
    kva = h[:, Q_RANK:Q_RANK + KV_RANK]
    krw = h[:, Q_RANK + KV_RANK:]
    qn = qa * lax.rsqrt(jnp.mean(qa * qa, axis=1, keepdims=True) + RMS_EPS) * gq_ref[...]
    ckv_ref[0] = kva * lax.rsqrt(jnp.mean(kva * kva, axis=1, keepdims=True) + RMS_EPS) * gkv_ref[...]
    half = ROPE_DIM // 2
    kr = _rope3(krw, ck_ref[...], s1k_ref[...], s2k_ref[...], LANES - half, half)
    kr_ref[0] = kr[:, :ROPE_DIM]
    q = jnp.dot(qn.astype(BF16), wuq_ref[...], preferred_element_type=F32)
    cq, s1q, s2q = cq_ref[...], s1q_ref[...], s2q_ref[...]
    scale = (NOPE_DIM + ROPE_DIM) ** -0.5 * LOG2E
    for hd in range(H_C):
        sl = slice(hd * LANES, (hd + 1) * LANES)
        q_ref[0, :, sl] = (_rope3(q[:, sl], cq, s1q, s2q, LANES - half, half) * scale).astype(BF16)


def _proj_c(x, win, gq, gkv, wuq, tabs_q, tabs_k):
    nb, t, _ = x.shape
    tm = min(ROW_TILE, t)
    assert t % tm == 0
    tok = lambda width: pl.BlockSpec((1, tm, width), lambda b, i: (b, i, 0))
    tab = pl.BlockSpec((tm, LANES), lambda b, i: (i, 0))
    full = lambda a: pl.BlockSpec(a.shape, lambda b, i: (0, 0))
    return pl.pallas_call(
        _proj_c_kernel,
        grid=(nb, t // tm),
        in_specs=[tok(D_MODEL), full(win), full(gq), full(gkv), full(wuq)] + [tab] * 6,
        out_specs=[tok(H_C * LANES), tok(KV_RANK), tok(ROPE_DIM)],
        out_shape=[jax.ShapeDtypeStruct((nb, t, H_C * LANES), BF16),
                   jax.ShapeDtypeStruct((nb, t, KV_RANK), F32),
                   jax.ShapeDtypeStruct((nb, t, ROPE_DIM), F32)],
        compiler_params=_cparams(("parallel", "parallel")),
        name="proj_c",
    )(x, win, gq, gkv, wuq, *tabs_q, *tabs_k)


def _kv_up_kernel(ckv_ref, kr_ref, wk_ref, place_ref, wv_ref, k_ref, v_ref):
    cb = ckv_ref[...].astype(BF16)
    k = (jnp.dot(cb, wk_ref[...], preferred_element_type=F32)
         + jnp.dot(kr_ref[...].astype(BF16), place_ref[...], preferred_element_type=F32))
    k_ref[...] = k.astype(BF16)
    v_ref[...] = jnp.dot(cb, wv_ref[...], preferred_element_type=F32).astype(BF16)


def _kv_up(ckv, kr, wk, place, wv):
    n = ckv.shape[0]
    tm = min(ROW_TILE, n)
    assert n % tm == 0
    row = lambda width: pl.BlockSpec((tm, width), lambda i: (i, 0))
    full = lambda a: pl.BlockSpec(a.shape, lambda i: (0, 0))
    return pl.pallas_call(
        _kv_up_kernel,
        grid=(n // tm,),
        in_specs=[row(KV_RANK), row(ROPE_DIM), full(wk), full(place), full(wv)],
        out_specs=[row(H_C * LANES), row(H_C * V_DIM_C)],
        out_shape=[jax.ShapeDtypeStruct((n, H_C * LANES), BF16),
                   jax.ShapeDtypeStruct((n, H_C * V_DIM_C), BF16)],
        compiler_params=_cparams(("parallel",)),
        name="kv_up",
    )(ckv, kr, wk, place, wv)


def _rope_tables(pos, dim, lane0):
    half = dim // 2
    inv = ROPE_THETA ** (-jnp.arange(0, dim, 2, dtype=F32) / dim)
    ang = pos.astype(F32)[:, None] * inv[None, :]
    cos, sin = jnp.cos(ang), jnp.sin(ang)
    zero = jnp.zeros_like(sin)
    c = jnp.concatenate([cos, cos], axis=1)
    s1 = jnp.concatenate([-sin, zero], axis=1)
    s2 = jnp.concatenate([zero, sin], axis=1)
    if lane0 < 0:
        reps = LANES // dim
        return tuple(jnp.tile(a, (1, reps)) for a in (c, s1, s2))
    t = pos.shape[0]
    pad = lambda a, fill: jnp.concatenate(
        [jnp.full((t, lane0), fill, F32), a, jnp.full((t, LANES - lane0 - dim), fill, F32)], axis=1)
    return pad(c, 1.0), pad(s1, 0.0), pad(s2, 0.0)


def _pad_cols(a, width):
    return jnp.pad(a, ((0, 0), (0, width - a.shape[1])))


def _blocks(a, tk):
    nb, t, l = a.shape
    return a.reshape(nb, t // tk, tk, l)


def _cat_pad_time(cache, new, t_pad):
    nb, t0, l = cache.shape
    t1 = new.shape[1]
    return jnp.concatenate([cache, new, jnp.zeros((nb, t_pad - t0 - t1, l), cache.dtype)], axis=1)


def kernel(x_prompt, x_sample, cache_fox_k, cache_fox_v, cache_fox_logf, cache_diff_k, cache_diff_v, cache_mla_ckv, cache_mla_krope, w_in_ab, b_fgate, diff_lq1, diff_lk1, diff_lq2, diff_lk2, diff_subln, w_out_ab, w_in_c, mla_q_norm, mla_kv_norm, mla_w_uq, mla_w_ukv, w_out_c, ln1_g, ln1_b, ln2_g, ln2_b, moe_wg, moe_bg, moe_we, moe_be, moe_w1, moe_w3, moe_w2):
    bp, tp, d = x_prompt.shape
    bs, ts, _ = x_sample.shape
    past = cache_fox_k.shape[2]
    depth = ln1_g.shape[0]
    alpha = (2 * depth) ** 0.25
    tk = ATTN_BLOCK
    assert past % tk == 0 and ts == 16 and past % CHUNK == 0
    ns = bs * ts
    t_dec = past + tk
    nfull_dec = past // tk
    dec_shift = 4

    pos_p = jnp.arange(tp)
    pos_s = jnp.tile(past + jnp.arange(ts), bs)

    xp = x_prompt
    xs = x_sample.reshape(1, ns, d)
    out_ab_p, out_ab_s, out_c_p, out_c_s = [], [], [], []

    for i in range(depth):
        j = i // 2
        if i % 2 == 0:
            lam_init = 0.8 - 0.6 * math.exp(-0.3 * i)
            cuts = [0, A_WIDTH, 2 * A_WIDTH, 3 * A_WIDTH, 3 * A_WIDTH + H_A,
                    3 * A_WIDTH + H_A + B_QK_WIDTH, 3 * A_WIDTH + H_A + 2 * B_QK_WIDTH,
                    3 * A_WIDTH + H_A + 2 * B_QK_WIDTH + B_V_WIDTH]
            w = w_in_ab[j]
            piece = lambda a: w[:, cuts[a]:cuts[a + 1]]
            w6 = jnp.stack([piece(0), piece(1), piece(2), piece(4), piece(5), piece(6)]).astype(BF16)
            wf = _pad_cols(piece(3), LANES).astype(BF16)
            bf = _pad_cols(b_fgate[j][None, :], LANES)
            wout = w_out_ab[j].astype(BF16)
            diff_extra = (diff_lq1[j][None, :], diff_lk1[j][None, :], diff_lq2[j][None, :],
                          diff_lk2[j][None, :], diff_subln[j][None, :])

            tabs = _rope_tables(pos_p, HEAD_DIM, -1)
            (qa, ka, kab, va, vab, lf, qb, kb, kbb, vb, vbb) = _proj_ab(xp, w6, wf, bf, tabs)
            nkb = tp // tk
            cum = _cumsum_lanes(jnp.swapaxes(lf, 1, 2).reshape(bp * H_A, tp))
            cum = cum.reshape(bp, H_A // 2, 2, nkb, tk).transpose(0, 1, 3, 2, 4)
            oa = _attention("fox", qa, _blocks(kab, tk), _blocks(vab, tk), (cum,),
                            n_pairs=H_A // 2, mask_shift=0)
            ob = _attention("diff", qb, _blocks(kbb, tk), _blocks(vbb, tk), diff_extra,
                            n_pairs=H_B, mask_shift=int(math.log2(CHUNK)), lam_init=lam_init)
            out_ab_p.append((ka.reshape(bp, tp, H_A, HEAD_DIM), va.reshape(bp, tp, H_A, HEAD_DIM), lf,
                             kb.reshape(bp, tp, H_B, 2, HEAD_DIM), vb.reshape(bp, tp, H_B, 2 * HEAD_DIM)))
            xp2 = _outproj_ln(xp.reshape(bp * tp, d), [oa.reshape(bp * tp, -1), ob.reshape(bp * tp, -1)],
                              [wout[:A_WIDTH], wout[A_WIDTH:]], ln1_g[i][None, :], ln1_b[i][None, :], alpha)

            tabs = _rope_tables(pos_s, HEAD_DIM, -1)
            (qa, ka, kab, va, vab, lf, qb, kb, kbb, vb, vbb) = _proj_ab(xs, w6, wf, bf, tabs)
            rs = lambda a: a.reshape(bs, ts, a.shape[-1])
            cache_lf = jnp.swapaxes(cache_fox_logf[j].astype(F32), 1, 2)
            lf_all = jnp.concatenate([cache_lf, jnp.swapaxes(rs(lf), 1, 2),
                                      jnp.zeros((bs, H_A, tk - ts), F32)], axis=2)
            cum = _cumsum_lanes(lf_all.reshape(bs * H_A, t_dec))
            cum = cum.reshape(bs, H_A // 2, 2, t_dec // tk, tk).transpose(0, 1, 3, 2, 4)
            flat = lambda c: c.reshape(bs, past, -1).astype(BF16)
            k_all = _blocks(_cat_pad_time(flat(cache_fox_k[j]), rs(kab), t_dec), tk)
            v_all = _blocks(_cat_pad_time(flat(cache_fox_v[j]), rs(vab), t_dec), tk)
            oa = _attention("fox", rs(qa), k_all, v_all, (cum,), n_pairs=H_A // 2, mask_shift=0,
                            nfull_static=nfull_dec)
            k_all = _blocks(_cat_pad_time(flat(cache_diff_k[j]), rs(kbb), t_dec), tk)
            v_all = _blocks(_cat_pad_time(flat(cache_diff_v[j]), rs(vbb), t_dec), tk)
            ob = _attention("diff", rs(qb), k_all, v_all, diff_extra, n_pairs=H_B, mask_shift=dec_shift,
                            nfull_static=nfull_dec, lam_init=lam_init)
            out_ab_s.append((ka.reshape(bs, ts, H_A, HEAD_DIM), va.reshape(bs, ts, H_A, HEAD_DIM),
                             lf.reshape(bs, ts, H_A), kb.reshape(bs, ts, H_B, 2, HEAD_DIM),
                             vb.reshape(bs, ts, H_B, 2 * HEAD_DIM)))
            xs2 = _outproj_ln(xs.reshape(ns, d), [oa.reshape(ns, -1), ob.reshape(ns, -1)],
                              [wout[:A_WIDTH], wout[A_WIDTH:]], ln1_g[i][None, :], ln1_b[i][None, :], alpha)
        else:
            wc = w_in_c[j]
            kr_cols = _pad_cols(wc[:, Q_RANK + KV_RANK:], LANES)
            win = jnp.concatenate([wc[:, :Q_RANK + KV_RANK], kr_cols], axis=1).astype(BF16)
            wuq = jnp.pad(mla_w_uq[j].reshape(Q_RANK, H_C, NOPE_DIM + ROPE_DIM),
                          ((0, 0), (0, 0), (0, LANES - NOPE_DIM - ROPE_DIM))).reshape(Q_RANK, H_C * LANES)
            wuq = wuq.astype(BF16)
            wukv = mla_w_ukv[j].reshape(KV_RANK, H_C, NOPE_DIM + V_DIM_C)
            wk = jnp.pad(wukv[:, :, :NOPE_DIM], ((0, 0), (0, 0), (0, LANES - NOPE_DIM)))
            wk = wk.reshape(KV_RANK, H_C * LANES).astype(BF16)
            wv = wukv[:, :, NOPE_DIM:].reshape(KV_RANK, H_C * V_DIM_C).astype(BF16)
            place = jnp.tile(_pad_cols(jnp.concatenate(
                [jnp.zeros((ROPE_DIM, NOPE_DIM), F32), jnp.eye(ROPE_DIM, dtype=F32)], axis=1), LANES),
                (1, H_C)).astype(BF16)
            gq = mla_q_norm[j][None, :]
            gkv = mla_kv_norm[j][None, :]
            wout = w_out_c[j].astype(BF16)

            q, ckv, kr = _proj_c(xp, win, gq, gkv, wuq, _rope_tables(pos_p, ROPE_DIM, NOPE_DIM),
                                 _rope_tables(pos_p, ROPE_DIM, 0))
            kc, vc = _kv_up(ckv.reshape(bp * tp, KV_RANK), kr.reshape(bp * tp, ROPE_DIM), wk, place, wv)
            oc = _attention("mla", q, _blocks(kc.reshape(bp, tp, -1), tk), _blocks(vc.reshape(bp, tp, -1), tk),
                            (), n_pairs=H_C // 2, mask_shift=int(math.log2(CHUNK)))
            out_c_p.append((ckv, kr))
            xp2 = _outproj_ln(xp.reshape(bp * tp, d), [oc.reshape(bp * tp, -1)], [wout],
                              ln1_g[i][None, :], ln1_b[i][None, :], alpha)

            q, ckv, kr = _proj_c(xs, win, gq, gkv, wuq, _rope_tables(pos_s, ROPE_DIM, NOPE_DIM),
                                 _rope_tables(pos_s, ROPE_DIM, 0))
            ckv_all = _cat_pad_time(cache_mla_ckv[j].astype(F32), ckv.reshape(bs, ts, KV_RANK), t_dec)
            kr_all = _cat_pad_time(cache_mla_krope[j].astype(F32), kr.reshape(bs, ts, ROPE_DIM), t_dec)
            kc, vc = _kv_up(ckv_all.reshape(bs * t_dec, KV_RANK), kr_all.reshape(bs * t_dec, ROPE_DIM),
                            wk, place, wv)
            oc = _attention("mla", q.reshape(bs, ts, -1), _blocks(kc.reshape(bs, t_dec, -1), tk),
                            _blocks(vc.reshape(bs, t_dec, -1), tk), (), n_pairs=H_C // 2,
                            mask_shift=dec_shift, nfull_static=nfull_dec)
            out_c_s.append((ckv.reshape(bs, ts, KV_RANK), kr.reshape(bs, ts, ROPE_DIM)))
            xs2 = _outproj_ln(xs.reshape(ns, d), [oc.reshape(ns, -1)], [wout],
                              ln1_g[i][None, :], ln1_b[i][None, :], alpha)

        wr = _pad_cols(jnp.concatenate(
            [moe_wg[i]] + [moe_we[i][gi] for gi in range(N_GROUPS)], axis=1), LANES)
        wrh = wr.astype(BF16)
        wrl = (wr - wrh.astype(F32)).astype(BF16)
        br = _pad_cols(jnp.concatenate([moe_bg[i], moe_be[i].reshape(-1)])[None, :], LANES)
        w13 = jnp.concatenate([moe_w1[i], moe_w3[i]], axis=2).astype(BF16)
        w2 = moe_w2[i].astype(BF16)
        g2, b2 = ln2_g[i][None, :], ln2_b[i][None, :]
        xp = _moe_ln(xp2, wrh, wrl, br, w13, w2, g2, b2, alpha).reshape(bp, tp, d)
        xs = _moe_ln(xs2, wrh, wrl, br, w13, w2, g2, b2, alpha).reshape(1, ns, d)

    stack = lambda rows, n: jnp.stack([r[n] for r in rows])
    return (xp, xs.reshape(bs, ts, d),
            stack(out_ab_p, 0), stack(out_ab_p, 1), stack(out_ab_p, 2), stack(out_ab_p, 3), stack(out_ab_p, 4),
            stack(out_c_p, 0), stack(out_c_p, 1),
            stack(out_ab_s, 0), stack(out_ab_s, 1), stack(out_ab_s, 2), stack(out_ab_s, 3), stack(out_ab_s, 4),
            stack(out_c_s, 0), stack(out_c_s, 1))
```

```python
import functools
import math

import jax
import jax.numpy as jnp
from jax import lax
from jax.experimental import pallas as pl
from jax.experimental.pallas import tpu as pltpu

F32 = jnp.float32
BF16 = jnp.bfloat16

D_MODEL = 1024
CHUNK = 64
HEAD_DIM = 64
ROPE_THETA = 10000.0
H_A = 8
H_B = 4
H_C = 16
Q_RANK = 256
KV_RANK = 128
NOPE_DIM = 64
ROPE_DIM = 32
V_DIM_C = 64
N_GROUPS = 4
EXPERTS_PER_GROUP = 4
N_EXPERTS = N_GROUPS * EXPERTS_PER_GROUP
D_EXPERT = 256
A_WIDTH = H_A * HEAD_DIM
B_QK_WIDTH = H_B * 2 * HEAD_DIM
B_V_WIDTH = H_B * 2 * HEAD_DIM
FGATE_BIAS = 3.0
LN_EPS = 1e-5
RMS_EPS = 1e-6
NEG_INF = -1e30
LOG2E = math.log2(math.e)

LANES = 128
SUBLANES = 8
BIAS_PIECES = 3
DEC_Q_ROWS = 128
VMEM_LIMIT = 48 * 1024 * 1024
ATTN_BLOCK = 512
ROW_TILE = 512
MOE_TILE = 1024
GATE_LANE0 = N_GROUPS


def _cparams(sem):
    return pltpu.CompilerParams(dimension_semantics=sem, vmem_limit_bytes=VMEM_LIMIT)


def _rope3(x, c, s1, s2, shift_up, shift_down):
    return x * c + pltpu.roll(x, shift_up, 1) * s1 + pltpu.roll(x, shift_down, 1) * s2


def _layer_norm(y, g, b):
    mu = jnp.mean(y, axis=-1, keepdims=True)
    d = y - mu
    var = jnp.mean(d * d, axis=-1, keepdims=True)
    return d * lax.rsqrt(var + LN_EPS) * g + b


def _proj_ab_kernel(x_ref, w_ref, wf_ref, bf_ref, c_ref, s1_ref, s2_ref,
                    qa_ref, ka_ref, kab_ref, va_ref, vab_ref, lf_ref,
                    qb_ref, kb_ref, kbb_ref, vb_ref, vbb_ref):
    xb = x_ref[0].astype(BF16)

    def mm(i):
        return jnp.dot(xb, w_ref[i], preferred_element_type=F32)

    qa_ref[0] = (mm(0) * (HEAD_DIM ** -0.5 * LOG2E)).astype(BF16)
    ka = mm(1)
    ka_ref[0] = ka
    kab_ref[0] = ka.astype(BF16)
    va = mm(2)
    va_ref[0] = va
    vab_ref[0] = va.astype(BF16)

    z = jnp.dot(xb, wf_ref[...], preferred_element_type=F32) + bf_ref[...]
    lf = jnp.minimum(z, 0.0) - jnp.log1p(jnp.exp(-jnp.abs(z)))
    lf_ref[0] = lf[:, :H_A]

    c, s1, s2 = c_ref[...], s1_ref[...], s2_ref[...]
    qb = mm(3)
    kb = mm(4)
    for s in range(B_QK_WIDTH // LANES):
        sl = slice(s * LANES, (s + 1) * LANES)
        qs = _rope3(qb[:, sl], c, s1, s2, LANES - HEAD_DIM // 2, HEAD_DIM // 2)
        qb_ref[0, :, sl] = (qs * (HEAD_DIM ** -0.5 * LOG2E)).astype(BF16)
        ks = _rope3(kb[:, sl], c, s1, s2, LANES - HEAD_DIM // 2, HEAD_DIM // 2)
        kb_ref[0, :, sl] = ks
        kbb_ref[0, :, sl] = ks.astype(BF16)
    vb = mm(5)
    vb_ref[0] = vb
    vbb_ref[0] = vb.astype(BF16)


def _proj_ab(x, w6, wf, bf, tabs):
    nb, t, _ = x.shape
    tm = min(ROW_TILE, t)
    assert t % tm == 0
    w = A_WIDTH
    tok = lambda width: pl.BlockSpec((1, tm, width), lambda b, i: (b, i, 0))
    tab = pl.BlockSpec((tm, LANES), lambda b, i: (i, 0))
    full = lambda a: pl.BlockSpec(a.shape, lambda b, i: (0,) * a.ndim)
    sds = lambda width, dt: jax.ShapeDtypeStruct((nb, t, width), dt)
    return pl.pallas_call(
        _proj_ab_kernel,
        grid=(nb, t // tm),
        in_specs=[tok(D_MODEL), full(w6), full(wf), full(bf), tab, tab, tab],
        out_specs=[tok(w), tok(w), tok(w), tok(w), tok(w), tok(H_A), tok(w), tok(w), tok(w), tok(w), tok(w)],
        out_shape=[sds(w, BF16), sds(w, F32), sds(w, BF16), sds(w, F32), sds(w, BF16), sds(H_A, F32),
                   sds(w, BF16), sds(w, F32), sds(w, BF16), sds(w, F32), sds(w, BF16)],
        compiler_params=_cparams(("parallel", "parallel")),
        name="proj_ab",
    )(x, w6, wf, bf, *tabs)


def _cumsum_kernel(x_ref, o_ref, carry_ref):
    @pl.when(pl.program_id(0) == 0)
    def _():
        carry_ref[...] = jnp.zeros_like(carry_ref)

    x = x_ref[...]
    tc = x.shape[1]
    hi = x.astype(BF16)
    r1 = x - hi.astype(F32)
    mid = r1.astype(BF16)
    lo = (r1 - mid.astype(F32)).astype(BF16)
    row = lax.broadcasted_iota(jnp.int32, (tc, tc), 0)
    col = lax.broadcasted_iota(jnp.int32, (tc, tc), 1)
    upper = (row <= col).astype(BF16)
    dot = lambda a: jnp.dot(a, upper, preferred_element_type=F32)
    c = dot(hi) + dot(mid) + dot(lo) + carry_ref[...]
    carry_ref[...] = c[:, tc - 1:tc]
    bias = c * (-LOG2E)
    b_hi = bias.astype(BF16)
    b_r = bias - b_hi.astype(F32)
    b_mid = b_r.astype(BF16)
    o_ref[0] = b_hi
    o_ref[1] = b_mid
    o_ref[2] = (b_r - b_mid.astype(F32)).astype(BF16)


def _decay_bias(logf_t):
    r, t = logf_t.shape
    tc = min(ATTN_BLOCK, t)
    assert t % tc == 0
    return pl.pallas_call(
        _cumsum_kernel,
        grid=(t // tc,),
        in_specs=[pl.BlockSpec((r, tc), lambda i: (0, i))],
        out_specs=pl.BlockSpec((BIAS_PIECES, r, tc), lambda i: (0, 0, i)),
        out_shape=jax.ShapeDtypeStruct((BIAS_PIECES, r, t), BF16),
        scratch_shapes=[pltpu.VMEM((r, 1), F32)],
        compiler_params=_cparams(("arbitrary",)),
        name="cumsum",
    )(logf_t)


def _attn_kernel(*refs, mode, tq, tk, mask_shift, nfull_static, lam_init):
    if mode == "diff":
        q_ref, k_ref, vt_ref, lq1_ref, lk1_ref, lq2_ref, lk2_ref, sub_ref, o_ref, m_sc, l_sc, acc_sc, sa_sc, sb_sc = refs
    else:
        q_ref, k_ref, vt_ref, o_ref, m_sc, l_sc, acc_sc, sa_sc, sb_sc = refs

    qi = pl.program_id(2)
    q = q_ref[0]
    lane = lax.broadcasted_iota(jnp.int32, (1, LANES), 1)
    if mode == "mla":
        qs = [q[:, :LANES], q[:, LANES:]]
    else:
        zero = jnp.zeros_like(q)
        qs = [jnp.where(lane < HEAD_DIM, q, zero), jnp.where(lane >= HEAD_DIM, q, zero)]
        if mode == "fox":
            def pick(i):
                hot = jnp.where((lane >= BIAS_PIECES * i) & (lane < BIAS_PIECES * (i + 1)), 1.0, 0.0)
                return jnp.broadcast_to(hot, (tq, LANES)).astype(BF16)

            qs = [jnp.concatenate([qs[i], pick(i)], axis=1) for i in range(2)]

    m_sc[...] = jnp.full(m_sc.shape, NEG_INF, F32)
    l_sc[...] = jnp.zeros(l_sc.shape, F32)
    acc_sc[...] = jnp.zeros(acc_sc.shape, F32)

    def scores(j, s_sc):
        k = k_ref[0, j]
        for i in range(2):
            ki = k[:, i * LANES:(i + 1) * LANES] if mode == "mla" else k
            s_sc[i] = lax.dot_general(ki, qs[i], (((1,), (1,)), ((), ())), preferred_element_type=F32)

    def consume(j, s_sc, masked):
        vt = vt_ref[0, j]
        for i in range(2):
            st = s_sc[i]
            if masked:
                key = lax.broadcasted_iota(jnp.int32, (tk, tq), 0)
                qry = lax.broadcasted_iota(jnp.int32, (tk, tq), 1)
                vis = lax.shift_right_logical(key, mask_shift) <= lax.shift_right_logical(qry, mask_shift)
                st = jnp.where(vis, st, NEG_INF)
            m_prev = m_sc[i]
            m_new = jnp.maximum(m_prev, jnp.max(st, axis=0, keepdims=True))
            alpha = jnp.exp2(m_prev - m_new)
            p = jnp.exp2(st - m_new)
            l_sc[i] = alpha * l_sc[i] + jnp.sum(p.reshape(tk // SUBLANES, SUBLANES, tq), axis=0)
            acc_sc[i] = alpha * acc_sc[i] + jnp.dot(vt, p.astype(BF16), preferred_element_type=F32)
            m_sc[i] = m_new

    nfull = qi if nfull_static is None else nfull_static

    def pair(jj, carry):
        scores(2 * jj + 1, sb_sc)
        consume(2 * jj, sa_sc, False)
        scores(2 * jj + 2, sa_sc)
        consume(2 * jj + 1, sb_sc, False)
        return carry

    def tail_even():
        consume(nfull, sa_sc, True)

    def tail_odd():
        scores(nfull, sb_sc)
        consume(nfull - 1, sa_sc, False)
        consume(nfull, sb_sc, True)

    scores(0, sa_sc)
    if nfull_static is None:
        lax.fori_loop(0, lax.shift_right_logical(nfull, 1), pair, 0)
        pl.when((nfull & 1) == 0)(tail_even)
        pl.when((nfull & 1) == 1)(tail_odd)
    else:
        lax.fori_loop(0, nfull // 2, pair, 0)
        tail_even() if nfull % 2 == 0 else tail_odd()

    o0 = acc_sc[0] / jnp.sum(l_sc[0], axis=0, keepdims=True)
    o1 = acc_sc[1] / jnp.sum(l_sc[1], axis=0, keepdims=True)
    if mode == "diff":
        lam = (jnp.exp(jnp.sum(lq1_ref[...] * lk1_ref[...], axis=1, keepdims=True))
               - jnp.exp(jnp.sum(lq2_ref[...] * lk2_ref[...], axis=1, keepdims=True)) + lam_init)
        o = o0 - lam * o1
        ms = jnp.mean(o * o, axis=0, keepdims=True)
        o = (o * lax.rsqrt(ms + RMS_EPS)).T * sub_ref[...] * (1.0 - lam_init)
    else:
        row = lax.broadcasted_iota(jnp.int32, (LANES, 1), 0)
        o = jnp.where(row < HEAD_DIM, o0, o1).T
    o_ref[0] = o.astype(o_ref.dtype)


def _attention(mode, q, k, vt, extra, *, n_pairs, mask_shift, nfull_static=None, lam_init=0.0):
    nb, t_q, _ = q.shape
    _, nkb, tk, _ = k.shape
    tq = min(ATTN_BLOCK, t_q)
    assert t_q % tq == 0
    if nfull_static is None:
        assert tq == tk and nkb == t_q // tq
    qw = 2 * LANES if mode == "mla" else LANES
    kw = LANES if mode == "diff" else 2 * LANES
    in_specs = [
        pl.BlockSpec((1, tq, qw), lambda b, p, i: (b, i, p)),
        pl.BlockSpec((1, nkb, tk, kw), lambda b, p, i: (b, 0, 0, p)),
        pl.BlockSpec((1, nkb, LANES, tk), lambda b, p, i: (b, 0, p, 0)),
    ]
    if mode == "diff":
        in_specs += [pl.BlockSpec(a.shape, lambda b, p, i: (0, 0)) for a in extra]
    kern = functools.partial(_attn_kernel, mode=mode, tq=tq, tk=tk, mask_shift=mask_shift,
                             nfull_static=nfull_static, lam_init=lam_init)
    return pl.pallas_call(
        kern,
        grid=(nb, n_pairs, t_q // tq),
        in_specs=in_specs,
        out_specs=pl.BlockSpec((1, tq, LANES), lambda b, p, i: (b, i, p)),
        out_shape=jax.ShapeDtypeStruct((nb, t_q, n_pairs * LANES), BF16),
        scratch_shapes=[pltpu.VMEM((2, 1, tq), F32), pltpu.VMEM((2, SUBLANES, tq), F32),
                        pltpu.VMEM((2, LANES, tq), F32),
                        pltpu.VMEM((2, tk, tq), F32), pltpu.VMEM((2, tk, tq), F32)],
        compiler_params=_cparams(("parallel", "parallel", "arbitrary")),
        name="attn_" + mode,
    )(q, k, vt, *extra)


def _outproj_ln_kernel(*refs, n_in, alpha):
    x_ref = refs[0]
    o_refs = refs[1:1 + n_in]
    w_refs = refs[1 + n_in:1 + 2 * n_in]
    g_ref, b_ref, y_ref = refs[1 + 2 * n_in:]
    mix = jnp.dot(o_refs[0][...], w_refs[0][...], preferred_element_type=F32)
    for o_r, w_r in zip(o_refs[1:], w_refs[1:]):
        mix = mix + jnp.dot(o_r[...], w_r[...], preferred_element_type=F32)
    y_ref[...] = _layer_norm(alpha * x_ref[...] + mix, g_ref[...], b_ref[...])


def _outproj_ln(x, outs, ws, g, b, alpha):
    n, d = x.shape
    tm = min(ROW_TILE, n)
    assert n % tm == 0
    row = lambda width: pl.BlockSpec((tm, width), lambda i: (i, 0))
    full = lambda a: pl.BlockSpec(a.shape, lambda i: (0, 0))
    return pl.pallas_call(
        functools.partial(_outproj_ln_kernel, n_in=len(outs), alpha=alpha),
        grid=(n // tm,),
        in_specs=[row(d)] + [row(o.shape[1]) for o in outs] + [full(w) for w in ws] + [full(g), full(b)],
        out_specs=row(d),
        out_shape=jax.ShapeDtypeStruct((n, d), F32),
        compiler_params=_cparams(("parallel",)),
        name="outproj_ln",
    )(x, *outs, *ws, g, b)


def _route(logits):
    lane = lax.broadcasted_iota(jnp.int32, logits.shape, 1).astype(F32)
    big = float(1 << 20)
    is_g = lane < N_GROUPS
    lg = jnp.where(is_g, logits, NEG_INF)
    eg = jnp.where(is_g, jnp.exp(lg - jnp.max(lg, axis=1, keepdims=True)), 0.0)
    pg = eg / jnp.sum(eg, axis=1, keepdims=True)
    p_g = jnp.max(pg, axis=1, keepdims=True)
    gidx = jnp.min(jnp.where(is_g & (pg == p_g), lane, big), axis=1, keepdims=True)
    lo = GATE_LANE0 + EXPERTS_PER_GROUP * gidx
    sel = (lane >= lo) & (lane < lo + EXPERTS_PER_GROUP)
    le = jnp.where(sel, logits, NEG_INF)
    ee = jnp.where(sel, jnp.exp(le - jnp.max(le, axis=1, keepdims=True)), 0.0)
    pe = ee / jnp.sum(ee, axis=1, keepdims=True)
    v1 = jnp.max(jnp.where(sel, pe, -1.0), axis=1, keepdims=True)
    i1 = jnp.min(jnp.where(sel & (pe == v1), lane, big), axis=1, keepdims=True)
    rest = sel & (lane != i1)
    v2 = jnp.max(jnp.where(rest, pe, -1.0), axis=1, keepdims=True)
    i2 = jnp.min(jnp.where(rest & (pe == v2), lane, big), axis=1, keepdims=True)
    tot = v1 + v2
    w1 = v1 / tot * p_g
    w2 = v2 / tot * p_g
    return jnp.where(lane == i1, w1, jnp.where(lane == i2, w2, 0.0))


def _moe_ln_kernel(x_ref, wrh_ref, wrl_ref, br_ref, w13_ref, w2_ref, g_ref, b_ref, y_ref,
                   xb_sc, gate_sc, acc_sc, *, alpha):
    e = pl.program_id(1)

    @pl.when(e == 0)
    def _():
        x = x_ref[...]
        xh = x.astype(BF16)
        xl = (x - xh.astype(F32)).astype(BF16)
        xb_sc[...] = xh
        logits = (jnp.dot(xh, wrh_ref[...], preferred_element_type=F32)
                  + jnp.dot(xl, wrh_ref[...], preferred_element_type=F32)
                  + jnp.dot(xh, wrl_ref[...], preferred_element_type=F32) + br_ref[...])
        gate_sc[...] = _route(logits)
        acc_sc[...] = jnp.zeros_like(acc_sc)

    h = jnp.dot(xb_sc[...], w13_ref[0], preferred_element_type=F32)
    h1 = h[:, :D_EXPERT]
    h3 = h[:, D_EXPERT:]
    hdn = (h1 * jax.nn.sigmoid(h1)) * h3
    y = jnp.dot(hdn.astype(BF16), w2_ref[0], preferred_element_type=F32)
    lane = lax.broadcasted_iota(jnp.int32, (1, LANES), 1)
    ge = jnp.sum(jnp.where(lane == e + GATE_LANE0, gate_sc[...], 0.0), axis=1, keepdims=True)
    acc_sc[...] += ge * y

    @pl.when(e == pl.num_programs(1) - 1)
    def _():
        y_ref[...] = _layer_norm(alpha * x_ref[...] + acc_sc[...], g_ref[...], b_ref[...])


def _moe_ln(x, wrh, wrl, br, w13, w2, g, b, alpha):
    n, d = x.shape
    tm = min(MOE_TILE, n)
    assert n % tm == 0
    ne = w13.shape[0]
    full = lambda a: pl.BlockSpec(a.shape, lambda i, e: (0, 0))
    return pl.pallas_call(
        functools.partial(_moe_ln_kernel, alpha=alpha),
        grid=(n // tm, ne),
        in_specs=[pl.BlockSpec((tm, d), lambda i, e: (i, 0)), full(wrh), full(wrl), full(br),
                  pl.BlockSpec((1,) + w13.shape[1:], lambda i, e: (e, 0, 0)),
                  pl.BlockSpec((1,) + w2.shape[1:], lambda i, e: (e, 0, 0)),
                  full(g), full(b)],
        out_specs=pl.BlockSpec((tm, d), lambda i, e: (i, 0)),
        out_shape=jax.ShapeDtypeStruct((n, d), F32),
        scratch_shapes=[pltpu.VMEM((tm, d), BF16), pltpu.VMEM((tm, LANES), F32), pltpu.VMEM((tm, d), F32)],
        compiler_params=_cparams(("parallel", "arbitrary")),
        name="moe_ln",
    )(x, wrh, wrl, br, w13, w2, g, b)


def _proj_c_kernel(x_ref, win_ref, gq_ref, gkv_ref, wuq_ref, cq_ref, s1q_ref, s2q_ref,
                   ck_ref, s1k_ref, s2k_ref, q_ref, ckv_ref, kr_ref):
    xb = x_ref[0].astype(BF16)
    h = jnp.dot(xb, win_ref[...], preferred_element_type=F32)
    qa = h[:, :Q_RANK]
    kva = h[:, Q_RANK:Q_RANK + KV_RANK]
    krw = h[:, Q_RANK + KV_RANK:]
    qn = qa * lax.rsqrt(jnp.mean(qa * qa, axis=1, keepdims=True) + RMS_EPS) * gq_ref[...]
    ckv_ref[0] = kva * lax.rsqrt(jnp.mean(kva * kva, axis=1, keepdims=True) + RMS_EPS) * gkv_ref[...]
    half = ROPE_DIM // 2
    kr = _rope3(krw, ck_ref[...], s1k_ref[...], s2k_ref[...], LANES - half, half)
    kr_ref[0] = kr[:, :ROPE_DIM]
    q = jnp.dot(qn.astype(BF16), wuq_ref[...], preferred_element_type=F32)
    cq, s1q, s2q = cq_ref[...], s1q_ref[...], s2q_ref[...]
    scale = (NOPE_DIM + ROPE_DIM) ** -0.5 * LOG2E
    for hd in range(H_C):
        sl = slice(hd * LANES, (hd + 1) * LANES)
        q_ref[0, :, sl] = (_rope3(q[:, sl], cq, s1q, s2q, LANES - half, half) * scale).astype(BF16)


def _proj_c(x, win, gq, gkv, wuq, tabs_q, tabs_k):
    nb, t, _ = x.shape
    tm = min(ROW_TILE, t)
    assert t % tm == 0
    tok = lambda width: pl.BlockSpec((1, tm, width), lambda b, i: (b, i, 0))
    tab = pl.BlockSpec((tm, LANES), lambda b, i: (i, 0))
    full = lambda a: pl.BlockSpec(a.shape, lambda b, i: (0, 0))
    return pl.pallas_call(
        _proj_c_kernel,
        grid=(nb, t // tm),
        in_specs=[tok(D_MODEL), full(win), full(gq), full(gkv), full(wuq)] + [tab] * 6,
        out_specs=[tok(H_C * LANES), tok(KV_RANK), tok(ROPE_DIM)],
        out_shape=[jax.ShapeDtypeStruct((nb, t, H_C * LANES), BF16),
                   jax.ShapeDtypeStruct((nb, t, KV_RANK), F32),
                   jax.ShapeDtypeStruct((nb, t, ROPE_DIM), F32)],
        compiler_params=_cparams(("parallel", "parallel")),
        name="proj_c",
    )(x, win, gq, gkv, wuq, *tabs_q, *tabs_k)


def _kv_up_kernel(ckv_ref, kr_ref, wk_ref, place_ref, wv_ref, k_ref, v_ref):
    cb = ckv_ref[...].astype(BF16)
    k = (jnp.dot(cb, wk_ref[...], preferred_element_type=F32)
         + jnp.dot(kr_ref[...].astype(BF16), place_ref[...], preferred_element_type=F32))
    k_ref[...] = k.astype(BF16)
    v_ref[...] = jnp.dot(cb, wv_ref[...], preferred_element_type=F32).astype(BF16)


def _kv_up(ckv, kr, wk, place, wv):
    n = ckv.shape[0]
    tm = min(ROW_TILE, n)
    assert n % tm == 0
    row = lambda width: pl.BlockSpec((tm, width), lambda i: (i, 0))
    full = lambda a: pl.BlockSpec(a.shape, lambda i: (0, 0))
    return pl.pallas_call(
        _kv_up_kernel,
        grid=(n // tm,),
        in_specs=[row(KV_RANK), row(ROPE_DIM), full(wk), full(place), full(wv)],
        out_specs=[row(H_C * LANES), row(H_C * V_DIM_C)],
        out_shape=[jax.ShapeDtypeStruct((n, H_C * LANES), BF16),
                   jax.ShapeDtypeStruct((n, H_C * V_DIM_C), BF16)],
        compiler_params=_cparams(("parallel",)),
        name="kv_up",
    )(ckv, kr, wk, place, wv)


def _rope_tables(pos, dim, lane0):
    half = dim // 2
    inv = ROPE_THETA ** (-jnp.arange(0, dim, 2, dtype=F32) / dim)
    ang = pos.astype(F32)[:, None] * inv[None, :]
    cos, sin = jnp.cos(ang), jnp.sin(ang)
    zero = jnp.zeros_like(sin)
    c = jnp.concatenate([cos, cos], axis=1)
    s1 = jnp.concatenate([-sin, zero], axis=1)
    s2 = jnp.concatenate([zero, sin], axis=1)
    if lane0 < 0:
        reps = LANES // dim
        return tuple(jnp.tile(a, (1, reps)) for a in (c, s1, s2))
    t = pos.shape[0]
    pad = lambda a, fill: jnp.concatenate(
        [jnp.full((t, lane0), fill, F32), a, jnp.full((t, LANES - lane0 - dim), fill, F32)], axis=1)
    return pad(c, 1.0), pad(s1, 0.0), pad(s2, 0.0)


def _pad_cols(a, width):
    return jnp.pad(a, ((0, 0), (0, width - a.shape[1])))


def _blocks(a, tk):
    nb, t, l = a.shape
    return a.reshape(nb, t // tk, tk, l)


def _vt_blocks(v, tk):
    nb, t, l = v.shape
    return v.reshape(nb, t // tk, tk, l).transpose(0, 1, 3, 2)


def _fox_keys(kb, pieces, tk):
    nb, t, w = kb.shape
    n_pairs = w // LANES
    pc = pieces.reshape(BIAS_PIECES, nb, n_pairs, 2, t).transpose(1, 4, 2, 3, 0)
    pc = pc.reshape(nb, t, n_pairs, 2 * BIAS_PIECES)
    pc = jnp.pad(pc, ((0, 0), (0, 0), (0, 0), (0, LANES - 2 * BIAS_PIECES)))
    ka = jnp.concatenate([kb.reshape(nb, t, n_pairs, LANES), pc], axis=-1)
    return _blocks(ka.reshape(nb, t, n_pairs * 2 * LANES), tk)


def _pad_rows(q, rows):
    return jnp.pad(q, ((0, 0), (0, rows - q.shape[1]), (0, 0)))


def _cat_pad_time(cache, new, t_pad):
    nb, t0, l = cache.shape
    t1 = new.shape[1]
    return jnp.concatenate([cache, new, jnp.zeros((nb, t_pad - t0 - t1, l), cache.dtype)], axis=1)


def kernel(x_prompt, x_sample, cache_fox_k, cache_fox_v, cache_fox_logf, cache_diff_k, cache_diff_v, cache_mla_ckv, cache_mla_krope, w_in_ab, b_fgate, diff_lq1, diff_lk1, diff_lq2, diff_lk2, diff_subln, w_out_ab, w_in_c, mla_q_norm, mla_kv_norm, mla_w_uq, mla_w_ukv, w_out_c, ln1_g, ln1_b, ln2_g, ln2_b, moe_wg, moe_bg, moe_we, moe_be, moe_w1, moe_w3, moe_w2):
    bp, tp, d = x_prompt.shape
    bs, ts, _ = x_sample.shape
    past = cache_fox_k.shape[2]
    depth = ln1_g.shape[0]
    alpha = (2 * depth) ** 0.25
    tk = ATTN_BLOCK
    assert past % tk == 0 and ts == 16 and past % CHUNK == 0
    ns = bs * ts
    t_dec = past + tk
    nfull_dec = past // tk
    dec_shift = 4

    pos_p = jnp.arange(tp)
    pos_s = jnp.tile(past + jnp.arange(ts), bs)

    xp = x_prompt
    xs = x_sample.reshape(1, ns, d)
    out_ab_p, out_ab_s, out_c_p, out_c_s = [], [], [], []

    for i in range(depth):
        j = i // 2
        if i % 2 == 0:
            lam_init = 0.8 - 0.6 * math.exp(-0.3 * i)
            cuts = [0, A_WIDTH, 2 * A_WIDTH, 3 * A_WIDTH, 3 * A_WIDTH + H_A,
                    3 * A_WIDTH + H_A + B_QK_WIDTH, 3 * A_WIDTH + H_A + 2 * B_QK_WIDTH,
                    3 * A_WIDTH + H_A + 2 * B_QK_WIDTH + B_V_WIDTH]
            w = w_in_ab[j]
            piece = lambda a: w[:, cuts[a]:cuts[a + 1]]
            w6 = jnp.stack([piece(0), piece(1), piece(2), piece(4), piece(5), piece(6)]).astype(BF16)
            wf = _pad_cols(piece(3), LANES).astype(BF16)
            bf = _pad_cols(b_fgate[j][None, :], LANES)
            wout = w_out_ab[j].astype(BF16)
            diff_extra = (diff_lq1[j][None, :], diff_lk1[j][None, :], diff_lq2[j][None, :],
                          diff_lk2[j][None, :], diff_subln[j][None, :])

            tabs = _rope_tables(pos_p, HEAD_DIM, -1)
            (qa, ka, kab, va, vab, lf, qb, kb, kbb, vb, vbb) = _proj_ab(xp, w6, wf, bf, tabs)
            bias = _decay_bias(jnp.swapaxes(lf, 1, 2).reshape(bp * H_A, tp))
            oa = _attention("fox", qa, _fox_keys(kab, bias, tk), _vt_blocks(vab, tk), (),
                            n_pairs=H_A // 2, mask_shift=0)
            ob = _attention("diff", qb, _blocks(kbb, tk), _vt_blocks(vbb, tk), diff_extra,
                            n_pairs=H_B, mask_shift=int(math.log2(CHUNK)), lam_init=lam_init)
            out_ab_p.append((ka.reshape(bp, tp, H_A, HEAD_DIM), va.reshape(bp, tp, H_A, HEAD_DIM), lf,
                             kb.reshape(bp, tp, H_B, 2, HEAD_DIM), vb.reshape(bp, tp, H_B, 2 * HEAD_DIM)))
            xp2 = _outproj_ln(xp.reshape(bp * tp, d), [oa.reshape(bp * tp, -1), ob.reshape(bp * tp, -1)],
                              [wout[:A_WIDTH], wout[A_WIDTH:]], ln1_g[i][None, :], ln1_b[i][None, :], alpha)

            tabs = _rope_tables(pos_s, HEAD_DIM, -1)
            (qa, ka, kab, va, vab, lf, qb, kb, kbb, vb, vbb) = _proj_ab(xs, w6, wf, bf, tabs)
            rs = lambda a: a.reshape(bs, ts, a.shape[-1])
            cache_lf = jnp.swapaxes(cache_fox_logf[j].astype(F32), 1, 2)
            lf_all = jnp.concatenate([cache_lf, jnp.swapaxes(rs(lf), 1, 2),
                                      jnp.zeros((bs, H_A, tk - ts), F32)], axis=2)
            bias = _decay_bias(lf_all.reshape(bs * H_A, t_dec))
            flat = lambda c: c.reshape(bs, past, -1).astype(BF16)
            qdec = lambda a: _pad_rows(rs(a), DEC_Q_ROWS)
            k_all = _fox_keys(_cat_pad_time(flat(cache_fox_k[j]), rs(kab), t_dec), bias, tk)
            v_all = _vt_blocks(_cat_pad_time(flat(cache_fox_v[j]), rs(vab), t_dec), tk)
            oa = _attention("fox", qdec(qa), k_all, v_all, (), n_pairs=H_A // 2, mask_shift=0,
                            nfull_static=nfull_dec)[:, :ts]
            k_all = _blocks(_cat_pad_time(flat(cache_diff_k[j]), rs(kbb), t_dec), tk)
            v_all = _vt_blocks(_cat_pad_time(flat(cache_diff_v[j]), rs(vbb), t_dec), tk)
            ob = _attention("diff", qdec(qb), k_all, v_all, diff_extra, n_pairs=H_B, mask_shift=dec_shift,
                            nfull_static=nfull_dec, lam_init=lam_init)[:, :ts]
            out_ab_s.append((ka.reshape(bs, ts, H_A, HEAD_DIM), va.reshape(bs, ts, H_A, HEAD_DIM),
                             lf.reshape(bs, ts, H_A), kb.reshape(bs, ts, H_B, 2, HEAD_DIM),
                             vb.reshape(bs, ts, H_B, 2 * HEAD_DIM)))
            xs2 = _outproj_ln(xs.reshape(ns, d), [oa.reshape(ns, -1), ob.reshape(ns, -1)],
                              [wout[:A_WIDTH], wout[A_WIDTH:]], ln1_g[i][None, :], ln1_b[i][None, :], alpha)
        else:
            wc = w_in_c[j]
            kr_cols = _pad_cols(wc[:, Q_RANK + KV_RANK:], LANES)
            win = jnp.concatenate([wc[:, :Q_RANK + KV_RANK], kr_cols], axis=1).astype(BF16)
            wuq = jnp.pad(mla_w_uq[j].reshape(Q_RANK, H_C, NOPE_DIM + ROPE_DIM),
                          ((0, 0), (0, 0), (0, LANES - NOPE_DIM - ROPE_DIM))).reshape(Q_RANK, H_C * LANES)
            wuq = wuq.astype(BF16)
            wukv = mla_w_ukv[j].reshape(KV_RANK, H_C, NOPE_DIM + V_DIM_C)
            wk = jnp.pad(wukv[:, :, :NOPE_DIM], ((0, 0), (0, 0), (0, LANES - NOPE_DIM)))
            wk = wk.reshape(KV_RANK, H_C * LANES).astype(BF16)
            wv = wukv[:, :, NOPE_DIM:].reshape(KV_RANK, H_C * V_DIM_C).astype(BF16)
            place = jnp.tile(_pad_cols(jnp.concatenate(
                [jnp.zeros((ROPE_DIM, NOPE_DIM), F32), jnp.eye(ROPE_DIM, dtype=F32)], axis=1), LANES),
                (1, H_C)).astype(BF16)
            gq = mla_q_norm[j][None, :]
            gkv = mla_kv_norm[j][None, :]
            wout = w_out_c[j].astype(BF16)

            q, ckv, kr = _proj_c(xp, win, gq, gkv, wuq, _rope_tables(pos_p, ROPE_DIM, NOPE_DIM),
                                 _rope_tables(pos_p, ROPE_DIM, 0))
            kc, vc = _kv_up(ckv.reshape(bp * tp, KV_RANK), kr.reshape(bp * tp, ROPE_DIM), wk, place, wv)
            oc = _attention("mla", q, _blocks(kc.reshape(bp, tp, -1), tk), _vt_blocks(vc.reshape(bp, tp, -1), tk),
                            (), n_pairs=H_C // 2, mask_shift=int(math.log2(CHUNK)))
            out_c_p.append((ckv, kr))
            xp2 = _outproj_ln(xp.reshape(bp * tp, d), [oc.reshape(bp * tp, -1)], [wout],
                              ln1_g[i][None, :], ln1_b[i][None, :], alpha)

            q, ckv, kr = _proj_c(xs, win, gq, gkv, wuq, _rope_tables(pos_s, ROPE_DIM, NOPE_DIM),
                                 _rope_tables(pos_s, ROPE_DIM, 0))
            ckv_all = _cat_pad_time(cache_mla_ckv[j].astype(F32), ckv.reshape(bs, ts, KV_RANK), t_dec)
            kr_all = _cat_pad_time(cache_mla_krope[j].astype(F32), kr.reshape(bs, ts, ROPE_DIM), t_dec)
            kc, vc = _kv_up(ckv_all.reshape(bs * t_dec, KV_RANK), kr_all.reshape(bs * t_dec, ROPE_DIM),
                            wk, place, wv)
            oc = _attention("mla", _pad_rows(q.reshape(bs, ts, -1), DEC_Q_ROWS),
                            _blocks(kc.reshape(bs, t_dec, -1), tk), _vt_blocks(vc.reshape(bs, t_dec, -1), tk),
                            (), n_pairs=H_C // 2, mask_shift=dec_shift, nfull_static=nfull_dec)[:, :ts]
            out_c_s.append((ckv.reshape(bs, ts, KV_RANK), kr.reshape(bs, ts, ROPE_DIM)))
            xs2 = _outproj_ln(xs.reshape(ns, d), [oc.reshape(ns, -1)], [wout],
                              ln1_g[i][None, :], ln1_b[i][None, :], alpha)

        wr = _pad_cols(jnp.concatenate(
            [moe_wg[i]] + [moe_we[i][gi] for gi in range(N_GROUPS)], axis=1), LANES)
        wrh = wr.astype(BF16)
        wrl = (wr - wrh.astype(F32)).astype(BF16)
        br = _pad_cols(jnp.concatenate([moe_bg[i], moe_be[i].reshape(-1)])[None, :], LANES)
        w13 = jnp.concatenate([moe_w1[i], moe_w3[i]], axis=2).astype(BF16)
        w2 = moe_w2[i].astype(BF16)
        g2, b2 = ln2_g[i][None, :], ln2_b[i][None, :]
        xp = _moe_ln(xp2, wrh, wrl, br, w13, w2, g2, b2, alpha).reshape(bp, tp, d)
        xs = _moe_ln(xs2, wrh, wrl, br, w13, w2, g2, b2, alpha).reshape(1, ns, d)

    stack = lambda rows, n: jnp.stack([r[n] for r in rows])
    return (xp, xs.reshape(bs, ts, d),
            stack(out_ab_p, 0), stack(out_ab_p, 1), stack(out_ab_p, 2), stack(out_ab_p, 3), stack(out_ab_p, 4),
            stack(out_c_p, 0), stack(out_c_p, 1),
            stack(out_ab_s, 0), stack(out_ab_s, 1), stack(out_ab_s, 2), stack(out_ab_s, 3), stack(out_ab_s, 4),
            stack(out_c_s, 0), stack(out_c_s, 1))
```

```python
import functools
import math

import jax
import jax.numpy as jnp
from jax import lax
from jax.experimental import pallas as pl
from jax.experimental.pallas import tpu as pltpu

F32 = jnp.float32
BF16 = jnp.bfloat16

D_MODEL = 1024
CHUNK = 64
HEAD_DIM = 64
ROPE_THETA = 10000.0
H_A = 8
H_B = 4
H_C = 16
Q_RANK = 256
KV_RANK = 128
NOPE_DIM = 64
ROPE_DIM = 32
V_DIM_C = 64
N_GROUPS = 4
EXPERTS_PER_GROUP = 4
N_EXPERTS = N_GROUPS * EXPERTS_PER_GROUP
D_EXPERT = 256
A_WIDTH = H_A * HEAD_DIM
B_QK_WIDTH = H_B * 2 * HEAD_DIM
B_V_WIDTH = H_B * 2 * HEAD_DIM
FGATE_BIAS = 3.0
LN_EPS = 1e-5
RMS_EPS = 1e-6
NEG_INF = -1e30
LOG2E = math.log2(math.e)

LANES = 128
SUBLANES = 8
BIAS_PIECES = 3
DEC_Q_ROWS = 128
VMEM_LIMIT = 48 * 1024 * 1024
ATTN_BLOCK = 512
ROW_TILE = 512
MOE_TILE = 1024
GATE_LANE0 = N_GROUPS


def _cparams(sem):
    return pltpu.CompilerParams(dimension_semantics=sem, vmem_limit_bytes=VMEM_LIMIT)


def _rope3(x, c, s1, s2, shift_up, shift_down):
    return x * c + pltpu.roll(x, shift_up, 1) * s1 + pltpu.roll(x, shift_down, 1) * s2


def _layer_norm(y, g, b):
    mu = jnp.mean(y, axis=-1, keepdims=True)
    d = y - mu
    var = jnp.mean(d * d, axis=-1, keepdims=True)
    return d * lax.rsqrt(var + LN_EPS) * g + b


def _split3(x):
    hi = x.astype(BF16)
    r1 = x - hi.astype(F32)
    mid = r1.astype(BF16)
    return hi, mid, (r1 - mid.astype(F32)).astype(BF16)


def _proj_ab_kernel(x_ref, w_ref, wvt_ref, wf_ref, bf_ref, c_ref, s1_ref, s2_ref,
                    qa_ref, ka_ref, kab_ref, va_ref, vat_ref, lf_ref, lfw_ref,
                    qb_ref, kb_ref, kbb_ref, vb_ref, vbt_ref):
    xb = x_ref[0].astype(BF16)

    def mm(i):
        return jnp.dot(xb, w_ref[i], preferred_element_type=F32)

    def mm_t(i):
        return lax.dot_general(wvt_ref[i], xb, (((1,), (1,)), ((), ())), preferred_element_type=F32)

    qa_ref[0] = (mm(0) * (HEAD_DIM ** -0.5 * LOG2E)).astype(BF16)
    ka = mm(1)
    ka_ref[0] = ka
    kab_ref[0] = ka.astype(BF16)
    va_ref[0] = mm(2)
    vat_ref[0, 0] = mm_t(0).astype(BF16)

    z = jnp.dot(xb, wf_ref[...], preferred_element_type=F32) + bf_ref[...]
    lf = jnp.minimum(z, 0.0) - jnp.log1p(jnp.exp(-jnp.abs(z)))
    lf_ref[0] = lf[:, :H_A]
    lfw_ref[0] = lf

    c, s1, s2 = c_ref[...], s1_ref[...], s2_ref[...]
    qb = mm(3)
    kb = mm(4)
    for s in range(B_QK_WIDTH // LANES):
        sl = slice(s * LANES, (s + 1) * LANES)
        qs = _rope3(qb[:, sl], c, s1, s2, LANES - HEAD_DIM // 2, HEAD_DIM // 2)
        qb_ref[0, :, sl] = (qs * (HEAD_DIM ** -0.5 * LOG2E)).astype(BF16)
        ks = _rope3(kb[:, sl], c, s1, s2, LANES - HEAD_DIM // 2, HEAD_DIM // 2)
        kb_ref[0, :, sl] = ks
        kbb_ref[0, :, sl] = ks.astype(BF16)
    vb_ref[0] = mm(5)
    vbt_ref[0, 0] = mm_t(1).astype(BF16)


def _proj_ab(x, w6, wvt, wf, bf, tabs):
    nb, t, _ = x.shape
    tm = min(ROW_TILE, t)
    assert t % tm == 0
    w = A_WIDTH
    tok = lambda width: pl.BlockSpec((1, tm, width), lambda b, i: (b, i, 0))
    tr = pl.BlockSpec((1, 1, w, tm), lambda b, i: (b, i, 0, 0))
    tab = pl.BlockSpec((tm, LANES), lambda b, i: (i, 0))
    full = lambda a: pl.BlockSpec(a.shape, lambda b, i: (0,) * a.ndim)
    sds = lambda width, dt: jax.ShapeDtypeStruct((nb, t, width), dt)
    sds_t = jax.ShapeDtypeStruct((nb, t // tm, w, tm), BF16)
    return pl.pallas_call(
        _proj_ab_kernel,
        grid=(nb, t // tm),
        in_specs=[tok(D_MODEL), full(w6), full(wvt), full(wf), full(bf), tab, tab, tab],
        out_specs=[tok(w), tok(w), tok(w), tok(w), tr, tok(H_A), tok(LANES), tok(w), tok(w), tok(w), tok(w), tr],
        out_shape=[sds(w, BF16), sds(w, F32), sds(w, BF16), sds(w, F32), sds_t, sds(H_A, F32), sds(LANES, F32),
                   sds(w, BF16), sds(w, F32), sds(w, BF16), sds(w, F32), sds_t],
        compiler_params=_cparams(("parallel", "parallel")),
        name="proj_ab",
    )(x, w6, wvt, wf, bf, *tabs)


def _decay_bias_kernel(lf_ref, o_ref, carry_ref):
    @pl.when(pl.program_id(1) == 0)
    def _():
        carry_ref[...] = jnp.zeros_like(carry_ref)

    x = lf_ref[0]
    tc = x.shape[0]
    src = lax.broadcasted_iota(jnp.int32, (LANES, LANES), 0)
    dst = lax.broadcasted_iota(jnp.int32, (LANES, LANES), 1)
    spread = ((dst >= BIAS_PIECES * src) & (dst < BIAS_PIECES * (src + 1)) & (src < H_A)).astype(BF16)
    xr = sum(jnp.dot(p, spread, preferred_element_type=F32) for p in _split3(x))
    row = lax.broadcasted_iota(jnp.int32, (tc, tc), 0)
    col = lax.broadcasted_iota(jnp.int32, (tc, tc), 1)
    lower = (col <= row).astype(BF16)
    c = sum(jnp.dot(lower, p, preferred_element_type=F32) for p in _split3(xr)) + carry_ref[...]
    carry_ref[...] = c[tc - 1:tc, :]
    hi, mid, lo = (p.astype(F32) for p in _split3(c * (-LOG2E)))
    lane = lax.broadcasted_iota(jnp.int32, (1, LANES), 1).astype(F32)
    piece = lane - BIAS_PIECES * jnp.floor((lane + 0.5) * (1.0 / BIAS_PIECES))
    o_ref[0] = jnp.where(piece == 0.0, hi, jnp.where(piece == 1.0, mid, lo)).astype(BF16)


def _decay_bias(lf_wide):
    nb, t, _ = lf_wide.shape
    tc = min(ATTN_BLOCK, t)
    assert t % tc == 0
    spec = pl.BlockSpec((1, tc, LANES), lambda b, i: (b, i, 0))
    return pl.pallas_call(
        _decay_bias_kernel,
        grid=(nb, t // tc),
        in_specs=[spec],
        out_specs=spec,
        out_shape=jax.ShapeDtypeStruct((nb, t, LANES), BF16),
        scratch_shapes=[pltpu.VMEM((1, LANES), F32)],
        compiler_params=_cparams(("parallel", "arbitrary")),
        name="cumsum",
    )(lf_wide)


def _attn_kernel(*refs, mode, tq, tk, mask_shift, nfull_static, lam_init):
    if mode == "diff":
        q_ref, k_ref, vt_ref, lq1_ref, lk1_ref, lq2_ref, lk2_ref, sub_ref, o_ref = refs[:9]
    elif mode == "fox":
        q_ref, k_ref, vt_ref, b_ref, o_ref = refs[:5]
    else:
        q_ref, k_ref, vt_ref, o_ref = refs[:4]
    m_sc, l_sc, acc_sc, sa, bma, sb, bmb = refs[-7:]
    sa_sc, sb_sc = (sa, bma), (sb, bmb)

    qi = pl.program_id(2)
    q = q_ref[0]
    lane = lax.broadcasted_iota(jnp.int32, (1, LANES), 1)
    if mode == "mla":
        qs = [q[:, :LANES], q[:, LANES:]]
    else:
        zero = jnp.zeros_like(q)
        qs = [jnp.where(lane < HEAD_DIM, q, zero), jnp.where(lane >= HEAD_DIM, q, zero)]
        if mode == "fox":
            def pick(i):
                lo = BIAS_PIECES * (2 * pl.program_id(1) + i)
                hot = jnp.where((lane >= lo) & (lane < lo + BIAS_PIECES), 1.0, 0.0)
                return jnp.broadcast_to(hot, (tq, LANES)).astype(BF16)

            qs = [jnp.concatenate([qs[i], pick(i)], axis=1) for i in range(2)]

    m_sc[...] = jnp.full(m_sc.shape, NEG_INF, F32)
    l_sc[...] = jnp.zeros(l_sc.shape, F32)
    acc_sc[...] = jnp.zeros(acc_sc.shape, F32)

    def scores(j, bufs):
        s_sc, bm_sc = bufs
        k = k_ref[0, j]
        if mode == "fox":
            k = jnp.concatenate([k, b_ref[0, j]], axis=1)
        for i in range(2):
            ki = k[:, i * LANES:(i + 1) * LANES] if mode == "mla" else k
            st = lax.dot_general(ki, qs[i], (((1,), (1,)), ((), ())), preferred_element_type=F32)
            s_sc[i] = st
            bm_sc[i] = jnp.max(st, axis=0, keepdims=True)

    def consume(j, bufs, masked):
        s_sc, bm_sc = bufs
        vt = vt_ref[0, j]
        for i in range(2):
            st = s_sc[i]
            if masked:
                key = lax.broadcasted_iota(jnp.int32, (tk, tq), 0)
                qry = lax.broadcasted_iota(jnp.int32, (tk, tq), 1)
                vis = lax.shift_right_logical(key, mask_shift) <= lax.shift_right_logical(qry, mask_shift)
                st = jnp.where(vis, st, NEG_INF)
                blk_max = jnp.max(st, axis=0, keepdims=True)
            else:
                blk_max = bm_sc[i]
            m_prev = m_sc[i]
            m_new = jnp.maximum(m_prev, blk_max)
            alpha = jnp.exp2(m_prev - m_new)
            p = jnp.exp2(st - m_new)
            l_sc[i] = alpha * l_sc[i] + jnp.sum(p.reshape(tk // SUBLANES, SUBLANES, tq), axis=0)
            vi = vt if mode == "diff" else vt[i * HEAD_DIM:(i + 1) * HEAD_DIM]
            acc_sc[i] = alpha * acc_sc[i] + jnp.dot(vi, p.astype(BF16), preferred_element_type=F32)
            m_sc[i] = m_new

    nfull = qi if nfull_static is None else nfull_static

    def pair(jj, carry):
        scores(2 * jj + 1, sb_sc)
        consume(2 * jj, sa_sc, False)
        scores(2 * jj + 2, sa_sc)
        consume(2 * jj + 1, sb_sc, False)
        return carry

    def tail_even():
        consume(nfull, sa_sc, True)

    def tail_odd():
        scores(nfull, sb_sc)
        consume(nfull - 1, sa_sc, False)
        consume(nfull, sb_sc, True)

    scores(0, sa_sc)
    if nfull_static is None:
        lax.fori_loop(0, lax.shift_right_logical(nfull, 1), pair, 0)
        pl.when((nfull & 1) == 0)(tail_even)
        pl.when((nfull & 1) == 1)(tail_odd)
    else:
        lax.fori_loop(0, nfull // 2, pair, 0)
        tail_even() if nfull % 2 == 0 else tail_odd()

    o0 = acc_sc[0] / jnp.sum(l_sc[0], axis=0, keepdims=True)
    o1 = acc_sc[1] / jnp.sum(l_sc[1], axis=0, keepdims=True)
    if mode == "diff":
        lam = (jnp.exp(jnp.sum(lq1_ref[...] * lk1_ref[...], axis=1, keepdims=True))
               - jnp.exp(jnp.sum(lq2_ref[...] * lk2_ref[...], axis=1, keepdims=True)) + lam_init)
        o = o0 - lam * o1
        ms = jnp.mean(o * o, axis=0, keepdims=True)
        o = (o * lax.rsqrt(ms + RMS_EPS)).T * sub_ref[...] * (1.0 - lam_init)
    else:
        o = jnp.concatenate([o0, o1], axis=0).T
    o_ref[0] = o.astype(o_ref.dtype)


def _attention(mode, q, k, vt, extra, *, n_pairs, mask_shift, nfull_static=None, lam_init=0.0):
    nb, t_q, _ = q.shape
    _, nkb, tk, _ = k.shape
    tq = min(ATTN_BLOCK, t_q)
    assert t_q % tq == 0
    if nfull_static is None:
        assert tq == tk and nkb == t_q // tq
    qw = 2 * LANES if mode == "mla" else LANES
    in_specs = [
        pl.BlockSpec((1, tq, qw), lambda b, p, i: (b, i, p)),
        pl.BlockSpec((1, nkb, tk, qw), lambda b, p, i: (b, 0, 0, p)),
        pl.BlockSpec((1, nkb, LANES, tk), lambda b, p, i: (b, 0, p, 0)),
    ]
    if mode == "fox":
        in_specs.append(pl.BlockSpec((1, nkb, tk, LANES), lambda b, p, i: (b, 0, 0, 0)))
    elif mode == "diff":
        in_specs += [pl.BlockSpec(a.shape, lambda b, p, i: (0, 0)) for a in extra]
    kern = functools.partial(_attn_kernel, mode=mode, tq=tq, tk=tk, mask_shift=mask_shift,
                             nfull_static=nfull_static, lam_init=lam_init)
    return pl.pallas_call(
        kern,
        grid=(nb, n_pairs, t_q // tq),
        in_specs=in_specs,
        out_specs=pl.BlockSpec((1, tq, LANES), lambda b, p, i: (b, i, p)),
        out_shape=jax.ShapeDtypeStruct((nb, t_q, n_pairs * LANES), BF16),
        scratch_shapes=[pltpu.VMEM((2, 1, tq), F32), pltpu.VMEM((2, SUBLANES, tq), F32),
                        pltpu.VMEM((2, LANES if mode == "diff" else HEAD_DIM, tq), F32),
                        pltpu.VMEM((2, tk, tq), F32), pltpu.VMEM((2, 1, tq), F32),
                        pltpu.VMEM((2, tk, tq), F32), pltpu.VMEM((2, 1, tq), F32)],
        compiler_params=_cparams(("parallel", "parallel", "arbitrary")),
        name="attn_" + mode,
    )(q, k, vt, *extra)


def _outproj_ln_kernel(*refs, n_in, alpha):
    x_ref = refs[0]
    o_refs = refs[1:1 + n_in]
    w_refs = refs[1 + n_in:1 + 2 * n_in]
    g_ref, b_ref, y_ref = refs[1 + 2 * n_in:]
    mix = jnp.dot(o_refs[0][...], w_refs[0][...], preferred_element_type=F32)
    for o_r, w_r in zip(o_refs[1:], w_refs[1:]):
        mix = mix + jnp.dot(o_r[...], w_r[...], preferred_element_type=F32)
    y_ref[...] = _layer_norm(alpha * x_ref[...] + mix, g_ref[...], b_ref[...])


def _outproj_ln(x, outs, ws, g, b, alpha):
    n, d = x.shape
    tm = min(ROW_TILE, n)
    assert n % tm == 0
    row = lambda width: pl.BlockSpec((tm, width), lambda i: (i, 0))
    full = lambda a: pl.BlockSpec(a.shape, lambda i: (0, 0))
    return pl.pallas_call(
        functools.partial(_outproj_ln_kernel, n_in=len(outs), alpha=alpha),
        grid=(n // tm,),
        in_specs=[row(d)] + [row(o.shape[1]) for o in outs] + [full(w) for w in ws] + [full(g), full(b)],
        out_specs=row(d),
        out_shape=jax.ShapeDtypeStruct((n, d), F32),
        compiler_params=_cparams(("parallel",)),
        name="outproj_ln",
    )(x, *outs, *ws, g, b)


def _route(logits):
    lane = lax.broadcasted_iota(jnp.int32, logits.shape, 1).astype(F32)
    big = float(1 << 20)
    is_g = lane < N_GROUPS
    lg = jnp.where(is_g, logits, NEG_INF)
    eg = jnp.where(is_g, jnp.exp(lg - jnp.max(lg, axis=1, keepdims=True)), 0.0)
    pg = eg / jnp.sum(eg, axis=1, keepdims=True)
    p_g = jnp.max(pg, axis=1, keepdims=True)
    gidx = jnp.min(jnp.where(is_g & (pg == p_g), lane, big), axis=1, keepdims=True)
    lo = GATE_LANE0 + EXPERTS_PER_GROUP * gidx
    sel = (lane >= lo) & (lane < lo + EXPERTS_PER_GROUP)
    le = jnp.where(sel, logits, NEG_INF)
    ee = jnp.where(sel, jnp.exp(le - jnp.max(le, axis=1, keepdims=True)), 0.0)
    pe = ee / jnp.sum(ee, axis=1, keepdims=True)
    v1 = jnp.max(jnp.where(sel, pe, -1.0), axis=1, keepdims=True)
    i1 = jnp.min(jnp.where(sel & (pe == v1), lane, big), axis=1, keepdims=True)
    rest = sel & (lane != i1)
    v2 = jnp.max(jnp.where(rest, pe, -1.0), axis=1, keepdims=True)
    i2 = jnp.min(jnp.where(rest & (pe == v2), lane, big), axis=1, keepdims=True)
    tot = v1 + v2
    w1 = v1 / tot * p_g
    w2 = v2 / tot * p_g
    return jnp.where(lane == i1, w1, jnp.where(lane == i2, w2, 0.0))


def _moe_ln_kernel(x_ref, wrh_ref, wrl_ref, br_ref, w13_ref, w2_ref, g_ref, b_ref, y_ref,
                   xb_sc, gate_sc, acc_sc, *, alpha):
    e = pl.program_id(1)

    @pl.when(e == 0)
    def _():
        x = x_ref[...]
        xh = x.astype(BF16)
        xl = (x - xh.astype(F32)).astype(BF16)
        xb_sc[...] = xh
        logits = (jnp.dot(xh, wrh_ref[...], preferred_element_type=F32)
                  + jnp.dot(xl, wrh_ref[...], preferred_element_type=F32)
                  + jnp.dot(xh, wrl_ref[...], preferred_element_type=F32) + br_ref[...])
        gate_sc[...] = _route(logits)
        acc_sc[...] = jnp.zeros_like(acc_sc)

    h = jnp.dot(xb_sc[...], w13_ref[0], preferred_element_type=F32)
    h1 = h[:, :D_EXPERT]
    h3 = h[:, D_EXPERT:]
    hdn = (h1 * jax.nn.sigmoid(h1)) * h3
    y = jnp.dot(hdn.astype(BF16), w2_ref[0], preferred_element_type=F32)
    lane = lax.broadcasted_iota(jnp.int32, (1, LANES), 1)
    ge = jnp.sum(jnp.where(lane == e + GATE_LANE0, gate_sc[...], 0.0), axis=1, keepdims=True)
    acc_sc[...] += ge * y

    @pl.when(e == pl.num_programs(1) - 1)
    def _():
        y_ref[...] = _layer_norm(alpha * x_ref[...] + acc_sc[...], g_ref[...], b_ref[...])


def _moe_ln(x, wrh, wrl, br, w13, w2, g, b, alpha):
    n, d = x.shape
    tm = min(MOE_TILE, n)
    assert n % tm == 0
    ne = w13.shape[0]
    full = lambda a: pl.BlockSpec(a.shape, lambda i, e: (0, 0))
    return pl.pallas_call(
        functools.partial(_moe_ln_kernel, alpha=alpha),
        grid=(n // tm, ne),
        in_specs=[pl.BlockSpec((tm, d), lambda i, e: (i, 0)), full(wrh), full(wrl), full(br),
                  pl.BlockSpec((1,) + w13.shape[1:], lambda i, e: (e, 0, 0)),
                  pl.BlockSpec((1,) + w2.shape[1:], lambda i, e: (e, 0, 0)),
                  full(g), full(b)],
        out_specs=pl.BlockSpec((tm, d), lambda i, e: (i, 0)),
        out_shape=jax.ShapeDtypeStruct((n, d), F32),
        scratch_shapes=[pltpu.VMEM((tm, d), BF16), pltpu.VMEM((tm, LANES), F32), pltpu.VMEM((tm, d), F32)],
        compiler_params=_cparams(("parallel", "arbitrary")),
        name="moe_ln",
    )(x, wrh, wrl, br, w13, w2, g, b)


def _proj_c_kernel(x_ref, win_ref, gq_ref, gkv_ref, wuq_ref, cq_ref, s1q_ref, s2q_ref,
                   ck_ref, s1k_ref, s2k_ref, q_ref, ckv_ref, kr_ref):
    xb = x_ref[0].astype(BF16)
    h = jnp.dot(xb, win_ref[...], preferred_element_type=F32)
    qa = h[:, :Q_RANK]
    kva = h[:, Q_RANK:Q_RANK + KV_RANK]
    krw = h[:, Q_RANK + KV_RANK:]
    qn = qa * lax.rsqrt(jnp.mean(qa * qa, axis=1, keepdims=True) + RMS_EPS) * gq_ref[...]
    ckv_ref[0] = kva * lax.rsqrt(jnp.mean(kva * kva, axis=1, keepdims=True) + RMS_EPS) * gkv_ref[...]
    half = ROPE_DIM // 2
    kr = _rope3(krw, ck_ref[...], s1k_ref[...], s2k_ref[...], LANES - half, half)
    kr_ref[0] = kr[:, :ROPE_DIM]
    q = jnp.dot(qn.astype(BF16), wuq_ref[...], preferred_element_type=F32)
    cq, s1q, s2q = cq_ref[...], s1q_ref[...], s2q_ref[...]
    scale = (NOPE_DIM + ROPE_DIM) ** -0.5 * LOG2E
    for hd in range(H_C):
        sl = slice(hd * LANES, (hd + 1) * LANES)
        q_ref[0, :, sl] = (_rope3(q[:, sl], cq, s1q, s2q, LANES - half, half) * scale).astype(BF16)


def _proj_c(x, win, gq, gkv, wuq, tabs_q, tabs_k):
    nb, t, _ = x.shape
    tm = min(ROW_TILE, t)
    assert t % tm == 0
    tok = lambda width: pl.BlockSpec((1, tm, width), lambda b, i: (b, i, 0))
    tab = pl.BlockSpec((tm, LANES), lambda b, i: (i, 0))
    full = lambda a: pl.BlockSpec(a.shape, lambda b, i: (0, 0))
    return pl.pallas_call(
        _proj_c_kernel,
        grid=(nb, t // tm),
        in_specs=[tok(D_MODEL), full(win), full(gq), full(gkv), full(wuq)] + [tab] * 6,
        out_specs=[tok(H_C * LANES), tok(KV_RANK), tok(ROPE_DIM)],
        out_shape=[jax.ShapeDtypeStruct((nb, t, H_C * LANES), BF16),
                   jax.ShapeDtypeStruct((nb, t, KV_RANK), F32),
                   jax.ShapeDtypeStruct((nb, t, ROPE_DIM), F32)],
        compiler_params=_cparams(("parallel", "parallel")),
        name="proj_c",
    )(x, win, gq, gkv, wuq, *tabs_q, *tabs_k)


def _kv_up_kernel(ckv_ref, kr_ref, wk_ref, place_ref, wvt_ref, k_ref, vt_ref):
    cb = ckv_ref[...].astype(BF16)
    k = (jnp.dot(cb, wk_ref[...], preferred_element_type=F32)
         + jnp.dot(kr_ref[...].astype(BF16), place_ref[...], preferred_element_type=F32))
    k_ref[0] = k.astype(BF16)
    vt = lax.dot_general(wvt_ref[...], cb, (((1,), (1,)), ((), ())), preferred_element_type=F32)
    vt_ref[0] = vt.astype(BF16)


def _kv_up(ckv, kr, wk, place, wvt):
    n = ckv.shape[0]
    tm = ATTN_BLOCK
    assert n % tm == 0
    row = lambda width: pl.BlockSpec((tm, width), lambda i: (i, 0))
    full = lambda a: pl.BlockSpec(a.shape, lambda i: (0, 0))
    return pl.pallas_call(
        _kv_up_kernel,
        grid=(n // tm,),
        in_specs=[row(KV_RANK), row(ROPE_DIM), full(wk), full(place), full(wvt)],
        out_specs=[pl.BlockSpec((1, tm, H_C * LANES), lambda i: (i, 0, 0)),
                   pl.BlockSpec((1, H_C * V_DIM_C, tm), lambda i: (i, 0, 0))],
        out_shape=[jax.ShapeDtypeStruct((n // tm, tm, H_C * LANES), BF16),
                   jax.ShapeDtypeStruct((n // tm, H_C * V_DIM_C, tm), BF16)],
        compiler_params=_cparams(("parallel",)),
        name="kv_up",
    )(ckv, kr, wk, place, wvt)


def _rope_tables(pos, dim, lane0):
    half = dim // 2
    inv = ROPE_THETA ** (-jnp.arange(0, dim, 2, dtype=F32) / dim)
    ang = pos.astype(F32)[:, None] * inv[None, :]
    cos, sin = jnp.cos(ang), jnp.sin(ang)
    zero = jnp.zeros_like(sin)
    c = jnp.concatenate([cos, cos], axis=1)
    s1 = jnp.concatenate([-sin, zero], axis=1)
    s2 = jnp.concatenate([zero, sin], axis=1)
    if lane0 < 0:
        reps = LANES // dim
        return tuple(jnp.tile(a, (1, reps)) for a in (c, s1, s2))
    t = pos.shape[0]
    pad = lambda a, fill: jnp.concatenate(
        [jnp.full((t, lane0), fill, F32), a, jnp.full((t, LANES - lane0 - dim), fill, F32)], axis=1)
    return pad(c, 1.0), pad(s1, 0.0), pad(s2, 0.0)


def _pad_cols(a, width):
    return jnp.pad(a, ((0, 0), (0, width - a.shape[1])))


def _blocks(a, tk):
    nb, t, l = a.shape
    return a.reshape(nb, t // tk, tk, l)


def _vt_blocks(v, tk):
    nb, t, l = v.shape
    return v.reshape(nb, t // tk, tk, l).transpose(0, 1, 3, 2)


def _pad_rows(q, rows):
    return jnp.pad(q, ((0, 0), (0, rows - q.shape[1]), (0, 0)))


def _cat_pad_time(cache, new, t_pad):
    nb, t0, l = cache.shape
    t1 = new.shape[1]
    return jnp.concatenate([cache, new, jnp.zeros((nb, t_pad - t0 - t1, l), cache.dtype)], axis=1)


def kernel(x_prompt, x_sample, cache_fox_k, cache_fox_v, cache_fox_logf, cache_diff_k, cache_diff_v, cache_mla_ckv, cache_mla_krope, w_in_ab, b_fgate, diff_lq1, diff_lk1, diff_lq2, diff_lk2, diff_subln, w_out_ab, w_in_c, mla_q_norm, mla_kv_norm, mla_w_uq, mla_w_ukv, w_out_c, ln1_g, ln1_b, ln2_g, ln2_b, moe_wg, moe_bg, moe_we, moe_be, moe_w1, moe_w3, moe_w2):
    bp, tp, d = x_prompt.shape
    bs, ts, _ = x_sample.shape
    past = cache_fox_k.shape[2]
    depth = ln1_g.shape[0]
    alpha = (2 * depth) ** 0.25
    tk = ATTN_BLOCK
    assert past % tk == 0 and ts == 16 and past % CHUNK == 0
    ns = bs * ts
    t_dec = past + tk
    nfull_dec = past // tk
    dec_shift = 4

    pos_p = jnp.arange(tp)
    pos_s = jnp.tile(past + jnp.arange(ts), bs)

    xp = x_prompt
    xs = x_sample.reshape(1, ns, d)
    out_ab_p, out_ab_s, out_c_p, out_c_s = [], [], [], []

    for i in range(depth):
        j = i // 2
        if i % 2 == 0:
            lam_init = 0.8 - 0.6 * math.exp(-0.3 * i)
            cuts = [0, A_WIDTH, 2 * A_WIDTH, 3 * A_WIDTH, 3 * A_WIDTH + H_A,
                    3 * A_WIDTH + H_A + B_QK_WIDTH, 3 * A_WIDTH + H_A + 2 * B_QK_WIDTH,
                    3 * A_WIDTH + H_A + 2 * B_QK_WIDTH + B_V_WIDTH]
            w = w_in_ab[j]
            piece = lambda a: w[:, cuts[a]:cuts[a + 1]]
            w6 = jnp.stack([piece(0), piece(1), piece(2), piece(4), piece(5), piece(6)]).astype(BF16)
            wvt = jnp.stack([piece(2).T, piece(6).T]).astype(BF16)
            wf = _pad_cols(piece(3), LANES).astype(BF16)
            bf = _pad_cols(b_fgate[j][None, :], LANES)
            wout = w_out_ab[j].astype(BF16)
            diff_extra = (diff_lq1[j][None, :], diff_lk1[j][None, :], diff_lq2[j][None, :],
                          diff_lk2[j][None, :], diff_subln[j][None, :])

            tabs = _rope_tables(pos_p, HEAD_DIM, -1)
            (qa, ka, kab, va, vat, lf, lfw, qb, kb, kbb, vb, vbt) = _proj_ab(xp, w6, wvt, wf, bf, tabs)
            bias = _blocks(_decay_bias(lfw), tk)
            oa = _attention("fox", qa, _blocks(kab, tk), vat, (bias,), n_pairs=H_A // 2, mask_shift=0)
            ob = _attention("diff", qb, _blocks(kbb, tk), vbt, diff_extra,
                            n_pairs=H_B, mask_shift=int(math.log2(CHUNK)), lam_init=lam_init)
            out_ab_p.append((ka.reshape(bp, tp, H_A, HEAD_DIM), va.reshape(bp, tp, H_A, HEAD_DIM), lf,
                             kb.reshape(bp, tp, H_B, 2, HEAD_DIM), vb.reshape(bp, tp, H_B, 2 * HEAD_DIM)))
            xp2 = _outproj_ln(xp.reshape(bp * tp, d), [oa.reshape(bp * tp, -1), ob.reshape(bp * tp, -1)],
                              [wout[:A_WIDTH], wout[A_WIDTH:]], ln1_g[i][None, :], ln1_b[i][None, :], alpha)

            tabs = _rope_tables(pos_s, HEAD_DIM, -1)
            (qa, ka, kab, va, _, lf, lfw, qb, kb, kbb, vb, _) = _proj_ab(xs, w6, wvt, wf, bf, tabs)
            rs = lambda a: a.reshape(bs, ts, a.shape[-1])
            cache_lfw = jnp.pad(cache_fox_logf[j].astype(F32), ((0, 0), (0, 0), (0, LANES - H_A)))
            bias = _blocks(_decay_bias(_cat_pad_time(cache_lfw, rs(lfw), t_dec)), tk)
            flat = lambda c: c.reshape(bs, past, -1).astype(BF16)
            qdec = lambda a: _pad_rows(rs(a), DEC_Q_ROWS)
            k_all = _blocks(_cat_pad_time(flat(cache_fox_k[j]), rs(kab), t_dec), tk)
            v_all = _vt_blocks(_cat_pad_time(flat(cache_fox_v[j]), rs(va).astype(BF16), t_dec), tk)
            oa = _attention("fox", qdec(qa), k_all, v_all, (bias,), n_pairs=H_A // 2, mask_shift=0,
                            nfull_static=nfull_dec)[:, :ts]
            k_all = _blocks(_cat_pad_time(flat(cache_diff_k[j]), rs(kbb), t_dec), tk)
            v_all = _vt_blocks(_cat_pad_time(flat(cache_diff_v[j]), rs(vb).astype(BF16), t_dec), tk)
            ob = _attention("diff", qdec(qb), k_all, v_all, diff_extra, n_pairs=H_B, mask_shift=dec_shift,
                            nfull_static=nfull_dec, lam_init=lam_init)[:, :ts]
            out_ab_s.append((ka.reshape(bs, ts, H_A, HEAD_DIM), va.reshape(bs, ts, H_A, HEAD_DIM),
                             lf.reshape(bs, ts, H_A), kb.reshape(bs, ts, H_B, 2, HEAD_DIM),
                             vb.reshape(bs, ts, H_B, 2 * HEAD_DIM)))
            xs2 = _outproj_ln(xs.reshape(ns, d), [oa.reshape(ns, -1), ob.reshape(ns, -1)],
                              [wout[:A_WIDTH], wout[A_WIDTH:]], ln1_g[i][None, :], ln1_b[i][None, :], alpha)
        else:
            wc = w_in_c[j]
            kr_cols = _pad_cols(wc[:, Q_RANK + KV_RANK:], LANES)
            win = jnp.concatenate([wc[:, :Q_RANK + KV_RANK], kr_cols], axis=1).astype(BF16)
            wuq = jnp.pad(mla_w_uq[j].reshape(Q_RANK, H_C, NOPE_DIM + ROPE_DIM),
                          ((0, 0), (0, 0), (0, LANES - NOPE_DIM - ROPE_DIM))).reshape(Q_RANK, H_C * LANES)
            wuq = wuq.astype(BF16)
            wukv = mla_w_ukv[j].reshape(KV_RANK, H_C, NOPE_DIM + V_DIM_C)
            wk = jnp.pad(wukv[:, :, :NOPE_DIM], ((0, 0), (0, 0), (0, LANES - NOPE_DIM)))
            wk = wk.reshape(KV_RANK, H_C * LANES).astype(BF16)
            wvt = wukv[:, :, NOPE_DIM:].reshape(KV_RANK, H_C * V_DIM_C).T.astype(BF16)
            place = jnp.tile(_pad_cols(jnp.concatenate(
                [jnp.zeros((ROPE_DIM, NOPE_DIM), F32), jnp.eye(ROPE_DIM, dtype=F32)], axis=1), LANES),
                (1, H_C)).astype(BF16)
            gq = mla_q_norm[j][None, :]
            gkv = mla_kv_norm[j][None, :]
            wout = w_out_c[j].astype(BF16)

            q, ckv, kr = _proj_c(xp, win, gq, gkv, wuq, _rope_tables(pos_p, ROPE_DIM, NOPE_DIM),
                                 _rope_tables(pos_p, ROPE_DIM, 0))
            kc, vct = _kv_up(ckv.reshape(bp * tp, KV_RANK), kr.reshape(bp * tp, ROPE_DIM), wk, place, wvt)
            per_seq = lambda a, nb: a.reshape((nb, a.shape[0] // nb) + a.shape[1:])
            oc = _attention("mla", q, per_seq(kc, bp), per_seq(vct, bp), (), n_pairs=H_C // 2,
                            mask_shift=int(math.log2(CHUNK)))
            out_c_p.append((ckv, kr))
            xp2 = _outproj_ln(xp.reshape(bp * tp, d), [oc.reshape(bp * tp, -1)], [wout],
                              ln1_g[i][None, :], ln1_b[i][None, :], alpha)

            q, ckv, kr = _proj_c(xs, win, gq, gkv, wuq, _rope_tables(pos_s, ROPE_DIM, NOPE_DIM),
                                 _rope_tables(pos_s, ROPE_DIM, 0))
            ckv_all = _cat_pad_time(cache_mla_ckv[j].astype(F32), ckv.reshape(bs, ts, KV_RANK), t_dec)
            kr_all = _cat_pad_time(cache_mla_krope[j].astype(F32), kr.reshape(bs, ts, ROPE_DIM), t_dec)
            kc, vct = _kv_up(ckv_all.reshape(bs * t_dec, KV_RANK), kr_all.reshape(bs * t_dec, ROPE_DIM),
                             wk, place, wvt)
            oc = _attention("mla", _pad_rows(q.reshape(bs, ts, -1), DEC_Q_ROWS), per_seq(kc, bs),
                            per_seq(vct, bs), (), n_pairs=H_C // 2, mask_shift=dec_shift,
                            nfull_static=nfull_dec)[:, :ts]
            out_c_s.append((ckv.reshape(bs, ts, KV_RANK), kr.reshape(bs, ts, ROPE_DIM)))
            xs2 = _outproj_ln(xs.reshape(ns, d), [oc.reshape(ns, -1)], [wout],
                              ln1_g[i][None, :], ln1_b[i][None, :], alpha)

        wr = _pad_cols(jnp.concatenate(
            [moe_wg[i]] + [moe_we[i][gi] for gi in range(N_GROUPS)], axis=1), LANES)
        wrh = wr.astype(BF16)
        wrl = (wr - wrh.astype(F32)).astype(BF16)
        br = _pad_cols(jnp.concatenate([moe_bg[i], moe_be[i].reshape(-1)])[None, :], LANES)
        w13 = jnp.concatenate([moe_w1[i], moe_w3[i]], axis=2).astype(BF16)
        w2 = moe_w2[i].astype(BF16)
        g2, b2 = ln2_g[i][None, :], ln2_b[i][None, :]
        xp = _moe_ln(xp2, wrh, wrl, br, w13, w2, g2, b2, alpha).reshape(bp, tp, d)
        xs = _moe_ln(xs2, wrh, wrl, br, w13, w2, g2, b2, alpha).reshape(1, ns, d)

    stack = lambda rows, n: jnp.stack([r[n] for r in rows])
    return (xp, xs.reshape(bs, ts, d),
            stack(out_ab_p, 0), stack(out_ab_p, 1), stack(out_ab_p, 2), stack(out_ab_p, 3), stack(out_ab_p, 4),
            stack(out_c_p, 0), stack(out_c_p, 1),
            stack(out_ab_s, 0), stack(out_ab_s, 1), stack(out_ab_s, 2), stack(out_ab_s, 3), stack(out_ab_s, 4),
            stack(out_c_s, 0), stack(out_c_s, 1))
```

```python
import functools
import math

import jax
import jax.numpy as jnp
from jax import lax
from jax.experimental import pallas as pl
from jax.experimental.pallas import tpu as pltpu

F32 = jnp.float32
BF16 = jnp.bfloat16

D_MODEL = 1024
CHUNK = 64
HEAD_DIM = 64
ROPE_THETA = 10000.0
H_A = 8
H_B = 4
H_C = 16
Q_RANK = 256
KV_RANK = 128
NOPE_DIM = 64
ROPE_DIM = 32
V_DIM_C = 64
N_GROUPS = 4
EXPERTS_PER_GROUP = 4
N_EXPERTS = N_GROUPS * EXPERTS_PER_GROUP
D_EXPERT = 256
A_WIDTH = H_A * HEAD_DIM
B_QK_WIDTH = H_B * 2 * HEAD_DIM
B_V_WIDTH = H_B * 2 * HEAD_DIM
FGATE_BIAS = 3.0
LN_EPS = 1e-5
RMS_EPS = 1e-6
NEG_INF = -1e30
LOG2E = math.log2(math.e)

LANES = 128
BF16_ROWS = 16
BIAS_PIECES = 3
DEC_Q_ROWS = 128
VMEM_LIMIT = 48 * 1024 * 1024
ATTN_BLOCK = 512
ROW_TILE = 512
MOE_TILE = 1024
GATE_LANE0 = N_GROUPS


def _cparams(sem):
    return pltpu.CompilerParams(dimension_semantics=sem, vmem_limit_bytes=VMEM_LIMIT)


def _rope3(x, c, s1, s2, shift_up, shift_down):
    return x * c + pltpu.roll(x, shift_up, 1) * s1 + pltpu.roll(x, shift_down, 1) * s2


def _layer_norm(y, g, b):
    mu = jnp.mean(y, axis=-1, keepdims=True)
    d = y - mu
    var = jnp.mean(d * d, axis=-1, keepdims=True)
    return d * lax.rsqrt(var + LN_EPS) * g + b


def _split3(x):
    hi = x.astype(BF16)
    r1 = x - hi.astype(F32)
    mid = r1.astype(BF16)
    return hi, mid, (r1 - mid.astype(F32)).astype(BF16)


def _proj_ab_kernel(x_ref, w_ref, wvt_ref, wf_ref, bf_ref, c_ref, s1_ref, s2_ref,
                    qa_ref, ka_ref, kab_ref, va_ref, vat_ref, lf_ref, lfw_ref,
                    qb_ref, kb_ref, kbb_ref, vb_ref, vbt_ref):
    xb = x_ref[0].astype(BF16)

    def mm(i):
        return jnp.dot(xb, w_ref[i], preferred_element_type=F32)

    def mm_t(i):
        return lax.dot_general(wvt_ref[i], xb, (((1,), (1,)), ((), ())), preferred_element_type=F32)

    qa_ref[0] = (mm(0) * (HEAD_DIM ** -0.5 * LOG2E)).astype(BF16)
    ka = mm(1)
    ka_ref[0] = ka
    kab_ref[0] = ka.astype(BF16)
    va_ref[0] = mm(2)
    vat_ref[0, 0] = mm_t(0).astype(BF16)

    z = jnp.dot(xb, wf_ref[...], preferred_element_type=F32) + bf_ref[...]
    lf = jnp.minimum(z, 0.0) - jnp.log1p(jnp.exp(-jnp.abs(z)))
    lf_ref[0] = lf[:, :H_A]
    lfw_ref[0] = lf

    c, s1, s2 = c_ref[...], s1_ref[...], s2_ref[...]
    qb = mm(3)
    kb = mm(4)
    for s in range(B_QK_WIDTH // LANES):
        sl = slice(s * LANES, (s + 1) * LANES)
        qs = _rope3(qb[:, sl], c, s1, s2, LANES - HEAD_DIM // 2, HEAD_DIM // 2)
        qb_ref[0, :, sl] = (qs * (HEAD_DIM ** -0.5 * LOG2E)).astype(BF16)
        ks = _rope3(kb[:, sl], c, s1, s2, LANES - HEAD_DIM // 2, HEAD_DIM // 2)
        kb_ref[0, :, sl] = ks
        kbb_ref[0, :, sl] = ks.astype(BF16)
    vb_ref[0] = mm(5)
    vbt_ref[0, 0] = mm_t(1).astype(BF16)


def _proj_ab(x, w6, wvt, wf, bf, tabs):
    nb, t, _ = x.shape
    tm = min(ROW_TILE, t)
    assert t % tm == 0
    w = A_WIDTH
    tok = lambda width: pl.BlockSpec((1, tm, width), lambda b, i: (b, i, 0))
    tr = pl.BlockSpec((1, 1, w, tm), lambda b, i: (b, i, 0, 0))
    tab = pl.BlockSpec((tm, LANES), lambda b, i: (i, 0))
    full = lambda a: pl.BlockSpec(a.shape, lambda b, i: (0,) * a.ndim)
    sds = lambda width, dt: jax.ShapeDtypeStruct((nb, t, width), dt)
    sds_t = jax.ShapeDtypeStruct((nb, t // tm, w, tm), BF16)
    return pl.pallas_call(
        _proj_ab_kernel,
        grid=(nb, t // tm),
        in_specs=[tok(D_MODEL), full(w6), full(wvt), full(wf), full(bf), tab, tab, tab],
        out_specs=[tok(w), tok(w), tok(w), tok(w), tr, tok(H_A), tok(LANES), tok(w), tok(w), tok(w), tok(w), tr],
        out_shape=[sds(w, BF16), sds(w, F32), sds(w, BF16), sds(w, F32), sds_t, sds(H_A, F32), sds(LANES, F32),
                   sds(w, BF16), sds(w, F32), sds(w, BF16), sds(w, F32), sds_t],
        compiler_params=_cparams(("parallel", "parallel")),
        name="proj_ab",
    )(x, w6, wvt, wf, bf, *tabs)


def _decay_bias_kernel(lf_ref, o_ref, carry_ref):
    @pl.when(pl.program_id(1) == 0)
    def _():
        carry_ref[...] = jnp.zeros_like(carry_ref)

    x = lf_ref[0]
    tc = x.shape[0]
    src = lax.broadcasted_iota(jnp.int32, (LANES, LANES), 0)
    dst = lax.broadcasted_iota(jnp.int32, (LANES, LANES), 1)
    spread = ((dst >= BIAS_PIECES * src) & (dst < BIAS_PIECES * (src + 1)) & (src < H_A)).astype(BF16)
    xr = sum(jnp.dot(p, spread, preferred_element_type=F32) for p in _split3(x))
    row = lax.broadcasted_iota(jnp.int32, (tc, tc), 0)
    col = lax.broadcasted_iota(jnp.int32, (tc, tc), 1)
    lower = (col <= row).astype(BF16)
    c = sum(jnp.dot(lower, p, preferred_element_type=F32) for p in _split3(xr)) + carry_ref[...]
    carry_ref[...] = c[tc - 1:tc, :]
    hi, mid, lo = (p.astype(F32) for p in _split3(c * (-LOG2E)))
    lane = lax.broadcasted_iota(jnp.int32, (1, LANES), 1).astype(F32)
    piece = lane - BIAS_PIECES * jnp.floor((lane + 0.5) * (1.0 / BIAS_PIECES))
    o_ref[0] = jnp.where(piece == 0.0, hi, jnp.where(piece == 1.0, mid, lo)).astype(BF16)


def _decay_bias(lf_wide):
    nb, t, _ = lf_wide.shape
    tc = min(ATTN_BLOCK, t)
    assert t % tc == 0
    spec = pl.BlockSpec((1, tc, LANES), lambda b, i: (b, i, 0))
    return pl.pallas_call(
        _decay_bias_kernel,
        grid=(nb, t // tc),
        in_specs=[spec],
        out_specs=spec,
        out_shape=jax.ShapeDtypeStruct((nb, t, LANES), BF16),
        scratch_shapes=[pltpu.VMEM((1, LANES), F32)],
        compiler_params=_cparams(("parallel", "arbitrary")),
        name="cumsum",
    )(lf_wide)


def _attn_kernel(*refs, mode, tq, tk, mask_shift, nfull_static, n_diag, lam_init):
    if mode == "diff":
        q_ref, k_ref, vt_ref, lq1_ref, lk1_ref, lq2_ref, lk2_ref, sub_ref, o_ref = refs[:9]
    elif mode == "fox":
        q_ref, k_ref, vt_ref, b_ref, o_ref = refs[:5]
    else:
        q_ref, k_ref, vt_ref, o_ref = refs[:4]
    m_sc, acc_sc, sa, bma, sb, bmb = refs[-6:]
    v_rows = LANES if mode == "diff" else HEAD_DIM
    sa_sc, sb_sc = (sa, bma), (sb, bmb)

    qi = pl.program_id(2)
    q = q_ref[0]
    lane = lax.broadcasted_iota(jnp.int32, (1, LANES), 1)
    if mode == "mla":
        qs = [q[:, :LANES], q[:, LANES:]]
    else:
        zero = jnp.zeros_like(q)
        qs = [jnp.where(lane < HEAD_DIM, q, zero), jnp.where(lane >= HEAD_DIM, q, zero)]
        if mode == "fox":
            def pick(i):
                lo = BIAS_PIECES * (2 * pl.program_id(1) + i)
                hot = jnp.where((lane >= lo) & (lane < lo + BIAS_PIECES), 1.0, 0.0)
                return jnp.broadcast_to(hot, (tq, LANES)).astype(BF16)

            qs = [jnp.concatenate([qs[i], pick(i)], axis=1) for i in range(2)]

    m_sc[...] = jnp.full(m_sc.shape, NEG_INF, F32)
    acc_sc[...] = jnp.zeros(acc_sc.shape, F32)

    def scores(j, bufs):
        s_sc, bm_sc = bufs
        k = k_ref[0, j]
        if mode == "fox":
            k = jnp.concatenate([k, b_ref[0, j]], axis=1)
        for i in range(2):
            ki = k[:, i * LANES:(i + 1) * LANES] if mode == "mla" else k
            st = lax.dot_general(ki, qs[i], (((1,), (1,)), ((), ())), preferred_element_type=F32)
            s_sc[i] = st
            bm_sc[i] = jnp.max(st, axis=0, keepdims=True)

    def consume(j, bufs, diag=None):
        s_sc, bm_sc = bufs
        vt = vt_ref[0, j]
        masked = diag is not None
        for i in range(2):
            st = s_sc[i]
            if masked:
                key = lax.broadcasted_iota(jnp.int32, (tk, tq), 0) + diag * tk
                qry = lax.broadcasted_iota(jnp.int32, (tk, tq), 1)
                vis = lax.shift_right_logical(key, mask_shift) <= lax.shift_right_logical(qry, mask_shift)
                st = jnp.where(vis, st, NEG_INF)
                blk_max = jnp.max(st, axis=0, keepdims=True)
            else:
                blk_max = bm_sc[i]
            m_prev = m_sc[i]
            m_new = jnp.maximum(m_prev, blk_max)
            alpha = jnp.exp2(m_prev - m_new)
            p = jnp.exp2(st - m_new).astype(BF16)
            vi = vt if mode == "diff" else vt[i * HEAD_DIM:(i + 1) * HEAD_DIM]
            vi = jnp.concatenate([vi, jnp.ones((BF16_ROWS, tk), BF16)], axis=0)
            acc_sc[i] = alpha * acc_sc[i] + jnp.dot(vi, p, preferred_element_type=F32)
            m_sc[i] = m_new

    n_pairs_full = qi * (n_diag // 2) if nfull_static is None else nfull_static // 2
    nfull = 2 * n_pairs_full

    def pair(jj, carry):
        scores(2 * jj + 1, sb_sc)
        consume(2 * jj, sa_sc)
        scores(2 * jj + 2, sa_sc)
        consume(2 * jj + 1, sb_sc)
        return carry

    scores(0, sa_sc)
    lax.fori_loop(0, n_pairs_full, pair, 0)
    if n_diag == 2:
        scores(nfull + 1, sb_sc)
    consume(nfull, sa_sc, diag=0)
    if n_diag == 2:
        consume(nfull + 1, sb_sc, diag=1)

    o0 = acc_sc[0, :v_rows] / acc_sc[0, v_rows:v_rows + 1]
    o1 = acc_sc[1, :v_rows] / acc_sc[1, v_rows:v_rows + 1]
    if mode == "diff":
        lam = (jnp.exp(jnp.sum(lq1_ref[...] * lk1_ref[...], axis=1, keepdims=True))
               - jnp.exp(jnp.sum(lq2_ref[...] * lk2_ref[...], axis=1, keepdims=True)) + lam_init)
        o = o0 - lam * o1
        ms = jnp.mean(o * o, axis=0, keepdims=True)
        o = (o * lax.rsqrt(ms + RMS_EPS)).T * sub_ref[...] * (1.0 - lam_init)
    else:
        o = jnp.concatenate([o0, o1], axis=0).T
    o_ref[0] = o.astype(o_ref.dtype)


def _attention(mode, q, k, vt, extra, *, n_pairs, mask_shift, nfull_static=None, lam_init=0.0):
    nb, t_q, _ = q.shape
    _, nkb, tk, _ = k.shape
    if nfull_static is None:
        tq, n_diag = 2 * tk, 2
        assert t_q % tq == 0 and nkb == t_q // tk
    else:
        tq, n_diag = t_q, 1
        assert nfull_static % 2 == 0 and nkb == nfull_static + 1
    qw = 2 * LANES if mode == "mla" else LANES
    in_specs = [
        pl.BlockSpec((1, tq, qw), lambda b, p, i: (b, i, p)),
        pl.BlockSpec((1, nkb, tk, qw), lambda b, p, i: (b, 0, 0, p)),
        pl.BlockSpec((1, nkb, LANES, tk), lambda b, p, i: (b, 0, p, 0)),
    ]
    if mode == "fox":
        in_specs.append(pl.BlockSpec((1, nkb, tk, LANES), lambda b, p, i: (b, 0, 0, 0)))
    elif mode == "diff":
        in_specs += [pl.BlockSpec(a.shape, lambda b, p, i: (0, 0)) for a in extra]
    kern = functools.partial(_attn_kernel, mode=mode, tq=tq, tk=tk, mask_shift=mask_shift,
                             nfull_static=nfull_static, n_diag=n_diag, lam_init=lam_init)
    return pl.pallas_call(
        kern,
        grid=(nb, n_pairs, t_q // tq),
        in_specs=in_specs,
        out_specs=pl.BlockSpec((1, tq, LANES), lambda b, p, i: (b, i, p)),
        out_shape=jax.ShapeDtypeStruct((nb, t_q, n_pairs * LANES), BF16),
        scratch_shapes=[pltpu.VMEM((2, 1, tq), F32),
                        pltpu.VMEM((2, (LANES if mode == "diff" else HEAD_DIM) + BF16_ROWS, tq), F32),
                        pltpu.VMEM((2, tk, tq), F32), pltpu.VMEM((2, 1, tq), F32),
                        pltpu.VMEM((2, tk, tq), F32), pltpu.VMEM((2, 1, tq), F32)],
        compiler_params=_cparams(("parallel", "parallel", "arbitrary")),
        name="attn_" + mode,
    )(q, k, vt, *extra)


def _outproj_ln_kernel(*refs, n_in, alpha):
    x_ref = refs[0]
    o_refs = refs[1:1 + n_in]
    w_refs = refs[1 + n_in:1 + 2 * n_in]
    g_ref, b_ref, y_ref = refs[1 + 2 * n_in:]
    mix = jnp.dot(o_refs[0][...], w_refs[0][...], preferred_element_type=F32)
    for o_r, w_r in zip(o_refs[1:], w_refs[1:]):
        mix = mix + jnp.dot(o_r[...], w_r[...], preferred_element_type=F32)
    y_ref[...] = _layer_norm(alpha * x_ref[...] + mix, g_ref[...], b_ref[...])


def _outproj_ln(x, outs, ws, g, b, alpha):
    n, d = x.shape
    tm = min(ROW_TILE, n)
    assert n % tm == 0
    row = lambda width: pl.BlockSpec((tm, width), lambda i: (i, 0))
    full = lambda a: pl.BlockSpec(a.shape, lambda i: (0, 0))
    return pl.pallas_call(
        functools.partial(_outproj_ln_kernel, n_in=len(outs), alpha=alpha),
        grid=(n // tm,),
        in_specs=[row(d)] + [row(o.shape[1]) for o in outs] + [full(w) for w in ws] + [full(g), full(b)],
        out_specs=row(d),
        out_shape=jax.ShapeDtypeStruct((n, d), F32),
        compiler_params=_cparams(("parallel",)),
        name="outproj_ln",
    )(x, *outs, *ws, g, b)


def _route(logits):
    lane = lax.broadcasted_iota(jnp.int32, logits.shape, 1).astype(F32)
    big = float(1 << 20)
    is_g = lane < N_GROUPS
    lg = jnp.where(is_g, logits, NEG_INF)
    eg = jnp.where(is_g, jnp.exp(lg - jnp.max(lg, axis=1, keepdims=True)), 0.0)
    pg = eg / jnp.sum(eg, axis=1, keepdims=True)
    p_g = jnp.max(pg, axis=1, keepdims=True)
    gidx = jnp.min(jnp.where(is_g & (pg == p_g), lane, big), axis=1, keepdims=True)
    lo = GATE_LANE0 + EXPERTS_PER_GROUP * gidx
    sel = (lane >= lo) & (lane < lo + EXPERTS_PER_GROUP)
    le = jnp.where(sel, logits, NEG_INF)
    ee = jnp.where(sel, jnp.exp(le - jnp.max(le, axis=1, keepdims=True)), 0.0)
    pe = ee / jnp.sum(ee, axis=1, keepdims=True)
    v1 = jnp.max(jnp.where(sel, pe, -1.0), axis=1, keepdims=True)
    i1 = jnp.min(jnp.where(sel & (pe == v1), lane, big), axis=1, keepdims=True)
    rest = sel & (lane != i1)
    v2 = jnp.max(jnp.where(rest, pe, -1.0), axis=1, keepdims=True)
    i2 = jnp.min(jnp.where(rest & (pe == v2), lane, big), axis=1, keepdims=True)
    tot = v1 + v2
    w1 = v1 / tot * p_g
    w2 = v2 / tot * p_g
    return jnp.where(lane == i1, w1, jnp.where(lane == i2, w2, 0.0))


def _moe_ln_kernel(x_ref, wrh_ref, wrl_ref, br_ref, w13_ref, w2_ref, g_ref, b_ref, y_ref,
                   xb_sc, gate_sc, acc_sc, *, alpha):
    e = pl.program_id(1)

    @pl.when(e == 0)
    def _():
        x = x_ref[...]
        xh = x.astype(BF16)
        xl = (x - xh.astype(F32)).astype(BF16)
        xb_sc[...] = xh
        logits = (jnp.dot(xh, wrh_ref[...], preferred_element_type=F32)
                  + jnp.dot(xl, wrh_ref[...], preferred_element_type=F32)
                  + jnp.dot(xh, wrl_ref[...], preferred_element_type=F32) + br_ref[...])
        gate_sc[...] = _route(logits)
        acc_sc[...] = jnp.zeros_like(acc_sc)

    h = jnp.dot(xb_sc[...], w13_ref[0], preferred_element_type=F32)
    h1 = h[:, :D_EXPERT]
    h3 = h[:, D_EXPERT:]
    hdn = (h1 * jax.nn.sigmoid(h1)) * h3
    y = jnp.dot(hdn.astype(BF16), w2_ref[0], preferred_element_type=F32)
    lane = lax.broadcasted_iota(jnp.int32, (1, LANES), 1)
    ge = jnp.sum(jnp.where(lane == e + GATE_LANE0, gate_sc[...], 0.0), axis=1, keepdims=True)
    acc_sc[...] += ge * y

    @pl.when(e == pl.num_programs(1) - 1)
    def _():
        y_ref[...] = _layer_norm(alpha * x_ref[...] + acc_sc[...], g_ref[...], b_ref[...])


def _moe_ln(x, wrh, wrl, br, w13, w2, g, b, alpha):
    n, d = x.shape
    tm = min(MOE_TILE, n)
    assert n % tm == 0
    ne = w13.shape[0]
    full = lambda a: pl.BlockSpec(a.shape, lambda i, e: (0, 0))
    return pl.pallas_call(
        functools.partial(_moe_ln_kernel, alpha=alpha),
        grid=(n // tm, ne),
        in_specs=[pl.BlockSpec((tm, d), lambda i, e: (i, 0)), full(wrh), full(wrl), full(br),
                  pl.BlockSpec((1,) + w13.shape[1:], lambda i, e: (e, 0, 0)),
                  pl.BlockSpec((1,) + w2.shape[1:], lambda i, e: (e, 0, 0)),
                  full(g), full(b)],
        out_specs=pl.BlockSpec((tm, d), lambda i, e: (i, 0)),
        out_shape=jax.ShapeDtypeStruct((n, d), F32),
        scratch_shapes=[pltpu.VMEM((tm, d), BF16), pltpu.VMEM((tm, LANES), F32), pltpu.VMEM((tm, d), F32)],
        compiler_params=_cparams(("parallel", "arbitrary")),
        name="moe_ln",
    )(x, wrh, wrl, br, w13, w2, g, b)


def _proj_c_kernel(x_ref, win_ref, gq_ref, gkv_ref, wuq_ref, cq_ref, s1q_ref, s2q_ref,
                   ck_ref, s1k_ref, s2k_ref, q_ref, ckv_ref, kr_ref):
    xb = x_ref[0].astype(BF16)
    h = jnp.dot(xb, win_ref[...], preferred_element_type=F32)
    qa = h[:, :Q_RANK]
    kva = h[:, Q_RANK:Q_RANK + KV_RANK]
    krw = h[:, Q_RANK + KV_RANK:]
    qn = qa * lax.rsqrt(jnp.mean(qa * qa, axis=1, keepdims=True) + RMS_EPS) * gq_ref[...]
    ckv_ref[0] = kva * lax.rsqrt(jnp.mean(kva * kva, axis=1, keepdims=True) + RMS_EPS) * gkv_ref[...]
    half = ROPE_DIM // 2
    kr = _rope3(krw, ck_ref[...], s1k_ref[...], s2k_ref[...], LANES - half, half)
    kr_ref[0] = kr[:, :ROPE_DIM]
    q = jnp.dot(qn.astype(BF16), wuq_ref[...], preferred_element_type=F32)
    cq, s1q, s2q = cq_ref[...], s1q_ref[...], s2q_ref[...]
    scale = (NOPE_DIM + ROPE_DIM) ** -0.5 * LOG2E
    for hd in range(H_C):
        sl = slice(hd * LANES, (hd + 1) * LANES)
        q_ref[0, :, sl] = (_rope3(q[:, sl], cq, s1q, s2q, LANES - half, half) * scale).astype(BF16)


def _proj_c(x, win, gq, gkv, wuq, tabs_q, tabs_k):
    nb, t, _ = x.shape
    tm = min(ROW_TILE, t)
    assert t % tm == 0
    tok = lambda width: pl.BlockSpec((1, tm, width), lambda b, i: (b, i, 0))
    tab = pl.BlockSpec((tm, LANES), lambda b, i: (i, 0))
    full = lambda a: pl.BlockSpec(a.shape, lambda b, i: (0, 0))
    return pl.pallas_call(
        _proj_c_kernel,
        grid=(nb, t // tm),
        in_specs=[tok(D_MODEL), full(win), full(gq), full(gkv), full(wuq)] + [tab] * 6,
        out_specs=[tok(H_C * LANES), tok(KV_RANK), tok(ROPE_DIM)],
        out_shape=[jax.ShapeDtypeStruct((nb, t, H_C * LANES), BF16),
                   jax.ShapeDtypeStruct((nb, t, KV_RANK), F32),
                   jax.ShapeDtypeStruct((nb, t, ROPE_DIM), F32)],
        compiler_params=_cparams(("parallel", "parallel")),
        name="proj_c",
    )(x, win, gq, gkv, wuq, *tabs_q, *tabs_k)


def _kv_up_kernel(ckv_ref, kr_ref, wk_ref, place_ref, wvt_ref, k_ref, vt_ref):
    cb = ckv_ref[...].astype(BF16)
    k = (jnp.dot(cb, wk_ref[...], preferred_element_type=F32)
         + jnp.dot(kr_ref[...].astype(BF16), place_ref[...], preferred_element_type=F32))
    k_ref[0] = k.astype(BF16)
    vt = lax.dot_general(wvt_ref[...], cb, (((1,), (1,)), ((), ())), preferred_element_type=F32)
    vt_ref[0] = vt.astype(BF16)


def _kv_up(ckv, kr, wk, place, wvt):
    n = ckv.shape[0]
    tm = ATTN_BLOCK
    assert n % tm == 0
    row = lambda width: pl.BlockSpec((tm, width), lambda i: (i, 0))
    full = lambda a: pl.BlockSpec(a.shape, lambda i: (0, 0))
    return pl.pallas_call(
        _kv_up_kernel,
        grid=(n // tm,),
        in_specs=[row(KV_RANK), row(ROPE_DIM), full(wk), full(place), full(wvt)],
        out_specs=[pl.BlockSpec((1, tm, H_C * LANES), lambda i: (i, 0, 0)),
                   pl.BlockSpec((1, H_C * V_DIM_C, tm), lambda i: (i, 0, 0))],
        out_shape=[jax.ShapeDtypeStruct((n // tm, tm, H_C * LANES), BF16),
                   jax.ShapeDtypeStruct((n // tm, H_C * V_DIM_C, tm), BF16)],
        compiler_params=_cparams(("parallel",)),
        name="kv_up",
    )(ckv, kr, wk, place, wvt)


def _rope_tables(pos, dim, lane0):
    half = dim // 2
    inv = ROPE_THETA ** (-jnp.arange(0, dim, 2, dtype=F32) / dim)
    ang = pos.astype(F32)[:, None] * inv[None, :]
    cos, sin = jnp.cos(ang), jnp.sin(ang)
    zero = jnp.zeros_like(sin)
    c = jnp.concatenate([cos, cos], axis=1)
    s1 = jnp.concatenate([-sin, zero], axis=1)
    s2 = jnp.concatenate([zero, sin], axis=1)
    if lane0 < 0:
        reps = LANES // dim
        return tuple(jnp.tile(a, (1, reps)) for a in (c, s1, s2))
    t = pos.shape[0]
    pad = lambda a, fill: jnp.concatenate(
        [jnp.full((t, lane0), fill, F32), a, jnp.full((t, LANES - lane0 - dim), fill, F32)], axis=1)
    return pad(c, 1.0), pad(s1, 0.0), pad(s2, 0.0)


def _pad_cols(a, width):
    return jnp.pad(a, ((0, 0), (0, width - a.shape[1])))


def _blocks(a, tk):
    nb, t, l = a.shape
    return a.reshape(nb, t // tk, tk, l)


def _vt_blocks(v, tk):
    nb, t, l = v.shape
    return v.reshape(nb, t // tk, tk, l).transpose(0, 1, 3, 2)


def _pad_rows(q, rows):
    return jnp.pad(q, ((0, 0), (0, rows - q.shape[1]), (0, 0)))


def _cat_pad_time(cache, new, t_pad):
    nb, t0, l = cache.shape
    t1 = new.shape[1]
    return jnp.concatenate([cache, new, jnp.zeros((nb, t_pad - t0 - t1, l), cache.dtype)], axis=1)


def kernel(x_prompt, x_sample, cache_fox_k, cache_fox_v, cache_fox_logf, cache_diff_k, cache_diff_v, cache_mla_ckv, cache_mla_krope, w_in_ab, b_fgate, diff_lq1, diff_lk1, diff_lq2, diff_lk2, diff_subln, w_out_ab, w_in_c, mla_q_norm, mla_kv_norm, mla_w_uq, mla_w_ukv, w_out_c, ln1_g, ln1_b, ln2_g, ln2_b, moe_wg, moe_bg, moe_we, moe_be, moe_w1, moe_w3, moe_w2):
    bp, tp, d = x_prompt.shape
    bs, ts, _ = x_sample.shape
    past = cache_fox_k.shape[2]
    depth = ln1_g.shape[0]
    alpha = (2 * depth) ** 0.25
    tk = ATTN_BLOCK
    assert past % tk == 0 and ts == 16 and past % CHUNK == 0
    ns = bs * ts
    t_dec = past + tk
    nfull_dec = past // tk
    dec_shift = 4

    pos_p = jnp.arange(tp)
    pos_s = jnp.tile(past + jnp.arange(ts), bs)

    xp = x_prompt
    xs = x_sample.reshape(1, ns, d)
    out_ab_p, out_ab_s, out_c_p, out_c_s = [], [], [], []

    for i in range(depth):
        j = i // 2
        if i % 2 == 0:
            lam_init = 0.8 - 0.6 * math.exp(-0.3 * i)
            cuts = [0, A_WIDTH, 2 * A_WIDTH, 3 * A_WIDTH, 3 * A_WIDTH + H_A,
                    3 * A_WIDTH + H_A + B_QK_WIDTH, 3 * A_WIDTH + H_A + 2 * B_QK_WIDTH,
                    3 * A_WIDTH + H_A + 2 * B_QK_WIDTH + B_V_WIDTH]
            w = w_in_ab[j]
            piece = lambda a: w[:, cuts[a]:cuts[a + 1]]
            w6 = jnp.stack([piece(0), piece(1), piece(2), piece(4), piece(5), piece(6)]).astype(BF16)
            wvt = jnp.stack([piece(2).T, piece(6).T]).astype(BF16)
            wf = _pad_cols(piece(3), LANES).astype(BF16)
            bf = _pad_cols(b_fgate[j][None, :], LANES)
            wout = w_out_ab[j].astype(BF16)
            diff_extra = (diff_lq1[j][None, :], diff_lk1[j][None, :], diff_lq2[j][None, :],
                          diff_lk2[j][None, :], diff_subln[j][None, :])

            tabs = _rope_tables(pos_p, HEAD_DIM, -1)
            (qa, ka, kab, va, vat, lf, lfw, qb, kb, kbb, vb, vbt) = _proj_ab(xp, w6, wvt, wf, bf, tabs)
            bias = _blocks(_decay_bias(lfw), tk)
            oa = _attention("fox", qa, _blocks(kab, tk), vat, (bias,), n_pairs=H_A // 2, mask_shift=0)
            ob = _attention("diff", qb, _blocks(kbb, tk), vbt, diff_extra,
                            n_pairs=H_B, mask_shift=int(math.log2(CHUNK)), lam_init=lam_init)
            out_ab_p.append((ka.reshape(bp, tp, H_A, HEAD_DIM), va.reshape(bp, tp, H_A, HEAD_DIM), lf,
                             kb.reshape(bp, tp, H_B, 2, HEAD_DIM), vb.reshape(bp, tp, H_B, 2 * HEAD_DIM)))
            xp2 = _outproj_ln(xp.reshape(bp * tp, d), [oa.reshape(bp * tp, -1), ob.reshape(bp * tp, -1)],
                              [wout[:A_WIDTH], wout[A_WIDTH:]], ln1_g[i][None, :], ln1_b[i][None, :], alpha)

            tabs = _rope_tables(pos_s, HEAD_DIM, -1)
            (qa, ka, kab, va, _, lf, lfw, qb, kb, kbb, vb, _) = _proj_ab(xs, w6, wvt, wf, bf, tabs)
            rs = lambda a: a.reshape(bs, ts, a.shape[-1])
            cache_lfw = jnp.pad(cache_fox_logf[j].astype(F32), ((0, 0), (0, 0), (0, LANES - H_A)))
            bias = _blocks(_decay_bias(_cat_pad_time(cache_lfw, rs(lfw), t_dec)), tk)
            flat = lambda c: c.reshape(bs, past, -1).astype(BF16)
            qdec = lambda a: _pad_rows(rs(a), DEC_Q_ROWS)
            k_all = _blocks(_cat_pad_time(flat(cache_fox_k[j]), rs(kab), t_dec), tk)
            v_all = _vt_blocks(_cat_pad_time(flat(cache_fox_v[j]), rs(va).astype(BF16), t_dec), tk)
            oa = _attention("fox", qdec(qa), k_all, v_all, (bias,), n_pairs=H_A // 2, mask_shift=0,
                            nfull_static=nfull_dec)[:, :ts]
            k_all = _blocks(_cat_pad_time(flat(cache_diff_k[j]), rs(kbb), t_dec), tk)
            v_all = _vt_blocks(_cat_pad_time(flat(cache_diff_v[j]), rs(vb).astype(BF16), t_dec), tk)
            ob = _attention("diff", qdec(qb), k_all, v_all, diff_extra, n_pairs=H_B, mask_shift=dec_shift,
                            nfull_static=nfull_dec, lam_init=lam_init)[:, :ts]
            out_ab_s.append((ka.reshape(bs, ts, H_A, HEAD_DIM), va.reshape(bs, ts, H_A, HEAD_DIM),
                             lf.reshape(bs, ts, H_A), kb.reshape(bs, ts, H_B, 2, HEAD_DIM),
                             vb.reshape(bs, ts, H_B, 2 * HEAD_DIM)))
            xs2 = _outproj_ln(xs.reshape(ns, d), [oa.reshape(ns, -1), ob.reshape(ns, -1)],
                              [wout[:A_WIDTH], wout[A_WIDTH:]], ln1_g[i][None, :], ln1_b[i][None, :], alpha)
        else:
            wc = w_in_c[j]
            kr_cols = _pad_cols(wc[:, Q_RANK + KV_RANK:], LANES)
            win = jnp.concatenate([wc[:, :Q_RANK + KV_RANK], kr_cols], axis=1).astype(BF16)
            wuq = jnp.pad(mla_w_uq[j].reshape(Q_RANK, H_C, NOPE_DIM + ROPE_DIM),
                          ((0, 0), (0, 0), (0, LANES - NOPE_DIM - ROPE_DIM))).reshape(Q_RANK, H_C * LANES)
            wuq = wuq.astype(BF16)
            wukv = mla_w_ukv[j].reshape(KV_RANK, H_C, NOPE_DIM + V_DIM_C)
            wk = jnp.pad(wukv[:, :, :NOPE_DIM], ((0, 0), (0, 0), (0, LANES - NOPE_DIM)))
            wk = wk.reshape(KV_RANK, H_C * LANES).astype(BF16)
            wvt = wukv[:, :, NOPE_DIM:].reshape(KV_RANK, H_C * V_DIM_C).T.astype(BF16)
            place = jnp.tile(_pad_cols(jnp.concatenate(
                [jnp.zeros((ROPE_DIM, NOPE_DIM), F32), jnp.eye(ROPE_DIM, dtype=F32)], axis=1), LANES),
                (1, H_C)).astype(BF16)
            gq = mla_q_norm[j][None, :]
            gkv = mla_kv_norm[j][None, :]
            wout = w_out_c[j].astype(BF16)

            q, ckv, kr = _proj_c(xp, win, gq, gkv, wuq, _rope_tables(pos_p, ROPE_DIM, NOPE_DIM),
                                 _rope_tables(pos_p, ROPE_DIM, 0))
            kc, vct = _kv_up(ckv.reshape(bp * tp, KV_RANK), kr.reshape(bp * tp, ROPE_DIM), wk, place, wvt)
            per_seq = lambda a, nb: a.reshape((nb, a.shape[0] // nb) + a.shape[1:])
            oc = _attention("mla", q, per_seq(kc, bp), per_seq(vct, bp), (), n_pairs=H_C // 2,
                            mask_shift=int(math.log2(CHUNK)))
            out_c_p.append((ckv, kr))
            xp2 = _outproj_ln(xp.reshape(bp * tp, d), [oc.reshape(bp * tp, -1)], [wout],
                              ln1_g[i][None, :], ln1_b[i][None, :], alpha)

            q, ckv, kr = _proj_c(xs, win, gq, gkv, wuq, _rope_tables(pos_s, ROPE_DIM, NOPE_DIM),
                                 _rope_tables(pos_s, ROPE_DIM, 0))
            ckv_all = _cat_pad_time(cache_mla_ckv[j].astype(F32), ckv.reshape(bs, ts, KV_RANK), t_dec)
            kr_all = _cat_pad_time(cache_mla_krope[j].astype(F32), kr.reshape(bs, ts, ROPE_DIM), t_dec)
            kc, vct = _kv_up(ckv_all.reshape(bs * t_dec, KV_RANK), kr_all.reshape(bs * t_dec, ROPE_DIM),
                             wk, place, wvt)
            oc = _attention("mla", _pad_rows(q.reshape(bs, ts, -1), DEC_Q_ROWS), per_seq(kc, bs),
                            per_seq(vct, bs), (), n_pairs=H_C // 2, mask_shift=dec_shift,
                            nfull_static=nfull_dec)[:, :ts]
            out_c_s.append((ckv.reshape(bs, ts, KV_RANK), kr.reshape(bs, ts, ROPE_DIM)))
            xs2 = _outproj_ln(xs.reshape(ns, d), [oc.reshape(ns, -1)], [wout],
                              ln1_g[i][None, :], ln1_b[i][None, :], alpha)

        wr = _pad_cols(jnp.concatenate(
            [moe_wg[i]] + [moe_we[i][gi] for gi in range(N_GROUPS)], axis=1), LANES)
        wrh = wr.astype(BF16)
        wrl = (wr - wrh.astype(F32)).astype(BF16)
        br = _pad_cols(jnp.concatenate([moe_bg[i], moe_be[i].reshape(-1)])[None, :], LANES)
        w13 = jnp.concatenate([moe_w1[i], moe_w3[i]], axis=2).astype(BF16)
        w2 = moe_w2[i].astype(BF16)
        g2, b2 = ln2_g[i][None, :], ln2_b[i][None, :]
        xp = _moe_ln(xp2, wrh, wrl, br, w13, w2, g2, b2, alpha).reshape(bp, tp, d)
        xs = _moe_ln(xs2, wrh, wrl, br, w13, w2, g2, b2, alpha).reshape(1, ns, d)

    stack = lambda rows, n: jnp.stack([r[n] for r in rows])
    return (xp, xs.reshape(bs, ts, d),
            stack(out_ab_p, 0), stack(out_ab_p, 1), stack(out_ab_p, 2), stack(out_ab_p, 3), stack(out_ab_p, 4),
            stack(out_c_p, 0), stack(out_c_p, 1),
            stack(out_ab_s, 0), stack(out_ab_s, 1), stack(out_ab_s, 2), stack(out_ab_s, 3), stack(out_ab_s, 4),
            stack(out_c_s, 0), stack(out_c_s, 1))
```

```python
import functools
import math

import jax
import jax.numpy as jnp
from jax import lax
from jax.experimental import pallas as pl
from jax.experimental.pallas import tpu as pltpu

F32 = jnp.float32
BF16 = jnp.bfloat16

D_MODEL = 1024
CHUNK = 64
HEAD_DIM = 64
ROPE_THETA = 10000.0
H_A = 8
H_B = 4
H_C = 16
Q_RANK = 256
KV_RANK = 128
NOPE_DIM = 64
ROPE_DIM = 32
V_DIM_C = 64
N_GROUPS = 4
EXPERTS_PER_GROUP = 4
N_EXPERTS = N_GROUPS * EXPERTS_PER_GROUP
D_EXPERT = 256
A_WIDTH = H_A * HEAD_DIM
B_QK_WIDTH = H_B * 2 * HEAD_DIM
B_V_WIDTH = H_B * 2 * HEAD_DIM
FGATE_BIAS = 3.0
LN_EPS = 1e-5
RMS_EPS = 1e-6
NEG_INF = -1e30
LOG2E = math.log2(math.e)

LANES = 128
BF16_ROWS = 16
PANEL = 256
BIAS_PIECES = 3
DEC_Q_ROWS = 128
VMEM_LIMIT = 48 * 1024 * 1024
ATTN_BLOCK = 512
ROW_TILE = 512
MOE_TILE = 1024
GATE_LANE0 = N_GROUPS


def _cparams(sem):
    return pltpu.CompilerParams(dimension_semantics=sem, vmem_limit_bytes=VMEM_LIMIT)


def _rope3(x, c, s1, s2, shift_up, shift_down):
    return x * c + pltpu.roll(x, shift_up, 1) * s1 + pltpu.roll(x, shift_down, 1) * s2


def _layer_norm(y, g, b):
    mu = jnp.mean(y, axis=-1, keepdims=True)
    d = y - mu
    var = jnp.mean(d * d, axis=-1, keepdims=True)
    return d * lax.rsqrt(var + LN_EPS) * g + b


def _split3(x):
    hi = x.astype(BF16)
    r1 = x - hi.astype(F32)
    mid = r1.astype(BF16)
    return hi, mid, (r1 - mid.astype(F32)).astype(BF16)


def _proj_ab_kernel(x_ref, w_ref, wvt_ref, wf_ref, bf_ref, c_ref, s1_ref, s2_ref,
                    qa_ref, ka_ref, kab_ref, va_ref, vat_ref, lf_ref, lfw_ref,
                    qb_ref, kb_ref, kbb_ref, vb_ref, vbt_ref):
    xb = x_ref[0].astype(BF16)

    def mm(i):
        return jnp.dot(xb, w_ref[i], preferred_element_type=F32)

    def mm_t(i):
        return lax.dot_general(wvt_ref[i], xb, (((1,), (1,)), ((), ())), preferred_element_type=F32)

    qa_ref[0] = (mm(0) * (HEAD_DIM ** -0.5 * LOG2E)).astype(BF16)
    ka = mm(1)
    ka_ref[0] = ka
    kab_ref[0] = ka.astype(BF16)
    va_ref[0] = mm(2)
    vat_ref[0, 0] = mm_t(0).astype(BF16)

    z = jnp.dot(xb, wf_ref[...], preferred_element_type=F32) + bf_ref[...]
    lf = jnp.minimum(z, 0.0) - jnp.log1p(jnp.exp(-jnp.abs(z)))
    lf_ref[0] = lf[:, :H_A]
    lfw_ref[0] = lf

    c, s1, s2 = c_ref[...], s1_ref[...], s2_ref[...]
    qb = mm(3)
    kb = mm(4)
    for s in range(B_QK_WIDTH // LANES):
        sl = slice(s * LANES, (s + 1) * LANES)
        qs = _rope3(qb[:, sl], c, s1, s2, LANES - HEAD_DIM // 2, HEAD_DIM // 2)
        qb_ref[0, :, sl] = (qs * (HEAD_DIM ** -0.5 * LOG2E)).astype(BF16)
        ks = _rope3(kb[:, sl], c, s1, s2, LANES - HEAD_DIM // 2, HEAD_DIM // 2)
        kb_ref[0, :, sl] = ks
        kbb_ref[0, :, sl] = ks.astype(BF16)
    vb_ref[0] = mm(5)
    vbt_ref[0, 0] = mm_t(1).astype(BF16)


def _proj_ab(x, w6, wvt, wf, bf, tabs):
    nb, t, _ = x.shape
    tm = min(ROW_TILE, t)
    assert t % tm == 0
    w = A_WIDTH
    tok = lambda width: pl.BlockSpec((1, tm, width), lambda b, i: (b, i, 0))
    tr = pl.BlockSpec((1, 1, w, tm), lambda b, i: (b, i, 0, 0))
    tab = pl.BlockSpec((tm, LANES), lambda b, i: (i, 0))
    full = lambda a: pl.BlockSpec(a.shape, lambda b, i: (0,) * a.ndim)
    sds = lambda width, dt: jax.ShapeDtypeStruct((nb, t, width), dt)
    sds_t = jax.ShapeDtypeStruct((nb, t // tm, w, tm), BF16)
    return pl.pallas_call(
        _proj_ab_kernel,
        grid=(nb, t // tm),
        in_specs=[tok(D_MODEL), full(w6), full(wvt), full(wf), full(bf), tab, tab, tab],
        out_specs=[tok(w), tok(w), tok(w), tok(w), tr, tok(H_A), tok(LANES), tok(w), tok(w), tok(w), tok(w), tr],
        out_shape=[sds(w, BF16), sds(w, F32), sds(w, BF16), sds(w, F32), sds_t, sds(H_A, F32), sds(LANES, F32),
                   sds(w, BF16), sds(w, F32), sds(w, BF16), sds(w, F32), sds_t],
        compiler_params=_cparams(("parallel", "parallel")),
        name="proj_ab",
    )(x, w6, wvt, wf, bf, *tabs)


def _decay_bias_kernel(lf_ref, o_ref, carry_ref):
    @pl.when(pl.program_id(1) == 0)
    def _():
        carry_ref[...] = jnp.zeros_like(carry_ref)

    x = lf_ref[0]
    tc = x.shape[0]
    src = lax.broadcasted_iota(jnp.int32, (LANES, LANES), 0)
    dst = lax.broadcasted_iota(jnp.int32, (LANES, LANES), 1)
    spread = ((dst >= BIAS_PIECES * src) & (dst < BIAS_PIECES * (src + 1)) & (src < H_A)).astype(BF16)
    xr = sum(jnp.dot(p, spread, preferred_element_type=F32) for p in _split3(x))
    row = lax.broadcasted_iota(jnp.int32, (tc, tc), 0)
    col = lax.broadcasted_iota(jnp.int32, (tc, tc), 1)
    lower = (col <= row).astype(BF16)
    c = sum(jnp.dot(lower, p, preferred_element_type=F32) for p in _split3(xr)) + carry_ref[...]
    carry_ref[...] = c[tc - 1:tc, :]
    hi, mid, lo = (p.astype(F32) for p in _split3(c * (-LOG2E)))
    lane = lax.broadcasted_iota(jnp.int32, (1, LANES), 1).astype(F32)
    piece = lane - BIAS_PIECES * jnp.floor((lane + 0.5) * (1.0 / BIAS_PIECES))
    o_ref[0] = jnp.where(piece == 0.0, hi, jnp.where(piece == 1.0, mid, lo)).astype(BF16)


def _decay_bias(lf_wide):
    nb, t, _ = lf_wide.shape
    tc = min(ATTN_BLOCK, t)
    assert t % tc == 0
    spec = pl.BlockSpec((1, tc, LANES), lambda b, i: (b, i, 0))
    return pl.pallas_call(
        _decay_bias_kernel,
        grid=(nb, t // tc),
        in_specs=[spec],
        out_specs=spec,
        out_shape=jax.ShapeDtypeStruct((nb, t, LANES), BF16),
        scratch_shapes=[pltpu.VMEM((1, LANES), F32)],
        compiler_params=_cparams(("parallel", "arbitrary")),
        name="cumsum",
    )(lf_wide)


def _attn_kernel(*refs, mode, tq, tk, mask_shift, nfull_static, n_diag, lam_init):
    if mode == "diff":
        q_ref, k_ref, vt_ref, lq1_ref, lk1_ref, lq2_ref, lk2_ref, sub_ref, o_ref = refs[:9]
    elif mode == "fox":
        q_ref, k_ref, vt_ref, b_ref, o_ref = refs[:5]
    else:
        q_ref, k_ref, vt_ref, o_ref = refs[:4]
    m_sc, acc_sc, sa, bma, sb, bmb = refs[-6:]
    v_rows = LANES if mode == "diff" else HEAD_DIM
    sa_sc, sb_sc = (sa, bma), (sb, bmb)

    qi = pl.program_id(2)
    q = q_ref[0]
    lane = lax.broadcasted_iota(jnp.int32, (1, LANES), 1)
    if mode == "mla":
        qs = [q[:, :LANES], q[:, LANES:]]
    else:
        zero = jnp.zeros_like(q)
        qs = [jnp.where(lane < HEAD_DIM, q, zero), jnp.where(lane >= HEAD_DIM, q, zero)]
        if mode == "fox":
            def pick(i):
                lo = BIAS_PIECES * (2 * pl.program_id(1) + i)
                hot = jnp.where((lane >= lo) & (lane < lo + BIAS_PIECES), 1.0, 0.0)
                return jnp.broadcast_to(hot, (tq, LANES)).astype(BF16)

            qs = [jnp.concatenate([qs[i], pick(i)], axis=1) for i in range(2)]

    m_sc[...] = jnp.full(m_sc.shape, NEG_INF, F32)
    acc_sc[...] = jnp.zeros(acc_sc.shape, F32)

    pw = min(PANEL, tq)

    def scores(j, bufs, c):
        s_sc, bm_sc = bufs
        cs = slice(c * pw, (c + 1) * pw)
        k = k_ref[0, j]
        if mode == "fox":
            k = jnp.concatenate([k, b_ref[0, j]], axis=1)
        for i in range(2):
            ki = k[:, i * LANES:(i + 1) * LANES] if mode == "mla" else k
            st = lax.dot_general(ki, qs[i][cs], (((1,), (1,)), ((), ())), preferred_element_type=F32)
            s_sc[i, :, cs] = st
            bm_sc[i, :, cs] = jnp.max(st, axis=0, keepdims=True)

    def consume(j, bufs, c, diag):
        s_sc, bm_sc = bufs
        cs = slice(c * pw, (c + 1) * pw)
        vt = vt_ref[0, j]
        for i in range(2):
            st = s_sc[i, :, cs]
            if diag is not None:
                key = lax.broadcasted_iota(jnp.int32, (tk, pw), 0) + diag * tk
                qry = lax.broadcasted_iota(jnp.int32, (tk, pw), 1) + c * pw
                vis = lax.shift_right_logical(key, mask_shift) <= lax.shift_right_logical(qry, mask_shift)
                st = jnp.where(vis, st, NEG_INF)
                blk_max = jnp.max(st, axis=0, keepdims=True)
            else:
                blk_max = bm_sc[i, :, cs]
            m_prev = m_sc[i, :, cs]
            m_new = jnp.maximum(m_prev, blk_max)
            alpha = jnp.exp2(m_prev - m_new)
            p = jnp.exp2(st - m_new).astype(BF16)
            vi = vt if mode == "diff" else vt[i * HEAD_DIM:(i + 1) * HEAD_DIM]
            vi = jnp.concatenate([vi, jnp.ones((BF16_ROWS, tk), BF16)], axis=0)
            acc_sc[i, :, cs] = alpha * acc_sc[i, :, cs] + jnp.dot(vi, p, preferred_element_type=F32)
            m_sc[i, :, cs] = m_new

    def stage(nxt, cur, diag=None):
        for c in range(tq // pw):
            if nxt is not None:
                scores(nxt[0], nxt[1], c)
            consume(cur[0], cur[1], c, diag)

    n_pairs_full = qi * (n_diag // 2) if nfull_static is None else nfull_static // 2
    nfull = 2 * n_pairs_full

    def pair(jj, carry):
        stage((2 * jj + 1, sb_sc), (2 * jj, sa_sc))
        stage((2 * jj + 2, sa_sc), (2 * jj + 1, sb_sc))
        return carry

    for c in range(tq // pw):
        scores(0, sa_sc, c)
    lax.fori_loop(0, n_pairs_full, pair, 0)
    if n_diag == 2:
        stage((nfull + 1, sb_sc), (nfull, sa_sc), diag=0)
        stage(None, (nfull + 1, sb_sc), diag=1)
    else:
        stage(None, (nfull, sa_sc), diag=0)

    o0 = acc_sc[0, :v_rows] / acc_sc[0, v_rows:v_rows + 1]
    o1 = acc_sc[1, :v_rows] / acc_sc[1, v_rows:v_rows + 1]
    if mode == "diff":
        lam = (jnp.exp(jnp.sum(lq1_ref[...] * lk1_ref[...], axis=1, keepdims=True))
               - jnp.exp(jnp.sum(lq2_ref[...] * lk2_ref[...], axis=1, keepdims=True)) + lam_init)
        o = o0 - lam * o1
        ms = jnp.mean(o * o, axis=0, keepdims=True)
        o = (o * lax.rsqrt(ms + RMS_EPS)).T * sub_ref[...] * (1.0 - lam_init)
    else:
        o = jnp.concatenate([o0, o1], axis=0).T
    o_ref[0] = o.astype(o_ref.dtype)


def _attention(mode, q, k, vt, extra, *, n_pairs, mask_shift, nfull_static=None, lam_init=0.0):
    nb, t_q, _ = q.shape
    _, nkb, tk, _ = k.shape
    if nfull_static is None:
        tq, n_diag = 2 * tk, 2
        assert t_q % tq == 0 and nkb == t_q // tk
    else:
        tq, n_diag = t_q, 1
        assert nfull_static % 2 == 0 and nkb == nfull_static + 1
    qw = 2 * LANES if mode == "mla" else LANES
    in_specs = [
        pl.BlockSpec((1, tq, qw), lambda b, p, i: (b, i, p)),
        pl.BlockSpec((1, nkb, tk, qw), lambda b, p, i: (b, 0, 0, p)),
        pl.BlockSpec((1, nkb, LANES, tk), lambda b, p, i: (b, 0, p, 0)),
    ]
    if mode == "fox":
        in_specs.append(pl.BlockSpec((1, nkb, tk, LANES), lambda b, p, i: (b, 0, 0, 0)))
    elif mode == "diff":
        in_specs += [pl.BlockSpec(a.shape, lambda b, p, i: (0, 0)) for a in extra]
    kern = functools.partial(_attn_kernel, mode=mode, tq=tq, tk=tk, mask_shift=mask_shift,
                             nfull_static=nfull_static, n_diag=n_diag, lam_init=lam_init)
    return pl.pallas_call(
        kern,
        grid=(nb, n_pairs, t_q // tq),
        in_specs=in_specs,
        out_specs=pl.BlockSpec((1, tq, LANES), lambda b, p, i: (b, i, p)),
        out_shape=jax.ShapeDtypeStruct((nb, t_q, n_pairs * LANES), BF16),
        scratch_shapes=[pltpu.VMEM((2, 1, tq), F32),
                        pltpu.VMEM((2, (LANES if mode == "diff" else HEAD_DIM) + BF16_ROWS, tq), F32),
                        pltpu.VMEM((2, tk, tq), F32), pltpu.VMEM((2, 1, tq), F32),
                        pltpu.VMEM((2, tk, tq), F32), pltpu.VMEM((2, 1, tq), F32)],
        compiler_params=_cparams(("parallel", "parallel", "arbitrary")),
        name="attn_" + mode,
    )(q, k, vt, *extra)


def _outproj_ln_kernel(*refs, n_in, alpha):
    x_ref = refs[0]
    o_refs = refs[1:1 + n_in]
    w_refs = refs[1 + n_in:1 + 2 * n_in]
    g_ref, b_ref, y_ref = refs[1 + 2 * n_in:]
    mix = jnp.dot(o_refs[0][...], w_refs[0][...], preferred_element_type=F32)
    for o_r, w_r in zip(o_refs[1:], w_refs[1:]):
        mix = mix + jnp.dot(o_r[...], w_r[...], preferred_element_type=F32)
    y_ref[...] = _layer_norm(alpha * x_ref[...] + mix, g_ref[...], b_ref[...])


def _outproj_ln(x, outs, ws, g, b, alpha):
    n, d = x.shape
    tm = min(ROW_TILE, n)
    assert n % tm == 0
    row = lambda width: pl.BlockSpec((tm, width), lambda i: (i, 0))
    full = lambda a: pl.BlockSpec(a.shape, lambda i: (0, 0))
    return pl.pallas_call(
        functools.partial(_outproj_ln_kernel, n_in=len(outs), alpha=alpha),
        grid=(n // tm,),
        in_specs=[row(d)] + [row(o.shape[1]) for o in outs] + [full(w) for w in ws] + [full(g), full(b)],
        out_specs=row(d),
        out_shape=jax.ShapeDtypeStruct((n, d), F32),
        compiler_params=_cparams(("parallel",)),
        name="outproj_ln",
    )(x, *outs, *ws, g, b)


def _route(logits):
    lane = lax.broadcasted_iota(jnp.int32, logits.shape, 1).astype(F32)
    big = float(1 << 20)
    is_g = lane < N_GROUPS
    lg = jnp.where(is_g, logits, NEG_INF)
    eg = jnp.where(is_g, jnp.exp(lg - jnp.max(lg, axis=1, keepdims=True)), 0.0)
    pg = eg / jnp.sum(eg, axis=1, keepdims=True)
    p_g = jnp.max(pg, axis=1, keepdims=True)
    gidx = jnp.min(jnp.where(is_g & (pg == p_g), lane, big), axis=1, keepdims=True)
    lo = GATE_LANE0 + EXPERTS_PER_GROUP * gidx
    sel = (lane >= lo) & (lane < lo + EXPERTS_PER_GROUP)
    le = jnp.where(sel, logits, NEG_INF)
    ee = jnp.where(sel, jnp.exp(le - jnp.max(le, axis=1, keepdims=True)), 0.0)
    pe = ee / jnp.sum(ee, axis=1, keepdims=True)
    v1 = jnp.max(jnp.where(sel, pe, -1.0), axis=1, keepdims=True)
    i1 = jnp.min(jnp.where(sel & (pe == v1), lane, big), axis=1, keepdims=True)
    rest = sel & (lane != i1)
    v2 = jnp.max(jnp.where(rest, pe, -1.0), axis=1, keepdims=True)
    i2 = jnp.min(jnp.where(rest & (pe == v2), lane, big), axis=1, keepdims=True)
    tot = v1 + v2
    w1 = v1 / tot * p_g
    w2 = v2 / tot * p_g
    return jnp.where(lane == i1, w1, jnp.where(lane == i2, w2, 0.0))


def _moe_ln_kernel(x_ref, wrh_ref, wrl_ref, br_ref, w13_ref, w2_ref, g_ref, b_ref, y_ref,
                   xb_sc, gate_sc, acc_sc, *, alpha):
    e = pl.program_id(1)

    @pl.when(e == 0)
    def _():
        x = x_ref[...]
        xh = x.astype(BF16)
        xl = (x - xh.astype(F32)).astype(BF16)
        xb_sc[...] = xh
        logits = (jnp.dot(xh, wrh_ref[...], preferred_element_type=F32)
                  + jnp.dot(xl, wrh_ref[...], preferred_element_type=F32)
                  + jnp.dot(xh, wrl_ref[...], preferred_element_type=F32) + br_ref[...])
        gate_sc[...] = _route(logits)
        acc_sc[...] = jnp.zeros_like(acc_sc)

    h = jnp.dot(xb_sc[...], w13_ref[0], preferred_element_type=F32)
    h1 = h[:, :D_EXPERT]
    h3 = h[:, D_EXPERT:]
    hdn = (h1 * jax.nn.sigmoid(h1)) * h3
    y = jnp.dot(hdn.astype(BF16), w2_ref[0], preferred_element_type=F32)
    lane = lax.broadcasted_iota(jnp.int32, (1, LANES), 1)
    ge = jnp.sum(jnp.where(lane == e + GATE_LANE0, gate_sc[...], 0.0), axis=1, keepdims=True)
    acc_sc[...] += ge * y

    @pl.when(e == pl.num_programs(1) - 1)
    def _():
        y_ref[...] = _layer_norm(alpha * x_ref[...] + acc_sc[...], g_ref[...], b_ref[...])


def _moe_ln(x, wrh, wrl, br, w13, w2, g, b, alpha):
    n, d = x.shape
    tm = min(MOE_TILE, n)
    assert n % tm == 0
    ne = w13.shape[0]
    full = lambda a: pl.BlockSpec(a.shape, lambda i, e: (0, 0))
    return pl.pallas_call(
        functools.partial(_moe_ln_kernel, alpha=alpha),
        grid=(n // tm, ne),
        in_specs=[pl.BlockSpec((tm, d), lambda i, e: (i, 0)), full(wrh), full(wrl), full(br),
                  pl.BlockSpec((1,) + w13.shape[1:], lambda i, e: (e, 0, 0)),
                  pl.BlockSpec((1,) + w2.shape[1:], lambda i, e: (e, 0, 0)),
                  full(g), full(b)],
        out_specs=pl.BlockSpec((tm, d), lambda i, e: (i, 0)),
        out_shape=jax.ShapeDtypeStruct((n, d), F32),
        scratch_shapes=[pltpu.VMEM((tm, d), BF16), pltpu.VMEM((tm, LANES), F32), pltpu.VMEM((tm, d), F32)],
        compiler_params=_cparams(("parallel", "arbitrary")),
        name="moe_ln",
    )(x, wrh, wrl, br, w13, w2, g, b)


def _proj_c_kernel(x_ref, win_ref, gq_ref, gkv_ref, wuq_ref, cq_ref, s1q_ref, s2q_ref,
                   ck_ref, s1k_ref, s2k_ref, q_ref, ckv_ref, kr_ref):
    xb = x_ref[0].astype(BF16)
    h = jnp.dot(xb, win_ref[...], preferred_element_type=F32)
    qa = h[:, :Q_RANK]
    kva = h[:, Q_RANK:Q_RANK + KV_RANK]
    krw = h[:, Q_RANK + KV_RANK:]
    qn = qa * lax.rsqrt(jnp.mean(qa * qa, axis=1, keepdims=True) + RMS_EPS) * gq_ref[...]
    ckv_ref[0] = kva * lax.rsqrt(jnp.mean(kva * kva, axis=1, keepdims=True) + RMS_EPS) * gkv_ref[...]
    half = ROPE_DIM // 2
    kr = _rope3(krw, ck_ref[...], s1k_ref[...], s2k_ref[...], LANES - half, half)
    kr_ref[0] = kr[:, :ROPE_DIM]
    q = jnp.dot(qn.astype(BF16), wuq_ref[...], preferred_element_type=F32)
    cq, s1q, s2q = cq_ref[...], s1q_ref[...], s2q_ref[...]
    scale = (NOPE_DIM + ROPE_DIM) ** -0.5 * LOG2E
    for hd in range(H_C):
        sl = slice(hd * LANES, (hd + 1) * LANES)
        q_ref[0, :, sl] = (_rope3(q[:, sl], cq, s1q, s2q, LANES - half, half) * scale).astype(BF16)


def _proj_c(x, win, gq, gkv, wuq, tabs_q, tabs_k):
    nb, t, _ = x.shape
    tm = min(ROW_TILE, t)
    assert t % tm == 0
    tok = lambda width: pl.BlockSpec((1, tm, width), lambda b, i: (b, i, 0))
    tab = pl.BlockSpec((tm, LANES), lambda b, i: (i, 0))
    full = lambda a: pl.BlockSpec(a.shape, lambda b, i: (0, 0))
    return pl.pallas_call(
        _proj_c_kernel,
        grid=(nb, t // tm),
        in_specs=[tok(D_MODEL), full(win), full(gq), full(gkv), full(wuq)] + [tab] * 6,
        out_specs=[tok(H_C * LANES), tok(KV_RANK), tok(ROPE_DIM)],
        out_shape=[jax.ShapeDtypeStruct((nb, t, H_C * LANES), BF16),
                   jax.ShapeDtypeStruct((nb, t, KV_RANK), F32),
                   jax.ShapeDtypeStruct((nb, t, ROPE_DIM), F32)],
        compiler_params=_cparams(("parallel", "parallel")),
        name="proj_c",
    )(x, win, gq, gkv, wuq, *tabs_q, *tabs_k)


def _kv_up_kernel(ckv_ref, kr_ref, wk_ref, place_ref, wvt_ref, k_ref, vt_ref):
    cb = ckv_ref[...].astype(BF16)
    k = (jnp.dot(cb, wk_ref[...], preferred_element_type=F32)
         + jnp.dot(kr_ref[...].astype(BF16), place_ref[...], preferred_element_type=F32))
    k_ref[0] = k.astype(BF16)
    vt = lax.dot_general(wvt_ref[...], cb, (((1,), (1,)), ((), ())), preferred_element_type=F32)
    vt_ref[0] = vt.astype(BF16)


def _kv_up(ckv, kr, wk, place, wvt):
    n = ckv.shape[0]
    tm = ATTN_BLOCK
    assert n % tm == 0
    row = lambda width: pl.BlockSpec((tm, width), lambda i: (i, 0))
    full = lambda a: pl.BlockSpec(a.shape, lambda i: (0, 0))
    return pl.pallas_call(
        _kv_up_kernel,
        grid=(n // tm,),
        in_specs=[row(KV_RANK), row(ROPE_DIM), full(wk), full(place), full(wvt)],
        out_specs=[pl.BlockSpec((1, tm, H_C * LANES), lambda i: (i, 0, 0)),
                   pl.BlockSpec((1, H_C * V_DIM_C, tm), lambda i: (i, 0, 0))],
        out_shape=[jax.ShapeDtypeStruct((n // tm, tm, H_C * LANES), BF16),
                   jax.ShapeDtypeStruct((n // tm, H_C * V_DIM_C, tm), BF16)],
        compiler_params=_cparams(("parallel",)),
        name="kv_up",
    )(ckv, kr, wk, place, wvt)


def _rope_tables(pos, dim, lane0):
    half = dim // 2
    inv = ROPE_THETA ** (-jnp.arange(0, dim, 2, dtype=F32) / dim)
    ang = pos.astype(F32)[:, None] * inv[None, :]
    cos, sin = jnp.cos(ang), jnp.sin(ang)
    zero = jnp.zeros_like(sin)
    c = jnp.concatenate([cos, cos], axis=1)
    s1 = jnp.concatenate([-sin, zero], axis=1)
    s2 = jnp.concatenate([zero, sin], axis=1)
    if lane0 < 0:
        reps = LANES // dim
        return tuple(jnp.tile(a, (1, reps)) for a in (c, s1, s2))
    t = pos.shape[0]
    pad = lambda a, fill: jnp.concatenate(
        [jnp.full((t, lane0), fill, F32), a, jnp.full((t, LANES - lane0 - dim), fill, F32)], axis=1)
    return pad(c, 1.0), pad(s1, 0.0), pad(s2, 0.0)


def _pad_cols(a, width):
    return jnp.pad(a, ((0, 0), (0, width - a.shape[1])))


def _blocks(a, tk):
    nb, t, l = a.shape
    return a.reshape(nb, t // tk, tk, l)


def _vt_blocks(v, tk):
    nb, t, l = v.shape
    return v.reshape(nb, t // tk, tk, l).transpose(0, 1, 3, 2)


def _pad_rows(q, rows):
    return jnp.pad(q, ((0, 0), (0, rows - q.shape[1]), (0, 0)))


def _cat_pad_time(cache, new, t_pad):
    nb, t0, l = cache.shape
    t1 = new.shape[1]
    return jnp.concatenate([cache, new, jnp.zeros((nb, t_pad - t0 - t1, l), cache.dtype)], axis=1)


def kernel(x_prompt, x_sample, cache_fox_k, cache_fox_v, cache_fox_logf, cache_diff_k, cache_diff_v, cache_mla_ckv, cache_mla_krope, w_in_ab, b_fgate, diff_lq1, diff_lk1, diff_lq2, diff_lk2, diff_subln, w_out_ab, w_in_c, mla_q_norm, mla_kv_norm, mla_w_uq, mla_w_ukv, w_out_c, ln1_g, ln1_b, ln2_g, ln2_b, moe_wg, moe_bg, moe_we, moe_be, moe_w1, moe_w3, moe_w2):
    bp, tp, d = x_prompt.shape
    bs, ts, _ = x_sample.shape
    past = cache_fox_k.shape[2]
    depth = ln1_g.shape[0]
    alpha = (2 * depth) ** 0.25
    tk = ATTN_BLOCK
    assert past % tk == 0 and ts == 16 and past % CHUNK == 0
    ns = bs * ts
    t_dec = past + tk
    nfull_dec = past // tk
    dec_shift = 4

    pos_p = jnp.arange(tp)
    pos_s = jnp.tile(past + jnp.arange(ts), bs)

    xp = x_prompt
    xs = x_sample.reshape(1, ns, d)
    out_ab_p, out_ab_s, out_c_p, out_c_s = [], [], [], []

    for i in range(depth):
        j = i // 2
        if i % 2 == 0:
            lam_init = 0.8 - 0.6 * math.exp(-0.3 * i)
            cuts = [0, A_WIDTH, 2 * A_WIDTH, 3 * A_WIDTH, 3 * A_WIDTH + H_A,
                    3 * A_WIDTH + H_A + B_QK_WIDTH, 3 * A_WIDTH + H_A + 2 * B_QK_WIDTH,
                    3 * A_WIDTH + H_A + 2 * B_QK_WIDTH + B_V_WIDTH]
            w = w_in_ab[j]
            piece = lambda a: w[:, cuts[a]:cuts[a + 1]]
            w6 = jnp.stack([piece(0), piece(1), piece(2), piece(4), piece(5), piece(6)]).astype(BF16)
            wvt = jnp.stack([piece(2).T, piece(6).T]).astype(BF16)
            wf = _pad_cols(piece(3), LANES).astype(BF16)
            bf = _pad_cols(b_fgate[j][None, :], LANES)
            wout = w_out_ab[j].astype(BF16)
            diff_extra = (diff_lq1[j][None, :], diff_lk1[j][None, :], diff_lq2[j][None, :],
                          diff_lk2[j][None, :], diff_subln[j][None, :])

            tabs = _rope_tables(pos_p, HEAD_DIM, -1)
            (qa, ka, kab, va, vat, lf, lfw, qb, kb, kbb, vb, vbt) = _proj_ab(xp, w6, wvt, wf, bf, tabs)
            bias = _blocks(_decay_bias(lfw), tk)
            oa = _attention("fox", qa, _blocks(kab, tk), vat, (bias,), n_pairs=H_A // 2, mask_shift=0)
            ob = _attention("diff", qb, _blocks(kbb, tk), vbt, diff_extra,
                            n_pairs=H_B, mask_shift=int(math.log2(CHUNK)), lam_init=lam_init)
            out_ab_p.append((ka.reshape(bp, tp, H_A, HEAD_DIM), va.reshape(bp, tp, H_A, HEAD_DIM), lf,
                             kb.reshape(bp, tp, H_B, 2, HEAD_DIM), vb.reshape(bp, tp, H_B, 2 * HEAD_DIM)))
            xp2 = _outproj_ln(xp.reshape(bp * tp, d), [oa.reshape(bp * tp, -1), ob.reshape(bp * tp, -1)],
                              [wout[:A_WIDTH], wout[A_WIDTH:]], ln1_g[i][None, :], ln1_b[i][None, :], alpha)

            tabs = _rope_tables(pos_s, HEAD_DIM, -1)
            (qa, ka, kab, va, _, lf, lfw, qb, kb, kbb, vb, _) = _proj_ab(xs, w6, wvt, wf, bf, tabs)
            rs = lambda a: a.reshape(bs, ts, a.shape[-1])
            cache_lfw = jnp.pad(cache_fox_logf[j].astype(F32), ((0, 0), (0, 0), (0, LANES - H_A)))
            bias = _blocks(_decay_bias(_cat_pad_time(cache_lfw, rs(lfw), t_dec)), tk)
            flat = lambda c: c.reshape(bs, past, -1).astype(BF16)
            qdec = lambda a: _pad_rows(rs(a), DEC_Q_ROWS)
            k_all = _blocks(_cat_pad_time(flat(cache_fox_k[j]), rs(kab), t_dec), tk)
            v_all = _vt_blocks(_cat_pad_time(flat(cache_fox_v[j]), rs(va).astype(BF16), t_dec), tk)
            oa = _attention("fox", qdec(qa), k_all, v_all, (bias,), n_pairs=H_A // 2, mask_shift=0,
                            nfull_static=nfull_dec)[:, :ts]
            k_all = _blocks(_cat_pad_time(flat(cache_diff_k[j]), rs(kbb), t_dec), tk)
            v_all = _vt_blocks(_cat_pad_time(flat(cache_diff_v[j]), rs(vb).astype(BF16), t_dec), tk)
            ob = _attention("diff", qdec(qb), k_all, v_all, diff_extra, n_pairs=H_B, mask_shift=dec_shift,
                            nfull_static=nfull_dec, lam_init=lam_init)[:, :ts]
            out_ab_s.append((ka.reshape(bs, ts, H_A, HEAD_DIM), va.reshape(bs, ts, H_A, HEAD_DIM),
                             lf.reshape(bs, ts, H_A), kb.reshape(bs, ts, H_B, 2, HEAD_DIM),
                             vb.reshape(bs, ts, H_B, 2 * HEAD_DIM)))
            xs2 = _outproj_ln(xs.reshape(ns, d), [oa.reshape(ns, -1), ob.reshape(ns, -1)],
                              [wout[:A_WIDTH], wout[A_WIDTH:]], ln1_g[i][None, :], ln1_b[i][None, :], alpha)
        else:
            wc = w_in_c[j]
            kr_cols = _pad_cols(wc[:, Q_RANK + KV_RANK:], LANES)
            win = jnp.concatenate([wc[:, :Q_RANK + KV_RANK], kr_cols], axis=1).astype(BF16)
            wuq = jnp.pad(mla_w_uq[j].reshape(Q_RANK, H_C, NOPE_DIM + ROPE_DIM),
                          ((0, 0), (0, 0), (0, LANES - NOPE_DIM - ROPE_DIM))).reshape(Q_RANK, H_C * LANES)
            wuq = wuq.astype(BF16)
            wukv = mla_w_ukv[j].reshape(KV_RANK, H_C, NOPE_DIM + V_DIM_C)
            wk = jnp.pad(wukv[:, :, :NOPE_DIM], ((0, 0), (0, 0), (0, LANES - NOPE_DIM)))
            wk = wk.reshape(KV_RANK, H_C * LANES).astype(BF16)
            wvt = wukv[:, :, NOPE_DIM:].reshape(KV_RANK, H_C * V_DIM_C).T.astype(BF16)
            place = jnp.tile(_pad_cols(jnp.concatenate(
                [jnp.zeros((ROPE_DIM, NOPE_DIM), F32), jnp.eye(ROPE_DIM, dtype=F32)], axis=1), LANES),
                (1, H_C)).astype(BF16)
            gq = mla_q_norm[j][None, :]
            gkv = mla_kv_norm[j][None, :]
            wout = w_out_c[j].astype(BF16)

            q, ckv, kr = _proj_c(xp, win, gq, gkv, wuq, _rope_tables(pos_p, ROPE_DIM, NOPE_DIM),
                                 _rope_tables(pos_p, ROPE_DIM, 0))
            kc, vct = _kv_up(ckv.reshape(bp * tp, KV_RANK), kr.reshape(bp * tp, ROPE_DIM), wk, place, wvt)
            per_seq = lambda a, nb: a.reshape((nb, a.shape[0] // nb) + a.shape[1:])
            oc = _attention("mla", q, per_seq(kc, bp), per_seq(vct, bp), (), n_pairs=H_C // 2,
                            mask_shift=int(math.log2(CHUNK)))
            out_c_p.append((ckv, kr))
            xp2 = _outproj_ln(xp.reshape(bp * tp, d), [oc.reshape(bp * tp, -1)], [wout],
                              ln1_g[i][None, :], ln1_b[i][None, :], alpha)

            q, ckv, kr = _proj_c(xs, win, gq, gkv, wuq, _rope_tables(pos_s, ROPE_DIM, NOPE_DIM),
                                 _rope_tables(pos_s, ROPE_DIM, 0))
            ckv_all = _cat_pad_time(cache_mla_ckv[j].astype(F32), ckv.reshape(bs, ts, KV_RANK), t_dec)
            kr_all = _cat_pad_time(cache_mla_krope[j].astype(F32), kr.reshape(bs, ts, ROPE_DIM), t_dec)
            kc, vct = _kv_up(ckv_all.reshape(bs * t_dec, KV_RANK), kr_all.reshape(bs * t_dec, ROPE_DIM),
                             wk, place, wvt)
            oc = _attention("mla", _pad_rows(q.reshape(bs, ts, -1), DEC_Q_ROWS), per_seq(kc, bs),
                            per_seq(vct, bs), (), n_pairs=H_C // 2, mask_shift=dec_shift,
                            nfull_static=nfull_dec)[:, :ts]
            out_c_s.append((ckv.reshape(bs, ts, KV_RANK), kr.reshape(bs, ts, ROPE_DIM)))
            xs2 = _outproj_ln(xs.reshape(ns, d), [oc.reshape(ns, -1)], [wout],
                              ln1_g[i][None, :], ln1_b[i][None, :], alpha)

        wr = _pad_cols(jnp.concatenate(
            [moe_wg[i]] + [moe_we[i][gi] for gi in range(N_GROUPS)], axis=1), LANES)
        wrh = wr.astype(BF16)
        wrl = (wr - wrh.astype(F32)).astype(BF16)
        br = _pad_cols(jnp.concatenate([moe_bg[i], moe_be[i].reshape(-1)])[None, :], LANES)
        w13 = jnp.concatenate([moe_w1[i], moe_w3[i]], axis=2).astype(BF16)
        w2 = moe_w2[i].astype(BF16)
        g2, b2 = ln2_g[i][None, :], ln2_b[i][None, :]
        xp = _moe_ln(xp2, wrh, wrl, br, w13, w2, g2, b2, alpha).reshape(bp, tp, d)
        xs = _moe_ln(xs2, wrh, wrl, br, w13, w2, g2, b2, alpha).reshape(1, ns, d)

    stack = lambda rows, n: jnp.stack([r[n] for r in rows])
    return (xp, xs.reshape(bs, ts, d),
            stack(out_ab_p, 0), stack(out_ab_p, 1), stack(out_ab_p, 2), stack(out_ab_p, 3), stack(out_ab_p, 4),
            stack(out_c_p, 0), stack(out_c_p, 1),
            stack(out_ab_s, 0), stack(out_ab_s, 1), stack(out_ab_s, 2), stack(out_ab_s, 3), stack(out_ab_s, 4),
            stack(out_c_s, 0), stack(out_c_s, 1))
```

```python
import functools
import math

import jax
import jax.numpy as jnp
from jax import lax
from jax.experimental import pallas as pl
from jax.experimental.pallas import tpu as pltpu

F32 = jnp.float32
BF16 = jnp.bfloat16

D_MODEL = 1024
CHUNK = 64
HEAD_DIM = 64
ROPE_THETA = 10000.0
H_A = 8
H_B = 4
H_C = 16
Q_RANK = 256
KV_RANK = 128
NOPE_DIM = 64
ROPE_DIM = 32
V_DIM_C = 64
N_GROUPS = 4
EXPERTS_PER_GROUP = 4
N_EXPERTS = N_GROUPS * EXPERTS_PER_GROUP
D_EXPERT = 256
A_WIDTH = H_A * HEAD_DIM
B_QK_WIDTH = H_B * 2 * HEAD_DIM
B_V_WIDTH = H_B * 2 * HEAD_DIM
FGATE_BIAS = 3.0
LN_EPS = 1e-5
RMS_EPS = 1e-6
NEG_INF = -1e30
LOG2E = math.log2(math.e)

LANES = 128
BF16_ROWS = 16
PANEL = 256
BIAS_PIECES = 3
DEC_Q_ROWS = 128
VMEM_LIMIT = 48 * 1024 * 1024
ATTN_BLOCK = 512
ROW_TILE = 512
MOE_TILE = 1024
GATE_LANE0 = N_GROUPS


def _cparams(sem):
    return pltpu.CompilerParams(dimension_semantics=sem, vmem_limit_bytes=VMEM_LIMIT)


def _rope3(x, c, s1, s2, shift_up, shift_down):
    return x * c + pltpu.roll(x, shift_up, 1) * s1 + pltpu.roll(x, shift_down, 1) * s2


def _layer_norm(y, g, b):
    mu = jnp.mean(y, axis=-1, keepdims=True)
    d = y - mu
    var = jnp.mean(d * d, axis=-1, keepdims=True)
    return d * lax.rsqrt(var + LN_EPS) * g + b


def _split3(x):
    hi = x.astype(BF16)
    r1 = x - hi.astype(F32)
    mid = r1.astype(BF16)
    return hi, mid, (r1 - mid.astype(F32)).astype(BF16)


def _proj_ab_kernel(x_ref, w_ref, wvt_ref, wf_ref, bf_ref, c_ref, s1_ref, s2_ref,
                    qa_ref, ka_ref, kab_ref, va_ref, vat_ref, lf_ref, lfw_ref,
                    qb_ref, kb_ref, kbb_ref, vb_ref, vbt_ref):
    xb = x_ref[0].astype(BF16)

    def mm(i):
        return jnp.dot(xb, w_ref[i], preferred_element_type=F32)

    def mm_t(i):
        return lax.dot_general(wvt_ref[i], xb, (((1,), (1,)), ((), ())), preferred_element_type=F32)

    qa_ref[0] = (mm(0) * (HEAD_DIM ** -0.5 * LOG2E)).astype(BF16)
    ka = mm(1)
    ka_ref[0] = ka
    kab_ref[0] = ka.astype(BF16)
    va_ref[0] = mm(2)
    vat_ref[0, 0] = mm_t(0).astype(BF16)

    z = jnp.dot(xb, wf_ref[...], preferred_element_type=F32) + bf_ref[...]
    lf = jnp.minimum(z, 0.0) - jnp.log1p(jnp.exp(-jnp.abs(z)))
    lf_ref[0] = lf[:, :H_A]
    lfw_ref[0] = lf

    c, s1, s2 = c_ref[...], s1_ref[...], s2_ref[...]
    qb = mm(3)
    kb = mm(4)
    for s in range(B_QK_WIDTH // LANES):
        sl = slice(s * LANES, (s + 1) * LANES)
        qs = _rope3(qb[:, sl], c, s1, s2, LANES - HEAD_DIM // 2, HEAD_DIM // 2)
        qb_ref[0, :, sl] = (qs * (HEAD_DIM ** -0.5 * LOG2E)).astype(BF16)
        ks = _rope3(kb[:, sl], c, s1, s2, LANES - HEAD_DIM // 2, HEAD_DIM // 2)
        kb_ref[0, :, sl] = ks
        kbb_ref[0, :, sl] = ks.astype(BF16)
    vb_ref[0] = mm(5)
    vbt_ref[0, 0] = mm_t(1).astype(BF16)


def _proj_ab(x, w6, wvt, wf, bf, tabs):
    nb, t, _ = x.shape
    tm = min(ROW_TILE, t)
    assert t % tm == 0
    w = A_WIDTH
    tok = lambda width: pl.BlockSpec((1, tm, width), lambda b, i: (b, i, 0))
    tr = pl.BlockSpec((1, 1, w, tm), lambda b, i: (b, i, 0, 0))
    tab = pl.BlockSpec((tm, LANES), lambda b, i: (i, 0))
    full = lambda a: pl.BlockSpec(a.shape, lambda b, i: (0,) * a.ndim)
    sds = lambda width, dt: jax.ShapeDtypeStruct((nb, t, width), dt)
    sds_t = jax.ShapeDtypeStruct((nb, t // tm, w, tm), BF16)
    return pl.pallas_call(
        _proj_ab_kernel,
        grid=(nb, t // tm),
        in_specs=[tok(D_MODEL), full(w6), full(wvt), full(wf), full(bf), tab, tab, tab],
        out_specs=[tok(w), tok(w), tok(w), tok(w), tr, tok(H_A), tok(LANES), tok(w), tok(w), tok(w), tok(w), tr],
        out_shape=[sds(w, BF16), sds(w, F32), sds(w, BF16), sds(w, F32), sds_t, sds(H_A, F32), sds(LANES, F32),
                   sds(w, BF16), sds(w, F32), sds(w, BF16), sds(w, F32), sds_t],
        compiler_params=_cparams(("parallel", "parallel")),
        name="proj_ab",
    )(x, w6, wvt, wf, bf, *tabs)


def _decay_bias_kernel(lf_ref, o_ref, carry_ref):
    @pl.when(pl.program_id(1) == 0)
    def _():
        carry_ref[...] = jnp.zeros_like(carry_ref)

    x = lf_ref[0]
    tc = x.shape[0]
    src = lax.broadcasted_iota(jnp.int32, (LANES, LANES), 0)
    dst = lax.broadcasted_iota(jnp.int32, (LANES, LANES), 1)
    spread = ((dst >= BIAS_PIECES * src) & (dst < BIAS_PIECES * (src + 1)) & (src < H_A)).astype(BF16)
    xr = sum(jnp.dot(p, spread, preferred_element_type=F32) for p in _split3(x))
    row = lax.broadcasted_iota(jnp.int32, (tc, tc), 0)
    col = lax.broadcasted_iota(jnp.int32, (tc, tc), 1)
    lower = (col <= row).astype(BF16)
    c = sum(jnp.dot(lower, p, preferred_element_type=F32) for p in _split3(xr)) + carry_ref[...]
    carry_ref[...] = c[tc - 1:tc, :]
    hi, mid, lo = (p.astype(F32) for p in _split3(c * (-LOG2E)))
    lane = lax.broadcasted_iota(jnp.int32, (1, LANES), 1).astype(F32)
    piece = lane - BIAS_PIECES * jnp.floor((lane + 0.5) * (1.0 / BIAS_PIECES))
    o_ref[0] = jnp.where(piece == 0.0, hi, jnp.where(piece == 1.0, mid, lo)).astype(BF16)


def _decay_bias(lf_wide):
    nb, t, _ = lf_wide.shape
    tc = min(ATTN_BLOCK, t)
    assert t % tc == 0
    spec = pl.BlockSpec((1, tc, LANES), lambda b, i: (b, i, 0))
    return pl.pallas_call(
        _decay_bias_kernel,
        grid=(nb, t // tc),
        in_specs=[spec],
        out_specs=spec,
        out_shape=jax.ShapeDtypeStruct((nb, t, LANES), BF16),
        scratch_shapes=[pltpu.VMEM((1, LANES), F32)],
        compiler_params=_cparams(("parallel", "arbitrary")),
        name="cumsum",
    )(lf_wide)


def _attn_kernel(*refs, mode, tq, tk, mask_shift, nfull_static, n_diag, lam_init):
    if mode == "diff":
        q_ref, k_ref, vt_ref, lq1_ref, lk1_ref, lq2_ref, lk2_ref, sub_ref, o_ref = refs[:9]
    elif mode == "fox":
        q_ref, k_ref, vt_ref, b_ref, o_ref = refs[:5]
    else:
        q_ref, k_ref, vt_ref, o_ref = refs[:4]
    m_sc, acc_sc, sa, bma, sb, bmb = refs[-6:]
    v_rows = LANES if mode == "diff" else HEAD_DIM
    sa_sc, sb_sc = (sa, bma), (sb, bmb)

    qi = pl.program_id(2)
    q = q_ref[0]
    lane = lax.broadcasted_iota(jnp.int32, (1, LANES), 1)
    if mode == "mla":
        qs = [q[:, :LANES], q[:, LANES:]]
    else:
        zero = jnp.zeros_like(q)
        qs = [jnp.where(lane < HEAD_DIM, q, zero), jnp.where(lane >= HEAD_DIM, q, zero)]
        if mode == "fox":
            def pick(i):
                lo = BIAS_PIECES * (2 * pl.program_id(1) + i)
                hot = jnp.where((lane >= lo) & (lane < lo + BIAS_PIECES), 1.0, 0.0)
                return jnp.broadcast_to(hot, (tq, LANES)).astype(BF16)

            qs = [jnp.concatenate([qs[i], pick(i)], axis=1) for i in range(2)]

    m_sc[...] = jnp.full(m_sc.shape, NEG_INF, F32)
    acc_sc[...] = jnp.zeros(acc_sc.shape, F32)

    pw = min(PANEL, tq)

    def scores(j, bufs, c):
        s_sc, bm_sc = bufs
        cs = slice(c * pw, (c + 1) * pw)
        k = k_ref[0, j]
        if mode == "fox":
            k = jnp.concatenate([k, b_ref[0, j]], axis=1)
        for i in range(2):
            ki = k[:, i * LANES:(i + 1) * LANES] if mode == "mla" else k
            st = lax.dot_general(ki, qs[i][cs], (((1,), (1,)), ((), ())), preferred_element_type=F32)
            s_sc[i, c] = st
            bm_sc[i, c] = jnp.max(st, axis=0, keepdims=True)

    def consume(j, bufs, c, diag):
        s_sc, bm_sc = bufs
        cs = slice(c * pw, (c + 1) * pw)
        vt = vt_ref[0, j]
        for i in range(2):
            st = s_sc[i, c]
            if diag is not None:
                key = lax.broadcasted_iota(jnp.int32, (tk, pw), 0) + diag * tk
                qry = lax.broadcasted_iota(jnp.int32, (tk, pw), 1) + c * pw
                vis = lax.shift_right_logical(key, mask_shift) <= lax.shift_right_logical(qry, mask_shift)
                st = jnp.where(vis, st, NEG_INF)
                blk_max = jnp.max(st, axis=0, keepdims=True)
            else:
                blk_max = bm_sc[i, c]
            m_prev = m_sc[i, c]
            m_new = jnp.maximum(m_prev, blk_max)
            alpha = jnp.exp2(m_prev - m_new)
            p = jnp.exp2(st - m_new).astype(BF16)
            vi = vt if mode == "diff" else vt[i * HEAD_DIM:(i + 1) * HEAD_DIM]
            vi = jnp.concatenate([vi, jnp.ones((BF16_ROWS, tk), BF16)], axis=0)
            acc_sc[i, c] = alpha * acc_sc[i, c] + jnp.dot(vi, p, preferred_element_type=F32)
            m_sc[i, c] = m_new

    def stage(nxt, cur, diag=None):
        for c in range(tq // pw):
            if nxt is not None:
                scores(nxt[0], nxt[1], c)
            consume(cur[0], cur[1], c, diag)

    n_pairs_full = qi * (n_diag // 2) if nfull_static is None else nfull_static // 2
    nfull = 2 * n_pairs_full

    def pair(jj, carry):
        stage((2 * jj + 1, sb_sc), (2 * jj, sa_sc))
        stage((2 * jj + 2, sa_sc), (2 * jj + 1, sb_sc))
        return carry

    for c in range(tq // pw):
        scores(0, sa_sc, c)
    lax.fori_loop(0, n_pairs_full, pair, 0)
    if n_diag == 2:
        stage((nfull + 1, sb_sc), (nfull, sa_sc), diag=0)
        stage(None, (nfull + 1, sb_sc), diag=1)
    else:
        stage(None, (nfull, sa_sc), diag=0)

    if mode == "diff":
        lam = (jnp.exp(jnp.sum(lq1_ref[...] * lk1_ref[...], axis=1, keepdims=True))
               - jnp.exp(jnp.sum(lq2_ref[...] * lk2_ref[...], axis=1, keepdims=True)) + lam_init)
    for c in range(tq // pw):
        o0 = acc_sc[0, c, :v_rows] / acc_sc[0, c, v_rows:v_rows + 1]
        o1 = acc_sc[1, c, :v_rows] / acc_sc[1, c, v_rows:v_rows + 1]
        if mode == "diff":
            o = o0 - lam * o1
            ms = jnp.mean(o * o, axis=0, keepdims=True)
            o = (o * lax.rsqrt(ms + RMS_EPS)).T * sub_ref[...] * (1.0 - lam_init)
        else:
            o = jnp.concatenate([o0, o1], axis=0).T
        o_ref[0, c * pw:(c + 1) * pw, :] = o.astype(o_ref.dtype)


def _attention(mode, q, k, vt, extra, *, n_pairs, mask_shift, nfull_static=None, lam_init=0.0):
    nb, t_q, _ = q.shape
    _, nkb, tk, _ = k.shape
    if nfull_static is None:
        tq, n_diag = 2 * tk, 2
        assert t_q % tq == 0 and nkb == t_q // tk
    else:
        tq, n_diag = t_q, 1
        assert nfull_static % 2 == 0 and nkb == nfull_static + 1
    pw = min(PANEL, tq)
    n_pan = tq // pw
    qw = 2 * LANES if mode == "mla" else LANES
    in_specs = [
        pl.BlockSpec((1, tq, qw), lambda b, p, i: (b, i, p)),
        pl.BlockSpec((1, nkb, tk, qw), lambda b, p, i: (b, 0, 0, p)),
        pl.BlockSpec((1, nkb, LANES, tk), lambda b, p, i: (b, 0, p, 0)),
    ]
    if mode == "fox":
        in_specs.append(pl.BlockSpec((1, nkb, tk, LANES), lambda b, p, i: (b, 0, 0, 0)))
    elif mode == "diff":
        in_specs += [pl.BlockSpec(a.shape, lambda b, p, i: (0, 0)) for a in extra]
    kern = functools.partial(_attn_kernel, mode=mode, tq=tq, tk=tk, mask_shift=mask_shift,
                             nfull_static=nfull_static, n_diag=n_diag, lam_init=lam_init)
    return pl.pallas_call(
        kern,
        grid=(nb, n_pairs, t_q // tq),
        in_specs=in_specs,
        out_specs=pl.BlockSpec((1, tq, LANES), lambda b, p, i: (b, i, p)),
        out_shape=jax.ShapeDtypeStruct((nb, t_q, n_pairs * LANES), BF16),
        scratch_shapes=[pltpu.VMEM((2, n_pan, 1, pw), F32),
                        pltpu.VMEM((2, n_pan, (LANES if mode == "diff" else HEAD_DIM) + BF16_ROWS, pw), F32),
                        pltpu.VMEM((2, n_pan, tk, pw), F32), pltpu.VMEM((2, n_pan, 1, pw), F32),
                        pltpu.VMEM((2, n_pan, tk, pw), F32), pltpu.VMEM((2, n_pan, 1, pw), F32)],
        compiler_params=_cparams(("parallel", "parallel", "arbitrary")),
        name="attn_" + mode,
    )(q, k, vt, *extra)


def _outproj_ln_kernel(*refs, n_in, alpha):
    x_ref = refs[0]
    o_refs = refs[1:1 + n_in]
    w_refs = refs[1 + n_in:1 + 2 * n_in]
    g_ref, b_ref, y_ref = refs[1 + 2 * n_in:]
    mix = jnp.dot(o_refs[0][...], w_refs[0][...], preferred_element_type=F32)
    for o_r, w_r in zip(o_refs[1:], w_refs[1:]):
        mix = mix + jnp.dot(o_r[...], w_r[...], preferred_element_type=F32)
    y_ref[...] = _layer_norm(alpha * x_ref[...] + mix, g_ref[...], b_ref[...])


def _outproj_ln(x, outs, ws, g, b, alpha):
    n, d = x.shape
    tm = min(ROW_TILE, n)
    assert n % tm == 0
    row = lambda width: pl.BlockSpec((tm, width), lambda i: (i, 0))
    full = lambda a: pl.BlockSpec(a.shape, lambda i: (0, 0))
    return pl.pallas_call(
        functools.partial(_outproj_ln_kernel, n_in=len(outs), alpha=alpha),
        grid=(n // tm,),
        in_specs=[row(d)] + [row(o.shape[1]) for o in outs] + [full(w) for w in ws] + [full(g), full(b)],
        out_specs=row(d),
        out_shape=jax.ShapeDtypeStruct((n, d), F32),
        compiler_params=_cparams(("parallel",)),
        name="outproj_ln",
    )(x, *outs, *ws, g, b)


def _route(logits):
    lane = lax.broadcasted_iota(jnp.int32, logits.shape, 1).astype(F32)
    big = float(1 << 20)
    is_g = lane < N_GROUPS
    lg = jnp.where(is_g, logits, NEG_INF)
    eg = jnp.where(is_g, jnp.exp(lg - jnp.max(lg, axis=1, keepdims=True)), 0.0)
    pg = eg / jnp.sum(eg, axis=1, keepdims=True)
    p_g = jnp.max(pg, axis=1, keepdims=True)
    gidx = jnp.min(jnp.where(is_g & (pg == p_g), lane, big), axis=1, keepdims=True)
    lo = GATE_LANE0 + EXPERTS_PER_GROUP * gidx
    sel = (lane >= lo) & (lane < lo + EXPERTS_PER_GROUP)
    le = jnp.where(sel, logits, NEG_INF)
    ee = jnp.where(sel, jnp.exp(le - jnp.max(le, axis=1, keepdims=True)), 0.0)
    pe = ee / jnp.sum(ee, axis=1, keepdims=True)
    v1 = jnp.max(jnp.where(sel, pe, -1.0), axis=1, keepdims=True)
    i1 = jnp.min(jnp.where(sel & (pe == v1), lane, big), axis=1, keepdims=True)
    rest = sel & (lane != i1)
    v2 = jnp.max(jnp.where(rest, pe, -1.0), axis=1, keepdims=True)
    i2 = jnp.min(jnp.where(rest & (pe == v2), lane, big), axis=1, keepdims=True)
    tot = v1 + v2
    w1 = v1 / tot * p_g
    w2 = v2 / tot * p_g
    return jnp.where(lane == i1, w1, jnp.where(lane == i2, w2, 0.0))


def _moe_ln_kernel(x_ref, wrh_ref, wrl_ref, br_ref, w13_ref, w2_ref, g_ref, b_ref, y_ref,
                   xb_sc, gate_sc, acc_sc, *, alpha):
    e = pl.program_id(1)

    @pl.when(e == 0)
    def _():
        x = x_ref[...]
        xh = x.astype(BF16)
        xl = (x - xh.astype(F32)).astype(BF16)
        xb_sc[...] = xh
        logits = (jnp.dot(xh, wrh_ref[...], preferred_element_type=F32)
                  + jnp.dot(xl, wrh_ref[...], preferred_element_type=F32)
                  + jnp.dot(xh, wrl_ref[...], preferred_element_type=F32) + br_ref[...])
        gate_sc[...] = _route(logits)
        acc_sc[...] = jnp.zeros_like(acc_sc)

    h = jnp.dot(xb_sc[...], w13_ref[0], preferred_element_type=F32)
    h1 = h[:, :D_EXPERT]
    h3 = h[:, D_EXPERT:]
    hdn = (h1 * jax.nn.sigmoid(h1)) * h3
    y = jnp.dot(hdn.astype(BF16), w2_ref[0], preferred_element_type=F32)
    lane = lax.broadcasted_iota(jnp.int32, (1, LANES), 1)
    ge = jnp.sum(jnp.where(lane == e + GATE_LANE0, gate_sc[...], 0.0), axis=1, keepdims=True)
    acc_sc[...] += ge * y

    @pl.when(e == pl.num_programs(1) - 1)
    def _():
        y_ref[...] = _layer_norm(alpha * x_ref[...] + acc_sc[...], g_ref[...], b_ref[...])


def _moe_ln(x, wrh, wrl, br, w13, w2, g, b, alpha):
    n, d = x.shape
    tm = min(MOE_TILE, n)
    assert n % tm == 0
    ne = w13.shape[0]
    full = lambda a: pl.BlockSpec(a.shape, lambda i, e: (0, 0))
    return pl.pallas_call(
        functools.partial(_moe_ln_kernel, alpha=alpha),
        grid=(n // tm, ne),
        in_specs=[pl.BlockSpec((tm, d), lambda i, e: (i, 0)), full(wrh), full(wrl), full(br),
                  pl.BlockSpec((1,) + w13.shape[1:], lambda i, e: (e, 0, 0)),
                  pl.BlockSpec((1,) + w2.shape[1:], lambda i, e: (e, 0, 0)),
                  full(g), full(b)],
        out_specs=pl.BlockSpec((tm, d), lambda i, e: (i, 0)),
        out_shape=jax.ShapeDtypeStruct((n, d), F32),
        scratch_shapes=[pltpu.VMEM((tm, d), BF16), pltpu.VMEM((tm, LANES), F32), pltpu.VMEM((tm, d), F32)],
        compiler_params=_cparams(("parallel", "arbitrary")),
        name="moe_ln",
    )(x, wrh, wrl, br, w13, w2, g, b)


def _proj_c_kernel(x_ref, win_ref, gq_ref, gkv_ref, wuq_ref, cq_ref, s1q_ref, s2q_ref,
                   ck_ref, s1k_ref, s2k_ref, q_ref, ckv_ref, kr_ref):
    xb = x_ref[0].astype(BF16)
    h = jnp.dot(xb, win_ref[...], preferred_element_type=F32)
    qa = h[:, :Q_RANK]
    kva = h[:, Q_RANK:Q_RANK + KV_RANK]
    krw = h[:, Q_RANK + KV_RANK:]
    qn = qa * lax.rsqrt(jnp.mean(qa * qa, axis=1, keepdims=True) + RMS_EPS) * gq_ref[...]
    ckv_ref[0] = kva * lax.rsqrt(jnp.mean(kva * kva, axis=1, keepdims=True) + RMS_EPS) * gkv_ref[...]
    half = ROPE_DIM // 2
    kr = _rope3(krw, ck_ref[...], s1k_ref[...], s2k_ref[...], LANES - half, half)
    kr_ref[0] = kr[:, :ROPE_DIM]
    q = jnp.dot(qn.astype(BF16), wuq_ref[...], preferred_element_type=F32)
    cq, s1q, s2q = cq_ref[...], s1q_ref[...], s2q_ref[...]
    scale = (NOPE_DIM + ROPE_DIM) ** -0.5 * LOG2E
    for hd in range(H_C):
        sl = slice(hd * LANES, (hd + 1) * LANES)
        q_ref[0, :, sl] = (_rope3(q[:, sl], cq, s1q, s2q, LANES - half, half) * scale).astype(BF16)


def _proj_c(x, win, gq, gkv, wuq, tabs_q, tabs_k):
    nb, t, _ = x.shape
    tm = min(ROW_TILE, t)
    assert t % tm == 0
    tok = lambda width: pl.BlockSpec((1, tm, width), lambda b, i: (b, i, 0))
    tab = pl.BlockSpec((tm, LANES), lambda b, i: (i, 0))
    full = lambda a: pl.BlockSpec(a.shape, lambda b, i: (0, 0))
    return pl.pallas_call(
        _proj_c_kernel,
        grid=(nb, t // tm),
        in_specs=[tok(D_MODEL), full(win), full(gq), full(gkv), full(wuq)] + [tab] * 6,
        out_specs=[tok(H_C * LANES), tok(KV_RANK), tok(ROPE_DIM)],
        out_shape=[jax.ShapeDtypeStruct((nb, t, H_C * LANES), BF16),
                   jax.ShapeDtypeStruct((nb, t, KV_RANK), F32),
                   jax.ShapeDtypeStruct((nb, t, ROPE_DIM), F32)],
        compiler_params=_cparams(("parallel", "parallel")),
        name="proj_c",
    )(x, win, gq, gkv, wuq, *tabs_q, *tabs_k)


def _kv_up_kernel(ckv_ref, kr_ref, wk_ref, place_ref, wvt_ref, k_ref, vt_ref):
    cb = ckv_ref[...].astype(BF16)
    k = (jnp.dot(cb, wk_ref[...], preferred_element_type=F32)
         + jnp.dot(kr_ref[...].astype(BF16), place_ref[...], preferred_element_type=F32))
    k_ref[0] = k.astype(BF16)
    vt = lax.dot_general(wvt_ref[...], cb, (((1,), (1,)), ((), ())), preferred_element_type=F32)
    vt_ref[0] = vt.astype(BF16)


def _kv_up(ckv, kr, wk, place, wvt):
    n = ckv.shape[0]
    tm = ATTN_BLOCK
    assert n % tm == 0
    row = lambda width: pl.BlockSpec((tm, width), lambda i: (i, 0))
    full = lambda a: pl.BlockSpec(a.shape, lambda i: (0, 0))
    return pl.pallas_call(
        _kv_up_kernel,
        grid=(n // tm,),
        in_specs=[row(KV_RANK), row(ROPE_DIM), full(wk), full(place), full(wvt)],
        out_specs=[pl.BlockSpec((1, tm, H_C * LANES), lambda i: (i, 0, 0)),
                   pl.BlockSpec((1, H_C * V_DIM_C, tm), lambda i: (i, 0, 0))],
        out_shape=[jax.ShapeDtypeStruct((n // tm, tm, H_C * LANES), BF16),
                   jax.ShapeDtypeStruct((n // tm, H_C * V_DIM_C, tm), BF16)],
        compiler_params=_cparams(("parallel",)),
        name="kv_up",
    )(ckv, kr, wk, place, wvt)


def _rope_tables(pos, dim, lane0):
    half = dim // 2
    inv = ROPE_THETA ** (-jnp.arange(0, dim, 2, dtype=F32) / dim)
    ang = pos.astype(F32)[:, None] * inv[None, :]
    cos, sin = jnp.cos(ang), jnp.sin(ang)
    zero = jnp.zeros_like(sin)
    c = jnp.concatenate([cos, cos], axis=1)
    s1 = jnp.concatenate([-sin, zero], axis=1)
    s2 = jnp.concatenate([zero, sin], axis=1)
    if lane0 < 0:
        reps = LANES // dim
        return tuple(jnp.tile(a, (1, reps)) for a in (c, s1, s2))
    t = pos.shape[0]
    pad = lambda a, fill: jnp.concatenate(
        [jnp.full((t, lane0), fill, F32), a, jnp.full((t, LANES - lane0 - dim), fill, F32)], axis=1)
    return pad(c, 1.0), pad(s1, 0.0), pad(s2, 0.0)


def _pad_cols(a, width):
    return jnp.pad(a, ((0, 0), (0, width - a.shape[1])))


def _blocks(a, tk):
    nb, t, l = a.shape
    return a.reshape(nb, t // tk, tk, l)


def _vt_blocks(v, tk):
    nb, t, l = v.shape
    return v.reshape(nb, t // tk, tk, l).transpose(0, 1, 3, 2)


def _pad_rows(q, rows):
    return jnp.pad(q, ((0, 0), (0, rows - q.shape[1]), (0, 0)))


def _cat_pad_time(cache, new, t_pad):
    nb, t0, l = cache.shape
    t1 = new.shape[1]
    return jnp.concatenate([cache, new, jnp.zeros((nb, t_pad - t0 - t1, l), cache.dtype)], axis=1)


def kernel(x_prompt, x_sample, cache_fox_k, cache_fox_v, cache_fox_logf, cache_diff_k, cache_diff_v, cache_mla_ckv, cache_mla_krope, w_in_ab, b_fgate, diff_lq1, diff_lk1, diff_lq2, diff_lk2, diff_subln, w_out_ab, w_in_c, mla_q_norm, mla_kv_norm, mla_w_uq, mla_w_ukv, w_out_c, ln1_g, ln1_b, ln2_g, ln2_b, moe_wg, moe_bg, moe_we, moe_be, moe_w1, moe_w3, moe_w2):
    bp, tp, d = x_prompt.shape
    bs, ts, _ = x_sample.shape
    past = cache_fox_k.shape[2]
    depth = ln1_g.shape[0]
    alpha = (2 * depth) ** 0.25
    tk = ATTN_BLOCK
    assert past % tk == 0 and ts == 16 and past % CHUNK == 0
    ns = bs * ts
    t_dec = past + tk
    nfull_dec = past // tk
    dec_shift = 4

    pos_p = jnp.arange(tp)
    pos_s = jnp.tile(past + jnp.arange(ts), bs)

    xp = x_prompt
    xs = x_sample.reshape(1, ns, d)
    out_ab_p, out_ab_s, out_c_p, out_c_s = [], [], [], []

    for i in range(depth):
        j = i // 2
        if i % 2 == 0:
            lam_init = 0.8 - 0.6 * math.exp(-0.3 * i)
            cuts = [0, A_WIDTH, 2 * A_WIDTH, 3 * A_WIDTH, 3 * A_WIDTH + H_A,
                    3 * A_WIDTH + H_A + B_QK_WIDTH, 3 * A_WIDTH + H_A + 2 * B_QK_WIDTH,
                    3 * A_WIDTH + H_A + 2 * B_QK_WIDTH + B_V_WIDTH]
            w = w_in_ab[j]
            piece = lambda a: w[:, cuts[a]:cuts[a + 1]]
            w6 = jnp.stack([piece(0), piece(1), piece(2), piece(4), piece(5), piece(6)]).astype(BF16)
            wvt = jnp.stack([piece(2).T, piece(6).T]).astype(BF16)
            wf = _pad_cols(piece(3), LANES).astype(BF16)
            bf = _pad_cols(b_fgate[j][None, :], LANES)
            wout = w_out_ab[j].astype(BF16)
            diff_extra = (diff_lq1[j][None, :], diff_lk1[j][None, :], diff_lq2[j][None, :],
                          diff_lk2[j][None, :], diff_subln[j][None, :])

            tabs = _rope_tables(pos_p, HEAD_DIM, -1)
            (qa, ka, kab, va, vat, lf, lfw, qb, kb, kbb, vb, vbt) = _proj_ab(xp, w6, wvt, wf, bf, tabs)
            bias = _blocks(_decay_bias(lfw), tk)
            oa = _attention("fox", qa, _blocks(kab, tk), vat, (bias,), n_pairs=H_A // 2, mask_shift=0)
            ob = _attention("diff", qb, _blocks(kbb, tk), vbt, diff_extra,
                            n_pairs=H_B, mask_shift=int(math.log2(CHUNK)), lam_init=lam_init)
            out_ab_p.append((ka.reshape(bp, tp, H_A, HEAD_DIM), va.reshape(bp, tp, H_A, HEAD_DIM), lf,
                             kb.reshape(bp, tp, H_B, 2, HEAD_DIM), vb.reshape(bp, tp, H_B, 2 * HEAD_DIM)))
            xp2 = _outproj_ln(xp.reshape(bp * tp, d), [oa.reshape(bp * tp, -1), ob.reshape(bp * tp, -1)],
                              [wout[:A_WIDTH], wout[A_WIDTH:]], ln1_g[i][None, :], ln1_b[i][None, :], alpha)

            tabs = _rope_tables(pos_s, HEAD_DIM, -1)
            (qa, ka, kab, va, _, lf, lfw, qb, kb, kbb, vb, _) = _proj_ab(xs, w6, wvt, wf, bf, tabs)
            rs = lambda a: a.reshape(bs, ts, a.shape[-1])
            cache_lfw = jnp.pad(cache_fox_logf[j].astype(F32), ((0, 0), (0, 0), (0, LANES - H_A)))
            bias = _blocks(_decay_bias(_cat_pad_time(cache_lfw, rs(lfw), t_dec)), tk)
            flat = lambda c: c.reshape(bs, past, -1).astype(BF16)
            qdec = lambda a: _pad_rows(rs(a), DEC_Q_ROWS)
            k_all = _blocks(_cat_pad_time(flat(cache_fox_k[j]), rs(kab), t_dec), tk)
            v_all = _vt_blocks(_cat_pad_time(flat(cache_fox_v[j]), rs(va).astype(BF16), t_dec), tk)
            oa = _attention("fox", qdec(qa), k_all, v_all, (bias,), n_pairs=H_A // 2, mask_shift=0,
                            nfull_static=nfull_dec)[:, :ts]
            k_all = _blocks(_cat_pad_time(flat(cache_diff_k[j]), rs(kbb), t_dec), tk)
            v_all = _vt_blocks(_cat_pad_time(flat(cache_diff_v[j]), rs(vb).astype(BF16), t_dec), tk)
            ob = _attention("diff", qdec(qb), k_all, v_all, diff_extra, n_pairs=H_B, mask_shift=dec_shift,
                            nfull_static=nfull_dec, lam_init=lam_init)[:, :ts]
            out_ab_s.append((ka.reshape(bs, ts, H_A, HEAD_DIM), va.reshape(bs, ts, H_A, HEAD_DIM),
                             lf.reshape(bs, ts, H_A), kb.reshape(bs, ts, H_B, 2, HEAD_DIM),
                             vb.reshape(bs, ts, H_B, 2 * HEAD_DIM)))
            xs2 = _outproj_ln(xs.reshape(ns, d), [oa.reshape(ns, -1), ob.reshape(ns, -1)],
                              [wout[:A_WIDTH], wout[A_WIDTH:]], ln1_g[i][None, :], ln1_b[i][None, :], alpha)
        else:
            wc = w_in_c[j]
            kr_cols = _pad_cols(wc[:, Q_RANK + KV_RANK:], LANES)
            win = jnp.concatenate([wc[:, :Q_RANK + KV_RANK], kr_cols], axis=1).astype(BF16)
            wuq = jnp.pad(mla_w_uq[j].reshape(Q_RANK, H_C, NOPE_DIM + ROPE_DIM),
                          ((0, 0), (0, 0), (0, LANES - NOPE_DIM - ROPE_DIM))).reshape(Q_RANK, H_C * LANES)
            wuq = wuq.astype(BF16)
            wukv = mla_w_ukv[j].reshape(KV_RANK, H_C, NOPE_DIM + V_DIM_C)
            wk = jnp.pad(wukv[:, :, :NOPE_DIM], ((0, 0), (0, 0), (0, LANES - NOPE_DIM)))
            wk = wk.reshape(KV_RANK, H_C * LANES).astype(BF16)
            wvt = wukv[:, :, NOPE_DIM:].reshape(KV_RANK, H_C * V_DIM_C).T.astype(BF16)
            place = jnp.tile(_pad_cols(jnp.concatenate(
                [jnp.zeros((ROPE_DIM, NOPE_DIM), F32), jnp.eye(ROPE_DIM, dtype=F32)], axis=1), LANES),
                (1, H_C)).astype(BF16)
            gq = mla_q_norm[j][None, :]
            gkv = mla_kv_norm[j][None, :]
            wout = w_out_c[j].astype(BF16)

            q, ckv, kr = _proj_c(xp, win, gq, gkv, wuq, _rope_tables(pos_p, ROPE_DIM, NOPE_DIM),
                                 _rope_tables(pos_p, ROPE_DIM, 0))
            kc, vct = _kv_up(ckv.reshape(bp * tp, KV_RANK), kr.reshape(bp * tp, ROPE_DIM), wk, place, wvt)
            per_seq = lambda a, nb: a.reshape((nb, a.shape[0] // nb) + a.shape[1:])
            oc = _attention("mla", q, per_seq(kc, bp), per_seq(vct, bp), (), n_pairs=H_C // 2,
                            mask_shift=int(math.log2(CHUNK)))
            out_c_p.append((ckv, kr))
            xp2 = _outproj_ln(xp.reshape(bp * tp, d), [oc.reshape(bp * tp, -1)], [wout],
                              ln1_g[i][None, :], ln1_b[i][None, :], alpha)

            q, ckv, kr = _proj_c(xs, win, gq, gkv, wuq, _rope_tables(pos_s, ROPE_DIM, NOPE_DIM),
                                 _rope_tables(pos_s, ROPE_DIM, 0))
            ckv_all = _cat_pad_time(cache_mla_ckv[j].astype(F32), ckv.reshape(bs, ts, KV_RANK), t_dec)
            kr_all = _cat_pad_time(cache_mla_krope[j].astype(F32), kr.reshape(bs, ts, ROPE_DIM), t_dec)
            kc, vct = _kv_up(ckv_all.reshape(bs * t_dec, KV_RANK), kr_all.reshape(bs * t_dec, ROPE_DIM),
                             wk, place, wvt)
            oc = _attention("mla", _pad_rows(q.reshape(bs, ts, -1), DEC_Q_ROWS), per_seq(kc, bs),
                            per_seq(vct, bs), (), n_pairs=H_C // 2, mask_shift=dec_shift,
                            nfull_static=nfull_dec)[:, :ts]
            out_c_s.append((ckv.reshape(bs, ts, KV_RANK), kr.reshape(bs, ts, ROPE_DIM)))
            xs2 = _outproj_ln(xs.reshape(ns, d), [oc.reshape(ns, -1)], [wout],
                              ln1_g[i][None, :], ln1_b[i][None, :], alpha)

        wr = _pad_cols(jnp.concatenate(
            [moe_wg[i]] + [moe_we[i][gi] for gi in range(N_GROUPS)], axis=1), LANES)
        wrh = wr.astype(BF16)
        wrl = (wr - wrh.astype(F32)).astype(BF16)
        br = _pad_cols(jnp.concatenate([moe_bg[i], moe_be[i].reshape(-1)])[None, :], LANES)
        w13 = jnp.concatenate([moe_w1[i], moe_w3[i]], axis=2).astype(BF16)
        w2 = moe_w2[i].astype(BF16)
        g2, b2 = ln2_g[i][None, :], ln2_b[i][None, :]
        xp = _moe_ln(xp2, wrh, wrl, br, w13, w2, g2, b2, alpha).reshape(bp, tp, d)
        xs = _moe_ln(xs2, wrh, wrl, br, w13, w2, g2, b2, alpha).reshape(1, ns, d)

    stack = lambda rows, n: jnp.stack([r[n] for r in rows])
    return (xp, xs.reshape(bs, ts, d),
            stack(out_ab_p, 0), stack(out_ab_p, 1), stack(out_ab_p, 2), stack(out_ab_p, 3), stack(out_ab_p, 4),
            stack(out_c_p, 0), stack(out_c_p, 1),
            stack(out_ab_s, 0), stack(out_ab_s, 1), stack(out_ab_s, 2), stack(out_ab_s, 3), stack(out_ab_s, 4),
            stack(out_c_s, 0), stack(out_c_s, 1))
```

```python
import functools
import math

import jax
import jax.numpy as jnp
from jax import lax
from jax.experimental import pallas as pl
from jax.experimental.pallas import tpu as pltpu

F32 = jnp.float32
BF16 = jnp.bfloat16

D_MODEL = 1024
CHUNK = 64
HEAD_DIM = 64
ROPE_THETA = 10000.0
H_A = 8
H_B = 4
H_C = 16
Q_RANK = 256
KV_RANK = 128
NOPE_DIM = 64
ROPE_DIM = 32
V_DIM_C = 64
N_GROUPS = 4
EXPERTS_PER_GROUP = 4
N_EXPERTS = N_GROUPS * EXPERTS_PER_GROUP
D_EXPERT = 256
A_WIDTH = H_A * HEAD_DIM
B_QK_WIDTH = H_B * 2 * HEAD_DIM
B_V_WIDTH = H_B * 2 * HEAD_DIM
FGATE_BIAS = 3.0
LN_EPS = 1e-5
RMS_EPS = 1e-6
NEG_INF = -1e30
LOG2E = math.log2(math.e)

LANES = 128
BF16_ROWS = 16
PANEL = {"fox": 512, "diff": 256, "mla": 512}
BIAS_PIECES = 3
DEC_Q_ROWS = 128
VMEM_LIMIT = 48 * 1024 * 1024
ATTN_BLOCK = 512
ROW_TILE = 512
MOE_TILE = 1024
GATE_LANE0 = N_GROUPS


def _cparams(sem):
    return pltpu.CompilerParams(dimension_semantics=sem, vmem_limit_bytes=VMEM_LIMIT)


def _rope3(x, c, s1, s2, shift_up, shift_down):
    return x * c + pltpu.roll(x, shift_up, 1) * s1 + pltpu.roll(x, shift_down, 1) * s2


def _layer_norm(y, g, b):
    mu = jnp.mean(y, axis=-1, keepdims=True)
    d = y - mu
    var = jnp.mean(d * d, axis=-1, keepdims=True)
    return d * lax.rsqrt(var + LN_EPS) * g + b


def _split3(x):
    hi = x.astype(BF16)
    r1 = x - hi.astype(F32)
    mid = r1.astype(BF16)
    return hi, mid, (r1 - mid.astype(F32)).astype(BF16)


def _proj_ab_kernel(x_ref, w_ref, wvt_ref, wf_ref, bf_ref, c_ref, s1_ref, s2_ref,
                    qa_ref, ka_ref, kab_ref, va_ref, vat_ref, lf_ref, lfw_ref,
                    qb_ref, kb_ref, kbb_ref, vb_ref, vbt_ref):
    xb = x_ref[0].astype(BF16)

    def mm(i):
        return jnp.dot(xb, w_ref[i], preferred_element_type=F32)

    def mm_t(i):
        return lax.dot_general(wvt_ref[i], xb, (((1,), (1,)), ((), ())), preferred_element_type=F32)

    qa_ref[0] = (mm(0) * (HEAD_DIM ** -0.5 * LOG2E)).astype(BF16)
    ka = mm(1)
    ka_ref[0] = ka
    kab_ref[0] = ka.astype(BF16)
    va_ref[0] = mm(2)
    vat_ref[0, 0] = mm_t(0).astype(BF16)

    z = jnp.dot(xb, wf_ref[...], preferred_element_type=F32) + bf_ref[...]
    lf = jnp.minimum(z, 0.0) - jnp.log1p(jnp.exp(-jnp.abs(z)))
    lf_ref[0] = lf[:, :H_A]
    lfw_ref[0] = lf

    c, s1, s2 = c_ref[...], s1_ref[...], s2_ref[...]
    qb = mm(3)
    kb = mm(4)
    for s in range(B_QK_WIDTH // LANES):
        sl = slice(s * LANES, (s + 1) * LANES)
        qs = _rope3(qb[:, sl], c, s1, s2, LANES - HEAD_DIM // 2, HEAD_DIM // 2)
        qb_ref[0, :, sl] = (qs * (HEAD_DIM ** -0.5 * LOG2E)).astype(BF16)
        ks = _rope3(kb[:, sl], c, s1, s2, LANES - HEAD_DIM // 2, HEAD_DIM // 2)
        kb_ref[0, :, sl] = ks
        kbb_ref[0, :, sl] = ks.astype(BF16)
    vb_ref[0] = mm(5)
    vbt_ref[0, 0] = mm_t(1).astype(BF16)


def _proj_ab(x, w6, wvt, wf, bf, tabs):
    nb, t, _ = x.shape
    tm = min(ROW_TILE, t)
    assert t % tm == 0
    w = A_WIDTH
    tok = lambda width: pl.BlockSpec((1, tm, width), lambda b, i: (b, i, 0))
    tr = pl.BlockSpec((1, 1, w, tm), lambda b, i: (b, i, 0, 0))
    tab = pl.BlockSpec((tm, LANES), lambda b, i: (i, 0))
    full = lambda a: pl.BlockSpec(a.shape, lambda b, i: (0,) * a.ndim)
    sds = lambda width, dt: jax.ShapeDtypeStruct((nb, t, width), dt)
    sds_t = jax.ShapeDtypeStruct((nb, t // tm, w, tm), BF16)
    return pl.pallas_call(
        _proj_ab_kernel,
        grid=(nb, t // tm),
        in_specs=[tok(D_MODEL), full(w6), full(wvt), full(wf), full(bf), tab, tab, tab],
        out_specs=[tok(w), tok(w), tok(w), tok(w), tr, tok(H_A), tok(LANES), tok(w), tok(w), tok(w), tok(w), tr],
        out_shape=[sds(w, BF16), sds(w, F32), sds(w, BF16), sds(w, F32), sds_t, sds(H_A, F32), sds(LANES, F32),
                   sds(w, BF16), sds(w, F32), sds(w, BF16), sds(w, F32), sds_t],
        compiler_params=_cparams(("parallel", "parallel")),
        name="proj_ab",
    )(x, w6, wvt, wf, bf, *tabs)


def _decay_bias_kernel(lf_ref, o_ref, carry_ref):
    @pl.when(pl.program_id(1) == 0)
    def _():
        carry_ref[...] = jnp.zeros_like(carry_ref)

    x = lf_ref[0]
    tc = x.shape[0]
    src = lax.broadcasted_iota(jnp.int32, (LANES, LANES), 0)
    dst = lax.broadcasted_iota(jnp.int32, (LANES, LANES), 1)
    spread = ((dst >= BIAS_PIECES * src) & (dst < BIAS_PIECES * (src + 1)) & (src < H_A)).astype(BF16)
    xr = sum(jnp.dot(p, spread, preferred_element_type=F32) for p in _split3(x))
    row = lax.broadcasted_iota(jnp.int32, (tc, tc), 0)
    col = lax.broadcasted_iota(jnp.int32, (tc, tc), 1)
    lower = (col <= row).astype(BF16)
    c = sum(jnp.dot(lower, p, preferred_element_type=F32) for p in _split3(xr)) + carry_ref[...]
    carry_ref[...] = c[tc - 1:tc, :]
    hi, mid, lo = (p.astype(F32) for p in _split3(c * (-LOG2E)))
    lane = lax.broadcasted_iota(jnp.int32, (1, LANES), 1).astype(F32)
    piece = lane - BIAS_PIECES * jnp.floor((lane + 0.5) * (1.0 / BIAS_PIECES))
    o_ref[0] = jnp.where(piece == 0.0, hi, jnp.where(piece == 1.0, mid, lo)).astype(BF16)


def _decay_bias(lf_wide):
    nb, t, _ = lf_wide.shape
    tc = min(ATTN_BLOCK, t)
    assert t % tc == 0
    spec = pl.BlockSpec((1, tc, LANES), lambda b, i: (b, i, 0))
    return pl.pallas_call(
        _decay_bias_kernel,
        grid=(nb, t // tc),
        in_specs=[spec],
        out_specs=spec,
        out_shape=jax.ShapeDtypeStruct((nb, t, LANES), BF16),
        scratch_shapes=[pltpu.VMEM((1, LANES), F32)],
        compiler_params=_cparams(("parallel", "arbitrary")),
        name="cumsum",
    )(lf_wide)


def _attn_kernel(*refs, mode, tq, tk, mask_shift, nfull_static, n_diag, lam_init):
    if mode == "diff":
        q_ref, k_ref, vt_ref, lq1_ref, lk1_ref, lq2_ref, lk2_ref, sub_ref, o_ref = refs[:9]
    elif mode == "fox":
        q_ref, k_ref, vt_ref, b_ref, o_ref = refs[:5]
    else:
        q_ref, k_ref, vt_ref, o_ref = refs[:4]
    m_sc, acc_sc, sa, bma, sb, bmb = refs[-6:]
    v_rows = LANES if mode == "diff" else HEAD_DIM
    sa_sc, sb_sc = (sa, bma), (sb, bmb)

    qi = pl.program_id(2)
    q = q_ref[0]
    lane = lax.broadcasted_iota(jnp.int32, (1, LANES), 1)
    if mode == "mla":
        qs = [q[:, :LANES], q[:, LANES:]]
    else:
        zero = jnp.zeros_like(q)
        qs = [jnp.where(lane < HEAD_DIM, q, zero), jnp.where(lane >= HEAD_DIM, q, zero)]
        if mode == "fox":
            def pick(i):
                lo = BIAS_PIECES * (2 * pl.program_id(1) + i)
                hot = jnp.where((lane >= lo) & (lane < lo + BIAS_PIECES), 1.0, 0.0)
                return jnp.broadcast_to(hot, (tq, LANES)).astype(BF16)

            qs = [jnp.concatenate([qs[i], pick(i)], axis=1) for i in range(2)]

    m_sc[...] = jnp.full(m_sc.shape, NEG_INF, F32)
    acc_sc[...] = jnp.zeros(acc_sc.shape, F32)

    pw = min(PANEL[mode], tq)

    def scores(j, bufs, c):
        s_sc, bm_sc = bufs
        cs = slice(c * pw, (c + 1) * pw)
        k = k_ref[0, j]
        if mode == "fox":
            k = jnp.concatenate([k, b_ref[0, j]], axis=1)
        for i in range(2):
            ki = k[:, i * LANES:(i + 1) * LANES] if mode == "mla" else k
            st = lax.dot_general(ki, qs[i][cs], (((1,), (1,)), ((), ())), preferred_element_type=F32)
            s_sc[i, c] = st
            bm_sc[i, c] = jnp.max(st, axis=0, keepdims=True)

    def consume(j, bufs, c, diag):
        s_sc, bm_sc = bufs
        cs = slice(c * pw, (c + 1) * pw)
        vt = vt_ref[0, j]
        for i in range(2):
            st = s_sc[i, c]
            if diag is not None:
                key = lax.broadcasted_iota(jnp.int32, (tk, pw), 0) + diag * tk
                qry = lax.broadcasted_iota(jnp.int32, (tk, pw), 1) + c * pw
                vis = lax.shift_right_logical(key, mask_shift) <= lax.shift_right_logical(qry, mask_shift)
                st = jnp.where(vis, st, NEG_INF)
                blk_max = jnp.max(st, axis=0, keepdims=True)
            else:
                blk_max = bm_sc[i, c]
            m_prev = m_sc[i, c]
            m_new = jnp.maximum(m_prev, blk_max)
            alpha = jnp.exp2(m_prev - m_new)
            p = jnp.exp2(st - m_new).astype(BF16)
            vi = vt if mode == "diff" else vt[i * HEAD_DIM:(i + 1) * HEAD_DIM]
            vi = jnp.concatenate([vi, jnp.ones((BF16_ROWS, tk), BF16)], axis=0)
            acc_sc[i, c] = alpha * acc_sc[i, c] + jnp.dot(vi, p, preferred_element_type=F32)
            m_sc[i, c] = m_new

    def stage(nxt, cur, diag=None):
        for c in range(tq // pw):
            if nxt is not None:
                scores(nxt[0], nxt[1], c)
            consume(cur[0], cur[1], c, diag)

    n_pairs_full = qi * (n_diag // 2) if nfull_static is None else nfull_static // 2
    nfull = 2 * n_pairs_full

    def pair(jj, carry):
        stage((2 * jj + 1, sb_sc), (2 * jj, sa_sc))
        stage((2 * jj + 2, sa_sc), (2 * jj + 1, sb_sc))
        return carry

    for c in range(tq // pw):
        scores(0, sa_sc, c)
    lax.fori_loop(0, n_pairs_full, pair, 0)
    if n_diag == 2:
        stage((nfull + 1, sb_sc), (nfull, sa_sc), diag=0)
        stage(None, (nfull + 1, sb_sc), diag=1)
    else:
        stage(None, (nfull, sa_sc), diag=0)

    if mode == "diff":
        lam = (jnp.exp(jnp.sum(lq1_ref[...] * lk1_ref[...], axis=1, keepdims=True))
               - jnp.exp(jnp.sum(lq2_ref[...] * lk2_ref[...], axis=1, keepdims=True)) + lam_init)
    for c in range(tq // pw):
        o0 = acc_sc[0, c, :v_rows] / acc_sc[0, c, v_rows:v_rows + 1]
        o1 = acc_sc[1, c, :v_rows] / acc_sc[1, c, v_rows:v_rows + 1]
        if mode == "diff":
            o = o0 - lam * o1
            ms = jnp.mean(o * o, axis=0, keepdims=True)
            o = (o * lax.rsqrt(ms + RMS_EPS)).T * sub_ref[...] * (1.0 - lam_init)
        else:
            o = jnp.concatenate([o0, o1], axis=0).T
        o_ref[0, c * pw:(c + 1) * pw, :] = o.astype(o_ref.dtype)


def _attention(mode, q, k, vt, extra, *, n_pairs, mask_shift, nfull_static=None, lam_init=0.0):
    nb, t_q, _ = q.shape
    _, nkb, tk, _ = k.shape
    if nfull_static is None:
        tq, n_diag = 2 * tk, 2
        assert t_q % tq == 0 and nkb == t_q // tk
    else:
        tq, n_diag = t_q, 1
        assert nfull_static % 2 == 0 and nkb == nfull_static + 1
    pw = min(PANEL[mode], tq)
    n_pan = tq // pw
    qw = 2 * LANES if mode == "mla" else LANES
    in_specs = [
        pl.BlockSpec((1, tq, qw), lambda b, p, i: (b, i, p)),
        pl.BlockSpec((1, nkb, tk, qw), lambda b, p, i: (b, 0, 0, p)),
        pl.BlockSpec((1, nkb, LANES, tk), lambda b, p, i: (b, 0, p, 0)),
    ]
    if mode == "fox":
        in_specs.append(pl.BlockSpec((1, nkb, tk, LANES), lambda b, p, i: (b, 0, 0, 0)))
    elif mode == "diff":
        in_specs += [pl.BlockSpec(a.shape, lambda b, p, i: (0, 0)) for a in extra]
    kern = functools.partial(_attn_kernel, mode=mode, tq=tq, tk=tk, mask_shift=mask_shift,
                             nfull_static=nfull_static, n_diag=n_diag, lam_init=lam_init)
    return pl.pallas_call(
        kern,
        grid=(nb, n_pairs, t_q // tq),
        in_specs=in_specs,
        out_specs=pl.BlockSpec((1, tq, LANES), lambda b, p, i: (b, i, p)),
        out_shape=jax.ShapeDtypeStruct((nb, t_q, n_pairs * LANES), BF16),
        scratch_shapes=[pltpu.VMEM((2, n_pan, 1, pw), F32),
                        pltpu.VMEM((2, n_pan, (LANES if mode == "diff" else HEAD_DIM) + BF16_ROWS, pw), F32),
                        pltpu.VMEM((2, n_pan, tk, pw), F32), pltpu.VMEM((2, n_pan, 1, pw), F32),
                        pltpu.VMEM((2, n_pan, tk, pw), F32), pltpu.VMEM((2, n_pan, 1, pw), F32)],
        compiler_params=_cparams(("parallel", "parallel", "arbitrary")),
        name="attn_" + mode,
    )(q, k, vt, *extra)


def _outproj_ln_kernel(*refs, n_in, alpha):
    x_ref = refs[0]
    o_refs = refs[1:1 + n_in]
    w_refs = refs[1 + n_in:1 + 2 * n_in]
    g_ref, b_ref, y_ref = refs[1 + 2 * n_in:]
    mix = jnp.dot(o_refs[0][...], w_refs[0][...], preferred_element_type=F32)
    for o_r, w_r in zip(o_refs[1:], w_refs[1:]):
        mix = mix + jnp.dot(o_r[...], w_r[...], preferred_element_type=F32)
    y_ref[...] = _layer_norm(alpha * x_ref[...] + mix, g_ref[...], b_ref[...])


def _outproj_ln(x, outs, ws, g, b, alpha):
    n, d = x.shape
    tm = min(ROW_TILE, n)
    assert n % tm == 0
    row = lambda width: pl.BlockSpec((tm, width), lambda i: (i, 0))
    full = lambda a: pl.BlockSpec(a.shape, lambda i: (0, 0))
    return pl.pallas_call(
        functools.partial(_outproj_ln_kernel, n_in=len(outs), alpha=alpha),
        grid=(n // tm,),
        in_specs=[row(d)] + [row(o.shape[1]) for o in outs] + [full(w) for w in ws] + [full(g), full(b)],
        out_specs=row(d),
        out_shape=jax.ShapeDtypeStruct((n, d), F32),
        compiler_params=_cparams(("parallel",)),
        name="outproj_ln",
    )(x, *outs, *ws, g, b)


def _route(logits):
    lane = lax.broadcasted_iota(jnp.int32, logits.shape, 1).astype(F32)
    big = float(1 << 20)
    is_g = lane < N_GROUPS
    lg = jnp.where(is_g, logits, NEG_INF)
    eg = jnp.where(is_g, jnp.exp(lg - jnp.max(lg, axis=1, keepdims=True)), 0.0)
    pg = eg / jnp.sum(eg, axis=1, keepdims=True)
    p_g = jnp.max(pg, axis=1, keepdims=True)
    gidx = jnp.min(jnp.where(is_g & (pg == p_g), lane, big), axis=1, keepdims=True)
    lo = GATE_LANE0 + EXPERTS_PER_GROUP * gidx
    sel = (lane >= lo) & (lane < lo + EXPERTS_PER_GROUP)
    le = jnp.where(sel, logits, NEG_INF)
    ee = jnp.where(sel, jnp.exp(le - jnp.max(le, axis=1, keepdims=True)), 0.0)
    pe = ee / jnp.sum(ee, axis=1, keepdims=True)
    v1 = jnp.max(jnp.where(sel, pe, -1.0), axis=1, keepdims=True)
    i1 = jnp.min(jnp.where(sel & (pe == v1), lane, big), axis=1, keepdims=True)
    rest = sel & (lane != i1)
    v2 = jnp.max(jnp.where(rest, pe, -1.0), axis=1, keepdims=True)
    i2 = jnp.min(jnp.where(rest & (pe == v2), lane, big), axis=1, keepdims=True)
    tot = v1 + v2
    w1 = v1 / tot * p_g
    w2 = v2 / tot * p_g
    return jnp.where(lane == i1, w1, jnp.where(lane == i2, w2, 0.0))


def _moe_ln_kernel(x_ref, wrh_ref, wrl_ref, br_ref, w13_ref, w2_ref, g_ref, b_ref, y_ref,
                   xb_sc, gate_sc, acc_sc, *, alpha):
    e = pl.program_id(1)

    @pl.when(e == 0)
    def _():
        x = x_ref[...]
        xh = x.astype(BF16)
        xl = (x - xh.astype(F32)).astype(BF16)
        xb_sc[...] = xh
        logits = (jnp.dot(xh, wrh_ref[...], preferred_element_type=F32)
                  + jnp.dot(xl, wrh_ref[...], preferred_element_type=F32)
                  + jnp.dot(xh, wrl_ref[...], preferred_element_type=F32) + br_ref[...])
        gate_sc[...] = _route(logits)
        acc_sc[...] = jnp.zeros_like(acc_sc)

    h = jnp.dot(xb_sc[...], w13_ref[0], preferred_element_type=F32)
    h1 = h[:, :D_EXPERT]
    h3 = h[:, D_EXPERT:]
    hdn = (h1 * jax.nn.sigmoid(h1)) * h3
    y = jnp.dot(hdn.astype(BF16), w2_ref[0], preferred_element_type=F32)
    lane = lax.broadcasted_iota(jnp.int32, (1, LANES), 1)
    ge = jnp.sum(jnp.where(lane == e + GATE_LANE0, gate_sc[...], 0.0), axis=1, keepdims=True)
    acc_sc[...] += ge * y

    @pl.when(e == pl.num_programs(1) - 1)
    def _():
        y_ref[...] = _layer_norm(alpha * x_ref[...] + acc_sc[...], g_ref[...], b_ref[...])


def _moe_ln(x, wrh, wrl, br, w13, w2, g, b, alpha):
    n, d = x.shape
    tm = min(MOE_TILE, n)
    assert n % tm == 0
    ne = w13.shape[0]
    full = lambda a: pl.BlockSpec(a.shape, lambda i, e: (0, 0))
    return pl.pallas_call(
        functools.partial(_moe_ln_kernel, alpha=alpha),
        grid=(n // tm, ne),
        in_specs=[pl.BlockSpec((tm, d), lambda i, e: (i, 0)), full(wrh), full(wrl), full(br),
                  pl.BlockSpec((1,) + w13.shape[1:], lambda i, e: (e, 0, 0)),
                  pl.BlockSpec((1,) + w2.shape[1:], lambda i, e: (e, 0, 0)),
                  full(g), full(b)],
        out_specs=pl.BlockSpec((tm, d), lambda i, e: (i, 0)),
        out_shape=jax.ShapeDtypeStruct((n, d), F32),
        scratch_shapes=[pltpu.VMEM((tm, d), BF16), pltpu.VMEM((tm, LANES), F32), pltpu.VMEM((tm, d), F32)],
        compiler_params=_cparams(("parallel", "arbitrary")),
        name="moe_ln",
    )(x, wrh, wrl, br, w13, w2, g, b)


def _proj_c_kernel(x_ref, win_ref, gq_ref, gkv_ref, wuq_ref, cq_ref, s1q_ref, s2q_ref,
                   ck_ref, s1k_ref, s2k_ref, q_ref, ckv_ref, kr_ref):
    xb = x_ref[0].astype(BF16)
    h = jnp.dot(xb, win_ref[...], preferred_element_type=F32)
    qa = h[:, :Q_RANK]
    kva = h[:, Q_RANK:Q_RANK + KV_RANK]
    krw = h[:, Q_RANK + KV_RANK:]
    qn = qa * lax.rsqrt(jnp.mean(qa * qa, axis=1, keepdims=True) + RMS_EPS) * gq_ref[...]
    ckv_ref[0] = kva * lax.rsqrt(jnp.mean(kva * kva, axis=1, keepdims=True) + RMS_EPS) * gkv_ref[...]
    half = ROPE_DIM // 2
    kr = _rope3(krw, ck_ref[...], s1k_ref[...], s2k_ref[...], LANES - half, half)
    kr_ref[0] = kr[:, :ROPE_DIM]
    q = jnp.dot(qn.astype(BF16), wuq_ref[...], preferred_element_type=F32)
    cq, s1q, s2q = cq_ref[...], s1q_ref[...], s2q_ref[...]
    scale = (NOPE_DIM + ROPE_DIM) ** -0.5 * LOG2E
    for hd in range(H_C):
        sl = slice(hd * LANES, (hd + 1) * LANES)
        q_ref[0, :, sl] = (_rope3(q[:, sl], cq, s1q, s2q, LANES - half, half) * scale).astype(BF16)


def _proj_c(x, win, gq, gkv, wuq, tabs_q, tabs_k):
    nb, t, _ = x.shape
    tm = min(ROW_TILE, t)
    assert t % tm == 0
    tok = lambda width: pl.BlockSpec((1, tm, width), lambda b, i: (b, i, 0))
    tab = pl.BlockSpec((tm, LANES), lambda b, i: (i, 0))
    full = lambda a: pl.BlockSpec(a.shape, lambda b, i: (0, 0))
    return pl.pallas_call(
        _proj_c_kernel,
        grid=(nb, t // tm),
        in_specs=[tok(D_MODEL), full(win), full(gq), full(gkv), full(wuq)] + [tab] * 6,
        out_specs=[tok(H_C * LANES), tok(KV_RANK), tok(ROPE_DIM)],
        out_shape=[jax.ShapeDtypeStruct((nb, t, H_C * LANES), BF16),
                   jax.ShapeDtypeStruct((nb, t, KV_RANK), F32),
                   jax.ShapeDtypeStruct((nb, t, ROPE_DIM), F32)],
        compiler_params=_cparams(("parallel", "parallel")),
        name="proj_c",
    )(x, win, gq, gkv, wuq, *tabs_q, *tabs_k)


def _kv_up_kernel(ckv_ref, kr_ref, wk_ref, place_ref, wvt_ref, k_ref, vt_ref):
    cb = ckv_ref[...].astype(BF16)
    k = (jnp.dot(cb, wk_ref[...], preferred_element_type=F32)
         + jnp.dot(kr_ref[...].astype(BF16), place_ref[...], preferred_element_type=F32))
    k_ref[0] = k.astype(BF16)
    vt = lax.dot_general(wvt_ref[...], cb, (((1,), (1,)), ((), ())), preferred_element_type=F32)
    vt_ref[0] = vt.astype(BF16)


def _kv_up(ckv, kr, wk, place, wvt):
    n = ckv.shape[0]
    tm = ATTN_BLOCK
    assert n % tm == 0
    row = lambda width: pl.BlockSpec((tm, width), lambda i: (i, 0))
    full = lambda a: pl.BlockSpec(a.shape, lambda i: (0, 0))
    return pl.pallas_call(
        _kv_up_kernel,
        grid=(n // tm,),
        in_specs=[row(KV_RANK), row(ROPE_DIM), full(wk), full(place), full(wvt)],
        out_specs=[pl.BlockSpec((1, tm, H_C * LANES), lambda i: (i, 0, 0)),
                   pl.BlockSpec((1, H_C * V_DIM_C, tm), lambda i: (i, 0, 0))],
        out_shape=[jax.ShapeDtypeStruct((n // tm, tm, H_C * LANES), BF16),
                   jax.ShapeDtypeStruct((n // tm, H_C * V_DIM_C, tm), BF16)],
        compiler_params=_cparams(("parallel",)),
        name="kv_up",
    )(ckv, kr, wk, place, wvt)


def _rope_tables(pos, dim, lane0):
    half = dim // 2
    inv = ROPE_THETA ** (-jnp.arange(0, dim, 2, dtype=F32) / dim)
    ang = pos.astype(F32)[:, None] * inv[None, :]
    cos, sin = jnp.cos(ang), jnp.sin(ang)
    zero = jnp.zeros_like(sin)
    c = jnp.concatenate([cos, cos], axis=1)
    s1 = jnp.concatenate([-sin, zero], axis=1)
    s2 = jnp.concatenate([zero, sin], axis=1)
    if lane0 < 0:
        reps = LANES // dim
        return tuple(jnp.tile(a, (1, reps)) for a in (c, s1, s2))
    t = pos.shape[0]
    pad = lambda a, fill: jnp.concatenate(
        [jnp.full((t, lane0), fill, F32), a, jnp.full((t, LANES - lane0 - dim), fill, F32)], axis=1)
    return pad(c, 1.0), pad(s1, 0.0), pad(s2, 0.0)


def _pad_cols(a, width):
    return jnp.pad(a, ((0, 0), (0, width - a.shape[1])))


def _blocks(a, tk):
    nb, t, l = a.shape
    return a.reshape(nb, t // tk, tk, l)


def _vt_blocks(v, tk):
    nb, t, l = v.shape
    return v.reshape(nb, t // tk, tk, l).transpose(0, 1, 3, 2)


def _pad_rows(q, rows):
    return jnp.pad(q, ((0, 0), (0, rows - q.shape[1]), (0, 0)))


def _cat_pad_time(cache, new, t_pad):
    nb, t0, l = cache.shape
    t1 = new.shape[1]
    return jnp.concatenate([cache, new, jnp.zeros((nb, t_pad - t0 - t1, l), cache.dtype)], axis=1)


def kernel(x_prompt, x_sample, cache_fox_k, cache_fox_v, cache_fox_logf, cache_diff_k, cache_diff_v, cache_mla_ckv, cache_mla_krope, w_in_ab, b_fgate, diff_lq1, diff_lk1, diff_lq2, diff_lk2, diff_subln, w_out_ab, w_in_c, mla_q_norm, mla_kv_norm, mla_w_uq, mla_w_ukv, w_out_c, ln1_g, ln1_b, ln2_g, ln2_b, moe_wg, moe_bg, moe_we, moe_be, moe_w1, moe_w3, moe_w2):
    bp, tp, d = x_prompt.shape
    bs, ts, _ = x_sample.shape
    past = cache_fox_k.shape[2]
    depth = ln1_g.shape[0]
    alpha = (2 * depth) ** 0.25
    tk = ATTN_BLOCK
    assert past % tk == 0 and ts == 16 and past % CHUNK == 0
    ns = bs * ts
    t_dec = past + tk
    nfull_dec = past // tk
    dec_shift = 4

    pos_p = jnp.arange(tp)
    pos_s = jnp.tile(past + jnp.arange(ts), bs)

    xp = x_prompt
    xs = x_sample.reshape(1, ns, d)
    out_ab_p, out_ab_s, out_c_p, out_c_s = [], [], [], []

    for i in range(depth):
        j = i // 2
        if i % 2 == 0:
            lam_init = 0.8 - 0.6 * math.exp(-0.3 * i)
            cuts = [0, A_WIDTH, 2 * A_WIDTH, 3 * A_WIDTH, 3 * A_WIDTH + H_A,
                    3 * A_WIDTH + H_A + B_QK_WIDTH, 3 * A_WIDTH + H_A + 2 * B_QK_WIDTH,
                    3 * A_WIDTH + H_A + 2 * B_QK_WIDTH + B_V_WIDTH]
            w = w_in_ab[j]
            piece = lambda a: w[:, cuts[a]:cuts[a + 1]]
            w6 = jnp.stack([piece(0), piece(1), piece(2), piece(4), piece(5), piece(6)]).astype(BF16)
            wvt = jnp.stack([piece(2).T, piece(6).T]).astype(BF16)
            wf = _pad_cols(piece(3), LANES).astype(BF16)
            bf = _pad_cols(b_fgate[j][None, :], LANES)
            wout = w_out_ab[j].astype(BF16)
            diff_extra = (diff_lq1[j][None, :], diff_lk1[j][None, :], diff_lq2[j][None, :],
                          diff_lk2[j][None, :], diff_subln[j][None, :])

            tabs = _rope_tables(pos_p, HEAD_DIM, -1)
            (qa, ka, kab, va, vat, lf, lfw, qb, kb, kbb, vb, vbt) = _proj_ab(xp, w6, wvt, wf, bf, tabs)
            bias = _blocks(_decay_bias(lfw), tk)
            oa = _attention("fox", qa, _blocks(kab, tk), vat, (bias,), n_pairs=H_A // 2, mask_shift=0)
            ob = _attention("diff", qb, _blocks(kbb, tk), vbt, diff_extra,
                            n_pairs=H_B, mask_shift=int(math.log2(CHUNK)), lam_init=lam_init)
            out_ab_p.append((ka.reshape(bp, tp, H_A, HEAD_DIM), va.reshape(bp, tp, H_A, HEAD_DIM), lf,
                             kb.reshape(bp, tp, H_B, 2, HEAD_DIM), vb.reshape(bp, tp, H_B, 2 * HEAD_DIM)))
            xp2 = _outproj_ln(xp.reshape(bp * tp, d), [oa.reshape(bp * tp, -1), ob.reshape(bp * tp, -1)],
                              [wout[:A_WIDTH], wout[A_WIDTH:]], ln1_g[i][None, :], ln1_b[i][None, :], alpha)

            tabs = _rope_tables(pos_s, HEAD_DIM, -1)
            (qa, ka, kab, va, _, lf, lfw, qb, kb, kbb, vb, _) = _proj_ab(xs, w6, wvt, wf, bf, tabs)
            rs = lambda a: a.reshape(bs, ts, a.shape[-1])
            cache_lfw = jnp.pad(cache_fox_logf[j].astype(F32), ((0, 0), (0, 0), (0, LANES - H_A)))
            bias = _blocks(_decay_bias(_cat_pad_time(cache_lfw, rs(lfw), t_dec)), tk)
            flat = lambda c: c.reshape(bs, past, -1).astype(BF16)
            qdec = lambda a: _pad_rows(rs(a), DEC_Q_ROWS)
            k_all = _blocks(_cat_pad_time(flat(cache_fox_k[j]), rs(kab), t_dec), tk)
            v_all = _vt_blocks(_cat_pad_time(flat(cache_fox_v[j]), rs(va).astype(BF16), t_dec), tk)
            oa = _attention("fox", qdec(qa), k_all, v_all, (bias,), n_pairs=H_A // 2, mask_shift=0,
                            nfull_static=nfull_dec)[:, :ts]
            k_all = _blocks(_cat_pad_time(flat(cache_diff_k[j]), rs(kbb), t_dec), tk)
            v_all = _vt_blocks(_cat_pad_time(flat(cache_diff_v[j]), rs(vb).astype(BF16), t_dec), tk)
            ob = _attention("diff", qdec(qb), k_all, v_all, diff_extra, n_pairs=H_B, mask_shift=dec_shift,
                            nfull_static=nfull_dec, lam_init=lam_init)[:, :ts]
            out_ab_s.append((ka.reshape(bs, ts, H_A, HEAD_DIM), va.reshape(bs, ts, H_A, HEAD_DIM),
                             lf.reshape(bs, ts, H_A), kb.reshape(bs, ts, H_B, 2, HEAD_DIM),
                             vb.reshape(bs, ts, H_B, 2 * HEAD_DIM)))
            xs2 = _outproj_ln(xs.reshape(ns, d), [oa.reshape(ns, -1), ob.reshape(ns, -1)],
                              [wout[:A_WIDTH], wout[A_WIDTH:]], ln1_g[i][None, :], ln1_b[i][None, :], alpha)
        else:
            wc = w_in_c[j]
            kr_cols = _pad_cols(wc[:, Q_RANK + KV_RANK:], LANES)
            win = jnp.concatenate([wc[:, :Q_RANK + KV_RANK], kr_cols], axis=1).astype(BF16)
            wuq = jnp.pad(mla_w_uq[j].reshape(Q_RANK, H_C, NOPE_DIM + ROPE_DIM),
                          ((0, 0), (0, 0), (0, LANES - NOPE_DIM - ROPE_DIM))).reshape(Q_RANK, H_C * LANES)
            wuq = wuq.astype(BF16)
            wukv = mla_w_ukv[j].reshape(KV_RANK, H_C, NOPE_DIM + V_DIM_C)
            wk = jnp.pad(wukv[:, :, :NOPE_DIM], ((0, 0), (0, 0), (0, LANES - NOPE_DIM)))
            wk = wk.reshape(KV_RANK, H_C * LANES).astype(BF16)
            wvt = wukv[:, :, NOPE_DIM:].reshape(KV_RANK, H_C * V_DIM_C).T.astype(BF16)
            place = jnp.tile(_pad_cols(jnp.concatenate(
                [jnp.zeros((ROPE_DIM, NOPE_DIM), F32), jnp.eye(ROPE_DIM, dtype=F32)], axis=1), LANES),
                (1, H_C)).astype(BF16)
            gq = mla_q_norm[j][None, :]
            gkv = mla_kv_norm[j][None, :]
            wout = w_out_c[j].astype(BF16)

            q, ckv, kr = _proj_c(xp, win, gq, gkv, wuq, _rope_tables(pos_p, ROPE_DIM, NOPE_DIM),
                                 _rope_tables(pos_p, ROPE_DIM, 0))
            kc, vct = _kv_up(ckv.reshape(bp * tp, KV_RANK), kr.reshape(bp * tp, ROPE_DIM), wk, place, wvt)
            per_seq = lambda a, nb: a.reshape((nb, a.shape[0] // nb) + a.shape[1:])
            oc = _attention("mla", q, per_seq(kc, bp), per_seq(vct, bp), (), n_pairs=H_C // 2,
                            mask_shift=int(math.log2(CHUNK)))
            out_c_p.append((ckv, kr))
            xp2 = _outproj_ln(xp.reshape(bp * tp, d), [oc.reshape(bp * tp, -1)], [wout],
                              ln1_g[i][None, :], ln1_b[i][None, :], alpha)

            q, ckv, kr = _proj_c(xs, win, gq, gkv, wuq, _rope_tables(pos_s, ROPE_DIM, NOPE_DIM),
                                 _rope_tables(pos_s, ROPE_DIM, 0))
            ckv_all = _cat_pad_time(cache_mla_ckv[j].astype(F32), ckv.reshape(bs, ts, KV_RANK), t_dec)
            kr_all = _cat_pad_time(cache_mla_krope[j].astype(F32), kr.reshape(bs, ts, ROPE_DIM), t_dec)
            kc, vct = _kv_up(ckv_all.reshape(bs * t_dec, KV_RANK), kr_all.reshape(bs * t_dec, ROPE_DIM),
                             wk, place, wvt)
            oc = _attention("mla", _pad_rows(q.reshape(bs, ts, -1), DEC_Q_ROWS), per_seq(kc, bs),
                            per_seq(vct, bs), (), n_pairs=H_C // 2, mask_shift=dec_shift,
                            nfull_static=nfull_dec)[:, :ts]
            out_c_s.append((ckv.reshape(bs, ts, KV_RANK), kr.reshape(bs, ts, ROPE_DIM)))
            xs2 = _outproj_ln(xs.reshape(ns, d), [oc.reshape(ns, -1)], [wout],
                              ln1_g[i][None, :], ln1_b[i][None, :], alpha)

        wr = _pad_cols(jnp.concatenate(
            [moe_wg[i]] + [moe_we[i][gi] for gi in range(N_GROUPS)], axis=1), LANES)
        wrh = wr.astype(BF16)
        wrl = (wr - wrh.astype(F32)).astype(BF16)
        br = _pad_cols(jnp.concatenate([moe_bg[i], moe_be[i].reshape(-1)])[None, :], LANES)
        w13 = jnp.concatenate([moe_w1[i], moe_w3[i]], axis=2).astype(BF16)
        w2 = moe_w2[i].astype(BF16)
        g2, b2 = ln2_g[i][None, :], ln2_b[i][None, :]
        xp = _moe_ln(xp2, wrh, wrl, br, w13, w2, g2, b2, alpha).reshape(bp, tp, d)
        xs = _moe_ln(xs2, wrh, wrl, br, w13, w2, g2, b2, alpha).reshape(1, ns, d)

    stack = lambda rows, n: jnp.stack([r[n] for r in rows])
    return (xp, xs.reshape(bs, ts, d),
            stack(out_ab_p, 0), stack(out_ab_p, 1), stack(out_ab_p, 2), stack(out_ab_p, 3), stack(out_ab_p, 4),
            stack(out_c_p, 0), stack(out_c_p, 1),
            stack(out_ab_s, 0), stack(out_ab_s, 1), stack(out_ab_s, 2), stack(out_ab_s, 3), stack(out_ab_s, 4),
            stack(out_c_s, 0), stack(out_c_s, 1))
```

```python
import functools
import math

import jax
import jax.numpy as jnp
from jax import lax
from jax.experimental import pallas as pl
from jax.experimental.pallas import tpu as pltpu

F32 = jnp.float32
BF16 = jnp.bfloat16

D_MODEL = 1024
CHUNK = 64
HEAD_DIM = 64
ROPE_THETA = 10000.0
H_A = 8
H_B = 4
H_C = 16
Q_RANK = 256
KV_RANK = 128
NOPE_DIM = 64
ROPE_DIM = 32
V_DIM_C = 64
N_GROUPS = 4
EXPERTS_PER_GROUP = 4
N_EXPERTS = N_GROUPS * EXPERTS_PER_GROUP
D_EXPERT = 256
A_WIDTH = H_A * HEAD_DIM
B_QK_WIDTH = H_B * 2 * HEAD_DIM
B_V_WIDTH = H_B * 2 * HEAD_DIM
FGATE_BIAS = 3.0
LN_EPS = 1e-5
RMS_EPS = 1e-6
NEG_INF = -1e30
LOG2E = math.log2(math.e)

LANES = 128
BF16_ROWS = 16
PANEL = {"fox": 1024, "diff": 256, "mla": 1024}
BIAS_PIECES = 3
DEC_Q_ROWS = 128
VMEM_LIMIT = 48 * 1024 * 1024
ATTN_BLOCK = 512
ROW_TILE = 512
MOE_TILE = 1024
GATE_LANE0 = N_GROUPS


def _cparams(sem):
    return pltpu.CompilerParams(dimension_semantics=sem, vmem_limit_bytes=VMEM_LIMIT)


def _rope3(x, c, s1, s2, shift_up, shift_down):
    return x * c + pltpu.roll(x, shift_up, 1) * s1 + pltpu.roll(x, shift_down, 1) * s2


def _layer_norm(y, g, b):
    mu = jnp.mean(y, axis=-1, keepdims=True)
    d = y - mu
    var = jnp.mean(d * d, axis=-1, keepdims=True)
    return d * lax.rsqrt(var + LN_EPS) * g + b


def _split3(x):
    hi = x.astype(BF16)
    r1 = x - hi.astype(F32)
    mid = r1.astype(BF16)
    return hi, mid, (r1 - mid.astype(F32)).astype(BF16)


def _proj_ab_kernel(x_ref, w_ref, wvt_ref, wf_ref, bf_ref, c_ref, s1_ref, s2_ref,
                    qa_ref, ka_ref, kab_ref, va_ref, vat_ref, lf_ref, lfw_ref,
                    qb_ref, kb_ref, kbb_ref, vb_ref, vbt_ref):
    xb = x_ref[0].astype(BF16)

    def mm(i):
        return jnp.dot(xb, w_ref[i], preferred_element_type=F32)

    def mm_t(i):
        return lax.dot_general(wvt_ref[i], xb, (((1,), (1,)), ((), ())), preferred_element_type=F32)

    qa_ref[0] = (mm(0) * (HEAD_DIM ** -0.5 * LOG2E)).astype(BF16)
    ka = mm(1)
    ka_ref[0] = ka
    kab_ref[0] = ka.astype(BF16)
    va_ref[0] = mm(2)
    vat_ref[0, 0] = mm_t(0).astype(BF16)

    z = jnp.dot(xb, wf_ref[...], preferred_element_type=F32) + bf_ref[...]
    lf = jnp.minimum(z, 0.0) - jnp.log1p(jnp.exp(-jnp.abs(z)))
    lf_ref[0] = lf[:, :H_A]
    lfw_ref[0] = lf

    c, s1, s2 = c_ref[...], s1_ref[...], s2_ref[...]
    qb = mm(3)
    kb = mm(4)
    for s in range(B_QK_WIDTH // LANES):
        sl = slice(s * LANES, (s + 1) * LANES)
        qs = _rope3(qb[:, sl], c, s1, s2, LANES - HEAD_DIM // 2, HEAD_DIM // 2)
        qb_ref[0, :, sl] = (qs * (HEAD_DIM ** -0.5 * LOG2E)).astype(BF16)
        ks = _rope3(kb[:, sl], c, s1, s2, LANES - HEAD_DIM // 2, HEAD_DIM // 2)
        kb_ref[0, :, sl] = ks
        kbb_ref[0, :, sl] = ks.astype(BF16)
    vb_ref[0] = mm(5)
    vbt_ref[0, 0] = mm_t(1).astype(BF16)


def _proj_ab(x, w6, wvt, wf, bf, tabs):
    nb, t, _ = x.shape
    tm = min(ROW_TILE, t)
    assert t % tm == 0
    w = A_WIDTH
    tok = lambda width: pl.BlockSpec((1, tm, width), lambda b, i: (b, i, 0))
    tr = pl.BlockSpec((1, 1, w, tm), lambda b, i: (b, i, 0, 0))
    tab = pl.BlockSpec((tm, LANES), lambda b, i: (i, 0))
    full = lambda a: pl.BlockSpec(a.shape, lambda b, i: (0,) * a.ndim)
    sds = lambda width, dt: jax.ShapeDtypeStruct((nb, t, width), dt)
    sds_t = jax.ShapeDtypeStruct((nb, t // tm, w, tm), BF16)
    return pl.pallas_call(
        _proj_ab_kernel,
        grid=(nb, t // tm),
        in_specs=[tok(D_MODEL), full(w6), full(wvt), full(wf), full(bf), tab, tab, tab],
        out_specs=[tok(w), tok(w), tok(w), tok(w), tr, tok(H_A), tok(LANES), tok(w), tok(w), tok(w), tok(w), tr],
        out_shape=[sds(w, BF16), sds(w, F32), sds(w, BF16), sds(w, F32), sds_t, sds(H_A, F32), sds(LANES, F32),
                   sds(w, BF16), sds(w, F32), sds(w, BF16), sds(w, F32), sds_t],
        compiler_params=_cparams(("parallel", "parallel")),
        name="proj_ab",
    )(x, w6, wvt, wf, bf, *tabs)


def _decay_bias_kernel(lf_ref, spread_ref, lower_ref, o_ref, carry_ref):
    @pl.when(pl.program_id(1) == 0)
    def _():
        carry_ref[...] = jnp.zeros_like(carry_ref)

    x = lf_ref[0]
    tc = x.shape[0]
    spread = spread_ref[...]
    xr = sum(jnp.dot(p, spread, preferred_element_type=F32) for p in _split3(x))
    lower = lower_ref[...]
    c = sum(jnp.dot(lower, p, preferred_element_type=F32) for p in _split3(xr)) + carry_ref[...]
    carry_ref[...] = c[tc - 1:tc, :]
    hi, mid, lo = (p.astype(F32) for p in _split3(c * (-LOG2E)))
    lane = lax.broadcasted_iota(jnp.int32, (1, LANES), 1).astype(F32)
    piece = lane - BIAS_PIECES * jnp.floor((lane + 0.5) * (1.0 / BIAS_PIECES))
    o_ref[0] = jnp.where(piece == 0.0, hi, jnp.where(piece == 1.0, mid, lo)).astype(BF16)


def _decay_bias(lf_wide):
    nb, t, _ = lf_wide.shape
    tc = min(ATTN_BLOCK, t)
    assert t % tc == 0
    spec = pl.BlockSpec((1, tc, LANES), lambda b, i: (b, i, 0))
    src = jnp.arange(LANES)[:, None]
    dst = jnp.arange(LANES)[None, :]
    spread = ((dst // BIAS_PIECES == src) & (src < H_A)).astype(BF16)
    lower = jnp.tril(jnp.ones((tc, tc), BF16))
    const = lambda a: pl.BlockSpec(a.shape, lambda b, i: (0, 0))
    return pl.pallas_call(
        _decay_bias_kernel,
        grid=(nb, t // tc),
        in_specs=[spec, const(spread), const(lower)],
        out_specs=spec,
        out_shape=jax.ShapeDtypeStruct((nb, t, LANES), BF16),
        scratch_shapes=[pltpu.VMEM((1, LANES), F32)],
        compiler_params=_cparams(("parallel", "arbitrary")),
        name="cumsum",
    )(lf_wide, spread, lower)


def _attn_kernel(*refs, mode, tq, tk, mask_shift, nfull_static, n_diag, lam_init):
    if mode == "diff":
        q_ref, k_ref, vt_ref, lq1_ref, lk1_ref, lq2_ref, lk2_ref, sub_ref, o_ref = refs[:9]
    elif mode == "fox":
        q_ref, k_ref, vt_ref, b_ref, o_ref = refs[:5]
    else:
        q_ref, k_ref, vt_ref, o_ref = refs[:4]
    m_sc, acc_sc, sa, bma, sb, bmb = refs[-6:]
    v_rows = LANES if mode == "diff" else HEAD_DIM
    sa_sc, sb_sc = (sa, bma), (sb, bmb)

    qi = pl.program_id(2)
    q = q_ref[0]
    lane = lax.broadcasted_iota(jnp.int32, (1, LANES), 1)
    if mode == "mla":
        qs = [q[:, :LANES], q[:, LANES:]]
    else:
        zero = jnp.zeros_like(q)
        qs = [jnp.where(lane < HEAD_DIM, q, zero), jnp.where(lane >= HEAD_DIM, q, zero)]
        if mode == "fox":
            def pick(i):
                lo = BIAS_PIECES * (2 * pl.program_id(1) + i)
                hot = jnp.where((lane >= lo) & (lane < lo + BIAS_PIECES), 1.0, 0.0)
                return jnp.broadcast_to(hot, (tq, LANES)).astype(BF16)

            qs = [jnp.concatenate([qs[i], pick(i)], axis=1) for i in range(2)]

    m_sc[...] = jnp.full(m_sc.shape, NEG_INF, F32)
    acc_sc[...] = jnp.zeros(acc_sc.shape, F32)

    pw = min(PANEL[mode], tq)

    def scores(j, bufs, c):
        s_sc, bm_sc = bufs
        cs = slice(c * pw, (c + 1) * pw)
        k = k_ref[0, j]
        if mode == "fox":
            k = jnp.concatenate([k, b_ref[0, j]], axis=1)
        for i in range(2):
            ki = k[:, i * LANES:(i + 1) * LANES] if mode == "mla" else k
            st = lax.dot_general(ki, qs[i][cs], (((1,), (1,)), ((), ())), preferred_element_type=F32)
            s_sc[i, c] = st
            bm_sc[i, c] = jnp.max(st, axis=0, keepdims=True)

    def consume(j, bufs, c, diag):
        s_sc, bm_sc = bufs
        cs = slice(c * pw, (c + 1) * pw)
        vt = vt_ref[0, j]
        for i in range(2):
            st = s_sc[i, c]
            if diag is not None:
                key = lax.broadcasted_iota(jnp.int32, (tk, pw), 0) + diag * tk
                qry = lax.broadcasted_iota(jnp.int32, (tk, pw), 1) + c * pw
                vis = lax.shift_right_logical(key, mask_shift) <= lax.shift_right_logical(qry, mask_shift)
                st = jnp.where(vis, st, NEG_INF)
                blk_max = jnp.max(st, axis=0, keepdims=True)
            else:
                blk_max = bm_sc[i, c]
            m_prev = m_sc[i, c]
            m_new = jnp.maximum(m_prev, blk_max)
            alpha = jnp.exp2(m_prev - m_new)
            p = jnp.exp2(st - m_new).astype(BF16)
            vi = vt if mode == "diff" else vt[i * HEAD_DIM:(i + 1) * HEAD_DIM]
            vi = jnp.concatenate([vi, jnp.ones((BF16_ROWS, tk), BF16)], axis=0)
            acc_sc[i, c] = alpha * acc_sc[i, c] + jnp.dot(vi, p, preferred_element_type=F32)
            m_sc[i, c] = m_new

    def stage(nxt, cur, diag=None):
        for c in range(tq // pw):
            if nxt is not None:
                scores(nxt[0], nxt[1], c)
            consume(cur[0], cur[1], c, diag)

    n_pairs_full = qi * (n_diag // 2) if nfull_static is None else nfull_static // 2
    nfull = 2 * n_pairs_full

    def pair(jj, carry):
        stage((2 * jj + 1, sb_sc), (2 * jj, sa_sc))
        stage((2 * jj + 2, sa_sc), (2 * jj + 1, sb_sc))
        return carry

    for c in range(tq // pw):
        scores(0, sa_sc, c)
    lax.fori_loop(0, n_pairs_full, pair, 0)
    if n_diag == 2:
        stage((nfull + 1, sb_sc), (nfull, sa_sc), diag=0)
        stage(None, (nfull + 1, sb_sc), diag=1)
    else:
        stage(None, (nfull, sa_sc), diag=0)

    if mode == "diff":
        lam = (jnp.exp(jnp.sum(lq1_ref[...] * lk1_ref[...], axis=1, keepdims=True))
               - jnp.exp(jnp.sum(lq2_ref[...] * lk2_ref[...], axis=1, keepdims=True)) + lam_init)
    for c in range(tq // pw):
        o0 = acc_sc[0, c, :v_rows] / acc_sc[0, c, v_rows:v_rows + 1]
        o1 = acc_sc[1, c, :v_rows] / acc_sc[1, c, v_rows:v_rows + 1]
        if mode == "diff":
            o = o0 - lam * o1
            ms = jnp.mean(o * o, axis=0, keepdims=True)
            o = (o * lax.rsqrt(ms + RMS_EPS)).T * sub_ref[...] * (1.0 - lam_init)
        else:
            o = jnp.concatenate([o0, o1], axis=0).T
        o_ref[0, c * pw:(c + 1) * pw, :] = o.astype(o_ref.dtype)


def _attention(mode, q, k, vt, extra, *, n_pairs, mask_shift, nfull_static=None, lam_init=0.0):
    nb, t_q, _ = q.shape
    _, nkb, tk, _ = k.shape
    if nfull_static is None:
        tq, n_diag = 2 * tk, 2
        assert t_q % tq == 0 and nkb == t_q // tk
    else:
        tq, n_diag = t_q, 1
        assert nfull_static % 2 == 0 and nkb == nfull_static + 1
    pw = min(PANEL[mode], tq)
    n_pan = tq // pw
    qw = 2 * LANES if mode == "mla" else LANES
    in_specs = [
        pl.BlockSpec((1, tq, qw), lambda b, p, i: (b, i, p)),
        pl.BlockSpec((1, nkb, tk, qw), lambda b, p, i: (b, 0, 0, p)),
        pl.BlockSpec((1, nkb, LANES, tk), lambda b, p, i: (b, 0, p, 0)),
    ]
    if mode == "fox":
        in_specs.append(pl.BlockSpec((1, nkb, tk, LANES), lambda b, p, i: (b, 0, 0, 0)))
    elif mode == "diff":
        in_specs += [pl.BlockSpec(a.shape, lambda b, p, i: (0, 0)) for a in extra]
    kern = functools.partial(_attn_kernel, mode=mode, tq=tq, tk=tk, mask_shift=mask_shift,
                             nfull_static=nfull_static, n_diag=n_diag, lam_init=lam_init)
    return pl.pallas_call(
        kern,
        grid=(nb, n_pairs, t_q // tq),
        in_specs=in_specs,
        out_specs=pl.BlockSpec((1, tq, LANES), lambda b, p, i: (b, i, p)),
        out_shape=jax.ShapeDtypeStruct((nb, t_q, n_pairs * LANES), BF16),
        scratch_shapes=[pltpu.VMEM((2, n_pan, 1, pw), F32),
                        pltpu.VMEM((2, n_pan, (LANES if mode == "diff" else HEAD_DIM) + BF16_ROWS, pw), F32),
                        pltpu.VMEM((2, n_pan, tk, pw), F32), pltpu.VMEM((2, n_pan, 1, pw), F32),
                        pltpu.VMEM((2, n_pan, tk, pw), F32), pltpu.VMEM((2, n_pan, 1, pw), F32)],
        compiler_params=_cparams(("parallel", "parallel", "arbitrary")),
        name="attn_" + mode,
    )(q, k, vt, *extra)


def _outproj_ln_kernel(*refs, n_in, alpha):
    x_ref = refs[0]
    o_refs = refs[1:1 + n_in]
    w_refs = refs[1 + n_in:1 + 2 * n_in]
    g_ref, b_ref, y_ref = refs[1 + 2 * n_in:]
    mix = jnp.dot(o_refs[0][...], w_refs[0][...], preferred_element_type=F32)
    for o_r, w_r in zip(o_refs[1:], w_refs[1:]):
        mix = mix + jnp.dot(o_r[...], w_r[...], preferred_element_type=F32)
    y_ref[...] = _layer_norm(alpha * x_ref[...] + mix, g_ref[...], b_ref[...])


def _outproj_ln(x, outs, ws, g, b, alpha):
    n, d = x.shape
    tm = min(ROW_TILE, n)
    assert n % tm == 0
    row = lambda width: pl.BlockSpec((tm, width), lambda i: (i, 0))
    full = lambda a: pl.BlockSpec(a.shape, lambda i: (0, 0))
    return pl.pallas_call(
        functools.partial(_outproj_ln_kernel, n_in=len(outs), alpha=alpha),
        grid=(n // tm,),
        in_specs=[row(d)] + [row(o.shape[1]) for o in outs] + [full(w) for w in ws] + [full(g), full(b)],
        out_specs=row(d),
        out_shape=jax.ShapeDtypeStruct((n, d), F32),
        compiler_params=_cparams(("parallel",)),
        name="outproj_ln",
    )(x, *outs, *ws, g, b)


def _route(logits):
    lane = lax.broadcasted_iota(jnp.int32, logits.shape, 1).astype(F32)
    big = float(1 << 20)
    is_g = lane < N_GROUPS
    lg = jnp.where(is_g, logits, NEG_INF)
    eg = jnp.where(is_g, jnp.exp(lg - jnp.max(lg, axis=1, keepdims=True)), 0.0)
    pg = eg / jnp.sum(eg, axis=1, keepdims=True)
    p_g = jnp.max(pg, axis=1, keepdims=True)
    gidx = jnp.min(jnp.where(is_g & (pg == p_g), lane, big), axis=1, keepdims=True)
    lo = GATE_LANE0 + EXPERTS_PER_GROUP * gidx
    sel = (lane >= lo) & (lane < lo + EXPERTS_PER_GROUP)
    le = jnp.where(sel, logits, NEG_INF)
    ee = jnp.where(sel, jnp.exp(le - jnp.max(le, axis=1, keepdims=True)), 0.0)
    pe = ee / jnp.sum(ee, axis=1, keepdims=True)
    v1 = jnp.max(jnp.where(sel, pe, -1.0), axis=1, keepdims=True)
    i1 = jnp.min(jnp.where(sel & (pe == v1), lane, big), axis=1, keepdims=True)
    rest = sel & (lane != i1)
    v2 = jnp.max(jnp.where(rest, pe, -1.0), axis=1, keepdims=True)
    i2 = jnp.min(jnp.where(rest & (pe == v2), lane, big), axis=1, keepdims=True)
    tot = v1 + v2
    w1 = v1 / tot * p_g
    w2 = v2 / tot * p_g
    return jnp.where(lane == i1, w1, jnp.where(lane == i2, w2, 0.0))


def _moe_ln_kernel(x_ref, wrh_ref, wrl_ref, br_ref, w1_ref, w3_ref, w2_ref, g_ref, b_ref, y_ref,
                   xb_sc, gate_sc, acc_sc, *, alpha):
    e = pl.program_id(1)

    @pl.when(e == 0)
    def _():
        x = x_ref[...]
        xh = x.astype(BF16)
        xl = (x - xh.astype(F32)).astype(BF16)
        xb_sc[...] = xh
        logits = (jnp.dot(xh, wrh_ref[...], preferred_element_type=F32)
                  + jnp.dot(xl, wrh_ref[...], preferred_element_type=F32)
                  + jnp.dot(xh, wrl_ref[...], preferred_element_type=F32) + br_ref[...])
        gate_sc[...] = _route(logits)
        acc_sc[...] = jnp.zeros_like(acc_sc)

    xb = xb_sc[...]
    h1 = jnp.dot(xb, w1_ref[0].astype(BF16), preferred_element_type=F32)
    h3 = jnp.dot(xb, w3_ref[0].astype(BF16), preferred_element_type=F32)
    hdn = (h1 * jax.nn.sigmoid(h1)) * h3
    y = jnp.dot(hdn.astype(BF16), w2_ref[0].astype(BF16), preferred_element_type=F32)
    lane = lax.broadcasted_iota(jnp.int32, (1, LANES), 1)
    ge = jnp.sum(jnp.where(lane == e + GATE_LANE0, gate_sc[...], 0.0), axis=1, keepdims=True)
    acc_sc[...] += ge * y

    @pl.when(e == pl.num_programs(1) - 1)
    def _():
        y_ref[...] = _layer_norm(alpha * x_ref[...] + acc_sc[...], g_ref[...], b_ref[...])


def _moe_ln(x, wrh, wrl, br, w1, w3, w2, g, b, alpha):
    n, d = x.shape
    tm = min(MOE_TILE, n)
    assert n % tm == 0
    ne = w1.shape[0]
    per_expert = lambda a: pl.BlockSpec((1,) + a.shape[1:], lambda i, e: (e, 0, 0))
    full = lambda a: pl.BlockSpec(a.shape, lambda i, e: (0, 0))
    return pl.pallas_call(
        functools.partial(_moe_ln_kernel, alpha=alpha),
        grid=(n // tm, ne),
        in_specs=[pl.BlockSpec((tm, d), lambda i, e: (i, 0)), full(wrh), full(wrl), full(br),
                  per_expert(w1), per_expert(w3), per_expert(w2),
                  full(g), full(b)],
        out_specs=pl.BlockSpec((tm, d), lambda i, e: (i, 0)),
        out_shape=jax.ShapeDtypeStruct((n, d), F32),
        scratch_shapes=[pltpu.VMEM((tm, d), BF16), pltpu.VMEM((tm, LANES), F32), pltpu.VMEM((tm, d), F32)],
        compiler_params=_cparams(("parallel", "arbitrary")),
        name="moe_ln",
    )(x, wrh, wrl, br, w1, w3, w2, g, b)


def _proj_c_kernel(x_ref, win_ref, gq_ref, gkv_ref, wuq_ref, cq_ref, s1q_ref, s2q_ref,
                   ck_ref, s1k_ref, s2k_ref, q_ref, ckv_ref, kr_ref):
    xb = x_ref[0].astype(BF16)
    h = jnp.dot(xb, win_ref[...], preferred_element_type=F32)
    qa = h[:, :Q_RANK]
    kva = h[:, Q_RANK:Q_RANK + KV_RANK]
    krw = h[:, Q_RANK + KV_RANK:]
    qn = qa * lax.rsqrt(jnp.mean(qa * qa, axis=1, keepdims=True) + RMS_EPS) * gq_ref[...]
    ckv_ref[0] = kva * lax.rsqrt(jnp.mean(kva * kva, axis=1, keepdims=True) + RMS_EPS) * gkv_ref[...]
    half = ROPE_DIM // 2
    kr = _rope3(krw, ck_ref[...], s1k_ref[...], s2k_ref[...], LANES - half, half)
    kr_ref[0] = kr[:, :ROPE_DIM]
    q = jnp.dot(qn.astype(BF16), wuq_ref[...], preferred_element_type=F32)
    cq, s1q, s2q = cq_ref[...], s1q_ref[...], s2q_ref[...]
    scale = (NOPE_DIM + ROPE_DIM) ** -0.5 * LOG2E
    for hd in range(H_C):
        sl = slice(hd * LANES, (hd + 1) * LANES)
        q_ref[0, :, sl] = (_rope3(q[:, sl], cq, s1q, s2q, LANES - half, half) * scale).astype(BF16)


def _proj_c(x, win, gq, gkv, wuq, tabs_q, tabs_k):
    nb, t, _ = x.shape
    tm = min(ROW_TILE, t)
    assert t % tm == 0
    tok = lambda width: pl.BlockSpec((1, tm, width), lambda b, i: (b, i, 0))
    tab = pl.BlockSpec((tm, LANES), lambda b, i: (i, 0))
    full = lambda a: pl.BlockSpec(a.shape, lambda b, i: (0, 0))
    return pl.pallas_call(
        _proj_c_kernel,
        grid=(nb, t // tm),
        in_specs=[tok(D_MODEL), full(win), full(gq), full(gkv), full(wuq)] + [tab] * 6,
        out_specs=[tok(H_C * LANES), tok(KV_RANK), tok(ROPE_DIM)],
        out_shape=[jax.ShapeDtypeStruct((nb, t, H_C * LANES), BF16),
                   jax.ShapeDtypeStruct((nb, t, KV_RANK), F32),
                   jax.ShapeDtypeStruct((nb, t, ROPE_DIM), F32)],
        compiler_params=_cparams(("parallel", "parallel")),
        name="proj_c",
    )(x, win, gq, gkv, wuq, *tabs_q, *tabs_k)


def _kv_up_kernel(ckv_ref, kr_ref, wk_ref, place_ref, wvt_ref, k_ref, vt_ref):
    cb = ckv_ref[...].astype(BF16)
    k = (jnp.dot(cb, wk_ref[...], preferred_element_type=F32)
         + jnp.dot(kr_ref[...].astype(BF16), place_ref[...], preferred_element_type=F32))
    k_ref[0] = k.astype(BF16)
    vt = lax.dot_general(wvt_ref[...], cb, (((1,), (1,)), ((), ())), preferred_element_type=F32)
    vt_ref[0] = vt.astype(BF16)


def _kv_up(ckv, kr, wk, place, wvt):
    n = ckv.shape[0]
    tm = ATTN_BLOCK
    assert n % tm == 0
    row = lambda width: pl.BlockSpec((tm, width), lambda i: (i, 0))
    full = lambda a: pl.BlockSpec(a.shape, lambda i: (0, 0))
    return pl.pallas_call(
        _kv_up_kernel,
        grid=(n // tm,),
        in_specs=[row(KV_RANK), row(ROPE_DIM), full(wk), full(place), full(wvt)],
        out_specs=[pl.BlockSpec((1, tm, H_C * LANES), lambda i: (i, 0, 0)),
                   pl.BlockSpec((1, H_C * V_DIM_C, tm), lambda i: (i, 0, 0))],
        out_shape=[jax.ShapeDtypeStruct((n // tm, tm, H_C * LANES), BF16),
                   jax.ShapeDtypeStruct((n // tm, H_C * V_DIM_C, tm), BF16)],
        compiler_params=_cparams(("parallel",)),
        name="kv_up",
    )(ckv, kr, wk, place, wvt)


def _rope_tables(pos, dim, lane0):
    half = dim // 2
    inv = ROPE_THETA ** (-jnp.arange(0, dim, 2, dtype=F32) / dim)
    ang = pos.astype(F32)[:, None] * inv[None, :]
    cos, sin = jnp.cos(ang), jnp.sin(ang)
    zero = jnp.zeros_like(sin)
    c = jnp.concatenate([cos, cos], axis=1)
    s1 = jnp.concatenate([-sin, zero], axis=1)
    s2 = jnp.concatenate([zero, sin], axis=1)
    if lane0 < 0:
        reps = LANES // dim
        return tuple(jnp.tile(a, (1, reps)) for a in (c, s1, s2))
    t = pos.shape[0]
    pad = lambda a, fill: jnp.concatenate(
        [jnp.full((t, lane0), fill, F32), a, jnp.full((t, LANES - lane0 - dim), fill, F32)], axis=1)
    return pad(c, 1.0), pad(s1, 0.0), pad(s2, 0.0)


def _pad_cols(a, width):
    return jnp.pad(a, ((0, 0), (0, width - a.shape[1])))


def _blocks(a, tk):
    nb, t, l = a.shape
    return a.reshape(nb, t // tk, tk, l)


def _vt_blocks(v, tk):
    nb, t, l = v.shape
    return v.reshape(nb, t // tk, tk, l).transpose(0, 1, 3, 2)


def _pad_rows(q, rows):
    return jnp.pad(q, ((0, 0), (0, rows - q.shape[1]), (0, 0)))


def _cat_pad_time(cache, new, t_pad):
    nb, t0, l = cache.shape
    t1 = new.shape[1]
    return jnp.concatenate([cache, new, jnp.zeros((nb, t_pad - t0 - t1, l), cache.dtype)], axis=1)


def kernel(x_prompt, x_sample, cache_fox_k, cache_fox_v, cache_fox_logf, cache_diff_k, cache_diff_v, cache_mla_ckv, cache_mla_krope, w_in_ab, b_fgate, diff_lq1, diff_lk1, diff_lq2, diff_lk2, diff_subln, w_out_ab, w_in_c, mla_q_norm, mla_kv_norm, mla_w_uq, mla_w_ukv, w_out_c, ln1_g, ln1_b, ln2_g, ln2_b, moe_wg, moe_bg, moe_we, moe_be, moe_w1, moe_w3, moe_w2):
    bp, tp, d = x_prompt.shape
    bs, ts, _ = x_sample.shape
    past = cache_fox_k.shape[2]
    depth = ln1_g.shape[0]
    alpha = (2 * depth) ** 0.25
    tk = ATTN_BLOCK
    assert past % tk == 0 and ts == 16 and past % CHUNK == 0
    ns = bs * ts
    t_dec = past + tk
    nfull_dec = past // tk
    dec_shift = 4

    pos_p = jnp.arange(tp)
    pos_s = jnp.tile(past + jnp.arange(ts), bs)

    xp = x_prompt
    xs = x_sample.reshape(1, ns, d)
    out_ab_p, out_ab_s, out_c_p, out_c_s = [], [], [], []

    for i in range(depth):
        j = i // 2
        if i % 2 == 0:
            lam_init = 0.8 - 0.6 * math.exp(-0.3 * i)
            cuts = [0, A_WIDTH, 2 * A_WIDTH, 3 * A_WIDTH, 3 * A_WIDTH + H_A,
                    3 * A_WIDTH + H_A + B_QK_WIDTH, 3 * A_WIDTH + H_A + 2 * B_QK_WIDTH,
                    3 * A_WIDTH + H_A + 2 * B_QK_WIDTH + B_V_WIDTH]
            w = w_in_ab[j]
            piece = lambda a: w[:, cuts[a]:cuts[a + 1]]
            w6 = jnp.stack([piece(0), piece(1), piece(2), piece(4), piece(5), piece(6)]).astype(BF16)
            wvt = jnp.stack([piece(2).T, piece(6).T]).astype(BF16)
            wf = _pad_cols(piece(3), LANES).astype(BF16)
            bf = _pad_cols(b_fgate[j][None, :], LANES)
            wout = w_out_ab[j].astype(BF16)
            diff_extra = (diff_lq1[j][None, :], diff_lk1[j][None, :], diff_lq2[j][None, :],
                          diff_lk2[j][None, :], diff_subln[j][None, :])

            tabs = _rope_tables(pos_p, HEAD_DIM, -1)
            (qa, ka, kab, va, vat, lf, lfw, qb, kb, kbb, vb, vbt) = _proj_ab(xp, w6, wvt, wf, bf, tabs)
            bias = _blocks(_decay_bias(lfw), tk)
            oa = _attention("fox", qa, _blocks(kab, tk), vat, (bias,), n_pairs=H_A // 2, mask_shift=0)
            ob = _attention("diff", qb, _blocks(kbb, tk), vbt, diff_extra,
                            n_pairs=H_B, mask_shift=int(math.log2(CHUNK)), lam_init=lam_init)
            out_ab_p.append((ka.reshape(bp, tp, H_A, HEAD_DIM), va.reshape(bp, tp, H_A, HEAD_DIM), lf,
                             kb.reshape(bp, tp, H_B, 2, HEAD_DIM), vb.reshape(bp, tp, H_B, 2 * HEAD_DIM)))
            xp2 = _outproj_ln(xp.reshape(bp * tp, d), [oa.reshape(bp * tp, -1), ob.reshape(bp * tp, -1)],
                              [wout[:A_WIDTH], wout[A_WIDTH:]], ln1_g[i][None, :], ln1_b[i][None, :], alpha)

            tabs = _rope_tables(pos_s, HEAD_DIM, -1)
            (qa, ka, kab, va, _, lf, lfw, qb, kb, kbb, vb, _) = _proj_ab(xs, w6, wvt, wf, bf, tabs)
            rs = lambda a: a.reshape(bs, ts, a.shape[-1])
            cache_lfw = jnp.pad(cache_fox_logf[j].astype(F32), ((0, 0), (0, 0), (0, LANES - H_A)))
            bias = _blocks(_decay_bias(_cat_pad_time(cache_lfw, rs(lfw), t_dec)), tk)
            flat = lambda c: c.reshape(bs, past, -1).astype(BF16)
            qdec = lambda a: _pad_rows(rs(a), DEC_Q_ROWS)
            k_all = _blocks(_cat_pad_time(flat(cache_fox_k[j]), rs(kab), t_dec), tk)
            v_all = _vt_blocks(_cat_pad_time(flat(cache_fox_v[j]), rs(va).astype(BF16), t_dec), tk)
            oa = _attention("fox", qdec(qa), k_all, v_all, (bias,), n_pairs=H_A // 2, mask_shift=0,
                            nfull_static=nfull_dec)[:, :ts]
            k_all = _blocks(_cat_pad_time(flat(cache_diff_k[j]), rs(kbb), t_dec), tk)
            v_all = _vt_blocks(_cat_pad_time(flat(cache_diff_v[j]), rs(vb).astype(BF16), t_dec), tk)
            ob = _attention("diff", qdec(qb), k_all, v_all, diff_extra, n_pairs=H_B, mask_shift=dec_shift,
                            nfull_static=nfull_dec, lam_init=lam_init)[:, :ts]
            out_ab_s.append((ka.reshape(bs, ts, H_A, HEAD_DIM), va.reshape(bs, ts, H_A, HEAD_DIM),
                             lf.reshape(bs, ts, H_A), kb.reshape(bs, ts, H_B, 2, HEAD_DIM),
                             vb.reshape(bs, ts, H_B, 2 * HEAD_DIM)))
            xs2 = _outproj_ln(xs.reshape(ns, d), [oa.reshape(ns, -1), ob.reshape(ns, -1)],
                              [wout[:A_WIDTH], wout[A_WIDTH:]], ln1_g[i][None, :], ln1_b[i][None, :], alpha)
        else:
            wc = w_in_c[j]
            kr_cols = _pad_cols(wc[:, Q_RANK + KV_RANK:], LANES)
            win = jnp.concatenate([wc[:, :Q_RANK + KV_RANK], kr_cols], axis=1).astype(BF16)
            wuq = jnp.pad(mla_w_uq[j].reshape(Q_RANK, H_C, NOPE_DIM + ROPE_DIM),
                          ((0, 0), (0, 0), (0, LANES - NOPE_DIM - ROPE_DIM))).reshape(Q_RANK, H_C * LANES)
            wuq = wuq.astype(BF16)
            wukv = mla_w_ukv[j].reshape(KV_RANK, H_C, NOPE_DIM + V_DIM_C)
            wk = jnp.pad(wukv[:, :, :NOPE_DIM], ((0, 0), (0, 0), (0, LANES - NOPE_DIM)))
            wk = wk.reshape(KV_RANK, H_C * LANES).astype(BF16)
            wvt = wukv[:, :, NOPE_DIM:].reshape(KV_RANK, H_C * V_DIM_C).T.astype(BF16)
            place = jnp.tile(_pad_cols(jnp.concatenate(
                [jnp.zeros((ROPE_DIM, NOPE_DIM), F32), jnp.eye(ROPE_DIM, dtype=F32)], axis=1), LANES),
                (1, H_C)).astype(BF16)
            gq = mla_q_norm[j][None, :]
            gkv = mla_kv_norm[j][None, :]
            wout = w_out_c[j].astype(BF16)

            q, ckv, kr = _proj_c(xp, win, gq, gkv, wuq, _rope_tables(pos_p, ROPE_DIM, NOPE_DIM),
                                 _rope_tables(pos_p, ROPE_DIM, 0))
            kc, vct = _kv_up(ckv.reshape(bp * tp, KV_RANK), kr.reshape(bp * tp, ROPE_DIM), wk, place, wvt)
            per_seq = lambda a, nb: a.reshape((nb, a.shape[0] // nb) + a.shape[1:])
            oc = _attention("mla", q, per_seq(kc, bp), per_seq(vct, bp), (), n_pairs=H_C // 2,
                            mask_shift=int(math.log2(CHUNK)))
            out_c_p.append((ckv, kr))
            xp2 = _outproj_ln(xp.reshape(bp * tp, d), [oc.reshape(bp * tp, -1)], [wout],
                              ln1_g[i][None, :], ln1_b[i][None, :], alpha)

            q, ckv, kr = _proj_c(xs, win, gq, gkv, wuq, _rope_tables(pos_s, ROPE_DIM, NOPE_DIM),
                                 _rope_tables(pos_s, ROPE_DIM, 0))
            ckv_all = _cat_pad_time(cache_mla_ckv[j].astype(F32), ckv.reshape(bs, ts, KV_RANK), t_dec)
            kr_all = _cat_pad_time(cache_mla_krope[j].astype(F32), kr.reshape(bs, ts, ROPE_DIM), t_dec)
            kc, vct = _kv_up(ckv_all.reshape(bs * t_dec, KV_RANK), kr_all.reshape(bs * t_dec, ROPE_DIM),
                             wk, place, wvt)
            oc = _attention("mla", _pad_rows(q.reshape(bs, ts, -1), DEC_Q_ROWS), per_seq(kc, bs),
                            per_seq(vct, bs), (), n_pairs=H_C // 2, mask_shift=dec_shift,
                            nfull_static=nfull_dec)[:, :ts]
            out_c_s.append((ckv.reshape(bs, ts, KV_RANK), kr.reshape(bs, ts, ROPE_DIM)))
            xs2 = _outproj_ln(xs.reshape(ns, d), [oc.reshape(ns, -1)], [wout],
                              ln1_g[i][None, :], ln1_b[i][None, :], alpha)

        wr = _pad_cols(jnp.concatenate(
            [moe_wg[i]] + [moe_we[i][gi] for gi in range(N_GROUPS)], axis=1), LANES)
        wrh = wr.astype(BF16)
        wrl = (wr - wrh.astype(F32)).astype(BF16)
        br = _pad_cols(jnp.concatenate([moe_bg[i], moe_be[i].reshape(-1)])[None, :], LANES)
        moe_w = (moe_w1[i], moe_w3[i], moe_w2[i])
        g2, b2 = ln2_g[i][None, :], ln2_b[i][None, :]
        xp = _moe_ln(xp2, wrh, wrl, br, *moe_w, g2, b2, alpha).reshape(bp, tp, d)
        xs = _moe_ln(xs2, wrh, wrl, br, *moe_w, g2, b2, alpha).reshape(1, ns, d)

    stack = lambda rows, n: jnp.stack([r[n] for r in rows])
    return (xp, xs.reshape(bs, ts, d),
            stack(out_ab_p, 0), stack(out_ab_p, 1), stack(out_ab_p, 2), stack(out_ab_p, 3), stack(out_ab_p, 4),
            stack(out_c_p, 0), stack(out_c_p, 1),
            stack(out_ab_s, 0), stack(out_ab_s, 1), stack(out_ab_s, 2), stack(out_ab_s, 3), stack(out_ab_s, 4),
            stack(out_c_s, 0), stack(out_c_s, 1))
```

```python
import functools
import math

import jax
import jax.numpy as jnp
from jax import lax
from jax.experimental import pallas as pl
from jax.experimental.pallas import tpu as pltpu

F32 = jnp.float32
BF16 = jnp.bfloat16

D_MODEL = 1024
CHUNK = 64
HEAD_DIM = 64
ROPE_THETA = 10000.0
H_A = 8
H_B = 4
H_C = 16
Q_RANK = 256
KV_RANK = 128
NOPE_DIM = 64
ROPE_DIM = 32
V_DIM_C = 64
N_GROUPS = 4
EXPERTS_PER_GROUP = 4
N_EXPERTS = N_GROUPS * EXPERTS_PER_GROUP
D_EXPERT = 256
A_WIDTH = H_A * HEAD_DIM
B_QK_WIDTH = H_B * 2 * HEAD_DIM
B_V_WIDTH = H_B * 2 * HEAD_DIM
FGATE_BIAS = 3.0
LN_EPS = 1e-5
RMS_EPS = 1e-6
NEG_INF = -1e30
LOG2E = math.log2(math.e)

LANES = 128
BF16_ROWS = 16
PANEL = {"fox": 1024, "diff": 256, "mla": 1024}
BIAS_PIECES = 3
DEC_Q_ROWS = 128
VMEM_LIMIT = 48 * 1024 * 1024
ATTN_BLOCK = 512
ROW_TILE = 512
MOE_TILE = 1024
BIAS_ROWS_PER_STEP = 4096
GATE_LANE0 = N_GROUPS


def _cparams(sem):
    return pltpu.CompilerParams(dimension_semantics=sem, vmem_limit_bytes=VMEM_LIMIT)


def _rope3(x, c, s1, s2, shift_up, shift_down):
    return x * c + pltpu.roll(x, shift_up, 1) * s1 + pltpu.roll(x, shift_down, 1) * s2


def _layer_norm(y, g, b):
    mu = jnp.mean(y, axis=-1, keepdims=True)
    d = y - mu
    var = jnp.mean(d * d, axis=-1, keepdims=True)
    return d * lax.rsqrt(var + LN_EPS) * g + b


def _split3(x):
    hi = x.astype(BF16)
    r1 = x - hi.astype(F32)
    mid = r1.astype(BF16)
    return hi, mid, (r1 - mid.astype(F32)).astype(BF16)


def _proj_ab_kernel(x_ref, w_ref, wvt_ref, wf_ref, bf_ref, c_ref, s1_ref, s2_ref,
                    qa_ref, ka_ref, kab_ref, va_ref, vat_ref, lf_ref, lfw_ref,
                    qb_ref, kb_ref, kbb_ref, vb_ref, vbt_ref):
    xb = x_ref[0].astype(BF16)

    def mm(i):
        return jnp.dot(xb, w_ref[i], preferred_element_type=F32)

    def mm_t(i):
        return lax.dot_general(wvt_ref[i], xb, (((1,), (1,)), ((), ())), preferred_element_type=F32)

    qa_ref[0] = (mm(0) * (HEAD_DIM ** -0.5 * LOG2E)).astype(BF16)
    ka = mm(1)
    ka_ref[0] = ka
    kab_ref[0] = ka.astype(BF16)
    va_ref[0] = mm(2)
    vat_ref[0, 0] = mm_t(0).astype(BF16)

    z = jnp.dot(xb, wf_ref[...], preferred_element_type=F32) + bf_ref[...]
    lf = jnp.minimum(z, 0.0) - jnp.log1p(jnp.exp(-jnp.abs(z)))
    lf_ref[0] = lf[:, :H_A]
    lfw_ref[0] = lf

    c, s1, s2 = c_ref[...], s1_ref[...], s2_ref[...]
    qb = mm(3)
    kb = mm(4)
    for s in range(B_QK_WIDTH // LANES):
        sl = slice(s * LANES, (s + 1) * LANES)
        qs = _rope3(qb[:, sl], c, s1, s2, LANES - HEAD_DIM // 2, HEAD_DIM // 2)
        qb_ref[0, :, sl] = (qs * (HEAD_DIM ** -0.5 * LOG2E)).astype(BF16)
        ks = _rope3(kb[:, sl], c, s1, s2, LANES - HEAD_DIM // 2, HEAD_DIM // 2)
        kb_ref[0, :, sl] = ks
        kbb_ref[0, :, sl] = ks.astype(BF16)
    vb_ref[0] = mm(5)
    vbt_ref[0, 0] = mm_t(1).astype(BF16)


def _proj_ab(x, w6, wvt, wf, bf, tabs):
    nb, t, _ = x.shape
    tm = min(ROW_TILE, t)
    assert t % tm == 0
    w = A_WIDTH
    tok = lambda width: pl.BlockSpec((1, tm, width), lambda b, i: (b, i, 0))
    tr = pl.BlockSpec((1, 1, w, tm), lambda b, i: (b, i, 0, 0))
    tab = pl.BlockSpec((tm, LANES), lambda b, i: (i, 0))
    full = lambda a: pl.BlockSpec(a.shape, lambda b, i: (0,) * a.ndim)
    sds = lambda width, dt: jax.ShapeDtypeStruct((nb, t, width), dt)
    sds_t = jax.ShapeDtypeStruct((nb, t // tm, w, tm), BF16)
    return pl.pallas_call(
        _proj_ab_kernel,
        grid=(nb, t // tm),
        in_specs=[tok(D_MODEL), full(w6), full(wvt), full(wf), full(bf), tab, tab, tab],
        out_specs=[tok(w), tok(w), tok(w), tok(w), tr, tok(H_A), tok(LANES), tok(w), tok(w), tok(w), tok(w), tr],
        out_shape=[sds(w, BF16), sds(w, F32), sds(w, BF16), sds(w, F32), sds_t, sds(H_A, F32), sds(LANES, F32),
                   sds(w, BF16), sds(w, F32), sds(w, BF16), sds(w, F32), sds_t],
        compiler_params=_cparams(("parallel", "parallel")),
        name="proj_ab",
    )(x, w6, wvt, wf, bf, *tabs)


def _decay_bias_kernel(lf_ref, spread_ref, lower_ref, o_ref, carry_ref):
    @pl.when(pl.program_id(1) == 0)
    def _():
        carry_ref[...] = jnp.zeros_like(carry_ref)

    spread = spread_ref[...]
    lower = lower_ref[...]
    tc = lower.shape[0]
    lane = lax.broadcasted_iota(jnp.int32, (1, LANES), 1).astype(F32)
    piece = lane - BIAS_PIECES * jnp.floor((lane + 0.5) * (1.0 / BIAS_PIECES))
    carry = carry_ref[...]
    for r in range(lf_ref.shape[1] // tc):
        rows = slice(r * tc, (r + 1) * tc)
        x = lf_ref[0, rows, :]
        xr = sum(jnp.dot(p, spread, preferred_element_type=F32) for p in _split3(x))
        c = sum(jnp.dot(lower, p, preferred_element_type=F32) for p in _split3(xr)) + carry
        carry = c[tc - 1:tc, :]
        hi, mid, lo = (p.astype(F32) for p in _split3(c * (-LOG2E)))
        o_ref[0, rows, :] = jnp.where(piece == 0.0, hi, jnp.where(piece == 1.0, mid, lo)).astype(BF16)
    carry_ref[...] = carry


def _decay_bias(lf_wide):
    nb, t, _ = lf_wide.shape
    tc = min(ATTN_BLOCK, t)
    tb = min(BIAS_ROWS_PER_STEP, t)
    assert t % tb == 0 and tb % tc == 0
    spec = pl.BlockSpec((1, tb, LANES), lambda b, i: (b, i, 0))
    src = jnp.arange(LANES)[:, None]
    dst = jnp.arange(LANES)[None, :]
    spread = ((dst // BIAS_PIECES == src) & (src < H_A)).astype(BF16)
    lower = jnp.tril(jnp.ones((tc, tc), BF16))
    const = lambda a: pl.BlockSpec(a.shape, lambda b, i: (0, 0))
    return pl.pallas_call(
        _decay_bias_kernel,
        grid=(nb, t // tb),
        in_specs=[spec, const(spread), const(lower)],
        out_specs=spec,
        out_shape=jax.ShapeDtypeStruct((nb, t, LANES), BF16),
        scratch_shapes=[pltpu.VMEM((1, LANES), F32)],
        compiler_params=_cparams(("parallel", "arbitrary")),
        name="cumsum",
    )(lf_wide, spread, lower)


def _attn_kernel(*refs, mode, tq, tk, mask_shift, nfull_static, n_diag, lam_init):
    if mode == "diff":
        q_ref, k_ref, vt_ref, lq1_ref, lk1_ref, lq2_ref, lk2_ref, sub_ref, o_ref = refs[:9]
    elif mode == "fox":
        q_ref, k_ref, vt_ref, b_ref, o_ref = refs[:5]
    else:
        q_ref, k_ref, vt_ref, o_ref = refs[:4]
    m_sc, acc_sc, sa, bma, sb, bmb = refs[-6:]
    v_rows = LANES if mode == "diff" else HEAD_DIM
    sa_sc, sb_sc = (sa, bma), (sb, bmb)

    qi = pl.program_id(2)
    q = q_ref[0]
    lane = lax.broadcasted_iota(jnp.int32, (1, LANES), 1)
    if mode == "mla":
        qs = [q[:, :LANES], q[:, LANES:]]
    else:
        zero = jnp.zeros_like(q)
        qs = [jnp.where(lane < HEAD_DIM, q, zero), jnp.where(lane >= HEAD_DIM, q, zero)]
        if mode == "fox":
            def pick(i):
                lo = BIAS_PIECES * (2 * pl.program_id(1) + i)
                hot = jnp.where((lane >= lo) & (lane < lo + BIAS_PIECES), 1.0, 0.0)
                return jnp.broadcast_to(hot, (tq, LANES)).astype(BF16)

            qs = [jnp.concatenate([qs[i], pick(i)], axis=1) for i in range(2)]

    m_sc[...] = jnp.full(m_sc.shape, NEG_INF, F32)
    acc_sc[...] = jnp.zeros(acc_sc.shape, F32)

    pw = min(PANEL[mode], tq)

    def scores(j, bufs, c):
        s_sc, bm_sc = bufs
        cs = slice(c * pw, (c + 1) * pw)
        k = k_ref[0, j]
        if mode == "fox":
            k = jnp.concatenate([k, b_ref[0, j]], axis=1)
        for i in range(2):
            ki = k[:, i * LANES:(i + 1) * LANES] if mode == "mla" else k
            st = lax.dot_general(ki, qs[i][cs], (((1,), (1,)), ((), ())), preferred_element_type=F32)
            s_sc[i, c] = st
            bm_sc[i, c] = jnp.max(st, axis=0, keepdims=True)

    def consume(j, bufs, c, diag):
        s_sc, bm_sc = bufs
        cs = slice(c * pw, (c + 1) * pw)
        vt = vt_ref[0, j]
        for i in range(2):
            st = s_sc[i, c]
            if diag is not None:
                key = lax.broadcasted_iota(jnp.int32, (tk, pw), 0) + diag * tk
                qry = lax.broadcasted_iota(jnp.int32, (tk, pw), 1) + c * pw
                vis = lax.shift_right_logical(key, mask_shift) <= lax.shift_right_logical(qry, mask_shift)
                st = jnp.where(vis, st, NEG_INF)
                blk_max = jnp.max(st, axis=0, keepdims=True)
            else:
                blk_max = bm_sc[i, c]
            m_prev = m_sc[i, c]
            m_new = jnp.maximum(m_prev, blk_max)
            alpha = jnp.exp2(m_prev - m_new)
            p = jnp.exp2(st - m_new).astype(BF16)
            vi = vt if mode == "diff" else vt[i * HEAD_DIM:(i + 1) * HEAD_DIM]
            vi = jnp.concatenate([vi, jnp.ones((BF16_ROWS, tk), BF16)], axis=0)
            acc_sc[i, c] = alpha * acc_sc[i, c] + jnp.dot(vi, p, preferred_element_type=F32)
            m_sc[i, c] = m_new

    def stage(nxt, cur, diag=None):
        for c in range(tq // pw):
            if nxt is not None:
                scores(nxt[0], nxt[1], c)
            consume(cur[0], cur[1], c, diag)

    n_pairs_full = qi * (n_diag // 2) if nfull_static is None else nfull_static // 2
    nfull = 2 * n_pairs_full

    def pair(jj, carry):
        stage((2 * jj + 1, sb_sc), (2 * jj, sa_sc))
        stage((2 * jj + 2, sa_sc), (2 * jj + 1, sb_sc))
        return carry

    for c in range(tq // pw):
        scores(0, sa_sc, c)
    lax.fori_loop(0, n_pairs_full, pair, 0)
    if n_diag == 2:
        stage((nfull + 1, sb_sc), (nfull, sa_sc), diag=0)
        stage(None, (nfull + 1, sb_sc), diag=1)
    else:
        stage(None, (nfull, sa_sc), diag=0)

    if mode == "diff":
        lam = (jnp.exp(jnp.sum(lq1_ref[...] * lk1_ref[...], axis=1, keepdims=True))
               - jnp.exp(jnp.sum(lq2_ref[...] * lk2_ref[...], axis=1, keepdims=True)) + lam_init)
    for c in range(tq // pw):
        o0 = acc_sc[0, c, :v_rows] / acc_sc[0, c, v_rows:v_rows + 1]
        o1 = acc_sc[1, c, :v_rows] / acc_sc[1, c, v_rows:v_rows + 1]
        if mode == "diff":
            o = o0 - lam * o1
            ms = jnp.mean(o * o, axis=0, keepdims=True)
            o = (o * lax.rsqrt(ms + RMS_EPS)).T * sub_ref[...] * (1.0 - lam_init)
        else:
            o = jnp.concatenate([o0, o1], axis=0).T
        o_ref[0, c * pw:(c + 1) * pw, :] = o.astype(o_ref.dtype)


def _attention(mode, q, k, vt, extra, *, n_pairs, mask_shift, nfull_static=None, lam_init=0.0):
    nb, t_q, _ = q.shape
    _, nkb, tk, _ = k.shape
    if nfull_static is None:
        tq, n_diag = 2 * tk, 2
        assert t_q % tq == 0 and nkb == t_q // tk
    else:
        tq, n_diag = t_q, 1
        assert nfull_static % 2 == 0 and nkb == nfull_static + 1
    pw = min(PANEL[mode], tq)
    n_pan = tq // pw
    qw = 2 * LANES if mode == "mla" else LANES
    in_specs = [
        pl.BlockSpec((1, tq, qw), lambda b, p, i: (b, i, p)),
        pl.BlockSpec((1, nkb, tk, qw), lambda b, p, i: (b, 0, 0, p)),
        pl.BlockSpec((1, nkb, LANES, tk), lambda b, p, i: (b, 0, p, 0)),
    ]
    if mode == "fox":
        in_specs.append(pl.BlockSpec((1, nkb, tk, LANES), lambda b, p, i: (b, 0, 0, 0)))
    elif mode == "diff":
        in_specs += [pl.BlockSpec(a.shape, lambda b, p, i: (0, 0)) for a in extra]
    kern = functools.partial(_attn_kernel, mode=mode, tq=tq, tk=tk, mask_shift=mask_shift,
                             nfull_static=nfull_static, n_diag=n_diag, lam_init=lam_init)
    return pl.pallas_call(
        kern,
        grid=(nb, n_pairs, t_q // tq),
        in_specs=in_specs,
        out_specs=pl.BlockSpec((1, tq, LANES), lambda b, p, i: (b, i, p)),
        out_shape=jax.ShapeDtypeStruct((nb, t_q, n_pairs * LANES), BF16),
        scratch_shapes=[pltpu.VMEM((2, n_pan, 1, pw), F32),
                        pltpu.VMEM((2, n_pan, (LANES if mode == "diff" else HEAD_DIM) + BF16_ROWS, pw), F32),
                        pltpu.VMEM((2, n_pan, tk, pw), F32), pltpu.VMEM((2, n_pan, 1, pw), F32),
                        pltpu.VMEM((2, n_pan, tk, pw), F32), pltpu.VMEM((2, n_pan, 1, pw), F32)],
        compiler_params=_cparams(("parallel", "parallel", "arbitrary")),
        name="attn_" + mode,
    )(q, k, vt, *extra)


def _outproj_ln_kernel(*refs, n_in, alpha):
    x_ref = refs[0]
    o_refs = refs[1:1 + n_in]
    w_refs = refs[1 + n_in:1 + 2 * n_in]
    g_ref, b_ref, y_ref = refs[1 + 2 * n_in:]
    mix = jnp.dot(o_refs[0][...], w_refs[0][...], preferred_element_type=F32)
    for o_r, w_r in zip(o_refs[1:], w_refs[1:]):
        mix = mix + jnp.dot(o_r[...], w_r[...], preferred_element_type=F32)
    y_ref[...] = _layer_norm(alpha * x_ref[...] + mix, g_ref[...], b_ref[...])


def _outproj_ln(x, outs, ws, g, b, alpha):
    n, d = x.shape
    tm = min(ROW_TILE, n)
    assert n % tm == 0
    row = lambda width: pl.BlockSpec((tm, width), lambda i: (i, 0))
    full = lambda a: pl.BlockSpec(a.shape, lambda i: (0, 0))
    return pl.pallas_call(
        functools.partial(_outproj_ln_kernel, n_in=len(outs), alpha=alpha),
        grid=(n // tm,),
        in_specs=[row(d)] + [row(o.shape[1]) for o in outs] + [full(w) for w in ws] + [full(g), full(b)],
        out_specs=row(d),
        out_shape=jax.ShapeDtypeStruct((n, d), F32),
        compiler_params=_cparams(("parallel",)),
        name="outproj_ln",
    )(x, *outs, *ws, g, b)


def _route(logits):
    lane = lax.broadcasted_iota(jnp.int32, logits.shape, 1).astype(F32)
    big = float(1 << 20)
    is_g = lane < N_GROUPS
    lg = jnp.where(is_g, logits, NEG_INF)
    eg = jnp.where(is_g, jnp.exp(lg - jnp.max(lg, axis=1, keepdims=True)), 0.0)
    pg = eg / jnp.sum(eg, axis=1, keepdims=True)
    p_g = jnp.max(pg, axis=1, keepdims=True)
    gidx = jnp.min(jnp.where(is_g & (pg == p_g), lane, big), axis=1, keepdims=True)
    lo = GATE_LANE0 + EXPERTS_PER_GROUP * gidx
    sel = (lane >= lo) & (lane < lo + EXPERTS_PER_GROUP)
    le = jnp.where(sel, logits, NEG_INF)
    ee = jnp.where(sel, jnp.exp(le - jnp.max(le, axis=1, keepdims=True)), 0.0)
    pe = ee / jnp.sum(ee, axis=1, keepdims=True)
    v1 = jnp.max(jnp.where(sel, pe, -1.0), axis=1, keepdims=True)
    i1 = jnp.min(jnp.where(sel & (pe == v1), lane, big), axis=1, keepdims=True)
    rest = sel & (lane != i1)
    v2 = jnp.max(jnp.where(rest, pe, -1.0), axis=1, keepdims=True)
    i2 = jnp.min(jnp.where(rest & (pe == v2), lane, big), axis=1, keepdims=True)
    tot = v1 + v2
    w1 = v1 / tot * p_g
    w2 = v2 / tot * p_g
    return jnp.where(lane == i1, w1, jnp.where(lane == i2, w2, 0.0))


def _moe_ln_kernel(x_ref, wrh_ref, wrl_ref, br_ref, w1_ref, w3_ref, w2_ref, g_ref, b_ref, y_ref,
                   xb_sc, gate_sc, acc_sc, *, alpha):
    e = pl.program_id(1)

    @pl.when(e == 0)
    def _():
        x = x_ref[...]
        xh = x.astype(BF16)
        xl = (x - xh.astype(F32)).astype(BF16)
        xb_sc[...] = xh
        logits = (jnp.dot(xh, wrh_ref[...], preferred_element_type=F32)
                  + jnp.dot(xl, wrh_ref[...], preferred_element_type=F32)
                  + jnp.dot(xh, wrl_ref[...], preferred_element_type=F32) + br_ref[...])
        gate_sc[...] = _route(logits)
        acc_sc[...] = jnp.zeros_like(acc_sc)

    xb = xb_sc[...]
    h1 = jnp.dot(xb, w1_ref[0].astype(BF16), preferred_element_type=F32)
    h3 = jnp.dot(xb, w3_ref[0].astype(BF16), preferred_element_type=F32)
    hdn = (h1 * jax.nn.sigmoid(h1)) * h3
    y = jnp.dot(hdn.astype(BF16), w2_ref[0].astype(BF16), preferred_element_type=F32)
    lane = lax.broadcasted_iota(jnp.int32, (1, LANES), 1)
    ge = jnp.sum(jnp.where(lane == e + GATE_LANE0, gate_sc[...], 0.0), axis=1, keepdims=True)
    acc_sc[...] += ge * y

    @pl.when(e == pl.num_programs(1) - 1)
    def _():
        y_ref[...] = _layer_norm(alpha * x_ref[...] + acc_sc[...], g_ref[...], b_ref[...])


def _moe_ln(x, wrh, wrl, br, w1, w3, w2, g, b, alpha):
    n, d = x.shape
    tm = min(MOE_TILE, n)
    assert n % tm == 0
    ne = w1.shape[0]
    per_expert = lambda a: pl.BlockSpec((1,) + a.shape[1:], lambda i, e: (e, 0, 0))
    full = lambda a: pl.BlockSpec(a.shape, lambda i, e: (0, 0))
    return pl.pallas_call(
        functools.partial(_moe_ln_kernel, alpha=alpha),
        grid=(n // tm, ne),
        in_specs=[pl.BlockSpec((tm, d), lambda i, e: (i, 0)), full(wrh), full(wrl), full(br),
                  per_expert(w1), per_expert(w3), per_expert(w2),
                  full(g), full(b)],
        out_specs=pl.BlockSpec((tm, d), lambda i, e: (i, 0)),
        out_shape=jax.ShapeDtypeStruct((n, d), F32),
        scratch_shapes=[pltpu.VMEM((tm, d), BF16), pltpu.VMEM((tm, LANES), F32), pltpu.VMEM((tm, d), F32)],
        compiler_params=_cparams(("parallel", "arbitrary")),
        name="moe_ln",
    )(x, wrh, wrl, br, w1, w3, w2, g, b)


def _proj_c_kernel(x_ref, win_ref, gq_ref, gkv_ref, wuq_ref, cq_ref, s1q_ref, s2q_ref,
                   ck_ref, s1k_ref, s2k_ref, q_ref, ckv_ref, kr_ref):
    xb = x_ref[0].astype(BF16)
    h = jnp.dot(xb, win_ref[...], preferred_element_type=F32)
    qa = h[:, :Q_RANK]
    kva = h[:, Q_RANK:Q_RANK + KV_RANK]
    krw = h[:, Q_RANK + KV_RANK:]
    qn = qa * lax.rsqrt(jnp.mean(qa * qa, axis=1, keepdims=True) + RMS_EPS) * gq_ref[...]
    ckv_ref[0] = kva * lax.rsqrt(jnp.mean(kva * kva, axis=1, keepdims=True) + RMS_EPS) * gkv_ref[...]
    half = ROPE_DIM // 2
    kr = _rope3(krw, ck_ref[...], s1k_ref[...], s2k_ref[...], LANES - half, half)
    kr_ref[0] = kr[:, :ROPE_DIM]
    q = jnp.dot(qn.astype(BF16), wuq_ref[...], preferred_element_type=F32)
    cq, s1q, s2q = cq_ref[...], s1q_ref[...], s2q_ref[...]
    scale = (NOPE_DIM + ROPE_DIM) ** -0.5 * LOG2E
    for hd in range(H_C):
        sl = slice(hd * LANES, (hd + 1) * LANES)
        q_ref[0, :, sl] = (_rope3(q[:, sl], cq, s1q, s2q, LANES - half, half) * scale).astype(BF16)


def _proj_c(x, win, gq, gkv, wuq, tabs_q, tabs_k):
    nb, t, _ = x.shape
    tm = min(ROW_TILE, t)
    assert t % tm == 0
    tok = lambda width: pl.BlockSpec((1, tm, width), lambda b, i: (b, i, 0))
    tab = pl.BlockSpec((tm, LANES), lambda b, i: (i, 0))
    full = lambda a: pl.BlockSpec(a.shape, lambda b, i: (0, 0))
    return pl.pallas_call(
        _proj_c_kernel,
        grid=(nb, t // tm),
        in_specs=[tok(D_MODEL), full(win), full(gq), full(gkv), full(wuq)] + [tab] * 6,
        out_specs=[tok(H_C * LANES), tok(KV_RANK), tok(ROPE_DIM)],
        out_shape=[jax.ShapeDtypeStruct((nb, t, H_C * LANES), BF16),
                   jax.ShapeDtypeStruct((nb, t, KV_RANK), F32),
                   jax.ShapeDtypeStruct((nb, t, ROPE_DIM), F32)],
        compiler_params=_cparams(("parallel", "parallel")),
        name="proj_c",
    )(x, win, gq, gkv, wuq, *tabs_q, *tabs_k)


def _kv_up_kernel(ckv_ref, kr_ref, wk_ref, place_ref, wvt_ref, k_ref, vt_ref):
    cb = ckv_ref[...].astype(BF16)
    k = (jnp.dot(cb, wk_ref[...], preferred_element_type=F32)
         + jnp.dot(kr_ref[...].astype(BF16), place_ref[...], preferred_element_type=F32))
    k_ref[0] = k.astype(BF16)
    vt = lax.dot_general(wvt_ref[...], cb, (((1,), (1,)), ((), ())), preferred_element_type=F32)
    vt_ref[0] = vt.astype(BF16)


def _kv_up(ckv, kr, wk, place, wvt):
    n = ckv.shape[0]
    tm = ATTN_BLOCK
    assert n % tm == 0
    row = lambda width: pl.BlockSpec((tm, width), lambda i: (i, 0))
    full = lambda a: pl.BlockSpec(a.shape, lambda i: (0, 0))
    return pl.pallas_call(
        _kv_up_kernel,
        grid=(n // tm,),
        in_specs=[row(KV_RANK), row(ROPE_DIM), full(wk), full(place), full(wvt)],
        out_specs=[pl.BlockSpec((1, tm, H_C * LANES), lambda i: (i, 0, 0)),
                   pl.BlockSpec((1, H_C * V_DIM_C, tm), lambda i: (i, 0, 0))],
        out_shape=[jax.ShapeDtypeStruct((n // tm, tm, H_C * LANES), BF16),
                   jax.ShapeDtypeStruct((n // tm, H_C * V_DIM_C, tm), BF16)],
        compiler_params=_cparams(("parallel",)),
        name="kv_up",
    )(ckv, kr, wk, place, wvt)


def _mla_decode_kernel(q_ref, ckv_ref, kr_ref, ckvn_ref, krn_ref, wabs_ref, wv_ref, o_ref):
    q = q_ref[0]
    ts = q.shape[0]
    qs = jnp.concatenate(
        [jnp.dot(q[:, h * LANES:(h + 1) * LANES], wabs_ref[h], preferred_element_type=F32).astype(BF16)
         for h in range(H_C)], axis=0)
    kc = jnp.concatenate([ckv_ref[0].astype(BF16), kr_ref[0].astype(BF16)], axis=1)
    kn = jnp.concatenate([ckvn_ref[0].astype(BF16), krn_ref[0].astype(BF16)], axis=1)
    nt = (((1,), (1,)), ((), ()))
    sc = lax.dot_general(qs, kc, nt, preferred_element_type=F32)
    sn = lax.dot_general(qs, kn, nt, preferred_element_type=F32)
    m = jnp.maximum(jnp.max(sc, axis=1, keepdims=True), jnp.max(sn, axis=1, keepdims=True))
    pc = jnp.exp2(sc - m)
    pn = jnp.exp2(sn - m)
    l = jnp.sum(pc, axis=1, keepdims=True) + jnp.sum(pn, axis=1, keepdims=True)
    ol = (jnp.dot(pc.astype(BF16), kc[:, :KV_RANK], preferred_element_type=F32)
          + jnp.dot(pn.astype(BF16), kn[:, :KV_RANK], preferred_element_type=F32)) / l
    olb = ol.astype(BF16)
    o = jnp.dot(olb[:ts], wv_ref[0], preferred_element_type=F32)
    for h in range(1, H_C):
        o = o + jnp.dot(olb[h * ts:(h + 1) * ts], wv_ref[h], preferred_element_type=F32)
    o_ref[0] = o.astype(o_ref.dtype)


def _mla_decode(q, ckv_c, kr_c, ckv_n, kr_n, w_abs, w_vout):
    nb, ts, _ = q.shape
    per_b = lambda a: pl.BlockSpec((1,) + a.shape[1:], lambda b: (b, 0, 0))
    full = lambda a: pl.BlockSpec(a.shape, lambda b: (0, 0, 0))
    return pl.pallas_call(
        _mla_decode_kernel,
        grid=(nb,),
        in_specs=[per_b(q), per_b(ckv_c), per_b(kr_c), per_b(ckv_n), per_b(kr_n), full(w_abs), full(w_vout)],
        out_specs=pl.BlockSpec((1, ts, H_C * V_DIM_C), lambda b: (b, 0, 0)),
        out_shape=jax.ShapeDtypeStruct((nb, ts, H_C * V_DIM_C), BF16),
        compiler_params=_cparams(("parallel",)),
        name="mla_decode",
    )(q, ckv_c, kr_c, ckv_n, kr_n, w_abs, w_vout)


def _rope_tables(pos, dim, lane0):
    half = dim // 2
    inv = ROPE_THETA ** (-jnp.arange(0, dim, 2, dtype=F32) / dim)
    ang = pos.astype(F32)[:, None] * inv[None, :]
    cos, sin = jnp.cos(ang), jnp.sin(ang)
    zero = jnp.zeros_like(sin)
    c = jnp.concatenate([cos, cos], axis=1)
    s1 = jnp.concatenate([-sin, zero], axis=1)
    s2 = jnp.concatenate([zero, sin], axis=1)
    if lane0 < 0:
        reps = LANES // dim
        return tuple(jnp.tile(a, (1, reps)) for a in (c, s1, s2))
    t = pos.shape[0]
    pad = lambda a, fill: jnp.concatenate(
        [jnp.full((t, lane0), fill, F32), a, jnp.full((t, LANES - lane0 - dim), fill, F32)], axis=1)
    return pad(c, 1.0), pad(s1, 0.0), pad(s2, 0.0)


def _pad_cols(a, width):
    return jnp.pad(a, ((0, 0), (0, width - a.shape[1])))


def _blocks(a, tk):
    nb, t, l = a.shape
    return a.reshape(nb, t // tk, tk, l)


def _vt_blocks(v, tk):
    nb, t, l = v.shape
    return v.reshape(nb, t // tk, tk, l).transpose(0, 1, 3, 2)


def _pad_rows(q, rows):
    return jnp.pad(q, ((0, 0), (0, rows - q.shape[1]), (0, 0)))


def _cat_pad_time(cache, new, t_pad):
    nb, t0, l = cache.shape
    t1 = new.shape[1]
    return jnp.concatenate([cache, new, jnp.zeros((nb, t_pad - t0 - t1, l), cache.dtype)], axis=1)


def kernel(x_prompt, x_sample, cache_fox_k, cache_fox_v, cache_fox_logf, cache_diff_k, cache_diff_v, cache_mla_ckv, cache_mla_krope, w_in_ab, b_fgate, diff_lq1, diff_lk1, diff_lq2, diff_lk2, diff_subln, w_out_ab, w_in_c, mla_q_norm, mla_kv_norm, mla_w_uq, mla_w_ukv, w_out_c, ln1_g, ln1_b, ln2_g, ln2_b, moe_wg, moe_bg, moe_we, moe_be, moe_w1, moe_w3, moe_w2):
    bp, tp, d = x_prompt.shape
    bs, ts, _ = x_sample.shape
    past = cache_fox_k.shape[2]
    depth = ln1_g.shape[0]
    alpha = (2 * depth) ** 0.25
    tk = ATTN_BLOCK
    assert past % tk == 0 and ts == 16 and past % CHUNK == 0
    ns = bs * ts
    t_dec = past + tk
    nfull_dec = past // tk
    dec_shift = 4

    pos_p = jnp.arange(tp)
    pos_s = jnp.tile(past + jnp.arange(ts), bs)

    xp = x_prompt
    xs = x_sample.reshape(1, ns, d)
    out_ab_p, out_ab_s, out_c_p, out_c_s = [], [], [], []

    for i in range(depth):
        j = i // 2
        if i % 2 == 0:
            lam_init = 0.8 - 0.6 * math.exp(-0.3 * i)
            cuts = [0, A_WIDTH, 2 * A_WIDTH, 3 * A_WIDTH, 3 * A_WIDTH + H_A,
                    3 * A_WIDTH + H_A + B_QK_WIDTH, 3 * A_WIDTH + H_A + 2 * B_QK_WIDTH,
                    3 * A_WIDTH + H_A + 2 * B_QK_WIDTH + B_V_WIDTH]
            w = w_in_ab[j]
            piece = lambda a: w[:, cuts[a]:cuts[a + 1]]
            w6 = jnp.stack([piece(0), piece(1), piece(2), piece(4), piece(5), piece(6)]).astype(BF16)
            wvt = jnp.stack([piece(2).T, piece(6).T]).astype(BF16)
            wf = _pad_cols(piece(3), LANES).astype(BF16)
            bf = _pad_cols(b_fgate[j][None, :], LANES)
            wout = w_out_ab[j].astype(BF16)
            diff_extra = (diff_lq1[j][None, :], diff_lk1[j][None, :], diff_lq2[j][None, :],
                          diff_lk2[j][None, :], diff_subln[j][None, :])

            tabs = _rope_tables(pos_p, HEAD_DIM, -1)
            (qa, ka, kab, va, vat, lf, lfw, qb, kb, kbb, vb, vbt) = _proj_ab(xp, w6, wvt, wf, bf, tabs)
            bias = _blocks(_decay_bias(lfw), tk)
            oa = _attention("fox", qa, _blocks(kab, tk), vat, (bias,), n_pairs=H_A // 2, mask_shift=0)
            ob = _attention("diff", qb, _blocks(kbb, tk), vbt, diff_extra,
                            n_pairs=H_B, mask_shift=int(math.log2(CHUNK)), lam_init=lam_init)
            out_ab_p.append((ka.reshape(bp, tp, H_A, HEAD_DIM), va.reshape(bp, tp, H_A, HEAD_DIM), lf,
                             kb.reshape(bp, tp, H_B, 2, HEAD_DIM), vb.reshape(bp, tp, H_B, 2 * HEAD_DIM)))
            xp2 = _outproj_ln(xp.reshape(bp * tp, d), [oa.reshape(bp * tp, -1), ob.reshape(bp * tp, -1)],
                              [wout[:A_WIDTH], wout[A_WIDTH:]], ln1_g[i][None, :], ln1_b[i][None, :], alpha)

            tabs = _rope_tables(pos_s, HEAD_DIM, -1)
            (qa, ka, kab, va, _, lf, lfw, qb, kb, kbb, vb, _) = _proj_ab(xs, w6, wvt, wf, bf, tabs)
            rs = lambda a: a.reshape(bs, ts, a.shape[-1])
            cache_lfw = jnp.pad(cache_fox_logf[j].astype(F32), ((0, 0), (0, 0), (0, LANES - H_A)))
            bias = _blocks(_decay_bias(_cat_pad_time(cache_lfw, rs(lfw), t_dec)), tk)
            flat = lambda c: c.reshape(bs, past, -1).astype(BF16)
            qdec = lambda a: _pad_rows(rs(a), DEC_Q_ROWS)
            k_all = _blocks(_cat_pad_time(flat(cache_fox_k[j]), rs(kab), t_dec), tk)
            v_all = _vt_blocks(_cat_pad_time(flat(cache_fox_v[j]), rs(va).astype(BF16), t_dec), tk)
            oa = _attention("fox", qdec(qa), k_all, v_all, (bias,), n_pairs=H_A // 2, mask_shift=0,
                            nfull_static=nfull_dec)[:, :ts]
            k_all = _blocks(_cat_pad_time(flat(cache_diff_k[j]), rs(kbb), t_dec), tk)
            v_all = _vt_blocks(_cat_pad_time(flat(cache_diff_v[j]), rs(vb).astype(BF16), t_dec), tk)
            ob = _attention("diff", qdec(qb), k_all, v_all, diff_extra, n_pairs=H_B, mask_shift=dec_shift,
                            nfull_static=nfull_dec, lam_init=lam_init)[:, :ts]
            out_ab_s.append((ka.reshape(bs, ts, H_A, HEAD_DIM), va.reshape(bs, ts, H_A, HEAD_DIM),
                             lf.reshape(bs, ts, H_A), kb.reshape(bs, ts, H_B, 2, HEAD_DIM),
                             vb.reshape(bs, ts, H_B, 2 * HEAD_DIM)))
            xs2 = _outproj_ln(xs.reshape(ns, d), [oa.reshape(ns, -1), ob.reshape(ns, -1)],
                              [wout[:A_WIDTH], wout[A_WIDTH:]], ln1_g[i][None, :], ln1_b[i][None, :], alpha)
        else:
            wc = w_in_c[j]
            kr_cols = _pad_cols(wc[:, Q_RANK + KV_RANK:], LANES)
            win = jnp.concatenate([wc[:, :Q_RANK + KV_RANK], kr_cols], axis=1).astype(BF16)
            wuq = jnp.pad(mla_w_uq[j].reshape(Q_RANK, H_C, NOPE_DIM + ROPE_DIM),
                          ((0, 0), (0, 0), (0, LANES - NOPE_DIM - ROPE_DIM))).reshape(Q_RANK, H_C * LANES)
            wuq = wuq.astype(BF16)
            wukv = mla_w_ukv[j].reshape(KV_RANK, H_C, NOPE_DIM + V_DIM_C)
            wk = jnp.pad(wukv[:, :, :NOPE_DIM], ((0, 0), (0, 0), (0, LANES - NOPE_DIM)))
            wk = wk.reshape(KV_RANK, H_C * LANES).astype(BF16)
            wvt = wukv[:, :, NOPE_DIM:].reshape(KV_RANK, H_C * V_DIM_C).T.astype(BF16)
            place = jnp.tile(_pad_cols(jnp.concatenate(
                [jnp.zeros((ROPE_DIM, NOPE_DIM), F32), jnp.eye(ROPE_DIM, dtype=F32)], axis=1), LANES),
                (1, H_C)).astype(BF16)
            gq = mla_q_norm[j][None, :]
            gkv = mla_kv_norm[j][None, :]
            wout = w_out_c[j].astype(BF16)

            q, ckv, kr = _proj_c(xp, win, gq, gkv, wuq, _rope_tables(pos_p, ROPE_DIM, NOPE_DIM),
                                 _rope_tables(pos_p, ROPE_DIM, 0))
            kc, vct = _kv_up(ckv.reshape(bp * tp, KV_RANK), kr.reshape(bp * tp, ROPE_DIM), wk, place, wvt)
            per_seq = lambda a, nb: a.reshape((nb, a.shape[0] // nb) + a.shape[1:])
            oc = _attention("mla", q, per_seq(kc, bp), per_seq(vct, bp), (), n_pairs=H_C // 2,
                            mask_shift=int(math.log2(CHUNK)))
            out_c_p.append((ckv, kr))
            xp2 = _outproj_ln(xp.reshape(bp * tp, d), [oc.reshape(bp * tp, -1)], [wout],
                              ln1_g[i][None, :], ln1_b[i][None, :], alpha)

            q, ckv, kr = _proj_c(xs, win, gq, gkv, wuq, _rope_tables(pos_s, ROPE_DIM, NOPE_DIM),
                                 _rope_tables(pos_s, ROPE_DIM, 0))
            w_abs = jnp.zeros((H_C, LANES, 2 * LANES), F32)
            w_abs = w_abs.at[:, :NOPE_DIM, :KV_RANK].set(jnp.transpose(wukv[:, :, :NOPE_DIM], (1, 2, 0)))
            w_abs = w_abs.at[:, NOPE_DIM:NOPE_DIM + ROPE_DIM, KV_RANK:KV_RANK + ROPE_DIM].set(
                jnp.eye(ROPE_DIM, dtype=F32))
            w_vout = jnp.zeros((H_C, KV_RANK, H_C * V_DIM_C), F32)
            for hd in range(H_C):
                w_vout = w_vout.at[hd, :, hd * V_DIM_C:(hd + 1) * V_DIM_C].set(wukv[:, hd, NOPE_DIM:])
            wide = lambda a: jnp.pad(a.astype(F32), ((0, 0), (0, 0), (0, LANES - ROPE_DIM)))
            oc = _mla_decode(q.reshape(bs, ts, -1), cache_mla_ckv[j].astype(F32), wide(cache_mla_krope[j]),
                             ckv.reshape(bs, ts, KV_RANK), wide(kr.reshape(bs, ts, ROPE_DIM)),
                             w_abs.astype(BF16), w_vout.astype(BF16))
            out_c_s.append((ckv.reshape(bs, ts, KV_RANK), kr.reshape(bs, ts, ROPE_DIM)))
            xs2 = _outproj_ln(xs.reshape(ns, d), [oc.reshape(ns, -1)], [wout],
                              ln1_g[i][None, :], ln1_b[i][None, :], alpha)

        wr = _pad_cols(jnp.concatenate(
            [moe_wg[i]] + [moe_we[i][gi] for gi in range(N_GROUPS)], axis=1), LANES)
        wrh = wr.astype(BF16)
        wrl = (wr - wrh.astype(F32)).astype(BF16)
        br = _pad_cols(jnp.concatenate([moe_bg[i], moe_be[i].reshape(-1)])[None, :], LANES)
        moe_w = (moe_w1[i], moe_w3[i], moe_w2[i])
        g2, b2 = ln2_g[i][None, :], ln2_b[i][None, :]
        xp = _moe_ln(xp2, wrh, wrl, br, *moe_w, g2, b2, alpha).reshape(bp, tp, d)
        xs = _moe_ln(xs2, wrh, wrl, br, *moe_w, g2, b2, alpha).reshape(1, ns, d)

    stack = lambda rows, n: jnp.stack([r[n] for r in rows])
    return (xp, xs.reshape(bs, ts, d),
            stack(out_ab_p, 0), stack(out_ab_p, 1), stack(out_ab_p, 2), stack(out_ab_p, 3), stack(out_ab_p, 4),
            stack(out_c_p, 0), stack(out_c_p, 1),
            stack(out_ab_s, 0), stack(out_ab_s, 1), stack(out_ab_s, 2), stack(out_ab_s, 3), stack(out_ab_s, 4),
            stack(out_c_s, 0), stack(out_c_s, 1))
```

```python
import functools
import math

import jax
import jax.numpy as jnp
from jax import lax
from jax.experimental import pallas as pl
from jax.experimental.pallas import tpu as pltpu

F32 = jnp.float32
BF16 = jnp.bfloat16

D_MODEL = 1024
CHUNK = 64
HEAD_DIM = 64
ROPE_THETA = 10000.0
H_A = 8
H_B = 4
H_C = 16
Q_RANK = 256
KV_RANK = 128
NOPE_DIM = 64
ROPE_DIM = 32
V_DIM_C = 64
N_GROUPS = 4
EXPERTS_PER_GROUP = 4
N_EXPERTS = N_GROUPS * EXPERTS_PER_GROUP
D_EXPERT = 256
A_WIDTH = H_A * HEAD_DIM
B_QK_WIDTH = H_B * 2 * HEAD_DIM
B_V_WIDTH = H_B * 2 * HEAD_DIM
FGATE_BIAS = 3.0
LN_EPS = 1e-5
RMS_EPS = 1e-6
NEG_INF = -1e30
LOG2E = math.log2(math.e)

LANES = 128
BF16_ROWS = 16
PANEL = {"fox": 1024, "diff": 256, "mla": 1024}
BIAS_PIECES = 3
VMEM_LIMIT = 48 * 1024 * 1024
ATTN_BLOCK = 512
ROW_TILE = 512
MOE_TILE = 1024
BIAS_ROWS_PER_STEP = 4096
GATE_LANE0 = N_GROUPS


def _cparams(sem):
    return pltpu.CompilerParams(dimension_semantics=sem, vmem_limit_bytes=VMEM_LIMIT)


def _rope3(x, c, s1, s2, shift_up, shift_down):
    return x * c + pltpu.roll(x, shift_up, 1) * s1 + pltpu.roll(x, shift_down, 1) * s2


def _layer_norm(y, g, b):
    mu = jnp.mean(y, axis=-1, keepdims=True)
    d = y - mu
    var = jnp.mean(d * d, axis=-1, keepdims=True)
    return d * lax.rsqrt(var + LN_EPS) * g + b


def _split3(x):
    hi = x.astype(BF16)
    r1 = x - hi.astype(F32)
    mid = r1.astype(BF16)
    return hi, mid, (r1 - mid.astype(F32)).astype(BF16)


def _proj_ab_kernel(x_ref, w_ref, wvt_ref, wf_ref, bf_ref, c_ref, s1_ref, s2_ref,
                    qa_ref, ka_ref, kab_ref, va_ref, vat_ref, lf_ref, lfw_ref,
                    qb_ref, kb_ref, kbb_ref, vb_ref, vbt_ref):
    xb = x_ref[0].astype(BF16)

    def mm(i):
        return jnp.dot(xb, w_ref[i], preferred_element_type=F32)

    def mm_t(i):
        return lax.dot_general(wvt_ref[i], xb, (((1,), (1,)), ((), ())), preferred_element_type=F32)

    qa_ref[0] = (mm(0) * (HEAD_DIM ** -0.5 * LOG2E)).astype(BF16)
    ka = mm(1)
    ka_ref[0] = ka
    kab_ref[0] = ka.astype(BF16)
    va_ref[0] = mm(2)
    vat_ref[0, 0] = mm_t(0).astype(BF16)

    z = jnp.dot(xb, wf_ref[...], preferred_element_type=F32) + bf_ref[...]
    lf = jnp.minimum(z, 0.0) - jnp.log1p(jnp.exp(-jnp.abs(z)))
    lf_ref[0] = lf[:, :H_A]
    lfw_ref[0] = lf

    c, s1, s2 = c_ref[...], s1_ref[...], s2_ref[...]
    qb = mm(3)
    kb = mm(4)
    for s in range(B_QK_WIDTH // LANES):
        sl = slice(s * LANES, (s + 1) * LANES)
        qs = _rope3(qb[:, sl], c, s1, s2, LANES - HEAD_DIM // 2, HEAD_DIM // 2)
        qb_ref[0, :, sl] = (qs * (HEAD_DIM ** -0.5 * LOG2E)).astype(BF16)
        ks = _rope3(kb[:, sl], c, s1, s2, LANES - HEAD_DIM // 2, HEAD_DIM // 2)
        kb_ref[0, :, sl] = ks
        kbb_ref[0, :, sl] = ks.astype(BF16)
    vb_ref[0] = mm(5)
    vbt_ref[0, 0] = mm_t(1).astype(BF16)


def _proj_ab(x, w6, wvt, wf, bf, tabs):
    nb, t, _ = x.shape
    tm = min(ROW_TILE, t)
    assert t % tm == 0
    w = A_WIDTH
    tok = lambda width: pl.BlockSpec((1, tm, width), lambda b, i: (b, i, 0))
    tr = pl.BlockSpec((1, 1, w, tm), lambda b, i: (b, i, 0, 0))
    tab = pl.BlockSpec((tm, LANES), lambda b, i: (i, 0))
    full = lambda a: pl.BlockSpec(a.shape, lambda b, i: (0,) * a.ndim)
    sds = lambda width, dt: jax.ShapeDtypeStruct((nb, t, width), dt)
    sds_t = jax.ShapeDtypeStruct((nb, t // tm, w, tm), BF16)
    return pl.pallas_call(
        _proj_ab_kernel,
        grid=(nb, t // tm),
        in_specs=[tok(D_MODEL), full(w6), full(wvt), full(wf), full(bf), tab, tab, tab],
        out_specs=[tok(w), tok(w), tok(w), tok(w), tr, tok(H_A), tok(LANES), tok(w), tok(w), tok(w), tok(w), tr],
        out_shape=[sds(w, BF16), sds(w, F32), sds(w, BF16), sds(w, F32), sds_t, sds(H_A, F32), sds(LANES, F32),
                   sds(w, BF16), sds(w, F32), sds(w, BF16), sds(w, F32), sds_t],
        compiler_params=_cparams(("parallel", "parallel")),
        name="proj_ab",
    )(x, w6, wvt, wf, bf, *tabs)


def _decay_bias_kernel(lf_ref, spread_ref, lower_ref, o_ref, carry_ref):
    @pl.when(pl.program_id(1) == 0)
    def _():
        carry_ref[...] = jnp.zeros_like(carry_ref)

    spread = spread_ref[...]
    lower = lower_ref[...]
    tc = lower.shape[0]
    lane = lax.broadcasted_iota(jnp.int32, (1, LANES), 1).astype(F32)
    piece = lane - BIAS_PIECES * jnp.floor((lane + 0.5) * (1.0 / BIAS_PIECES))
    carry = carry_ref[...]
    for r in range(lf_ref.shape[1] // tc):
        rows = slice(r * tc, (r + 1) * tc)
        x = lf_ref[0, rows, :]
        xr = sum(jnp.dot(p, spread, preferred_element_type=F32) for p in _split3(x))
        c = sum(jnp.dot(lower, p, preferred_element_type=F32) for p in _split3(xr)) + carry
        carry = c[tc - 1:tc, :]
        hi, mid, lo = (p.astype(F32) for p in _split3(c * (-LOG2E)))
        o_ref[0, rows, :] = jnp.where(piece == 0.0, hi, jnp.where(piece == 1.0, mid, lo)).astype(BF16)
    carry_ref[...] = carry


def _decay_bias(lf_wide):
    nb, t, _ = lf_wide.shape
    tc = min(ATTN_BLOCK, t)
    tb = min(BIAS_ROWS_PER_STEP, t)
    assert t % tb == 0 and tb % tc == 0
    spec = pl.BlockSpec((1, tb, LANES), lambda b, i: (b, i, 0))
    src = jnp.arange(LANES)[:, None]
    dst = jnp.arange(LANES)[None, :]
    spread = ((dst // BIAS_PIECES == src) & (src < H_A)).astype(BF16)
    lower = jnp.tril(jnp.ones((tc, tc), BF16))
    const = lambda a: pl.BlockSpec(a.shape, lambda b, i: (0, 0))
    return pl.pallas_call(
        _decay_bias_kernel,
        grid=(nb, t // tb),
        in_specs=[spec, const(spread), const(lower)],
        out_specs=spec,
        out_shape=jax.ShapeDtypeStruct((nb, t, LANES), BF16),
        scratch_shapes=[pltpu.VMEM((1, LANES), F32)],
        compiler_params=_cparams(("parallel", "arbitrary")),
        name="cumsum",
    )(lf_wide, spread, lower)


def _attn_kernel(*refs, mode, tq, tk, mask_shift, lam_init):
    if mode == "diff":
        q_ref, k_ref, vt_ref, lq1_ref, lk1_ref, lq2_ref, lk2_ref, sub_ref, o_ref = refs[:9]
    elif mode == "fox":
        q_ref, k_ref, vt_ref, b_ref, o_ref = refs[:5]
    else:
        q_ref, k_ref, vt_ref, o_ref = refs[:4]
    m_sc, acc_sc, sa, bma, sb, bmb = refs[-6:]
    v_rows = LANES if mode == "diff" else HEAD_DIM
    sa_sc, sb_sc = (sa, bma), (sb, bmb)

    qi = pl.program_id(2)
    q = q_ref[0]
    lane = lax.broadcasted_iota(jnp.int32, (1, LANES), 1)
    if mode == "mla":
        qs = [q[:, :LANES], q[:, LANES:]]
    else:
        zero = jnp.zeros_like(q)
        qs = [jnp.where(lane < HEAD_DIM, q, zero), jnp.where(lane >= HEAD_DIM, q, zero)]
        if mode == "fox":
            def pick(i):
                lo = BIAS_PIECES * (2 * pl.program_id(1) + i)
                hot = jnp.where((lane >= lo) & (lane < lo + BIAS_PIECES), 1.0, 0.0)
                return jnp.broadcast_to(hot, (tq, LANES)).astype(BF16)

            qs = [jnp.concatenate([qs[i], pick(i)], axis=1) for i in range(2)]

    m_sc[...] = jnp.full(m_sc.shape, NEG_INF, F32)
    acc_sc[...] = jnp.zeros(acc_sc.shape, F32)

    pw = min(PANEL[mode], tq)

    def scores(j, bufs, c):
        s_sc, bm_sc = bufs
        cs = slice(c * pw, (c + 1) * pw)
        k = k_ref[0, j]
        if mode == "fox":
            k = jnp.concatenate([k, b_ref[0, j]], axis=1)
        for i in range(2):
            ki = k[:, i * LANES:(i + 1) * LANES] if mode == "mla" else k
            st = lax.dot_general(ki, qs[i][cs], (((1,), (1,)), ((), ())), preferred_element_type=F32)
            s_sc[i, c] = st
            bm_sc[i, c] = jnp.max(st, axis=0, keepdims=True)

    def consume(j, bufs, c, diag):
        s_sc, bm_sc = bufs
        cs = slice(c * pw, (c + 1) * pw)
        vt = vt_ref[0, j]
        for i in range(2):
            st = s_sc[i, c]
            if diag is not None:
                key = lax.broadcasted_iota(jnp.int32, (tk, pw), 0) + diag * tk
                qry = lax.broadcasted_iota(jnp.int32, (tk, pw), 1) + c * pw
                vis = lax.shift_right_logical(key, mask_shift) <= lax.shift_right_logical(qry, mask_shift)
                st = jnp.where(vis, st, NEG_INF)
                blk_max = jnp.max(st, axis=0, keepdims=True)
            else:
                blk_max = bm_sc[i, c]
            m_prev = m_sc[i, c]
            m_new = jnp.maximum(m_prev, blk_max)
            alpha = jnp.exp2(m_prev - m_new)
            p = jnp.exp2(st - m_new).astype(BF16)
            vi = vt if mode == "diff" else vt[i * HEAD_DIM:(i + 1) * HEAD_DIM]
            vi = jnp.concatenate([vi, jnp.ones((BF16_ROWS, tk), BF16)], axis=0)
            acc_sc[i, c] = alpha * acc_sc[i, c] + jnp.dot(vi, p, preferred_element_type=F32)
            m_sc[i, c] = m_new

    def stage(nxt, cur, diag=None):
        for c in range(tq // pw):
            if nxt is not None:
                scores(nxt[0], nxt[1], c)
            consume(cur[0], cur[1], c, diag)

    nfull = 2 * qi

    def pair(jj, carry):
        stage((2 * jj + 1, sb_sc), (2 * jj, sa_sc))
        stage((2 * jj + 2, sa_sc), (2 * jj + 1, sb_sc))
        return carry

    for c in range(tq // pw):
        scores(0, sa_sc, c)
    lax.fori_loop(0, qi, pair, 0)
    stage((nfull + 1, sb_sc), (nfull, sa_sc), diag=0)
    stage(None, (nfull + 1, sb_sc), diag=1)

    if mode == "diff":
        lam = (jnp.exp(jnp.sum(lq1_ref[...] * lk1_ref[...], axis=1, keepdims=True))
               - jnp.exp(jnp.sum(lq2_ref[...] * lk2_ref[...], axis=1, keepdims=True)) + lam_init)
    for c in range(tq // pw):
        o0 = acc_sc[0, c, :v_rows] / acc_sc[0, c, v_rows:v_rows + 1]
        o1 = acc_sc[1, c, :v_rows] / acc_sc[1, c, v_rows:v_rows + 1]
        if mode == "diff":
            o = o0 - lam * o1
            ms = jnp.mean(o * o, axis=0, keepdims=True)
            o = (o * lax.rsqrt(ms + RMS_EPS)).T * sub_ref[...] * (1.0 - lam_init)
        else:
            o = jnp.concatenate([o0, o1], axis=0).T
        o_ref[0, c * pw:(c + 1) * pw, :] = o.astype(o_ref.dtype)


def _attention(mode, q, k, vt, extra, *, n_pairs, mask_shift, lam_init=0.0):
    nb, t_q, _ = q.shape
    _, nkb, tk, _ = k.shape
    tq = 2 * tk
    assert t_q % tq == 0 and nkb == t_q // tk
    pw = min(PANEL[mode], tq)
    n_pan = tq // pw
    qw = 2 * LANES if mode == "mla" else LANES
    in_specs = [
        pl.BlockSpec((1, tq, qw), lambda b, p, i: (b, i, p)),
        pl.BlockSpec((1, nkb, tk, qw), lambda b, p, i: (b, 0, 0, p)),
        pl.BlockSpec((1, nkb, LANES, tk), lambda b, p, i: (b, 0, p, 0)),
    ]
    if mode == "fox":
        in_specs.append(pl.BlockSpec((1, nkb, tk, LANES), lambda b, p, i: (b, 0, 0, 0)))
    elif mode == "diff":
        in_specs += [pl.BlockSpec(a.shape, lambda b, p, i: (0, 0)) for a in extra]
    kern = functools.partial(_attn_kernel, mode=mode, tq=tq, tk=tk, mask_shift=mask_shift, lam_init=lam_init)
    return pl.pallas_call(
        kern,
        grid=(nb, n_pairs, t_q // tq),
        in_specs=in_specs,
        out_specs=pl.BlockSpec((1, tq, LANES), lambda b, p, i: (b, i, p)),
        out_shape=jax.ShapeDtypeStruct((nb, t_q, n_pairs * LANES), BF16),
        scratch_shapes=[pltpu.VMEM((2, n_pan, 1, pw), F32),
                        pltpu.VMEM((2, n_pan, (LANES if mode == "diff" else HEAD_DIM) + BF16_ROWS, pw), F32),
                        pltpu.VMEM((2, n_pan, tk, pw), F32), pltpu.VMEM((2, n_pan, 1, pw), F32),
                        pltpu.VMEM((2, n_pan, tk, pw), F32), pltpu.VMEM((2, n_pan, 1, pw), F32)],
        compiler_params=_cparams(("parallel", "parallel", "arbitrary")),
        name="attn_" + mode,
    )(q, k, vt, *extra)


def _decode_attn_kernel(*refs, mode, lam_init):
    if mode == "fox":
        q_ref, kc_ref, vc_ref, kn_ref, vn_ref, b_ref, o_ref = refs
    else:
        q_ref, kc_ref, vc_ref, kn_ref, vn_ref, lq1_ref, lk1_ref, lq2_ref, lk2_ref, sub_ref, o_ref = refs
        lam = (jnp.exp(jnp.sum(lq1_ref[...] * lk1_ref[...], axis=1, keepdims=True))
               - jnp.exp(jnp.sum(lq2_ref[...] * lk2_ref[...], axis=1, keepdims=True)) + lam_init)
    ts = q_ref.shape[1]
    past = kc_ref.shape[1]
    lane = lax.broadcasted_iota(jnp.int32, (1, LANES), 1)
    row = lax.broadcasted_iota(jnp.int32, (ts, ts), 0)
    col = lax.broadcasted_iota(jnp.int32, (ts, ts), 1)
    nt = (((1,), (1,)), ((), ()))
    for p in range(q_ref.shape[2] // LANES):
        sl = slice(p * LANES, (p + 1) * LANES)
        q = q_ref[0, :, sl]
        kc = kc_ref[0, :, sl].astype(BF16)
        kn = kn_ref[0, :, sl]
        vc = vc_ref[0, :, sl].astype(BF16)
        vn = vn_ref[0, :, sl].astype(BF16)
        if mode == "fox":
            kc = jnp.concatenate([kc, b_ref[0, :past, :]], axis=1)
            kn = jnp.concatenate([kn, b_ref[0, past:past + ts, :]], axis=1)
        zero = jnp.zeros_like(q)
        outs = []
        for i in range(2):
            qi = jnp.where(lane < HEAD_DIM, q, zero) if i == 0 else jnp.where(lane >= HEAD_DIM, q, zero)
            if mode == "fox":
                lo = BIAS_PIECES * (2 * p + i)
                hot = jnp.where((lane >= lo) & (lane < lo + BIAS_PIECES), 1.0, 0.0)
                qi = jnp.concatenate([qi, jnp.broadcast_to(hot, (ts, LANES)).astype(BF16)], axis=1)
            sc = lax.dot_general(qi, kc, nt, preferred_element_type=F32)
            sn = lax.dot_general(qi, kn, nt, preferred_element_type=F32)
            if mode == "fox":
                sn = jnp.where(col <= row, sn, NEG_INF)
            m = jnp.maximum(jnp.max(sc, axis=1, keepdims=True), jnp.max(sn, axis=1, keepdims=True))
            pc = jnp.exp2(sc - m)
            pn = jnp.exp2(sn - m)
            l = jnp.sum(pc, axis=1, keepdims=True) + jnp.sum(pn, axis=1, keepdims=True)
            outs.append((jnp.dot(pc.astype(BF16), vc, preferred_element_type=F32)
                         + jnp.dot(pn.astype(BF16), vn, preferred_element_type=F32)) / l)
        if mode == "fox":
            o = jnp.where(lane < HEAD_DIM, outs[0], outs[1])
        else:
            o = outs[0] - lam * outs[1]
            ms = jnp.mean(o * o, axis=1, keepdims=True)
            o = o * lax.rsqrt(ms + RMS_EPS) * sub_ref[...] * (1.0 - lam_init)
        o_ref[0, :, sl] = o.astype(o_ref.dtype)


def _decode_attention(mode, q, k_cache, v_cache, k_new, v_new, extra, lam_init=0.0):
    nb, ts, w = q.shape
    assert k_cache.shape[1] % CHUNK == 0 and ts <= CHUNK
    per_b = lambda a: pl.BlockSpec((1,) + a.shape[1:], lambda b: (b, 0, 0))
    if mode == "fox":
        extra_specs = [per_b(extra[0])]
    else:
        extra_specs = [pl.BlockSpec(a.shape, lambda b: (0, 0)) for a in extra]
    return pl.pallas_call(
        functools.partial(_decode_attn_kernel, mode=mode, lam_init=lam_init),
        grid=(nb,),
        in_specs=[per_b(q), per_b(k_cache), per_b(v_cache), per_b(k_new), per_b(v_new)] + extra_specs,
        out_specs=pl.BlockSpec((1, ts, w), lambda b: (b, 0, 0)),
        out_shape=jax.ShapeDtypeStruct((nb, ts, w), BF16),
        compiler_params=_cparams(("parallel",)),
        name="decode_" + mode,
    )(q, k_cache, v_cache, k_new, v_new, *extra)


def _outproj_ln_kernel(*refs, n_in, alpha):
    x_ref = refs[0]
    o_refs = refs[1:1 + n_in]
    w_refs = refs[1 + n_in:1 + 2 * n_in]
    g_ref, b_ref, y_ref = refs[1 + 2 * n_in:]
    mix = jnp.dot(o_refs[0][...], w_refs[0][...], preferred_element_type=F32)
    for o_r, w_r in zip(o_refs[1:], w_refs[1:]):
        mix = mix + jnp.dot(o_r[...], w_r[...], preferred_element_type=F32)
    y_ref[...] = _layer_norm(alpha * x_ref[...] + mix, g_ref[...], b_ref[...])


def _outproj_ln(x, outs, ws, g, b, alpha):
    n, d = x.shape
    tm = min(ROW_TILE, n)
    assert n % tm == 0
    row = lambda width: pl.BlockSpec((tm, width), lambda i: (i, 0))
    full = lambda a: pl.BlockSpec(a.shape, lambda i: (0, 0))
    return pl.pallas_call(
        functools.partial(_outproj_ln_kernel, n_in=len(outs), alpha=alpha),
        grid=(n // tm,),
        in_specs=[row(d)] + [row(o.shape[1]) for o in outs] + [full(w) for w in ws] + [full(g), full(b)],
        out_specs=row(d),
        out_shape=jax.ShapeDtypeStruct((n, d), F32),
        compiler_params=_cparams(("parallel",)),
        name="outproj_ln",
    )(x, *outs, *ws, g, b)


def _route(logits):
    lane = lax.broadcasted_iota(jnp.int32, logits.shape, 1).astype(F32)
    big = float(1 << 20)
    is_g = lane < N_GROUPS
    lg = jnp.where(is_g, logits, NEG_INF)
    eg = jnp.where(is_g, jnp.exp(lg - jnp.max(lg, axis=1, keepdims=True)), 0.0)
    pg = eg / jnp.sum(eg, axis=1, keepdims=True)
    p_g = jnp.max(pg, axis=1, keepdims=True)
    gidx = jnp.min(jnp.where(is_g & (pg == p_g), lane, big), axis=1, keepdims=True)
    lo = GATE_LANE0 + EXPERTS_PER_GROUP * gidx
    sel = (lane >= lo) & (lane < lo + EXPERTS_PER_GROUP)
    le = jnp.where(sel, logits, NEG_INF)
    ee = jnp.where(sel, jnp.exp(le - jnp.max(le, axis=1, keepdims=True)), 0.0)
    pe = ee / jnp.sum(ee, axis=1, keepdims=True)
    v1 = jnp.max(jnp.where(sel, pe, -1.0), axis=1, keepdims=True)
    i1 = jnp.min(jnp.where(sel & (pe == v1), lane, big), axis=1, keepdims=True)
    rest = sel & (lane != i1)
    v2 = jnp.max(jnp.where(rest, pe, -1.0), axis=1, keepdims=True)
    i2 = jnp.min(jnp.where(rest & (pe == v2), lane, big), axis=1, keepdims=True)
    tot = v1 + v2
    w1 = v1 / tot * p_g
    w2 = v2 / tot * p_g
    return jnp.where(lane == i1, w1, jnp.where(lane == i2, w2, 0.0))


def _moe_ln_kernel(x_ref, wrh_ref, wrl_ref, br_ref, w1_ref, w3_ref, w2_ref, g_ref, b_ref, y_ref,
                   xb_sc, gate_sc, acc_sc, *, alpha):
    e = pl.program_id(1)

    @pl.when(e == 0)
    def _():
        x = x_ref[...]
        xh = x.astype(BF16)
        xl = (x - xh.astype(F32)).astype(BF16)
        xb_sc[...] = xh
        logits = (jnp.dot(xh, wrh_ref[...], preferred_element_type=F32)
                  + jnp.dot(xl, wrh_ref[...], preferred_element_type=F32)
                  + jnp.dot(xh, wrl_ref[...], preferred_element_type=F32) + br_ref[...])
        gate_sc[...] = _route(logits)
        acc_sc[...] = jnp.zeros_like(acc_sc)

    xb = xb_sc[...]
    h1 = jnp.dot(xb, w1_ref[0].astype(BF16), preferred_element_type=F32)
    h3 = jnp.dot(xb, w3_ref[0].astype(BF16), preferred_element_type=F32)
    hdn = (h1 * jax.nn.sigmoid(h1)) * h3
    y = jnp.dot(hdn.astype(BF16), w2_ref[0].astype(BF16), preferred_element_type=F32)
    lane = lax.broadcasted_iota(jnp.int32, (1, LANES), 1)
    ge = jnp.sum(jnp.where(lane == e + GATE_LANE0, gate_sc[...], 0.0), axis=1, keepdims=True)
    acc_sc[...] += ge * y

    @pl.when(e == pl.num_programs(1) - 1)
    def _():
        y_ref[...] = _layer_norm(alpha * x_ref[...] + acc_sc[...], g_ref[...], b_ref[...])


def _moe_ln(x, wrh, wrl, br, w1, w3, w2, g, b, alpha):
    n, d = x.shape
    tm = min(MOE_TILE, n)
    assert n % tm == 0
    ne = w1.shape[0]
    per_expert = lambda a: pl.BlockSpec((1,) + a.shape[1:], lambda i, e: (e, 0, 0))
    full = lambda a: pl.BlockSpec(a.shape, lambda i, e: (0, 0))
    return pl.pallas_call(
        functools.partial(_moe_ln_kernel, alpha=alpha),
        grid=(n // tm, ne),
        in_specs=[pl.BlockSpec((tm, d), lambda i, e: (i, 0)), full(wrh), full(wrl), full(br),
                  per_expert(w1), per_expert(w3), per_expert(w2),
                  full(g), full(b)],
        out_specs=pl.BlockSpec((tm, d), lambda i, e: (i, 0)),
        out_shape=jax.ShapeDtypeStruct((n, d), F32),
        scratch_shapes=[pltpu.VMEM((tm, d), BF16), pltpu.VMEM((tm, LANES), F32), pltpu.VMEM((tm, d), F32)],
        compiler_params=_cparams(("parallel", "arbitrary")),
        name="moe_ln",
    )(x, wrh, wrl, br, w1, w3, w2, g, b)


def _proj_c_kernel(x_ref, win_ref, gq_ref, gkv_ref, wuq_ref, cq_ref, s1q_ref, s2q_ref,
                   ck_ref, s1k_ref, s2k_ref, q_ref, ckv_ref, kr_ref):
    xb = x_ref[0].astype(BF16)
    h = jnp.dot(xb, win_ref[...], preferred_element_type=F32)
    qa = h[:, :Q_RANK]
    kva = h[:, Q_RANK:Q_RANK + KV_RANK]
    krw = h[:, Q_RANK + KV_RANK:]
    qn = qa * lax.rsqrt(jnp.mean(qa * qa, axis=1, keepdims=True) + RMS_EPS) * gq_ref[...]
    ckv_ref[0] = kva * lax.rsqrt(jnp.mean(kva * kva, axis=1, keepdims=True) + RMS_EPS) * gkv_ref[...]
    half = ROPE_DIM // 2
    kr = _rope3(krw, ck_ref[...], s1k_ref[...], s2k_ref[...], LANES - half, half)
    kr_ref[0] = kr[:, :ROPE_DIM]
    q = jnp.dot(qn.astype(BF16), wuq_ref[...], preferred_element_type=F32)
    cq, s1q, s2q = cq_ref[...], s1q_ref[...], s2q_ref[...]
    scale = (NOPE_DIM + ROPE_DIM) ** -0.5 * LOG2E
    for hd in range(H_C):
        sl = slice(hd * LANES, (hd + 1) * LANES)
        q_ref[0, :, sl] = (_rope3(q[:, sl], cq, s1q, s2q, LANES - half, half) * scale).astype(BF16)


def _proj_c(x, win, gq, gkv, wuq, tabs_q, tabs_k):
    nb, t, _ = x.shape
    tm = min(ROW_TILE, t)
    assert t % tm == 0
    tok = lambda width: pl.BlockSpec((1, tm, width), lambda b, i: (b, i, 0))
    tab = pl.BlockSpec((tm, LANES), lambda b, i: (i, 0))
    full = lambda a: pl.BlockSpec(a.shape, lambda b, i: (0, 0))
    return pl.pallas_call(
        _proj_c_kernel,
        grid=(nb, t // tm),
        in_specs=[tok(D_MODEL), full(win), full(gq), full(gkv), full(wuq)] + [tab] * 6,
        out_specs=[tok(H_C * LANES), tok(KV_RANK), tok(ROPE_DIM)],
        out_shape=[jax.ShapeDtypeStruct((nb, t, H_C * LANES), BF16),
                   jax.ShapeDtypeStruct((nb, t, KV_RANK), F32),
                   jax.ShapeDtypeStruct((nb, t, ROPE_DIM), F32)],
        compiler_params=_cparams(("parallel", "parallel")),
        name="proj_c",
    )(x, win, gq, gkv, wuq, *tabs_q, *tabs_k)


def _kv_up_kernel(ckv_ref, kr_ref, wk_ref, place_ref, wvt_ref, k_ref, vt_ref):
    cb = ckv_ref[...].astype(BF16)
    k = (jnp.dot(cb, wk_ref[...], preferred_element_type=F32)
         + jnp.dot(kr_ref[...].astype(BF16), place_ref[...], preferred_element_type=F32))
    k_ref[0] = k.astype(BF16)
    vt = lax.dot_general(wvt_ref[...], cb, (((1,), (1,)), ((), ())), preferred_element_type=F32)
    vt_ref[0] = vt.astype(BF16)


def _kv_up(ckv, kr, wk, place, wvt):
    n = ckv.shape[0]
    tm = ATTN_BLOCK
    assert n % tm == 0
    row = lambda width: pl.BlockSpec((tm, width), lambda i: (i, 0))
    full = lambda a: pl.BlockSpec(a.shape, lambda i: (0, 0))
    return pl.pallas_call(
        _kv_up_kernel,
        grid=(n // tm,),
        in_specs=[row(KV_RANK), row(ROPE_DIM), full(wk), full(place), full(wvt)],
        out_specs=[pl.BlockSpec((1, tm, H_C * LANES), lambda i: (i, 0, 0)),
                   pl.BlockSpec((1, H_C * V_DIM_C, tm), lambda i: (i, 0, 0))],
        out_shape=[jax.ShapeDtypeStruct((n // tm, tm, H_C * LANES), BF16),
                   jax.ShapeDtypeStruct((n // tm, H_C * V_DIM_C, tm), BF16)],
        compiler_params=_cparams(("parallel",)),
        name="kv_up",
    )(ckv, kr, wk, place, wvt)


def _mla_decode_kernel(q_ref, ckv_ref, kr_ref, ckvn_ref, krn_ref, wabs_ref, wv_ref, o_ref):
    q = q_ref[0]
    ts = q.shape[0]
    qs = jnp.concatenate(
        [jnp.dot(q[:, h * LANES:(h + 1) * LANES], wabs_ref[h], preferred_element_type=F32).astype(BF16)
         for h in range(H_C)], axis=0)
    kc = jnp.concatenate([ckv_ref[0].astype(BF16), kr_ref[0].astype(BF16)], axis=1)
    kn = jnp.concatenate([ckvn_ref[0].astype(BF16), krn_ref[0].astype(BF16)], axis=1)
    nt = (((1,), (1,)), ((), ()))
    sc = lax.dot_general(qs, kc, nt, preferred_element_type=F32)
    sn = lax.dot_general(qs, kn, nt, preferred_element_type=F32)
    m = jnp.maximum(jnp.max(sc, axis=1, keepdims=True), jnp.max(sn, axis=1, keepdims=True))
    pc = jnp.exp2(sc - m)
    pn = jnp.exp2(sn - m)
    l = jnp.sum(pc, axis=1, keepdims=True) + jnp.sum(pn, axis=1, keepdims=True)
    ol = (jnp.dot(pc.astype(BF16), kc[:, :KV_RANK], preferred_element_type=F32)
          + jnp.dot(pn.astype(BF16), kn[:, :KV_RANK], preferred_element_type=F32)) / l
    olb = ol.astype(BF16)
    o = jnp.dot(olb[:ts], wv_ref[0], preferred_element_type=F32)
    for h in range(1, H_C):
        o = o + jnp.dot(olb[h * ts:(h + 1) * ts], wv_ref[h], preferred_element_type=F32)
    o_ref[0] = o.astype(o_ref.dtype)


def _mla_decode(q, ckv_c, kr_c, ckv_n, kr_n, w_abs, w_vout):
    nb, ts, _ = q.shape
    assert ckv_c.shape[1] % CHUNK == 0 and ts <= CHUNK
    per_b = lambda a: pl.BlockSpec((1,) + a.shape[1:], lambda b: (b, 0, 0))
    full = lambda a: pl.BlockSpec(a.shape, lambda b: (0, 0, 0))
    return pl.pallas_call(
        _mla_decode_kernel,
        grid=(nb,),
        in_specs=[per_b(q), per_b(ckv_c), per_b(kr_c), per_b(ckv_n), per_b(kr_n), full(w_abs), full(w_vout)],
        out_specs=pl.BlockSpec((1, ts, H_C * V_DIM_C), lambda b: (b, 0, 0)),
        out_shape=jax.ShapeDtypeStruct((nb, ts, H_C * V_DIM_C), BF16),
        compiler_params=_cparams(("parallel",)),
        name="mla_decode",
    )(q, ckv_c, kr_c, ckv_n, kr_n, w_abs, w_vout)


def _rope_tables(pos, dim, lane0):
    half = dim // 2
    inv = ROPE_THETA ** (-jnp.arange(0, dim, 2, dtype=F32) / dim)
    ang = pos.astype(F32)[:, None] * inv[None, :]
    cos, sin = jnp.cos(ang), jnp.sin(ang)
    zero = jnp.zeros_like(sin)
    c = jnp.concatenate([cos, cos], axis=1)
    s1 = jnp.concatenate([-sin, zero], axis=1)
    s2 = jnp.concatenate([zero, sin], axis=1)
    if lane0 < 0:
        reps = LANES // dim
        return tuple(jnp.tile(a, (1, reps)) for a in (c, s1, s2))
    t = pos.shape[0]
    pad = lambda a, fill: jnp.concatenate(
        [jnp.full((t, lane0), fill, F32), a, jnp.full((t, LANES - lane0 - dim), fill, F32)], axis=1)
    return pad(c, 1.0), pad(s1, 0.0), pad(s2, 0.0)


def _pad_cols(a, width):
    return jnp.pad(a, ((0, 0), (0, width - a.shape[1])))


def _blocks(a, tk):
    nb, t, l = a.shape
    return a.reshape(nb, t // tk, tk, l)


def _cat_pad_time(cache, new, t_pad):
    nb, t0, l = cache.shape
    t1 = new.shape[1]
    return jnp.concatenate([cache, new, jnp.zeros((nb, t_pad - t0 - t1, l), cache.dtype)], axis=1)


def kernel(x_prompt, x_sample, cache_fox_k, cache_fox_v, cache_fox_logf, cache_diff_k, cache_diff_v, cache_mla_ckv, cache_mla_krope, w_in_ab, b_fgate, diff_lq1, diff_lk1, diff_lq2, diff_lk2, diff_subln, w_out_ab, w_in_c, mla_q_norm, mla_kv_norm, mla_w_uq, mla_w_ukv, w_out_c, ln1_g, ln1_b, ln2_g, ln2_b, moe_wg, moe_bg, moe_we, moe_be, moe_w1, moe_w3, moe_w2):
    bp, tp, d = x_prompt.shape
    bs, ts, _ = x_sample.shape
    past = cache_fox_k.shape[2]
    depth = ln1_g.shape[0]
    alpha = (2 * depth) ** 0.25
    tk = ATTN_BLOCK
    assert past % tk == 0
    ns = bs * ts
    t_dec = past + tk

    pos_p = jnp.arange(tp)
    pos_s = jnp.tile(past + jnp.arange(ts), bs)

    xp = x_prompt
    xs = x_sample.reshape(1, ns, d)
    out_ab_p, out_ab_s, out_c_p, out_c_s = [], [], [], []

    for i in range(depth):
        j = i // 2
        if i % 2 == 0:
            lam_init = 0.8 - 0.6 * math.exp(-0.3 * i)
            cuts = [0, A_WIDTH, 2 * A_WIDTH, 3 * A_WIDTH, 3 * A_WIDTH + H_A,
                    3 * A_WIDTH + H_A + B_QK_WIDTH, 3 * A_WIDTH + H_A + 2 * B_QK_WIDTH,
                    3 * A_WIDTH + H_A + 2 * B_QK_WIDTH + B_V_WIDTH]
            w = w_in_ab[j]
            piece = lambda a: w[:, cuts[a]:cuts[a + 1]]
            w6 = jnp.stack([piece(0), piece(1), piece(2), piece(4), piece(5), piece(6)]).astype(BF16)
            wvt = jnp.stack([piece(2).T, piece(6).T]).astype(BF16)
            wf = _pad_cols(piece(3), LANES).astype(BF16)
            bf = _pad_cols(b_fgate[j][None, :], LANES)
            wout = w_out_ab[j].astype(BF16)
            diff_extra = (diff_lq1[j][None, :], diff_lk1[j][None, :], diff_lq2[j][None, :],
                          diff_lk2[j][None, :], diff_subln[j][None, :])

            tabs = _rope_tables(pos_p, HEAD_DIM, -1)
            (qa, ka, kab, va, vat, lf, lfw, qb, kb, kbb, vb, vbt) = _proj_ab(xp, w6, wvt, wf, bf, tabs)
            bias = _blocks(_decay_bias(lfw), tk)
            oa = _attention("fox", qa, _blocks(kab, tk), vat, (bias,), n_pairs=H_A // 2, mask_shift=0)
            ob = _attention("diff", qb, _blocks(kbb, tk), vbt, diff_extra,
                            n_pairs=H_B, mask_shift=int(math.log2(CHUNK)), lam_init=lam_init)
            out_ab_p.append((ka.reshape(bp, tp, H_A, HEAD_DIM), va.reshape(bp, tp, H_A, HEAD_DIM), lf,
                             kb.reshape(bp, tp, H_B, 2, HEAD_DIM), vb.reshape(bp, tp, H_B, 2 * HEAD_DIM)))
            xp2 = _outproj_ln(xp.reshape(bp * tp, d), [oa.reshape(bp * tp, -1), ob.reshape(bp * tp, -1)],
                              [wout[:A_WIDTH], wout[A_WIDTH:]], ln1_g[i][None, :], ln1_b[i][None, :], alpha)

            tabs = _rope_tables(pos_s, HEAD_DIM, -1)
            (qa, ka, kab, va, _, lf, lfw, qb, kb, kbb, vb, _) = _proj_ab(xs, w6, wvt, wf, bf, tabs)
            rs = lambda a: a.reshape(bs, ts, a.shape[-1])
            cache_lfw = jnp.pad(cache_fox_logf[j].astype(F32), ((0, 0), (0, 0), (0, LANES - H_A)))
            bias = _decay_bias(_cat_pad_time(cache_lfw, rs(lfw), t_dec))
            flat = lambda c: c.reshape(bs, past, -1)
            oa = _decode_attention("fox", rs(qa), flat(cache_fox_k[j]), flat(cache_fox_v[j]), rs(kab), rs(va),
                                   (bias,))
            ob = _decode_attention("diff", rs(qb), flat(cache_diff_k[j]), flat(cache_diff_v[j]), rs(kbb),
                                   rs(vb), diff_extra, lam_init=lam_init)
            out_ab_s.append((ka.reshape(bs, ts, H_A, HEAD_DIM), va.reshape(bs, ts, H_A, HEAD_DIM),
                             lf.reshape(bs, ts, H_A), kb.reshape(bs, ts, H_B, 2, HEAD_DIM),
                             vb.reshape(bs, ts, H_B, 2 * HEAD_DIM)))
            xs2 = _outproj_ln(xs.reshape(ns, d), [oa.reshape(ns, -1), ob.reshape(ns, -1)],
                              [wout[:A_WIDTH], wout[A_WIDTH:]], ln1_g[i][None, :], ln1_b[i][None, :], alpha)
        else:
            wc = w_in_c[j]
            kr_cols = _pad_cols(wc[:, Q_RANK + KV_RANK:], LANES)
            win = jnp.concatenate([wc[:, :Q_RANK + KV_RANK], kr_cols], axis=1).astype(BF16)
            wuq = jnp.pad(mla_w_uq[j].reshape(Q_RANK, H_C, NOPE_DIM + ROPE_DIM),
                          ((0, 0), (0, 0), (0, LANES - NOPE_DIM - ROPE_DIM))).reshape(Q_RANK, H_C * LANES)
            wuq = wuq.astype(BF16)
            wukv = mla_w_ukv[j].reshape(KV_RANK, H_C, NOPE_DIM + V_DIM_C)
            wk = jnp.pad(wukv[:, :, :NOPE_DIM], ((0, 0), (0, 0), (0, LANES - NOPE_DIM)))
            wk = wk.reshape(KV_RANK, H_C * LANES).astype(BF16)
            wvt = wukv[:, :, NOPE_DIM:].reshape(KV_RANK, H_C * V_DIM_C).T.astype(BF16)
            place = jnp.tile(_pad_cols(jnp.concatenate(
                [jnp.zeros((ROPE_DIM, NOPE_DIM), F32), jnp.eye(ROPE_DIM, dtype=F32)], axis=1), LANES),
                (1, H_C)).astype(BF16)
            gq = mla_q_norm[j][None, :]
            gkv = mla_kv_norm[j][None, :]
            wout = w_out_c[j].astype(BF16)

            q, ckv, kr = _proj_c(xp, win, gq, gkv, wuq, _rope_tables(pos_p, ROPE_DIM, NOPE_DIM),
                                 _rope_tables(pos_p, ROPE_DIM, 0))
            kc, vct = _kv_up(ckv.reshape(bp * tp, KV_RANK), kr.reshape(bp * tp, ROPE_DIM), wk, place, wvt)
            per_seq = lambda a, nb: a.reshape((nb, a.shape[0] // nb) + a.shape[1:])
            oc = _attention("mla", q, per_seq(kc, bp), per_seq(vct, bp), (), n_pairs=H_C // 2,
                            mask_shift=int(math.log2(CHUNK)))
            out_c_p.append((ckv, kr))
            xp2 = _outproj_ln(xp.reshape(bp * tp, d), [oc.reshape(bp * tp, -1)], [wout],
                              ln1_g[i][None, :], ln1_b[i][None, :], alpha)

            q, ckv, kr = _proj_c(xs, win, gq, gkv, wuq, _rope_tables(pos_s, ROPE_DIM, NOPE_DIM),
                                 _rope_tables(pos_s, ROPE_DIM, 0))
            w_abs = jnp.zeros((H_C, LANES, 2 * LANES), F32)
            w_abs = w_abs.at[:, :NOPE_DIM, :KV_RANK].set(jnp.transpose(wukv[:, :, :NOPE_DIM], (1, 2, 0)))
            w_abs = w_abs.at[:, NOPE_DIM:NOPE_DIM + ROPE_DIM, KV_RANK:KV_RANK + ROPE_DIM].set(
                jnp.eye(ROPE_DIM, dtype=F32))
            w_vout = jnp.zeros((H_C, KV_RANK, H_C * V_DIM_C), F32)
            for hd in range(H_C):
                w_vout = w_vout.at[hd, :, hd * V_DIM_C:(hd + 1) * V_DIM_C].set(wukv[:, hd, NOPE_DIM:])
            wide = lambda a: jnp.pad(a.astype(F32), ((0, 0), (0, 0), (0, LANES - ROPE_DIM)))
            oc = _mla_decode(q.reshape(bs, ts, -1), cache_mla_ckv[j].astype(F32), wide(cache_mla_krope[j]),
                             ckv.reshape(bs, ts, KV_RANK), wide(kr.reshape(bs, ts, ROPE_DIM)),
                             w_abs.astype(BF16), w_vout.astype(BF16))
            out_c_s.append((ckv.reshape(bs, ts, KV_RANK), kr.reshape(bs, ts, ROPE_DIM)))
            xs2 = _outproj_ln(xs.reshape(ns, d), [oc.reshape(ns, -1)], [wout],
                              ln1_g[i][None, :], ln1_b[i][None, :], alpha)

        wr = _pad_cols(jnp.concatenate(
            [moe_wg[i]] + [moe_we[i][gi] for gi in range(N_GROUPS)], axis=1), LANES)
        wrh = wr.astype(BF16)
        wrl = (wr - wrh.astype(F32)).astype(BF16)
        br = _pad_cols(jnp.concatenate([moe_bg[i], moe_be[i].reshape(-1)])[None, :], LANES)
        moe_w = (moe_w1[i], moe_w3[i], moe_w2[i])
        g2, b2 = ln2_g[i][None, :], ln2_b[i][None, :]
        xp = _moe_ln(xp2, wrh, wrl, br, *moe_w, g2, b2, alpha).reshape(bp, tp, d)
        xs = _moe_ln(xs2, wrh, wrl, br, *moe_w, g2, b2, alpha).reshape(1, ns, d)

    stack = lambda rows, n: jnp.stack([r[n] for r in rows])
    return (xp, xs.reshape(bs, ts, d),
            stack(out_ab_p, 0), stack(out_ab_p, 1), stack(out_ab_p, 2), stack(out_ab_p, 3), stack(out_ab_p, 4),
            stack(out_c_p, 0), stack(out_c_p, 1),
            stack(out_ab_s, 0), stack(out_ab_s, 1), stack(out_ab_s, 2), stack(out_ab_s, 3), stack(out_ab_s, 4),
            stack(out_c_s, 0), stack(out_c_s, 1))
```

```python
import functools
import math

import jax
import jax.numpy as jnp
from jax import lax
from jax.experimental import pallas as pl
from jax.experimental.pallas import tpu as pltpu

F32 = jnp.float32
BF16 = jnp.bfloat16

D_MODEL = 1024
CHUNK = 64
HEAD_DIM = 64
ROPE_THETA = 10000.0
H_A = 8
H_B = 4
H_C = 16
Q_RANK = 256
KV_RANK = 128
NOPE_DIM = 64
ROPE_DIM = 32
V_DIM_C = 64
N_GROUPS = 4
EXPERTS_PER_GROUP = 4
N_EXPERTS = N_GROUPS * EXPERTS_PER_GROUP
D_EXPERT = 256
A_WIDTH = H_A * HEAD_DIM
B_QK_WIDTH = H_B * 2 * HEAD_DIM
B_V_WIDTH = H_B * 2 * HEAD_DIM
FGATE_BIAS = 3.0
LN_EPS = 1e-5
RMS_EPS = 1e-6
NEG_INF = -1e30
LOG2E = math.log2(math.e)

LANES = 128
BF16_ROWS = 16
PANEL = {"fox": 1024, "diff": 256, "mla": 1024}
BIAS_PIECES = 3
VMEM_LIMIT = 48 * 1024 * 1024
ATTN_BLOCK = 512
ROW_TILE = 512
MOE_TILE = 1024
BIAS_ROWS_PER_STEP = 4096
GATE_LANE0 = N_GROUPS


def _cparams(sem):
    return pltpu.CompilerParams(dimension_semantics=sem, vmem_limit_bytes=VMEM_LIMIT)


def _rope3(x, c, s1, s2, shift_up, shift_down):
    return x * c + pltpu.roll(x, shift_up, 1) * s1 + pltpu.roll(x, shift_down, 1) * s2


def _layer_norm(y, g, b):
    mu = jnp.mean(y, axis=-1, keepdims=True)
    d = y - mu
    var = jnp.mean(d * d, axis=-1, keepdims=True)
    return d * lax.rsqrt(var + LN_EPS) * g + b


def _split3(x):
    hi = x.astype(BF16)
    r1 = x - hi.astype(F32)
    mid = r1.astype(BF16)
    return hi, mid, (r1 - mid.astype(F32)).astype(BF16)


def _proj_ab_kernel(x_ref, w_ref, wvt_ref, wf_ref, bf_ref, c_ref, s1_ref, s2_ref,
                    qa_ref, ka_ref, kab_ref, va_ref, vat_ref, lf_ref, lfw_ref,
                    qb_ref, kb_ref, kbb_ref, vb_ref, vbt_ref):
    xb = x_ref[0].astype(BF16)

    def mm(i):
        return jnp.dot(xb, w_ref[i], preferred_element_type=F32)

    def mm_t(i):
        return lax.dot_general(wvt_ref[i], xb, (((1,), (1,)), ((), ())), preferred_element_type=F32)

    qa_ref[0] = (mm(0) * (HEAD_DIM ** -0.5 * LOG2E)).astype(BF16)
    ka = mm(1)
    ka_ref[0] = ka
    kab_ref[0] = ka.astype(BF16)
    va_ref[0] = mm(2)
    vat_ref[0, 0] = mm_t(0).astype(BF16)

    z = jnp.dot(xb, wf_ref[...], preferred_element_type=F32) + bf_ref[...]
    lf = jnp.minimum(z, 0.0) - jnp.log1p(jnp.exp(-jnp.abs(z)))
    lf_ref[0] = lf[:, :H_A]
    lfw_ref[0] = lf

    c, s1, s2 = c_ref[...], s1_ref[...], s2_ref[...]
    qb = mm(3)
    kb = mm(4)
    for s in range(B_QK_WIDTH // LANES):
        sl = slice(s * LANES, (s + 1) * LANES)
        qs = _rope3(qb[:, sl], c, s1, s2, LANES - HEAD_DIM // 2, HEAD_DIM // 2)
        qb_ref[0, :, sl] = (qs * (HEAD_DIM ** -0.5 * LOG2E)).astype(BF16)
        ks = _rope3(kb[:, sl], c, s1, s2, LANES - HEAD_DIM // 2, HEAD_DIM // 2)
        kb_ref[0, :, sl] = ks
        kbb_ref[0, :, sl] = ks.astype(BF16)
    vb_ref[0] = mm(5)
    vbt_ref[0, 0] = mm_t(1).astype(BF16)


def _proj_ab(x, w6, wvt, wf, bf, tabs):
    nb, t, _ = x.shape
    tm = min(ROW_TILE, t)
    assert t % tm == 0
    w = A_WIDTH
    tok = lambda width: pl.BlockSpec((1, tm, width), lambda b, i: (b, i, 0))
    tr = pl.BlockSpec((1, 1, w, tm), lambda b, i: (b, i, 0, 0))
    tab = pl.BlockSpec((tm, LANES), lambda b, i: (i, 0))
    full = lambda a: pl.BlockSpec(a.shape, lambda b, i: (0,) * a.ndim)
    sds = lambda width, dt: jax.ShapeDtypeStruct((nb, t, width), dt)
    sds_t = jax.ShapeDtypeStruct((nb, t // tm, w, tm), BF16)
    return pl.pallas_call(
        _proj_ab_kernel,
        grid=(nb, t // tm),
        in_specs=[tok(D_MODEL), full(w6), full(wvt), full(wf), full(bf), tab, tab, tab],
        out_specs=[tok(w), tok(w), tok(w), tok(w), tr, tok(H_A), tok(LANES), tok(w), tok(w), tok(w), tok(w), tr],
        out_shape=[sds(w, BF16), sds(w, F32), sds(w, BF16), sds(w, F32), sds_t, sds(H_A, F32), sds(LANES, F32),
                   sds(w, BF16), sds(w, F32), sds(w, BF16), sds(w, F32), sds_t],
        compiler_params=_cparams(("parallel", "parallel")),
        name="proj_ab",
    )(x, w6, wvt, wf, bf, *tabs)


def _decay_bias_kernel(lf_ref, spread_ref, lower_ref, o_ref, carry_ref):
    @pl.when(pl.program_id(1) == 0)
    def _():
        carry_ref[...] = jnp.zeros_like(carry_ref)

    spread = spread_ref[...]
    lower = lower_ref[...]
    tc = lower.shape[0]
    lane = lax.broadcasted_iota(jnp.int32, (1, LANES), 1).astype(F32)
    piece = lane - BIAS_PIECES * jnp.floor((lane + 0.5) * (1.0 / BIAS_PIECES))
    carry = carry_ref[...]
    for r in range(lf_ref.shape[1] // tc):
        rows = slice(r * tc, (r + 1) * tc)
        x = lf_ref[0, rows, :]
        xr = sum(jnp.dot(p, spread, preferred_element_type=F32) for p in _split3(x))
        c = sum(jnp.dot(lower, p, preferred_element_type=F32) for p in _split3(xr)) + carry
        carry = c[tc - 1:tc, :]
        hi, mid, lo = (p.astype(F32) for p in _split3(c * (-LOG2E)))
        o_ref[0, rows, :] = jnp.where(piece == 0.0, hi, jnp.where(piece == 1.0, mid, lo)).astype(BF16)
    carry_ref[...] = carry


def _decay_bias(lf_wide):
    nb, t, _ = lf_wide.shape
    tc = min(ATTN_BLOCK, t)
    tb = min(BIAS_ROWS_PER_STEP, t)
    assert t % tb == 0 and tb % tc == 0
    spec = pl.BlockSpec((1, tb, LANES), lambda b, i: (b, i, 0))
    src = jnp.arange(LANES)[:, None]
    dst = jnp.arange(LANES)[None, :]
    spread = ((dst // BIAS_PIECES == src) & (src < H_A)).astype(BF16)
    lower = jnp.tril(jnp.ones((tc, tc), BF16))
    const = lambda a: pl.BlockSpec(a.shape, lambda b, i: (0, 0))
    return pl.pallas_call(
        _decay_bias_kernel,
        grid=(nb, t // tb),
        in_specs=[spec, const(spread), const(lower)],
        out_specs=spec,
        out_shape=jax.ShapeDtypeStruct((nb, t, LANES), BF16),
        scratch_shapes=[pltpu.VMEM((1, LANES), F32)],
        compiler_params=_cparams(("parallel", "arbitrary")),
        name="cumsum",
    )(lf_wide, spread, lower)


def _attn_kernel(*refs, mode, tq, tk, mask_shift, lam_init):
    if mode == "diff":
        q_ref, k_ref, vt_ref, lq1_ref, lk1_ref, lq2_ref, lk2_ref, sub_ref, o_ref = refs[:9]
    elif mode == "fox":
        q_ref, k_ref, vt_ref, b_ref, o_ref = refs[:5]
    else:
        q_ref, k_ref, vt_ref, o_ref = refs[:4]
    m_sc, acc_sc, sa, bma, sb, bmb = refs[-6:]
    v_rows = LANES if mode == "diff" else HEAD_DIM
    sa_sc, sb_sc = (sa, bma), (sb, bmb)

    qi = pl.program_id(2)
    q = q_ref[0]
    lane = lax.broadcasted_iota(jnp.int32, (1, LANES), 1)
    if mode == "mla":
        qs = [q[:, :LANES], q[:, LANES:]]
    else:
        zero = jnp.zeros_like(q)
        qs = [jnp.where(lane < HEAD_DIM, q, zero), jnp.where(lane >= HEAD_DIM, q, zero)]
        if mode == "fox":
            def pick(i):
                lo = BIAS_PIECES * (2 * pl.program_id(1) + i)
                hot = jnp.where((lane >= lo) & (lane < lo + BIAS_PIECES), 1.0, 0.0)
                return jnp.broadcast_to(hot, (tq, LANES)).astype(BF16)

            qs = [jnp.concatenate([qs[i], pick(i)], axis=1) for i in range(2)]

    m_sc[...] = jnp.full(m_sc.shape, NEG_INF, F32)
    acc_sc[...] = jnp.zeros(acc_sc.shape, F32)

    pw = min(PANEL[mode], tq)

    def scores(j, bufs, q0, q1):
        s_sc, bm_sc = bufs
        cs = slice(q0, q1)
        k = k_ref[0, j]
        if mode == "fox":
            k = jnp.concatenate([k, b_ref[0, j]], axis=1)
        for i in range(2):
            ki = k[:, i * LANES:(i + 1) * LANES] if mode == "mla" else k
            st = lax.dot_general(ki, qs[i][cs], (((1,), (1,)), ((), ())), preferred_element_type=F32)
            s_sc[i, :, cs] = st
            bm_sc[i, :, cs] = jnp.max(st, axis=0, keepdims=True)

    def consume(j, bufs, q0, q1, key0=None):
        s_sc, bm_sc = bufs
        cs = slice(q0, q1)
        vt = vt_ref[0, j]
        masked = key0 is not None and ((key0 + tk - 1) >> mask_shift) > (q0 >> mask_shift)
        for i in range(2):
            st = s_sc[i, :, cs]
            if masked:
                key = lax.broadcasted_iota(jnp.int32, (tk, q1 - q0), 0) + key0
                qry = lax.broadcasted_iota(jnp.int32, (tk, q1 - q0), 1) + q0
                vis = lax.shift_right_logical(key, mask_shift) <= lax.shift_right_logical(qry, mask_shift)
                st = jnp.where(vis, st, NEG_INF)
                blk_max = jnp.max(st, axis=0, keepdims=True)
            else:
                blk_max = bm_sc[i, :, cs]
            m_prev = m_sc[i, :, cs]
            m_new = jnp.maximum(m_prev, blk_max)
            alpha = jnp.exp2(m_prev - m_new)
            p = jnp.exp2(st - m_new).astype(BF16)
            vi = vt if mode == "diff" else vt[i * HEAD_DIM:(i + 1) * HEAD_DIM]
            vi = jnp.concatenate([vi, jnp.ones((BF16_ROWS, tk), BF16)], axis=0)
            acc_sc[i, :, cs] = alpha * acc_sc[i, :, cs] + jnp.dot(vi, p, preferred_element_type=F32)
            m_sc[i, :, cs] = m_new

    def stage(nxt, cur):
        for q0 in range(0, tq, pw):
            scores(nxt[0], nxt[1], q0, q0 + pw)
            consume(cur[0], cur[1], q0, q0 + pw)

    nfull = 2 * qi

    def pair(jj, carry):
        stage((2 * jj + 1, sb_sc), (2 * jj, sa_sc))
        stage((2 * jj + 2, sa_sc), (2 * jj + 1, sb_sc))
        return carry

    for q0 in range(0, tq, pw):
        scores(0, sa_sc, q0, q0 + pw)
    lax.fori_loop(0, qi, pair, 0)

    pt = min(pw, tk)
    sees_second = lambda q1: (tk >> mask_shift) <= ((q1 - 1) >> mask_shift)
    for q0 in range(0, tq, pt):
        if sees_second(q0 + pt):
            scores(nfull + 1, sb_sc, q0, q0 + pt)
        consume(nfull, sa_sc, q0, q0 + pt, key0=0)
    for q0 in range(0, tq, pt):
        if sees_second(q0 + pt):
            consume(nfull + 1, sb_sc, q0, q0 + pt, key0=tk)

    if mode == "diff":
        lam = (jnp.exp(jnp.sum(lq1_ref[...] * lk1_ref[...], axis=1, keepdims=True))
               - jnp.exp(jnp.sum(lq2_ref[...] * lk2_ref[...], axis=1, keepdims=True)) + lam_init)
    for q0 in range(0, tq, pw):
        cs = slice(q0, q0 + pw)
        o0 = acc_sc[0, :v_rows, cs] / acc_sc[0, v_rows:v_rows + 1, cs]
        o1 = acc_sc[1, :v_rows, cs] / acc_sc[1, v_rows:v_rows + 1, cs]
        if mode == "diff":
            o = o0 - lam * o1
            ms = jnp.mean(o * o, axis=0, keepdims=True)
            o = (o * lax.rsqrt(ms + RMS_EPS)).T * sub_ref[...] * (1.0 - lam_init)
        else:
            o = jnp.concatenate([o0, o1], axis=0).T
        o_ref[0, cs, :] = o.astype(o_ref.dtype)


def _attention(mode, q, k, vt, extra, *, n_pairs, mask_shift, lam_init=0.0):
    nb, t_q, _ = q.shape
    _, nkb, tk, _ = k.shape
    tq = 2 * tk
    assert t_q % tq == 0 and nkb == t_q // tk
    qw = 2 * LANES if mode == "mla" else LANES
    in_specs = [
        pl.BlockSpec((1, tq, qw), lambda b, p, i: (b, i, p)),
        pl.BlockSpec((1, nkb, tk, qw), lambda b, p, i: (b, 0, 0, p)),
        pl.BlockSpec((1, nkb, LANES, tk), lambda b, p, i: (b, 0, p, 0)),
    ]
    if mode == "fox":
        in_specs.append(pl.BlockSpec((1, nkb, tk, LANES), lambda b, p, i: (b, 0, 0, 0)))
    elif mode == "diff":
        in_specs += [pl.BlockSpec(a.shape, lambda b, p, i: (0, 0)) for a in extra]
    kern = functools.partial(_attn_kernel, mode=mode, tq=tq, tk=tk, mask_shift=mask_shift, lam_init=lam_init)
    return pl.pallas_call(
        kern,
        grid=(nb, n_pairs, t_q // tq),
        in_specs=in_specs,
        out_specs=pl.BlockSpec((1, tq, LANES), lambda b, p, i: (b, i, p)),
        out_shape=jax.ShapeDtypeStruct((nb, t_q, n_pairs * LANES), BF16),
        scratch_shapes=[pltpu.VMEM((2, 1, tq), F32),
                        pltpu.VMEM((2, (LANES if mode == "diff" else HEAD_DIM) + BF16_ROWS, tq), F32),
                        pltpu.VMEM((2, tk, tq), F32), pltpu.VMEM((2, 1, tq), F32),
                        pltpu.VMEM((2, tk, tq), F32), pltpu.VMEM((2, 1, tq), F32)],
        compiler_params=_cparams(("parallel", "parallel", "arbitrary")),
        name="attn_" + mode,
    )(q, k, vt, *extra)


def _decode_attn_kernel(*refs, mode, lam_init):
    if mode == "fox":
        q_ref, kc_ref, vc_ref, kn_ref, vn_ref, b_ref, o_ref = refs
    else:
        q_ref, kc_ref, vc_ref, kn_ref, vn_ref, lq1_ref, lk1_ref, lq2_ref, lk2_ref, sub_ref, o_ref = refs
        lam = (jnp.exp(jnp.sum(lq1_ref[...] * lk1_ref[...], axis=1, keepdims=True))
               - jnp.exp(jnp.sum(lq2_ref[...] * lk2_ref[...], axis=1, keepdims=True)) + lam_init)
    ts = q_ref.shape[1]
    past = kc_ref.shape[1]
    lane = lax.broadcasted_iota(jnp.int32, (1, LANES), 1)
    row = lax.broadcasted_iota(jnp.int32, (ts, ts), 0)
    col = lax.broadcasted_iota(jnp.int32, (ts, ts), 1)
    nt = (((1,), (1,)), ((), ()))
    for p in range(q_ref.shape[2] // LANES):
        sl = slice(p * LANES, (p + 1) * LANES)
        q = q_ref[0, :, sl]
        kc = kc_ref[0, :, sl].astype(BF16)
        kn = kn_ref[0, :, sl]
        vc = vc_ref[0, :, sl].astype(BF16)
        vn = vn_ref[0, :, sl].astype(BF16)
        if mode == "fox":
            kc = jnp.concatenate([kc, b_ref[0, :past, :]], axis=1)
            kn = jnp.concatenate([kn, b_ref[0, past:past + ts, :]], axis=1)
        zero = jnp.zeros_like(q)
        outs = []
        for i in range(2):
            qi = jnp.where(lane < HEAD_DIM, q, zero) if i == 0 else jnp.where(lane >= HEAD_DIM, q, zero)
            if mode == "fox":
                lo = BIAS_PIECES * (2 * p + i)
                hot = jnp.where((lane >= lo) & (lane < lo + BIAS_PIECES), 1.0, 0.0)
                qi = jnp.concatenate([qi, jnp.broadcast_to(hot, (ts, LANES)).astype(BF16)], axis=1)
            sc = lax.dot_general(qi, kc, nt, preferred_element_type=F32)
            sn = lax.dot_general(qi, kn, nt, preferred_element_type=F32)
            if mode == "fox":
                sn = jnp.where(col <= row, sn, NEG_INF)
            m = jnp.maximum(jnp.max(sc, axis=1, keepdims=True), jnp.max(sn, axis=1, keepdims=True))
            pc = jnp.exp2(sc - m)
            pn = jnp.exp2(sn - m)
            l = jnp.sum(pc, axis=1, keepdims=True) + jnp.sum(pn, axis=1, keepdims=True)
            outs.append((jnp.dot(pc.astype(BF16), vc, preferred_element_type=F32)
                         + jnp.dot(pn.astype(BF16), vn, preferred_element_type=F32)) / l)
        if mode == "fox":
            o = jnp.where(lane < HEAD_DIM, outs[0], outs[1])
        else:
            o = outs[0] - lam * outs[1]
            ms = jnp.mean(o * o, axis=1, keepdims=True)
            o = o * lax.rsqrt(ms + RMS_EPS) * sub_ref[...] * (1.0 - lam_init)
        o_ref[0, :, sl] = o.astype(o_ref.dtype)


def _decode_attention(mode, q, k_cache, v_cache, k_new, v_new, extra, lam_init=0.0):
    nb, ts, w = q.shape
    assert k_cache.shape[1] % CHUNK == 0 and ts <= CHUNK
    per_b = lambda a: pl.BlockSpec((1,) + a.shape[1:], lambda b: (b, 0, 0))
    if mode == "fox":
        extra_specs = [per_b(extra[0])]
    else:
        extra_specs = [pl.BlockSpec(a.shape, lambda b: (0, 0)) for a in extra]
    return pl.pallas_call(
        functools.partial(_decode_attn_kernel, mode=mode, lam_init=lam_init),
        grid=(nb,),
        in_specs=[per_b(q), per_b(k_cache), per_b(v_cache), per_b(k_new), per_b(v_new)] + extra_specs,
        out_specs=pl.BlockSpec((1, ts, w), lambda b: (b, 0, 0)),
        out_shape=jax.ShapeDtypeStruct((nb, ts, w), BF16),
        compiler_params=_cparams(("parallel",)),
        name="decode_" + mode,
    )(q, k_cache, v_cache, k_new, v_new, *extra)


def _outproj_ln_kernel(*refs, n_in, alpha):
    x_ref = refs[0]
    o_refs = refs[1:1 + n_in]
    w_refs = refs[1 + n_in:1 + 2 * n_in]
    g_ref, b_ref, y_ref = refs[1 + 2 * n_in:]
    mix = jnp.dot(o_refs[0][...], w_refs[0][...], preferred_element_type=F32)
    for o_r, w_r in zip(o_refs[1:], w_refs[1:]):
        mix = mix + jnp.dot(o_r[...], w_r[...], preferred_element_type=F32)
    y_ref[...] = _layer_norm(alpha * x_ref[...] + mix, g_ref[...], b_ref[...])


def _outproj_ln(x, outs, ws, g, b, alpha):
    n, d = x.shape
    tm = min(ROW_TILE, n)
    assert n % tm == 0
    row = lambda width: pl.BlockSpec((tm, width), lambda i: (i, 0))
    full = lambda a: pl.BlockSpec(a.shape, lambda i: (0, 0))
    return pl.pallas_call(
        functools.partial(_outproj_ln_kernel, n_in=len(outs), alpha=alpha),
        grid=(n // tm,),
        in_specs=[row(d)] + [row(o.shape[1]) for o in outs] + [full(w) for w in ws] + [full(g), full(b)],
        out_specs=row(d),
        out_shape=jax.ShapeDtypeStruct((n, d), F32),
        compiler_params=_cparams(("parallel",)),
        name="outproj_ln",
    )(x, *outs, *ws, g, b)


def _route(logits):
    lane = lax.broadcasted_iota(jnp.int32, logits.shape, 1).astype(F32)
    big = float(1 << 20)
    is_g = lane < N_GROUPS
    lg = jnp.where(is_g, logits, NEG_INF)
    eg = jnp.where(is_g, jnp.exp(lg - jnp.max(lg, axis=1, keepdims=True)), 0.0)
    pg = eg / jnp.sum(eg, axis=1, keepdims=True)
    p_g = jnp.max(pg, axis=1, keepdims=True)
    gidx = jnp.min(jnp.where(is_g & (pg == p_g), lane, big), axis=1, keepdims=True)
    lo = GATE_LANE0 + EXPERTS_PER_GROUP * gidx
    sel = (lane >= lo) & (lane < lo + EXPERTS_PER_GROUP)
    le = jnp.where(sel, logits, NEG_INF)
    ee = jnp.where(sel, jnp.exp(le - jnp.max(le, axis=1, keepdims=True)), 0.0)
    pe = ee / jnp.sum(ee, axis=1, keepdims=True)
    v1 = jnp.max(jnp.where(sel, pe, -1.0), axis=1, keepdims=True)
    i1 = jnp.min(jnp.where(sel & (pe == v1), lane, big), axis=1, keepdims=True)
    rest = sel & (lane != i1)
    v2 = jnp.max(jnp.where(rest, pe, -1.0), axis=1, keepdims=True)
    i2 = jnp.min(jnp.where(rest & (pe == v2), lane, big), axis=1, keepdims=True)
    tot = v1 + v2
    w1 = v1 / tot * p_g
    w2 = v2 / tot * p_g
    return jnp.where(lane == i1, w1, jnp.where(lane == i2, w2, 0.0))


def _moe_ln_kernel(x_ref, wrh_ref, wrl_ref, br_ref, w1_ref, w3_ref, w2_ref, g_ref, b_ref, y_ref,
                   xb_sc, gate_sc, acc_sc, *, alpha):
    e = pl.program_id(1)

    @pl.when(e == 0)
    def _():
        x = x_ref[...]
        xh = x.astype(BF16)
        xl = (x - xh.astype(F32)).astype(BF16)
        xb_sc[...] = xh
        logits = (jnp.dot(xh, wrh_ref[...], preferred_element_type=F32)
                  + jnp.dot(xl, wrh_ref[...], preferred_element_type=F32)
                  + jnp.dot(xh, wrl_ref[...], preferred_element_type=F32) + br_ref[...])
        gate_sc[...] = _route(logits)
        acc_sc[...] = jnp.zeros_like(acc_sc)

    xb = xb_sc[...]
    h1 = jnp.dot(xb, w1_ref[0].astype(BF16), preferred_element_type=F32)
    h3 = jnp.dot(xb, w3_ref[0].astype(BF16), preferred_element_type=F32)
    hdn = (h1 * jax.nn.sigmoid(h1)) * h3
    y = jnp.dot(hdn.astype(BF16), w2_ref[0].astype(BF16), preferred_element_type=F32)
    lane = lax.broadcasted_iota(jnp.int32, (1, LANES), 1)
    ge = jnp.sum(jnp.where(lane == e + GATE_LANE0, gate_sc[...], 0.0), axis=1, keepdims=True)
    acc_sc[...] += ge * y

    @pl.when(e == pl.num_programs(1) - 1)
    def _():
        y_ref[...] = _layer_norm(alpha * x_ref[...] + acc_sc[...], g_ref[...], b_ref[...])


def _moe_ln(x, wrh, wrl, br, w1, w3, w2, g, b, alpha):
    n, d = x.shape
    tm = min(MOE_TILE, n)
    assert n % tm == 0
    ne = w1.shape[0]
    per_expert = lambda a: pl.BlockSpec((1,) + a.shape[1:], lambda i, e: (e, 0, 0))
    full = lambda a: pl.BlockSpec(a.shape, lambda i, e: (0, 0))
    return pl.pallas_call(
        functools.partial(_moe_ln_kernel, alpha=alpha),
        grid=(n // tm, ne),
        in_specs=[pl.BlockSpec((tm, d), lambda i, e: (i, 0)), full(wrh), full(wrl), full(br),
                  per_expert(w1), per_expert(w3), per_expert(w2),
                  full(g), full(b)],
        out_specs=pl.BlockSpec((tm, d), lambda i, e: (i, 0)),
        out_shape=jax.ShapeDtypeStruct((n, d), F32),
        scratch_shapes=[pltpu.VMEM((tm, d), BF16), pltpu.VMEM((tm, LANES), F32), pltpu.VMEM((tm, d), F32)],
        compiler_params=_cparams(("parallel", "arbitrary")),
        name="moe_ln",
    )(x, wrh, wrl, br, w1, w3, w2, g, b)


def _proj_c_kernel(x_ref, win_ref, gq_ref, gkv_ref, wuq_ref, cq_ref, s1q_ref, s2q_ref,
                   ck_ref, s1k_ref, s2k_ref, q_ref, ckv_ref, kr_ref):
    xb = x_ref[0].astype(BF16)
    h = jnp.dot(xb, win_ref[...], preferred_element_type=F32)
    qa = h[:, :Q_RANK]
    kva = h[:, Q_RANK:Q_RANK + KV_RANK]
    krw = h[:, Q_RANK + KV_RANK:]
    qn = qa * lax.rsqrt(jnp.mean(qa * qa, axis=1, keepdims=True) + RMS_EPS) * gq_ref[...]
    ckv_ref[0] = kva * lax.rsqrt(jnp.mean(kva * kva, axis=1, keepdims=True) + RMS_EPS) * gkv_ref[...]
    half = ROPE_DIM // 2
    kr = _rope3(krw, ck_ref[...], s1k_ref[...], s2k_ref[...], LANES - half, half)
    kr_ref[0] = kr[:, :ROPE_DIM]
    q = jnp.dot(qn.astype(BF16), wuq_ref[...], preferred_element_type=F32)
    cq, s1q, s2q = cq_ref[...], s1q_ref[...], s2q_ref[...]
    scale = (NOPE_DIM + ROPE_DIM) ** -0.5 * LOG2E
    for hd in range(H_C):
        sl = slice(hd * LANES, (hd + 1) * LANES)
        q_ref[0, :, sl] = (_rope3(q[:, sl], cq, s1q, s2q, LANES - half, half) * scale).astype(BF16)


def _proj_c(x, win, gq, gkv, wuq, tabs_q, tabs_k):
    nb, t, _ = x.shape
    tm = min(ROW_TILE, t)
    assert t % tm == 0
    tok = lambda width: pl.BlockSpec((1, tm, width), lambda b, i: (b, i, 0))
    tab = pl.BlockSpec((tm, LANES), lambda b, i: (i, 0))
    full = lambda a: pl.BlockSpec(a.shape, lambda b, i: (0, 0))
    return pl.pallas_call(
        _proj_c_kernel,
        grid=(nb, t // tm),
        in_specs=[tok(D_MODEL), full(win), full(gq), full(gkv), full(wuq)] + [tab] * 6,
        out_specs=[tok(H_C * LANES), tok(KV_RANK), tok(ROPE_DIM)],
        out_shape=[jax.ShapeDtypeStruct((nb, t, H_C * LANES), BF16),
                   jax.ShapeDtypeStruct((nb, t, KV_RANK), F32),
                   jax.ShapeDtypeStruct((nb, t, ROPE_DIM), F32)],
        compiler_params=_cparams(("parallel", "parallel")),
        name="proj_c",
    )(x, win, gq, gkv, wuq, *tabs_q, *tabs_k)


def _kv_up_kernel(ckv_ref, kr_ref, wk_ref, place_ref, wvt_ref, k_ref, vt_ref):
    cb = ckv_ref[...].astype(BF16)
    k = (jnp.dot(cb, wk_ref[...], preferred_element_type=F32)
         + jnp.dot(kr_ref[...].astype(BF16), place_ref[...], preferred_element_type=F32))
    k_ref[0] = k.astype(BF16)
    vt = lax.dot_general(wvt_ref[...], cb, (((1,), (1,)), ((), ())), preferred_element_type=F32)
    vt_ref[0] = vt.astype(BF16)


def _kv_up(ckv, kr, wk, place, wvt):
    n = ckv.shape[0]
    tm = ATTN_BLOCK
    assert n % tm == 0
    row = lambda width: pl.BlockSpec((tm, width), lambda i: (i, 0))
    full = lambda a: pl.BlockSpec(a.shape, lambda i: (0, 0))
    return pl.pallas_call(
        _kv_up_kernel,
        grid=(n // tm,),
        in_specs=[row(KV_RANK), row(ROPE_DIM), full(wk), full(place), full(wvt)],
        out_specs=[pl.BlockSpec((1, tm, H_C * LANES), lambda i: (i, 0, 0)),
                   pl.BlockSpec((1, H_C * V_DIM_C, tm), lambda i: (i, 0, 0))],
        out_shape=[jax.ShapeDtypeStruct((n // tm, tm, H_C * LANES), BF16),
                   jax.ShapeDtypeStruct((n // tm, H_C * V_DIM_C, tm), BF16)],
        compiler_params=_cparams(("parallel",)),
        name="kv_up",
    )(ckv, kr, wk, place, wvt)


def _mla_decode_kernel(q_ref, ckv_ref, kr_ref, ckvn_ref, krn_ref, wabs_ref, wv_ref, o_ref):
    q = q_ref[0]
    ts = q.shape[0]
    qs = jnp.concatenate(
        [jnp.dot(q[:, h * LANES:(h + 1) * LANES], wabs_ref[h], preferred_element_type=F32).astype(BF16)
         for h in range(H_C)], axis=0)
    kc = jnp.concatenate([ckv_ref[0].astype(BF16), kr_ref[0].astype(BF16)], axis=1)
    kn = jnp.concatenate([ckvn_ref[0].astype(BF16), krn_ref[0].astype(BF16)], axis=1)
    nt = (((1,), (1,)), ((), ()))
    sc = lax.dot_general(qs, kc, nt, preferred_element_type=F32)
    sn = lax.dot_general(qs, kn, nt, preferred_element_type=F32)
    m = jnp.maximum(jnp.max(sc, axis=1, keepdims=True), jnp.max(sn, axis=1, keepdims=True))
    pc = jnp.exp2(sc - m)
    pn = jnp.exp2(sn - m)
    l = jnp.sum(pc, axis=1, keepdims=True) + jnp.sum(pn, axis=1, keepdims=True)
    ol = (jnp.dot(pc.astype(BF16), kc[:, :KV_RANK], preferred_element_type=F32)
          + jnp.dot(pn.astype(BF16), kn[:, :KV_RANK], preferred_element_type=F32)) / l
    olb = ol.astype(BF16)
    o = jnp.dot(olb[:ts], wv_ref[0], preferred_element_type=F32)
    for h in range(1, H_C):
        o = o + jnp.dot(olb[h * ts:(h + 1) * ts], wv_ref[h], preferred_element_type=F32)
    o_ref[0] = o.astype(o_ref.dtype)


def _mla_decode(q, ckv_c, kr_c, ckv_n, kr_n, w_abs, w_vout):
    nb, ts, _ = q.shape
    assert ckv_c.shape[1] % CHUNK == 0 and ts <= CHUNK
    per_b = lambda a: pl.BlockSpec((1,) + a.shape[1:], lambda b: (b, 0, 0))
    full = lambda a: pl.BlockSpec(a.shape, lambda b: (0, 0, 0))
    return pl.pallas_call(
        _mla_decode_kernel,
        grid=(nb,),
        in_specs=[per_b(q), per_b(ckv_c), per_b(kr_c), per_b(ckv_n), per_b(kr_n), full(w_abs), full(w_vout)],
        out_specs=pl.BlockSpec((1, ts, H_C * V_DIM_C), lambda b: (b, 0, 0)),
        out_shape=jax.ShapeDtypeStruct((nb, ts, H_C * V_DIM_C), BF16),
        compiler_params=_cparams(("parallel",)),
        name="mla_decode",
    )(q, ckv_c, kr_c, ckv_n, kr_n, w_abs, w_vout)


def _rope_tables(pos, dim, lane0):
    half = dim // 2
    inv = ROPE_THETA ** (-jnp.arange(0, dim, 2, dtype=F32) / dim)
    ang = pos.astype(F32)[:, None] * inv[None, :]
    cos, sin = jnp.cos(ang), jnp.sin(ang)
    zero = jnp.zeros_like(sin)
    c = jnp.concatenate([cos, cos], axis=1)
    s1 = jnp.concatenate([-sin, zero], axis=1)
    s2 = jnp.concatenate([zero, sin], axis=1)
    if lane0 < 0:
        reps = LANES // dim
        return tuple(jnp.tile(a, (1, reps)) for a in (c, s1, s2))
    t = pos.shape[0]
    pad = lambda a, fill: jnp.concatenate(
        [jnp.full((t, lane0), fill, F32), a, jnp.full((t, LANES - lane0 - dim), fill, F32)], axis=1)
    return pad(c, 1.0), pad(s1, 0.0), pad(s2, 0.0)


def _pad_cols(a, width):
    return jnp.pad(a, ((0, 0), (0, width - a.shape[1])))


def _blocks(a, tk):
    nb, t, l = a.shape
    return a.reshape(nb, t // tk, tk, l)


def _cat_pad_time(cache, new, t_pad):
    nb, t0, l = cache.shape
    t1 = new.shape[1]
    return jnp.concatenate([cache, new, jnp.zeros((nb, t_pad - t0 - t1, l), cache.dtype)], axis=1)


def kernel(x_prompt, x_sample, cache_fox_k, cache_fox_v, cache_fox_logf, cache_diff_k, cache_diff_v, cache_mla_ckv, cache_mla_krope, w_in_ab, b_fgate, diff_lq1, diff_lk1, diff_lq2, diff_lk2, diff_subln, w_out_ab, w_in_c, mla_q_norm, mla_kv_norm, mla_w_uq, mla_w_ukv, w_out_c, ln1_g, ln1_b, ln2_g, ln2_b, moe_wg, moe_bg, moe_we, moe_be, moe_w1, moe_w3, moe_w2):
    bp, tp, d = x_prompt.shape
    bs, ts, _ = x_sample.shape
    past = cache_fox_k.shape[2]
    depth = ln1_g.shape[0]
    alpha = (2 * depth) ** 0.25
    tk = ATTN_BLOCK
    assert past % tk == 0
    ns = bs * ts
    t_dec = past + tk

    pos_p = jnp.arange(tp)
    pos_s = jnp.tile(past + jnp.arange(ts), bs)

    xp = x_prompt
    xs = x_sample.reshape(1, ns, d)
    out_ab_p, out_ab_s, out_c_p, out_c_s = [], [], [], []

    for i in range(depth):
        j = i // 2
        if i % 2 == 0:
            lam_init = 0.8 - 0.6 * math.exp(-0.3 * i)
            cuts = [0, A_WIDTH, 2 * A_WIDTH, 3 * A_WIDTH, 3 * A_WIDTH + H_A,
                    3 * A_WIDTH + H_A + B_QK_WIDTH, 3 * A_WIDTH + H_A + 2 * B_QK_WIDTH,
                    3 * A_WIDTH + H_A + 2 * B_QK_WIDTH + B_V_WIDTH]
            w = w_in_ab[j]
            piece = lambda a: w[:, cuts[a]:cuts[a + 1]]
            w6 = jnp.stack([piece(0), piece(1), piece(2), piece(4), piece(5), piece(6)]).astype(BF16)
            wvt = jnp.stack([piece(2).T, piece(6).T]).astype(BF16)
            wf = _pad_cols(piece(3), LANES).astype(BF16)
            bf = _pad_cols(b_fgate[j][None, :], LANES)
            wout = w_out_ab[j].astype(BF16)
            diff_extra = (diff_lq1[j][None, :], diff_lk1[j][None, :], diff_lq2[j][None, :],
                          diff_lk2[j][None, :], diff_subln[j][None, :])

            tabs = _rope_tables(pos_p, HEAD_DIM, -1)
            (qa, ka, kab, va, vat, lf, lfw, qb, kb, kbb, vb, vbt) = _proj_ab(xp, w6, wvt, wf, bf, tabs)
            bias = _blocks(_decay_bias(lfw), tk)
            oa = _attention("fox", qa, _blocks(kab, tk), vat, (bias,), n_pairs=H_A // 2, mask_shift=0)
            ob = _attention("diff", qb, _blocks(kbb, tk), vbt, diff_extra,
                            n_pairs=H_B, mask_shift=int(math.log2(CHUNK)), lam_init=lam_init)
            out_ab_p.append((ka.reshape(bp, tp, H_A, HEAD_DIM), va.reshape(bp, tp, H_A, HEAD_DIM), lf,
                             kb.reshape(bp, tp, H_B, 2, HEAD_DIM), vb.reshape(bp, tp, H_B, 2 * HEAD_DIM)))
            xp2 = _outproj_ln(xp.reshape(bp * tp, d), [oa.reshape(bp * tp, -1), ob.reshape(bp * tp, -1)],
                              [wout[:A_WIDTH], wout[A_WIDTH:]], ln1_g[i][None, :], ln1_b[i][None, :], alpha)

            tabs = _rope_tables(pos_s, HEAD_DIM, -1)
            (qa, ka, kab, va, _, lf, lfw, qb, kb, kbb, vb, _) = _proj_ab(xs, w6, wvt, wf, bf, tabs)
            rs = lambda a: a.reshape(bs, ts, a.shape[-1])
            cache_lfw = jnp.pad(cache_fox_logf[j].astype(F32), ((0, 0), (0, 0), (0, LANES - H_A)))
            bias = _decay_bias(_cat_pad_time(cache_lfw, rs(lfw), t_dec))
            flat = lambda c: c.reshape(bs, past, -1)
            oa = _decode_attention("fox", rs(qa), flat(cache_fox_k[j]), flat(cache_fox_v[j]), rs(kab), rs(va),
                                   (bias,))
            ob = _decode_attention("diff", rs(qb), flat(cache_diff_k[j]), flat(cache_diff_v[j]), rs(kbb),
                                   rs(vb), diff_extra, lam_init=lam_init)
            out_ab_s.append((ka.reshape(bs, ts, H_A, HEAD_DIM), va.reshape(bs, ts, H_A, HEAD_DIM),
                             lf.reshape(bs, ts, H_A), kb.reshape(bs, ts, H_B, 2, HEAD_DIM),
                             vb.reshape(bs, ts, H_B, 2 * HEAD_DIM)))
            xs2 = _outproj_ln(xs.reshape(ns, d), [oa.reshape(ns, -1), ob.reshape(ns, -1)],
                              [wout[:A_WIDTH], wout[A_WIDTH:]], ln1_g[i][None, :], ln1_b[i][None, :], alpha)
        else:
            wc = w_in_c[j]
            kr_cols = _pad_cols(wc[:, Q_RANK + KV_RANK:], LANES)
            win = jnp.concatenate([wc[:, :Q_RANK + KV_RANK], kr_cols], axis=1).astype(BF16)
            wuq = jnp.pad(mla_w_uq[j].reshape(Q_RANK, H_C, NOPE_DIM + ROPE_DIM),
                          ((0, 0), (0, 0), (0, LANES - NOPE_DIM - ROPE_DIM))).reshape(Q_RANK, H_C * LANES)
            wuq = wuq.astype(BF16)
            wukv = mla_w_ukv[j].reshape(KV_RANK, H_C, NOPE_DIM + V_DIM_C)
            wk = jnp.pad(wukv[:, :, :NOPE_DIM], ((0, 0), (0, 0), (0, LANES - NOPE_DIM)))
            wk = wk.reshape(KV_RANK, H_C * LANES).astype(BF16)
            wvt = wukv[:, :, NOPE_DIM:].reshape(KV_RANK, H_C * V_DIM_C).T.astype(BF16)
            place = jnp.tile(_pad_cols(jnp.concatenate(
                [jnp.zeros((ROPE_DIM, NOPE_DIM), F32), jnp.eye(ROPE_DIM, dtype=F32)], axis=1), LANES),
                (1, H_C)).astype(BF16)
            gq = mla_q_norm[j][None, :]
            gkv = mla_kv_norm[j][None, :]
            wout = w_out_c[j].astype(BF16)

            q, ckv, kr = _proj_c(xp, win, gq, gkv, wuq, _rope_tables(pos_p, ROPE_DIM, NOPE_DIM),
                                 _rope_tables(pos_p, ROPE_DIM, 0))
            kc, vct = _kv_up(ckv.reshape(bp * tp, KV_RANK), kr.reshape(bp * tp, ROPE_DIM), wk, place, wvt)
            per_seq = lambda a, nb: a.reshape((nb, a.shape[0] // nb) + a.shape[1:])
            oc = _attention("mla", q, per_seq(kc, bp), per_seq(vct, bp), (), n_pairs=H_C // 2,
                            mask_shift=int(math.log2(CHUNK)))
            out_c_p.append((ckv, kr))
            xp2 = _outproj_ln(xp.reshape(bp * tp, d), [oc.reshape(bp * tp, -1)], [wout],
                              ln1_g[i][None, :], ln1_b[i][None, :], alpha)

            q, ckv, kr = _proj_c(xs, win, gq, gkv, wuq, _rope_tables(pos_s, ROPE_DIM, NOPE_DIM),
                                 _rope_tables(pos_s, ROPE_DIM, 0))
            w_abs = jnp.zeros((H_C, LANES, 2 * LANES), F32)
            w_abs = w_abs.at[:, :NOPE_DIM, :KV_RANK].set(jnp.transpose(wukv[:, :, :NOPE_DIM], (1, 2, 0)))
            w_abs = w_abs.at[:, NOPE_DIM:NOPE_DIM + ROPE_DIM, KV_RANK:KV_RANK + ROPE_DIM].set(
                jnp.eye(ROPE_DIM, dtype=F32))
            w_vout = jnp.zeros((H_C, KV_RANK, H_C * V_DIM_C), F32)
            for hd in range(H_C):
                w_vout = w_vout.at[hd, :, hd * V_DIM_C:(hd + 1) * V_DIM_C].set(wukv[:, hd, NOPE_DIM:])
            wide = lambda a: jnp.pad(a.astype(F32), ((0, 0), (0, 0), (0, LANES - ROPE_DIM)))
            oc = _mla_decode(q.reshape(bs, ts, -1), cache_mla_ckv[j].astype(F32), wide(cache_mla_krope[j]),
                             ckv.reshape(bs, ts, KV_RANK), wide(kr.reshape(bs, ts, ROPE_DIM)),
                             w_abs.astype(BF16), w_vout.astype(BF16))
            out_c_s.append((ckv.reshape(bs, ts, KV_RANK), kr.reshape(bs, ts, ROPE_DIM)))
            xs2 = _outproj_ln(xs.reshape(ns, d), [oc.reshape(ns, -1)], [wout],
                              ln1_g[i][None, :], ln1_b[i][None, :], alpha)

        wr = _pad_cols(jnp.concatenate(
            [moe_wg[i]] + [moe_we[i][gi] for gi in range(N_GROUPS)], axis=1), LANES)
        wrh = wr.astype(BF16)
        wrl = (wr - wrh.astype(F32)).astype(BF16)
        br = _pad_cols(jnp.concatenate([moe_bg[i], moe_be[i].reshape(-1)])[None, :], LANES)
        moe_w = (moe_w1[i], moe_w3[i], moe_w2[i])
        g2, b2 = ln2_g[i][None, :], ln2_b[i][None, :]
        xp = _moe_ln(xp2, wrh, wrl, br, *moe_w, g2, b2, alpha).reshape(bp, tp, d)
        xs = _moe_ln(xs2, wrh, wrl, br, *moe_w, g2, b2, alpha).reshape(1, ns, d)

    stack = lambda rows, n: jnp.stack([r[n] for r in rows])
    return (xp, xs.reshape(bs, ts, d),
            stack(out_ab_p, 0), stack(out_ab_p, 1), stack(out_ab_p, 2), stack(out_ab_p, 3), stack(out_ab_p, 4),
            stack(out_c_p, 0), stack(out_c_p, 1),
            stack(out_ab_s, 0), stack(out_ab_s, 1), stack(out_ab_s, 2), stack(out_ab_s, 3), stack(out_ab_s, 4),
            stack(out_c_s, 0), stack(out_c_s, 1))
```

```python
import functools
import math

import jax
import jax.numpy as jnp
from jax import lax
from jax.experimental import pallas as pl
from jax.experimental.pallas import tpu as pltpu

F32 = jnp.float32
BF16 = jnp.bfloat16

D_MODEL = 1024
CHUNK = 64
HEAD_DIM = 64
ROPE_THETA = 10000.0
H_A = 8
H_B = 4
H_C = 16
Q_RANK = 256
KV_RANK = 128
NOPE_DIM = 64
ROPE_DIM = 32
V_DIM_C = 64
N_GROUPS = 4
EXPERTS_PER_GROUP = 4
N_EXPERTS = N_GROUPS * EXPERTS_PER_GROUP
D_EXPERT = 256
A_WIDTH = H_A * HEAD_DIM
B_QK_WIDTH = H_B * 2 * HEAD_DIM
B_V_WIDTH = H_B * 2 * HEAD_DIM
FGATE_BIAS = 3.0
LN_EPS = 1e-5
RMS_EPS = 1e-6
NEG_INF = -1e30
LOG2E = math.log2(math.e)

LANES = 128
BF16_ROWS = 16
PANEL = {"fox": 1024, "diff": 256, "mla": 1024}
BIAS_PIECES = 3
VMEM_LIMIT = 48 * 1024 * 1024
ATTN_BLOCK = 512
ROW_TILE = 512
MOE_TILE = 1024
BIAS_ROWS_PER_STEP = 4096
GATE_LANE0 = N_GROUPS


def _cparams(sem):
    return pltpu.CompilerParams(dimension_semantics=sem, vmem_limit_bytes=VMEM_LIMIT)


def _rope3(x, c, s1, s2, shift_up, shift_down):
    return x * c + pltpu.roll(x, shift_up, 1) * s1 + pltpu.roll(x, shift_down, 1) * s2


def _layer_norm(y, g, b):
    mu = jnp.mean(y, axis=-1, keepdims=True)
    d = y - mu
    var = jnp.mean(d * d, axis=-1, keepdims=True)
    return d * lax.rsqrt(var + LN_EPS) * g + b


def _split3(x):
    hi = x.astype(BF16)
    r1 = x - hi.astype(F32)
    mid = r1.astype(BF16)
    return hi, mid, (r1 - mid.astype(F32)).astype(BF16)


def _proj_ab_kernel(x_ref, w_ref, wvt_ref, wf_ref, bf_ref, c_ref, s1_ref, s2_ref,
                    qa_ref, ka_ref, kab_ref, va_ref, vat_ref, lf_ref, lfw_ref,
                    qb_ref, kb_ref, kbb_ref, vb_ref, vbt_ref):
    xb = x_ref[0].astype(BF16)

    def mm(i):
        return jnp.dot(xb, w_ref[i], preferred_element_type=F32)

    def mm_t(i):
        return lax.dot_general(wvt_ref[i], xb, (((1,), (1,)), ((), ())), preferred_element_type=F32)

    qa_ref[0] = (mm(0) * (HEAD_DIM ** -0.5 * LOG2E)).astype(BF16)
    ka = mm(1)
    ka_ref[0] = ka
    kab_ref[0] = ka.astype(BF16)
    va_ref[0] = mm(2)
    vat_ref[0, 0] = mm_t(0).astype(BF16)

    z = jnp.dot(xb, wf_ref[...], preferred_element_type=F32) + bf_ref[...]
    lf = jnp.minimum(z, 0.0) - jnp.log1p(jnp.exp(-jnp.abs(z)))
    lf_ref[0] = lf[:, :H_A]
    lfw_ref[0] = lf

    c, s1, s2 = c_ref[...], s1_ref[...], s2_ref[...]
    qb = mm(3)
    kb = mm(4)
    for s in range(B_QK_WIDTH // LANES):
        sl = slice(s * LANES, (s + 1) * LANES)
        qs = _rope3(qb[:, sl], c, s1, s2, LANES - HEAD_DIM // 2, HEAD_DIM // 2)
        qb_ref[0, :, sl] = (qs * (HEAD_DIM ** -0.5 * LOG2E)).astype(BF16)
        ks = _rope3(kb[:, sl], c, s1, s2, LANES - HEAD_DIM // 2, HEAD_DIM // 2)
        kb_ref[0, :, sl] = ks
        kbb_ref[0, :, sl] = ks.astype(BF16)
    vb_ref[0] = mm(5)
    vbt_ref[0, 0] = mm_t(1).astype(BF16)


def _proj_ab(x, w6, wvt, wf, bf, tabs):
    nb, t, _ = x.shape
    tm = min(ROW_TILE, t)
    assert t % tm == 0
    w = A_WIDTH
    tok = lambda width: pl.BlockSpec((1, tm, width), lambda b, i: (b, i, 0))
    tr = pl.BlockSpec((1, 1, w, tm), lambda b, i: (b, i, 0, 0))
    tab = pl.BlockSpec((tm, LANES), lambda b, i: (i, 0))
    full = lambda a: pl.BlockSpec(a.shape, lambda b, i: (0,) * a.ndim)
    sds = lambda width, dt: jax.ShapeDtypeStruct((nb, t, width), dt)
    sds_t = jax.ShapeDtypeStruct((nb, t // tm, w, tm), BF16)
    return pl.pallas_call(
        _proj_ab_kernel,
        grid=(nb, t // tm),
        in_specs=[tok(D_MODEL), full(w6), full(wvt), full(wf), full(bf), tab, tab, tab],
        out_specs=[tok(w), tok(w), tok(w), tok(w), tr, tok(H_A), tok(LANES), tok(w), tok(w), tok(w), tok(w), tr],
        out_shape=[sds(w, BF16), sds(w, F32), sds(w, BF16), sds(w, F32), sds_t, sds(H_A, F32), sds(LANES, F32),
                   sds(w, BF16), sds(w, F32), sds(w, BF16), sds(w, F32), sds_t],
        compiler_params=_cparams(("parallel", "parallel")),
        name="proj_ab",
    )(x, w6, wvt, wf, bf, *tabs)


def _decay_bias_kernel(lf_ref, spread_ref, lower_ref, o_ref, carry_ref):
    @pl.when(pl.program_id(1) == 0)
    def _():
        carry_ref[...] = jnp.zeros_like(carry_ref)

    spread = spread_ref[...]
    lower = lower_ref[...]
    tc = lower.shape[0]
    lane = lax.broadcasted_iota(jnp.int32, (1, LANES), 1).astype(F32)
    piece = lane - BIAS_PIECES * jnp.floor((lane + 0.5) * (1.0 / BIAS_PIECES))
    carry = carry_ref[...]
    for r in range(lf_ref.shape[1] // tc):
        rows = slice(r * tc, (r + 1) * tc)
        x = lf_ref[0, rows, :]
        xr = sum(jnp.dot(p, spread, preferred_element_type=F32) for p in _split3(x))
        c = sum(jnp.dot(lower, p, preferred_element_type=F32) for p in _split3(xr)) + carry
        carry = c[tc - 1:tc, :]
        hi, mid, lo = (p.astype(F32) for p in _split3(c * (-LOG2E)))
        o_ref[0, rows, :] = jnp.where(piece == 0.0, hi, jnp.where(piece == 1.0, mid, lo)).astype(BF16)
    carry_ref[...] = carry


def _decay_bias(lf_wide):
    nb, t, _ = lf_wide.shape
    tc = min(ATTN_BLOCK, t)
    tb = min(BIAS_ROWS_PER_STEP, t)
    assert t % tb == 0 and tb % tc == 0
    spec = pl.BlockSpec((1, tb, LANES), lambda b, i: (b, i, 0))
    src = jnp.arange(LANES)[:, None]
    dst = jnp.arange(LANES)[None, :]
    spread = ((dst // BIAS_PIECES == src) & (src < H_A)).astype(BF16)
    lower = jnp.tril(jnp.ones((tc, tc), BF16))
    const = lambda a: pl.BlockSpec(a.shape, lambda b, i: (0, 0))
    return pl.pallas_call(
        _decay_bias_kernel,
        grid=(nb, t // tb),
        in_specs=[spec, const(spread), const(lower)],
        out_specs=spec,
        out_shape=jax.ShapeDtypeStruct((nb, t, LANES), BF16),
        scratch_shapes=[pltpu.VMEM((1, LANES), F32)],
        compiler_params=_cparams(("parallel", "arbitrary")),
        name="cumsum",
    )(lf_wide, spread, lower)


def _attn_kernel(*refs, mode, tq, tk, mask_shift, lam_init):
    if mode == "diff":
        q_ref, k_ref, vt_ref, lq1_ref, lk1_ref, lq2_ref, lk2_ref, sub_ref, o_ref = refs[:9]
    elif mode == "fox":
        q_ref, k_ref, vt_ref, b_ref, o_ref = refs[:5]
    else:
        q_ref, k_ref, vt_ref, o_ref = refs[:4]
    m_sc, acc_sc, sa, bma, sb, bmb = refs[-6:]
    v_rows = LANES if mode == "diff" else HEAD_DIM
    sa_sc, sb_sc = (sa, bma), (sb, bmb)

    qi = pl.program_id(2)
    q = q_ref[0]
    lane = lax.broadcasted_iota(jnp.int32, (1, LANES), 1)
    if mode == "mla":
        qs = [q[:, :LANES], q[:, LANES:]]
    else:
        zero = jnp.zeros_like(q)
        qs = [jnp.where(lane < HEAD_DIM, q, zero), jnp.where(lane >= HEAD_DIM, q, zero)]
        if mode == "fox":
            def pick(i):
                lo = BIAS_PIECES * (2 * pl.program_id(1) + i)
                hot = jnp.where((lane >= lo) & (lane < lo + BIAS_PIECES), 1.0, 0.0)
                return jnp.broadcast_to(hot, (tq, LANES)).astype(BF16)

            qs = [jnp.concatenate([qs[i], pick(i)], axis=1) for i in range(2)]

    m_sc[...] = jnp.full(m_sc.shape, NEG_INF, F32)
    acc_sc[...] = jnp.zeros(acc_sc.shape, F32)

    pw = min(PANEL[mode], tq)

    def scores(j, bufs, q0, q1):
        s_sc, bm_sc = bufs
        cs = slice(q0, q1)
        k = k_ref[0, j]
        if mode == "fox":
            k = jnp.concatenate([k, b_ref[0, j]], axis=1)
        for i in range(2):
            ki = k[:, i * LANES:(i + 1) * LANES] if mode == "mla" else k
            st = lax.dot_general(ki, qs[i][cs], (((1,), (1,)), ((), ())), preferred_element_type=F32)
            s_sc[i, :, cs] = st
            bm_sc[i, :, cs] = jnp.max(st, axis=0, keepdims=True)

    def consume(j, bufs, q0, q1, key0=None):
        s_sc, bm_sc = bufs
        cs = slice(q0, q1)
        vt = vt_ref[0, j]
        masked = key0 is not None and ((key0 + tk - 1) >> mask_shift) > (q0 >> mask_shift)
        for i in range(2):
            st = s_sc[i, :, cs]
            if masked:
                key = lax.broadcasted_iota(jnp.int32, (tk, q1 - q0), 0) + key0
                qry = lax.broadcasted_iota(jnp.int32, (tk, q1 - q0), 1) + q0
                vis = lax.shift_right_logical(key, mask_shift) <= lax.shift_right_logical(qry, mask_shift)
                st = jnp.where(vis, st, NEG_INF)
                blk_max = jnp.max(st, axis=0, keepdims=True)
            else:
                blk_max = bm_sc[i, :, cs]
            m_prev = m_sc[i, :, cs]
            m_new = jnp.maximum(m_prev, blk_max)
            alpha = jnp.exp2(m_prev - m_new)
            p = jnp.exp2(st - m_new).astype(BF16)
            vi = vt if mode == "diff" else vt[i * HEAD_DIM:(i + 1) * HEAD_DIM]
            vi = jnp.concatenate([vi, jnp.ones((BF16_ROWS, tk), BF16)], axis=0)
            acc_sc[i, :, cs] = alpha * acc_sc[i, :, cs] + jnp.dot(vi, p, preferred_element_type=F32)
            m_sc[i, :, cs] = m_new

    def stage(nxt, cur):
        for q0 in range(0, tq, pw):
            scores(nxt[0], nxt[1], q0, q0 + pw)
            consume(cur[0], cur[1], q0, q0 + pw)

    nfull = 2 * qi

    def pair(jj, carry):
        stage((2 * jj + 1, sb_sc), (2 * jj, sa_sc))
        stage((2 * jj + 2, sa_sc), (2 * jj + 1, sb_sc))
        return carry

    for q0 in range(0, tq, pw):
        scores(0, sa_sc, q0, q0 + pw)
    lax.fori_loop(0, qi, pair, 0)

    pt = min(pw, tk)
    sees_second = lambda q1: (tk >> mask_shift) <= ((q1 - 1) >> mask_shift)
    for q0 in range(0, tq, pt):
        if sees_second(q0 + pt):
            scores(nfull + 1, sb_sc, q0, q0 + pt)
        consume(nfull, sa_sc, q0, q0 + pt, key0=0)
    for q0 in range(0, tq, pt):
        if sees_second(q0 + pt):
            consume(nfull + 1, sb_sc, q0, q0 + pt, key0=tk)

    if mode == "diff":
        lam = (jnp.exp(jnp.sum(lq1_ref[...] * lk1_ref[...], axis=1, keepdims=True))
               - jnp.exp(jnp.sum(lq2_ref[...] * lk2_ref[...], axis=1, keepdims=True)) + lam_init)
    for q0 in range(0, tq, pw):
        cs = slice(q0, q0 + pw)
        o0 = acc_sc[0, :v_rows, cs] / acc_sc[0, v_rows:v_rows + 1, cs]
        o1 = acc_sc[1, :v_rows, cs] / acc_sc[1, v_rows:v_rows + 1, cs]
        if mode == "diff":
            o = o0 - lam * o1
            ms = jnp.mean(o * o, axis=0, keepdims=True)
            o = (o * lax.rsqrt(ms + RMS_EPS)).T * sub_ref[...] * (1.0 - lam_init)
        else:
            o = jnp.concatenate([o0, o1], axis=0).T
        o_ref[0, cs, :] = o.astype(o_ref.dtype)


def _attention(mode, q, k, vt, extra, *, n_pairs, mask_shift, lam_init=0.0):
    nb, t_q, _ = q.shape
    _, nkb, tk, _ = k.shape
    tq = 2 * tk
    assert t_q % tq == 0 and nkb == t_q // tk
    qw = 2 * LANES if mode == "mla" else LANES
    in_specs = [
        pl.BlockSpec((1, tq, qw), lambda b, p, i: (b, i, p)),
        pl.BlockSpec((1, nkb, tk, qw), lambda b, p, i: (b, 0, 0, p)),
        pl.BlockSpec((1, nkb, LANES, tk), lambda b, p, i: (b, 0, p, 0)),
    ]
    if mode == "fox":
        in_specs.append(pl.BlockSpec((1, nkb, tk, LANES), lambda b, p, i: (b, 0, 0, 0)))
    elif mode == "diff":
        in_specs += [pl.BlockSpec(a.shape, lambda b, p, i: (0, 0)) for a in extra]
    kern = functools.partial(_attn_kernel, mode=mode, tq=tq, tk=tk, mask_shift=mask_shift, lam_init=lam_init)
    return pl.pallas_call(
        kern,
        grid=(nb, n_pairs, t_q // tq),
        in_specs=in_specs,
        out_specs=pl.BlockSpec((1, tq, LANES), lambda b, p, i: (b, i, p)),
        out_shape=jax.ShapeDtypeStruct((nb, t_q, n_pairs * LANES), BF16),
        scratch_shapes=[pltpu.VMEM((2, 1, tq), F32),
                        pltpu.VMEM((2, (LANES if mode == "diff" else HEAD_DIM) + BF16_ROWS, tq), F32),
                        pltpu.VMEM((2, tk, tq), F32), pltpu.VMEM((2, 1, tq), F32),
                        pltpu.VMEM((2, tk, tq), F32), pltpu.VMEM((2, 1, tq), F32)],
        compiler_params=_cparams(("parallel", "parallel", "arbitrary")),
        name="attn_" + mode,
    )(q, k, vt, *extra)


def _decode_attn_kernel(*refs, mode, lam_init):
    if mode == "fox":
        q_ref, kc_ref, vc_ref, kn_ref, vn_ref, b_ref, o_ref = refs
    else:
        q_ref, kc_ref, vc_ref, kn_ref, vn_ref, lq1_ref, lk1_ref, lq2_ref, lk2_ref, sub_ref, o_ref = refs
        lam = (jnp.exp(jnp.sum(lq1_ref[...] * lk1_ref[...], axis=1, keepdims=True))
               - jnp.exp(jnp.sum(lq2_ref[...] * lk2_ref[...], axis=1, keepdims=True)) + lam_init)
    ts = q_ref.shape[1]
    past = kc_ref.shape[1]
    lane = lax.broadcasted_iota(jnp.int32, (1, LANES), 1)
    row = lax.broadcasted_iota(jnp.int32, (ts, ts), 0)
    col = lax.broadcasted_iota(jnp.int32, (ts, ts), 1)
    nt = (((1,), (1,)), ((), ()))
    for p in range(q_ref.shape[2] // LANES):
        sl = slice(p * LANES, (p + 1) * LANES)
        q = q_ref[0, :, sl]
        kc = kc_ref[0, :, sl].astype(BF16)
        kn = kn_ref[0, :, sl]
        vc = vc_ref[0, :, sl].astype(BF16)
        vn = vn_ref[0, :, sl].astype(BF16)
        if mode == "fox":
            kc = jnp.concatenate([kc, b_ref[0, :past, :]], axis=1)
            kn = jnp.concatenate([kn, b_ref[0, past:past + ts, :]], axis=1)
        zero = jnp.zeros_like(q)
        outs = []
        for i in range(2):
            qi = jnp.where(lane < HEAD_DIM, q, zero) if i == 0 else jnp.where(lane >= HEAD_DIM, q, zero)
            if mode == "fox":
                lo = BIAS_PIECES * (2 * p + i)
                hot = jnp.where((lane >= lo) & (lane < lo + BIAS_PIECES), 1.0, 0.0)
                qi = jnp.concatenate([qi, jnp.broadcast_to(hot, (ts, LANES)).astype(BF16)], axis=1)
            sc = lax.dot_general(qi, kc, nt, preferred_element_type=F32)
            sn = lax.dot_general(qi, kn, nt, preferred_element_type=F32)
            if mode == "fox":
                sn = jnp.where(col <= row, sn, NEG_INF)
            m = jnp.maximum(jnp.max(sc, axis=1, keepdims=True), jnp.max(sn, axis=1, keepdims=True))
            pc = jnp.exp2(sc - m)
            pn = jnp.exp2(sn - m)
            l = jnp.sum(pc, axis=1, keepdims=True) + jnp.sum(pn, axis=1, keepdims=True)
            outs.append((jnp.dot(pc.astype(BF16), vc, preferred_element_type=F32)
                         + jnp.dot(pn.astype(BF16), vn, preferred_element_type=F32)) / l)
        if mode == "fox":
            o = jnp.where(lane < HEAD_DIM, outs[0], outs[1])
        else:
            o = outs[0] - lam * outs[1]
            ms = jnp.mean(o * o, axis=1, keepdims=True)
            o = o * lax.rsqrt(ms + RMS_EPS) * sub_ref[...] * (1.0 - lam_init)
        o_ref[0, :, sl] = o.astype(o_ref.dtype)


def _decode_attention(mode, q, k_cache, v_cache, k_new, v_new, extra, lam_init=0.0):
    nb, ts, w = q.shape
    assert k_cache.shape[1] % CHUNK == 0 and ts <= CHUNK
    per_b = lambda a: pl.BlockSpec((1,) + a.shape[1:], lambda b: (b, 0, 0))
    if mode == "fox":
        extra_specs = [per_b(extra[0])]
    else:
        extra_specs = [pl.BlockSpec(a.shape, lambda b: (0, 0)) for a in extra]
    return pl.pallas_call(
        functools.partial(_decode_attn_kernel, mode=mode, lam_init=lam_init),
        grid=(nb,),
        in_specs=[per_b(q), per_b(k_cache), per_b(v_cache), per_b(k_new), per_b(v_new)] + extra_specs,
        out_specs=pl.BlockSpec((1, ts, w), lambda b: (b, 0, 0)),
        out_shape=jax.ShapeDtypeStruct((nb, ts, w), BF16),
        compiler_params=_cparams(("parallel",)),
        name="decode_" + mode,
    )(q, k_cache, v_cache, k_new, v_new, *extra)


def _outproj_ln_kernel(*refs, n_in, alpha):
    x_ref = refs[0]
    o_refs = refs[1:1 + n_in]
    w_refs = refs[1 + n_in:1 + 2 * n_in]
    g_ref, b_ref, y_ref = refs[1 + 2 * n_in:]
    mix = jnp.dot(o_refs[0][...], w_refs[0][...], preferred_element_type=F32)
    for o_r, w_r in zip(o_refs[1:], w_refs[1:]):
        mix = mix + jnp.dot(o_r[...], w_r[...], preferred_element_type=F32)
    y_ref[...] = _layer_norm(alpha * x_ref[...] + mix, g_ref[...], b_ref[...])


def _outproj_ln(x, outs, ws, g, b, alpha):
    n, d = x.shape
    tm = min(ROW_TILE, n)
    assert n % tm == 0
    row = lambda width: pl.BlockSpec((tm, width), lambda i: (i, 0))
    full = lambda a: pl.BlockSpec(a.shape, lambda i: (0, 0))
    return pl.pallas_call(
        functools.partial(_outproj_ln_kernel, n_in=len(outs), alpha=alpha),
        grid=(n // tm,),
        in_specs=[row(d)] + [row(o.shape[1]) for o in outs] + [full(w) for w in ws] + [full(g), full(b)],
        out_specs=row(d),
        out_shape=jax.ShapeDtypeStruct((n, d), F32),
        compiler_params=_cparams(("parallel",)),
        name="outproj_ln",
    )(x, *outs, *ws, g, b)


def _route(logits):
    lane = lax.broadcasted_iota(jnp.int32, logits.shape, 1).astype(F32)
    big = float(1 << 20)
    is_g = lane < N_GROUPS
    lg = jnp.where(is_g, logits, NEG_INF)
    eg = jnp.where(is_g, jnp.exp(lg - jnp.max(lg, axis=1, keepdims=True)), 0.0)
    pg = eg / jnp.sum(eg, axis=1, keepdims=True)
    p_g = jnp.max(pg, axis=1, keepdims=True)
    gidx = jnp.min(jnp.where(is_g & (pg == p_g), lane, big), axis=1, keepdims=True)
    lo = GATE_LANE0 + EXPERTS_PER_GROUP * gidx
    sel = (lane >= lo) & (lane < lo + EXPERTS_PER_GROUP)
    le = jnp.where(sel, logits, NEG_INF)
    ee = jnp.where(sel, jnp.exp(le - jnp.max(le, axis=1, keepdims=True)), 0.0)
    pe = ee / jnp.sum(ee, axis=1, keepdims=True)
    v1 = jnp.max(jnp.where(sel, pe, -1.0), axis=1, keepdims=True)
    i1 = jnp.min(jnp.where(sel & (pe == v1), lane, big), axis=1, keepdims=True)
    rest = sel & (lane != i1)
    v2 = jnp.max(jnp.where(rest, pe, -1.0), axis=1, keepdims=True)
    i2 = jnp.min(jnp.where(rest & (pe == v2), lane, big), axis=1, keepdims=True)
    tot = v1 + v2
    w1 = v1 / tot * p_g
    w2 = v2 / tot * p_g
    return jnp.where(lane == i1, w1, jnp.where(lane == i2, w2, 0.0))


def _moe_ln_kernel(x_ref, wrh_ref, wrl_ref, br_ref, w1_ref, w3_ref, w2_ref, g_ref, b_ref, y_ref,
                   xb_sc, gate_sc, acc_sc, *, alpha):
    e = pl.program_id(1)

    @pl.when(e == 0)
    def _():
        x = x_ref[...]
        xh = x.astype(BF16)
        xl = (x - xh.astype(F32)).astype(BF16)
        xb_sc[...] = xh
        logits = (jnp.dot(xh, wrh_ref[...], preferred_element_type=F32)
                  + jnp.dot(xl, wrh_ref[...], preferred_element_type=F32)
                  + jnp.dot(xh, wrl_ref[...], preferred_element_type=F32) + br_ref[...])
        gate_sc[...] = _route(logits)
        acc_sc[...] = jnp.zeros_like(acc_sc)

    xb = xb_sc[...]
    h1 = jnp.dot(xb, w1_ref[0].astype(BF16), preferred_element_type=F32)
    h3 = jnp.dot(xb, w3_ref[0].astype(BF16), preferred_element_type=F32)
    hdn = (h1 * jax.nn.sigmoid(h1)) * h3
    y = jnp.dot(hdn.astype(BF16), w2_ref[0].astype(BF16), preferred_element_type=F32)
    lane = lax.broadcasted_iota(jnp.int32, (1, LANES), 1)
    ge = jnp.sum(jnp.where(lane == e + GATE_LANE0, gate_sc[...], 0.0), axis=1, keepdims=True)
    acc_sc[...] += ge * y

    @pl.when(e == pl.num_programs(1) - 1)
    def _():
        y_ref[...] = _layer_norm(alpha * x_ref[...] + acc_sc[...], g_ref[...], b_ref[...])


def _moe_ln(x, wrh, wrl, br, w1, w3, w2, g, b, alpha):
    n, d = x.shape
    tm = min(MOE_TILE, n)
    assert n % tm == 0
    ne = w1.shape[0]
    per_expert = lambda a: pl.BlockSpec((1,) + a.shape[1:], lambda i, e: (e, 0, 0))
    full = lambda a: pl.BlockSpec(a.shape, lambda i, e: (0, 0))
    return pl.pallas_call(
        functools.partial(_moe_ln_kernel, alpha=alpha),
        grid=(n // tm, ne),
        in_specs=[pl.BlockSpec((tm, d), lambda i, e: (i, 0)), full(wrh), full(wrl), full(br),
                  per_expert(w1), per_expert(w3), per_expert(w2),
                  full(g), full(b)],
        out_specs=pl.BlockSpec((tm, d), lambda i, e: (i, 0)),
        out_shape=jax.ShapeDtypeStruct((n, d), F32),
        scratch_shapes=[pltpu.VMEM((tm, d), BF16), pltpu.VMEM((tm, LANES), F32), pltpu.VMEM((tm, d), F32)],
        compiler_params=_cparams(("parallel", "arbitrary")),
        name="moe_ln",
    )(x, wrh, wrl, br, w1, w3, w2, g, b)


def _proj_c_kernel(x_ref, win_ref, gq_ref, gkv_ref, wuq_ref, wrot_ref, cq_ref, sq_ref,
                   ck_ref, s1k_ref, s2k_ref, q_ref, ckv_ref, kr_ref):
    xb = x_ref[0].astype(BF16)
    h = jnp.dot(xb, win_ref[...], preferred_element_type=F32)
    qa = h[:, :Q_RANK]
    kva = h[:, Q_RANK:Q_RANK + KV_RANK]
    krw = h[:, Q_RANK + KV_RANK:]
    qn = qa * lax.rsqrt(jnp.mean(qa * qa, axis=1, keepdims=True) + RMS_EPS) * gq_ref[...]
    ckv_ref[0] = kva * lax.rsqrt(jnp.mean(kva * kva, axis=1, keepdims=True) + RMS_EPS) * gkv_ref[...]
    half = ROPE_DIM // 2
    kr = _rope3(krw, ck_ref[...], s1k_ref[...], s2k_ref[...], LANES - half, half)
    kr_ref[0] = kr[:, :ROPE_DIM]
    qnb = qn.astype(BF16)
    q = jnp.dot(qnb, wuq_ref[...], preferred_element_type=F32)
    q_rot = jnp.dot(qnb, wrot_ref[...], preferred_element_type=F32)
    cq, sq = cq_ref[...], sq_ref[...]
    scale = (NOPE_DIM + ROPE_DIM) ** -0.5 * LOG2E
    for hd in range(H_C):
        sl = slice(hd * LANES, (hd + 1) * LANES)
        q_ref[0, :, sl] = ((q[:, sl] * cq + q_rot[:, sl] * sq) * scale).astype(BF16)


def _proj_c(x, win, gq, gkv, wuq, wrot, tabs_q, tabs_k):
    nb, t, _ = x.shape
    tm = min(ROW_TILE, t)
    assert t % tm == 0
    tok = lambda width: pl.BlockSpec((1, tm, width), lambda b, i: (b, i, 0))
    tab = pl.BlockSpec((tm, LANES), lambda b, i: (i, 0))
    full = lambda a: pl.BlockSpec(a.shape, lambda b, i: (0, 0))
    return pl.pallas_call(
        _proj_c_kernel,
        grid=(nb, t // tm),
        in_specs=[tok(D_MODEL), full(win), full(gq), full(gkv), full(wuq), full(wrot)] + [tab] * 5,
        out_specs=[tok(H_C * LANES), tok(KV_RANK), tok(ROPE_DIM)],
        out_shape=[jax.ShapeDtypeStruct((nb, t, H_C * LANES), BF16),
                   jax.ShapeDtypeStruct((nb, t, KV_RANK), F32),
                   jax.ShapeDtypeStruct((nb, t, ROPE_DIM), F32)],
        compiler_params=_cparams(("parallel", "parallel")),
        name="proj_c",
    )(x, win, gq, gkv, wuq, wrot, *tabs_q, *tabs_k)


def _kv_up_kernel(ckv_ref, kr_ref, wk_ref, place_ref, wvt_ref, k_ref, vt_ref):
    cb = ckv_ref[...].astype(BF16)
    k = (jnp.dot(cb, wk_ref[...], preferred_element_type=F32)
         + jnp.dot(kr_ref[...].astype(BF16), place_ref[...], preferred_element_type=F32))
    k_ref[0] = k.astype(BF16)
    vt = lax.dot_general(wvt_ref[...], cb, (((1,), (1,)), ((), ())), preferred_element_type=F32)
    vt_ref[0] = vt.astype(BF16)


def _kv_up(ckv, kr, wk, place, wvt):
    n = ckv.shape[0]
    tm = ATTN_BLOCK
    assert n % tm == 0
    row = lambda width: pl.BlockSpec((tm, width), lambda i: (i, 0))
    full = lambda a: pl.BlockSpec(a.shape, lambda i: (0, 0))
    return pl.pallas_call(
        _kv_up_kernel,
        grid=(n // tm,),
        in_specs=[row(KV_RANK), row(ROPE_DIM), full(wk), full(place), full(wvt)],
        out_specs=[pl.BlockSpec((1, tm, H_C * LANES), lambda i: (i, 0, 0)),
                   pl.BlockSpec((1, H_C * V_DIM_C, tm), lambda i: (i, 0, 0))],
        out_shape=[jax.ShapeDtypeStruct((n // tm, tm, H_C * LANES), BF16),
                   jax.ShapeDtypeStruct((n // tm, H_C * V_DIM_C, tm), BF16)],
        compiler_params=_cparams(("parallel",)),
        name="kv_up",
    )(ckv, kr, wk, place, wvt)


def _mla_decode_kernel(q_ref, ckv_ref, kr_ref, ckvn_ref, krn_ref, wabs_ref, wv_ref, o_ref):
    q = q_ref[0]
    ts = q.shape[0]
    qs = jnp.concatenate(
        [jnp.dot(q[:, h * LANES:(h + 1) * LANES], wabs_ref[h], preferred_element_type=F32).astype(BF16)
         for h in range(H_C)], axis=0)
    kc = jnp.concatenate([ckv_ref[0].astype(BF16), kr_ref[0].astype(BF16)], axis=1)
    kn = jnp.concatenate([ckvn_ref[0].astype(BF16), krn_ref[0].astype(BF16)], axis=1)
    nt = (((1,), (1,)), ((), ()))
    sc = lax.dot_general(qs, kc, nt, preferred_element_type=F32)
    sn = lax.dot_general(qs, kn, nt, preferred_element_type=F32)
    m = jnp.maximum(jnp.max(sc, axis=1, keepdims=True), jnp.max(sn, axis=1, keepdims=True))
    pc = jnp.exp2(sc - m)
    pn = jnp.exp2(sn - m)
    l = jnp.sum(pc, axis=1, keepdims=True) + jnp.sum(pn, axis=1, keepdims=True)
    ol = (jnp.dot(pc.astype(BF16), kc[:, :KV_RANK], preferred_element_type=F32)
          + jnp.dot(pn.astype(BF16), kn[:, :KV_RANK], preferred_element_type=F32)) / l
    olb = ol.astype(BF16)
    o = jnp.dot(olb[:ts], wv_ref[0], preferred_element_type=F32)
    for h in range(1, H_C):
        o = o + jnp.dot(olb[h * ts:(h + 1) * ts], wv_ref[h], preferred_element_type=F32)
    o_ref[0] = o.astype(o_ref.dtype)


def _mla_decode(q, ckv_c, kr_c, ckv_n, kr_n, w_abs, w_vout):
    nb, ts, _ = q.shape
    assert ckv_c.shape[1] % CHUNK == 0 and ts <= CHUNK
    per_b = lambda a: pl.BlockSpec((1,) + a.shape[1:], lambda b: (b, 0, 0))
    full = lambda a: pl.BlockSpec(a.shape, lambda b: (0, 0, 0))
    return pl.pallas_call(
        _mla_decode_kernel,
        grid=(nb,),
        in_specs=[per_b(q), per_b(ckv_c), per_b(kr_c), per_b(ckv_n), per_b(kr_n), full(w_abs), full(w_vout)],
        out_specs=pl.BlockSpec((1, ts, H_C * V_DIM_C), lambda b: (b, 0, 0)),
        out_shape=jax.ShapeDtypeStruct((nb, ts, H_C * V_DIM_C), BF16),
        compiler_params=_cparams(("parallel",)),
        name="mla_decode",
    )(q, ckv_c, kr_c, ckv_n, kr_n, w_abs, w_vout)


def _rope_tables(pos, dim, lane0):
    half = dim // 2
    inv = ROPE_THETA ** (-jnp.arange(0, dim, 2, dtype=F32) / dim)
    ang = pos.astype(F32)[:, None] * inv[None, :]
    cos, sin = jnp.cos(ang), jnp.sin(ang)
    zero = jnp.zeros_like(sin)
    c = jnp.concatenate([cos, cos], axis=1)
    s1 = jnp.concatenate([-sin, zero], axis=1)
    s2 = jnp.concatenate([zero, sin], axis=1)
    if lane0 < 0:
        reps = LANES // dim
        return tuple(jnp.tile(a, (1, reps)) for a in (c, s1, s2))
    t = pos.shape[0]
    pad = lambda a, fill: jnp.concatenate(
        [jnp.full((t, lane0), fill, F32), a, jnp.full((t, LANES - lane0 - dim), fill, F32)], axis=1)
    return pad(c, 1.0), pad(s1, 0.0), pad(s2, 0.0)


def _pad_cols(a, width):
    return jnp.pad(a, ((0, 0), (0, width - a.shape[1])))


def _blocks(a, tk):
    nb, t, l = a.shape
    return a.reshape(nb, t // tk, tk, l)


def _cat_pad_time(cache, new, t_pad):
    nb, t0, l = cache.shape
    t1 = new.shape[1]
    return jnp.concatenate([cache, new, jnp.zeros((nb, t_pad - t0 - t1, l), cache.dtype)], axis=1)


def kernel(x_prompt, x_sample, cache_fox_k, cache_fox_v, cache_fox_logf, cache_diff_k, cache_diff_v, cache_mla_ckv, cache_mla_krope, w_in_ab, b_fgate, diff_lq1, diff_lk1, diff_lq2, diff_lk2, diff_subln, w_out_ab, w_in_c, mla_q_norm, mla_kv_norm, mla_w_uq, mla_w_ukv, w_out_c, ln1_g, ln1_b, ln2_g, ln2_b, moe_wg, moe_bg, moe_we, moe_be, moe_w1, moe_w3, moe_w2):
    bp, tp, d = x_prompt.shape
    bs, ts, _ = x_sample.shape
    past = cache_fox_k.shape[2]
    depth = ln1_g.shape[0]
    alpha = (2 * depth) ** 0.25
    tk = ATTN_BLOCK
    assert past % tk == 0
    ns = bs * ts
    t_dec = past + tk

    pos_p = jnp.arange(tp)
    pos_s = jnp.tile(past + jnp.arange(ts), bs)

    xp = x_prompt
    xs = x_sample.reshape(1, ns, d)
    out_ab_p, out_ab_s, out_c_p, out_c_s = [], [], [], []

    for i in range(depth):
        j = i // 2
        if i % 2 == 0:
            lam_init = 0.8 - 0.6 * math.exp(-0.3 * i)
            cuts = [0, A_WIDTH, 2 * A_WIDTH, 3 * A_WIDTH, 3 * A_WIDTH + H_A,
                    3 * A_WIDTH + H_A + B_QK_WIDTH, 3 * A_WIDTH + H_A + 2 * B_QK_WIDTH,
                    3 * A_WIDTH + H_A + 2 * B_QK_WIDTH + B_V_WIDTH]
            w = w_in_ab[j]
            piece = lambda a: w[:, cuts[a]:cuts[a + 1]]
            w6 = jnp.stack([piece(0), piece(1), piece(2), piece(4), piece(5), piece(6)]).astype(BF16)
            wvt = jnp.stack([piece(2).T, piece(6).T]).astype(BF16)
            wf = _pad_cols(piece(3), LANES).astype(BF16)
            bf = _pad_cols(b_fgate[j][None, :], LANES)
            wout = w_out_ab[j].astype(BF16)
            diff_extra = (diff_lq1[j][None, :], diff_lk1[j][None, :], diff_lq2[j][None, :],
                          diff_lk2[j][None, :], diff_subln[j][None, :])

            tabs = _rope_tables(pos_p, HEAD_DIM, -1)
            (qa, ka, kab, va, vat, lf, lfw, qb, kb, kbb, vb, vbt) = _proj_ab(xp, w6, wvt, wf, bf, tabs)
            bias = _blocks(_decay_bias(lfw), tk)
            oa = _attention("fox", qa, _blocks(kab, tk), vat, (bias,), n_pairs=H_A // 2, mask_shift=0)
            ob = _attention("diff", qb, _blocks(kbb, tk), vbt, diff_extra,
                            n_pairs=H_B, mask_shift=int(math.log2(CHUNK)), lam_init=lam_init)
            out_ab_p.append((ka.reshape(bp, tp, H_A, HEAD_DIM), va.reshape(bp, tp, H_A, HEAD_DIM), lf,
                             kb.reshape(bp, tp, H_B, 2, HEAD_DIM), vb.reshape(bp, tp, H_B, 2 * HEAD_DIM)))
            xp2 = _outproj_ln(xp.reshape(bp * tp, d), [oa.reshape(bp * tp, -1), ob.reshape(bp * tp, -1)],
                              [wout[:A_WIDTH], wout[A_WIDTH:]], ln1_g[i][None, :], ln1_b[i][None, :], alpha)

            tabs = _rope_tables(pos_s, HEAD_DIM, -1)
            (qa, ka, kab, va, _, lf, lfw, qb, kb, kbb, vb, _) = _proj_ab(xs, w6, wvt, wf, bf, tabs)
            rs = lambda a: a.reshape(bs, ts, a.shape[-1])
            cache_lfw = jnp.pad(cache_fox_logf[j].astype(F32), ((0, 0), (0, 0), (0, LANES - H_A)))
            bias = _decay_bias(_cat_pad_time(cache_lfw, rs(lfw), t_dec))
            flat = lambda c: c.reshape(bs, past, -1)
            oa = _decode_attention("fox", rs(qa), flat(cache_fox_k[j]), flat(cache_fox_v[j]), rs(kab), rs(va),
                                   (bias,))
            ob = _decode_attention("diff", rs(qb), flat(cache_diff_k[j]), flat(cache_diff_v[j]), rs(kbb),
                                   rs(vb), diff_extra, lam_init=lam_init)
            out_ab_s.append((ka.reshape(bs, ts, H_A, HEAD_DIM), va.reshape(bs, ts, H_A, HEAD_DIM),
                             lf.reshape(bs, ts, H_A), kb.reshape(bs, ts, H_B, 2, HEAD_DIM),
                             vb.reshape(bs, ts, H_B, 2 * HEAD_DIM)))
            xs2 = _outproj_ln(xs.reshape(ns, d), [oa.reshape(ns, -1), ob.reshape(ns, -1)],
                              [wout[:A_WIDTH], wout[A_WIDTH:]], ln1_g[i][None, :], ln1_b[i][None, :], alpha)
        else:
            wc = w_in_c[j]
            kr_cols = _pad_cols(wc[:, Q_RANK + KV_RANK:], LANES)
            win = jnp.concatenate([wc[:, :Q_RANK + KV_RANK], kr_cols], axis=1).astype(BF16)
            wuq3 = jnp.pad(mla_w_uq[j].reshape(Q_RANK, H_C, NOPE_DIM + ROPE_DIM),
                           ((0, 0), (0, 0), (0, LANES - NOPE_DIM - ROPE_DIM)))
            wuq = wuq3.reshape(Q_RANK, H_C * LANES).astype(BF16)
            r0, r1, r2 = NOPE_DIM, NOPE_DIM + ROPE_DIM // 2, NOPE_DIM + ROPE_DIM
            wrot = jnp.zeros_like(wuq3).at[:, :, r0:r1].set(-wuq3[:, :, r1:r2]).at[:, :, r1:r2].set(wuq3[:, :, r0:r1])
            wrot = wrot.reshape(Q_RANK, H_C * LANES).astype(BF16)
            q_tabs = lambda pos: (lambda c, s1, s2: (c, s2 - s1))(*_rope_tables(pos, ROPE_DIM, NOPE_DIM))
            wukv = mla_w_ukv[j].reshape(KV_RANK, H_C, NOPE_DIM + V_DIM_C)
            wk = jnp.pad(wukv[:, :, :NOPE_DIM], ((0, 0), (0, 0), (0, LANES - NOPE_DIM)))
            wk = wk.reshape(KV_RANK, H_C * LANES).astype(BF16)
            wvt = wukv[:, :, NOPE_DIM:].reshape(KV_RANK, H_C * V_DIM_C).T.astype(BF16)
            place = jnp.tile(_pad_cols(jnp.concatenate(
                [jnp.zeros((ROPE_DIM, NOPE_DIM), F32), jnp.eye(ROPE_DIM, dtype=F32)], axis=1), LANES),
                (1, H_C)).astype(BF16)
            gq = mla_q_norm[j][None, :]
            gkv = mla_kv_norm[j][None, :]
            wout = w_out_c[j].astype(BF16)

            q, ckv, kr = _proj_c(xp, win, gq, gkv, wuq, wrot, q_tabs(pos_p), _rope_tables(pos_p, ROPE_DIM, 0))
            kc, vct = _kv_up(ckv.reshape(bp * tp, KV_RANK), kr.reshape(bp * tp, ROPE_DIM), wk, place, wvt)
            per_seq = lambda a, nb: a.reshape((nb, a.shape[0] // nb) + a.shape[1:])
            oc = _attention("mla", q, per_seq(kc, bp), per_seq(vct, bp), (), n_pairs=H_C // 2,
                            mask_shift=int(math.log2(CHUNK)))
            out_c_p.append((ckv, kr))
            xp2 = _outproj_ln(xp.reshape(bp * tp, d), [oc.reshape(bp * tp, -1)], [wout],
                              ln1_g[i][None, :], ln1_b[i][None, :], alpha)

            q, ckv, kr = _proj_c(xs, win, gq, gkv, wuq, wrot, q_tabs(pos_s), _rope_tables(pos_s, ROPE_DIM, 0))
            w_abs = jnp.zeros((H_C, LANES, 2 * LANES), F32)
            w_abs = w_abs.at[:, :NOPE_DIM, :KV_RANK].set(jnp.transpose(wukv[:, :, :NOPE_DIM], (1, 2, 0)))
            w_abs = w_abs.at[:, NOPE_DIM:NOPE_DIM + ROPE_DIM, KV_RANK:KV_RANK + ROPE_DIM].set(
                jnp.eye(ROPE_DIM, dtype=F32))
            w_vout = jnp.zeros((H_C, KV_RANK, H_C * V_DIM_C), F32)
            for hd in range(H_C):
                w_vout = w_vout.at[hd, :, hd * V_DIM_C:(hd + 1) * V_DIM_C].set(wukv[:, hd, NOPE_DIM:])
            wide = lambda a: jnp.pad(a.astype(F32), ((0, 0), (0, 0), (0, LANES - ROPE_DIM)))
            oc = _mla_decode(q.reshape(bs, ts, -1), cache_mla_ckv[j].astype(F32), wide(cache_mla_krope[j]),
                             ckv.reshape(bs, ts, KV_RANK), wide(kr.reshape(bs, ts, ROPE_DIM)),
                             w_abs.astype(BF16), w_vout.astype(BF16))
            out_c_s.append((ckv.reshape(bs, ts, KV_RANK), kr.reshape(bs, ts, ROPE_DIM)))
            xs2 = _outproj_ln(xs.reshape(ns, d), [oc.reshape(ns, -1)], [wout],
                              ln1_g[i][None, :], ln1_b[i][None, :], alpha)

        wr = _pad_cols(jnp.concatenate(
            [moe_wg[i]] + [moe_we[i][gi] for gi in range(N_GROUPS)], axis=1), LANES)
        wrh = wr.astype(BF16)
        wrl = (wr - wrh.astype(F32)).astype(BF16)
        br = _pad_cols(jnp.concatenate([moe_bg[i], moe_be[i].reshape(-1)])[None, :], LANES)
        moe_w = (moe_w1[i], moe_w3[i], moe_w2[i])
        g2, b2 = ln2_g[i][None, :], ln2_b[i][None, :]
        xp = _moe_ln(xp2, wrh, wrl, br, *moe_w, g2, b2, alpha).reshape(bp, tp, d)
        xs = _moe_ln(xs2, wrh, wrl, br, *moe_w, g2, b2, alpha).reshape(1, ns, d)

    stack = lambda rows, n: jnp.stack([r[n] for r in rows])
    return (xp, xs.reshape(bs, ts, d),
            stack(out_ab_p, 0), stack(out_ab_p, 1), stack(out_ab_p, 2), stack(out_ab_p, 3), stack(out_ab_p, 4),
            stack(out_c_p, 0), stack(out_c_p, 1),
            stack(out_ab_s, 0), stack(out_ab_s, 1), stack(out_ab_s, 2), stack(out_ab_s, 3), stack(out_ab_s, 4),
            stack(out_c_s, 0), stack(out_c_s, 1))
```

```python
import functools
import math

import jax
import jax.numpy as jnp
from jax import lax
from jax.experimental import pallas as pl
from jax.experimental.pallas import tpu as pltpu

F32 = jnp.float32
BF16 = jnp.bfloat16

D_MODEL = 1024
CHUNK = 64
HEAD_DIM = 64
ROPE_THETA = 10000.0
H_A = 8
H_B = 4
H_C = 16
Q_RANK = 256
KV_RANK = 128
NOPE_DIM = 64
ROPE_DIM = 32
V_DIM_C = 64
N_GROUPS = 4
EXPERTS_PER_GROUP = 4
N_EXPERTS = N_GROUPS * EXPERTS_PER_GROUP
D_EXPERT = 256
A_WIDTH = H_A * HEAD_DIM
B_QK_WIDTH = H_B * 2 * HEAD_DIM
B_V_WIDTH = H_B * 2 * HEAD_DIM
FGATE_BIAS = 3.0
LN_EPS = 1e-5
RMS_EPS = 1e-6
NEG_INF = -1e30
LOG2E = math.log2(math.e)

LANES = 128
BF16_ROWS = 16
PANEL = {"fox": 1024, "diff": 256, "mla": 1024}
BIAS_PIECES = 3
VMEM_LIMIT = 48 * 1024 * 1024
ATTN_BLOCK = 512
ROW_TILE = 512
MOE_TILE = 1024
BIAS_ROWS_PER_STEP = 4096
GATE_LANE0 = N_GROUPS


def _cparams(sem):
    return pltpu.CompilerParams(dimension_semantics=sem, vmem_limit_bytes=VMEM_LIMIT)


def _rope3(x, c, s1, s2, shift_up, shift_down):
    return x * c + pltpu.roll(x, shift_up, 1) * s1 + pltpu.roll(x, shift_down, 1) * s2


def _layer_norm(y, g, b):
    mu = jnp.mean(y, axis=-1, keepdims=True)
    d = y - mu
    var = jnp.mean(d * d, axis=-1, keepdims=True)
    return d * lax.rsqrt(var + LN_EPS) * g + b


def _split3(x):
    hi = x.astype(BF16)
    r1 = x - hi.astype(F32)
    mid = r1.astype(BF16)
    return hi, mid, (r1 - mid.astype(F32)).astype(BF16)


def _proj_ab_kernel(x_ref, w_ref, wvt_ref, wf_ref, bf_ref, c_ref, s1_ref, s2_ref,
                    qa_ref, ka_ref, kab_ref, va_ref, vat_ref, lf_ref, lfw_ref,
                    qb_ref, kb_ref, kbb_ref, vb_ref, vbt_ref):
    xb = x_ref[0].astype(BF16)

    def mm(i):
        return jnp.dot(xb, w_ref[i], preferred_element_type=F32)

    def mm_t(i):
        return lax.dot_general(wvt_ref[i], xb, (((1,), (1,)), ((), ())), preferred_element_type=F32)

    qa_ref[0] = (mm(0) * (HEAD_DIM ** -0.5 * LOG2E)).astype(BF16)
    ka = mm(1)
    ka_ref[0] = ka
    kab_ref[0] = ka.astype(BF16)
    va_ref[0] = mm(2)
    vat_ref[0, 0] = mm_t(0).astype(BF16)

    z = jnp.dot(xb, wf_ref[...], preferred_element_type=F32) + bf_ref[...]
    lf = jnp.minimum(z, 0.0) - jnp.log1p(jnp.exp(-jnp.abs(z)))
    lf_ref[0] = lf[:, :H_A]
    lfw_ref[0] = lf

    c, s1, s2 = c_ref[...], s1_ref[...], s2_ref[...]
    qb = mm(3)
    kb = mm(4)
    for s in range(B_QK_WIDTH // LANES):
        sl = slice(s * LANES, (s + 1) * LANES)
        qs = _rope3(qb[:, sl], c, s1, s2, LANES - HEAD_DIM // 2, HEAD_DIM // 2)
        qb_ref[0, :, sl] = (qs * (HEAD_DIM ** -0.5 * LOG2E)).astype(BF16)
        ks = _rope3(kb[:, sl], c, s1, s2, LANES - HEAD_DIM // 2, HEAD_DIM // 2)
        kb_ref[0, :, sl] = ks
        kbb_ref[0, :, sl] = ks.astype(BF16)
    vb_ref[0] = mm(5)
    vbt_ref[0, 0] = mm_t(1).astype(BF16)


def _proj_ab(x, w6, wvt, wf, bf, tabs):
    nb, t, _ = x.shape
    tm = min(ROW_TILE, t)
    assert t % tm == 0
    w = A_WIDTH
    tok = lambda width: pl.BlockSpec((1, tm, width), lambda b, i: (b, i, 0))
    tr = pl.BlockSpec((1, 1, w, tm), lambda b, i: (b, i, 0, 0))
    tab = pl.BlockSpec((tm, LANES), lambda b, i: (i, 0))
    full = lambda a: pl.BlockSpec(a.shape, lambda b, i: (0,) * a.ndim)
    sds = lambda width, dt: jax.ShapeDtypeStruct((nb, t, width), dt)
    sds_t = jax.ShapeDtypeStruct((nb, t // tm, w, tm), BF16)
    return pl.pallas_call(
        _proj_ab_kernel,
        grid=(nb, t // tm),
        in_specs=[tok(D_MODEL), full(w6), full(wvt), full(wf), full(bf), tab, tab, tab],
        out_specs=[tok(w), tok(w), tok(w), tok(w), tr, tok(H_A), tok(LANES), tok(w), tok(w), tok(w), tok(w), tr],
        out_shape=[sds(w, BF16), sds(w, F32), sds(w, BF16), sds(w, F32), sds_t, sds(H_A, F32), sds(LANES, F32),
                   sds(w, BF16), sds(w, F32), sds(w, BF16), sds(w, F32), sds_t],
        compiler_params=_cparams(("parallel", "parallel")),
        name="proj_ab",
    )(x, w6, wvt, wf, bf, *tabs)


def _decay_bias_kernel(lf_ref, spread_ref, lower_ref, o_ref, carry_ref):
    @pl.when(pl.program_id(1) == 0)
    def _():
        carry_ref[...] = jnp.zeros_like(carry_ref)

    spread = spread_ref[...]
    lower = lower_ref[...]
    tc = lower.shape[0]
    lane = lax.broadcasted_iota(jnp.int32, (1, LANES), 1).astype(F32)
    piece = lane - BIAS_PIECES * jnp.floor((lane + 0.5) * (1.0 / BIAS_PIECES))
    carry = carry_ref[...]
    for r in range(lf_ref.shape[1] // tc):
        rows = slice(r * tc, (r + 1) * tc)
        x = lf_ref[0, rows, :]
        xr = sum(jnp.dot(p, spread, preferred_element_type=F32) for p in _split3(x))
        c = sum(jnp.dot(lower, p, preferred_element_type=F32) for p in _split3(xr)) + carry
        carry = c[tc - 1:tc, :]
        hi, mid, lo = (p.astype(F32) for p in _split3(c * (-LOG2E)))
        o_ref[0, rows, :] = jnp.where(piece == 0.0, hi, jnp.where(piece == 1.0, mid, lo)).astype(BF16)
    carry_ref[...] = carry


def _decay_bias(lf_wide):
    nb, t, _ = lf_wide.shape
    tc = min(ATTN_BLOCK, t)
    tb = min(BIAS_ROWS_PER_STEP, t)
    assert t % tb == 0 and tb % tc == 0
    spec = pl.BlockSpec((1, tb, LANES), lambda b, i: (b, i, 0))
    src = jnp.arange(LANES)[:, None]
    dst = jnp.arange(LANES)[None, :]
    spread = ((dst // BIAS_PIECES == src) & (src < H_A)).astype(BF16)
    lower = jnp.tril(jnp.ones((tc, tc), BF16))
    const = lambda a: pl.BlockSpec(a.shape, lambda b, i: (0, 0))
    return pl.pallas_call(
        _decay_bias_kernel,
        grid=(nb, t // tb),
        in_specs=[spec, const(spread), const(lower)],
        out_specs=spec,
        out_shape=jax.ShapeDtypeStruct((nb, t, LANES), BF16),
        scratch_shapes=[pltpu.VMEM((1, LANES), F32)],
        compiler_params=_cparams(("parallel", "arbitrary")),
        name="cumsum",
    )(lf_wide, spread, lower)


def _attn_kernel(*refs, mode, tq, tk, mask_shift, lam_init):
    if mode == "diff":
        q_ref, k_ref, vt_ref, lq1_ref, lk1_ref, lq2_ref, lk2_ref, sub_ref, o_ref = refs[:9]
    elif mode == "fox":
        q_ref, k_ref, vt_ref, b_ref, o_ref = refs[:5]
    else:
        q_ref, k_ref, vt_ref, o_ref = refs[:4]
    m_sc, acc_sc, sa, bma, sb, bmb = refs[-6:]
    v_rows = LANES if mode == "diff" else HEAD_DIM
    sa_sc, sb_sc = (sa, bma), (sb, bmb)

    qi = pl.program_id(2)
    q = q_ref[0]
    lane = lax.broadcasted_iota(jnp.int32, (1, LANES), 1)
    if mode == "mla":
        qs = [q[:, :LANES], q[:, LANES:]]
    else:
        zero = jnp.zeros_like(q)
        qs = [jnp.where(lane < HEAD_DIM, q, zero), jnp.where(lane >= HEAD_DIM, q, zero)]
        if mode == "fox":
            def pick(i):
                lo = BIAS_PIECES * (2 * pl.program_id(1) + i)
                hot = jnp.where((lane >= lo) & (lane < lo + BIAS_PIECES), 1.0, 0.0)
                return jnp.broadcast_to(hot, (tq, LANES)).astype(BF16)

            qs = [jnp.concatenate([qs[i], pick(i)], axis=1) for i in range(2)]

    m_sc[...] = jnp.full(m_sc.shape, NEG_INF, F32)
    acc_sc[...] = jnp.zeros(acc_sc.shape, F32)

    pw = min(PANEL[mode], tq)

    def scores(j, bufs, q0, q1):
        s_sc, bm_sc = bufs
        cs = slice(q0, q1)
        k = k_ref[0, j]
        if mode == "fox":
            k = jnp.concatenate([k, b_ref[0, j]], axis=1)
        for i in range(2):
            ki = k[:, i * LANES:(i + 1) * LANES] if mode == "mla" else k
            st = lax.dot_general(ki, qs[i][cs], (((1,), (1,)), ((), ())), preferred_element_type=F32)
            s_sc[i, :, cs] = st
            bm_sc[i, :, cs] = jnp.max(st, axis=0, keepdims=True)

    def consume(j, bufs, q0, q1, key0=None):
        s_sc, bm_sc = bufs
        cs = slice(q0, q1)
        vt = vt_ref[0, j]
        masked = key0 is not None and ((key0 + tk - 1) >> mask_shift) > (q0 >> mask_shift)
        for i in range(2):
            st = s_sc[i, :, cs]
            if masked:
                key = lax.broadcasted_iota(jnp.int32, (tk, q1 - q0), 0) + key0
                qry = lax.broadcasted_iota(jnp.int32, (tk, q1 - q0), 1) + q0
                vis = lax.shift_right_logical(key, mask_shift) <= lax.shift_right_logical(qry, mask_shift)
                st = jnp.where(vis, st, NEG_INF)
                blk_max = jnp.max(st, axis=0, keepdims=True)
            else:
                blk_max = bm_sc[i, :, cs]
            m_prev = m_sc[i, :, cs]
            m_new = jnp.maximum(m_prev, blk_max)
            alpha = jnp.exp2(m_prev - m_new)
            p = jnp.exp2(st - m_new).astype(BF16)
            vi = vt if mode == "diff" else vt[i * HEAD_DIM:(i + 1) * HEAD_DIM]
            vi = jnp.concatenate([vi, jnp.ones((BF16_ROWS, tk), BF16)], axis=0)
            acc_sc[i, :, cs] = alpha * acc_sc[i, :, cs] + jnp.dot(vi, p, preferred_element_type=F32)
            m_sc[i, :, cs] = m_new

    def stage(nxt, cur):
        for q0 in range(0, tq, pw):
            scores(nxt[0], nxt[1], q0, q0 + pw)
            consume(cur[0], cur[1], q0, q0 + pw)

    nfull = 2 * qi

    def pair(jj, carry):
        stage((2 * jj + 1, sb_sc), (2 * jj, sa_sc))
        stage((2 * jj + 2, sa_sc), (2 * jj + 1, sb_sc))
        return carry

    for q0 in range(0, tq, pw):
        scores(0, sa_sc, q0, q0 + pw)
    lax.fori_loop(0, qi, pair, 0)

    pt = min(pw, tk)
    sees_second = lambda q1: (tk >> mask_shift) <= ((q1 - 1) >> mask_shift)
    for q0 in range(0, tq, pt):
        if sees_second(q0 + pt):
            scores(nfull + 1, sb_sc, q0, q0 + pt)
        consume(nfull, sa_sc, q0, q0 + pt, key0=0)
    for q0 in range(0, tq, pt):
        if sees_second(q0 + pt):
            consume(nfull + 1, sb_sc, q0, q0 + pt, key0=tk)

    if mode == "diff":
        lam = (jnp.exp(jnp.sum(lq1_ref[...] * lk1_ref[...], axis=1, keepdims=True))
               - jnp.exp(jnp.sum(lq2_ref[...] * lk2_ref[...], axis=1, keepdims=True)) + lam_init)
    for q0 in range(0, tq, pw):
        cs = slice(q0, q0 + pw)
        o0 = acc_sc[0, :v_rows, cs] / acc_sc[0, v_rows:v_rows + 1, cs]
        o1 = acc_sc[1, :v_rows, cs] / acc_sc[1, v_rows:v_rows + 1, cs]
        if mode == "diff":
            o = o0 - lam * o1
            ms = jnp.mean(o * o, axis=0, keepdims=True)
            o = (o * lax.rsqrt(ms + RMS_EPS)).T * sub_ref[...] * (1.0 - lam_init)
        else:
            o = jnp.concatenate([o0, o1], axis=0).T
        o_ref[0, cs, :] = o.astype(o_ref.dtype)


def _attention(mode, q, k, vt, extra, *, n_pairs, mask_shift, lam_init=0.0):
    nb, t_q, _ = q.shape
    _, nkb, tk, _ = k.shape
    tq = 2 * tk
    assert t_q % tq == 0 and nkb == t_q // tk
    qw = 2 * LANES if mode == "mla" else LANES
    in_specs = [
        pl.BlockSpec((1, tq, qw), lambda b, p, i: (b, i, p)),
        pl.BlockSpec((1, nkb, tk, qw), lambda b, p, i: (b, 0, 0, p)),
        pl.BlockSpec((1, nkb, LANES, tk), lambda b, p, i: (b, 0, p, 0)),
    ]
    if mode == "fox":
        in_specs.append(pl.BlockSpec((1, nkb, tk, LANES), lambda b, p, i: (b, 0, 0, 0)))
    elif mode == "diff":
        in_specs += [pl.BlockSpec(a.shape, lambda b, p, i: (0, 0)) for a in extra]
    kern = functools.partial(_attn_kernel, mode=mode, tq=tq, tk=tk, mask_shift=mask_shift, lam_init=lam_init)
    return pl.pallas_call(
        kern,
        grid=(nb, n_pairs, t_q // tq),
        in_specs=in_specs,
        out_specs=pl.BlockSpec((1, tq, LANES), lambda b, p, i: (b, i, p)),
        out_shape=jax.ShapeDtypeStruct((nb, t_q, n_pairs * LANES), BF16),
        scratch_shapes=[pltpu.VMEM((2, 1, tq), F32),
                        pltpu.VMEM((2, (LANES if mode == "diff" else HEAD_DIM) + BF16_ROWS, tq), F32),
                        pltpu.VMEM((2, tk, tq), F32), pltpu.VMEM((2, 1, tq), F32),
                        pltpu.VMEM((2, tk, tq), F32), pltpu.VMEM((2, 1, tq), F32)],
        compiler_params=_cparams(("parallel", "parallel", "arbitrary")),
        name="attn_" + mode,
    )(q, k, vt, *extra)


def _decode_attn_kernel(*refs, mode, lam_init):
    if mode == "fox":
        q_ref, kc_ref, vc_ref, kn_ref, vn_ref, b_ref, o_ref = refs
    else:
        q_ref, kc_ref, vc_ref, kn_ref, vn_ref, lq1_ref, lk1_ref, lq2_ref, lk2_ref, sub_ref, o_ref = refs
        lam = (jnp.exp(jnp.sum(lq1_ref[...] * lk1_ref[...], axis=1, keepdims=True))
               - jnp.exp(jnp.sum(lq2_ref[...] * lk2_ref[...], axis=1, keepdims=True)) + lam_init)
    ts = q_ref.shape[1]
    past = kc_ref.shape[1]
    lane = lax.broadcasted_iota(jnp.int32, (1, LANES), 1)
    row = lax.broadcasted_iota(jnp.int32, (ts, ts), 0)
    col = lax.broadcasted_iota(jnp.int32, (ts, ts), 1)
    nt = (((1,), (1,)), ((), ()))
    for p in range(q_ref.shape[2] // LANES):
        sl = slice(p * LANES, (p + 1) * LANES)
        q = q_ref[0, :, sl]
        kc = kc_ref[0, :, sl].astype(BF16)
        kn = kn_ref[0, :, sl]
        vc = vc_ref[0, :, sl].astype(BF16)
        vn = vn_ref[0, :, sl].astype(BF16)
        if mode == "fox":
            kc = jnp.concatenate([kc, b_ref[0, :past, :]], axis=1)
            kn = jnp.concatenate([kn, b_ref[0, past:past + ts, :]], axis=1)
        zero = jnp.zeros_like(q)
        outs = []
        for i in range(2):
            qi = jnp.where(lane < HEAD_DIM, q, zero) if i == 0 else jnp.where(lane >= HEAD_DIM, q, zero)
            if mode == "fox":
                lo = BIAS_PIECES * (2 * p + i)
                hot = jnp.where((lane >= lo) & (lane < lo + BIAS_PIECES), 1.0, 0.0)
                qi = jnp.concatenate([qi, jnp.broadcast_to(hot, (ts, LANES)).astype(BF16)], axis=1)
            sc = lax.dot_general(qi, kc, nt, preferred_element_type=F32)
            sn = lax.dot_general(qi, kn, nt, preferred_element_type=F32)
            if mode == "fox":
                sn = jnp.where(col <= row, sn, NEG_INF)
            m = jnp.maximum(jnp.max(sc, axis=1, keepdims=True), jnp.max(sn, axis=1, keepdims=True))
            pc = jnp.exp2(sc - m)
            pn = jnp.exp2(sn - m)
            l = jnp.sum(pc, axis=1, keepdims=True) + jnp.sum(pn, axis=1, keepdims=True)
            outs.append((jnp.dot(pc.astype(BF16), vc, preferred_element_type=F32)
                         + jnp.dot(pn.astype(BF16), vn, preferred_element_type=F32)) / l)
        if mode == "fox":
            o = jnp.where(lane < HEAD_DIM, outs[0], outs[1])
        else:
            o = outs[0] - lam * outs[1]
            ms = jnp.mean(o * o, axis=1, keepdims=True)
            o = o * lax.rsqrt(ms + RMS_EPS) * sub_ref[...] * (1.0 - lam_init)
        o_ref[0, :, sl] = o.astype(o_ref.dtype)


def _decode_attention(mode, q, k_cache, v_cache, k_new, v_new, extra, lam_init=0.0):
    nb, ts, w = q.shape
    assert k_cache.shape[1] % CHUNK == 0 and ts <= CHUNK
    per_b = lambda a: pl.BlockSpec((1,) + a.shape[1:], lambda b: (b, 0, 0))
    if mode == "fox":
        extra_specs = [per_b(extra[0])]
    else:
        extra_specs = [pl.BlockSpec(a.shape, lambda b: (0, 0)) for a in extra]
    return pl.pallas_call(
        functools.partial(_decode_attn_kernel, mode=mode, lam_init=lam_init),
        grid=(nb,),
        in_specs=[per_b(q), per_b(k_cache), per_b(v_cache), per_b(k_new), per_b(v_new)] + extra_specs,
        out_specs=pl.BlockSpec((1, ts, w), lambda b: (b, 0, 0)),
        out_shape=jax.ShapeDtypeStruct((nb, ts, w), BF16),
        compiler_params=_cparams(("parallel",)),
        name="decode_" + mode,
    )(q, k_cache, v_cache, k_new, v_new, *extra)


def _outproj_ln_kernel(*refs, n_in, alpha):
    x_ref = refs[0]
    o_refs = refs[1:1 + n_in]
    w_refs = refs[1 + n_in:1 + 2 * n_in]
    g_ref, b_ref, y_ref = refs[1 + 2 * n_in:]
    mix = jnp.dot(o_refs[0][...], w_refs[0][...], preferred_element_type=F32)
    for o_r, w_r in zip(o_refs[1:], w_refs[1:]):
        mix = mix + jnp.dot(o_r[...], w_r[...], preferred_element_type=F32)
    y_ref[...] = _layer_norm(alpha * x_ref[...] + mix, g_ref[...], b_ref[...])


def _outproj_ln(x, outs, ws, g, b, alpha):
    n, d = x.shape
    tm = min(ROW_TILE, n)
    assert n % tm == 0
    row = lambda width: pl.BlockSpec((tm, width), lambda i: (i, 0))
    full = lambda a: pl.BlockSpec(a.shape, lambda i: (0, 0))
    return pl.pallas_call(
        functools.partial(_outproj_ln_kernel, n_in=len(outs), alpha=alpha),
        grid=(n // tm,),
        in_specs=[row(d)] + [row(o.shape[1]) for o in outs] + [full(w) for w in ws] + [full(g), full(b)],
        out_specs=row(d),
        out_shape=jax.ShapeDtypeStruct((n, d), F32),
        compiler_params=_cparams(("parallel",)),
        name="outproj_ln",
    )(x, *outs, *ws, g, b)


def _route(logits):
    lane = lax.broadcasted_iota(jnp.int32, logits.shape, 1).astype(F32)
    big = float(1 << 20)
    is_g = lane < N_GROUPS
    lg = jnp.where(is_g, logits, NEG_INF)
    eg = jnp.where(is_g, jnp.exp(lg - jnp.max(lg, axis=1, keepdims=True)), 0.0)
    pg = eg / jnp.sum(eg, axis=1, keepdims=True)
    p_g = jnp.max(pg, axis=1, keepdims=True)
    gidx = jnp.min(jnp.where(is_g & (pg == p_g), lane, big), axis=1, keepdims=True)
    lo = GATE_LANE0 + EXPERTS_PER_GROUP * gidx
    sel = (lane >= lo) & (lane < lo + EXPERTS_PER_GROUP)
    le = jnp.where(sel, logits, NEG_INF)
    ee = jnp.where(sel, jnp.exp(le - jnp.max(le, axis=1, keepdims=True)), 0.0)
    pe = ee / jnp.sum(ee, axis=1, keepdims=True)
    v1 = jnp.max(jnp.where(sel, pe, -1.0), axis=1, keepdims=True)
    i1 = jnp.min(jnp.where(sel & (pe == v1), lane, big), axis=1, keepdims=True)
    rest = sel & (lane != i1)
    v2 = jnp.max(jnp.where(rest, pe, -1.0), axis=1, keepdims=True)
    i2 = jnp.min(jnp.where(rest & (pe == v2), lane, big), axis=1, keepdims=True)
    tot = v1 + v2
    w1 = v1 / tot * p_g
    w2 = v2 / tot * p_g
    return jnp.where(lane == i1, w1, jnp.where(lane == i2, w2, 0.0))


def _moe_ln_kernel(x_ref, wrh_ref, wrl_ref, br_ref, w1_ref, w3_ref, w2_ref, g_ref, b_ref, y_ref,
                   xb_sc, gate_sc, acc_sc, *, alpha):
    e = pl.program_id(1)

    @pl.when(e == 0)
    def _():
        x = x_ref[...]
        xh = x.astype(BF16)
        xl = (x - xh.astype(F32)).astype(BF16)
        xb_sc[...] = xh
        logits = (jnp.dot(xh, wrh_ref[...], preferred_element_type=F32)
                  + jnp.dot(xl, wrh_ref[...], preferred_element_type=F32)
                  + jnp.dot(xh, wrl_ref[...], preferred_element_type=F32) + br_ref[...])
        gate_sc[...] = _route(logits)
        acc_sc[...] = jnp.zeros_like(acc_sc)

    xb = xb_sc[...]
    h1 = jnp.dot(xb, w1_ref[0].astype(BF16), preferred_element_type=F32)
    h3 = jnp.dot(xb, w3_ref[0].astype(BF16), preferred_element_type=F32)
    hdn = (h1 * jax.nn.sigmoid(h1)) * h3
    y = jnp.dot(hdn.astype(BF16), w2_ref[0].astype(BF16), preferred_element_type=F32)
    lane = lax.broadcasted_iota(jnp.int32, (1, LANES), 1)
    ge = jnp.sum(jnp.where(lane == e + GATE_LANE0, gate_sc[...], 0.0), axis=1, keepdims=True)
    acc_sc[...] += ge * y

    @pl.when(e == pl.num_programs(1) - 1)
    def _():
        y_ref[...] = _layer_norm(alpha * x_ref[...] + acc_sc[...], g_ref[...], b_ref[...])


def _moe_ln(x, wrh, wrl, br, w1, w3, w2, g, b, alpha):
    n, d = x.shape
    tm = min(MOE_TILE, n)
    assert n % tm == 0
    ne = w1.shape[0]
    per_expert = lambda a: pl.BlockSpec((1,) + a.shape[1:], lambda i, e: (e, 0, 0))
    full = lambda a: pl.BlockSpec(a.shape, lambda i, e: (0, 0))
    return pl.pallas_call(
        functools.partial(_moe_ln_kernel, alpha=alpha),
        grid=(n // tm, ne),
        in_specs=[pl.BlockSpec((tm, d), lambda i, e: (i, 0)), full(wrh), full(wrl), full(br),
                  per_expert(w1), per_expert(w3), per_expert(w2),
                  full(g), full(b)],
        out_specs=pl.BlockSpec((tm, d), lambda i, e: (i, 0)),
        out_shape=jax.ShapeDtypeStruct((n, d), F32),
        scratch_shapes=[pltpu.VMEM((tm, d), BF16), pltpu.VMEM((tm, LANES), F32), pltpu.VMEM((tm, d), F32)],
        compiler_params=_cparams(("parallel", "arbitrary")),
        name="moe_ln",
    )(x, wrh, wrl, br, w1, w3, w2, g, b)


def _proj_c_kernel(x_ref, win_ref, gq_ref, gkv_ref, wuq_ref, wrot_ref, cq_ref, sq_ref,
                   ck_ref, s1k_ref, s2k_ref, q_ref, ckv_ref, kr_ref):
    xb = x_ref[0].astype(BF16)
    h = jnp.dot(xb, win_ref[...], preferred_element_type=F32)
    qa = h[:, :Q_RANK]
    kva = h[:, Q_RANK:Q_RANK + KV_RANK]
    krw = h[:, Q_RANK + KV_RANK:]
    qn = qa * lax.rsqrt(jnp.mean(qa * qa, axis=1, keepdims=True) + RMS_EPS) * gq_ref[...]
    ckv_ref[0] = kva * lax.rsqrt(jnp.mean(kva * kva, axis=1, keepdims=True) + RMS_EPS) * gkv_ref[...]
    half = ROPE_DIM // 2
    kr = _rope3(krw, ck_ref[...], s1k_ref[...], s2k_ref[...], LANES - half, half)
    kr_ref[0] = kr[:, :ROPE_DIM]
    qnb = qn.astype(BF16)
    q = jnp.dot(qnb, wuq_ref[...], preferred_element_type=F32)
    q_rot = jnp.dot(qnb, wrot_ref[...], preferred_element_type=F32)
    cq, sq = cq_ref[...], sq_ref[...]
    scale = (NOPE_DIM + ROPE_DIM) ** -0.5 * LOG2E
    for hd in range(H_C):
        sl = slice(hd * LANES, (hd + 1) * LANES)
        q_ref[0, :, sl] = ((q[:, sl] * cq + q_rot[:, sl] * sq) * scale).astype(BF16)


def _proj_c(x, win, gq, gkv, wuq, wrot, tabs_q, tabs_k):
    nb, t, _ = x.shape
    tm = min(ROW_TILE, t)
    assert t % tm == 0
    tok = lambda width: pl.BlockSpec((1, tm, width), lambda b, i: (b, i, 0))
    tab = pl.BlockSpec((tm, LANES), lambda b, i: (i, 0))
    full = lambda a: pl.BlockSpec(a.shape, lambda b, i: (0, 0))
    return pl.pallas_call(
        _proj_c_kernel,
        grid=(nb, t // tm),
        in_specs=[tok(D_MODEL), full(win), full(gq), full(gkv), full(wuq), full(wrot)] + [tab] * 5,
        out_specs=[tok(H_C * LANES), tok(KV_RANK), tok(ROPE_DIM)],
        out_shape=[jax.ShapeDtypeStruct((nb, t, H_C * LANES), BF16),
                   jax.ShapeDtypeStruct((nb, t, KV_RANK), F32),
                   jax.ShapeDtypeStruct((nb, t, ROPE_DIM), F32)],
        compiler_params=_cparams(("parallel", "parallel")),
        name="proj_c",
    )(x, win, gq, gkv, wuq, wrot, *tabs_q, *tabs_k)


def _kv_up_kernel(ckv_ref, kr_ref, wk_ref, place_ref, wvt_ref, k_ref, vt_ref):
    cb = ckv_ref[...].astype(BF16)
    k = (jnp.dot(cb, wk_ref[...], preferred_element_type=F32)
         + jnp.dot(kr_ref[...].astype(BF16), place_ref[...], preferred_element_type=F32))
    k_ref[0] = k.astype(BF16)
    vt = lax.dot_general(wvt_ref[...], cb, (((1,), (1,)), ((), ())), preferred_element_type=F32)
    vt_ref[0] = vt.astype(BF16)


def _kv_up(ckv, kr, wk, place, wvt):
    n = ckv.shape[0]
    tm = ATTN_BLOCK
    assert n % tm == 0
    row = lambda width: pl.BlockSpec((tm, width), lambda i: (i, 0))
    full = lambda a: pl.BlockSpec(a.shape, lambda i: (0, 0))
    return pl.pallas_call(
        _kv_up_kernel,
        grid=(n // tm,),
        in_specs=[row(KV_RANK), row(ROPE_DIM), full(wk), full(place), full(wvt)],
        out_specs=[pl.BlockSpec((1, tm, H_C * LANES), lambda i: (i, 0, 0)),
                   pl.BlockSpec((1, H_C * V_DIM_C, tm), lambda i: (i, 0, 0))],
        out_shape=[jax.ShapeDtypeStruct((n // tm, tm, H_C * LANES), BF16),
                   jax.ShapeDtypeStruct((n // tm, H_C * V_DIM_C, tm), BF16)],
        compiler_params=_cparams(("parallel",)),
        name="kv_up",
    )(ckv, kr, wk, place, wvt)


def _mla_decode_kernel(q_ref, ckv_ref, kr_ref, ckvn_ref, krn_ref, wabs_ref, wv_ref, o_ref):
    q = q_ref[0]
    ts = q.shape[0]
    qs = jnp.concatenate(
        [jnp.dot(q[:, h * LANES:(h + 1) * LANES], wabs_ref[h], preferred_element_type=F32).astype(BF16)
         for h in range(H_C)], axis=0)
    kc = jnp.concatenate([ckv_ref[0].astype(BF16), kr_ref[0].astype(BF16)], axis=1)
    kn = jnp.concatenate([ckvn_ref[0].astype(BF16), krn_ref[0].astype(BF16)], axis=1)
    nt = (((1,), (1,)), ((), ()))
    sc = lax.dot_general(qs, kc, nt, preferred_element_type=F32)
    sn = lax.dot_general(qs, kn, nt, preferred_element_type=F32)
    m = jnp.maximum(jnp.max(sc, axis=1, keepdims=True), jnp.max(sn, axis=1, keepdims=True))
    pc = jnp.exp2(sc - m)
    pn = jnp.exp2(sn - m)
    l = jnp.sum(pc, axis=1, keepdims=True) + jnp.sum(pn, axis=1, keepdims=True)
    ol = (jnp.dot(pc.astype(BF16), kc[:, :KV_RANK], preferred_element_type=F32)
          + jnp.dot(pn.astype(BF16), kn[:, :KV_RANK], preferred_element_type=F32)) / l
    olb = ol.astype(BF16)
    o = jnp.dot(olb[:ts], wv_ref[0], preferred_element_type=F32)
    for h in range(1, H_C):
        o = o + jnp.dot(olb[h * ts:(h + 1) * ts], wv_ref[h], preferred_element_type=F32)
    o_ref[0] = o.astype(o_ref.dtype)


def _mla_decode(q, ckv_c, kr_c, ckv_n, kr_n, w_abs, w_vout):
    nb, ts, _ = q.shape
    assert ckv_c.shape[1] % CHUNK == 0 and ts <= CHUNK
    per_b = lambda a: pl.BlockSpec((1,) + a.shape[1:], lambda b: (b, 0, 0))
    full = lambda a: pl.BlockSpec(a.shape, lambda b: (0, 0, 0))
    return pl.pallas_call(
        _mla_decode_kernel,
        grid=(nb,),
        in_specs=[per_b(q), per_b(ckv_c), per_b(kr_c), per_b(ckv_n), per_b(kr_n), full(w_abs), full(w_vout)],
        out_specs=pl.BlockSpec((1, ts, H_C * V_DIM_C), lambda b: (b, 0, 0)),
        out_shape=jax.ShapeDtypeStruct((nb, ts, H_C * V_DIM_C), BF16),
        compiler_params=_cparams(("parallel",)),
        name="mla_decode",
    )(q, ckv_c, kr_c, ckv_n, kr_n, w_abs, w_vout)


def _rope_tables(pos, dim, lane0):
    half = dim // 2
    inv = ROPE_THETA ** (-jnp.arange(0, dim, 2, dtype=F32) / dim)
    ang = pos.astype(F32)[:, None] * inv[None, :]
    cos, sin = jnp.cos(ang), jnp.sin(ang)
    zero = jnp.zeros_like(sin)
    c = jnp.concatenate([cos, cos], axis=1)
    s1 = jnp.concatenate([-sin, zero], axis=1)
    s2 = jnp.concatenate([zero, sin], axis=1)
    if lane0 < 0:
        reps = LANES // dim
        return tuple(jnp.tile(a, (1, reps)) for a in (c, s1, s2))
    t = pos.shape[0]
    pad = lambda a, fill: jnp.concatenate(
        [jnp.full((t, lane0), fill, F32), a, jnp.full((t, LANES - lane0 - dim), fill, F32)], axis=1)
    return pad(c, 1.0), pad(s1, 0.0), pad(s2, 0.0)


def _pad_cols(a, width):
    return jnp.pad(a, ((0, 0), (0, width - a.shape[1])))


def _blocks(a, tk):
    nb, t, l = a.shape
    return a.reshape(nb, t // tk, tk, l)


def _cat_pad_time(cache, new, t_pad):
    nb, t0, l = cache.shape
    t1 = new.shape[1]
    return jnp.concatenate([cache, new, jnp.zeros((nb, t_pad - t0 - t1, l), cache.dtype)], axis=1)


def kernel(x_prompt, x_sample, cache_fox_k, cache_fox_v, cache_fox_logf, cache_diff_k, cache_diff_v, cache_mla_ckv, cache_mla_krope, w_in_ab, b_fgate, diff_lq1, diff_lk1, diff_lq2, diff_lk2, diff_subln, w_out_ab, w_in_c, mla_q_norm, mla_kv_norm, mla_w_uq, mla_w_ukv, w_out_c, ln1_g, ln1_b, ln2_g, ln2_b, moe_wg, moe_bg, moe_we, moe_be, moe_w1, moe_w3, moe_w2):
    bp, tp, d = x_prompt.shape
    bs, ts, _ = x_sample.shape
    past = cache_fox_k.shape[2]
    depth = ln1_g.shape[0]
    alpha = (2 * depth) ** 0.25
    tk = ATTN_BLOCK
    assert past % tk == 0
    ns = bs * ts
    t_dec = past + tk

    pos_p = jnp.arange(tp)
    pos_s = jnp.tile(past + jnp.arange(ts), bs)

    xp = x_prompt
    xs = x_sample.reshape(1, ns, d)
    out_ab_p, out_ab_s, out_c_p, out_c_s = [], [], [], []

    for i in range(depth):
        j = i // 2
        if i % 2 == 0:
            lam_init = 0.8 - 0.6 * math.exp(-0.3 * i)
            cuts = [0, A_WIDTH, 2 * A_WIDTH, 3 * A_WIDTH, 3 * A_WIDTH + H_A,
                    3 * A_WIDTH + H_A + B_QK_WIDTH, 3 * A_WIDTH + H_A + 2 * B_QK_WIDTH,
                    3 * A_WIDTH + H_A + 2 * B_QK_WIDTH + B_V_WIDTH]
            w = w_in_ab[j]
            piece = lambda a: w[:, cuts[a]:cuts[a + 1]]
            w6 = jnp.stack([piece(0), piece(1), piece(2), piece(4), piece(5), piece(6)]).astype(BF16)
            wvt = jnp.stack([piece(2).T, piece(6).T]).astype(BF16)
            wf = _pad_cols(piece(3), LANES).astype(BF16)
            bf = _pad_cols(b_fgate[j][None, :], LANES)
            wout = w_out_ab[j].astype(BF16)
            diff_extra = (diff_lq1[j][None, :], diff_lk1[j][None, :], diff_lq2[j][None, :],
                          diff_lk2[j][None, :], diff_subln[j][None, :])

            tabs = _rope_tables(pos_p, HEAD_DIM, -1)
            (qa, ka, kab, va, vat, lf, lfw, qb, kb, kbb, vb, vbt) = _proj_ab(xp, w6, wvt, wf, bf, tabs)
            bias = _blocks(_decay_bias(lfw), tk)
            oa = _attention("fox", qa, _blocks(kab, tk), vat, (bias,), n_pairs=H_A // 2, mask_shift=0)
            ob = _attention("diff", qb, _blocks(kbb, tk), vbt, diff_extra,
                            n_pairs=H_B, mask_shift=int(math.log2(CHUNK)), lam_init=lam_init)
            out_ab_p.append((ka.reshape(bp, tp, H_A, HEAD_DIM), va.reshape(bp, tp, H_A, HEAD_DIM), lf,
                             kb.reshape(bp, tp, H_B, 2, HEAD_DIM), vb.reshape(bp, tp, H_B, 2 * HEAD_DIM)))
            xp2 = _outproj_ln(xp.reshape(bp * tp, d), [oa.reshape(bp * tp, -1), ob.reshape(bp * tp, -1)],
                              [wout[:A_WIDTH], wout[A_WIDTH:]], ln1_g[i][None, :], ln1_b[i][None, :], alpha)

            tabs = _rope_tables(pos_s, HEAD_DIM, -1)
            (qa, ka, kab, va, _, lf, lfw, qb, kb, kbb, vb, _) = _proj_ab(xs, w6, wvt, wf, bf, tabs)
            rs = lambda a: a.reshape(bs, ts, a.shape[-1])
            cache_lfw = jnp.pad(cache_fox_logf[j].astype(F32), ((0, 0), (0, 0), (0, LANES - H_A)))
            bias = _decay_bias(_cat_pad_time(cache_lfw, rs(lfw), t_dec))
            flat = lambda c: c.reshape(bs, past, -1)
            oa = _decode_attention("fox", rs(qa), flat(cache_fox_k[j]), flat(cache_fox_v[j]), rs(kab), rs(va),
                                   (bias,))
            ob = _decode_attention("diff", rs(qb), flat(cache_diff_k[j]), flat(cache_diff_v[j]), rs(kbb),
                                   rs(vb), diff_extra, lam_init=lam_init)
            out_ab_s.append((ka.reshape(bs, ts, H_A, HEAD_DIM), va.reshape(bs, ts, H_A, HEAD_DIM),
                             lf.reshape(bs, ts, H_A), kb.reshape(bs, ts, H_B, 2, HEAD_DIM),
                             vb.reshape(bs, ts, H_B, 2 * HEAD_DIM)))
            xs2 = _outproj_ln(xs.reshape(ns, d), [oa.reshape(ns, -1), ob.reshape(ns, -1)],
                              [wout[:A_WIDTH], wout[A_WIDTH:]], ln1_g[i][None, :], ln1_b[i][None, :], alpha)
        else:
            wc = w_in_c[j]
            kr_cols = _pad_cols(wc[:, Q_RANK + KV_RANK:], LANES)
            win = jnp.concatenate([wc[:, :Q_RANK + KV_RANK], kr_cols], axis=1).astype(BF16)
            wuq3 = jnp.pad(mla_w_uq[j].reshape(Q_RANK, H_C, NOPE_DIM + ROPE_DIM),
                           ((0, 0), (0, 0), (0, LANES - NOPE_DIM - ROPE_DIM)))
            wuq = wuq3.reshape(Q_RANK, H_C * LANES).astype(BF16)
            r0, r1, r2 = NOPE_DIM, NOPE_DIM + ROPE_DIM // 2, NOPE_DIM + ROPE_DIM
            wrot = jnp.zeros_like(wuq3).at[:, :, r0:r1].set(-wuq3[:, :, r1:r2]).at[:, :, r1:r2].set(wuq3[:, :, r0:r1])
            wrot = wrot.reshape(Q_RANK, H_C * LANES).astype(BF16)
            q_tabs = lambda pos: (lambda c, s1, s2: (c, s2 - s1))(*_rope_tables(pos, ROPE_DIM, NOPE_DIM))
            wukv = mla_w_ukv[j].reshape(KV_RANK, H_C, NOPE_DIM + V_DIM_C)
            wk = jnp.pad(wukv[:, :, :NOPE_DIM], ((0, 0), (0, 0), (0, LANES - NOPE_DIM)))
            wk = wk.reshape(KV_RANK, H_C * LANES).astype(BF16)
            wvt = wukv[:, :, NOPE_DIM:].reshape(KV_RANK, H_C * V_DIM_C).T.astype(BF16)
            place = jnp.tile(_pad_cols(jnp.concatenate(
                [jnp.zeros((ROPE_DIM, NOPE_DIM), F32), jnp.eye(ROPE_DIM, dtype=F32)], axis=1), LANES),
                (1, H_C)).astype(BF16)
            gq = mla_q_norm[j][None, :]
            gkv = mla_kv_norm[j][None, :]
            wout = w_out_c[j].astype(BF16)

            q, ckv, kr = _proj_c(xp, win, gq, gkv, wuq, wrot, q_tabs(pos_p), _rope_tables(pos_p, ROPE_DIM, 0))
            kc, vct = _kv_up(ckv.reshape(bp * tp, KV_RANK), kr.reshape(bp * tp, ROPE_DIM), wk, place, wvt)
            per_seq = lambda a, nb: a.reshape((nb, a.shape[0] // nb) + a.shape[1:])
            oc = _attention("mla", q, per_seq(kc, bp), per_seq(vct, bp), (), n_pairs=H_C // 2,
                            mask_shift=int(math.log2(CHUNK)))
            out_c_p.append((ckv, kr))
            xp2 = _outproj_ln(xp.reshape(bp * tp, d), [oc.reshape(bp * tp, -1)], [wout],
                              ln1_g[i][None, :], ln1_b[i][None, :], alpha)

            q, ckv, kr = _proj_c(xs, win, gq, gkv, wuq, wrot, q_tabs(pos_s), _rope_tables(pos_s, ROPE_DIM, 0))
            w_abs = jnp.zeros((H_C, LANES, 2 * LANES), F32)
            w_abs = w_abs.at[:, :NOPE_DIM, :KV_RANK].set(jnp.transpose(wukv[:, :, :NOPE_DIM], (1, 2, 0)))
            w_abs = w_abs.at[:, NOPE_DIM:NOPE_DIM + ROPE_DIM, KV_RANK:KV_RANK + ROPE_DIM].set(
                jnp.eye(ROPE_DIM, dtype=F32))
            w_vout = jnp.einsum("khd,hg->hkgd", wukv[:, :, NOPE_DIM:], jnp.eye(H_C, dtype=F32))
            w_vout = w_vout.reshape(H_C, KV_RANK, H_C * V_DIM_C)
            wide = lambda a: jnp.pad(a.astype(F32), ((0, 0), (0, 0), (0, LANES - ROPE_DIM)))
            oc = _mla_decode(q.reshape(bs, ts, -1), cache_mla_ckv[j].astype(F32), wide(cache_mla_krope[j]),
                             ckv.reshape(bs, ts, KV_RANK), wide(kr.reshape(bs, ts, ROPE_DIM)),
                             w_abs.astype(BF16), w_vout.astype(BF16))
            out_c_s.append((ckv.reshape(bs, ts, KV_RANK), kr.reshape(bs, ts, ROPE_DIM)))
            xs2 = _outproj_ln(xs.reshape(ns, d), [oc.reshape(ns, -1)], [wout],
                              ln1_g[i][None, :], ln1_b[i][None, :], alpha)

        wr = _pad_cols(jnp.concatenate(
            [moe_wg[i]] + [moe_we[i][gi] for gi in range(N_GROUPS)], axis=1), LANES)
        wrh = wr.astype(BF16)
        wrl = (wr - wrh.astype(F32)).astype(BF16)
        br = _pad_cols(jnp.concatenate([moe_bg[i], moe_be[i].reshape(-1)])[None, :], LANES)
        moe_w = (moe_w1[i], moe_w3[i], moe_w2[i])
        g2, b2 = ln2_g[i][None, :], ln2_b[i][None, :]
        xp = _moe_ln(xp2, wrh, wrl, br, *moe_w, g2, b2, alpha).reshape(bp, tp, d)
        xs = _moe_ln(xs2, wrh, wrl, br, *moe_w, g2, b2, alpha).reshape(1, ns, d)

    stack = lambda rows, n: jnp.stack([r[n] for r in rows])
    return (xp, xs.reshape(bs, ts, d),
            stack(out_ab_p, 0), stack(out_ab_p, 1), stack(out_ab_p, 2), stack(out_ab_p, 3), stack(out_ab_p, 4),
            stack(out_c_p, 0), stack(out_c_p, 1),
            stack(out_ab_s, 0), stack(out_ab_s, 1), stack(out_ab_s, 2), stack(out_ab_s, 3), stack(out_ab_s, 4),
            stack(out_c_s, 0), stack(out_c_s, 1))
```

```python
import functools
import math

import jax
import jax.numpy as jnp
from jax import lax
from jax.experimental import pallas as pl
from jax.experimental.pallas import tpu as pltpu

F32 = jnp.float32
BF16 = jnp.bfloat16

D_MODEL = 1024
CHUNK = 64
HEAD_DIM = 64
ROPE_THETA = 10000.0
H_A = 8
H_B = 4
H_C = 16
Q_RANK = 256
KV_RANK = 128
NOPE_DIM = 64
ROPE_DIM = 32
V_DIM_C = 64
N_GROUPS = 4
EXPERTS_PER_GROUP = 4
N_EXPERTS = N_GROUPS * EXPERTS_PER_GROUP
D_EXPERT = 256
A_WIDTH = H_A * HEAD_DIM
B_QK_WIDTH = H_B * 2 * HEAD_DIM
B_V_WIDTH = H_B * 2 * HEAD_DIM
FGATE_BIAS = 3.0
LN_EPS = 1e-5
RMS_EPS = 1e-6
NEG_INF = -1e30
LOG2E = math.log2(math.e)

LANES = 128
BF16_ROWS = 16
PANEL = {"fox": 1024, "diff": 256, "mla": 1024}
BIAS_PIECES = 3
VMEM_LIMIT = 56 * 1024 * 1024
ATTN_BLOCK = 512
Q_KEY_BLOCKS = 4
ROW_TILE = 512
MOE_TILE = 1024
BIAS_ROWS_PER_STEP = 4096
GATE_LANE0 = N_GROUPS


def _cparams(sem):
    return pltpu.CompilerParams(dimension_semantics=sem, vmem_limit_bytes=VMEM_LIMIT)


def _rope3(x, c, s1, s2, shift_up, shift_down):
    return x * c + pltpu.roll(x, shift_up, 1) * s1 + pltpu.roll(x, shift_down, 1) * s2


def _layer_norm(y, g, b):
    mu = jnp.mean(y, axis=-1, keepdims=True)
    d = y - mu
    var = jnp.mean(d * d, axis=-1, keepdims=True)
    return d * lax.rsqrt(var + LN_EPS) * g + b


def _split3(x):
    hi = x.astype(BF16)
    r1 = x - hi.astype(F32)
    mid = r1.astype(BF16)
    return hi, mid, (r1 - mid.astype(F32)).astype(BF16)


def _proj_ab_kernel(x_ref, w_ref, wvt_ref, wf_ref, bf_ref, c_ref, s1_ref, s2_ref,
                    qa_ref, ka_ref, kab_ref, va_ref, vat_ref, lf_ref, lfw_ref,
                    qb_ref, kb_ref, kbb_ref, vb_ref, vbt_ref):
    xb = x_ref[0].astype(BF16)

    def mm(i):
        return jnp.dot(xb, w_ref[i], preferred_element_type=F32)

    def mm_t(i):
        return lax.dot_general(wvt_ref[i], xb, (((1,), (1,)), ((), ())), preferred_element_type=F32)

    qa_ref[0] = (mm(0) * (HEAD_DIM ** -0.5 * LOG2E)).astype(BF16)
    ka = mm(1)
    ka_ref[0] = ka
    kab_ref[0] = ka.astype(BF16)
    va_ref[0] = mm(2)
    vat_ref[0, 0] = mm_t(0).astype(BF16)

    z = jnp.dot(xb, wf_ref[...], preferred_element_type=F32) + bf_ref[...]
    lf = jnp.minimum(z, 0.0) - jnp.log1p(jnp.exp(-jnp.abs(z)))
    lf_ref[0] = lf[:, :H_A]
    lfw_ref[0] = lf

    c, s1, s2 = c_ref[...], s1_ref[...], s2_ref[...]
    qb = mm(3)
    kb = mm(4)
    for s in range(B_QK_WIDTH // LANES):
        sl = slice(s * LANES, (s + 1) * LANES)
        qs = _rope3(qb[:, sl], c, s1, s2, LANES - HEAD_DIM // 2, HEAD_DIM // 2)
        qb_ref[0, :, sl] = (qs * (HEAD_DIM ** -0.5 * LOG2E)).astype(BF16)
        ks = _rope3(kb[:, sl], c, s1, s2, LANES - HEAD_DIM // 2, HEAD_DIM // 2)
        kb_ref[0, :, sl] = ks
        kbb_ref[0, :, sl] = ks.astype(BF16)
    vb_ref[0] = mm(5)
    vbt_ref[0, 0] = mm_t(1).astype(BF16)


def _proj_ab(x, w6, wvt, wf, bf, tabs):
    nb, t, _ = x.shape
    tm = min(ROW_TILE, t)
    assert t % tm == 0
    w = A_WIDTH
    tok = lambda width: pl.BlockSpec((1, tm, width), lambda b, i: (b, i, 0))
    tr = pl.BlockSpec((1, 1, w, tm), lambda b, i: (b, i, 0, 0))
    tab = pl.BlockSpec((tm, LANES), lambda b, i: (i, 0))
    full = lambda a: pl.BlockSpec(a.shape, lambda b, i: (0,) * a.ndim)
    sds = lambda width, dt: jax.ShapeDtypeStruct((nb, t, width), dt)
    sds_t = jax.ShapeDtypeStruct((nb, t // tm, w, tm), BF16)
    return pl.pallas_call(
        _proj_ab_kernel,
        grid=(nb, t // tm),
        in_specs=[tok(D_MODEL), full(w6), full(wvt), full(wf), full(bf), tab, tab, tab],
        out_specs=[tok(w), tok(w), tok(w), tok(w), tr, tok(H_A), tok(LANES), tok(w), tok(w), tok(w), tok(w), tr],
        out_shape=[sds(w, BF16), sds(w, F32), sds(w, BF16), sds(w, F32), sds_t, sds(H_A, F32), sds(LANES, F32),
                   sds(w, BF16), sds(w, F32), sds(w, BF16), sds(w, F32), sds_t],
        compiler_params=_cparams(("parallel", "parallel")),
        name="proj_ab",
    )(x, w6, wvt, wf, bf, *tabs)


def _decay_bias_kernel(lf_ref, spread_ref, lower_ref, o_ref, carry_ref):
    @pl.when(pl.program_id(1) == 0)
    def _():
        carry_ref[...] = jnp.zeros_like(carry_ref)

    spread = spread_ref[...]
    lower = lower_ref[...]
    tc = lower.shape[0]
    lane = lax.broadcasted_iota(jnp.int32, (1, LANES), 1).astype(F32)
    piece = lane - BIAS_PIECES * jnp.floor((lane + 0.5) * (1.0 / BIAS_PIECES))
    carry = carry_ref[...]
    for r in range(lf_ref.shape[1] // tc):
        rows = slice(r * tc, (r + 1) * tc)
        x = lf_ref[0, rows, :]
        xr = sum(jnp.dot(p, spread, preferred_element_type=F32) for p in _split3(x))
        c = sum(jnp.dot(lower, p, preferred_element_type=F32) for p in _split3(xr)) + carry
        carry = c[tc - 1:tc, :]
        hi, mid, lo = (p.astype(F32) for p in _split3(c * (-LOG2E)))
        o_ref[0, rows, :] = jnp.where(piece == 0.0, hi, jnp.where(piece == 1.0, mid, lo)).astype(BF16)
    carry_ref[...] = carry


def _decay_bias(lf_wide):
    nb, t, _ = lf_wide.shape
    tc = min(ATTN_BLOCK, t)
    tb = min(BIAS_ROWS_PER_STEP, t)
    assert t % tb == 0 and tb % tc == 0
    spec = pl.BlockSpec((1, tb, LANES), lambda b, i: (b, i, 0))
    src = jnp.arange(LANES)[:, None]
    dst = jnp.arange(LANES)[None, :]
    spread = ((dst // BIAS_PIECES == src) & (src < H_A)).astype(BF16)
    lower = jnp.tril(jnp.ones((tc, tc), BF16))
    const = lambda a: pl.BlockSpec(a.shape, lambda b, i: (0, 0))
    return pl.pallas_call(
        _decay_bias_kernel,
        grid=(nb, t // tb),
        in_specs=[spec, const(spread), const(lower)],
        out_specs=spec,
        out_shape=jax.ShapeDtypeStruct((nb, t, LANES), BF16),
        scratch_shapes=[pltpu.VMEM((1, LANES), F32)],
        compiler_params=_cparams(("parallel", "arbitrary")),
        name="cumsum",
    )(lf_wide, spread, lower)


def _attn_kernel(*refs, mode, tq, tk, mask_shift, lam_init):
    if mode == "diff":
        q_ref, k_ref, vt_ref, lq1_ref, lk1_ref, lq2_ref, lk2_ref, sub_ref, o_ref = refs[:9]
    elif mode == "fox":
        q_ref, k_ref, vt_ref, b_ref, o_ref = refs[:5]
    else:
        q_ref, k_ref, vt_ref, o_ref = refs[:4]
    m_sc, acc_sc, sa, bma, sb, bmb = refs[-6:]
    v_rows = LANES if mode == "diff" else HEAD_DIM
    sa_sc, sb_sc = (sa, bma), (sb, bmb)

    qi = pl.program_id(2)
    q = q_ref[0]
    lane = lax.broadcasted_iota(jnp.int32, (1, LANES), 1)
    if mode == "mla":
        qs = [q[:, :LANES], q[:, LANES:]]
    else:
        zero = jnp.zeros_like(q)
        qs = [jnp.where(lane < HEAD_DIM, q, zero), jnp.where(lane >= HEAD_DIM, q, zero)]
        if mode == "fox":
            def pick(i):
                lo = BIAS_PIECES * (2 * pl.program_id(1) + i)
                hot = jnp.where((lane >= lo) & (lane < lo + BIAS_PIECES), 1.0, 0.0)
                return jnp.broadcast_to(hot, (tq, LANES)).astype(BF16)

            qs = [jnp.concatenate([qs[i], pick(i)], axis=1) for i in range(2)]

    m_sc[...] = jnp.full(m_sc.shape, NEG_INF, F32)
    acc_sc[...] = jnp.zeros(acc_sc.shape, F32)

    pw = min(PANEL[mode], tq)

    def scores(j, bufs, q0, q1):
        s_sc, bm_sc = bufs
        cs = slice(q0, q1)
        k = k_ref[0, j]
        if mode == "fox":
            k = jnp.concatenate([k, b_ref[0, j]], axis=1)
        for i in range(2):
            ki = k[:, i * LANES:(i + 1) * LANES] if mode == "mla" else k
            st = lax.dot_general(ki, qs[i][cs], (((1,), (1,)), ((), ())), preferred_element_type=F32)
            s_sc[i, :, cs] = st
            bm_sc[i, :, cs] = jnp.max(st, axis=0, keepdims=True)

    def consume(j, bufs, q0, q1, key0=None):
        s_sc, bm_sc = bufs
        cs = slice(q0, q1)
        vt = vt_ref[0, j]
        masked = key0 is not None and ((key0 + tk - 1) >> mask_shift) > (q0 >> mask_shift)
        for i in range(2):
            st = s_sc[i, :, cs]
            if masked:
                key = lax.broadcasted_iota(jnp.int32, (tk, q1 - q0), 0) + key0
                qry = lax.broadcasted_iota(jnp.int32, (tk, q1 - q0), 1) + q0
                vis = lax.shift_right_logical(key, mask_shift) <= lax.shift_right_logical(qry, mask_shift)
                st = jnp.where(vis, st, NEG_INF)
                blk_max = jnp.max(st, axis=0, keepdims=True)
            else:
                blk_max = bm_sc[i, :, cs]
            m_prev = m_sc[i, :, cs]
            m_new = jnp.maximum(m_prev, blk_max)
            alpha = jnp.exp2(m_prev - m_new)
            p = jnp.exp2(st - m_new).astype(BF16)
            vi = vt if mode == "diff" else vt[i * HEAD_DIM:(i + 1) * HEAD_DIM]
            vi = jnp.concatenate([vi, jnp.ones((BF16_ROWS, tk), BF16)], axis=0)
            acc_sc[i, :, cs] = alpha * acc_sc[i, :, cs] + jnp.dot(vi, p, preferred_element_type=F32)
            m_sc[i, :, cs] = m_new

    def stage(nxt, cur):
        for q0 in range(0, tq, pw):
            scores(nxt[0], nxt[1], q0, q0 + pw)
            consume(cur[0], cur[1], q0, q0 + pw)

    n_diag = tq // tk
    nfull = n_diag * qi
    bufs = (sa_sc, sb_sc)

    def trip(jj, carry):
        for d in range(n_diag):
            j = n_diag * jj + d
            stage((j + 1, bufs[(d + 1) % 2]), (j, bufs[d % 2]))
        return carry

    for q0 in range(0, tq, pw):
        scores(0, sa_sc, q0, q0 + pw)
    lax.fori_loop(0, qi, trip, 0)

    pt = min(pw, tk)
    sees = lambda d, q1: ((d * tk) >> mask_shift) <= ((q1 - 1) >> mask_shift)
    for d in range(n_diag):
        for q0 in range(0, tq, pt):
            if d + 1 < n_diag and sees(d + 1, q0 + pt):
                scores(nfull + d + 1, bufs[(d + 1) % 2], q0, q0 + pt)
            if sees(d, q0 + pt):
                consume(nfull + d, bufs[d % 2], q0, q0 + pt, key0=d * tk)

    if mode == "diff":
        lam = (jnp.exp(jnp.sum(lq1_ref[...] * lk1_ref[...], axis=1, keepdims=True))
               - jnp.exp(jnp.sum(lq2_ref[...] * lk2_ref[...], axis=1, keepdims=True)) + lam_init)
    for q0 in range(0, tq, pw):
        cs = slice(q0, q0 + pw)
        o0 = acc_sc[0, :v_rows, cs] / acc_sc[0, v_rows:v_rows + 1, cs]
        o1 = acc_sc[1, :v_rows, cs] / acc_sc[1, v_rows:v_rows + 1, cs]
        if mode == "diff":
            o = o0 - lam * o1
            ms = jnp.mean(o * o, axis=0, keepdims=True)
            o = (o * lax.rsqrt(ms + RMS_EPS)).T * sub_ref[...] * (1.0 - lam_init)
        else:
            o = jnp.concatenate([o0, o1], axis=0).T
        o_ref[0, cs, :] = o.astype(o_ref.dtype)


def _attention(mode, q, k, vt, extra, *, n_pairs, mask_shift, lam_init=0.0):
    nb, t_q, _ = q.shape
    _, nkb, tk, _ = k.shape
    tq = Q_KEY_BLOCKS * tk
    assert t_q % tq == 0 and nkb == t_q // tk
    qw = 2 * LANES if mode == "mla" else LANES
    in_specs = [
        pl.BlockSpec((1, tq, qw), lambda b, p, i: (b, i, p)),
        pl.BlockSpec((1, nkb, tk, qw), lambda b, p, i: (b, 0, 0, p)),
        pl.BlockSpec((1, nkb, LANES, tk), lambda b, p, i: (b, 0, p, 0)),
    ]
    if mode == "fox":
        in_specs.append(pl.BlockSpec((1, nkb, tk, LANES), lambda b, p, i: (b, 0, 0, 0)))
    elif mode == "diff":
        in_specs += [pl.BlockSpec(a.shape, lambda b, p, i: (0, 0)) for a in extra]
    kern = functools.partial(_attn_kernel, mode=mode, tq=tq, tk=tk, mask_shift=mask_shift, lam_init=lam_init)
    return pl.pallas_call(
        kern,
        grid=(nb, n_pairs, t_q // tq),
        in_specs=in_specs,
        out_specs=pl.BlockSpec((1, tq, LANES), lambda b, p, i: (b, i, p)),
        out_shape=jax.ShapeDtypeStruct((nb, t_q, n_pairs * LANES), BF16),
        scratch_shapes=[pltpu.VMEM((2, 1, tq), F32),
                        pltpu.VMEM((2, (LANES if mode == "diff" else HEAD_DIM) + BF16_ROWS, tq), F32),
                        pltpu.VMEM((2, tk, tq), F32), pltpu.VMEM((2, 1, tq), F32),
                        pltpu.VMEM((2, tk, tq), F32), pltpu.VMEM((2, 1, tq), F32)],
        compiler_params=_cparams(("parallel", "parallel", "arbitrary")),
        name="attn_" + mode,
    )(q, k, vt, *extra)


def _decode_attn_kernel(*refs, mode, lam_init):
    if mode == "fox":
        q_ref, kc_ref, vc_ref, kn_ref, vn_ref, b_ref, o_ref = refs
    else:
        q_ref, kc_ref, vc_ref, kn_ref, vn_ref, lq1_ref, lk1_ref, lq2_ref, lk2_ref, sub_ref, o_ref = refs
        lam = (jnp.exp(jnp.sum(lq1_ref[...] * lk1_ref[...], axis=1, keepdims=True))
               - jnp.exp(jnp.sum(lq2_ref[...] * lk2_ref[...], axis=1, keepdims=True)) + lam_init)
    ts = q_ref.shape[1]
    past = kc_ref.shape[1]
    lane = lax.broadcasted_iota(jnp.int32, (1, LANES), 1)
    row = lax.broadcasted_iota(jnp.int32, (ts, ts), 0)
    col = lax.broadcasted_iota(jnp.int32, (ts, ts), 1)
    nt = (((1,), (1,)), ((), ()))
    for p in range(q_ref.shape[2] // LANES):
        sl = slice(p * LANES, (p + 1) * LANES)
        q = q_ref[0, :, sl]
        kc = kc_ref[0, :, sl].astype(BF16)
        kn = kn_ref[0, :, sl]
        vc = vc_ref[0, :, sl].astype(BF16)
        vn = vn_ref[0, :, sl].astype(BF16)
        if mode == "fox":
            kc = jnp.concatenate([kc, b_ref[0, :past, :]], axis=1)
            kn = jnp.concatenate([kn, b_ref[0, past:past + ts, :]], axis=1)
        zero = jnp.zeros_like(q)
        outs = []
        for i in range(2):
            qi = jnp.where(lane < HEAD_DIM, q, zero) if i == 0 else jnp.where(lane >= HEAD_DIM, q, zero)
            if mode == "fox":
                lo = BIAS_PIECES * (2 * p + i)
                hot = jnp.where((lane >= lo) & (lane < lo + BIAS_PIECES), 1.0, 0.0)
                qi = jnp.concatenate([qi, jnp.broadcast_to(hot, (ts, LANES)).astype(BF16)], axis=1)
            sc = lax.dot_general(qi, kc, nt, preferred_element_type=F32)
            sn = lax.dot_general(qi, kn, nt, preferred_element_type=F32)
            if mode == "fox":
                sn = jnp.where(col <= row, sn, NEG_INF)
            m = jnp.maximum(jnp.max(sc, axis=1, keepdims=True), jnp.max(sn, axis=1, keepdims=True))
            pc = jnp.exp2(sc - m)
            pn = jnp.exp2(sn - m)
            l = jnp.sum(pc, axis=1, keepdims=True) + jnp.sum(pn, axis=1, keepdims=True)
            outs.append((jnp.dot(pc.astype(BF16), vc, preferred_element_type=F32)
                         + jnp.dot(pn.astype(BF16), vn, preferred_element_type=F32)) / l)
        if mode == "fox":
            o = jnp.where(lane < HEAD_DIM, outs[0], outs[1])
        else:
            o = outs[0] - lam * outs[1]
            ms = jnp.mean(o * o, axis=1, keepdims=True)
            o = o * lax.rsqrt(ms + RMS_EPS) * sub_ref[...] * (1.0 - lam_init)
        o_ref[0, :, sl] = o.astype(o_ref.dtype)


def _decode_attention(mode, q, k_cache, v_cache, k_new, v_new, extra, lam_init=0.0):
    nb, ts, w = q.shape
    assert k_cache.shape[1] % CHUNK == 0 and ts <= CHUNK
    per_b = lambda a: pl.BlockSpec((1,) + a.shape[1:], lambda b: (b, 0, 0))
    if mode == "fox":
        extra_specs = [per_b(extra[0])]
    else:
        extra_specs = [pl.BlockSpec(a.shape, lambda b: (0, 0)) for a in extra]
    return pl.pallas_call(
        functools.partial(_decode_attn_kernel, mode=mode, lam_init=lam_init),
        grid=(nb,),
        in_specs=[per_b(q), per_b(k_cache), per_b(v_cache), per_b(k_new), per_b(v_new)] + extra_specs,
        out_specs=pl.BlockSpec((1, ts, w), lambda b: (b, 0, 0)),
        out_shape=jax.ShapeDtypeStruct((nb, ts, w), BF16),
        compiler_params=_cparams(("parallel",)),
        name="decode_" + mode,
    )(q, k_cache, v_cache, k_new, v_new, *extra)


def _outproj_ln_kernel(*refs, n_in, alpha):
    x_ref = refs[0]
    o_refs = refs[1:1 + n_in]
    w_refs = refs[1 + n_in:1 + 2 * n_in]
    g_ref, b_ref, y_ref = refs[1 + 2 * n_in:]
    mix = jnp.dot(o_refs[0][...], w_refs[0][...], preferred_element_type=F32)
    for o_r, w_r in zip(o_refs[1:], w_refs[1:]):
        mix = mix + jnp.dot(o_r[...], w_r[...], preferred_element_type=F32)
    y_ref[...] = _layer_norm(alpha * x_ref[...] + mix, g_ref[...], b_ref[...])


def _outproj_ln(x, outs, ws, g, b, alpha):
    n, d = x.shape
    tm = min(ROW_TILE, n)
    assert n % tm == 0
    row = lambda width: pl.BlockSpec((tm, width), lambda i: (i, 0))
    full = lambda a: pl.BlockSpec(a.shape, lambda i: (0, 0))
    return pl.pallas_call(
        functools.partial(_outproj_ln_kernel, n_in=len(outs), alpha=alpha),
        grid=(n // tm,),
        in_specs=[row(d)] + [row(o.shape[1]) for o in outs] + [full(w) for w in ws] + [full(g), full(b)],
        out_specs=row(d),
        out_shape=jax.ShapeDtypeStruct((n, d), F32),
        compiler_params=_cparams(("parallel",)),
        name="outproj_ln",
    )(x, *outs, *ws, g, b)


def _route(logits):
    lane = lax.broadcasted_iota(jnp.int32, logits.shape, 1).astype(F32)
    big = float(1 << 20)
    is_g = lane < N_GROUPS
    lg = jnp.where(is_g, logits, NEG_INF)
    eg = jnp.where(is_g, jnp.exp(lg - jnp.max(lg, axis=1, keepdims=True)), 0.0)
    pg = eg / jnp.sum(eg, axis=1, keepdims=True)
    p_g = jnp.max(pg, axis=1, keepdims=True)
    gidx = jnp.min(jnp.where(is_g & (pg == p_g), lane, big), axis=1, keepdims=True)
    lo = GATE_LANE0 + EXPERTS_PER_GROUP * gidx
    sel = (lane >= lo) & (lane < lo + EXPERTS_PER_GROUP)
    le = jnp.where(sel, logits, NEG_INF)
    ee = jnp.where(sel, jnp.exp(le - jnp.max(le, axis=1, keepdims=True)), 0.0)
    pe = ee / jnp.sum(ee, axis=1, keepdims=True)
    v1 = jnp.max(jnp.where(sel, pe, -1.0), axis=1, keepdims=True)
    i1 = jnp.min(jnp.where(sel & (pe == v1), lane, big), axis=1, keepdims=True)
    rest = sel & (lane != i1)
    v2 = jnp.max(jnp.where(rest, pe, -1.0), axis=1, keepdims=True)
    i2 = jnp.min(jnp.where(rest & (pe == v2), lane, big), axis=1, keepdims=True)
    tot = v1 + v2
    w1 = v1 / tot * p_g
    w2 = v2 / tot * p_g
    return jnp.where(lane == i1, w1, jnp.where(lane == i2, w2, 0.0))


def _moe_ln_kernel(x_ref, wrh_ref, wrl_ref, br_ref, w1_ref, w3_ref, w2_ref, g_ref, b_ref, y_ref,
                   xb_sc, gate_sc, acc_sc, *, alpha):
    e = pl.program_id(1)

    @pl.when(e == 0)
    def _():
        x = x_ref[...]
        xh = x.astype(BF16)
        xl = (x - xh.astype(F32)).astype(BF16)
        xb_sc[...] = xh
        logits = (jnp.dot(xh, wrh_ref[...], preferred_element_type=F32)
                  + jnp.dot(xl, wrh_ref[...], preferred_element_type=F32)
                  + jnp.dot(xh, wrl_ref[...], preferred_element_type=F32) + br_ref[...])
        gate_sc[...] = _route(logits)
        acc_sc[...] = jnp.zeros_like(acc_sc)

    xb = xb_sc[...]
    h1 = jnp.dot(xb, w1_ref[0].astype(BF16), preferred_element_type=F32)
    h3 = jnp.dot(xb, w3_ref[0].astype(BF16), preferred_element_type=F32)
    hdn = (h1 * jax.nn.sigmoid(h1)) * h3
    y = jnp.dot(hdn.astype(BF16), w2_ref[0].astype(BF16), preferred_element_type=F32)
    lane = lax.broadcasted_iota(jnp.int32, (1, LANES), 1)
    ge = jnp.sum(jnp.where(lane == e + GATE_LANE0, gate_sc[...], 0.0), axis=1, keepdims=True)
    acc_sc[...] += ge * y

    @pl.when(e == pl.num_programs(1) - 1)
    def _():
        y_ref[...] = _layer_norm(alpha * x_ref[...] + acc_sc[...], g_ref[...], b_ref[...])


def _moe_ln(x, wrh, wrl, br, w1, w3, w2, g, b, alpha):
    n, d = x.shape
    tm = min(MOE_TILE, n)
    assert n % tm == 0
    ne = w1.shape[0]
    per_expert = lambda a: pl.BlockSpec((1,) + a.shape[1:], lambda i, e: (e, 0, 0))
    full = lambda a: pl.BlockSpec(a.shape, lambda i, e: (0, 0))
    return pl.pallas_call(
        functools.partial(_moe_ln_kernel, alpha=alpha),
        grid=(n // tm, ne),
        in_specs=[pl.BlockSpec((tm, d), lambda i, e: (i, 0)), full(wrh), full(wrl), full(br),
                  per_expert(w1), per_expert(w3), per_expert(w2),
                  full(g), full(b)],
        out_specs=pl.BlockSpec((tm, d), lambda i, e: (i, 0)),
        out_shape=jax.ShapeDtypeStruct((n, d), F32),
        scratch_shapes=[pltpu.VMEM((tm, d), BF16), pltpu.VMEM((tm, LANES), F32), pltpu.VMEM((tm, d), F32)],
        compiler_params=_cparams(("parallel", "arbitrary")),
        name="moe_ln",
    )(x, wrh, wrl, br, w1, w3, w2, g, b)


def _proj_c_kernel(x_ref, win_ref, gq_ref, gkv_ref, wuq_ref, wrot_ref, cq_ref, sq_ref,
                   ck_ref, s1k_ref, s2k_ref, q_ref, ckv_ref, kr_ref):
    xb = x_ref[0].astype(BF16)
    h = jnp.dot(xb, win_ref[...], preferred_element_type=F32)
    qa = h[:, :Q_RANK]
    kva = h[:, Q_RANK:Q_RANK + KV_RANK]
    krw = h[:, Q_RANK + KV_RANK:]
    qn = qa * lax.rsqrt(jnp.mean(qa * qa, axis=1, keepdims=True) + RMS_EPS) * gq_ref[...]
    ckv_ref[0] = kva * lax.rsqrt(jnp.mean(kva * kva, axis=1, keepdims=True) + RMS_EPS) * gkv_ref[...]
    half = ROPE_DIM // 2
    kr = _rope3(krw, ck_ref[...], s1k_ref[...], s2k_ref[...], LANES - half, half)
    kr_ref[0] = kr[:, :ROPE_DIM]
    qnb = qn.astype(BF16)
    q = jnp.dot(qnb, wuq_ref[...], preferred_element_type=F32)
    q_rot = jnp.dot(qnb, wrot_ref[...], preferred_element_type=F32)
    cq, sq = cq_ref[...], sq_ref[...]
    scale = (NOPE_DIM + ROPE_DIM) ** -0.5 * LOG2E
    for hd in range(H_C):
        sl = slice(hd * LANES, (hd + 1) * LANES)
        q_ref[0, :, sl] = ((q[:, sl] * cq + q_rot[:, sl] * sq) * scale).astype(BF16)


def _proj_c(x, win, gq, gkv, wuq, wrot, tabs_q, tabs_k):
    nb, t, _ = x.shape
    tm = min(ROW_TILE, t)
    assert t % tm == 0
    tok = lambda width: pl.BlockSpec((1, tm, width), lambda b, i: (b, i, 0))
    tab = pl.BlockSpec((tm, LANES), lambda b, i: (i, 0))
    full = lambda a: pl.BlockSpec(a.shape, lambda b, i: (0, 0))
    return pl.pallas_call(
        _proj_c_kernel,
        grid=(nb, t // tm),
        in_specs=[tok(D_MODEL), full(win), full(gq), full(gkv), full(wuq), full(wrot)] + [tab] * 5,
        out_specs=[tok(H_C * LANES), tok(KV_RANK), tok(ROPE_DIM)],
        out_shape=[jax.ShapeDtypeStruct((nb, t, H_C * LANES), BF16),
                   jax.ShapeDtypeStruct((nb, t, KV_RANK), F32),
                   jax.ShapeDtypeStruct((nb, t, ROPE_DIM), F32)],
        compiler_params=_cparams(("parallel", "parallel")),
        name="proj_c",
    )(x, win, gq, gkv, wuq, wrot, *tabs_q, *tabs_k)


def _kv_up_kernel(ckv_ref, kr_ref, wk_ref, place_ref, wvt_ref, k_ref, vt_ref):
    cb = ckv_ref[...].astype(BF16)
    k = (jnp.dot(cb, wk_ref[...], preferred_element_type=F32)
         + jnp.dot(kr_ref[...].astype(BF16), place_ref[...], preferred_element_type=F32))
    k_ref[0] = k.astype(BF16)
    vt = lax.dot_general(wvt_ref[...], cb, (((1,), (1,)), ((), ())), preferred_element_type=F32)
    vt_ref[0] = vt.astype(BF16)


def _kv_up(ckv, kr, wk, place, wvt):
    n = ckv.shape[0]
    tm = ATTN_BLOCK
    assert n % tm == 0
    row = lambda width: pl.BlockSpec((tm, width), lambda i: (i, 0))
    full = lambda a: pl.BlockSpec(a.shape, lambda i: (0, 0))
    return pl.pallas_call(
        _kv_up_kernel,
        grid=(n // tm,),
        in_specs=[row(KV_RANK), row(ROPE_DIM), full(wk), full(place), full(wvt)],
        out_specs=[pl.BlockSpec((1, tm, H_C * LANES), lambda i: (i, 0, 0)),
                   pl.BlockSpec((1, H_C * V_DIM_C, tm), lambda i: (i, 0, 0))],
        out_shape=[jax.ShapeDtypeStruct((n // tm, tm, H_C * LANES), BF16),
                   jax.ShapeDtypeStruct((n // tm, H_C * V_DIM_C, tm), BF16)],
        compiler_params=_cparams(("parallel",)),
        name="kv_up",
    )(ckv, kr, wk, place, wvt)


def _mla_decode_kernel(q_ref, ckv_ref, kr_ref, ckvn_ref, krn_ref, wabs_ref, wv_ref, o_ref):
    q = q_ref[0]
    ts = q.shape[0]
    qs = jnp.concatenate(
        [jnp.dot(q[:, h * LANES:(h + 1) * LANES], wabs_ref[h], preferred_element_type=F32).astype(BF16)
         for h in range(H_C)], axis=0)
    kc = jnp.concatenate([ckv_ref[0].astype(BF16), kr_ref[0].astype(BF16)], axis=1)
    kn = jnp.concatenate([ckvn_ref[0].astype(BF16), krn_ref[0].astype(BF16)], axis=1)
    nt = (((1,), (1,)), ((), ()))
    sc = lax.dot_general(qs, kc, nt, preferred_element_type=F32)
    sn = lax.dot_general(qs, kn, nt, preferred_element_type=F32)
    m = jnp.maximum(jnp.max(sc, axis=1, keepdims=True), jnp.max(sn, axis=1, keepdims=True))
    pc = jnp.exp2(sc - m)
    pn = jnp.exp2(sn - m)
    l = jnp.sum(pc, axis=1, keepdims=True) + jnp.sum(pn, axis=1, keepdims=True)
    ol = (jnp.dot(pc.astype(BF16), kc[:, :KV_RANK], preferred_element_type=F32)
          + jnp.dot(pn.astype(BF16), kn[:, :KV_RANK], preferred_element_type=F32)) / l
    olb = ol.astype(BF16)
    o = jnp.dot(olb[:ts], wv_ref[0], preferred_element_type=F32)
    for h in range(1, H_C):
        o = o + jnp.dot(olb[h * ts:(h + 1) * ts], wv_ref[h], preferred_element_type=F32)
    o_ref[0] = o.astype(o_ref.dtype)


def _mla_decode(q, ckv_c, kr_c, ckv_n, kr_n, w_abs, w_vout):
    nb, ts, _ = q.shape
    assert ckv_c.shape[1] % CHUNK == 0 and ts <= CHUNK
    per_b = lambda a: pl.BlockSpec((1,) + a.shape[1:], lambda b: (b, 0, 0))
    full = lambda a: pl.BlockSpec(a.shape, lambda b: (0, 0, 0))
    return pl.pallas_call(
        _mla_decode_kernel,
        grid=(nb,),
        in_specs=[per_b(q), per_b(ckv_c), per_b(kr_c), per_b(ckv_n), per_b(kr_n), full(w_abs), full(w_vout)],
        out_specs=pl.BlockSpec((1, ts, H_C * V_DIM_C), lambda b: (b, 0, 0)),
        out_shape=jax.ShapeDtypeStruct((nb, ts, H_C * V_DIM_C), BF16),
        compiler_params=_cparams(("parallel",)),
        name="mla_decode",
    )(q, ckv_c, kr_c, ckv_n, kr_n, w_abs, w_vout)


def _rope_tables(pos, dim, lane0):
    half = dim // 2
    inv = ROPE_THETA ** (-jnp.arange(0, dim, 2, dtype=F32) / dim)
    ang = pos.astype(F32)[:, None] * inv[None, :]
    cos, sin = jnp.cos(ang), jnp.sin(ang)
    zero = jnp.zeros_like(sin)
    c = jnp.concatenate([cos, cos], axis=1)
    s1 = jnp.concatenate([-sin, zero], axis=1)
    s2 = jnp.concatenate([zero, sin], axis=1)
    if lane0 < 0:
        reps = LANES // dim
        return tuple(jnp.tile(a, (1, reps)) for a in (c, s1, s2))
    t = pos.shape[0]
    pad = lambda a, fill: jnp.concatenate(
        [jnp.full((t, lane0), fill, F32), a, jnp.full((t, LANES - lane0 - dim), fill, F32)], axis=1)
    return pad(c, 1.0), pad(s1, 0.0), pad(s2, 0.0)


def _pad_cols(a, width):
    return jnp.pad(a, ((0, 0), (0, width - a.shape[1])))


def _blocks(a, tk):
    nb, t, l = a.shape
    return a.reshape(nb, t // tk, tk, l)


def _cat_pad_time(cache, new, t_pad):
    nb, t0, l = cache.shape
    t1 = new.shape[1]
    return jnp.concatenate([cache, new, jnp.zeros((nb, t_pad - t0 - t1, l), cache.dtype)], axis=1)


def kernel(x_prompt, x_sample, cache_fox_k, cache_fox_v, cache_fox_logf, cache_diff_k, cache_diff_v, cache_mla_ckv, cache_mla_krope, w_in_ab, b_fgate, diff_lq1, diff_lk1, diff_lq2, diff_lk2, diff_subln, w_out_ab, w_in_c, mla_q_norm, mla_kv_norm, mla_w_uq, mla_w_ukv, w_out_c, ln1_g, ln1_b, ln2_g, ln2_b, moe_wg, moe_bg, moe_we, moe_be, moe_w1, moe_w3, moe_w2):
    bp, tp, d = x_prompt.shape
    bs, ts, _ = x_sample.shape
    past = cache_fox_k.shape[2]
    depth = ln1_g.shape[0]
    alpha = (2 * depth) ** 0.25
    tk = ATTN_BLOCK
    assert past % tk == 0
    ns = bs * ts
    t_dec = past + tk

    pos_p = jnp.arange(tp)
    pos_s = jnp.tile(past + jnp.arange(ts), bs)

    xp = x_prompt
    xs = x_sample.reshape(1, ns, d)
    out_ab_p, out_ab_s, out_c_p, out_c_s = [], [], [], []

    for i in range(depth):
        j = i // 2
        if i % 2 == 0:
            lam_init = 0.8 - 0.6 * math.exp(-0.3 * i)
            cuts = [0, A_WIDTH, 2 * A_WIDTH, 3 * A_WIDTH, 3 * A_WIDTH + H_A,
                    3 * A_WIDTH + H_A + B_QK_WIDTH, 3 * A_WIDTH + H_A + 2 * B_QK_WIDTH,
                    3 * A_WIDTH + H_A + 2 * B_QK_WIDTH + B_V_WIDTH]
            w = w_in_ab[j]
            piece = lambda a: w[:, cuts[a]:cuts[a + 1]]
            w6 = jnp.stack([piece(0), piece(1), piece(2), piece(4), piece(5), piece(6)]).astype(BF16)
            wvt = jnp.stack([piece(2).T, piece(6).T]).astype(BF16)
            wf = _pad_cols(piece(3), LANES).astype(BF16)
            bf = _pad_cols(b_fgate[j][None, :], LANES)
            wout = w_out_ab[j].astype(BF16)
            diff_extra = (diff_lq1[j][None, :], diff_lk1[j][None, :], diff_lq2[j][None, :],
                          diff_lk2[j][None, :], diff_subln[j][None, :])

            tabs = _rope_tables(pos_p, HEAD_DIM, -1)
            (qa, ka, kab, va, vat, lf, lfw, qb, kb, kbb, vb, vbt) = _proj_ab(xp, w6, wvt, wf, bf, tabs)
            bias = _blocks(_decay_bias(lfw), tk)
            oa = _attention("fox", qa, _blocks(kab, tk), vat, (bias,), n_pairs=H_A // 2, mask_shift=0)
            ob = _attention("diff", qb, _blocks(kbb, tk), vbt, diff_extra,
                            n_pairs=H_B, mask_shift=int(math.log2(CHUNK)), lam_init=lam_init)
            out_ab_p.append((ka.reshape(bp, tp, H_A, HEAD_DIM), va.reshape(bp, tp, H_A, HEAD_DIM), lf,
                             kb.reshape(bp, tp, H_B, 2, HEAD_DIM), vb.reshape(bp, tp, H_B, 2 * HEAD_DIM)))
            xp2 = _outproj_ln(xp.reshape(bp * tp, d), [oa.reshape(bp * tp, -1), ob.reshape(bp * tp, -1)],
                              [wout[:A_WIDTH], wout[A_WIDTH:]], ln1_g[i][None, :], ln1_b[i][None, :], alpha)

            tabs = _rope_tables(pos_s, HEAD_DIM, -1)
            (qa, ka, kab, va, _, lf, lfw, qb, kb, kbb, vb, _) = _proj_ab(xs, w6, wvt, wf, bf, tabs)
            rs = lambda a: a.reshape(bs, ts, a.shape[-1])
            cache_lfw = jnp.pad(cache_fox_logf[j].astype(F32), ((0, 0), (0, 0), (0, LANES - H_A)))
            bias = _decay_bias(_cat_pad_time(cache_lfw, rs(lfw), t_dec))
            flat = lambda c: c.reshape(bs, past, -1)
            oa = _decode_attention("fox", rs(qa), flat(cache_fox_k[j]), flat(cache_fox_v[j]), rs(kab), rs(va),
                                   (bias,))
            ob = _decode_attention("diff", rs(qb), flat(cache_diff_k[j]), flat(cache_diff_v[j]), rs(kbb),
                                   rs(vb), diff_extra, lam_init=lam_init)
            out_ab_s.append((ka.reshape(bs, ts, H_A, HEAD_DIM), va.reshape(bs, ts, H_A, HEAD_DIM),
                             lf.reshape(bs, ts, H_A), kb.reshape(bs, ts, H_B, 2, HEAD_DIM),
                             vb.reshape(bs, ts, H_B, 2 * HEAD_DIM)))
            xs2 = _outproj_ln(xs.reshape(ns, d), [oa.reshape(ns, -1), ob.reshape(ns, -1)],
                              [wout[:A_WIDTH], wout[A_WIDTH:]], ln1_g[i][None, :], ln1_b[i][None, :], alpha)
        else:
            wc = w_in_c[j]
            kr_cols = _pad_cols(wc[:, Q_RANK + KV_RANK:], LANES)
            win = jnp.concatenate([wc[:, :Q_RANK + KV_RANK], kr_cols], axis=1).astype(BF16)
            wuq3 = jnp.pad(mla_w_uq[j].reshape(Q_RANK, H_C, NOPE_DIM + ROPE_DIM),
                           ((0, 0), (0, 0), (0, LANES - NOPE_DIM - ROPE_DIM)))
            wuq = wuq3.reshape(Q_RANK, H_C * LANES).astype(BF16)
            r0, r1, r2 = NOPE_DIM, NOPE_DIM + ROPE_DIM // 2, NOPE_DIM + ROPE_DIM
            wrot = jnp.zeros_like(wuq3).at[:, :, r0:r1].set(-wuq3[:, :, r1:r2]).at[:, :, r1:r2].set(wuq3[:, :, r0:r1])
            wrot = wrot.reshape(Q_RANK, H_C * LANES).astype(BF16)
            q_tabs = lambda pos: (lambda c, s1, s2: (c, s2 - s1))(*_rope_tables(pos, ROPE_DIM, NOPE_DIM))
            wukv = mla_w_ukv[j].reshape(KV_RANK, H_C, NOPE_DIM + V_DIM_C)
            wk = jnp.pad(wukv[:, :, :NOPE_DIM], ((0, 0), (0, 0), (0, LANES - NOPE_DIM)))
            wk = wk.reshape(KV_RANK, H_C * LANES).astype(BF16)
            wvt = wukv[:, :, NOPE_DIM:].reshape(KV_RANK, H_C * V_DIM_C).T.astype(BF16)
            place = jnp.tile(_pad_cols(jnp.concatenate(
                [jnp.zeros((ROPE_DIM, NOPE_DIM), F32), jnp.eye(ROPE_DIM, dtype=F32)], axis=1), LANES),
                (1, H_C)).astype(BF16)
            gq = mla_q_norm[j][None, :]
            gkv = mla_kv_norm[j][None, :]
            wout = w_out_c[j].astype(BF16)

            q, ckv, kr = _proj_c(xp, win, gq, gkv, wuq, wrot, q_tabs(pos_p), _rope_tables(pos_p, ROPE_DIM, 0))
            kc, vct = _kv_up(ckv.reshape(bp * tp, KV_RANK), kr.reshape(bp * tp, ROPE_DIM), wk, place, wvt)
            per_seq = lambda a, nb: a.reshape((nb, a.shape[0] // nb) + a.shape[1:])
            oc = _attention("mla", q, per_seq(kc, bp), per_seq(vct, bp), (), n_pairs=H_C // 2,
                            mask_shift=int(math.log2(CHUNK)))
            out_c_p.append((ckv, kr))
            xp2 = _outproj_ln(xp.reshape(bp * tp, d), [oc.reshape(bp * tp, -1)], [wout],
                              ln1_g[i][None, :], ln1_b[i][None, :], alpha)

            q, ckv, kr = _proj_c(xs, win, gq, gkv, wuq, wrot, q_tabs(pos_s), _rope_tables(pos_s, ROPE_DIM, 0))
            w_abs = jnp.zeros((H_C, LANES, 2 * LANES), F32)
            w_abs = w_abs.at[:, :NOPE_DIM, :KV_RANK].set(jnp.transpose(wukv[:, :, :NOPE_DIM], (1, 2, 0)))
            w_abs = w_abs.at[:, NOPE_DIM:NOPE_DIM + ROPE_DIM, KV_RANK:KV_RANK + ROPE_DIM].set(
                jnp.eye(ROPE_DIM, dtype=F32))
            w_vout = jnp.einsum("khd,hg->hkgd", wukv[:, :, NOPE_DIM:], jnp.eye(H_C, dtype=F32))
            w_vout = w_vout.reshape(H_C, KV_RANK, H_C * V_DIM_C)
            wide = lambda a: jnp.pad(a.astype(F32), ((0, 0), (0, 0), (0, LANES - ROPE_DIM)))
            oc = _mla_decode(q.reshape(bs, ts, -1), cache_mla_ckv[j].astype(F32), wide(cache_mla_krope[j]),
                             ckv.reshape(bs, ts, KV_RANK), wide(kr.reshape(bs, ts, ROPE_DIM)),
                             w_abs.astype(BF16), w_vout.astype(BF16))
            out_c_s.append((ckv.reshape(bs, ts, KV_RANK), kr.reshape(bs, ts, ROPE_DIM)))
            xs2 = _outproj_ln(xs.reshape(ns, d), [oc.reshape(ns, -1)], [wout],
                              ln1_g[i][None, :], ln1_b[i][None, :], alpha)

        wr = _pad_cols(jnp.concatenate(
            [moe_wg[i]] + [moe_we[i][gi] for gi in range(N_GROUPS)], axis=1), LANES)
        wrh = wr.astype(BF16)
        wrl = (wr - wrh.astype(F32)).astype(BF16)
        br = _pad_cols(jnp.concatenate([moe_bg[i], moe_be[i].reshape(-1)])[None, :], LANES)
        moe_w = (moe_w1[i], moe_w3[i], moe_w2[i])
        g2, b2 = ln2_g[i][None, :], ln2_b[i][None, :]
        xp = _moe_ln(xp2, wrh, wrl, br, *moe_w, g2, b2, alpha).reshape(bp, tp, d)
        xs = _moe_ln(xs2, wrh, wrl, br, *moe_w, g2, b2, alpha).reshape(1, ns, d)

    stack = lambda rows, n: jnp.stack([r[n] for r in rows])
    return (xp, xs.reshape(bs, ts, d),
            stack(out_ab_p, 0), stack(out_ab_p, 1), stack(out_ab_p, 2), stack(out_ab_p, 3), stack(out_ab_p, 4),
            stack(out_c_p, 0), stack(out_c_p, 1),
            stack(out_ab_s, 0), stack(out_ab_s, 1), stack(out_ab_s, 2), stack(out_ab_s, 3), stack(out_ab_s, 4),
            stack(out_c_s, 0), stack(out_c_s, 1))
```

```python
import functools
import math

import jax
import jax.numpy as jnp
from jax import lax
from jax.experimental import pallas as pl
from jax.experimental.pallas import tpu as pltpu

F32 = jnp.float32
BF16 = jnp.bfloat16

D_MODEL = 1024
CHUNK = 64
HEAD_DIM = 64
ROPE_THETA = 10000.0
H_A = 8
H_B = 4
H_C = 16
Q_RANK = 256
KV_RANK = 128
NOPE_DIM = 64
ROPE_DIM = 32
V_DIM_C = 64
N_GROUPS = 4
EXPERTS_PER_GROUP = 4
N_EXPERTS = N_GROUPS * EXPERTS_PER_GROUP
D_EXPERT = 256
A_WIDTH = H_A * HEAD_DIM
B_QK_WIDTH = H_B * 2 * HEAD_DIM
B_V_WIDTH = H_B * 2 * HEAD_DIM
FGATE_BIAS = 3.0
LN_EPS = 1e-5
RMS_EPS = 1e-6
NEG_INF = -1e30
LOG2E = math.log2(math.e)

LANES = 128
BF16_ROWS = 16
PANEL = {"fox": 2048, "diff": 256, "mla": 2048}
BIAS_PIECES = 3
VMEM_LIMIT = 56 * 1024 * 1024
ATTN_BLOCK = 512
Q_KEY_BLOCKS = 4
ROW_TILE = 512
MOE_TILE = 1024
BIAS_ROWS_PER_STEP = 4096
GATE_LANE0 = N_GROUPS


def _cparams(sem):
    return pltpu.CompilerParams(dimension_semantics=sem, vmem_limit_bytes=VMEM_LIMIT)


def _rope3(x, c, s1, s2, shift_up, shift_down):
    return x * c + pltpu.roll(x, shift_up, 1) * s1 + pltpu.roll(x, shift_down, 1) * s2


def _layer_norm(y, g, b):
    mu = jnp.mean(y, axis=-1, keepdims=True)
    d = y - mu
    var = jnp.mean(d * d, axis=-1, keepdims=True)
    return d * lax.rsqrt(var + LN_EPS) * g + b


def _split3(x):
    hi = x.astype(BF16)
    r1 = x - hi.astype(F32)
    mid = r1.astype(BF16)
    return hi, mid, (r1 - mid.astype(F32)).astype(BF16)


def _proj_ab_kernel(x_ref, w_ref, wvt_ref, wf_ref, bf_ref, c_ref, s1_ref, s2_ref,
                    qa_ref, ka_ref, kab_ref, va_ref, vat_ref, lf_ref, lfw_ref,
                    qb_ref, kb_ref, kbb_ref, vb_ref, vbt_ref):
    xb = x_ref[0].astype(BF16)

    def mm(i):
        return jnp.dot(xb, w_ref[i], preferred_element_type=F32)

    def mm_t(i):
        return lax.dot_general(wvt_ref[i], xb, (((1,), (1,)), ((), ())), preferred_element_type=F32)

    qa_ref[0] = (mm(0) * (HEAD_DIM ** -0.5 * LOG2E)).astype(BF16)
    ka = mm(1)
    ka_ref[0] = ka
    kab_ref[0] = ka.astype(BF16)
    va_ref[0] = mm(2)
    vat_ref[0, 0] = mm_t(0).astype(BF16)

    z = jnp.dot(xb, wf_ref[...], preferred_element_type=F32) + bf_ref[...]
    lf = jnp.minimum(z, 0.0) - jnp.log1p(jnp.exp(-jnp.abs(z)))
    lf_ref[0] = lf[:, :H_A]
    lfw_ref[0] = lf

    c, s1, s2 = c_ref[...], s1_ref[...], s2_ref[...]
    qb = mm(3)
    kb = mm(4)
    for s in range(B_QK_WIDTH // LANES):
        sl = slice(s * LANES, (s + 1) * LANES)
        qs = _rope3(qb[:, sl], c, s1, s2, LANES - HEAD_DIM // 2, HEAD_DIM // 2)
        qb_ref[0, :, sl] = (qs * (HEAD_DIM ** -0.5 * LOG2E)).astype(BF16)
        ks = _rope3(kb[:, sl], c, s1, s2, LANES - HEAD_DIM // 2, HEAD_DIM // 2)
        kb_ref[0, :, sl] = ks
        kbb_ref[0, :, sl] = ks.astype(BF16)
    vb_ref[0] = mm(5)
    vbt_ref[0, 0] = mm_t(1).astype(BF16)


def _proj_ab(x, w6, wvt, wf, bf, tabs):
    nb, t, _ = x.shape
    tm = min(ROW_TILE, t)
    assert t % tm == 0
    w = A_WIDTH
    tok = lambda width: pl.BlockSpec((1, tm, width), lambda b, i: (b, i, 0))
    tr = pl.BlockSpec((1, 1, w, tm), lambda b, i: (b, i, 0, 0))
    tab = pl.BlockSpec((tm, LANES), lambda b, i: (i, 0))
    full = lambda a: pl.BlockSpec(a.shape, lambda b, i: (0,) * a.ndim)
    sds = lambda width, dt: jax.ShapeDtypeStruct((nb, t, width), dt)
    sds_t = jax.ShapeDtypeStruct((nb, t // tm, w, tm), BF16)
    return pl.pallas_call(
        _proj_ab_kernel,
        grid=(nb, t // tm),
        in_specs=[tok(D_MODEL), full(w6), full(wvt), full(wf), full(bf), tab, tab, tab],
        out_specs=[tok(w), tok(w), tok(w), tok(w), tr, tok(H_A), tok(LANES), tok(w), tok(w), tok(w), tok(w), tr],
        out_shape=[sds(w, BF16), sds(w, F32), sds(w, BF16), sds(w, F32), sds_t, sds(H_A, F32), sds(LANES, F32),
                   sds(w, BF16), sds(w, F32), sds(w, BF16), sds(w, F32), sds_t],
        compiler_params=_cparams(("parallel", "parallel")),
        name="proj_ab",
    )(x, w6, wvt, wf, bf, *tabs)


def _decay_bias_kernel(lf_ref, spread_ref, lower_ref, o_ref, carry_ref):
    @pl.when(pl.program_id(1) == 0)
    def _():
        carry_ref[...] = jnp.zeros_like(carry_ref)

    spread = spread_ref[...]
    lower = lower_ref[...]
    tc = lower.shape[0]
    lane = lax.broadcasted_iota(jnp.int32, (1, LANES), 1).astype(F32)
    piece = lane - BIAS_PIECES * jnp.floor((lane + 0.5) * (1.0 / BIAS_PIECES))
    carry = carry_ref[...]
    for r in range(lf_ref.shape[1] // tc):
        rows = slice(r * tc, (r + 1) * tc)
        x = lf_ref[0, rows, :]
        xr = sum(jnp.dot(p, spread, preferred_element_type=F32) for p in _split3(x))
        c = sum(jnp.dot(lower, p, preferred_element_type=F32) for p in _split3(xr)) + carry
        carry = c[tc - 1:tc, :]
        hi, mid, lo = (p.astype(F32) for p in _split3(c * (-LOG2E)))
        o_ref[0, rows, :] = jnp.where(piece == 0.0, hi, jnp.where(piece == 1.0, mid, lo)).astype(BF16)
    carry_ref[...] = carry


def _decay_bias(lf_wide):
    nb, t, _ = lf_wide.shape
    tc = min(ATTN_BLOCK, t)
    tb = min(BIAS_ROWS_PER_STEP, t)
    assert t % tb == 0 and tb % tc == 0
    spec = pl.BlockSpec((1, tb, LANES), lambda b, i: (b, i, 0))
    src = jnp.arange(LANES)[:, None]
    dst = jnp.arange(LANES)[None, :]
    spread = ((dst // BIAS_PIECES == src) & (src < H_A)).astype(BF16)
    lower = jnp.tril(jnp.ones((tc, tc), BF16))
    const = lambda a: pl.BlockSpec(a.shape, lambda b, i: (0, 0))
    return pl.pallas_call(
        _decay_bias_kernel,
        grid=(nb, t // tb),
        in_specs=[spec, const(spread), const(lower)],
        out_specs=spec,
        out_shape=jax.ShapeDtypeStruct((nb, t, LANES), BF16),
        scratch_shapes=[pltpu.VMEM((1, LANES), F32)],
        compiler_params=_cparams(("parallel", "arbitrary")),
        name="cumsum",
    )(lf_wide, spread, lower)


def _attn_kernel(*refs, mode, tq, tk, mask_shift, lam_init):
    if mode == "diff":
        q_ref, k_ref, vt_ref, lq1_ref, lk1_ref, lq2_ref, lk2_ref, sub_ref, o_ref = refs[:9]
    elif mode == "fox":
        q_ref, k_ref, vt_ref, b_ref, o_ref = refs[:5]
    else:
        q_ref, k_ref, vt_ref, o_ref = refs[:4]
    m_sc, acc_sc, sa, bma, sb, bmb = refs[-6:]
    v_rows = LANES if mode == "diff" else HEAD_DIM
    sa_sc, sb_sc = (sa, bma), (sb, bmb)

    qi = pl.program_id(2)
    q = q_ref[0]
    lane = lax.broadcasted_iota(jnp.int32, (1, LANES), 1)
    if mode == "mla":
        qs = [q[:, :LANES], q[:, LANES:]]
    else:
        zero = jnp.zeros_like(q)
        qs = [jnp.where(lane < HEAD_DIM, q, zero), jnp.where(lane >= HEAD_DIM, q, zero)]
        if mode == "fox":
            def pick(i):
                lo = BIAS_PIECES * (2 * pl.program_id(1) + i)
                hot = jnp.where((lane >= lo) & (lane < lo + BIAS_PIECES), 1.0, 0.0)
                return jnp.broadcast_to(hot, (tq, LANES)).astype(BF16)

            qs = [jnp.concatenate([qs[i], pick(i)], axis=1) for i in range(2)]

    m_sc[...] = jnp.full(m_sc.shape, NEG_INF, F32)
    acc_sc[...] = jnp.zeros(acc_sc.shape, F32)

    pw = min(PANEL[mode], tq)

    def scores(j, bufs, q0, q1):
        s_sc, bm_sc = bufs
        cs = slice(q0, q1)
        k = k_ref[0, j]
        if mode == "fox":
            k = jnp.concatenate([k, b_ref[0, j]], axis=1)
        for i in range(2):
            ki = k[:, i * LANES:(i + 1) * LANES] if mode == "mla" else k
            st = lax.dot_general(ki, qs[i][cs], (((1,), (1,)), ((), ())), preferred_element_type=F32)
            s_sc[i, :, cs] = st
            bm_sc[i, :, cs] = jnp.max(st, axis=0, keepdims=True)

    def consume(j, bufs, q0, q1, key0=None):
        s_sc, bm_sc = bufs
        cs = slice(q0, q1)
        vt = vt_ref[0, j]
        masked = key0 is not None and ((key0 + tk - 1) >> mask_shift) > (q0 >> mask_shift)
        for i in range(2):
            st = s_sc[i, :, cs]
            if masked:
                key = lax.broadcasted_iota(jnp.int32, (tk, q1 - q0), 0) + key0
                qry = lax.broadcasted_iota(jnp.int32, (tk, q1 - q0), 1) + q0
                vis = lax.shift_right_logical(key, mask_shift) <= lax.shift_right_logical(qry, mask_shift)
                st = jnp.where(vis, st, NEG_INF)
                blk_max = jnp.max(st, axis=0, keepdims=True)
            else:
                blk_max = bm_sc[i, :, cs]
            m_prev = m_sc[i, :, cs]
            m_new = jnp.maximum(m_prev, blk_max)
            alpha = jnp.exp2(m_prev - m_new)
            p = jnp.exp2(st - m_new).astype(BF16)
            vi = vt if mode == "diff" else vt[i * HEAD_DIM:(i + 1) * HEAD_DIM]
            vi = jnp.concatenate([vi, jnp.ones((BF16_ROWS, tk), BF16)], axis=0)
            acc_sc[i, :, cs] = alpha * acc_sc[i, :, cs] + jnp.dot(vi, p, preferred_element_type=F32)
            m_sc[i, :, cs] = m_new

    def stage(nxt, cur):
        for q0 in range(0, tq, pw):
            scores(nxt[0], nxt[1], q0, q0 + pw)
            consume(cur[0], cur[1], q0, q0 + pw)

    n_diag = tq // tk
    nfull = n_diag * qi
    bufs = (sa_sc, sb_sc)

    def trip(jj, carry):
        for d in range(n_diag):
            j = n_diag * jj + d
            stage((j + 1, bufs[(d + 1) % 2]), (j, bufs[d % 2]))
        return carry

    for q0 in range(0, tq, pw):
        scores(0, sa_sc, q0, q0 + pw)
    lax.fori_loop(0, qi, trip, 0)

    pt = min(pw, tk)
    sees = lambda d, q1: ((d * tk) >> mask_shift) <= ((q1 - 1) >> mask_shift)
    for d in range(n_diag):
        for q0 in range(0, tq, pt):
            if d + 1 < n_diag and sees(d + 1, q0 + pt):
                scores(nfull + d + 1, bufs[(d + 1) % 2], q0, q0 + pt)
            if sees(d, q0 + pt):
                consume(nfull + d, bufs[d % 2], q0, q0 + pt, key0=d * tk)

    if mode == "diff":
        lam = (jnp.exp(jnp.sum(lq1_ref[...] * lk1_ref[...], axis=1, keepdims=True))
               - jnp.exp(jnp.sum(lq2_ref[...] * lk2_ref[...], axis=1, keepdims=True)) + lam_init)
    for q0 in range(0, tq, pw):
        cs = slice(q0, q0 + pw)
        o0 = acc_sc[0, :v_rows, cs] / acc_sc[0, v_rows:v_rows + 1, cs]
        o1 = acc_sc[1, :v_rows, cs] / acc_sc[1, v_rows:v_rows + 1, cs]
        if mode == "diff":
            o = o0 - lam * o1
            ms = jnp.mean(o * o, axis=0, keepdims=True)
            o = (o * lax.rsqrt(ms + RMS_EPS)).T * sub_ref[...] * (1.0 - lam_init)
        else:
            o = jnp.concatenate([o0, o1], axis=0).T
        o_ref[0, cs, :] = o.astype(o_ref.dtype)


def _attention(mode, q, k, vt, extra, *, n_pairs, mask_shift, lam_init=0.0):
    nb, t_q, _ = q.shape
    _, nkb, tk, _ = k.shape
    tq = Q_KEY_BLOCKS * tk
    assert t_q % tq == 0 and nkb == t_q // tk
    qw = 2 * LANES if mode == "mla" else LANES
    in_specs = [
        pl.BlockSpec((1, tq, qw), lambda b, p, i: (b, i, p)),
        pl.BlockSpec((1, nkb, tk, qw), lambda b, p, i: (b, 0, 0, p)),
        pl.BlockSpec((1, nkb, LANES, tk), lambda b, p, i: (b, 0, p, 0)),
    ]
    if mode == "fox":
        in_specs.append(pl.BlockSpec((1, nkb, tk, LANES), lambda b, p, i: (b, 0, 0, 0)))
    elif mode == "diff":
        in_specs += [pl.BlockSpec(a.shape, lambda b, p, i: (0, 0)) for a in extra]
    kern = functools.partial(_attn_kernel, mode=mode, tq=tq, tk=tk, mask_shift=mask_shift, lam_init=lam_init)
    return pl.pallas_call(
        kern,
        grid=(nb, n_pairs, t_q // tq),
        in_specs=in_specs,
        out_specs=pl.BlockSpec((1, tq, LANES), lambda b, p, i: (b, i, p)),
        out_shape=jax.ShapeDtypeStruct((nb, t_q, n_pairs * LANES), BF16),
        scratch_shapes=[pltpu.VMEM((2, 1, tq), F32),
                        pltpu.VMEM((2, (LANES if mode == "diff" else HEAD_DIM) + BF16_ROWS, tq), F32),
                        pltpu.VMEM((2, tk, tq), F32), pltpu.VMEM((2, 1, tq), F32),
                        pltpu.VMEM((2, tk, tq), F32), pltpu.VMEM((2, 1, tq), F32)],
        compiler_params=_cparams(("parallel", "parallel", "arbitrary")),
        name="attn_" + mode,
    )(q, k, vt, *extra)


def _decode_attn_kernel(*refs, mode, lam_init):
    if mode == "fox":
        q_ref, kc_ref, vc_ref, kn_ref, vn_ref, b_ref, o_ref = refs
    else:
        q_ref, kc_ref, vc_ref, kn_ref, vn_ref, lq1_ref, lk1_ref, lq2_ref, lk2_ref, sub_ref, o_ref = refs
        lam = (jnp.exp(jnp.sum(lq1_ref[...] * lk1_ref[...], axis=1, keepdims=True))
               - jnp.exp(jnp.sum(lq2_ref[...] * lk2_ref[...], axis=1, keepdims=True)) + lam_init)
    ts = q_ref.shape[1]
    past = kc_ref.shape[1]
    lane = lax.broadcasted_iota(jnp.int32, (1, LANES), 1)
    row = lax.broadcasted_iota(jnp.int32, (ts, ts), 0)
    col = lax.broadcasted_iota(jnp.int32, (ts, ts), 1)
    nt = (((1,), (1,)), ((), ()))
    for p in range(q_ref.shape[2] // LANES):
        sl = slice(p * LANES, (p + 1) * LANES)
        q = q_ref[0, :, sl]
        kc = kc_ref[0, :, sl].astype(BF16)
        kn = kn_ref[0, :, sl]
        vc = vc_ref[0, :, sl].astype(BF16)
        vn = vn_ref[0, :, sl].astype(BF16)
        if mode == "fox":
            kc = jnp.concatenate([kc, b_ref[0, :past, :]], axis=1)
            kn = jnp.concatenate([kn, b_ref[0, past:past + ts, :]], axis=1)
        zero = jnp.zeros_like(q)
        outs = []
        for i in range(2):
            qi = jnp.where(lane < HEAD_DIM, q, zero) if i == 0 else jnp.where(lane >= HEAD_DIM, q, zero)
            if mode == "fox":
                lo = BIAS_PIECES * (2 * p + i)
                hot = jnp.where((lane >= lo) & (lane < lo + BIAS_PIECES), 1.0, 0.0)
                qi = jnp.concatenate([qi, jnp.broadcast_to(hot, (ts, LANES)).astype(BF16)], axis=1)
            sc = lax.dot_general(qi, kc, nt, preferred_element_type=F32)
            sn = lax.dot_general(qi, kn, nt, preferred_element_type=F32)
            if mode == "fox":
                sn = jnp.where(col <= row, sn, NEG_INF)
            m = jnp.maximum(jnp.max(sc, axis=1, keepdims=True), jnp.max(sn, axis=1, keepdims=True))
            pc = jnp.exp2(sc - m)
            pn = jnp.exp2(sn - m)
            l = jnp.sum(pc, axis=1, keepdims=True) + jnp.sum(pn, axis=1, keepdims=True)
            outs.append((jnp.dot(pc.astype(BF16), vc, preferred_element_type=F32)
                         + jnp.dot(pn.astype(BF16), vn, preferred_element_type=F32)) / l)
        if mode == "fox":
            o = jnp.where(lane < HEAD_DIM, outs[0], outs[1])
        else:
            o = outs[0] - lam * outs[1]
            ms = jnp.mean(o * o, axis=1, keepdims=True)
            o = o * lax.rsqrt(ms + RMS_EPS) * sub_ref[...] * (1.0 - lam_init)
        o_ref[0, :, sl] = o.astype(o_ref.dtype)


def _decode_attention(mode, q, k_cache, v_cache, k_new, v_new, extra, lam_init=0.0):
    nb, ts, w = q.shape
    assert k_cache.shape[1] % CHUNK == 0 and ts <= CHUNK
    per_b = lambda a: pl.BlockSpec((1,) + a.shape[1:], lambda b: (b, 0, 0))
    if mode == "fox":
        extra_specs = [per_b(extra[0])]
    else:
        extra_specs = [pl.BlockSpec(a.shape, lambda b: (0, 0)) for a in extra]
    return pl.pallas_call(
        functools.partial(_decode_attn_kernel, mode=mode, lam_init=lam_init),
        grid=(nb,),
        in_specs=[per_b(q), per_b(k_cache), per_b(v_cache), per_b(k_new), per_b(v_new)] + extra_specs,
        out_specs=pl.BlockSpec((1, ts, w), lambda b: (b, 0, 0)),
        out_shape=jax.ShapeDtypeStruct((nb, ts, w), BF16),
        compiler_params=_cparams(("parallel",)),
        name="decode_" + mode,
    )(q, k_cache, v_cache, k_new, v_new, *extra)


def _outproj_ln_kernel(*refs, n_in, alpha):
    x_ref = refs[0]
    o_refs = refs[1:1 + n_in]
    w_refs = refs[1 + n_in:1 + 2 * n_in]
    g_ref, b_ref, y_ref = refs[1 + 2 * n_in:]
    mix = jnp.dot(o_refs[0][...], w_refs[0][...], preferred_element_type=F32)
    for o_r, w_r in zip(o_refs[1:], w_refs[1:]):
        mix = mix + jnp.dot(o_r[...], w_r[...], preferred_element_type=F32)
    y_ref[...] = _layer_norm(alpha * x_ref[...] + mix, g_ref[...], b_ref[...])


def _outproj_ln(x, outs, ws, g, b, alpha):
    n, d = x.shape
    tm = min(ROW_TILE, n)
    assert n % tm == 0
    row = lambda width: pl.BlockSpec((tm, width), lambda i: (i, 0))
    full = lambda a: pl.BlockSpec(a.shape, lambda i: (0, 0))
    return pl.pallas_call(
        functools.partial(_outproj_ln_kernel, n_in=len(outs), alpha=alpha),
        grid=(n // tm,),
        in_specs=[row(d)] + [row(o.shape[1]) for o in outs] + [full(w) for w in ws] + [full(g), full(b)],
        out_specs=row(d),
        out_shape=jax.ShapeDtypeStruct((n, d), F32),
        compiler_params=_cparams(("parallel",)),
        name="outproj_ln",
    )(x, *outs, *ws, g, b)


def _route(logits):
    lane = lax.broadcasted_iota(jnp.int32, logits.shape, 1).astype(F32)
    big = float(1 << 20)
    is_g = lane < N_GROUPS
    lg = jnp.where(is_g, logits, NEG_INF)
    eg = jnp.where(is_g, jnp.exp(lg - jnp.max(lg, axis=1, keepdims=True)), 0.0)
    pg = eg / jnp.sum(eg, axis=1, keepdims=True)
    p_g = jnp.max(pg, axis=1, keepdims=True)
    gidx = jnp.min(jnp.where(is_g & (pg == p_g), lane, big), axis=1, keepdims=True)
    lo = GATE_LANE0 + EXPERTS_PER_GROUP * gidx
    sel = (lane >= lo) & (lane < lo + EXPERTS_PER_GROUP)
    le = jnp.where(sel, logits, NEG_INF)
    ee = jnp.where(sel, jnp.exp(le - jnp.max(le, axis=1, keepdims=True)), 0.0)
    pe = ee / jnp.sum(ee, axis=1, keepdims=True)
    v1 = jnp.max(jnp.where(sel, pe, -1.0), axis=1, keepdims=True)
    i1 = jnp.min(jnp.where(sel & (pe == v1), lane, big), axis=1, keepdims=True)
    rest = sel & (lane != i1)
    v2 = jnp.max(jnp.where(rest, pe, -1.0), axis=1, keepdims=True)
    i2 = jnp.min(jnp.where(rest & (pe == v2), lane, big), axis=1, keepdims=True)
    tot = v1 + v2
    w1 = v1 / tot * p_g
    w2 = v2 / tot * p_g
    return jnp.where(lane == i1, w1, jnp.where(lane == i2, w2, 0.0))


def _moe_ln_kernel(x_ref, wrh_ref, wrl_ref, br_ref, w1_ref, w3_ref, w2_ref, g_ref, b_ref, y_ref,
                   xb_sc, gate_sc, acc_sc, *, alpha):
    e = pl.program_id(1)

    @pl.when(e == 0)
    def _():
        x = x_ref[...]
        xh = x.astype(BF16)
        xl = (x - xh.astype(F32)).astype(BF16)
        xb_sc[...] = xh
        logits = (jnp.dot(xh, wrh_ref[...], preferred_element_type=F32)
                  + jnp.dot(xl, wrh_ref[...], preferred_element_type=F32)
                  + jnp.dot(xh, wrl_ref[...], preferred_element_type=F32) + br_ref[...])
        gate_sc[...] = _route(logits)
        acc_sc[...] = jnp.zeros_like(acc_sc)

    xb = xb_sc[...]
    h1 = jnp.dot(xb, w1_ref[0].astype(BF16), preferred_element_type=F32)
    h3 = jnp.dot(xb, w3_ref[0].astype(BF16), preferred_element_type=F32)
    hdn = (h1 * jax.nn.sigmoid(h1)) * h3
    y = jnp.dot(hdn.astype(BF16), w2_ref[0].astype(BF16), preferred_element_type=F32)
    lane = lax.broadcasted_iota(jnp.int32, (1, LANES), 1)
    ge = jnp.sum(jnp.where(lane == e + GATE_LANE0, gate_sc[...], 0.0), axis=1, keepdims=True)
    acc_sc[...] += ge * y

    @pl.when(e == pl.num_programs(1) - 1)
    def _():
        y_ref[...] = _layer_norm(alpha * x_ref[...] + acc_sc[...], g_ref[...], b_ref[...])


def _moe_ln(x, wrh, wrl, br, w1, w3, w2, g, b, alpha):
    n, d = x.shape
    tm = min(MOE_TILE, n)
    assert n % tm == 0
    ne = w1.shape[0]
    per_expert = lambda a: pl.BlockSpec((1,) + a.shape[1:], lambda i, e: (e, 0, 0))
    full = lambda a: pl.BlockSpec(a.shape, lambda i, e: (0, 0))
    return pl.pallas_call(
        functools.partial(_moe_ln_kernel, alpha=alpha),
        grid=(n // tm, ne),
        in_specs=[pl.BlockSpec((tm, d), lambda i, e: (i, 0)), full(wrh), full(wrl), full(br),
                  per_expert(w1), per_expert(w3), per_expert(w2),
                  full(g), full(b)],
        out_specs=pl.BlockSpec((tm, d), lambda i, e: (i, 0)),
        out_shape=jax.ShapeDtypeStruct((n, d), F32),
        scratch_shapes=[pltpu.VMEM((tm, d), BF16), pltpu.VMEM((tm, LANES), F32), pltpu.VMEM((tm, d), F32)],
        compiler_params=_cparams(("parallel", "arbitrary")),
        name="moe_ln",
    )(x, wrh, wrl, br, w1, w3, w2, g, b)


def _proj_c_kernel(x_ref, win_ref, gq_ref, gkv_ref, wuq_ref, wrot_ref, cq_ref, sq_ref,
                   ck_ref, s1k_ref, s2k_ref, q_ref, ckv_ref, kr_ref):
    xb = x_ref[0].astype(BF16)
    h = jnp.dot(xb, win_ref[...], preferred_element_type=F32)
    qa = h[:, :Q_RANK]
    kva = h[:, Q_RANK:Q_RANK + KV_RANK]
    krw = h[:, Q_RANK + KV_RANK:]
    qn = qa * lax.rsqrt(jnp.mean(qa * qa, axis=1, keepdims=True) + RMS_EPS) * gq_ref[...]
    ckv_ref[0] = kva * lax.rsqrt(jnp.mean(kva * kva, axis=1, keepdims=True) + RMS_EPS) * gkv_ref[...]
    half = ROPE_DIM // 2
    kr = _rope3(krw, ck_ref[...], s1k_ref[...], s2k_ref[...], LANES - half, half)
    kr_ref[0] = kr[:, :ROPE_DIM]
    qnb = qn.astype(BF16)
    q = jnp.dot(qnb, wuq_ref[...], preferred_element_type=F32)
    q_rot = jnp.dot(qnb, wrot_ref[...], preferred_element_type=F32)
    cq, sq = cq_ref[...], sq_ref[...]
    scale = (NOPE_DIM + ROPE_DIM) ** -0.5 * LOG2E
    for hd in range(H_C):
        sl = slice(hd * LANES, (hd + 1) * LANES)
        q_ref[0, :, sl] = ((q[:, sl] * cq + q_rot[:, sl] * sq) * scale).astype(BF16)


def _proj_c(x, win, gq, gkv, wuq, wrot, tabs_q, tabs_k):
    nb, t, _ = x.shape
    tm = min(ROW_TILE, t)
    assert t % tm == 0
    tok = lambda width: pl.BlockSpec((1, tm, width), lambda b, i: (b, i, 0))
    tab = pl.BlockSpec((tm, LANES), lambda b, i: (i, 0))
    full = lambda a: pl.BlockSpec(a.shape, lambda b, i: (0, 0))
    return pl.pallas_call(
        _proj_c_kernel,
        grid=(nb, t // tm),
        in_specs=[tok(D_MODEL), full(win), full(gq), full(gkv), full(wuq), full(wrot)] + [tab] * 5,
        out_specs=[tok(H_C * LANES), tok(KV_RANK), tok(ROPE_DIM)],
        out_shape=[jax.ShapeDtypeStruct((nb, t, H_C * LANES), BF16),
                   jax.ShapeDtypeStruct((nb, t, KV_RANK), F32),
                   jax.ShapeDtypeStruct((nb, t, ROPE_DIM), F32)],
        compiler_params=_cparams(("parallel", "parallel")),
        name="proj_c",
    )(x, win, gq, gkv, wuq, wrot, *tabs_q, *tabs_k)


def _kv_up_kernel(ckv_ref, kr_ref, wk_ref, place_ref, wvt_ref, k_ref, vt_ref):
    cb = ckv_ref[...].astype(BF16)
    k = (jnp.dot(cb, wk_ref[...], preferred_element_type=F32)
         + jnp.dot(kr_ref[...].astype(BF16), place_ref[...], preferred_element_type=F32))
    k_ref[0] = k.astype(BF16)
    vt = lax.dot_general(wvt_ref[...], cb, (((1,), (1,)), ((), ())), preferred_element_type=F32)
    vt_ref[0] = vt.astype(BF16)


def _kv_up(ckv, kr, wk, place, wvt):
    n = ckv.shape[0]
    tm = ATTN_BLOCK
    assert n % tm == 0
    row = lambda width: pl.BlockSpec((tm, width), lambda i: (i, 0))
    full = lambda a: pl.BlockSpec(a.shape, lambda i: (0, 0))
    return pl.pallas_call(
        _kv_up_kernel,
        grid=(n // tm,),
        in_specs=[row(KV_RANK), row(ROPE_DIM), full(wk), full(place), full(wvt)],
        out_specs=[pl.BlockSpec((1, tm, H_C * LANES), lambda i: (i, 0, 0)),
                   pl.BlockSpec((1, H_C * V_DIM_C, tm), lambda i: (i, 0, 0))],
        out_shape=[jax.ShapeDtypeStruct((n // tm, tm, H_C * LANES), BF16),
                   jax.ShapeDtypeStruct((n // tm, H_C * V_DIM_C, tm), BF16)],
        compiler_params=_cparams(("parallel",)),
        name="kv_up",
    )(ckv, kr, wk, place, wvt)


def _mla_decode_kernel(q_ref, ckv_ref, kr_ref, ckvn_ref, krn_ref, wabs_ref, wv_ref, o_ref):
    q = q_ref[0]
    ts = q.shape[0]
    qs = jnp.concatenate(
        [jnp.dot(q[:, h * LANES:(h + 1) * LANES], wabs_ref[h], preferred_element_type=F32).astype(BF16)
         for h in range(H_C)], axis=0)
    kc = jnp.concatenate([ckv_ref[0].astype(BF16), kr_ref[0].astype(BF16)], axis=1)
    kn = jnp.concatenate([ckvn_ref[0].astype(BF16), krn_ref[0].astype(BF16)], axis=1)
    nt = (((1,), (1,)), ((), ()))
    sc = lax.dot_general(qs, kc, nt, preferred_element_type=F32)
    sn = lax.dot_general(qs, kn, nt, preferred_element_type=F32)
    m = jnp.maximum(jnp.max(sc, axis=1, keepdims=True), jnp.max(sn, axis=1, keepdims=True))
    pc = jnp.exp2(sc - m)
    pn = jnp.exp2(sn - m)
    l = jnp.sum(pc, axis=1, keepdims=True) + jnp.sum(pn, axis=1, keepdims=True)
    ol = (jnp.dot(pc.astype(BF16), kc[:, :KV_RANK], preferred_element_type=F32)
          + jnp.dot(pn.astype(BF16), kn[:, :KV_RANK], preferred_element_type=F32)) / l
    olb = ol.astype(BF16)
    o = jnp.dot(olb[:ts], wv_ref[0], preferred_element_type=F32)
    for h in range(1, H_C):
        o = o + jnp.dot(olb[h * ts:(h + 1) * ts], wv_ref[h], preferred_element_type=F32)
    o_ref[0] = o.astype(o_ref.dtype)


def _mla_decode(q, ckv_c, kr_c, ckv_n, kr_n, w_abs, w_vout):
    nb, ts, _ = q.shape
    assert ckv_c.shape[1] % CHUNK == 0 and ts <= CHUNK
    per_b = lambda a: pl.BlockSpec((1,) + a.shape[1:], lambda b: (b, 0, 0))
    full = lambda a: pl.BlockSpec(a.shape, lambda b: (0, 0, 0))
    return pl.pallas_call(
        _mla_decode_kernel,
        grid=(nb,),
        in_specs=[per_b(q), per_b(ckv_c), per_b(kr_c), per_b(ckv_n), per_b(kr_n), full(w_abs), full(w_vout)],
        out_specs=pl.BlockSpec((1, ts, H_C * V_DIM_C), lambda b: (b, 0, 0)),
        out_shape=jax.ShapeDtypeStruct((nb, ts, H_C * V_DIM_C), BF16),
        compiler_params=_cparams(("parallel",)),
        name="mla_decode",
    )(q, ckv_c, kr_c, ckv_n, kr_n, w_abs, w_vout)


def _rope_tables(pos, dim, lane0):
    half = dim // 2
    inv = ROPE_THETA ** (-jnp.arange(0, dim, 2, dtype=F32) / dim)
    ang = pos.astype(F32)[:, None] * inv[None, :]
    cos, sin = jnp.cos(ang), jnp.sin(ang)
    zero = jnp.zeros_like(sin)
    c = jnp.concatenate([cos, cos], axis=1)
    s1 = jnp.concatenate([-sin, zero], axis=1)
    s2 = jnp.concatenate([zero, sin], axis=1)
    if lane0 < 0:
        reps = LANES // dim
        return tuple(jnp.tile(a, (1, reps)) for a in (c, s1, s2))
    t = pos.shape[0]
    pad = lambda a, fill: jnp.concatenate(
        [jnp.full((t, lane0), fill, F32), a, jnp.full((t, LANES - lane0 - dim), fill, F32)], axis=1)
    return pad(c, 1.0), pad(s1, 0.0), pad(s2, 0.0)


def _pad_cols(a, width):
    return jnp.pad(a, ((0, 0), (0, width - a.shape[1])))


def _blocks(a, tk):
    nb, t, l = a.shape
    return a.reshape(nb, t // tk, tk, l)


def _cat_pad_time(cache, new, t_pad):
    nb, t0, l = cache.shape
    t1 = new.shape[1]
    return jnp.concatenate([cache, new, jnp.zeros((nb, t_pad - t0 - t1, l), cache.dtype)], axis=1)


def kernel(x_prompt, x_sample, cache_fox_k, cache_fox_v, cache_fox_logf, cache_diff_k, cache_diff_v, cache_mla_ckv, cache_mla_krope, w_in_ab, b_fgate, diff_lq1, diff_lk1, diff_lq2, diff_lk2, diff_subln, w_out_ab, w_in_c, mla_q_norm, mla_kv_norm, mla_w_uq, mla_w_ukv, w_out_c, ln1_g, ln1_b, ln2_g, ln2_b, moe_wg, moe_bg, moe_we, moe_be, moe_w1, moe_w3, moe_w2):
    bp, tp, d = x_prompt.shape
    bs, ts, _ = x_sample.shape
    past = cache_fox_k.shape[2]
    depth = ln1_g.shape[0]
    alpha = (2 * depth) ** 0.25
    tk = ATTN_BLOCK
    assert past % tk == 0
    ns = bs * ts
    t_dec = past + tk

    pos_p = jnp.arange(tp)
    pos_s = jnp.tile(past + jnp.arange(ts), bs)

    xp = x_prompt
    xs = x_sample.reshape(1, ns, d)
    out_ab_p, out_ab_s, out_c_p, out_c_s = [], [], [], []

    for i in range(depth):
        j = i // 2
        if i % 2 == 0:
            lam_init = 0.8 - 0.6 * math.exp(-0.3 * i)
            cuts = [0, A_WIDTH, 2 * A_WIDTH, 3 * A_WIDTH, 3 * A_WIDTH + H_A,
                    3 * A_WIDTH + H_A + B_QK_WIDTH, 3 * A_WIDTH + H_A + 2 * B_QK_WIDTH,
                    3 * A_WIDTH + H_A + 2 * B_QK_WIDTH + B_V_WIDTH]
            w = w_in_ab[j]
            piece = lambda a: w[:, cuts[a]:cuts[a + 1]]
            w6 = jnp.stack([piece(0), piece(1), piece(2), piece(4), piece(5), piece(6)]).astype(BF16)
            wvt = jnp.stack([piece(2).T, piece(6).T]).astype(BF16)
            wf = _pad_cols(piece(3), LANES).astype(BF16)
            bf = _pad_cols(b_fgate[j][None, :], LANES)
            wout = w_out_ab[j].astype(BF16)
            diff_extra = (diff_lq1[j][None, :], diff_lk1[j][None, :], diff_lq2[j][None, :],
                          diff_lk2[j][None, :], diff_subln[j][None, :])

            tabs = _rope_tables(pos_p, HEAD_DIM, -1)
            (qa, ka, kab, va, vat, lf, lfw, qb, kb, kbb, vb, vbt) = _proj_ab(xp, w6, wvt, wf, bf, tabs)
            bias = _blocks(_decay_bias(lfw), tk)
            oa = _attention("fox", qa, _blocks(kab, tk), vat, (bias,), n_pairs=H_A // 2, mask_shift=0)
            ob = _attention("diff", qb, _blocks(kbb, tk), vbt, diff_extra,
                            n_pairs=H_B, mask_shift=int(math.log2(CHUNK)), lam_init=lam_init)
            out_ab_p.append((ka.reshape(bp, tp, H_A, HEAD_DIM), va.reshape(bp, tp, H_A, HEAD_DIM), lf,
                             kb.reshape(bp, tp, H_B, 2, HEAD_DIM), vb.reshape(bp, tp, H_B, 2 * HEAD_DIM)))
            xp2 = _outproj_ln(xp.reshape(bp * tp, d), [oa.reshape(bp * tp, -1), ob.reshape(bp * tp, -1)],
                              [wout[:A_WIDTH], wout[A_WIDTH:]], ln1_g[i][None, :], ln1_b[i][None, :], alpha)

            tabs = _rope_tables(pos_s, HEAD_DIM, -1)
            (qa, ka, kab, va, _, lf, lfw, qb, kb, kbb, vb, _) = _proj_ab(xs, w6, wvt, wf, bf, tabs)
            rs = lambda a: a.reshape(bs, ts, a.shape[-1])
            cache_lfw = jnp.pad(cache_fox_logf[j].astype(F32), ((0, 0), (0, 0), (0, LANES - H_A)))
            bias = _decay_bias(_cat_pad_time(cache_lfw, rs(lfw), t_dec))
            flat = lambda c: c.reshape(bs, past, -1)
            oa = _decode_attention("fox", rs(qa), flat(cache_fox_k[j]), flat(cache_fox_v[j]), rs(kab), rs(va),
                                   (bias,))
            ob = _decode_attention("diff", rs(qb), flat(cache_diff_k[j]), flat(cache_diff_v[j]), rs(kbb),
                                   rs(vb), diff_extra, lam_init=lam_init)
            out_ab_s.append((ka.reshape(bs, ts, H_A, HEAD_DIM), va.reshape(bs, ts, H_A, HEAD_DIM),
                             lf.reshape(bs, ts, H_A), kb.reshape(bs, ts, H_B, 2, HEAD_DIM),
                             vb.reshape(bs, ts, H_B, 2 * HEAD_DIM)))
            xs2 = _outproj_ln(xs.reshape(ns, d), [oa.reshape(ns, -1), ob.reshape(ns, -1)],
                              [wout[:A_WIDTH], wout[A_WIDTH:]], ln1_g[i][None, :], ln1_b[i][None, :], alpha)
        else:
            wc = w_in_c[j]
            kr_cols = _pad_cols(wc[:, Q_RANK + KV_RANK:], LANES)
            win = jnp.concatenate([wc[:, :Q_RANK + KV_RANK], kr_cols], axis=1).astype(BF16)
            wuq3 = jnp.pad(mla_w_uq[j].reshape(Q_RANK, H_C, NOPE_DIM + ROPE_DIM),
                           ((0, 0), (0, 0), (0, LANES - NOPE_DIM - ROPE_DIM)))
            wuq = wuq3.reshape(Q_RANK, H_C * LANES).astype(BF16)
            r0, r1, r2 = NOPE_DIM, NOPE_DIM + ROPE_DIM // 2, NOPE_DIM + ROPE_DIM
            wrot = jnp.zeros_like(wuq3).at[:, :, r0:r1].set(-wuq3[:, :, r1:r2]).at[:, :, r1:r2].set(wuq3[:, :, r0:r1])
            wrot = wrot.reshape(Q_RANK, H_C * LANES).astype(BF16)
            q_tabs = lambda pos: (lambda c, s1, s2: (c, s2 - s1))(*_rope_tables(pos, ROPE_DIM, NOPE_DIM))
            wukv = mla_w_ukv[j].reshape(KV_RANK, H_C, NOPE_DIM + V_DIM_C)
            wk = jnp.pad(wukv[:, :, :NOPE_DIM], ((0, 0), (0, 0), (0, LANES - NOPE_DIM)))
            wk = wk.reshape(KV_RANK, H_C * LANES).astype(BF16)
            wvt = wukv[:, :, NOPE_DIM:].reshape(KV_RANK, H_C * V_DIM_C).T.astype(BF16)
            place = jnp.tile(_pad_cols(jnp.concatenate(
                [jnp.zeros((ROPE_DIM, NOPE_DIM), F32), jnp.eye(ROPE_DIM, dtype=F32)], axis=1), LANES),
                (1, H_C)).astype(BF16)
            gq = mla_q_norm[j][None, :]
            gkv = mla_kv_norm[j][None, :]
            wout = w_out_c[j].astype(BF16)

            q, ckv, kr = _proj_c(xp, win, gq, gkv, wuq, wrot, q_tabs(pos_p), _rope_tables(pos_p, ROPE_DIM, 0))
            kc, vct = _kv_up(ckv.reshape(bp * tp, KV_RANK), kr.reshape(bp * tp, ROPE_DIM), wk, place, wvt)
            per_seq = lambda a, nb: a.reshape((nb, a.shape[0] // nb) + a.shape[1:])
            oc = _attention("mla", q, per_seq(kc, bp), per_seq(vct, bp), (), n_pairs=H_C // 2,
                            mask_shift=int(math.log2(CHUNK)))
            out_c_p.append((ckv, kr))
            xp2 = _outproj_ln(xp.reshape(bp * tp, d), [oc.reshape(bp * tp, -1)], [wout],
                              ln1_g[i][None, :], ln1_b[i][None, :], alpha)

            q, ckv, kr = _proj_c(xs, win, gq, gkv, wuq, wrot, q_tabs(pos_s), _rope_tables(pos_s, ROPE_DIM, 0))
            w_abs = jnp.zeros((H_C, LANES, 2 * LANES), F32)
            w_abs = w_abs.at[:, :NOPE_DIM, :KV_RANK].set(jnp.transpose(wukv[:, :, :NOPE_DIM], (1, 2, 0)))
            w_abs = w_abs.at[:, NOPE_DIM:NOPE_DIM + ROPE_DIM, KV_RANK:KV_RANK + ROPE_DIM].set(
                jnp.eye(ROPE_DIM, dtype=F32))
            w_vout = jnp.einsum("khd,hg->hkgd", wukv[:, :, NOPE_DIM:], jnp.eye(H_C, dtype=F32))
            w_vout = w_vout.reshape(H_C, KV_RANK, H_C * V_DIM_C)
            wide = lambda a: jnp.pad(a.astype(F32), ((0, 0), (0, 0), (0, LANES - ROPE_DIM)))
            oc = _mla_decode(q.reshape(bs, ts, -1), cache_mla_ckv[j].astype(F32), wide(cache_mla_krope[j]),
                             ckv.reshape(bs, ts, KV_RANK), wide(kr.reshape(bs, ts, ROPE_DIM)),
                             w_abs.astype(BF16), w_vout.astype(BF16))
            out_c_s.append((ckv.reshape(bs, ts, KV_RANK), kr.reshape(bs, ts, ROPE_DIM)))
            xs2 = _outproj_ln(xs.reshape(ns, d), [oc.reshape(ns, -1)], [wout],
                              ln1_g[i][None, :], ln1_b[i][None, :], alpha)

        wr = _pad_cols(jnp.concatenate(
            [moe_wg[i]] + [moe_we[i][gi] for gi in range(N_GROUPS)], axis=1), LANES)
        wrh = wr.astype(BF16)
        wrl = (wr - wrh.astype(F32)).astype(BF16)
        br = _pad_cols(jnp.concatenate([moe_bg[i], moe_be[i].reshape(-1)])[None, :], LANES)
        moe_w = (moe_w1[i], moe_w3[i], moe_w2[i])
        g2, b2 = ln2_g[i][None, :], ln2_b[i][None, :]
        xp = _moe_ln(xp2, wrh, wrl, br, *moe_w, g2, b2, alpha).reshape(bp, tp, d)
        xs = _moe_ln(xs2, wrh, wrl, br, *moe_w, g2, b2, alpha).reshape(1, ns, d)

    stack = lambda rows, n: jnp.stack([r[n] for r in rows])
    return (xp, xs.reshape(bs, ts, d),
            stack(out_ab_p, 0), stack(out_ab_p, 1), stack(out_ab_p, 2), stack(out_ab_p, 3), stack(out_ab_p, 4),
            stack(out_c_p, 0), stack(out_c_p, 1),
            stack(out_ab_s, 0), stack(out_ab_s, 1), stack(out_ab_s, 2), stack(out_ab_s, 3), stack(out_ab_s, 4),
            stack(out_c_s, 0), stack(out_c_s, 1))
```

```python
import functools
import math

import jax
import jax.numpy as jnp
from jax import lax
from jax.experimental import pallas as pl
from jax.experimental.pallas import tpu as pltpu

F32 = jnp.float32
BF16 = jnp.bfloat16

D_MODEL = 1024
CHUNK = 64
HEAD_DIM = 64
ROPE_THETA = 10000.0
H_A = 8
H_B = 4
H_C = 16
Q_RANK = 256
KV_RANK = 128
NOPE_DIM = 64
ROPE_DIM = 32
V_DIM_C = 64
N_GROUPS = 4
EXPERTS_PER_GROUP = 4
N_EXPERTS = N_GROUPS * EXPERTS_PER_GROUP
D_EXPERT = 256
A_WIDTH = H_A * HEAD_DIM
B_QK_WIDTH = H_B * 2 * HEAD_DIM
B_V_WIDTH = H_B * 2 * HEAD_DIM
FGATE_BIAS = 3.0
LN_EPS = 1e-5
RMS_EPS = 1e-6
NEG_INF = -1e30
LOG2E = math.log2(math.e)

LANES = 128
BF16_ROWS = 16
PANEL = {"fox": 1024, "diff": 256, "mla": 1024}
BIAS_PIECES = 3
VMEM_LIMIT = 48 * 1024 * 1024
ATTN_VMEM_LIMIT = 56 * 1024 * 1024
ATTN_BLOCK = 512
Q_KEY_BLOCKS = 4
ROW_TILE = 512
MOE_TILE = 1024
EXPERTS_PER_STEP = 2
BIAS_ROWS_PER_STEP = 4096
GATE_LANE0 = N_GROUPS


def _cparams(sem, vmem_limit=VMEM_LIMIT):
    return pltpu.CompilerParams(dimension_semantics=sem, vmem_limit_bytes=vmem_limit)


def _rope3(x, c, s1, s2, shift_up, shift_down):
    return x * c + pltpu.roll(x, shift_up, 1) * s1 + pltpu.roll(x, shift_down, 1) * s2


def _layer_norm(y, g, b):
    mu = jnp.mean(y, axis=-1, keepdims=True)
    d = y - mu
    var = jnp.mean(d * d, axis=-1, keepdims=True)
    return d * lax.rsqrt(var + LN_EPS) * g + b


def _split3(x):
    hi = x.astype(BF16)
    r1 = x - hi.astype(F32)
    mid = r1.astype(BF16)
    return hi, mid, (r1 - mid.astype(F32)).astype(BF16)


def _proj_ab_kernel(x_ref, w_ref, wvt_ref, wf_ref, bf_ref, c_ref, s1_ref, s2_ref,
                    qa_ref, ka_ref, kab_ref, va_ref, vat_ref, lf_ref, lfw_ref,
                    qb_ref, kb_ref, kbb_ref, vb_ref, vbt_ref):
    xb = x_ref[0].astype(BF16)

    def mm(i):
        return jnp.dot(xb, w_ref[i], preferred_element_type=F32)

    def mm_t(i):
        return lax.dot_general(wvt_ref[i], xb, (((1,), (1,)), ((), ())), preferred_element_type=F32)

    qa_ref[0] = (mm(0) * (HEAD_DIM ** -0.5 * LOG2E)).astype(BF16)
    ka = mm(1)
    ka_ref[0] = ka
    kab_ref[0] = ka.astype(BF16)
    va_ref[0] = mm(2)
    vat_ref[0, 0] = mm_t(0).astype(BF16)

    z = jnp.dot(xb, wf_ref[...], preferred_element_type=F32) + bf_ref[...]
    lf = jnp.minimum(z, 0.0) - jnp.log1p(jnp.exp(-jnp.abs(z)))
    lf_ref[0] = lf[:, :H_A]
    lfw_ref[0] = lf

    c, s1, s2 = c_ref[...], s1_ref[...], s2_ref[...]
    qb = mm(3)
    kb = mm(4)
    for s in range(B_QK_WIDTH // LANES):
        sl = slice(s * LANES, (s + 1) * LANES)
        qs = _rope3(qb[:, sl], c, s1, s2, LANES - HEAD_DIM // 2, HEAD_DIM // 2)
        qb_ref[0, :, sl] = (qs * (HEAD_DIM ** -0.5 * LOG2E)).astype(BF16)
        ks = _rope3(kb[:, sl], c, s1, s2, LANES - HEAD_DIM // 2, HEAD_DIM // 2)
        kb_ref[0, :, sl] = ks
        kbb_ref[0, :, sl] = ks.astype(BF16)
    vb_ref[0] = mm(5)
    vbt_ref[0, 0] = mm_t(1).astype(BF16)


def _proj_ab(x, w6, wvt, wf, bf, tabs):
    nb, t, _ = x.shape
    tm = min(ROW_TILE, t)
    assert t % tm == 0
    w = A_WIDTH
    tok = lambda width: pl.BlockSpec((1, tm, width), lambda b, i: (b, i, 0))
    tr = pl.BlockSpec((1, 1, w, tm), lambda b, i: (b, i, 0, 0))
    tab = pl.BlockSpec((tm, LANES), lambda b, i: (i, 0))
    full = lambda a: pl.BlockSpec(a.shape, lambda b, i: (0,) * a.ndim)
    sds = lambda width, dt: jax.ShapeDtypeStruct((nb, t, width), dt)
    sds_t = jax.ShapeDtypeStruct((nb, t // tm, w, tm), BF16)
    return pl.pallas_call(
        _proj_ab_kernel,
        grid=(nb, t // tm),
        in_specs=[tok(D_MODEL), full(w6), full(wvt), full(wf), full(bf), tab, tab, tab],
        out_specs=[tok(w), tok(w), tok(w), tok(w), tr, tok(H_A), tok(LANES), tok(w), tok(w), tok(w), tok(w), tr],
        out_shape=[sds(w, BF16), sds(w, F32), sds(w, BF16), sds(w, F32), sds_t, sds(H_A, F32), sds(LANES, F32),
                   sds(w, BF16), sds(w, F32), sds(w, BF16), sds(w, F32), sds_t],
        compiler_params=_cparams(("parallel", "parallel")),
        name="proj_ab",
    )(x, w6, wvt, wf, bf, *tabs)


def _decay_bias_kernel(lf_ref, spread_ref, lower_ref, o_ref, carry_ref):
    @pl.when(pl.program_id(1) == 0)
    def _():
        carry_ref[...] = jnp.zeros_like(carry_ref)

    spread = spread_ref[...]
    lower = lower_ref[...]
    tc = lower.shape[0]
    lane = lax.broadcasted_iota(jnp.int32, (1, LANES), 1).astype(F32)
    piece = lane - BIAS_PIECES * jnp.floor((lane + 0.5) * (1.0 / BIAS_PIECES))
    carry = carry_ref[...]
    for r in range(lf_ref.shape[1] // tc):
        rows = slice(r * tc, (r + 1) * tc)
        x = lf_ref[0, rows, :]
        xr = sum(jnp.dot(p, spread, preferred_element_type=F32) for p in _split3(x))
        c = sum(jnp.dot(lower, p, preferred_element_type=F32) for p in _split3(xr)) + carry
        carry = c[tc - 1:tc, :]
        hi, mid, lo = (p.astype(F32) for p in _split3(c * (-LOG2E)))
        o_ref[0, rows, :] = jnp.where(piece == 0.0, hi, jnp.where(piece == 1.0, mid, lo)).astype(BF16)
    carry_ref[...] = carry


def _decay_bias(lf_wide):
    nb, t, _ = lf_wide.shape
    tc = min(ATTN_BLOCK, t)
    tb = min(BIAS_ROWS_PER_STEP, t)
    assert t % tb == 0 and tb % tc == 0
    spec = pl.BlockSpec((1, tb, LANES), lambda b, i: (b, i, 0))
    src = jnp.arange(LANES)[:, None]
    dst = jnp.arange(LANES)[None, :]
    spread = ((dst // BIAS_PIECES == src) & (src < H_A)).astype(BF16)
    lower = jnp.tril(jnp.ones((tc, tc), BF16))
    const = lambda a: pl.BlockSpec(a.shape, lambda b, i: (0, 0))
    return pl.pallas_call(
        _decay_bias_kernel,
        grid=(nb, t // tb),
        in_specs=[spec, const(spread), const(lower)],
        out_specs=spec,
        out_shape=jax.ShapeDtypeStruct((nb, t, LANES), BF16),
        scratch_shapes=[pltpu.VMEM((1, LANES), F32)],
        compiler_params=_cparams(("parallel", "arbitrary")),
        name="cumsum",
    )(lf_wide, spread, lower)


def _attn_kernel(*refs, mode, tq, tk, mask_shift, lam_init):
    if mode == "diff":
        q_ref, k_ref, vt_ref, lq1_ref, lk1_ref, lq2_ref, lk2_ref, sub_ref, o_ref = refs[:9]
    elif mode == "fox":
        q_ref, k_ref, vt_ref, b_ref, o_ref = refs[:5]
    else:
        q_ref, k_ref, vt_ref, o_ref = refs[:4]
    m_sc, acc_sc, sa, bma, sb, bmb = refs[-6:]
    v_rows = LANES if mode == "diff" else HEAD_DIM
    sa_sc, sb_sc = (sa, bma), (sb, bmb)

    qi = pl.program_id(2)
    q = q_ref[0]
    lane = lax.broadcasted_iota(jnp.int32, (1, LANES), 1)
    if mode == "mla":
        qs = [q[:, :LANES], q[:, LANES:]]
    else:
        zero = jnp.zeros_like(q)
        qs = [jnp.where(lane < HEAD_DIM, q, zero), jnp.where(lane >= HEAD_DIM, q, zero)]
        if mode == "fox":
            def pick(i):
                lo = BIAS_PIECES * (2 * pl.program_id(1) + i)
                hot = jnp.where((lane >= lo) & (lane < lo + BIAS_PIECES), 1.0, 0.0)
                return jnp.broadcast_to(hot, (tq, LANES)).astype(BF16)

            qs = [jnp.concatenate([qs[i], pick(i)], axis=1) for i in range(2)]

    m_sc[...] = jnp.full(m_sc.shape, NEG_INF, F32)
    acc_sc[...] = jnp.zeros(acc_sc.shape, F32)

    pw = min(PANEL[mode], tq)

    def scores(j, bufs, q0, q1):
        s_sc, bm_sc = bufs
        cs = slice(q0, q1)
        k = k_ref[0, j]
        if mode == "fox":
            k = jnp.concatenate([k, b_ref[0, j]], axis=1)
        for i in range(2):
            ki = k[:, i * LANES:(i + 1) * LANES] if mode == "mla" else k
            st = lax.dot_general(ki, qs[i][cs], (((1,), (1,)), ((), ())), preferred_element_type=F32)
            s_sc[i, :, cs] = st
            bm_sc[i, :, cs] = jnp.max(st, axis=0, keepdims=True)

    def consume(j, bufs, q0, q1, key0=None):
        s_sc, bm_sc = bufs
        cs = slice(q0, q1)
        vt = vt_ref[0, j]
        masked = key0 is not None and ((key0 + tk - 1) >> mask_shift) > (q0 >> mask_shift)
        for i in range(2):
            st = s_sc[i, :, cs]
            if masked:
                key = lax.broadcasted_iota(jnp.int32, (tk, q1 - q0), 0) + key0
                qry = lax.broadcasted_iota(jnp.int32, (tk, q1 - q0), 1) + q0
                vis = lax.shift_right_logical(key, mask_shift) <= lax.shift_right_logical(qry, mask_shift)
                st = jnp.where(vis, st, NEG_INF)
                blk_max = jnp.max(st, axis=0, keepdims=True)
            else:
                blk_max = bm_sc[i, :, cs]
            m_prev = m_sc[i, :, cs]
            m_new = jnp.maximum(m_prev, blk_max)
            alpha = jnp.exp2(m_prev - m_new)
            p = jnp.exp2(st - m_new).astype(BF16)
            vi = vt if mode == "diff" else vt[i * HEAD_DIM:(i + 1) * HEAD_DIM]
            vi = jnp.concatenate([vi, jnp.ones((BF16_ROWS, tk), BF16)], axis=0)
            acc_sc[i, :, cs] = alpha * acc_sc[i, :, cs] + jnp.dot(vi, p, preferred_element_type=F32)
            m_sc[i, :, cs] = m_new

    def stage(nxt, cur):
        for q0 in range(0, tq, pw):
            scores(nxt[0], nxt[1], q0, q0 + pw)
            consume(cur[0], cur[1], q0, q0 + pw)

    n_diag = tq // tk
    nfull = n_diag * qi
    bufs = (sa_sc, sb_sc)

    def trip(jj, carry):
        for d in range(n_diag):
            j = n_diag * jj + d
            stage((j + 1, bufs[(d + 1) % 2]), (j, bufs[d % 2]))
        return carry

    for q0 in range(0, tq, pw):
        scores(0, sa_sc, q0, q0 + pw)
    lax.fori_loop(0, qi, trip, 0)

    pt = min(pw, tk)
    sees = lambda d, q1: ((d * tk) >> mask_shift) <= ((q1 - 1) >> mask_shift)
    for d in range(n_diag):
        for q0 in range(0, tq, pt):
            if d + 1 < n_diag and sees(d + 1, q0 + pt):
                scores(nfull + d + 1, bufs[(d + 1) % 2], q0, q0 + pt)
            if sees(d, q0 + pt):
                consume(nfull + d, bufs[d % 2], q0, q0 + pt, key0=d * tk)

    if mode == "diff":
        lam = (jnp.exp(jnp.sum(lq1_ref[...] * lk1_ref[...], axis=1, keepdims=True))
               - jnp.exp(jnp.sum(lq2_ref[...] * lk2_ref[...], axis=1, keepdims=True)) + lam_init)
    for q0 in range(0, tq, pw):
        cs = slice(q0, q0 + pw)
        o0 = acc_sc[0, :v_rows, cs] / acc_sc[0, v_rows:v_rows + 1, cs]
        o1 = acc_sc[1, :v_rows, cs] / acc_sc[1, v_rows:v_rows + 1, cs]
        if mode == "diff":
            o = o0 - lam * o1
            ms = jnp.mean(o * o, axis=0, keepdims=True)
            o = (o * lax.rsqrt(ms + RMS_EPS)).T * sub_ref[...] * (1.0 - lam_init)
        else:
            o = jnp.concatenate([o0, o1], axis=0).T
        o_ref[0, cs, :] = o.astype(o_ref.dtype)


def _attention(mode, q, k, vt, extra, *, n_pairs, mask_shift, lam_init=0.0):
    nb, t_q, _ = q.shape
    _, nkb, tk, _ = k.shape
    tq = Q_KEY_BLOCKS * tk
    assert t_q % tq == 0 and nkb == t_q // tk
    qw = 2 * LANES if mode == "mla" else LANES
    in_specs = [
        pl.BlockSpec((1, tq, qw), lambda b, p, i: (b, i, p)),
        pl.BlockSpec((1, nkb, tk, qw), lambda b, p, i: (b, 0, 0, p)),
        pl.BlockSpec((1, nkb, LANES, tk), lambda b, p, i: (b, 0, p, 0)),
    ]
    if mode == "fox":
        in_specs.append(pl.BlockSpec((1, nkb, tk, LANES), lambda b, p, i: (b, 0, 0, 0)))
    elif mode == "diff":
        in_specs += [pl.BlockSpec(a.shape, lambda b, p, i: (0, 0)) for a in extra]
    kern = functools.partial(_attn_kernel, mode=mode, tq=tq, tk=tk, mask_shift=mask_shift, lam_init=lam_init)
    return pl.pallas_call(
        kern,
        grid=(nb, n_pairs, t_q // tq),
        in_specs=in_specs,
        out_specs=pl.BlockSpec((1, tq, LANES), lambda b, p, i: (b, i, p)),
        out_shape=jax.ShapeDtypeStruct((nb, t_q, n_pairs * LANES), BF16),
        scratch_shapes=[pltpu.VMEM((2, 1, tq), F32),
                        pltpu.VMEM((2, (LANES if mode == "diff" else HEAD_DIM) + BF16_ROWS, tq), F32),
                        pltpu.VMEM((2, tk, tq), F32), pltpu.VMEM((2, 1, tq), F32),
                        pltpu.VMEM((2, tk, tq), F32), pltpu.VMEM((2, 1, tq), F32)],
        compiler_params=_cparams(("parallel", "parallel", "arbitrary"), ATTN_VMEM_LIMIT),
        name="attn_" + mode,
    )(q, k, vt, *extra)


def _decode_attn_kernel(*refs, mode, lam_init):
    if mode == "fox":
        q_ref, kc_ref, vc_ref, kn_ref, vn_ref, b_ref, o_ref = refs
    else:
        q_ref, kc_ref, vc_ref, kn_ref, vn_ref, lq1_ref, lk1_ref, lq2_ref, lk2_ref, sub_ref, o_ref = refs
        lam = (jnp.exp(jnp.sum(lq1_ref[...] * lk1_ref[...], axis=1, keepdims=True))
               - jnp.exp(jnp.sum(lq2_ref[...] * lk2_ref[...], axis=1, keepdims=True)) + lam_init)
    ts = q_ref.shape[1]
    past = kc_ref.shape[1]
    lane = lax.broadcasted_iota(jnp.int32, (1, LANES), 1)
    row = lax.broadcasted_iota(jnp.int32, (ts, ts), 0)
    col = lax.broadcasted_iota(jnp.int32, (ts, ts), 1)
    nt = (((1,), (1,)), ((), ()))
    for p in range(q_ref.shape[2] // LANES):
        sl = slice(p * LANES, (p + 1) * LANES)
        q = q_ref[0, :, sl]
        kc = kc_ref[0, :, sl].astype(BF16)
        kn = kn_ref[0, :, sl]
        vc = vc_ref[0, :, sl].astype(BF16)
        vn = vn_ref[0, :, sl].astype(BF16)
        if mode == "fox":
            kc = jnp.concatenate([kc, b_ref[0, :past, :]], axis=1)
            kn = jnp.concatenate([kn, b_ref[0, past:past + ts, :]], axis=1)
        zero = jnp.zeros_like(q)
        outs = []
        for i in range(2):
            qi = jnp.where(lane < HEAD_DIM, q, zero) if i == 0 else jnp.where(lane >= HEAD_DIM, q, zero)
            if mode == "fox":
                lo = BIAS_PIECES * (2 * p + i)
                hot = jnp.where((lane >= lo) & (lane < lo + BIAS_PIECES), 1.0, 0.0)
                qi = jnp.concatenate([qi, jnp.broadcast_to(hot, (ts, LANES)).astype(BF16)], axis=1)
            sc = lax.dot_general(qi, kc, nt, preferred_element_type=F32)
            sn = lax.dot_general(qi, kn, nt, preferred_element_type=F32)
            if mode == "fox":
                sn = jnp.where(col <= row, sn, NEG_INF)
            m = jnp.maximum(jnp.max(sc, axis=1, keepdims=True), jnp.max(sn, axis=1, keepdims=True))
            pc = jnp.exp2(sc - m)
            pn = jnp.exp2(sn - m)
            l = jnp.sum(pc, axis=1, keepdims=True) + jnp.sum(pn, axis=1, keepdims=True)
            outs.append((jnp.dot(pc.astype(BF16), vc, preferred_element_type=F32)
                         + jnp.dot(pn.astype(BF16), vn, preferred_element_type=F32)) / l)
        if mode == "fox":
            o = jnp.where(lane < HEAD_DIM, outs[0], outs[1])
        else:
            o = outs[0] - lam * outs[1]
            ms = jnp.mean(o * o, axis=1, keepdims=True)
            o = o * lax.rsqrt(ms + RMS_EPS) * sub_ref[...] * (1.0 - lam_init)
        o_ref[0, :, sl] = o.astype(o_ref.dtype)


def _decode_attention(mode, q, k_cache, v_cache, k_new, v_new, extra, lam_init=0.0):
    nb, ts, w = q.shape
    assert k_cache.shape[1] % CHUNK == 0 and ts <= CHUNK
    per_b = lambda a: pl.BlockSpec((1,) + a.shape[1:], lambda b: (b, 0, 0))
    if mode == "fox":
        extra_specs = [per_b(extra[0])]
    else:
        extra_specs = [pl.BlockSpec(a.shape, lambda b: (0, 0)) for a in extra]
    return pl.pallas_call(
        functools.partial(_decode_attn_kernel, mode=mode, lam_init=lam_init),
        grid=(nb,),
        in_specs=[per_b(q), per_b(k_cache), per_b(v_cache), per_b(k_new), per_b(v_new)] + extra_specs,
        out_specs=pl.BlockSpec((1, ts, w), lambda b: (b, 0, 0)),
        out_shape=jax.ShapeDtypeStruct((nb, ts, w), BF16),
        compiler_params=_cparams(("parallel",)),
        name="decode_" + mode,
    )(q, k_cache, v_cache, k_new, v_new, *extra)


def _outproj_ln_kernel(*refs, n_in, alpha):
    x_ref = refs[0]
    o_refs = refs[1:1 + n_in]
    w_refs = refs[1 + n_in:1 + 2 * n_in]
    g_ref, b_ref, y_ref = refs[1 + 2 * n_in:]
    mix = jnp.dot(o_refs[0][...], w_refs[0][...], preferred_element_type=F32)
    for o_r, w_r in zip(o_refs[1:], w_refs[1:]):
        mix = mix + jnp.dot(o_r[...], w_r[...], preferred_element_type=F32)
    y_ref[...] = _layer_norm(alpha * x_ref[...] + mix, g_ref[...], b_ref[...])


def _outproj_ln(x, outs, ws, g, b, alpha):
    n, d = x.shape
    tm = min(ROW_TILE, n)
    assert n % tm == 0
    row = lambda width: pl.BlockSpec((tm, width), lambda i: (i, 0))
    full = lambda a: pl.BlockSpec(a.shape, lambda i: (0, 0))
    return pl.pallas_call(
        functools.partial(_outproj_ln_kernel, n_in=len(outs), alpha=alpha),
        grid=(n // tm,),
        in_specs=[row(d)] + [row(o.shape[1]) for o in outs] + [full(w) for w in ws] + [full(g), full(b)],
        out_specs=row(d),
        out_shape=jax.ShapeDtypeStruct((n, d), F32),
        compiler_params=_cparams(("parallel",)),
        name="outproj_ln",
    )(x, *outs, *ws, g, b)


def _route(logits):
    lane = lax.broadcasted_iota(jnp.int32, logits.shape, 1).astype(F32)
    big = float(1 << 20)
    is_g = lane < N_GROUPS
    lg = jnp.where(is_g, logits, NEG_INF)
    eg = jnp.where(is_g, jnp.exp(lg - jnp.max(lg, axis=1, keepdims=True)), 0.0)
    pg = eg / jnp.sum(eg, axis=1, keepdims=True)
    p_g = jnp.max(pg, axis=1, keepdims=True)
    gidx = jnp.min(jnp.where(is_g & (pg == p_g), lane, big), axis=1, keepdims=True)
    lo = GATE_LANE0 + EXPERTS_PER_GROUP * gidx
    sel = (lane >= lo) & (lane < lo + EXPERTS_PER_GROUP)
    le = jnp.where(sel, logits, NEG_INF)
    ee = jnp.where(sel, jnp.exp(le - jnp.max(le, axis=1, keepdims=True)), 0.0)
    pe = ee / jnp.sum(ee, axis=1, keepdims=True)
    v1 = jnp.max(jnp.where(sel, pe, -1.0), axis=1, keepdims=True)
    i1 = jnp.min(jnp.where(sel & (pe == v1), lane, big), axis=1, keepdims=True)
    rest = sel & (lane != i1)
    v2 = jnp.max(jnp.where(rest, pe, -1.0), axis=1, keepdims=True)
    i2 = jnp.min(jnp.where(rest & (pe == v2), lane, big), axis=1, keepdims=True)
    tot = v1 + v2
    w1 = v1 / tot * p_g
    w2 = v2 / tot * p_g
    return jnp.where(lane == i1, w1, jnp.where(lane == i2, w2, 0.0))


def _moe_ln_kernel(x_ref, wrh_ref, wrl_ref, br_ref, w1_ref, w3_ref, w2_ref, g_ref, b_ref, y_ref,
                   xb_sc, gate_sc, acc_sc, *, alpha):
    e = pl.program_id(1)

    @pl.when(e == 0)
    def _():
        x = x_ref[...]
        xh = x.astype(BF16)
        xl = (x - xh.astype(F32)).astype(BF16)
        xb_sc[...] = xh
        logits = (jnp.dot(xh, wrh_ref[...], preferred_element_type=F32)
                  + jnp.dot(xl, wrh_ref[...], preferred_element_type=F32)
                  + jnp.dot(xh, wrl_ref[...], preferred_element_type=F32) + br_ref[...])
        gate_sc[...] = _route(logits)
        acc_sc[...] = jnp.zeros_like(acc_sc)

    xb = xb_sc[...]
    lane = lax.broadcasted_iota(jnp.int32, (1, LANES), 1)
    for s in range(w1_ref.shape[0]):
        h1 = jnp.dot(xb, w1_ref[s].astype(BF16), preferred_element_type=F32)
        h3 = jnp.dot(xb, w3_ref[s].astype(BF16), preferred_element_type=F32)
        hdn = (h1 * jax.nn.sigmoid(h1)) * h3
        y = jnp.dot(hdn.astype(BF16), w2_ref[s].astype(BF16), preferred_element_type=F32)
        expert_lane = e * w1_ref.shape[0] + s + GATE_LANE0
        ge = jnp.sum(jnp.where(lane == expert_lane, gate_sc[...], 0.0), axis=1, keepdims=True)
        acc_sc[...] += ge * y

    @pl.when(e == pl.num_programs(1) - 1)
    def _():
        y_ref[...] = _layer_norm(alpha * x_ref[...] + acc_sc[...], g_ref[...], b_ref[...])


def _moe_ln(x, wrh, wrl, br, w1, w3, w2, g, b, alpha):
    n, d = x.shape
    tm = min(MOE_TILE, n)
    assert n % tm == 0
    ne = w1.shape[0]
    assert ne % EXPERTS_PER_STEP == 0
    per_expert = lambda a: pl.BlockSpec((EXPERTS_PER_STEP,) + a.shape[1:], lambda i, e: (e, 0, 0))
    full = lambda a: pl.BlockSpec(a.shape, lambda i, e: (0, 0))
    return pl.pallas_call(
        functools.partial(_moe_ln_kernel, alpha=alpha),
        grid=(n // tm, ne // EXPERTS_PER_STEP),
        in_specs=[pl.BlockSpec((tm, d), lambda i, e: (i, 0)), full(wrh), full(wrl), full(br),
                  per_expert(w1), per_expert(w3), per_expert(w2),
                  full(g), full(b)],
        out_specs=pl.BlockSpec((tm, d), lambda i, e: (i, 0)),
        out_shape=jax.ShapeDtypeStruct((n, d), F32),
        scratch_shapes=[pltpu.VMEM((tm, d), BF16), pltpu.VMEM((tm, LANES), F32), pltpu.VMEM((tm, d), F32)],
        compiler_params=_cparams(("parallel", "arbitrary")),
        name="moe_ln",
    )(x, wrh, wrl, br, w1, w3, w2, g, b)


def _proj_c_kernel(x_ref, win_ref, gq_ref, gkv_ref, wuq_ref, wrot_ref, cq_ref, sq_ref,
                   ck_ref, s1k_ref, s2k_ref, q_ref, ckv_ref, kr_ref):
    xb = x_ref[0].astype(BF16)
    h = jnp.dot(xb, win_ref[...], preferred_element_type=F32)
    qa = h[:, :Q_RANK]
    kva = h[:, Q_RANK:Q_RANK + KV_RANK]
    krw = h[:, Q_RANK + KV_RANK:]
    qn = qa * lax.rsqrt(jnp.mean(qa * qa, axis=1, keepdims=True) + RMS_EPS) * gq_ref[...]
    ckv_ref[0] = kva * lax.rsqrt(jnp.mean(kva * kva, axis=1, keepdims=True) + RMS_EPS) * gkv_ref[...]
    half = ROPE_DIM // 2
    kr = _rope3(krw, ck_ref[...], s1k_ref[...], s2k_ref[...], LANES - half, half)
    kr_ref[0] = kr[:, :ROPE_DIM]
    qnb = qn.astype(BF16)
    q = jnp.dot(qnb, wuq_ref[...], preferred_element_type=F32)
    q_rot = jnp.dot(qnb, wrot_ref[...], preferred_element_type=F32)
    cq, sq = cq_ref[...], sq_ref[...]
    scale = (NOPE_DIM + ROPE_DIM) ** -0.5 * LOG2E
    for hd in range(H_C):
        sl = slice(hd * LANES, (hd + 1) * LANES)
        q_ref[0, :, sl] = ((q[:, sl] * cq + q_rot[:, sl] * sq) * scale).astype(BF16)


def _proj_c(x, win, gq, gkv, wuq, wrot, tabs_q, tabs_k):
    nb, t, _ = x.shape
    tm = min(ROW_TILE, t)
    assert t % tm == 0
    tok = lambda width: pl.BlockSpec((1, tm, width), lambda b, i: (b, i, 0))
    tab = pl.BlockSpec((tm, LANES), lambda b, i: (i, 0))
    full = lambda a: pl.BlockSpec(a.shape, lambda b, i: (0, 0))
    return pl.pallas_call(
        _proj_c_kernel,
        grid=(nb, t // tm),
        in_specs=[tok(D_MODEL), full(win), full(gq), full(gkv), full(wuq), full(wrot)] + [tab] * 5,
        out_specs=[tok(H_C * LANES), tok(KV_RANK), tok(ROPE_DIM)],
        out_shape=[jax.ShapeDtypeStruct((nb, t, H_C * LANES), BF16),
                   jax.ShapeDtypeStruct((nb, t, KV_RANK), F32),
                   jax.ShapeDtypeStruct((nb, t, ROPE_DIM), F32)],
        compiler_params=_cparams(("parallel", "parallel")),
        name="proj_c",
    )(x, win, gq, gkv, wuq, wrot, *tabs_q, *tabs_k)


def _kv_up_kernel(ckv_ref, kr_ref, wk_ref, place_ref, wvt_ref, k_ref, vt_ref):
    cb = ckv_ref[...].astype(BF16)
    k = (jnp.dot(cb, wk_ref[...], preferred_element_type=F32)
         + jnp.dot(kr_ref[...].astype(BF16), place_ref[...], preferred_element_type=F32))
    k_ref[0] = k.astype(BF16)
    vt = lax.dot_general(wvt_ref[...], cb, (((1,), (1,)), ((), ())), preferred_element_type=F32)
    vt_ref[0] = vt.astype(BF16)


def _kv_up(ckv, kr, wk, place, wvt):
    n = ckv.shape[0]
    tm = ATTN_BLOCK
    assert n % tm == 0
    row = lambda width: pl.BlockSpec((tm, width), lambda i: (i, 0))
    full = lambda a: pl.BlockSpec(a.shape, lambda i: (0, 0))
    return pl.pallas_call(
        _kv_up_kernel,
        grid=(n // tm,),
        in_specs=[row(KV_RANK), row(ROPE_DIM), full(wk), full(place), full(wvt)],
        out_specs=[pl.BlockSpec((1, tm, H_C * LANES), lambda i: (i, 0, 0)),
                   pl.BlockSpec((1, H_C * V_DIM_C, tm), lambda i: (i, 0, 0))],
        out_shape=[jax.ShapeDtypeStruct((n // tm, tm, H_C * LANES), BF16),
                   jax.ShapeDtypeStruct((n // tm, H_C * V_DIM_C, tm), BF16)],
        compiler_params=_cparams(("parallel",)),
        name="kv_up",
    )(ckv, kr, wk, place, wvt)


def _mla_decode_kernel(q_ref, ckv_ref, kr_ref, ckvn_ref, krn_ref, wabs_ref, wv_ref, o_ref):
    q = q_ref[0]
    ts = q.shape[0]
    qs = jnp.concatenate(
        [jnp.dot(q[:, h * LANES:(h + 1) * LANES], wabs_ref[h], preferred_element_type=F32).astype(BF16)
         for h in range(H_C)], axis=0)
    kc = jnp.concatenate([ckv_ref[0].astype(BF16), kr_ref[0].astype(BF16)], axis=1)
    kn = jnp.concatenate([ckvn_ref[0].astype(BF16), krn_ref[0].astype(BF16)], axis=1)
    nt = (((1,), (1,)), ((), ()))
    sc = lax.dot_general(qs, kc, nt, preferred_element_type=F32)
    sn = lax.dot_general(qs, kn, nt, preferred_element_type=F32)
    m = jnp.maximum(jnp.max(sc, axis=1, keepdims=True), jnp.max(sn, axis=1, keepdims=True))
    pc = jnp.exp2(sc - m)
    pn = jnp.exp2(sn - m)
    l = jnp.sum(pc, axis=1, keepdims=True) + jnp.sum(pn, axis=1, keepdims=True)
    ol = (jnp.dot(pc.astype(BF16), kc[:, :KV_RANK], preferred_element_type=F32)
          + jnp.dot(pn.astype(BF16), kn[:, :KV_RANK], preferred_element_type=F32)) / l
    olb = ol.astype(BF16)
    o = jnp.dot(olb[:ts], wv_ref[0], preferred_element_type=F32)
    for h in range(1, H_C):
        o = o + jnp.dot(olb[h * ts:(h + 1) * ts], wv_ref[h], preferred_element_type=F32)
    o_ref[0] = o.astype(o_ref.dtype)


def _mla_decode(q, ckv_c, kr_c, ckv_n, kr_n, w_abs, w_vout):
    nb, ts, _ = q.shape
    assert ckv_c.shape[1] % CHUNK == 0 and ts <= CHUNK
    per_b = lambda a: pl.BlockSpec((1,) + a.shape[1:], lambda b: (b, 0, 0))
    full = lambda a: pl.BlockSpec(a.shape, lambda b: (0, 0, 0))
    return pl.pallas_call(
        _mla_decode_kernel,
        grid=(nb,),
        in_specs=[per_b(q), per_b(ckv_c), per_b(kr_c), per_b(ckv_n), per_b(kr_n), full(w_abs), full(w_vout)],
        out_specs=pl.BlockSpec((1, ts, H_C * V_DIM_C), lambda b: (b, 0, 0)),
        out_shape=jax.ShapeDtypeStruct((nb, ts, H_C * V_DIM_C), BF16),
        compiler_params=_cparams(("parallel",)),
        name="mla_decode",
    )(q, ckv_c, kr_c, ckv_n, kr_n, w_abs, w_vout)


def _rope_tables(pos, dim, lane0):
    half = dim // 2
    inv = ROPE_THETA ** (-jnp.arange(0, dim, 2, dtype=F32) / dim)
    ang = pos.astype(F32)[:, None] * inv[None, :]
    cos, sin = jnp.cos(ang), jnp.sin(ang)
    zero = jnp.zeros_like(sin)
    c = jnp.concatenate([cos, cos], axis=1)
    s1 = jnp.concatenate([-sin, zero], axis=1)
    s2 = jnp.concatenate([zero, sin], axis=1)
    if lane0 < 0:
        reps = LANES // dim
        return tuple(jnp.tile(a, (1, reps)) for a in (c, s1, s2))
    t = pos.shape[0]
    pad = lambda a, fill: jnp.concatenate(
        [jnp.full((t, lane0), fill, F32), a, jnp.full((t, LANES - lane0 - dim), fill, F32)], axis=1)
    return pad(c, 1.0), pad(s1, 0.0), pad(s2, 0.0)


def _pad_cols(a, width):
    return jnp.pad(a, ((0, 0), (0, width - a.shape[1])))


def _blocks(a, tk):
    nb, t, l = a.shape
    return a.reshape(nb, t // tk, tk, l)


def _cat_pad_time(cache, new, t_pad):
    nb, t0, l = cache.shape
    t1 = new.shape[1]
    return jnp.concatenate([cache, new, jnp.zeros((nb, t_pad - t0 - t1, l), cache.dtype)], axis=1)


def kernel(x_prompt, x_sample, cache_fox_k, cache_fox_v, cache_fox_logf, cache_diff_k, cache_diff_v, cache_mla_ckv, cache_mla_krope, w_in_ab, b_fgate, diff_lq1, diff_lk1, diff_lq2, diff_lk2, diff_subln, w_out_ab, w_in_c, mla_q_norm, mla_kv_norm, mla_w_uq, mla_w_ukv, w_out_c, ln1_g, ln1_b, ln2_g, ln2_b, moe_wg, moe_bg, moe_we, moe_be, moe_w1, moe_w3, moe_w2):
    bp, tp, d = x_prompt.shape
    bs, ts, _ = x_sample.shape
    past = cache_fox_k.shape[2]
    depth = ln1_g.shape[0]
    alpha = (2 * depth) ** 0.25
    tk = ATTN_BLOCK
    assert past % tk == 0
    ns = bs * ts
    t_dec = past + tk

    pos_p = jnp.arange(tp)
    pos_s = jnp.tile(past + jnp.arange(ts), bs)

    xp = x_prompt
    xs = x_sample.reshape(1, ns, d)
    out_ab_p, out_ab_s, out_c_p, out_c_s = [], [], [], []

    for i in range(depth):
        j = i // 2
        if i % 2 == 0:
            lam_init = 0.8 - 0.6 * math.exp(-0.3 * i)
            cuts = [0, A_WIDTH, 2 * A_WIDTH, 3 * A_WIDTH, 3 * A_WIDTH + H_A,
                    3 * A_WIDTH + H_A + B_QK_WIDTH, 3 * A_WIDTH + H_A + 2 * B_QK_WIDTH,
                    3 * A_WIDTH + H_A + 2 * B_QK_WIDTH + B_V_WIDTH]
            w = w_in_ab[j]
            piece = lambda a: w[:, cuts[a]:cuts[a + 1]]
            w6 = jnp.stack([piece(0), piece(1), piece(2), piece(4), piece(5), piece(6)]).astype(BF16)
            wvt = jnp.stack([piece(2).T, piece(6).T]).astype(BF16)
            wf = _pad_cols(piece(3), LANES).astype(BF16)
            bf = _pad_cols(b_fgate[j][None, :], LANES)
            wout = w_out_ab[j].astype(BF16)
            diff_extra = (diff_lq1[j][None, :], diff_lk1[j][None, :], diff_lq2[j][None, :],
                          diff_lk2[j][None, :], diff_subln[j][None, :])

            tabs = _rope_tables(pos_p, HEAD_DIM, -1)
            (qa, ka, kab, va, vat, lf, lfw, qb, kb, kbb, vb, vbt) = _proj_ab(xp, w6, wvt, wf, bf, tabs)
            bias = _blocks(_decay_bias(lfw), tk)
            oa = _attention("fox", qa, _blocks(kab, tk), vat, (bias,), n_pairs=H_A // 2, mask_shift=0)
            ob = _attention("diff", qb, _blocks(kbb, tk), vbt, diff_extra,
                            n_pairs=H_B, mask_shift=int(math.log2(CHUNK)), lam_init=lam_init)
            out_ab_p.append((ka.reshape(bp, tp, H_A, HEAD_DIM), va.reshape(bp, tp, H_A, HEAD_DIM), lf,
                             kb.reshape(bp, tp, H_B, 2, HEAD_DIM), vb.reshape(bp, tp, H_B, 2 * HEAD_DIM)))
            xp2 = _outproj_ln(xp.reshape(bp * tp, d), [oa.reshape(bp * tp, -1), ob.reshape(bp * tp, -1)],
                              [wout[:A_WIDTH], wout[A_WIDTH:]], ln1_g[i][None, :], ln1_b[i][None, :], alpha)

            tabs = _rope_tables(pos_s, HEAD_DIM, -1)
            (qa, ka, kab, va, _, lf, lfw, qb, kb, kbb, vb, _) = _proj_ab(xs, w6, wvt, wf, bf, tabs)
            rs = lambda a: a.reshape(bs, ts, a.shape[-1])
            cache_lfw = jnp.pad(cache_fox_logf[j].astype(F32), ((0, 0), (0, 0), (0, LANES - H_A)))
            bias = _decay_bias(_cat_pad_time(cache_lfw, rs(lfw), t_dec))
            flat = lambda c: c.reshape(bs, past, -1)
            oa = _decode_attention("fox", rs(qa), flat(cache_fox_k[j]), flat(cache_fox_v[j]), rs(kab), rs(va),
                                   (bias,))
            ob = _decode_attention("diff", rs(qb), flat(cache_diff_k[j]), flat(cache_diff_v[j]), rs(kbb),
                                   rs(vb), diff_extra, lam_init=lam_init)
            out_ab_s.append((ka.reshape(bs, ts, H_A, HEAD_DIM), va.reshape(bs, ts, H_A, HEAD_DIM),
                             lf.reshape(bs, ts, H_A), kb.reshape(bs, ts, H_B, 2, HEAD_DIM),
                             vb.reshape(bs, ts, H_B, 2 * HEAD_DIM)))
            xs2 = _outproj_ln(xs.reshape(ns, d), [oa.reshape(ns, -1), ob.reshape(ns, -1)],
                              [wout[:A_WIDTH], wout[A_WIDTH:]], ln1_g[i][None, :], ln1_b[i][None, :], alpha)
        else:
            wc = w_in_c[j]
            kr_cols = _pad_cols(wc[:, Q_RANK + KV_RANK:], LANES)
            win = jnp.concatenate([wc[:, :Q_RANK + KV_RANK], kr_cols], axis=1).astype(BF16)
            wuq3 = jnp.pad(mla_w_uq[j].reshape(Q_RANK, H_C, NOPE_DIM + ROPE_DIM),
                           ((0, 0), (0, 0), (0, LANES - NOPE_DIM - ROPE_DIM)))
            wuq = wuq3.reshape(Q_RANK, H_C * LANES).astype(BF16)
            r0, r1, r2 = NOPE_DIM, NOPE_DIM + ROPE_DIM // 2, NOPE_DIM + ROPE_DIM
            wrot = jnp.zeros_like(wuq3).at[:, :, r0:r1].set(-wuq3[:, :, r1:r2]).at[:, :, r1:r2].set(wuq3[:, :, r0:r1])
            wrot = wrot.reshape(Q_RANK, H_C * LANES).astype(BF16)
            q_tabs = lambda pos: (lambda c, s1, s2: (c, s2 - s1))(*_rope_tables(pos, ROPE_DIM, NOPE_DIM))
            wukv = mla_w_ukv[j].reshape(KV_RANK, H_C, NOPE_DIM + V_DIM_C)
            wk = jnp.pad(wukv[:, :, :NOPE_DIM], ((0, 0), (0, 0), (0, LANES - NOPE_DIM)))
            wk = wk.reshape(KV_RANK, H_C * LANES).astype(BF16)
            wvt = wukv[:, :, NOPE_DIM:].reshape(KV_RANK, H_C * V_DIM_C).T.astype(BF16)
            place = jnp.tile(_pad_cols(jnp.concatenate(
                [jnp.zeros((ROPE_DIM, NOPE_DIM), F32), jnp.eye(ROPE_DIM, dtype=F32)], axis=1), LANES),
                (1, H_C)).astype(BF16)
            gq = mla_q_norm[j][None, :]
            gkv = mla_kv_norm[j][None, :]
            wout = w_out_c[j].astype(BF16)

            q, ckv, kr = _proj_c(xp, win, gq, gkv, wuq, wrot, q_tabs(pos_p), _rope_tables(pos_p, ROPE_DIM, 0))
            kc, vct = _kv_up(ckv.reshape(bp * tp, KV_RANK), kr.reshape(bp * tp, ROPE_DIM), wk, place, wvt)
            per_seq = lambda a, nb: a.reshape((nb, a.shape[0] // nb) + a.shape[1:])
            oc = _attention("mla", q, per_seq(kc, bp), per_seq(vct, bp), (), n_pairs=H_C // 2,
                            mask_shift=int(math.log2(CHUNK)))
            out_c_p.append((ckv, kr))
            xp2 = _outproj_ln(xp.reshape(bp * tp, d), [oc.reshape(bp * tp, -1)], [wout],
                              ln1_g[i][None, :], ln1_b[i][None, :], alpha)

            q, ckv, kr = _proj_c(xs, win, gq, gkv, wuq, wrot, q_tabs(pos_s), _rope_tables(pos_s, ROPE_DIM, 0))
            w_abs = jnp.zeros((H_C, LANES, 2 * LANES), F32)
            w_abs = w_abs.at[:, :NOPE_DIM, :KV_RANK].set(jnp.transpose(wukv[:, :, :NOPE_DIM], (1, 2, 0)))
            w_abs = w_abs.at[:, NOPE_DIM:NOPE_DIM + ROPE_DIM, KV_RANK:KV_RANK + ROPE_DIM].set(
                jnp.eye(ROPE_DIM, dtype=F32))
            w_vout = jnp.einsum("khd,hg->hkgd", wukv[:, :, NOPE_DIM:], jnp.eye(H_C, dtype=F32))
            w_vout = w_vout.reshape(H_C, KV_RANK, H_C * V_DIM_C)
            wide = lambda a: jnp.pad(a.astype(F32), ((0, 0), (0, 0), (0, LANES - ROPE_DIM)))
            oc = _mla_decode(q.reshape(bs, ts, -1), cache_mla_ckv[j].astype(F32), wide(cache_mla_krope[j]),
                             ckv.reshape(bs, ts, KV_RANK), wide(kr.reshape(bs, ts, ROPE_DIM)),
                             w_abs.astype(BF16), w_vout.astype(BF16))
            out_c_s.append((ckv.reshape(bs, ts, KV_RANK), kr.reshape(bs, ts, ROPE_DIM)))
            xs2 = _outproj_ln(xs.reshape(ns, d), [oc.reshape(ns, -1)], [wout],
                              ln1_g[i][None, :], ln1_b[i][None, :], alpha)

        wr = _pad_cols(jnp.concatenate(
            [moe_wg[i]] + [moe_we[i][gi] for gi in range(N_GROUPS)], axis=1), LANES)
        wrh = wr.astype(BF16)
        wrl = (wr - wrh.astype(F32)).astype(BF16)
        br = _pad_cols(jnp.concatenate([moe_bg[i], moe_be[i].reshape(-1)])[None, :], LANES)
        moe_w = (moe_w1[i], moe_w3[i], moe_w2[i])
        g2, b2 = ln2_g[i][None, :], ln2_b[i][None, :]
        xp = _moe_ln(xp2, wrh, wrl, br, *moe_w, g2, b2, alpha).reshape(bp, tp, d)
        xs = _moe_ln(xs2, wrh, wrl, br, *moe_w, g2, b2, alpha).reshape(1, ns, d)

    stack = lambda rows, n: jnp.stack([r[n] for r in rows])
    return (xp, xs.reshape(bs, ts, d),
            stack(out_ab_p, 0), stack(out_ab_p, 1), stack(out_ab_p, 2), stack(out_ab_p, 3), stack(out_ab_p, 4),
            stack(out_c_p, 0), stack(out_c_p, 1),
            stack(out_ab_s, 0), stack(out_ab_s, 1), stack(out_ab_s, 2), stack(out_ab_s, 3), stack(out_ab_s, 4),
            stack(out_c_s, 0), stack(out_c_s, 1))
```

```python
import functools
import math

import jax
import jax.numpy as jnp
from jax import lax
from jax.experimental import pallas as pl
from jax.experimental.pallas import tpu as pltpu

F32 = jnp.float32
BF16 = jnp.bfloat16

D_MODEL = 1024
CHUNK = 64
HEAD_DIM = 64
ROPE_THETA = 10000.0
H_A = 8
H_B = 4
H_C = 16
Q_RANK = 256
KV_RANK = 128
NOPE_DIM = 64
ROPE_DIM = 32
V_DIM_C = 64
N_GROUPS = 4
EXPERTS_PER_GROUP = 4
N_EXPERTS = N_GROUPS * EXPERTS_PER_GROUP
D_EXPERT = 256
A_WIDTH = H_A * HEAD_DIM
B_QK_WIDTH = H_B * 2 * HEAD_DIM
B_V_WIDTH = H_B * 2 * HEAD_DIM
FGATE_BIAS = 3.0
LN_EPS = 1e-5
RMS_EPS = 1e-6
NEG_INF = -1e30
LOG2E = math.log2(math.e)

LANES = 128
BF16_ROWS = 16
PANEL = {"fox": 1024, "diff": 256, "mla": 1024}
BIAS_PIECES = 3
VMEM_LIMIT = 48 * 1024 * 1024
ATTN_VMEM_LIMIT = 56 * 1024 * 1024
ATTN_BLOCK = 512
Q_KEY_BLOCKS = 4
ROW_TILE = 512
MOE_TILE = 1024
EXPERTS_PER_STEP = 4
MOE_VMEM_LIMIT = 60 * 1024 * 1024
BIAS_ROWS_PER_STEP = 4096
GATE_LANE0 = N_GROUPS


def _cparams(sem, vmem_limit=VMEM_LIMIT):
    return pltpu.CompilerParams(dimension_semantics=sem, vmem_limit_bytes=vmem_limit)


def _rope3(x, c, s1, s2, shift_up, shift_down):
    return x * c + pltpu.roll(x, shift_up, 1) * s1 + pltpu.roll(x, shift_down, 1) * s2


def _layer_norm(y, g, b):
    mu = jnp.mean(y, axis=-1, keepdims=True)
    d = y - mu
    var = jnp.mean(d * d, axis=-1, keepdims=True)
    return d * lax.rsqrt(var + LN_EPS) * g + b


def _split3(x):
    hi = x.astype(BF16)
    r1 = x - hi.astype(F32)
    mid = r1.astype(BF16)
    return hi, mid, (r1 - mid.astype(F32)).astype(BF16)


def _proj_ab_kernel(x_ref, w_ref, wvt_ref, wf_ref, bf_ref, c_ref, s1_ref, s2_ref,
                    qa_ref, ka_ref, kab_ref, va_ref, vat_ref, lf_ref, lfw_ref,
                    qb_ref, kb_ref, kbb_ref, vb_ref, vbt_ref):
    xb = x_ref[0].astype(BF16)

    def mm(i):
        return jnp.dot(xb, w_ref[i], preferred_element_type=F32)

    def mm_t(i):
        return lax.dot_general(wvt_ref[i], xb, (((1,), (1,)), ((), ())), preferred_element_type=F32)

    qa_ref[0] = (mm(0) * (HEAD_DIM ** -0.5 * LOG2E)).astype(BF16)
    ka = mm(1)
    ka_ref[0] = ka
    kab_ref[0] = ka.astype(BF16)
    va_ref[0] = mm(2)
    vat_ref[0, 0] = mm_t(0).astype(BF16)

    z = jnp.dot(xb, wf_ref[...], preferred_element_type=F32) + bf_ref[...]
    lf = jnp.minimum(z, 0.0) - jnp.log1p(jnp.exp(-jnp.abs(z)))
    lf_ref[0] = lf[:, :H_A]
    lfw_ref[0] = lf

    c, s1, s2 = c_ref[...], s1_ref[...], s2_ref[...]
    qb = mm(3)
    kb = mm(4)
    for s in range(B_QK_WIDTH // LANES):
        sl = slice(s * LANES, (s + 1) * LANES)
        qs = _rope3(qb[:, sl], c, s1, s2, LANES - HEAD_DIM // 2, HEAD_DIM // 2)
        qb_ref[0, :, sl] = (qs * (HEAD_DIM ** -0.5 * LOG2E)).astype(BF16)
        ks = _rope3(kb[:, sl], c, s1, s2, LANES - HEAD_DIM // 2, HEAD_DIM // 2)
        kb_ref[0, :, sl] = ks
        kbb_ref[0, :, sl] = ks.astype(BF16)
    vb_ref[0] = mm(5)
    vbt_ref[0, 0] = mm_t(1).astype(BF16)


def _proj_ab(x, w6, wvt, wf, bf, tabs):
    nb, t, _ = x.shape
    tm = min(ROW_TILE, t)
    assert t % tm == 0
    w = A_WIDTH
    tok = lambda width: pl.BlockSpec((1, tm, width), lambda b, i: (b, i, 0))
    tr = pl.BlockSpec((1, 1, w, tm), lambda b, i: (b, i, 0, 0))
    tab = pl.BlockSpec((tm, LANES), lambda b, i: (i, 0))
    full = lambda a: pl.BlockSpec(a.shape, lambda b, i: (0,) * a.ndim)
    sds = lambda width, dt: jax.ShapeDtypeStruct((nb, t, width), dt)
    sds_t = jax.ShapeDtypeStruct((nb, t // tm, w, tm), BF16)
    return pl.pallas_call(
        _proj_ab_kernel,
        grid=(nb, t // tm),
        in_specs=[tok(D_MODEL), full(w6), full(wvt), full(wf), full(bf), tab, tab, tab],
        out_specs=[tok(w), tok(w), tok(w), tok(w), tr, tok(H_A), tok(LANES), tok(w), tok(w), tok(w), tok(w), tr],
        out_shape=[sds(w, BF16), sds(w, F32), sds(w, BF16), sds(w, F32), sds_t, sds(H_A, F32), sds(LANES, F32),
                   sds(w, BF16), sds(w, F32), sds(w, BF16), sds(w, F32), sds_t],
        compiler_params=_cparams(("parallel", "parallel")),
        name="proj_ab",
    )(x, w6, wvt, wf, bf, *tabs)


def _decay_bias_kernel(lf_ref, spread_ref, lower_ref, o_ref, carry_ref):
    @pl.when(pl.program_id(1) == 0)
    def _():
        carry_ref[...] = jnp.zeros_like(carry_ref)

    spread = spread_ref[...]
    lower = lower_ref[...]
    tc = lower.shape[0]
    lane = lax.broadcasted_iota(jnp.int32, (1, LANES), 1).astype(F32)
    piece = lane - BIAS_PIECES * jnp.floor((lane + 0.5) * (1.0 / BIAS_PIECES))
    carry = carry_ref[...]
    for r in range(lf_ref.shape[1] // tc):
        rows = slice(r * tc, (r + 1) * tc)
        x = lf_ref[0, rows, :]
        xr = sum(jnp.dot(p, spread, preferred_element_type=F32) for p in _split3(x))
        c = sum(jnp.dot(lower, p, preferred_element_type=F32) for p in _split3(xr)) + carry
        carry = c[tc - 1:tc, :]
        hi, mid, lo = (p.astype(F32) for p in _split3(c * (-LOG2E)))
        o_ref[0, rows, :] = jnp.where(piece == 0.0, hi, jnp.where(piece == 1.0, mid, lo)).astype(BF16)
    carry_ref[...] = carry


def _decay_bias(lf_wide):
    nb, t, _ = lf_wide.shape
    tc = min(ATTN_BLOCK, t)
    tb = min(BIAS_ROWS_PER_STEP, t)
    assert t % tb == 0 and tb % tc == 0
    spec = pl.BlockSpec((1, tb, LANES), lambda b, i: (b, i, 0))
    src = jnp.arange(LANES)[:, None]
    dst = jnp.arange(LANES)[None, :]
    spread = ((dst // BIAS_PIECES == src) & (src < H_A)).astype(BF16)
    lower = jnp.tril(jnp.ones((tc, tc), BF16))
    const = lambda a: pl.BlockSpec(a.shape, lambda b, i: (0, 0))
    return pl.pallas_call(
        _decay_bias_kernel,
        grid=(nb, t // tb),
        in_specs=[spec, const(spread), const(lower)],
        out_specs=spec,
        out_shape=jax.ShapeDtypeStruct((nb, t, LANES), BF16),
        scratch_shapes=[pltpu.VMEM((1, LANES), F32)],
        compiler_params=_cparams(("parallel", "arbitrary")),
        name="cumsum",
    )(lf_wide, spread, lower)


def _attn_kernel(*refs, mode, tq, tk, mask_shift, lam_init):
    if mode == "diff":
        q_ref, k_ref, vt_ref, lq1_ref, lk1_ref, lq2_ref, lk2_ref, sub_ref, o_ref = refs[:9]
    elif mode == "fox":
        q_ref, k_ref, vt_ref, b_ref, o_ref = refs[:5]
    else:
        q_ref, k_ref, vt_ref, o_ref = refs[:4]
    m_sc, acc_sc, sa, bma, sb, bmb = refs[-6:]
    v_rows = LANES if mode == "diff" else HEAD_DIM
    sa_sc, sb_sc = (sa, bma), (sb, bmb)

    qi = pl.program_id(2)
    q = q_ref[0]
    lane = lax.broadcasted_iota(jnp.int32, (1, LANES), 1)
    if mode == "mla":
        qs = [q[:, :LANES], q[:, LANES:]]
    else:
        zero = jnp.zeros_like(q)
        qs = [jnp.where(lane < HEAD_DIM, q, zero), jnp.where(lane >= HEAD_DIM, q, zero)]
        if mode == "fox":
            def pick(i):
                lo = BIAS_PIECES * (2 * pl.program_id(1) + i)
                hot = jnp.where((lane >= lo) & (lane < lo + BIAS_PIECES), 1.0, 0.0)
                return jnp.broadcast_to(hot, (tq, LANES)).astype(BF16)

            qs = [jnp.concatenate([qs[i], pick(i)], axis=1) for i in range(2)]

    m_sc[...] = jnp.full(m_sc.shape, NEG_INF, F32)
    acc_sc[...] = jnp.zeros(acc_sc.shape, F32)

    pw = min(PANEL[mode], tq)

    def scores(j, bufs, q0, q1):
        s_sc, bm_sc = bufs
        cs = slice(q0, q1)
        k = k_ref[0, j]
        if mode == "fox":
            k = jnp.concatenate([k, b_ref[0, j]], axis=1)
        for i in range(2):
            ki = k[:, i * LANES:(i + 1) * LANES] if mode == "mla" else k
            st = lax.dot_general(ki, qs[i][cs], (((1,), (1,)), ((), ())), preferred_element_type=F32)
            s_sc[i, :, cs] = st
            bm_sc[i, :, cs] = jnp.max(st, axis=0, keepdims=True)

    def consume(j, bufs, q0, q1, key0=None):
        s_sc, bm_sc = bufs
        cs = slice(q0, q1)
        vt = vt_ref[0, j]
        masked = key0 is not None and ((key0 + tk - 1) >> mask_shift) > (q0 >> mask_shift)
        for i in range(2):
            st = s_sc[i, :, cs]
            if masked:
                key = lax.broadcasted_iota(jnp.int32, (tk, q1 - q0), 0) + key0
                qry = lax.broadcasted_iota(jnp.int32, (tk, q1 - q0), 1) + q0
                vis = lax.shift_right_logical(key, mask_shift) <= lax.shift_right_logical(qry, mask_shift)
                st = jnp.where(vis, st, NEG_INF)
                blk_max = jnp.max(st, axis=0, keepdims=True)
            else:
                blk_max = bm_sc[i, :, cs]
            m_prev = m_sc[i, :, cs]
            m_new = jnp.maximum(m_prev, blk_max)
            alpha = jnp.exp2(m_prev - m_new)
            p = jnp.exp2(st - m_new).astype(BF16)
            vi = vt if mode == "diff" else vt[i * HEAD_DIM:(i + 1) * HEAD_DIM]
            vi = jnp.concatenate([vi, jnp.ones((BF16_ROWS, tk), BF16)], axis=0)
            acc_sc[i, :, cs] = alpha * acc_sc[i, :, cs] + jnp.dot(vi, p, preferred_element_type=F32)
            m_sc[i, :, cs] = m_new

    def stage(nxt, cur):
        for q0 in range(0, tq, pw):
            scores(nxt[0], nxt[1], q0, q0 + pw)
            consume(cur[0], cur[1], q0, q0 + pw)

    n_diag = tq // tk
    nfull = n_diag * qi
    bufs = (sa_sc, sb_sc)

    def trip(jj, carry):
        for d in range(n_diag):
            j = n_diag * jj + d
            stage((j + 1, bufs[(d + 1) % 2]), (j, bufs[d % 2]))
        return carry

    for q0 in range(0, tq, pw):
        scores(0, sa_sc, q0, q0 + pw)
    lax.fori_loop(0, qi, trip, 0)

    pt = min(pw, tk)
    sees = lambda d, q1: ((d * tk) >> mask_shift) <= ((q1 - 1) >> mask_shift)
    for d in range(n_diag):
        for q0 in range(0, tq, pt):
            if d + 1 < n_diag and sees(d + 1, q0 + pt):
                scores(nfull + d + 1, bufs[(d + 1) % 2], q0, q0 + pt)
            if sees(d, q0 + pt):
                consume(nfull + d, bufs[d % 2], q0, q0 + pt, key0=d * tk)

    if mode == "diff":
        lam = (jnp.exp(jnp.sum(lq1_ref[...] * lk1_ref[...], axis=1, keepdims=True))
               - jnp.exp(jnp.sum(lq2_ref[...] * lk2_ref[...], axis=1, keepdims=True)) + lam_init)
    for q0 in range(0, tq, pw):
        cs = slice(q0, q0 + pw)
        o0 = acc_sc[0, :v_rows, cs] / acc_sc[0, v_rows:v_rows + 1, cs]
        o1 = acc_sc[1, :v_rows, cs] / acc_sc[1, v_rows:v_rows + 1, cs]
        if mode == "diff":
            o = o0 - lam * o1
            ms = jnp.mean(o * o, axis=0, keepdims=True)
            o = (o * lax.rsqrt(ms + RMS_EPS)).T * sub_ref[...] * (1.0 - lam_init)
        else:
            o = jnp.concatenate([o0, o1], axis=0).T
        o_ref[0, cs, :] = o.astype(o_ref.dtype)


def _attention(mode, q, k, vt, extra, *, n_pairs, mask_shift, lam_init=0.0):
    nb, t_q, _ = q.shape
    _, nkb, tk, _ = k.shape
    tq = Q_KEY_BLOCKS * tk
    assert t_q % tq == 0 and nkb == t_q // tk
    qw = 2 * LANES if mode == "mla" else LANES
    in_specs = [
        pl.BlockSpec((1, tq, qw), lambda b, p, i: (b, i, p)),
        pl.BlockSpec((1, nkb, tk, qw), lambda b, p, i: (b, 0, 0, p)),
        pl.BlockSpec((1, nkb, LANES, tk), lambda b, p, i: (b, 0, p, 0)),
    ]
    if mode == "fox":
        in_specs.append(pl.BlockSpec((1, nkb, tk, LANES), lambda b, p, i: (b, 0, 0, 0)))
    elif mode == "diff":
        in_specs += [pl.BlockSpec(a.shape, lambda b, p, i: (0, 0)) for a in extra]
    kern = functools.partial(_attn_kernel, mode=mode, tq=tq, tk=tk, mask_shift=mask_shift, lam_init=lam_init)
    return pl.pallas_call(
        kern,
        grid=(nb, n_pairs, t_q // tq),
        in_specs=in_specs,
        out_specs=pl.BlockSpec((1, tq, LANES), lambda b, p, i: (b, i, p)),
        out_shape=jax.ShapeDtypeStruct((nb, t_q, n_pairs * LANES), BF16),
        scratch_shapes=[pltpu.VMEM((2, 1, tq), F32),
                        pltpu.VMEM((2, (LANES if mode == "diff" else HEAD_DIM) + BF16_ROWS, tq), F32),
                        pltpu.VMEM((2, tk, tq), F32), pltpu.VMEM((2, 1, tq), F32),
                        pltpu.VMEM((2, tk, tq), F32), pltpu.VMEM((2, 1, tq), F32)],
        compiler_params=_cparams(("parallel", "parallel", "arbitrary"), ATTN_VMEM_LIMIT),
        name="attn_" + mode,
    )(q, k, vt, *extra)


def _decode_attn_kernel(*refs, mode, lam_init):
    if mode == "fox":
        q_ref, kc_ref, vc_ref, kn_ref, vn_ref, b_ref, o_ref = refs
    else:
        q_ref, kc_ref, vc_ref, kn_ref, vn_ref, lq1_ref, lk1_ref, lq2_ref, lk2_ref, sub_ref, o_ref = refs
        lam = (jnp.exp(jnp.sum(lq1_ref[...] * lk1_ref[...], axis=1, keepdims=True))
               - jnp.exp(jnp.sum(lq2_ref[...] * lk2_ref[...], axis=1, keepdims=True)) + lam_init)
    ts = q_ref.shape[1]
    past = kc_ref.shape[1]
    lane = lax.broadcasted_iota(jnp.int32, (1, LANES), 1)
    row = lax.broadcasted_iota(jnp.int32, (ts, ts), 0)
    col = lax.broadcasted_iota(jnp.int32, (ts, ts), 1)
    nt = (((1,), (1,)), ((), ()))
    for p in range(q_ref.shape[2] // LANES):
        sl = slice(p * LANES, (p + 1) * LANES)
        q = q_ref[0, :, sl]
        kc = kc_ref[0, :, sl].astype(BF16)
        kn = kn_ref[0, :, sl]
        vc = vc_ref[0, :, sl].astype(BF16)
        vn = vn_ref[0, :, sl].astype(BF16)
        if mode == "fox":
            kc = jnp.concatenate([kc, b_ref[0, :past, :]], axis=1)
            kn = jnp.concatenate([kn, b_ref[0, past:past + ts, :]], axis=1)
        zero = jnp.zeros_like(q)
        outs = []
        for i in range(2):
            qi = jnp.where(lane < HEAD_DIM, q, zero) if i == 0 else jnp.where(lane >= HEAD_DIM, q, zero)
            if mode == "fox":
                lo = BIAS_PIECES * (2 * p + i)
                hot = jnp.where((lane >= lo) & (lane < lo + BIAS_PIECES), 1.0, 0.0)
                qi = jnp.concatenate([qi, jnp.broadcast_to(hot, (ts, LANES)).astype(BF16)], axis=1)
            sc = lax.dot_general(qi, kc, nt, preferred_element_type=F32)
            sn = lax.dot_general(qi, kn, nt, preferred_element_type=F32)
            if mode == "fox":
                sn = jnp.where(col <= row, sn, NEG_INF)
            m = jnp.maximum(jnp.max(sc, axis=1, keepdims=True), jnp.max(sn, axis=1, keepdims=True))
            pc = jnp.exp2(sc - m)
            pn = jnp.exp2(sn - m)
            l = jnp.sum(pc, axis=1, keepdims=True) + jnp.sum(pn, axis=1, keepdims=True)
            outs.append((jnp.dot(pc.astype(BF16), vc, preferred_element_type=F32)
                         + jnp.dot(pn.astype(BF16), vn, preferred_element_type=F32)) / l)
        if mode == "fox":
            o = jnp.where(lane < HEAD_DIM, outs[0], outs[1])
        else:
            o = outs[0] - lam * outs[1]
            ms = jnp.mean(o * o, axis=1, keepdims=True)
            o = o * lax.rsqrt(ms + RMS_EPS) * sub_ref[...] * (1.0 - lam_init)
        o_ref[0, :, sl] = o.astype(o_ref.dtype)


def _decode_attention(mode, q, k_cache, v_cache, k_new, v_new, extra, lam_init=0.0):
    nb, ts, w = q.shape
    assert k_cache.shape[1] % CHUNK == 0 and ts <= CHUNK
    per_b = lambda a: pl.BlockSpec((1,) + a.shape[1:], lambda b: (b, 0, 0))
    if mode == "fox":
        extra_specs = [per_b(extra[0])]
    else:
        extra_specs = [pl.BlockSpec(a.shape, lambda b: (0, 0)) for a in extra]
    return pl.pallas_call(
        functools.partial(_decode_attn_kernel, mode=mode, lam_init=lam_init),
        grid=(nb,),
        in_specs=[per_b(q), per_b(k_cache), per_b(v_cache), per_b(k_new), per_b(v_new)] + extra_specs,
        out_specs=pl.BlockSpec((1, ts, w), lambda b: (b, 0, 0)),
        out_shape=jax.ShapeDtypeStruct((nb, ts, w), BF16),
        compiler_params=_cparams(("parallel",)),
        name="decode_" + mode,
    )(q, k_cache, v_cache, k_new, v_new, *extra)


def _outproj_ln_kernel(*refs, n_in, alpha):
    x_ref = refs[0]
    o_refs = refs[1:1 + n_in]
    w_refs = refs[1 + n_in:1 + 2 * n_in]
    g_ref, b_ref, y_ref = refs[1 + 2 * n_in:]
    mix = jnp.dot(o_refs[0][...], w_refs[0][...], preferred_element_type=F32)
    for o_r, w_r in zip(o_refs[1:], w_refs[1:]):
        mix = mix + jnp.dot(o_r[...], w_r[...], preferred_element_type=F32)
    y_ref[...] = _layer_norm(alpha * x_ref[...] + mix, g_ref[...], b_ref[...])


def _outproj_ln(x, outs, ws, g, b, alpha):
    n, d = x.shape
    tm = min(ROW_TILE, n)
    assert n % tm == 0
    row = lambda width: pl.BlockSpec((tm, width), lambda i: (i, 0))
    full = lambda a: pl.BlockSpec(a.shape, lambda i: (0, 0))
    return pl.pallas_call(
        functools.partial(_outproj_ln_kernel, n_in=len(outs), alpha=alpha),
        grid=(n // tm,),
        in_specs=[row(d)] + [row(o.shape[1]) for o in outs] + [full(w) for w in ws] + [full(g), full(b)],
        out_specs=row(d),
        out_shape=jax.ShapeDtypeStruct((n, d), F32),
        compiler_params=_cparams(("parallel",)),
        name="outproj_ln",
    )(x, *outs, *ws, g, b)


def _route(logits):
    lane = lax.broadcasted_iota(jnp.int32, logits.shape, 1).astype(F32)
    big = float(1 << 20)
    is_g = lane < N_GROUPS
    lg = jnp.where(is_g, logits, NEG_INF)
    eg = jnp.where(is_g, jnp.exp(lg - jnp.max(lg, axis=1, keepdims=True)), 0.0)
    pg = eg / jnp.sum(eg, axis=1, keepdims=True)
    p_g = jnp.max(pg, axis=1, keepdims=True)
    gidx = jnp.min(jnp.where(is_g & (pg == p_g), lane, big), axis=1, keepdims=True)
    lo = GATE_LANE0 + EXPERTS_PER_GROUP * gidx
    sel = (lane >= lo) & (lane < lo + EXPERTS_PER_GROUP)
    le = jnp.where(sel, logits, NEG_INF)
    ee = jnp.where(sel, jnp.exp(le - jnp.max(le, axis=1, keepdims=True)), 0.0)
    pe = ee / jnp.sum(ee, axis=1, keepdims=True)
    v1 = jnp.max(jnp.where(sel, pe, -1.0), axis=1, keepdims=True)
    i1 = jnp.min(jnp.where(sel & (pe == v1), lane, big), axis=1, keepdims=True)
    rest = sel & (lane != i1)
    v2 = jnp.max(jnp.where(rest, pe, -1.0), axis=1, keepdims=True)
    i2 = jnp.min(jnp.where(rest & (pe == v2), lane, big), axis=1, keepdims=True)
    tot = v1 + v2
    w1 = v1 / tot * p_g
    w2 = v2 / tot * p_g
    return jnp.where(lane == i1, w1, jnp.where(lane == i2, w2, 0.0))


def _moe_ln_kernel(x_ref, wrh_ref, wrl_ref, br_ref, w1_ref, w3_ref, w2_ref, g_ref, b_ref, y_ref,
                   xb_sc, gate_sc, acc_sc, *, alpha):
    e = pl.program_id(1)

    @pl.when(e == 0)
    def _():
        x = x_ref[...]
        xh = x.astype(BF16)
        xl = (x - xh.astype(F32)).astype(BF16)
        xb_sc[...] = xh
        logits = (jnp.dot(xh, wrh_ref[...], preferred_element_type=F32)
                  + jnp.dot(xl, wrh_ref[...], preferred_element_type=F32)
                  + jnp.dot(xh, wrl_ref[...], preferred_element_type=F32) + br_ref[...])
        gate_sc[...] = _route(logits)
        acc_sc[...] = jnp.zeros_like(acc_sc)

    xb = xb_sc[...]
    lane = lax.broadcasted_iota(jnp.int32, (1, LANES), 1)
    for s in range(w1_ref.shape[0]):
        h1 = jnp.dot(xb, w1_ref[s].astype(BF16), preferred_element_type=F32)
        h3 = jnp.dot(xb, w3_ref[s].astype(BF16), preferred_element_type=F32)
        hdn = (h1 * jax.nn.sigmoid(h1)) * h3
        y = jnp.dot(hdn.astype(BF16), w2_ref[s].astype(BF16), preferred_element_type=F32)
        expert_lane = e * w1_ref.shape[0] + s + GATE_LANE0
        ge = jnp.sum(jnp.where(lane == expert_lane, gate_sc[...], 0.0), axis=1, keepdims=True)
        acc_sc[...] += ge * y

    @pl.when(e == pl.num_programs(1) - 1)
    def _():
        y_ref[...] = _layer_norm(alpha * x_ref[...] + acc_sc[...], g_ref[...], b_ref[...])


def _moe_ln(x, wrh, wrl, br, w1, w3, w2, g, b, alpha):
    n, d = x.shape
    tm = min(MOE_TILE, n)
    assert n % tm == 0
    ne = w1.shape[0]
    assert ne % EXPERTS_PER_STEP == 0
    per_expert = lambda a: pl.BlockSpec((EXPERTS_PER_STEP,) + a.shape[1:], lambda i, e: (e, 0, 0))
    full = lambda a: pl.BlockSpec(a.shape, lambda i, e: (0, 0))
    return pl.pallas_call(
        functools.partial(_moe_ln_kernel, alpha=alpha),
        grid=(n // tm, ne // EXPERTS_PER_STEP),
        in_specs=[pl.BlockSpec((tm, d), lambda i, e: (i, 0)), full(wrh), full(wrl), full(br),
                  per_expert(w1), per_expert(w3), per_expert(w2),
                  full(g), full(b)],
        out_specs=pl.BlockSpec((tm, d), lambda i, e: (i, 0)),
        out_shape=jax.ShapeDtypeStruct((n, d), F32),
        scratch_shapes=[pltpu.VMEM((tm, d), BF16), pltpu.VMEM((tm, LANES), F32), pltpu.VMEM((tm, d), F32)],
        compiler_params=_cparams(("parallel", "arbitrary"), MOE_VMEM_LIMIT),
        name="moe_ln",
    )(x, wrh, wrl, br, w1, w3, w2, g, b)


def _proj_c_kernel(x_ref, win_ref, gq_ref, gkv_ref, wuq_ref, wrot_ref, cq_ref, sq_ref,
                   ck_ref, s1k_ref, s2k_ref, q_ref, ckv_ref, kr_ref):
    xb = x_ref[0].astype(BF16)
    h = jnp.dot(xb, win_ref[...], preferred_element_type=F32)
    qa = h[:, :Q_RANK]
    kva = h[:, Q_RANK:Q_RANK + KV_RANK]
    krw = h[:, Q_RANK + KV_RANK:]
    qn = qa * lax.rsqrt(jnp.mean(qa * qa, axis=1, keepdims=True) + RMS_EPS) * gq_ref[...]
    ckv_ref[0] = kva * lax.rsqrt(jnp.mean(kva * kva, axis=1, keepdims=True) + RMS_EPS) * gkv_ref[...]
    half = ROPE_DIM // 2
    kr = _rope3(krw, ck_ref[...], s1k_ref[...], s2k_ref[...], LANES - half, half)
    kr_ref[0] = kr[:, :ROPE_DIM]
    qnb = qn.astype(BF16)
    q = jnp.dot(qnb, wuq_ref[...], preferred_element_type=F32)
    q_rot = jnp.dot(qnb, wrot_ref[...], preferred_element_type=F32)
    cq, sq = cq_ref[...], sq_ref[...]
    scale = (NOPE_DIM + ROPE_DIM) ** -0.5 * LOG2E
    for hd in range(H_C):
        sl = slice(hd * LANES, (hd + 1) * LANES)
        q_ref[0, :, sl] = ((q[:, sl] * cq + q_rot[:, sl] * sq) * scale).astype(BF16)


def _proj_c(x, win, gq, gkv, wuq, wrot, tabs_q, tabs_k):
    nb, t, _ = x.shape
    tm = min(ROW_TILE, t)
    assert t % tm == 0
    tok = lambda width: pl.BlockSpec((1, tm, width), lambda b, i: (b, i, 0))
    tab = pl.BlockSpec((tm, LANES), lambda b, i: (i, 0))
    full = lambda a: pl.BlockSpec(a.shape, lambda b, i: (0, 0))
    return pl.pallas_call(
        _proj_c_kernel,
        grid=(nb, t // tm),
        in_specs=[tok(D_MODEL), full(win), full(gq), full(gkv), full(wuq), full(wrot)] + [tab] * 5,
        out_specs=[tok(H_C * LANES), tok(KV_RANK), tok(ROPE_DIM)],
        out_shape=[jax.ShapeDtypeStruct((nb, t, H_C * LANES), BF16),
                   jax.ShapeDtypeStruct((nb, t, KV_RANK), F32),
                   jax.ShapeDtypeStruct((nb, t, ROPE_DIM), F32)],
        compiler_params=_cparams(("parallel", "parallel")),
        name="proj_c",
    )(x, win, gq, gkv, wuq, wrot, *tabs_q, *tabs_k)


def _kv_up_kernel(ckv_ref, kr_ref, wk_ref, place_ref, wvt_ref, k_ref, vt_ref):
    cb = ckv_ref[...].astype(BF16)
    k = (jnp.dot(cb, wk_ref[...], preferred_element_type=F32)
         + jnp.dot(kr_ref[...].astype(BF16), place_ref[...], preferred_element_type=F32))
    k_ref[0] = k.astype(BF16)
    vt = lax.dot_general(wvt_ref[...], cb, (((1,), (1,)), ((), ())), preferred_element_type=F32)
    vt_ref[0] = vt.astype(BF16)


def _kv_up(ckv, kr, wk, place, wvt):
    n = ckv.shape[0]
    tm = ATTN_BLOCK
    assert n % tm == 0
    row = lambda width: pl.BlockSpec((tm, width), lambda i: (i, 0))
    full = lambda a: pl.BlockSpec(a.shape, lambda i: (0, 0))
    return pl.pallas_call(
        _kv_up_kernel,
        grid=(n // tm,),
        in_specs=[row(KV_RANK), row(ROPE_DIM), full(wk), full(place), full(wvt)],
        out_specs=[pl.BlockSpec((1, tm, H_C * LANES), lambda i: (i, 0, 0)),
                   pl.BlockSpec((1, H_C * V_DIM_C, tm), lambda i: (i, 0, 0))],
        out_shape=[jax.ShapeDtypeStruct((n // tm, tm, H_C * LANES), BF16),
                   jax.ShapeDtypeStruct((n // tm, H_C * V_DIM_C, tm), BF16)],
        compiler_params=_cparams(("parallel",)),
        name="kv_up",
    )(ckv, kr, wk, place, wvt)


def _mla_decode_kernel(q_ref, ckv_ref, kr_ref, ckvn_ref, krn_ref, wabs_ref, wv_ref, o_ref):
    q = q_ref[0]
    ts = q.shape[0]
    qs = jnp.concatenate(
        [jnp.dot(q[:, h * LANES:(h + 1) * LANES], wabs_ref[h], preferred_element_type=F32).astype(BF16)
         for h in range(H_C)], axis=0)
    kc = jnp.concatenate([ckv_ref[0].astype(BF16), kr_ref[0].astype(BF16)], axis=1)
    kn = jnp.concatenate([ckvn_ref[0].astype(BF16), krn_ref[0].astype(BF16)], axis=1)
    nt = (((1,), (1,)), ((), ()))
    sc = lax.dot_general(qs, kc, nt, preferred_element_type=F32)
    sn = lax.dot_general(qs, kn, nt, preferred_element_type=F32)
    m = jnp.maximum(jnp.max(sc, axis=1, keepdims=True), jnp.max(sn, axis=1, keepdims=True))
    pc = jnp.exp2(sc - m)
    pn = jnp.exp2(sn - m)
    l = jnp.sum(pc, axis=1, keepdims=True) + jnp.sum(pn, axis=1, keepdims=True)
    ol = (jnp.dot(pc.astype(BF16), kc[:, :KV_RANK], preferred_element_type=F32)
          + jnp.dot(pn.astype(BF16), kn[:, :KV_RANK], preferred_element_type=F32)) / l
    olb = ol.astype(BF16)
    o = jnp.dot(olb[:ts], wv_ref[0], preferred_element_type=F32)
    for h in range(1, H_C):
        o = o + jnp.dot(olb[h * ts:(h + 1) * ts], wv_ref[h], preferred_element_type=F32)
    o_ref[0] = o.astype(o_ref.dtype)


def _mla_decode(q, ckv_c, kr_c, ckv_n, kr_n, w_abs, w_vout):
    nb, ts, _ = q.shape
    assert ckv_c.shape[1] % CHUNK == 0 and ts <= CHUNK
    per_b = lambda a: pl.BlockSpec((1,) + a.shape[1:], lambda b: (b, 0, 0))
    full = lambda a: pl.BlockSpec(a.shape, lambda b: (0, 0, 0))
    return pl.pallas_call(
        _mla_decode_kernel,
        grid=(nb,),
        in_specs=[per_b(q), per_b(ckv_c), per_b(kr_c), per_b(ckv_n), per_b(kr_n), full(w_abs), full(w_vout)],
        out_specs=pl.BlockSpec((1, ts, H_C * V_DIM_C), lambda b: (b, 0, 0)),
        out_shape=jax.ShapeDtypeStruct((nb, ts, H_C * V_DIM_C), BF16),
        compiler_params=_cparams(("parallel",)),
        name="mla_decode",
    )(q, ckv_c, kr_c, ckv_n, kr_n, w_abs, w_vout)


def _rope_tables(pos, dim, lane0):
    half = dim // 2
    inv = ROPE_THETA ** (-jnp.arange(0, dim, 2, dtype=F32) / dim)
    ang = pos.astype(F32)[:, None] * inv[None, :]
    cos, sin = jnp.cos(ang), jnp.sin(ang)
    zero = jnp.zeros_like(sin)
    c = jnp.concatenate([cos, cos], axis=1)
    s1 = jnp.concatenate([-sin, zero], axis=1)
    s2 = jnp.concatenate([zero, sin], axis=1)
    if lane0 < 0:
        reps = LANES // dim
        return tuple(jnp.tile(a, (1, reps)) for a in (c, s1, s2))
    t = pos.shape[0]
    pad = lambda a, fill: jnp.concatenate(
        [jnp.full((t, lane0), fill, F32), a, jnp.full((t, LANES - lane0 - dim), fill, F32)], axis=1)
    return pad(c, 1.0), pad(s1, 0.0), pad(s2, 0.0)


def _pad_cols(a, width):
    return jnp.pad(a, ((0, 0), (0, width - a.shape[1])))


def _blocks(a, tk):
    nb, t, l = a.shape
    return a.reshape(nb, t // tk, tk, l)


def _cat_pad_time(cache, new, t_pad):
    nb, t0, l = cache.shape
    t1 = new.shape[1]
    return jnp.concatenate([cache, new, jnp.zeros((nb, t_pad - t0 - t1, l), cache.dtype)], axis=1)


def kernel(x_prompt, x_sample, cache_fox_k, cache_fox_v, cache_fox_logf, cache_diff_k, cache_diff_v, cache_mla_ckv, cache_mla_krope, w_in_ab, b_fgate, diff_lq1, diff_lk1, diff_lq2, diff_lk2, diff_subln, w_out_ab, w_in_c, mla_q_norm, mla_kv_norm, mla_w_uq, mla_w_ukv, w_out_c, ln1_g, ln1_b, ln2_g, ln2_b, moe_wg, moe_bg, moe_we, moe_be, moe_w1, moe_w3, moe_w2):
    bp, tp, d = x_prompt.shape
    bs, ts, _ = x_sample.shape
    past = cache_fox_k.shape[2]
    depth = ln1_g.shape[0]
    alpha = (2 * depth) ** 0.25
    tk = ATTN_BLOCK
    assert past % tk == 0
    ns = bs * ts
    t_dec = past + tk

    pos_p = jnp.arange(tp)
    pos_s = jnp.tile(past + jnp.arange(ts), bs)

    xp = x_prompt
    xs = x_sample.reshape(1, ns, d)
    out_ab_p, out_ab_s, out_c_p, out_c_s = [], [], [], []

    for i in range(depth):
        j = i // 2
        if i % 2 == 0:
            lam_init = 0.8 - 0.6 * math.exp(-0.3 * i)
            cuts = [0, A_WIDTH, 2 * A_WIDTH, 3 * A_WIDTH, 3 * A_WIDTH + H_A,
                    3 * A_WIDTH + H_A + B_QK_WIDTH, 3 * A_WIDTH + H_A + 2 * B_QK_WIDTH,
                    3 * A_WIDTH + H_A + 2 * B_QK_WIDTH + B_V_WIDTH]
            w = w_in_ab[j]
            piece = lambda a: w[:, cuts[a]:cuts[a + 1]]
            w6 = jnp.stack([piece(0), piece(1), piece(2), piece(4), piece(5), piece(6)]).astype(BF16)
            wvt = jnp.stack([piece(2).T, piece(6).T]).astype(BF16)
            wf = _pad_cols(piece(3), LANES).astype(BF16)
            bf = _pad_cols(b_fgate[j][None, :], LANES)
            wout = w_out_ab[j].astype(BF16)
            diff_extra = (diff_lq1[j][None, :], diff_lk1[j][None, :], diff_lq2[j][None, :],
                          diff_lk2[j][None, :], diff_subln[j][None, :])

            tabs = _rope_tables(pos_p, HEAD_DIM, -1)
            (qa, ka, kab, va, vat, lf, lfw, qb, kb, kbb, vb, vbt) = _proj_ab(xp, w6, wvt, wf, bf, tabs)
            bias = _blocks(_decay_bias(lfw), tk)
            oa = _attention("fox", qa, _blocks(kab, tk), vat, (bias,), n_pairs=H_A // 2, mask_shift=0)
            ob = _attention("diff", qb, _blocks(kbb, tk), vbt, diff_extra,
                            n_pairs=H_B, mask_shift=int(math.log2(CHUNK)), lam_init=lam_init)
            out_ab_p.append((ka.reshape(bp, tp, H_A, HEAD_DIM), va.reshape(bp, tp, H_A, HEAD_DIM), lf,
                             kb.reshape(bp, tp, H_B, 2, HEAD_DIM), vb.reshape(bp, tp, H_B, 2 * HEAD_DIM)))
            xp2 = _outproj_ln(xp.reshape(bp * tp, d), [oa.reshape(bp * tp, -1), ob.reshape(bp * tp, -1)],
                              [wout[:A_WIDTH], wout[A_WIDTH:]], ln1_g[i][None, :], ln1_b[i][None, :], alpha)

            tabs = _rope_tables(pos_s, HEAD_DIM, -1)
            (qa, ka, kab, va, _, lf, lfw, qb, kb, kbb, vb, _) = _proj_ab(xs, w6, wvt, wf, bf, tabs)
            rs = lambda a: a.reshape(bs, ts, a.shape[-1])
            cache_lfw = jnp.pad(cache_fox_logf[j].astype(F32), ((0, 0), (0, 0), (0, LANES - H_A)))
            bias = _decay_bias(_cat_pad_time(cache_lfw, rs(lfw), t_dec))
            flat = lambda c: c.reshape(bs, past, -1)
            oa = _decode_attention("fox", rs(qa), flat(cache_fox_k[j]), flat(cache_fox_v[j]), rs(kab), rs(va),
                                   (bias,))
            ob = _decode_attention("diff", rs(qb), flat(cache_diff_k[j]), flat(cache_diff_v[j]), rs(kbb),
                                   rs(vb), diff_extra, lam_init=lam_init)
            out_ab_s.append((ka.reshape(bs, ts, H_A, HEAD_DIM), va.reshape(bs, ts, H_A, HEAD_DIM),
                             lf.reshape(bs, ts, H_A), kb.reshape(bs, ts, H_B, 2, HEAD_DIM),
                             vb.reshape(bs, ts, H_B, 2 * HEAD_DIM)))
            xs2 = _outproj_ln(xs.reshape(ns, d), [oa.reshape(ns, -1), ob.reshape(ns, -1)],
                              [wout[:A_WIDTH], wout[A_WIDTH:]], ln1_g[i][None, :], ln1_b[i][None, :], alpha)
        else:
            wc = w_in_c[j]
            kr_cols = _pad_cols(wc[:, Q_RANK + KV_RANK:], LANES)
            win = jnp.concatenate([wc[:, :Q_RANK + KV_RANK], kr_cols], axis=1).astype(BF16)
            wuq3 = jnp.pad(mla_w_uq[j].reshape(Q_RANK, H_C, NOPE_DIM + ROPE_DIM),
                           ((0, 0), (0, 0), (0, LANES - NOPE_DIM - ROPE_DIM)))
            wuq = wuq3.reshape(Q_RANK, H_C * LANES).astype(BF16)
            r0, r1, r2 = NOPE_DIM, NOPE_DIM + ROPE_DIM // 2, NOPE_DIM + ROPE_DIM
            wrot = jnp.zeros_like(wuq3).at[:, :, r0:r1].set(-wuq3[:, :, r1:r2]).at[:, :, r1:r2].set(wuq3[:, :, r0:r1])
            wrot = wrot.reshape(Q_RANK, H_C * LANES).astype(BF16)
            q_tabs = lambda pos: (lambda c, s1, s2: (c, s2 - s1))(*_rope_tables(pos, ROPE_DIM, NOPE_DIM))
            wukv = mla_w_ukv[j].reshape(KV_RANK, H_C, NOPE_DIM + V_DIM_C)
            wk = jnp.pad(wukv[:, :, :NOPE_DIM], ((0, 0), (0, 0), (0, LANES - NOPE_DIM)))
            wk = wk.reshape(KV_RANK, H_C * LANES).astype(BF16)
            wvt = wukv[:, :, NOPE_DIM:].reshape(KV_RANK, H_C * V_DIM_C).T.astype(BF16)
            place = jnp.tile(_pad_cols(jnp.concatenate(
                [jnp.zeros((ROPE_DIM, NOPE_DIM), F32), jnp.eye(ROPE_DIM, dtype=F32)], axis=1), LANES),
                (1, H_C)).astype(BF16)
            gq = mla_q_norm[j][None, :]
            gkv = mla_kv_norm[j][None, :]
            wout = w_out_c[j].astype(BF16)

            q, ckv, kr = _proj_c(xp, win, gq, gkv, wuq, wrot, q_tabs(pos_p), _rope_tables(pos_p, ROPE_DIM, 0))
            kc, vct = _kv_up(ckv.reshape(bp * tp, KV_RANK), kr.reshape(bp * tp, ROPE_DIM), wk, place, wvt)
            per_seq = lambda a, nb: a.reshape((nb, a.shape[0] // nb) + a.shape[1:])
            oc = _attention("mla", q, per_seq(kc, bp), per_seq(vct, bp), (), n_pairs=H_C // 2,
                            mask_shift=int(math.log2(CHUNK)))
            out_c_p.append((ckv, kr))
            xp2 = _outproj_ln(xp.reshape(bp * tp, d), [oc.reshape(bp * tp, -1)], [wout],
                              ln1_g[i][None, :], ln1_b[i][None, :], alpha)

            q, ckv, kr = _proj_c(xs, win, gq, gkv, wuq, wrot, q_tabs(pos_s), _rope_tables(pos_s, ROPE_DIM, 0))
            w_abs = jnp.zeros((H_C, LANES, 2 * LANES), F32)
            w_abs = w_abs.at[:, :NOPE_DIM, :KV_RANK].set(jnp.transpose(wukv[:, :, :NOPE_DIM], (1, 2, 0)))
            w_abs = w_abs.at[:, NOPE_DIM:NOPE_DIM + ROPE_DIM, KV_RANK:KV_RANK + ROPE_DIM].set(
                jnp.eye(ROPE_DIM, dtype=F32))
            w_vout = jnp.einsum("khd,hg->hkgd", wukv[:, :, NOPE_DIM:], jnp.eye(H_C, dtype=F32))
            w_vout = w_vout.reshape(H_C, KV_RANK, H_C * V_DIM_C)
            wide = lambda a: jnp.pad(a.astype(F32), ((0, 0), (0, 0), (0, LANES - ROPE_DIM)))
            oc = _mla_decode(q.reshape(bs, ts, -1), cache_mla_ckv[j].astype(F32), wide(cache_mla_krope[j]),
                             ckv.reshape(bs, ts, KV_RANK), wide(kr.reshape(bs, ts, ROPE_DIM)),
                             w_abs.astype(BF16), w_vout.astype(BF16))
            out_c_s.append((ckv.reshape(bs, ts, KV_RANK), kr.reshape(bs, ts, ROPE_DIM)))
            xs2 = _outproj_ln(xs.reshape(ns, d), [oc.reshape(ns, -1)], [wout],
                              ln1_g[i][None, :], ln1_b[i][None, :], alpha)

        wr = _pad_cols(jnp.concatenate(
            [moe_wg[i]] + [moe_we[i][gi] for gi in range(N_GROUPS)], axis=1), LANES)
        wrh = wr.astype(BF16)
        wrl = (wr - wrh.astype(F32)).astype(BF16)
        br = _pad_cols(jnp.concatenate([moe_bg[i], moe_be[i].reshape(-1)])[None, :], LANES)
        moe_w = (moe_w1[i], moe_w3[i], moe_w2[i])
        g2, b2 = ln2_g[i][None, :], ln2_b[i][None, :]
        xp = _moe_ln(xp2, wrh, wrl, br, *moe_w, g2, b2, alpha).reshape(bp, tp, d)
        xs = _moe_ln(xs2, wrh, wrl, br, *moe_w, g2, b2, alpha).reshape(1, ns, d)

    stack = lambda rows, n: jnp.stack([r[n] for r in rows])
    return (xp, xs.reshape(bs, ts, d),
            stack(out_ab_p, 0), stack(out_ab_p, 1), stack(out_ab_p, 2), stack(out_ab_p, 3), stack(out_ab_p, 4),
            stack(out_c_p, 0), stack(out_c_p, 1),
            stack(out_ab_s, 0), stack(out_ab_s, 1), stack(out_ab_s, 2), stack(out_ab_s, 3), stack(out_ab_s, 4),
            stack(out_c_s, 0), stack(out_c_s, 1))
```

```python
import functools
import math

import jax
import jax.numpy as jnp
from jax import lax
from jax.experimental import pallas as pl
from jax.experimental.pallas import tpu as pltpu

F32 = jnp.float32
BF16 = jnp.bfloat16

D_MODEL = 1024
CHUNK = 64
HEAD_DIM = 64
ROPE_THETA = 10000.0
H_A = 8
H_B = 4
H_C = 16
Q_RANK = 256
KV_RANK = 128
NOPE_DIM = 64
ROPE_DIM = 32
V_DIM_C = 64
N_GROUPS = 4
EXPERTS_PER_GROUP = 4
N_EXPERTS = N_GROUPS * EXPERTS_PER_GROUP
D_EXPERT = 256
A_WIDTH = H_A * HEAD_DIM
B_QK_WIDTH = H_B * 2 * HEAD_DIM
B_V_WIDTH = H_B * 2 * HEAD_DIM
FGATE_BIAS = 3.0
LN_EPS = 1e-5
RMS_EPS = 1e-6
NEG_INF = -1e30
LOG2E = math.log2(math.e)

LANES = 128
BF16_ROWS = 16
PANEL = {"fox": 1024, "diff": 256, "mla": 1024}
BIAS_PIECES = 3
VMEM_LIMIT = 48 * 1024 * 1024
ATTN_VMEM_LIMIT = 56 * 1024 * 1024
ATTN_BLOCK = 512
Q_KEY_BLOCKS = 4
ROW_TILE = 512
MOE_TILE = 1024
EXPERTS_PER_STEP = 2
BIAS_ROWS_PER_STEP = 4096
GATE_LANE0 = N_GROUPS


def _cparams(sem, vmem_limit=VMEM_LIMIT):
    return pltpu.CompilerParams(dimension_semantics=sem, vmem_limit_bytes=vmem_limit)


def _rope3(x, c, s1, s2, shift_up, shift_down):
    return x * c + pltpu.roll(x, shift_up, 1) * s1 + pltpu.roll(x, shift_down, 1) * s2


def _layer_norm(y, g, b):
    mu = jnp.mean(y, axis=-1, keepdims=True)
    d = y - mu
    var = jnp.mean(d * d, axis=-1, keepdims=True)
    return d * lax.rsqrt(var + LN_EPS) * g + b


def _split3(x):
    hi = x.astype(BF16)
    r1 = x - hi.astype(F32)
    mid = r1.astype(BF16)
    return hi, mid, (r1 - mid.astype(F32)).astype(BF16)


def _proj_ab_kernel(x_ref, w_ref, wvt_ref, wf_ref, bf_ref, c_ref, s1_ref, s2_ref,
                    qa_ref, ka_ref, kab_ref, va_ref, vat_ref, lf_ref, lfw_ref,
                    qb_ref, kb_ref, kbb_ref, vb_ref, vbt_ref):
    xb = x_ref[0].astype(BF16)

    def mm(i):
        return jnp.dot(xb, w_ref[i], preferred_element_type=F32)

    def mm_t(i):
        return lax.dot_general(wvt_ref[i], xb, (((1,), (1,)), ((), ())), preferred_element_type=F32)

    qa_ref[0] = (mm(0) * (HEAD_DIM ** -0.5 * LOG2E)).astype(BF16)
    ka = mm(1)
    ka_ref[0] = ka
    kab_ref[0] = ka.astype(BF16)
    va_ref[0] = mm(2)
    vat_ref[0, 0] = mm_t(0).astype(BF16)

    z = jnp.dot(xb, wf_ref[...], preferred_element_type=F32) + bf_ref[...]
    lf = jnp.minimum(z, 0.0) - jnp.log1p(jnp.exp(-jnp.abs(z)))
    lf_ref[0] = lf[:, :H_A]
    lfw_ref[0] = lf

    c, s1, s2 = c_ref[...], s1_ref[...], s2_ref[...]
    qb = mm(3)
    kb = mm(4)
    for s in range(B_QK_WIDTH // LANES):
        sl = slice(s * LANES, (s + 1) * LANES)
        qs = _rope3(qb[:, sl], c, s1, s2, LANES - HEAD_DIM // 2, HEAD_DIM // 2)
        qb_ref[0, :, sl] = (qs * (HEAD_DIM ** -0.5 * LOG2E)).astype(BF16)
        ks = _rope3(kb[:, sl], c, s1, s2, LANES - HEAD_DIM // 2, HEAD_DIM // 2)
        kb_ref[0, :, sl] = ks
        kbb_ref[0, :, sl] = ks.astype(BF16)
    vb_ref[0] = mm(5)
    vbt_ref[0, 0] = mm_t(1).astype(BF16)


def _proj_ab(x, w6, wvt, wf, bf, tabs):
    nb, t, _ = x.shape
    tm = min(ROW_TILE, t)
    assert t % tm == 0
    w = A_WIDTH
    tok = lambda width: pl.BlockSpec((1, tm, width), lambda b, i: (b, i, 0))
    tr = pl.BlockSpec((1, 1, w, tm), lambda b, i: (b, i, 0, 0))
    tab = pl.BlockSpec((tm, LANES), lambda b, i: (i, 0))
    full = lambda a: pl.BlockSpec(a.shape, lambda b, i: (0,) * a.ndim)
    sds = lambda width, dt: jax.ShapeDtypeStruct((nb, t, width), dt)
    sds_t = jax.ShapeDtypeStruct((nb, t // tm, w, tm), BF16)
    return pl.pallas_call(
        _proj_ab_kernel,
        grid=(nb, t // tm),
        in_specs=[tok(D_MODEL), full(w6), full(wvt), full(wf), full(bf), tab, tab, tab],
        out_specs=[tok(w), tok(w), tok(w), tok(w), tr, tok(H_A), tok(LANES), tok(w), tok(w), tok(w), tok(w), tr],
        out_shape=[sds(w, BF16), sds(w, F32), sds(w, BF16), sds(w, F32), sds_t, sds(H_A, F32), sds(LANES, F32),
                   sds(w, BF16), sds(w, F32), sds(w, BF16), sds(w, F32), sds_t],
        compiler_params=_cparams(("parallel", "parallel")),
        name="proj_ab",
    )(x, w6, wvt, wf, bf, *tabs)


def _decay_bias_kernel(lf_ref, spread_ref, lower_ref, o_ref, carry_ref):
    @pl.when(pl.program_id(1) == 0)
    def _():
        carry_ref[...] = jnp.zeros_like(carry_ref)

    spread = spread_ref[...]
    lower = lower_ref[...]
    tc = lower.shape[0]
    lane = lax.broadcasted_iota(jnp.int32, (1, LANES), 1).astype(F32)
    piece = lane - BIAS_PIECES * jnp.floor((lane + 0.5) * (1.0 / BIAS_PIECES))
    carry = carry_ref[...]
    for r in range(lf_ref.shape[1] // tc):
        rows = slice(r * tc, (r + 1) * tc)
        x = lf_ref[0, rows, :]
        xr = sum(jnp.dot(p, spread, preferred_element_type=F32) for p in _split3(x))
        c = sum(jnp.dot(lower, p, preferred_element_type=F32) for p in _split3(xr)) + carry
        carry = c[tc - 1:tc, :]
        hi, mid, lo = (p.astype(F32) for p in _split3(c * (-LOG2E)))
        o_ref[0, rows, :] = jnp.where(piece == 0.0, hi, jnp.where(piece == 1.0, mid, lo)).astype(BF16)
    carry_ref[...] = carry


def _decay_bias(lf_wide):
    nb, t, _ = lf_wide.shape
    tc = min(ATTN_BLOCK, t)
    tb = min(BIAS_ROWS_PER_STEP, t)
    assert t % tb == 0 and tb % tc == 0
    spec = pl.BlockSpec((1, tb, LANES), lambda b, i: (b, i, 0))
    src = jnp.arange(LANES)[:, None]
    dst = jnp.arange(LANES)[None, :]
    spread = ((dst // BIAS_PIECES == src) & (src < H_A)).astype(BF16)
    lower = jnp.tril(jnp.ones((tc, tc), BF16))
    const = lambda a: pl.BlockSpec(a.shape, lambda b, i: (0, 0))
    return pl.pallas_call(
        _decay_bias_kernel,
        grid=(nb, t // tb),
        in_specs=[spec, const(spread), const(lower)],
        out_specs=spec,
        out_shape=jax.ShapeDtypeStruct((nb, t, LANES), BF16),
        scratch_shapes=[pltpu.VMEM((1, LANES), F32)],
        compiler_params=_cparams(("parallel", "arbitrary")),
        name="cumsum",
    )(lf_wide, spread, lower)


def _attn_kernel(*refs, mode, tq, tk, mask_shift, lam_init):
    if mode == "diff":
        q_ref, k_ref, vt_ref, lq1_ref, lk1_ref, lq2_ref, lk2_ref, sub_ref, o_ref = refs[:9]
    elif mode == "fox":
        q_ref, k_ref, vt_ref, b_ref, o_ref = refs[:5]
    else:
        q_ref, k_ref, vt_ref, o_ref = refs[:4]
    m_sc, acc_sc, sa, bma, sb, bmb = refs[-6:]
    v_rows = LANES if mode == "diff" else HEAD_DIM
    sa_sc, sb_sc = (sa, bma), (sb, bmb)

    qi = pl.program_id(2)
    q = q_ref[0]
    lane = lax.broadcasted_iota(jnp.int32, (1, LANES), 1)
    if mode == "mla":
        qs = [q[:, :LANES], q[:, LANES:]]
    else:
        zero = jnp.zeros_like(q)
        qs = [jnp.where(lane < HEAD_DIM, q, zero), jnp.where(lane >= HEAD_DIM, q, zero)]
        if mode == "fox":
            def pick(i):
                lo = BIAS_PIECES * (2 * pl.program_id(1) + i)
                hot = jnp.where((lane >= lo) & (lane < lo + BIAS_PIECES), 1.0, 0.0)
                return jnp.broadcast_to(hot, (tq, LANES)).astype(BF16)

            qs = [jnp.concatenate([qs[i], pick(i)], axis=1) for i in range(2)]

    m_sc[...] = jnp.full(m_sc.shape, NEG_INF, F32)
    acc_sc[...] = jnp.zeros(acc_sc.shape, F32)

    pw = min(PANEL[mode], tq)

    def scores(j, bufs, q0, q1):
        s_sc, bm_sc = bufs
        cs = slice(q0, q1)
        k = k_ref[0, j]
        if mode == "fox":
            k = jnp.concatenate([k, b_ref[0, j]], axis=1)
        for i in range(2):
            ki = k[:, i * LANES:(i + 1) * LANES] if mode == "mla" else k
            st = lax.dot_general(ki, qs[i][cs], (((1,), (1,)), ((), ())), preferred_element_type=F32)
            s_sc[i, :, cs] = st
            bm_sc[i, :, cs] = jnp.max(st, axis=0, keepdims=True)

    def consume(j, bufs, q0, q1, key0=None):
        s_sc, bm_sc = bufs
        cs = slice(q0, q1)
        vt = vt_ref[0, j]
        masked = key0 is not None and ((key0 + tk - 1) >> mask_shift) > (q0 >> mask_shift)
        for i in range(2):
            st = s_sc[i, :, cs]
            if masked:
                key = lax.broadcasted_iota(jnp.int32, (tk, q1 - q0), 0) + key0
                qry = lax.broadcasted_iota(jnp.int32, (tk, q1 - q0), 1) + q0
                vis = lax.shift_right_logical(key, mask_shift) <= lax.shift_right_logical(qry, mask_shift)
                st = jnp.where(vis, st, NEG_INF)
                blk_max = jnp.max(st, axis=0, keepdims=True)
            else:
                blk_max = bm_sc[i, :, cs]
            m_prev = m_sc[i, :, cs]
            m_new = jnp.maximum(m_prev, blk_max)
            alpha = jnp.exp2(m_prev - m_new)
            p = jnp.exp2(st - m_new).astype(BF16)
            vi = vt if mode == "diff" else vt[i * HEAD_DIM:(i + 1) * HEAD_DIM]
            vi = jnp.concatenate([vi, jnp.ones((BF16_ROWS, tk), BF16)], axis=0)
            acc_sc[i, :, cs] = alpha * acc_sc[i, :, cs] + jnp.dot(vi, p, preferred_element_type=F32)
            m_sc[i, :, cs] = m_new

    def stage(nxt, cur):
        for q0 in range(0, tq, pw):
            scores(nxt[0], nxt[1], q0, q0 + pw)
            consume(cur[0], cur[1], q0, q0 + pw)

    n_diag = tq // tk
    nfull = n_diag * qi
    bufs = (sa_sc, sb_sc)

    def trip(jj, carry):
        for d in range(n_diag):
            j = n_diag * jj + d
            stage((j + 1, bufs[(d + 1) % 2]), (j, bufs[d % 2]))
        return carry

    for q0 in range(0, tq, pw):
        scores(0, sa_sc, q0, q0 + pw)
    lax.fori_loop(0, qi, trip, 0)

    pt = min(pw, tk)
    sees = lambda d, q1: ((d * tk) >> mask_shift) <= ((q1 - 1) >> mask_shift)
    for d in range(n_diag):
        for q0 in range(0, tq, pt):
            if d + 1 < n_diag and sees(d + 1, q0 + pt):
                scores(nfull + d + 1, bufs[(d + 1) % 2], q0, q0 + pt)
            if sees(d, q0 + pt):
                consume(nfull + d, bufs[d % 2], q0, q0 + pt, key0=d * tk)

    if mode == "diff":
        lam = (jnp.exp(jnp.sum(lq1_ref[...] * lk1_ref[...], axis=1, keepdims=True))
               - jnp.exp(jnp.sum(lq2_ref[...] * lk2_ref[...], axis=1, keepdims=True)) + lam_init)
    for q0 in range(0, tq, pw):
        cs = slice(q0, q0 + pw)
        o0 = acc_sc[0, :v_rows, cs] / acc_sc[0, v_rows:v_rows + 1, cs]
        o1 = acc_sc[1, :v_rows, cs] / acc_sc[1, v_rows:v_rows + 1, cs]
        if mode == "diff":
            o = o0 - lam * o1
            ms = jnp.mean(o * o, axis=0, keepdims=True)
            o = (o * lax.rsqrt(ms + RMS_EPS)).T * sub_ref[...] * (1.0 - lam_init)
        else:
            o = jnp.concatenate([o0, o1], axis=0).T
        o_ref[0, cs, :] = o.astype(o_ref.dtype)


def _attention(mode, q, k, vt, extra, *, n_pairs, mask_shift, lam_init=0.0):
    nb, t_q, _ = q.shape
    _, nkb, tk, _ = k.shape
    tq = Q_KEY_BLOCKS * tk
    assert t_q % tq == 0 and nkb == t_q // tk
    qw = 2 * LANES if mode == "mla" else LANES
    in_specs = [
        pl.BlockSpec((1, tq, qw), lambda b, p, i: (b, i, p)),
        pl.BlockSpec((1, nkb, tk, qw), lambda b, p, i: (b, 0, 0, p)),
        pl.BlockSpec((1, nkb, LANES, tk), lambda b, p, i: (b, 0, p, 0)),
    ]
    if mode == "fox":
        in_specs.append(pl.BlockSpec((1, nkb, tk, LANES), lambda b, p, i: (b, 0, 0, 0)))
    elif mode == "diff":
        in_specs += [pl.BlockSpec(a.shape, lambda b, p, i: (0, 0)) for a in extra]
    kern = functools.partial(_attn_kernel, mode=mode, tq=tq, tk=tk, mask_shift=mask_shift, lam_init=lam_init)
    return pl.pallas_call(
        kern,
        grid=(nb, n_pairs, t_q // tq),
        in_specs=in_specs,
        out_specs=pl.BlockSpec((1, tq, LANES), lambda b, p, i: (b, i, p)),
        out_shape=jax.ShapeDtypeStruct((nb, t_q, n_pairs * LANES), BF16),
        scratch_shapes=[pltpu.VMEM((2, 1, tq), F32),
                        pltpu.VMEM((2, (LANES if mode == "diff" else HEAD_DIM) + BF16_ROWS, tq), F32),
                        pltpu.VMEM((2, tk, tq), F32), pltpu.VMEM((2, 1, tq), F32),
                        pltpu.VMEM((2, tk, tq), F32), pltpu.VMEM((2, 1, tq), F32)],
        compiler_params=_cparams(("parallel", "parallel", "arbitrary"), ATTN_VMEM_LIMIT),
        name="attn_" + mode,
    )(q, k, vt, *extra)


def _decode_attn_kernel(*refs, mode, lam_init):
    if mode == "fox":
        q_ref, kc_ref, vc_ref, kn_ref, vn_ref, b_ref, o_ref = refs
    else:
        q_ref, kc_ref, vc_ref, kn_ref, vn_ref, lq1_ref, lk1_ref, lq2_ref, lk2_ref, sub_ref, o_ref = refs
        lam = (jnp.exp(jnp.sum(lq1_ref[...] * lk1_ref[...], axis=1, keepdims=True))
               - jnp.exp(jnp.sum(lq2_ref[...] * lk2_ref[...], axis=1, keepdims=True)) + lam_init)
    ts = q_ref.shape[1]
    past = kc_ref.shape[1]
    lane = lax.broadcasted_iota(jnp.int32, (1, LANES), 1)
    row = lax.broadcasted_iota(jnp.int32, (ts, ts), 0)
    col = lax.broadcasted_iota(jnp.int32, (ts, ts), 1)
    nt = (((1,), (1,)), ((), ()))
    for p in range(q_ref.shape[2] // LANES):
        sl = slice(p * LANES, (p + 1) * LANES)
        q = q_ref[0, :, sl]
        kc = kc_ref[0, :, sl].astype(BF16)
        kn = kn_ref[0, :, sl]
        vc = vc_ref[0, :, sl].astype(BF16)
        vn = vn_ref[0, :, sl].astype(BF16)
        if mode == "fox":
            kc = jnp.concatenate([kc, b_ref[0, :past, :]], axis=1)
            kn = jnp.concatenate([kn, b_ref[0, past:past + ts, :]], axis=1)
        zero = jnp.zeros_like(q)
        outs = []
        for i in range(2):
            qi = jnp.where(lane < HEAD_DIM, q, zero) if i == 0 else jnp.where(lane >= HEAD_DIM, q, zero)
            if mode == "fox":
                lo = BIAS_PIECES * (2 * p + i)
                hot = jnp.where((lane >= lo) & (lane < lo + BIAS_PIECES), 1.0, 0.0)
                qi = jnp.concatenate([qi, jnp.broadcast_to(hot, (ts, LANES)).astype(BF16)], axis=1)
            sc = lax.dot_general(qi, kc, nt, preferred_element_type=F32)
            sn = lax.dot_general(qi, kn, nt, preferred_element_type=F32)
            if mode == "fox":
                sn = jnp.where(col <= row, sn, NEG_INF)
            m = jnp.maximum(jnp.max(sc, axis=1, keepdims=True), jnp.max(sn, axis=1, keepdims=True))
            pc = jnp.exp2(sc - m)
            pn = jnp.exp2(sn - m)
            l = jnp.sum(pc, axis=1, keepdims=True) + jnp.sum(pn, axis=1, keepdims=True)
            outs.append((jnp.dot(pc.astype(BF16), vc, preferred_element_type=F32)
                         + jnp.dot(pn.astype(BF16), vn, preferred_element_type=F32)) / l)
        if mode == "fox":
            o = jnp.where(lane < HEAD_DIM, outs[0], outs[1])
        else:
            o = outs[0] - lam * outs[1]
            ms = jnp.mean(o * o, axis=1, keepdims=True)
            o = o * lax.rsqrt(ms + RMS_EPS) * sub_ref[...] * (1.0 - lam_init)
        o_ref[0, :, sl] = o.astype(o_ref.dtype)


def _decode_attention(mode, q, k_cache, v_cache, k_new, v_new, extra, lam_init=0.0):
    nb, ts, w = q.shape
    assert k_cache.shape[1] % CHUNK == 0 and ts <= CHUNK
    per_b = lambda a: pl.BlockSpec((1,) + a.shape[1:], lambda b: (b, 0, 0))
    if mode == "fox":
        extra_specs = [per_b(extra[0])]
    else:
        extra_specs = [pl.BlockSpec(a.shape, lambda b: (0, 0)) for a in extra]
    return pl.pallas_call(
        functools.partial(_decode_attn_kernel, mode=mode, lam_init=lam_init),
        grid=(nb,),
        in_specs=[per_b(q), per_b(k_cache), per_b(v_cache), per_b(k_new), per_b(v_new)] + extra_specs,
        out_specs=pl.BlockSpec((1, ts, w), lambda b: (b, 0, 0)),
        out_shape=jax.ShapeDtypeStruct((nb, ts, w), BF16),
        compiler_params=_cparams(("parallel",)),
        name="decode_" + mode,
    )(q, k_cache, v_cache, k_new, v_new, *extra)


def _outproj_ln_kernel(*refs, n_in, alpha):
    x_ref = refs[0]
    o_refs = refs[1:1 + n_in]
    w_refs = refs[1 + n_in:1 + 2 * n_in]
    g_ref, b_ref, y_ref = refs[1 + 2 * n_in:]
    mix = jnp.dot(o_refs[0][...], w_refs[0][...], preferred_element_type=F32)
    for o_r, w_r in zip(o_refs[1:], w_refs[1:]):
        mix = mix + jnp.dot(o_r[...], w_r[...], preferred_element_type=F32)
    y_ref[...] = _layer_norm(alpha * x_ref[...] + mix, g_ref[...], b_ref[...])


def _outproj_ln(x, outs, ws, g, b, alpha):
    n, d = x.shape
    tm = min(MOE_TILE, n)
    assert n % tm == 0
    row = lambda width: pl.BlockSpec((tm, width), lambda i: (i, 0))
    full = lambda a: pl.BlockSpec(a.shape, lambda i: (0, 0))
    return pl.pallas_call(
        functools.partial(_outproj_ln_kernel, n_in=len(outs), alpha=alpha),
        grid=(n // tm,),
        in_specs=[row(d)] + [row(o.shape[1]) for o in outs] + [full(w) for w in ws] + [full(g), full(b)],
        out_specs=row(d),
        out_shape=jax.ShapeDtypeStruct((n, d), F32),
        compiler_params=_cparams(("parallel",)),
        name="outproj_ln",
    )(x, *outs, *ws, g, b)


def _route(logits):
    lane = lax.broadcasted_iota(jnp.int32, logits.shape, 1).astype(F32)
    big = float(1 << 20)
    is_g = lane < N_GROUPS
    lg = jnp.where(is_g, logits, NEG_INF)
    eg = jnp.where(is_g, jnp.exp(lg - jnp.max(lg, axis=1, keepdims=True)), 0.0)
    pg = eg / jnp.sum(eg, axis=1, keepdims=True)
    p_g = jnp.max(pg, axis=1, keepdims=True)
    gidx = jnp.min(jnp.where(is_g & (pg == p_g), lane, big), axis=1, keepdims=True)
    lo = GATE_LANE0 + EXPERTS_PER_GROUP * gidx
    sel = (lane >= lo) & (lane < lo + EXPERTS_PER_GROUP)
    le = jnp.where(sel, logits, NEG_INF)
    ee = jnp.where(sel, jnp.exp(le - jnp.max(le, axis=1, keepdims=True)), 0.0)
    pe = ee / jnp.sum(ee, axis=1, keepdims=True)
    v1 = jnp.max(jnp.where(sel, pe, -1.0), axis=1, keepdims=True)
    i1 = jnp.min(jnp.where(sel & (pe == v1), lane, big), axis=1, keepdims=True)
    rest = sel & (lane != i1)
    v2 = jnp.max(jnp.where(rest, pe, -1.0), axis=1, keepdims=True)
    i2 = jnp.min(jnp.where(rest & (pe == v2), lane, big), axis=1, keepdims=True)
    tot = v1 + v2
    w1 = v1 / tot * p_g
    w2 = v2 / tot * p_g
    return jnp.where(lane == i1, w1, jnp.where(lane == i2, w2, 0.0))


def _moe_ln_kernel(x_ref, wrh_ref, wrl_ref, br_ref, w1_ref, w3_ref, w2_ref, g_ref, b_ref, y_ref,
                   xb_sc, gate_sc, acc_sc, *, alpha):
    e = pl.program_id(1)

    @pl.when(e == 0)
    def _():
        x = x_ref[...]
        xh = x.astype(BF16)
        xl = (x - xh.astype(F32)).astype(BF16)
        xb_sc[...] = xh
        logits = (jnp.dot(xh, wrh_ref[...], preferred_element_type=F32)
                  + jnp.dot(xl, wrh_ref[...], preferred_element_type=F32)
                  + jnp.dot(xh, wrl_ref[...], preferred_element_type=F32) + br_ref[...])
        gate_sc[...] = _route(logits)
        acc_sc[...] = jnp.zeros_like(acc_sc)

    xb = xb_sc[...]
    lane = lax.broadcasted_iota(jnp.int32, (1, LANES), 1)
    for s in range(w1_ref.shape[0]):
        h1 = jnp.dot(xb, w1_ref[s].astype(BF16), preferred_element_type=F32)
        h3 = jnp.dot(xb, w3_ref[s].astype(BF16), preferred_element_type=F32)
        hdn = (h1 * jax.nn.sigmoid(h1)) * h3
        y = jnp.dot(hdn.astype(BF16), w2_ref[s].astype(BF16), preferred_element_type=F32)
        expert_lane = e * w1_ref.shape[0] + s + GATE_LANE0
        ge = jnp.sum(jnp.where(lane == expert_lane, gate_sc[...], 0.0), axis=1, keepdims=True)
        acc_sc[...] += ge * y

    @pl.when(e == pl.num_programs(1) - 1)
    def _():
        y_ref[...] = _layer_norm(alpha * x_ref[...] + acc_sc[...], g_ref[...], b_ref[...])


def _moe_ln(x, wrh, wrl, br, w1, w3, w2, g, b, alpha):
    n, d = x.shape
    tm = min(MOE_TILE, n)
    assert n % tm == 0
    ne = w1.shape[0]
    assert ne % EXPERTS_PER_STEP == 0
    per_expert = lambda a: pl.BlockSpec((EXPERTS_PER_STEP,) + a.shape[1:], lambda i, e: (e, 0, 0))
    full = lambda a: pl.BlockSpec(a.shape, lambda i, e: (0, 0))
    return pl.pallas_call(
        functools.partial(_moe_ln_kernel, alpha=alpha),
        grid=(n // tm, ne // EXPERTS_PER_STEP),
        in_specs=[pl.BlockSpec((tm, d), lambda i, e: (i, 0)), full(wrh), full(wrl), full(br),
                  per_expert(w1), per_expert(w3), per_expert(w2),
                  full(g), full(b)],
        out_specs=pl.BlockSpec((tm, d), lambda i, e: (i, 0)),
        out_shape=jax.ShapeDtypeStruct((n, d), F32),
        scratch_shapes=[pltpu.VMEM((tm, d), BF16), pltpu.VMEM((tm, LANES), F32), pltpu.VMEM((tm, d), F32)],
        compiler_params=_cparams(("parallel", "arbitrary")),
        name="moe_ln",
    )(x, wrh, wrl, br, w1, w3, w2, g, b)


def _proj_c_kernel(x_ref, win_ref, gq_ref, gkv_ref, wuq_ref, wrot_ref, cq_ref, sq_ref,
                   ck_ref, s1k_ref, s2k_ref, q_ref, ckv_ref, kr_ref):
    xb = x_ref[0].astype(BF16)
    h = jnp.dot(xb, win_ref[...], preferred_element_type=F32)
    qa = h[:, :Q_RANK]
    kva = h[:, Q_RANK:Q_RANK + KV_RANK]
    krw = h[:, Q_RANK + KV_RANK:]
    qn = qa * lax.rsqrt(jnp.mean(qa * qa, axis=1, keepdims=True) + RMS_EPS) * gq_ref[...]
    ckv_ref[0] = kva * lax.rsqrt(jnp.mean(kva * kva, axis=1, keepdims=True) + RMS_EPS) * gkv_ref[...]
    half = ROPE_DIM // 2
    kr = _rope3(krw, ck_ref[...], s1k_ref[...], s2k_ref[...], LANES - half, half)
    kr_ref[0] = kr[:, :ROPE_DIM]
    qnb = qn.astype(BF16)
    q = jnp.dot(qnb, wuq_ref[...], preferred_element_type=F32)
    q_rot = jnp.dot(qnb, wrot_ref[...], preferred_element_type=F32)
    cq, sq = cq_ref[...], sq_ref[...]
    scale = (NOPE_DIM + ROPE_DIM) ** -0.5 * LOG2E
    for hd in range(H_C):
        sl = slice(hd * LANES, (hd + 1) * LANES)
        q_ref[0, :, sl] = ((q[:, sl] * cq + q_rot[:, sl] * sq) * scale).astype(BF16)


def _proj_c(x, win, gq, gkv, wuq, wrot, tabs_q, tabs_k):
    nb, t, _ = x.shape
    tm = min(ROW_TILE, t)
    assert t % tm == 0
    tok = lambda width: pl.BlockSpec((1, tm, width), lambda b, i: (b, i, 0))
    tab = pl.BlockSpec((tm, LANES), lambda b, i: (i, 0))
    full = lambda a: pl.BlockSpec(a.shape, lambda b, i: (0, 0))
    return pl.pallas_call(
        _proj_c_kernel,
        grid=(nb, t // tm),
        in_specs=[tok(D_MODEL), full(win), full(gq), full(gkv), full(wuq), full(wrot)] + [tab] * 5,
        out_specs=[tok(H_C * LANES), tok(KV_RANK), tok(ROPE_DIM)],
        out_shape=[jax.ShapeDtypeStruct((nb, t, H_C * LANES), BF16),
                   jax.ShapeDtypeStruct((nb, t, KV_RANK), F32),
                   jax.ShapeDtypeStruct((nb, t, ROPE_DIM), F32)],
        compiler_params=_cparams(("parallel", "parallel")),
        name="proj_c",
    )(x, win, gq, gkv, wuq, wrot, *tabs_q, *tabs_k)


def _kv_up_kernel(ckv_ref, kr_ref, wk_ref, place_ref, wvt_ref, k_ref, vt_ref):
    cb = ckv_ref[...].astype(BF16)
    k = (jnp.dot(cb, wk_ref[...], preferred_element_type=F32)
         + jnp.dot(kr_ref[...].astype(BF16), place_ref[...], preferred_element_type=F32))
    k_ref[0] = k.astype(BF16)
    vt = lax.dot_general(wvt_ref[...], cb, (((1,), (1,)), ((), ())), preferred_element_type=F32)
    vt_ref[0] = vt.astype(BF16)


def _kv_up(ckv, kr, wk, place, wvt):
    n = ckv.shape[0]
    tm = ATTN_BLOCK
    assert n % tm == 0
    row = lambda width: pl.BlockSpec((tm, width), lambda i: (i, 0))
    full = lambda a: pl.BlockSpec(a.shape, lambda i: (0, 0))
    return pl.pallas_call(
        _kv_up_kernel,
        grid=(n // tm,),
        in_specs=[row(KV_RANK), row(ROPE_DIM), full(wk), full(place), full(wvt)],
        out_specs=[pl.BlockSpec((1, tm, H_C * LANES), lambda i: (i, 0, 0)),
                   pl.BlockSpec((1, H_C * V_DIM_C, tm), lambda i: (i, 0, 0))],
        out_shape=[jax.ShapeDtypeStruct((n // tm, tm, H_C * LANES), BF16),
                   jax.ShapeDtypeStruct((n // tm, H_C * V_DIM_C, tm), BF16)],
        compiler_params=_cparams(("parallel",)),
        name="kv_up",
    )(ckv, kr, wk, place, wvt)


def _mla_decode_kernel(q_ref, ckv_ref, kr_ref, ckvn_ref, krn_ref, wabs_ref, wv_ref, o_ref):
    q = q_ref[0]
    ts = q.shape[0]
    qs = jnp.concatenate(
        [jnp.dot(q[:, h * LANES:(h + 1) * LANES], wabs_ref[h], preferred_element_type=F32).astype(BF16)
         for h in range(H_C)], axis=0)
    kc = jnp.concatenate([ckv_ref[0].astype(BF16), kr_ref[0].astype(BF16)], axis=1)
    kn = jnp.concatenate([ckvn_ref[0].astype(BF16), krn_ref[0].astype(BF16)], axis=1)
    nt = (((1,), (1,)), ((), ()))
    sc = lax.dot_general(qs, kc, nt, preferred_element_type=F32)
    sn = lax.dot_general(qs, kn, nt, preferred_element_type=F32)
    m = jnp.maximum(jnp.max(sc, axis=1, keepdims=True), jnp.max(sn, axis=1, keepdims=True))
    pc = jnp.exp2(sc - m)
    pn = jnp.exp2(sn - m)
    l = jnp.sum(pc, axis=1, keepdims=True) + jnp.sum(pn, axis=1, keepdims=True)
    ol = (jnp.dot(pc.astype(BF16), kc[:, :KV_RANK], preferred_element_type=F32)
          + jnp.dot(pn.astype(BF16), kn[:, :KV_RANK], preferred_element_type=F32)) / l
    olb = ol.astype(BF16)
    o = jnp.dot(olb[:ts], wv_ref[0], preferred_element_type=F32)
    for h in range(1, H_C):
        o = o + jnp.dot(olb[h * ts:(h + 1) * ts], wv_ref[h], preferred_element_type=F32)
    o_ref[0] = o.astype(o_ref.dtype)


def _mla_decode(q, ckv_c, kr_c, ckv_n, kr_n, w_abs, w_vout):
    nb, ts, _ = q.shape
    assert ckv_c.shape[1] % CHUNK == 0 and ts <= CHUNK
    per_b = lambda a: pl.BlockSpec((1,) + a.shape[1:], lambda b: (b, 0, 0))
    full = lambda a: pl.BlockSpec(a.shape, lambda b: (0, 0, 0))
    return pl.pallas_call(
        _mla_decode_kernel,
        grid=(nb,),
        in_specs=[per_b(q), per_b(ckv_c), per_b(kr_c), per_b(ckv_n), per_b(kr_n), full(w_abs), full(w_vout)],
        out_specs=pl.BlockSpec((1, ts, H_C * V_DIM_C), lambda b: (b, 0, 0)),
        out_shape=jax.ShapeDtypeStruct((nb, ts, H_C * V_DIM_C), BF16),
        compiler_params=_cparams(("parallel",)),
        name="mla_decode",
    )(q, ckv_c, kr_c, ckv_n, kr_n, w_abs, w_vout)


def _rope_tables(pos, dim, lane0):
    half = dim // 2
    inv = ROPE_THETA ** (-jnp.arange(0, dim, 2, dtype=F32) / dim)
    ang = pos.astype(F32)[:, None] * inv[None, :]
    cos, sin = jnp.cos(ang), jnp.sin(ang)
    zero = jnp.zeros_like(sin)
    c = jnp.concatenate([cos, cos], axis=1)
    s1 = jnp.concatenate([-sin, zero], axis=1)
    s2 = jnp.concatenate([zero, sin], axis=1)
    if lane0 < 0:
        reps = LANES // dim
        return tuple(jnp.tile(a, (1, reps)) for a in (c, s1, s2))
    t = pos.shape[0]
    pad = lambda a, fill: jnp.concatenate(
        [jnp.full((t, lane0), fill, F32), a, jnp.full((t, LANES - lane0 - dim), fill, F32)], axis=1)
    return pad(c, 1.0), pad(s1, 0.0), pad(s2, 0.0)


def _pad_cols(a, width):
    return jnp.pad(a, ((0, 0), (0, width - a.shape[1])))


def _blocks(a, tk):
    nb, t, l = a.shape
    return a.reshape(nb, t // tk, tk, l)


def _cat_pad_time(cache, new, t_pad):
    nb, t0, l = cache.shape
    t1 = new.shape[1]
    return jnp.concatenate([cache, new, jnp.zeros((nb, t_pad - t0 - t1, l), cache.dtype)], axis=1)


def kernel(x_prompt, x_sample, cache_fox_k, cache_fox_v, cache_fox_logf, cache_diff_k, cache_diff_v, cache_mla_ckv, cache_mla_krope, w_in_ab, b_fgate, diff_lq1, diff_lk1, diff_lq2, diff_lk2, diff_subln, w_out_ab, w_in_c, mla_q_norm, mla_kv_norm, mla_w_uq, mla_w_ukv, w_out_c, ln1_g, ln1_b, ln2_g, ln2_b, moe_wg, moe_bg, moe_we, moe_be, moe_w1, moe_w3, moe_w2):
    bp, tp, d = x_prompt.shape
    bs, ts, _ = x_sample.shape
    past = cache_fox_k.shape[2]
    depth = ln1_g.shape[0]
    alpha = (2 * depth) ** 0.25
    tk = ATTN_BLOCK
    assert past % tk == 0
    ns = bs * ts
    t_dec = past + tk

    pos_p = jnp.arange(tp)
    pos_s = jnp.tile(past + jnp.arange(ts), bs)

    xp = x_prompt
    xs = x_sample.reshape(1, ns, d)
    out_ab_p, out_ab_s, out_c_p, out_c_s = [], [], [], []

    for i in range(depth):
        j = i // 2
        if i % 2 == 0:
            lam_init = 0.8 - 0.6 * math.exp(-0.3 * i)
            cuts = [0, A_WIDTH, 2 * A_WIDTH, 3 * A_WIDTH, 3 * A_WIDTH + H_A,
                    3 * A_WIDTH + H_A + B_QK_WIDTH, 3 * A_WIDTH + H_A + 2 * B_QK_WIDTH,
                    3 * A_WIDTH + H_A + 2 * B_QK_WIDTH + B_V_WIDTH]
            w = w_in_ab[j]
            piece = lambda a: w[:, cuts[a]:cuts[a + 1]]
            w6 = jnp.stack([piece(0), piece(1), piece(2), piece(4), piece(5), piece(6)]).astype(BF16)
            wvt = jnp.stack([piece(2).T, piece(6).T]).astype(BF16)
            wf = _pad_cols(piece(3), LANES).astype(BF16)
            bf = _pad_cols(b_fgate[j][None, :], LANES)
            wout = w_out_ab[j].astype(BF16)
            diff_extra = (diff_lq1[j][None, :], diff_lk1[j][None, :], diff_lq2[j][None, :],
                          diff_lk2[j][None, :], diff_subln[j][None, :])

            tabs = _rope_tables(pos_p, HEAD_DIM, -1)
            (qa, ka, kab, va, vat, lf, lfw, qb, kb, kbb, vb, vbt) = _proj_ab(xp, w6, wvt, wf, bf, tabs)
            bias = _blocks(_decay_bias(lfw), tk)
            oa = _attention("fox", qa, _blocks(kab, tk), vat, (bias,), n_pairs=H_A // 2, mask_shift=0)
            ob = _attention("diff", qb, _blocks(kbb, tk), vbt, diff_extra,
                            n_pairs=H_B, mask_shift=int(math.log2(CHUNK)), lam_init=lam_init)
            out_ab_p.append((ka.reshape(bp, tp, H_A, HEAD_DIM), va.reshape(bp, tp, H_A, HEAD_DIM), lf,
                             kb.reshape(bp, tp, H_B, 2, HEAD_DIM), vb.reshape(bp, tp, H_B, 2 * HEAD_DIM)))
            xp2 = _outproj_ln(xp.reshape(bp * tp, d), [oa.reshape(bp * tp, -1), ob.reshape(bp * tp, -1)],
                              [wout[:A_WIDTH], wout[A_WIDTH:]], ln1_g[i][None, :], ln1_b[i][None, :], alpha)

            tabs = _rope_tables(pos_s, HEAD_DIM, -1)
            (qa, ka, kab, va, _, lf, lfw, qb, kb, kbb, vb, _) = _proj_ab(xs, w6, wvt, wf, bf, tabs)
            rs = lambda a: a.reshape(bs, ts, a.shape[-1])
            cache_lfw = jnp.pad(cache_fox_logf[j].astype(F32), ((0, 0), (0, 0), (0, LANES - H_A)))
            bias = _decay_bias(_cat_pad_time(cache_lfw, rs(lfw), t_dec))
            flat = lambda c: c.reshape(bs, past, -1)
            oa = _decode_attention("fox", rs(qa), flat(cache_fox_k[j]), flat(cache_fox_v[j]), rs(kab), rs(va),
                                   (bias,))
            ob = _decode_attention("diff", rs(qb), flat(cache_diff_k[j]), flat(cache_diff_v[j]), rs(kbb),
                                   rs(vb), diff_extra, lam_init=lam_init)
            out_ab_s.append((ka.reshape(bs, ts, H_A, HEAD_DIM), va.reshape(bs, ts, H_A, HEAD_DIM),
                             lf.reshape(bs, ts, H_A), kb.reshape(bs, ts, H_B, 2, HEAD_DIM),
                             vb.reshape(bs, ts, H_B, 2 * HEAD_DIM)))
            xs2 = _outproj_ln(xs.reshape(ns, d), [oa.reshape(ns, -1), ob.reshape(ns, -1)],
                              [wout[:A_WIDTH], wout[A_WIDTH:]], ln1_g[i][None, :], ln1_b[i][None, :], alpha)
        else:
            wc = w_in_c[j]
            kr_cols = _pad_cols(wc[:, Q_RANK + KV_RANK:], LANES)
            win = jnp.concatenate([wc[:, :Q_RANK + KV_RANK], kr_cols], axis=1).astype(BF16)
            wuq3 = jnp.pad(mla_w_uq[j].reshape(Q_RANK, H_C, NOPE_DIM + ROPE_DIM),
                           ((0, 0), (0, 0), (0, LANES - NOPE_DIM - ROPE_DIM)))
            wuq = wuq3.reshape(Q_RANK, H_C * LANES).astype(BF16)
            r0, r1, r2 = NOPE_DIM, NOPE_DIM + ROPE_DIM // 2, NOPE_DIM + ROPE_DIM
            wrot = jnp.zeros_like(wuq3).at[:, :, r0:r1].set(-wuq3[:, :, r1:r2]).at[:, :, r1:r2].set(wuq3[:, :, r0:r1])
            wrot = wrot.reshape(Q_RANK, H_C * LANES).astype(BF16)
            q_tabs = lambda pos: (lambda c, s1, s2: (c, s2 - s1))(*_rope_tables(pos, ROPE_DIM, NOPE_DIM))
            wukv = mla_w_ukv[j].reshape(KV_RANK, H_C, NOPE_DIM + V_DIM_C)
            wk = jnp.pad(wukv[:, :, :NOPE_DIM], ((0, 0), (0, 0), (0, LANES - NOPE_DIM)))
            wk = wk.reshape(KV_RANK, H_C * LANES).astype(BF16)
            wvt = wukv[:, :, NOPE_DIM:].reshape(KV_RANK, H_C * V_DIM_C).T.astype(BF16)
            place = jnp.tile(_pad_cols(jnp.concatenate(
                [jnp.zeros((ROPE_DIM, NOPE_DIM), F32), jnp.eye(ROPE_DIM, dtype=F32)], axis=1), LANES),
                (1, H_C)).astype(BF16)
            gq = mla_q_norm[j][None, :]
            gkv = mla_kv_norm[j][None, :]
            wout = w_out_c[j].astype(BF16)

            q, ckv, kr = _proj_c(xp, win, gq, gkv, wuq, wrot, q_tabs(pos_p), _rope_tables(pos_p, ROPE_DIM, 0))
            kc, vct = _kv_up(ckv.reshape(bp * tp, KV_RANK), kr.reshape(bp * tp, ROPE_DIM), wk, place, wvt)
            per_seq = lambda a, nb: a.reshape((nb, a.shape[0] // nb) + a.shape[1:])
            oc = _attention("mla", q, per_seq(kc, bp), per_seq(vct, bp), (), n_pairs=H_C // 2,
                            mask_shift=int(math.log2(CHUNK)))
            out_c_p.append((ckv, kr))
            xp2 = _outproj_ln(xp.reshape(bp * tp, d), [oc.reshape(bp * tp, -1)], [wout],
                              ln1_g[i][None, :], ln1_b[i][None, :], alpha)

            q, ckv, kr = _proj_c(xs, win, gq, gkv, wuq, wrot, q_tabs(pos_s), _rope_tables(pos_s, ROPE_DIM, 0))
            w_abs = jnp.zeros((H_C, LANES, 2 * LANES), F32)
            w_abs = w_abs.at[:, :NOPE_DIM, :KV_RANK].set(jnp.transpose(wukv[:, :, :NOPE_DIM], (1, 2, 0)))
            w_abs = w_abs.at[:, NOPE_DIM:NOPE_DIM + ROPE_DIM, KV_RANK:KV_RANK + ROPE_DIM].set(
                jnp.eye(ROPE_DIM, dtype=F32))
            w_vout = jnp.einsum("khd,hg->hkgd", wukv[:, :, NOPE_DIM:], jnp.eye(H_C, dtype=F32))
            w_vout = w_vout.reshape(H_C, KV_RANK, H_C * V_DIM_C)
            wide = lambda a: jnp.pad(a.astype(F32), ((0, 0), (0, 0), (0, LANES - ROPE_DIM)))
            oc = _mla_decode(q.reshape(bs, ts, -1), cache_mla_ckv[j].astype(F32), wide(cache_mla_krope[j]),
                             ckv.reshape(bs, ts, KV_RANK), wide(kr.reshape(bs, ts, ROPE_DIM)),
                             w_abs.astype(BF16), w_vout.astype(BF16))
            out_c_s.append((ckv.reshape(bs, ts, KV_RANK), kr.reshape(bs, ts, ROPE_DIM)))
            xs2 = _outproj_ln(xs.reshape(ns, d), [oc.reshape(ns, -1)], [wout],
                              ln1_g[i][None, :], ln1_b[i][None, :], alpha)

        wr = _pad_cols(jnp.concatenate(
            [moe_wg[i]] + [moe_we[i][gi] for gi in range(N_GROUPS)], axis=1), LANES)
        wrh = wr.astype(BF16)
        wrl = (wr - wrh.astype(F32)).astype(BF16)
        br = _pad_cols(jnp.concatenate([moe_bg[i], moe_be[i].reshape(-1)])[None, :], LANES)
        moe_w = (moe_w1[i], moe_w3[i], moe_w2[i])
        g2, b2 = ln2_g[i][None, :], ln2_b[i][None, :]
        xp = _moe_ln(xp2, wrh, wrl, br, *moe_w, g2, b2, alpha).reshape(bp, tp, d)
        xs = _moe_ln(xs2, wrh, wrl, br, *moe_w, g2, b2, alpha).reshape(1, ns, d)

    stack = lambda rows, n: jnp.stack([r[n] for r in rows])
    return (xp, xs.reshape(bs, ts, d),
            stack(out_ab_p, 0), stack(out_ab_p, 1), stack(out_ab_p, 2), stack(out_ab_p, 3), stack(out_ab_p, 4),
            stack(out_c_p, 0), stack(out_c_p, 1),
            stack(out_ab_s, 0), stack(out_ab_s, 1), stack(out_ab_s, 2), stack(out_ab_s, 3), stack(out_ab_s, 4),
            stack(out_c_s, 0), stack(out_c_s, 1))
```

```python
import functools
import math

import jax
import jax.numpy as jnp
from jax import lax
from jax.experimental import pallas as pl
from jax.experimental.pallas import tpu as pltpu

F32 = jnp.float32
BF16 = jnp.bfloat16

D_MODEL = 1024
CHUNK = 64
HEAD_DIM = 64
ROPE_THETA = 10000.0
H_A = 8
H_B = 4
H_C = 16
Q_RANK = 256
KV_RANK = 128
NOPE_DIM = 64
ROPE_DIM = 32
V_DIM_C = 64
N_GROUPS = 4
EXPERTS_PER_GROUP = 4
N_EXPERTS = N_GROUPS * EXPERTS_PER_GROUP
D_EXPERT = 256
A_WIDTH = H_A * HEAD_DIM
B_QK_WIDTH = H_B * 2 * HEAD_DIM
B_V_WIDTH = H_B * 2 * HEAD_DIM
FGATE_BIAS = 3.0
LN_EPS = 1e-5
RMS_EPS = 1e-6
NEG_INF = -1e30
LOG2E = math.log2(math.e)

LANES = 128
BF16_ROWS = 16
PANEL = {"fox": 1024, "diff": 256, "mla": 1024}
BIAS_PIECES = 3
VMEM_LIMIT = 48 * 1024 * 1024
ATTN_VMEM_LIMIT = 56 * 1024 * 1024
ATTN_BLOCK = 512
Q_KEY_BLOCKS = 4
ROW_TILE = 512
MOE_TILE = 1024
EXPERTS_PER_STEP = 4
MOE_VMEM_LIMIT = 60 * 1024 * 1024
BIAS_ROWS_PER_STEP = 4096
GATE_LANE0 = N_GROUPS


def _cparams(sem, vmem_limit=VMEM_LIMIT):
    return pltpu.CompilerParams(dimension_semantics=sem, vmem_limit_bytes=vmem_limit)


def _rope3(x, c, s1, s2, shift_up, shift_down):
    return x * c + pltpu.roll(x, shift_up, 1) * s1 + pltpu.roll(x, shift_down, 1) * s2


def _layer_norm(y, g, b):
    mu = jnp.mean(y, axis=-1, keepdims=True)
    d = y - mu
    var = jnp.mean(d * d, axis=-1, keepdims=True)
    return d * lax.rsqrt(var + LN_EPS) * g + b


def _split3(x):
    hi = x.astype(BF16)
    r1 = x - hi.astype(F32)
    mid = r1.astype(BF16)
    return hi, mid, (r1 - mid.astype(F32)).astype(BF16)


def _proj_ab_kernel(x_ref, w_ref, wvt_ref, wf_ref, bf_ref, c_ref, s1_ref, s2_ref,
                    qa_ref, ka_ref, kab_ref, va_ref, vat_ref, lf_ref, lfw_ref,
                    qb_ref, kb_ref, kbb_ref, vb_ref, vbt_ref):
    xb = x_ref[0].astype(BF16)

    def mm(i):
        return jnp.dot(xb, w_ref[i], preferred_element_type=F32)

    def mm_t(i):
        return lax.dot_general(wvt_ref[i], xb, (((1,), (1,)), ((), ())), preferred_element_type=F32)

    qa_ref[0] = (mm(0) * (HEAD_DIM ** -0.5 * LOG2E)).astype(BF16)
    ka = mm(1)
    ka_ref[0] = ka
    kab_ref[0] = ka.astype(BF16)
    va_ref[0] = mm(2)
    vat_ref[0, 0] = mm_t(0).astype(BF16)

    z = jnp.dot(xb, wf_ref[...], preferred_element_type=F32) + bf_ref[...]
    lf = jnp.minimum(z, 0.0) - jnp.log1p(jnp.exp(-jnp.abs(z)))
    lf_ref[0] = lf[:, :H_A]
    lfw_ref[0] = lf

    c, s1, s2 = c_ref[...], s1_ref[...], s2_ref[...]
    qb = mm(3)
    kb = mm(4)
    for s in range(B_QK_WIDTH // LANES):
        sl = slice(s * LANES, (s + 1) * LANES)
        qs = _rope3(qb[:, sl], c, s1, s2, LANES - HEAD_DIM // 2, HEAD_DIM // 2)
        qb_ref[0, :, sl] = (qs * (HEAD_DIM ** -0.5 * LOG2E)).astype(BF16)
        ks = _rope3(kb[:, sl], c, s1, s2, LANES - HEAD_DIM // 2, HEAD_DIM // 2)
        kb_ref[0, :, sl] = ks
        kbb_ref[0, :, sl] = ks.astype(BF16)
    vb_ref[0] = mm(5)
    vbt_ref[0, 0] = mm_t(1).astype(BF16)


def _proj_ab(x, w6, wvt, wf, bf, tabs):
    nb, t, _ = x.shape
    tm = min(ROW_TILE, t)
    assert t % tm == 0
    w = A_WIDTH
    tok = lambda width: pl.BlockSpec((1, tm, width), lambda b, i: (b, i, 0))
    tr = pl.BlockSpec((1, 1, w, tm), lambda b, i: (b, i, 0, 0))
    tab = pl.BlockSpec((tm, LANES), lambda b, i: (i, 0))
    full = lambda a: pl.BlockSpec(a.shape, lambda b, i: (0,) * a.ndim)
    sds = lambda width, dt: jax.ShapeDtypeStruct((nb, t, width), dt)
    sds_t = jax.ShapeDtypeStruct((nb, t // tm, w, tm), BF16)
    return pl.pallas_call(
        _proj_ab_kernel,
        grid=(nb, t // tm),
        in_specs=[tok(D_MODEL), full(w6), full(wvt), full(wf), full(bf), tab, tab, tab],
        out_specs=[tok(w), tok(w), tok(w), tok(w), tr, tok(H_A), tok(LANES), tok(w), tok(w), tok(w), tok(w), tr],
        out_shape=[sds(w, BF16), sds(w, F32), sds(w, BF16), sds(w, F32), sds_t, sds(H_A, F32), sds(LANES, F32),
                   sds(w, BF16), sds(w, F32), sds(w, BF16), sds(w, F32), sds_t],
        compiler_params=_cparams(("parallel", "parallel")),
        name="proj_ab",
    )(x, w6, wvt, wf, bf, *tabs)


def _decay_bias_kernel(lf_ref, spread_ref, lower_ref, o_ref, carry_ref):
    @pl.when(pl.program_id(1) == 0)
    def _():
        carry_ref[...] = jnp.zeros_like(carry_ref)

    spread = spread_ref[...]
    lower = lower_ref[...]
    tc = lower.shape[0]
    lane = lax.broadcasted_iota(jnp.int32, (1, LANES), 1).astype(F32)
    piece = lane - BIAS_PIECES * jnp.floor((lane + 0.5) * (1.0 / BIAS_PIECES))
    carry = carry_ref[...]
    for r in range(lf_ref.shape[1] // tc):
        rows = slice(r * tc, (r + 1) * tc)
        x = lf_ref[0, rows, :]
        xr = sum(jnp.dot(p, spread, preferred_element_type=F32) for p in _split3(x))
        c = sum(jnp.dot(lower, p, preferred_element_type=F32) for p in _split3(xr)) + carry
        carry = c[tc - 1:tc, :]
        hi, mid, lo = (p.astype(F32) for p in _split3(c * (-LOG2E)))
        o_ref[0, rows, :] = jnp.where(piece == 0.0, hi, jnp.where(piece == 1.0, mid, lo)).astype(BF16)
    carry_ref[...] = carry


def _decay_bias(lf_wide):
    nb, t, _ = lf_wide.shape
    tc = min(ATTN_BLOCK, t)
    tb = min(BIAS_ROWS_PER_STEP, t)
    assert t % tb == 0 and tb % tc == 0
    spec = pl.BlockSpec((1, tb, LANES), lambda b, i: (b, i, 0))
    src = jnp.arange(LANES)[:, None]
    dst = jnp.arange(LANES)[None, :]
    spread = ((dst // BIAS_PIECES == src) & (src < H_A)).astype(BF16)
    lower = jnp.tril(jnp.ones((tc, tc), BF16))
    const = lambda a: pl.BlockSpec(a.shape, lambda b, i: (0, 0))
    return pl.pallas_call(
        _decay_bias_kernel,
        grid=(nb, t // tb),
        in_specs=[spec, const(spread), const(lower)],
        out_specs=spec,
        out_shape=jax.ShapeDtypeStruct((nb, t, LANES), BF16),
        scratch_shapes=[pltpu.VMEM((1, LANES), F32)],
        compiler_params=_cparams(("parallel", "arbitrary")),
        name="cumsum",
    )(lf_wide, spread, lower)


def _attn_kernel(*refs, mode, tq, tk, mask_shift, lam_init):
    if mode == "diff":
        q_ref, k_ref, vt_ref, lq1_ref, lk1_ref, lq2_ref, lk2_ref, sub_ref, o_ref = refs[:9]
    elif mode == "fox":
        q_ref, k_ref, vt_ref, b_ref, o_ref = refs[:5]
    else:
        q_ref, k_ref, vt_ref, o_ref = refs[:4]
    m_sc, acc_sc, sa, bma, sb, bmb = refs[-6:]
    v_rows = LANES if mode == "diff" else HEAD_DIM
    sa_sc, sb_sc = (sa, bma), (sb, bmb)

    qi = pl.program_id(2)
    q = q_ref[0]
    lane = lax.broadcasted_iota(jnp.int32, (1, LANES), 1)
    if mode == "mla":
        qs = [q[:, :LANES], q[:, LANES:]]
    else:
        zero = jnp.zeros_like(q)
        qs = [jnp.where(lane < HEAD_DIM, q, zero), jnp.where(lane >= HEAD_DIM, q, zero)]
        if mode == "fox":
            def pick(i):
                lo = BIAS_PIECES * (2 * pl.program_id(1) + i)
                hot = jnp.where((lane >= lo) & (lane < lo + BIAS_PIECES), 1.0, 0.0)
                return jnp.broadcast_to(hot, (tq, LANES)).astype(BF16)

            qs = [jnp.concatenate([qs[i], pick(i)], axis=1) for i in range(2)]

    m_sc[...] = jnp.full(m_sc.shape, NEG_INF, F32)
    acc_sc[...] = jnp.zeros(acc_sc.shape, F32)

    pw = min(PANEL[mode], tq)

    def scores(j, bufs, q0, q1):
        s_sc, bm_sc = bufs
        cs = slice(q0, q1)
        k = k_ref[0, j]
        if mode == "fox":
            k = jnp.concatenate([k, b_ref[0, j]], axis=1)
        for i in range(2):
            ki = k[:, i * LANES:(i + 1) * LANES] if mode == "mla" else k
            st = lax.dot_general(ki, qs[i][cs], (((1,), (1,)), ((), ())), preferred_element_type=F32)
            s_sc[i, :, cs] = st
            bm_sc[i, :, cs] = jnp.max(st, axis=0, keepdims=True)

    def consume(j, bufs, q0, q1, key0=None):
        s_sc, bm_sc = bufs
        cs = slice(q0, q1)
        vt = vt_ref[0, j]
        masked = key0 is not None and ((key0 + tk - 1) >> mask_shift) > (q0 >> mask_shift)
        for i in range(2):
            st = s_sc[i, :, cs]
            if masked:
                key = lax.broadcasted_iota(jnp.int32, (tk, q1 - q0), 0) + key0
                qry = lax.broadcasted_iota(jnp.int32, (tk, q1 - q0), 1) + q0
                vis = lax.shift_right_logical(key, mask_shift) <= lax.shift_right_logical(qry, mask_shift)
                st = jnp.where(vis, st, NEG_INF)
                blk_max = jnp.max(st, axis=0, keepdims=True)
            else:
                blk_max = bm_sc[i, :, cs]
            m_prev = m_sc[i, :, cs]
            m_new = jnp.maximum(m_prev, blk_max)
            alpha = jnp.exp2(m_prev - m_new)
            p = jnp.exp2(st - m_new).astype(BF16)
            vi = vt if mode == "diff" else vt[i * HEAD_DIM:(i + 1) * HEAD_DIM]
            vi = jnp.concatenate([vi, jnp.ones((BF16_ROWS, tk), BF16)], axis=0)
            acc_sc[i, :, cs] = alpha * acc_sc[i, :, cs] + jnp.dot(vi, p, preferred_element_type=F32)
            m_sc[i, :, cs] = m_new

    def stage(nxt, cur):
        for q0 in range(0, tq, pw):
            scores(nxt[0], nxt[1], q0, q0 + pw)
            consume(cur[0], cur[1], q0, q0 + pw)

    n_diag = tq // tk
    nfull = n_diag * qi
    bufs = (sa_sc, sb_sc)

    def trip(jj, carry):
        for d in range(n_diag):
            j = n_diag * jj + d
            stage((j + 1, bufs[(d + 1) % 2]), (j, bufs[d % 2]))
        return carry

    for q0 in range(0, tq, pw):
        scores(0, sa_sc, q0, q0 + pw)
    lax.fori_loop(0, qi, trip, 0)

    pt = min(pw, tk)
    sees = lambda d, q1: ((d * tk) >> mask_shift) <= ((q1 - 1) >> mask_shift)
    for d in range(n_diag):
        for q0 in range(0, tq, pt):
            if d + 1 < n_diag and sees(d + 1, q0 + pt):
                scores(nfull + d + 1, bufs[(d + 1) % 2], q0, q0 + pt)
            if sees(d, q0 + pt):
                consume(nfull + d, bufs[d % 2], q0, q0 + pt, key0=d * tk)

    if mode == "diff":
        lam = (jnp.exp(jnp.sum(lq1_ref[...] * lk1_ref[...], axis=1, keepdims=True))
               - jnp.exp(jnp.sum(lq2_ref[...] * lk2_ref[...], axis=1, keepdims=True)) + lam_init)
    for q0 in range(0, tq, pw):
        cs = slice(q0, q0 + pw)
        o0 = acc_sc[0, :v_rows, cs] / acc_sc[0, v_rows:v_rows + 1, cs]
        o1 = acc_sc[1, :v_rows, cs] / acc_sc[1, v_rows:v_rows + 1, cs]
        if mode == "diff":
            o = o0 - lam * o1
            ms = jnp.mean(o * o, axis=0, keepdims=True)
            o = (o * lax.rsqrt(ms + RMS_EPS)).T * sub_ref[...] * (1.0 - lam_init)
        else:
            o = jnp.concatenate([o0, o1], axis=0).T
        o_ref[0, cs, :] = o.astype(o_ref.dtype)


def _attention(mode, q, k, vt, extra, *, n_pairs, mask_shift, lam_init=0.0):
    nb, t_q, _ = q.shape
    _, nkb, tk, _ = k.shape
    tq = Q_KEY_BLOCKS * tk
    assert t_q % tq == 0 and nkb == t_q // tk
    qw = 2 * LANES if mode == "mla" else LANES
    in_specs = [
        pl.BlockSpec((1, tq, qw), lambda b, p, i: (b, i, p)),
        pl.BlockSpec((1, nkb, tk, qw), lambda b, p, i: (b, 0, 0, p)),
        pl.BlockSpec((1, nkb, LANES, tk), lambda b, p, i: (b, 0, p, 0)),
    ]
    if mode == "fox":
        in_specs.append(pl.BlockSpec((1, nkb, tk, LANES), lambda b, p, i: (b, 0, 0, 0)))
    elif mode == "diff":
        in_specs += [pl.BlockSpec(a.shape, lambda b, p, i: (0, 0)) for a in extra]
    kern = functools.partial(_attn_kernel, mode=mode, tq=tq, tk=tk, mask_shift=mask_shift, lam_init=lam_init)
    return pl.pallas_call(
        kern,
        grid=(nb, n_pairs, t_q // tq),
        in_specs=in_specs,
        out_specs=pl.BlockSpec((1, tq, LANES), lambda b, p, i: (b, i, p)),
        out_shape=jax.ShapeDtypeStruct((nb, t_q, n_pairs * LANES), BF16),
        scratch_shapes=[pltpu.VMEM((2, 1, tq), F32),
                        pltpu.VMEM((2, (LANES if mode == "diff" else HEAD_DIM) + BF16_ROWS, tq), F32),
                        pltpu.VMEM((2, tk, tq), F32), pltpu.VMEM((2, 1, tq), F32),
                        pltpu.VMEM((2, tk, tq), F32), pltpu.VMEM((2, 1, tq), F32)],
        compiler_params=_cparams(("parallel", "parallel", "arbitrary"), ATTN_VMEM_LIMIT),
        name="attn_" + mode,
    )(q, k, vt, *extra)


def _decode_attn_kernel(*refs, mode, lam_init):
    if mode == "fox":
        q_ref, kc_ref, vc_ref, kn_ref, vn_ref, b_ref, o_ref = refs
    else:
        q_ref, kc_ref, vc_ref, kn_ref, vn_ref, lq1_ref, lk1_ref, lq2_ref, lk2_ref, sub_ref, o_ref = refs
        lam = (jnp.exp(jnp.sum(lq1_ref[...] * lk1_ref[...], axis=1, keepdims=True))
               - jnp.exp(jnp.sum(lq2_ref[...] * lk2_ref[...], axis=1, keepdims=True)) + lam_init)
    ts = q_ref.shape[1]
    past = kc_ref.shape[1]
    lane = lax.broadcasted_iota(jnp.int32, (1, LANES), 1)
    row = lax.broadcasted_iota(jnp.int32, (ts, ts), 0)
    col = lax.broadcasted_iota(jnp.int32, (ts, ts), 1)
    nt = (((1,), (1,)), ((), ()))
    for p in range(q_ref.shape[2] // LANES):
        sl = slice(p * LANES, (p + 1) * LANES)
        q = q_ref[0, :, sl]
        kc = kc_ref[0, :, sl].astype(BF16)
        kn = kn_ref[0, :, sl]
        vc = vc_ref[0, :, sl].astype(BF16)
        vn = vn_ref[0, :, sl].astype(BF16)
        if mode == "fox":
            kc = jnp.concatenate([kc, b_ref[0, :past, :]], axis=1)
            kn = jnp.concatenate([kn, b_ref[0, past:past + ts, :]], axis=1)
        zero = jnp.zeros_like(q)
        outs = []
        for i in range(2):
            qi = jnp.where(lane < HEAD_DIM, q, zero) if i == 0 else jnp.where(lane >= HEAD_DIM, q, zero)
            if mode == "fox":
                lo = BIAS_PIECES * (2 * p + i)
                hot = jnp.where((lane >= lo) & (lane < lo + BIAS_PIECES), 1.0, 0.0)
                qi = jnp.concatenate([qi, jnp.broadcast_to(hot, (ts, LANES)).astype(BF16)], axis=1)
            sc = lax.dot_general(qi, kc, nt, preferred_element_type=F32)
            sn = lax.dot_general(qi, kn, nt, preferred_element_type=F32)
            if mode == "fox":
                sn = jnp.where(col <= row, sn, NEG_INF)
            m = jnp.maximum(jnp.max(sc, axis=1, keepdims=True), jnp.max(sn, axis=1, keepdims=True))
            pc = jnp.exp2(sc - m)
            pn = jnp.exp2(sn - m)
            l = jnp.sum(pc, axis=1, keepdims=True) + jnp.sum(pn, axis=1, keepdims=True)
            outs.append((jnp.dot(pc.astype(BF16), vc, preferred_element_type=F32)
                         + jnp.dot(pn.astype(BF16), vn, preferred_element_type=F32)) / l)
        if mode == "fox":
            o = jnp.where(lane < HEAD_DIM, outs[0], outs[1])
        else:
            o = outs[0] - lam * outs[1]
            ms = jnp.mean(o * o, axis=1, keepdims=True)
            o = o * lax.rsqrt(ms + RMS_EPS) * sub_ref[...] * (1.0 - lam_init)
        o_ref[0, :, sl] = o.astype(o_ref.dtype)


def _decode_attention(mode, q, k_cache, v_cache, k_new, v_new, extra, lam_init=0.0):
    nb, ts, w = q.shape
    assert k_cache.shape[1] % CHUNK == 0 and ts <= CHUNK
    per_b = lambda a: pl.BlockSpec((1,) + a.shape[1:], lambda b: (b, 0, 0))
    if mode == "fox":
        extra_specs = [per_b(extra[0])]
    else:
        extra_specs = [pl.BlockSpec(a.shape, lambda b: (0, 0)) for a in extra]
    return pl.pallas_call(
        functools.partial(_decode_attn_kernel, mode=mode, lam_init=lam_init),
        grid=(nb,),
        in_specs=[per_b(q), per_b(k_cache), per_b(v_cache), per_b(k_new), per_b(v_new)] + extra_specs,
        out_specs=pl.BlockSpec((1, ts, w), lambda b: (b, 0, 0)),
        out_shape=jax.ShapeDtypeStruct((nb, ts, w), BF16),
        compiler_params=_cparams(("parallel",)),
        name="decode_" + mode,
    )(q, k_cache, v_cache, k_new, v_new, *extra)


def _outproj_ln_kernel(*refs, n_in, alpha):
    x_ref = refs[0]
    o_refs = refs[1:1 + n_in]
    w_refs = refs[1 + n_in:1 + 2 * n_in]
    g_ref, b_ref, y_ref = refs[1 + 2 * n_in:]
    mix = jnp.dot(o_refs[0][...], w_refs[0][...], preferred_element_type=F32)
    for o_r, w_r in zip(o_refs[1:], w_refs[1:]):
        mix = mix + jnp.dot(o_r[...], w_r[...], preferred_element_type=F32)
    y_ref[...] = _layer_norm(alpha * x_ref[...] + mix, g_ref[...], b_ref[...])


def _outproj_ln(x, outs, ws, g, b, alpha):
    n, d = x.shape
    tm = min(MOE_TILE, n)
    assert n % tm == 0
    row = lambda width: pl.BlockSpec((tm, width), lambda i: (i, 0))
    full = lambda a: pl.BlockSpec(a.shape, lambda i: (0, 0))
    return pl.pallas_call(
        functools.partial(_outproj_ln_kernel, n_in=len(outs), alpha=alpha),
        grid=(n // tm,),
        in_specs=[row(d)] + [row(o.shape[1]) for o in outs] + [full(w) for w in ws] + [full(g), full(b)],
        out_specs=row(d),
        out_shape=jax.ShapeDtypeStruct((n, d), F32),
        compiler_params=_cparams(("parallel",)),
        name="outproj_ln",
    )(x, *outs, *ws, g, b)


def _route(logits):
    lane = lax.broadcasted_iota(jnp.int32, logits.shape, 1).astype(F32)
    big = float(1 << 20)
    is_g = lane < N_GROUPS
    lg = jnp.where(is_g, logits, NEG_INF)
    eg = jnp.where(is_g, jnp.exp(lg - jnp.max(lg, axis=1, keepdims=True)), 0.0)
    pg = eg / jnp.sum(eg, axis=1, keepdims=True)
    p_g = jnp.max(pg, axis=1, keepdims=True)
    gidx = jnp.min(jnp.where(is_g & (pg == p_g), lane, big), axis=1, keepdims=True)
    lo = GATE_LANE0 + EXPERTS_PER_GROUP * gidx
    sel = (lane >= lo) & (lane < lo + EXPERTS_PER_GROUP)
    le = jnp.where(sel, logits, NEG_INF)
    ee = jnp.where(sel, jnp.exp(le - jnp.max(le, axis=1, keepdims=True)), 0.0)
    pe = ee / jnp.sum(ee, axis=1, keepdims=True)
    v1 = jnp.max(jnp.where(sel, pe, -1.0), axis=1, keepdims=True)
    i1 = jnp.min(jnp.where(sel & (pe == v1), lane, big), axis=1, keepdims=True)
    rest = sel & (lane != i1)
    v2 = jnp.max(jnp.where(rest, pe, -1.0), axis=1, keepdims=True)
    i2 = jnp.min(jnp.where(rest & (pe == v2), lane, big), axis=1, keepdims=True)
    tot = v1 + v2
    w1 = v1 / tot * p_g
    w2 = v2 / tot * p_g
    return jnp.where(lane == i1, w1, jnp.where(lane == i2, w2, 0.0))


def _moe_ln_kernel(x_ref, wrh_ref, wrl_ref, br_ref, w1_ref, w3_ref, w2_ref, g_ref, b_ref, y_ref,
                   xb_sc, gate_sc, acc_sc, *, alpha):
    e = pl.program_id(1)

    @pl.when(e == 0)
    def _():
        x = x_ref[...]
        xh = x.astype(BF16)
        xl = (x - xh.astype(F32)).astype(BF16)
        xb_sc[...] = xh
        logits = (jnp.dot(xh, wrh_ref[...], preferred_element_type=F32)
                  + jnp.dot(xl, wrh_ref[...], preferred_element_type=F32)
                  + jnp.dot(xh, wrl_ref[...], preferred_element_type=F32) + br_ref[...])
        gate_sc[...] = _route(logits)
        acc_sc[...] = jnp.zeros_like(acc_sc)

    xb = xb_sc[...]
    lane = lax.broadcasted_iota(jnp.int32, (1, LANES), 1)
    for s in range(w1_ref.shape[0]):
        h1 = jnp.dot(xb, w1_ref[s].astype(BF16), preferred_element_type=F32)
        h3 = jnp.dot(xb, w3_ref[s].astype(BF16), preferred_element_type=F32)
        hdn = (h1 * jax.nn.sigmoid(h1)) * h3
        y = jnp.dot(hdn.astype(BF16), w2_ref[s].astype(BF16), preferred_element_type=F32)
        expert_lane = e * w1_ref.shape[0] + s + GATE_LANE0
        ge = jnp.sum(jnp.where(lane == expert_lane, gate_sc[...], 0.0), axis=1, keepdims=True)
        acc_sc[...] += ge * y

    @pl.when(e == pl.num_programs(1) - 1)
    def _():
        y_ref[...] = _layer_norm(alpha * x_ref[...] + acc_sc[...], g_ref[...], b_ref[...])


def _moe_ln(x, wrh, wrl, br, w1, w3, w2, g, b, alpha):
    n, d = x.shape
    tm = min(MOE_TILE, n)
    assert n % tm == 0
    ne = w1.shape[0]
    assert ne % EXPERTS_PER_STEP == 0
    per_expert = lambda a: pl.BlockSpec((EXPERTS_PER_STEP,) + a.shape[1:], lambda i, e: (e, 0, 0))
    full = lambda a: pl.BlockSpec(a.shape, lambda i, e: (0, 0))
    return pl.pallas_call(
        functools.partial(_moe_ln_kernel, alpha=alpha),
        grid=(n // tm, ne // EXPERTS_PER_STEP),
        in_specs=[pl.BlockSpec((tm, d), lambda i, e: (i, 0)), full(wrh), full(wrl), full(br),
                  per_expert(w1), per_expert(w3), per_expert(w2),
                  full(g), full(b)],
        out_specs=pl.BlockSpec((tm, d), lambda i, e: (i, 0)),
        out_shape=jax.ShapeDtypeStruct((n, d), F32),
        scratch_shapes=[pltpu.VMEM((tm, d), BF16), pltpu.VMEM((tm, LANES), F32), pltpu.VMEM((tm, d), F32)],
        compiler_params=_cparams(("parallel", "arbitrary"), MOE_VMEM_LIMIT),
        name="moe_ln",
    )(x, wrh, wrl, br, w1, w3, w2, g, b)


def _proj_c_kernel(x_ref, win_ref, gq_ref, gkv_ref, wuq_ref, wrot_ref, cq_ref, sq_ref,
                   ck_ref, s1k_ref, s2k_ref, q_ref, ckv_ref, kr_ref):
    xb = x_ref[0].astype(BF16)
    h = jnp.dot(xb, win_ref[...], preferred_element_type=F32)
    qa = h[:, :Q_RANK]
    kva = h[:, Q_RANK:Q_RANK + KV_RANK]
    krw = h[:, Q_RANK + KV_RANK:]
    qn = qa * lax.rsqrt(jnp.mean(qa * qa, axis=1, keepdims=True) + RMS_EPS) * gq_ref[...]
    ckv_ref[0] = kva * lax.rsqrt(jnp.mean(kva * kva, axis=1, keepdims=True) + RMS_EPS) * gkv_ref[...]
    half = ROPE_DIM // 2
    kr = _rope3(krw, ck_ref[...], s1k_ref[...], s2k_ref[...], LANES - half, half)
    kr_ref[0] = kr[:, :ROPE_DIM]
    qnb = qn.astype(BF16)
    q = jnp.dot(qnb, wuq_ref[...], preferred_element_type=F32)
    q_rot = jnp.dot(qnb, wrot_ref[...], preferred_element_type=F32)
    cq, sq = cq_ref[...], sq_ref[...]
    scale = (NOPE_DIM + ROPE_DIM) ** -0.5 * LOG2E
    for hd in range(H_C):
        sl = slice(hd * LANES, (hd + 1) * LANES)
        q_ref[0, :, sl] = ((q[:, sl] * cq + q_rot[:, sl] * sq) * scale).astype(BF16)


def _proj_c(x, win, gq, gkv, wuq, wrot, tabs_q, tabs_k):
    nb, t, _ = x.shape
    tm = min(ROW_TILE, t)
    assert t % tm == 0
    tok = lambda width: pl.BlockSpec((1, tm, width), lambda b, i: (b, i, 0))
    tab = pl.BlockSpec((tm, LANES), lambda b, i: (i, 0))
    full = lambda a: pl.BlockSpec(a.shape, lambda b, i: (0, 0))
    return pl.pallas_call(
        _proj_c_kernel,
        grid=(nb, t // tm),
        in_specs=[tok(D_MODEL), full(win), full(gq), full(gkv), full(wuq), full(wrot)] + [tab] * 5,
        out_specs=[tok(H_C * LANES), tok(KV_RANK), tok(ROPE_DIM)],
        out_shape=[jax.ShapeDtypeStruct((nb, t, H_C * LANES), BF16),
                   jax.ShapeDtypeStruct((nb, t, KV_RANK), F32),
                   jax.ShapeDtypeStruct((nb, t, ROPE_DIM), F32)],
        compiler_params=_cparams(("parallel", "parallel")),
        name="proj_c",
    )(x, win, gq, gkv, wuq, wrot, *tabs_q, *tabs_k)


def _kv_up_kernel(ckv_ref, kr_ref, wk_ref, place_ref, wvt_ref, k_ref, vt_ref):
    cb = ckv_ref[...].astype(BF16)
    k = (jnp.dot(cb, wk_ref[...], preferred_element_type=F32)
         + jnp.dot(kr_ref[...].astype(BF16), place_ref[...], preferred_element_type=F32))
    k_ref[0] = k.astype(BF16)
    vt = lax.dot_general(wvt_ref[...], cb, (((1,), (1,)), ((), ())), preferred_element_type=F32)
    vt_ref[0] = vt.astype(BF16)


def _kv_up(ckv, kr, wk, place, wvt):
    n = ckv.shape[0]
    tm = ATTN_BLOCK
    assert n % tm == 0
    row = lambda width: pl.BlockSpec((tm, width), lambda i: (i, 0))
    full = lambda a: pl.BlockSpec(a.shape, lambda i: (0, 0))
    return pl.pallas_call(
        _kv_up_kernel,
        grid=(n // tm,),
        in_specs=[row(KV_RANK), row(ROPE_DIM), full(wk), full(place), full(wvt)],
        out_specs=[pl.BlockSpec((1, tm, H_C * LANES), lambda i: (i, 0, 0)),
                   pl.BlockSpec((1, H_C * V_DIM_C, tm), lambda i: (i, 0, 0))],
        out_shape=[jax.ShapeDtypeStruct((n // tm, tm, H_C * LANES), BF16),
                   jax.ShapeDtypeStruct((n // tm, H_C * V_DIM_C, tm), BF16)],
        compiler_params=_cparams(("parallel",)),
        name="kv_up",
    )(ckv, kr, wk, place, wvt)


def _mla_decode_kernel(q_ref, ckv_ref, kr_ref, ckvn_ref, krn_ref, wabs_ref, wv_ref, o_ref):
    q = q_ref[0]
    ts = q.shape[0]
    qs = jnp.concatenate(
        [jnp.dot(q[:, h * LANES:(h + 1) * LANES], wabs_ref[h], preferred_element_type=F32).astype(BF16)
         for h in range(H_C)], axis=0)
    kc = jnp.concatenate([ckv_ref[0].astype(BF16), kr_ref[0].astype(BF16)], axis=1)
    kn = jnp.concatenate([ckvn_ref[0].astype(BF16), krn_ref[0].astype(BF16)], axis=1)
    nt = (((1,), (1,)), ((), ()))
    sc = lax.dot_general(qs, kc, nt, preferred_element_type=F32)
    sn = lax.dot_general(qs, kn, nt, preferred_element_type=F32)
    m = jnp.maximum(jnp.max(sc, axis=1, keepdims=True), jnp.max(sn, axis=1, keepdims=True))
    pc = jnp.exp2(sc - m)
    pn = jnp.exp2(sn - m)
    l = jnp.sum(pc, axis=1, keepdims=True) + jnp.sum(pn, axis=1, keepdims=True)
    ol = (jnp.dot(pc.astype(BF16), kc[:, :KV_RANK], preferred_element_type=F32)
          + jnp.dot(pn.astype(BF16), kn[:, :KV_RANK], preferred_element_type=F32)) / l
    olb = ol.astype(BF16)
    o = jnp.dot(olb[:ts], wv_ref[0], preferred_element_type=F32)
    for h in range(1, H_C):
        o = o + jnp.dot(olb[h * ts:(h + 1) * ts], wv_ref[h], preferred_element_type=F32)
    o_ref[0] = o.astype(o_ref.dtype)


def _mla_decode(q, ckv_c, kr_c, ckv_n, kr_n, w_abs, w_vout):
    nb, ts, _ = q.shape
    assert ckv_c.shape[1] % CHUNK == 0 and ts <= CHUNK
    per_b = lambda a: pl.BlockSpec((1,) + a.shape[1:], lambda b: (b, 0, 0))
    full = lambda a: pl.BlockSpec(a.shape, lambda b: (0, 0, 0))
    return pl.pallas_call(
        _mla_decode_kernel,
        grid=(nb,),
        in_specs=[per_b(q), per_b(ckv_c), per_b(kr_c), per_b(ckv_n), per_b(kr_n), full(w_abs), full(w_vout)],
        out_specs=pl.BlockSpec((1, ts, H_C * V_DIM_C), lambda b: (b, 0, 0)),
        out_shape=jax.ShapeDtypeStruct((nb, ts, H_C * V_DIM_C), BF16),
        compiler_params=_cparams(("parallel",)),
        name="mla_decode",
    )(q, ckv_c, kr_c, ckv_n, kr_n, w_abs, w_vout)


def _rope_tables(pos, dim, lane0):
    half = dim // 2
    inv = ROPE_THETA ** (-jnp.arange(0, dim, 2, dtype=F32) / dim)
    ang = pos.astype(F32)[:, None] * inv[None, :]
    cos, sin = jnp.cos(ang), jnp.sin(ang)
    zero = jnp.zeros_like(sin)
    c = jnp.concatenate([cos, cos], axis=1)
    s1 = jnp.concatenate([-sin, zero], axis=1)
    s2 = jnp.concatenate([zero, sin], axis=1)
    if lane0 < 0:
        reps = LANES // dim
        return tuple(jnp.tile(a, (1, reps)) for a in (c, s1, s2))
    t = pos.shape[0]
    pad = lambda a, fill: jnp.concatenate(
        [jnp.full((t, lane0), fill, F32), a, jnp.full((t, LANES - lane0 - dim), fill, F32)], axis=1)
    return pad(c, 1.0), pad(s1, 0.0), pad(s2, 0.0)


def _pad_cols(a, width):
    return jnp.pad(a, ((0, 0), (0, width - a.shape[1])))


def _blocks(a, tk):
    nb, t, l = a.shape
    return a.reshape(nb, t // tk, tk, l)


def _cat_pad_time(cache, new, t_pad):
    nb, t0, l = cache.shape
    t1 = new.shape[1]
    return jnp.concatenate([cache, new, jnp.zeros((nb, t_pad - t0 - t1, l), cache.dtype)], axis=1)


def kernel(x_prompt, x_sample, cache_fox_k, cache_fox_v, cache_fox_logf, cache_diff_k, cache_diff_v, cache_mla_ckv, cache_mla_krope, w_in_ab, b_fgate, diff_lq1, diff_lk1, diff_lq2, diff_lk2, diff_subln, w_out_ab, w_in_c, mla_q_norm, mla_kv_norm, mla_w_uq, mla_w_ukv, w_out_c, ln1_g, ln1_b, ln2_g, ln2_b, moe_wg, moe_bg, moe_we, moe_be, moe_w1, moe_w3, moe_w2):
    bp, tp, d = x_prompt.shape
    bs, ts, _ = x_sample.shape
    past = cache_fox_k.shape[2]
    depth = ln1_g.shape[0]
    alpha = (2 * depth) ** 0.25
    tk = ATTN_BLOCK
    assert past % tk == 0
    ns = bs * ts
    t_dec = past + tk

    pos_p = jnp.arange(tp)
    pos_s = jnp.tile(past + jnp.arange(ts), bs)

    xp = x_prompt
    xs = x_sample.reshape(1, ns, d)
    out_ab_p, out_ab_s, out_c_p, out_c_s = [], [], [], []

    for i in range(depth):
        j = i // 2
        if i % 2 == 0:
            lam_init = 0.8 - 0.6 * math.exp(-0.3 * i)
            cuts = [0, A_WIDTH, 2 * A_WIDTH, 3 * A_WIDTH, 3 * A_WIDTH + H_A,
                    3 * A_WIDTH + H_A + B_QK_WIDTH, 3 * A_WIDTH + H_A + 2 * B_QK_WIDTH,
                    3 * A_WIDTH + H_A + 2 * B_QK_WIDTH + B_V_WIDTH]
            w = w_in_ab[j]
            piece = lambda a: w[:, cuts[a]:cuts[a + 1]]
            w6 = jnp.stack([piece(0), piece(1), piece(2), piece(4), piece(5), piece(6)]).astype(BF16)
            wvt = jnp.stack([piece(2).T, piece(6).T]).astype(BF16)
            wf = _pad_cols(piece(3), LANES).astype(BF16)
            bf = _pad_cols(b_fgate[j][None, :], LANES)
            wout = w_out_ab[j].astype(BF16)
            diff_extra = (diff_lq1[j][None, :], diff_lk1[j][None, :], diff_lq2[j][None, :],
                          diff_lk2[j][None, :], diff_subln[j][None, :])

            tabs = _rope_tables(pos_p, HEAD_DIM, -1)
            (qa, ka, kab, va, vat, lf, lfw, qb, kb, kbb, vb, vbt) = _proj_ab(xp, w6, wvt, wf, bf, tabs)
            bias = _blocks(_decay_bias(lfw), tk)
            oa = _attention("fox", qa, _blocks(kab, tk), vat, (bias,), n_pairs=H_A // 2, mask_shift=0)
            ob = _attention("diff", qb, _blocks(kbb, tk), vbt, diff_extra,
                            n_pairs=H_B, mask_shift=int(math.log2(CHUNK)), lam_init=lam_init)
            out_ab_p.append((ka.reshape(bp, tp, H_A, HEAD_DIM), va.reshape(bp, tp, H_A, HEAD_DIM), lf,
                             kb.reshape(bp, tp, H_B, 2, HEAD_DIM), vb.reshape(bp, tp, H_B, 2 * HEAD_DIM)))
            xp2 = _outproj_ln(xp.reshape(bp * tp, d), [oa.reshape(bp * tp, -1), ob.reshape(bp * tp, -1)],
                              [wout[:A_WIDTH], wout[A_WIDTH:]], ln1_g[i][None, :], ln1_b[i][None, :], alpha)

            tabs = _rope_tables(pos_s, HEAD_DIM, -1)
            (qa, ka, kab, va, _, lf, lfw, qb, kb, kbb, vb, _) = _proj_ab(xs, w6, wvt, wf, bf, tabs)
            rs = lambda a: a.reshape(bs, ts, a.shape[-1])
            cache_lfw = jnp.pad(cache_fox_logf[j].astype(F32), ((0, 0), (0, 0), (0, LANES - H_A)))
            bias = _decay_bias(_cat_pad_time(cache_lfw, rs(lfw), t_dec))
            flat = lambda c: c.reshape(bs, past, -1)
            oa = _decode_attention("fox", rs(qa), flat(cache_fox_k[j]), flat(cache_fox_v[j]), rs(kab), rs(va),
                                   (bias,))
            ob = _decode_attention("diff", rs(qb), flat(cache_diff_k[j]), flat(cache_diff_v[j]), rs(kbb),
                                   rs(vb), diff_extra, lam_init=lam_init)
            out_ab_s.append((ka.reshape(bs, ts, H_A, HEAD_DIM), va.reshape(bs, ts, H_A, HEAD_DIM),
                             lf.reshape(bs, ts, H_A), kb.reshape(bs, ts, H_B, 2, HEAD_DIM),
                             vb.reshape(bs, ts, H_B, 2 * HEAD_DIM)))
            xs2 = _outproj_ln(xs.reshape(ns, d), [oa.reshape(ns, -1), ob.reshape(ns, -1)],
                              [wout[:A_WIDTH], wout[A_WIDTH:]], ln1_g[i][None, :], ln1_b[i][None, :], alpha)
        else:
            wc = w_in_c[j]
            kr_cols = _pad_cols(wc[:, Q_RANK + KV_RANK:], LANES)
            win = jnp.concatenate([wc[:, :Q_RANK + KV_RANK], kr_cols], axis=1).astype(BF16)
            wuq3 = jnp.pad(mla_w_uq[j].reshape(Q_RANK, H_C, NOPE_DIM + ROPE_DIM),
                           ((0, 0), (0, 0), (0, LANES - NOPE_DIM - ROPE_DIM)))
            wuq = wuq3.reshape(Q_RANK, H_C * LANES).astype(BF16)
            r0, r1, r2 = NOPE_DIM, NOPE_DIM + ROPE_DIM // 2, NOPE_DIM + ROPE_DIM
            wrot = jnp.zeros_like(wuq3).at[:, :, r0:r1].set(-wuq3[:, :, r1:r2]).at[:, :, r1:r2].set(wuq3[:, :, r0:r1])
            wrot = wrot.reshape(Q_RANK, H_C * LANES).astype(BF16)
            q_tabs = lambda pos: (lambda c, s1, s2: (c, s2 - s1))(*_rope_tables(pos, ROPE_DIM, NOPE_DIM))
            wukv = mla_w_ukv[j].reshape(KV_RANK, H_C, NOPE_DIM + V_DIM_C)
            wk = jnp.pad(wukv[:, :, :NOPE_DIM], ((0, 0), (0, 0), (0, LANES - NOPE_DIM)))
            wk = wk.reshape(KV_RANK, H_C * LANES).astype(BF16)
            wvt = wukv[:, :, NOPE_DIM:].reshape(KV_RANK, H_C * V_DIM_C).T.astype(BF16)
            place = jnp.tile(_pad_cols(jnp.concatenate(
                [jnp.zeros((ROPE_DIM, NOPE_DIM), F32), jnp.eye(ROPE_DIM, dtype=F32)], axis=1), LANES),
                (1, H_C)).astype(BF16)
            gq = mla_q_norm[j][None, :]
            gkv = mla_kv_norm[j][None, :]
            wout = w_out_c[j].astype(BF16)

            q, ckv, kr = _proj_c(xp, win, gq, gkv, wuq, wrot, q_tabs(pos_p), _rope_tables(pos_p, ROPE_DIM, 0))
            kc, vct = _kv_up(ckv.reshape(bp * tp, KV_RANK), kr.reshape(bp * tp, ROPE_DIM), wk, place, wvt)
            per_seq = lambda a, nb: a.reshape((nb, a.shape[0] // nb) + a.shape[1:])
            oc = _attention("mla", q, per_seq(kc, bp), per_seq(vct, bp), (), n_pairs=H_C // 2,
                            mask_shift=int(math.log2(CHUNK)))
            out_c_p.append((ckv, kr))
            xp2 = _outproj_ln(xp.reshape(bp * tp, d), [oc.reshape(bp * tp, -1)], [wout],
                              ln1_g[i][None, :], ln1_b[i][None, :], alpha)

            q, ckv, kr = _proj_c(xs, win, gq, gkv, wuq, wrot, q_tabs(pos_s), _rope_tables(pos_s, ROPE_DIM, 0))
            w_abs = jnp.zeros((H_C, LANES, 2 * LANES), F32)
            w_abs = w_abs.at[:, :NOPE_DIM, :KV_RANK].set(jnp.transpose(wukv[:, :, :NOPE_DIM], (1, 2, 0)))
            w_abs = w_abs.at[:, NOPE_DIM:NOPE_DIM + ROPE_DIM, KV_RANK:KV_RANK + ROPE_DIM].set(
                jnp.eye(ROPE_DIM, dtype=F32))
            w_vout = jnp.einsum("khd,hg->hkgd", wukv[:, :, NOPE_DIM:], jnp.eye(H_C, dtype=F32))
            w_vout = w_vout.reshape(H_C, KV_RANK, H_C * V_DIM_C)
            wide = lambda a: jnp.pad(a.astype(F32), ((0, 0), (0, 0), (0, LANES - ROPE_DIM)))
            oc = _mla_decode(q.reshape(bs, ts, -1), cache_mla_ckv[j].astype(F32), wide(cache_mla_krope[j]),
                             ckv.reshape(bs, ts, KV_RANK), wide(kr.reshape(bs, ts, ROPE_DIM)),
                             w_abs.astype(BF16), w_vout.astype(BF16))
            out_c_s.append((ckv.reshape(bs, ts, KV_RANK), kr.reshape(bs, ts, ROPE_DIM)))
            xs2 = _outproj_ln(xs.reshape(ns, d), [oc.reshape(ns, -1)], [wout],
                              ln1_g[i][None, :], ln1_b[i][None, :], alpha)

        wr = _pad_cols(jnp.concatenate(
            [moe_wg[i]] + [moe_we[i][gi] for gi in range(N_GROUPS)], axis=1), LANES)
        wrh = wr.astype(BF16)
        wrl = (wr - wrh.astype(F32)).astype(BF16)
        br = _pad_cols(jnp.concatenate([moe_bg[i], moe_be[i].reshape(-1)])[None, :], LANES)
        moe_w = (moe_w1[i], moe_w3[i], moe_w2[i])
        g2, b2 = ln2_g[i][None, :], ln2_b[i][None, :]
        xp = _moe_ln(xp2, wrh, wrl, br, *moe_w, g2, b2, alpha).reshape(bp, tp, d)
        xs = _moe_ln(xs2, wrh, wrl, br, *moe_w, g2, b2, alpha).reshape(1, ns, d)

    stack = lambda rows, n: jnp.stack([r[n] for r in rows])
    return (xp, xs.reshape(bs, ts, d),
            stack(out_ab_p, 0), stack(out_ab_p, 1), stack(out_ab_p, 2), stack(out_ab_p, 3), stack(out_ab_p, 4),
            stack(out_c_p, 0), stack(out_c_p, 1),
            stack(out_ab_s, 0), stack(out_ab_s, 1), stack(out_ab_s, 2), stack(out_ab_s, 3), stack(out_ab_s, 4),
            stack(out_c_s, 0), stack(out_c_s, 1))
```

```python
import functools
import math

import jax
import jax.numpy as jnp
from jax import lax
from jax.experimental import pallas as pl
from jax.experimental.pallas import tpu as pltpu

F32 = jnp.float32
BF16 = jnp.bfloat16

D_MODEL = 1024
CHUNK = 64
HEAD_DIM = 64
ROPE_THETA = 10000.0
H_A = 8
H_B = 4
H_C = 16
Q_RANK = 256
KV_RANK = 128
NOPE_DIM = 64
ROPE_DIM = 32
V_DIM_C = 64
N_GROUPS = 4
EXPERTS_PER_GROUP = 4
N_EXPERTS = N_GROUPS * EXPERTS_PER_GROUP
D_EXPERT = 256
A_WIDTH = H_A * HEAD_DIM
B_QK_WIDTH = H_B * 2 * HEAD_DIM
B_V_WIDTH = H_B * 2 * HEAD_DIM
FGATE_BIAS = 3.0
LN_EPS = 1e-5
RMS_EPS = 1e-6
NEG_INF = -1e30
LOG2E = math.log2(math.e)

LANES = 128
BF16_ROWS = 16
PANEL = {"fox": 1024, "diff": 256, "mla": 1024}
BIAS_PIECES = 3
VMEM_LIMIT = 48 * 1024 * 1024
ATTN_VMEM_LIMIT = 56 * 1024 * 1024
ATTN_BLOCK = 512
Q_KEY_BLOCKS = 4
ROW_TILE = 512
MOE_TILE = 1024
EXPERTS_PER_STEP = 4
MOE_VMEM_LIMIT = 60 * 1024 * 1024
BIAS_ROWS_PER_STEP = 4096
GATE_LANE0 = N_GROUPS


def _cparams(sem, vmem_limit=VMEM_LIMIT):
    return pltpu.CompilerParams(dimension_semantics=sem, vmem_limit_bytes=vmem_limit)


def _rope3(x, c, s1, s2, shift_up, shift_down):
    return x * c + pltpu.roll(x, shift_up, 1) * s1 + pltpu.roll(x, shift_down, 1) * s2


def _layer_norm(y, g, b):
    mu = jnp.mean(y, axis=-1, keepdims=True)
    d = y - mu
    var = jnp.mean(d * d, axis=-1, keepdims=True)
    return d * lax.rsqrt(var + LN_EPS) * g + b


def _split3(x):
    hi = x.astype(BF16)
    r1 = x - hi.astype(F32)
    mid = r1.astype(BF16)
    return hi, mid, (r1 - mid.astype(F32)).astype(BF16)


def _proj_ab_kernel(x_ref, w_ref, wvt_ref, wf_ref, bf_ref, c_ref, s1_ref, s2_ref,
                    qa_ref, ka_ref, kab_ref, va_ref, vat_ref, lf_ref, lfw_ref,
                    qb_ref, kb_ref, kbb_ref, vb_ref, vbt_ref):
    xb = x_ref[0].astype(BF16)

    def mm(i):
        return jnp.dot(xb, w_ref[i], preferred_element_type=F32)

    def mm_t(i):
        return lax.dot_general(wvt_ref[i], xb, (((1,), (1,)), ((), ())), preferred_element_type=F32)

    qa_ref[0] = (mm(0) * (HEAD_DIM ** -0.5 * LOG2E)).astype(BF16)
    ka = mm(1)
    ka_ref[0] = ka
    kab_ref[0] = ka.astype(BF16)
    va_ref[0] = mm(2)
    vat_ref[0, 0] = mm_t(0).astype(BF16)

    z = jnp.dot(xb, wf_ref[...], preferred_element_type=F32) + bf_ref[...]
    lf = jnp.minimum(z, 0.0) - jnp.log1p(jnp.exp(-jnp.abs(z)))
    lf_ref[0] = lf[:, :H_A]
    lfw_ref[0] = lf

    c, s1, s2 = c_ref[...], s1_ref[...], s2_ref[...]
    qb = mm(3)
    kb = mm(4)
    for s in range(B_QK_WIDTH // LANES):
        sl = slice(s * LANES, (s + 1) * LANES)
        qs = _rope3(qb[:, sl], c, s1, s2, LANES - HEAD_DIM // 2, HEAD_DIM // 2)
        qb_ref[0, :, sl] = (qs * (HEAD_DIM ** -0.5 * LOG2E)).astype(BF16)
        ks = _rope3(kb[:, sl], c, s1, s2, LANES - HEAD_DIM // 2, HEAD_DIM // 2)
        kb_ref[0, :, sl] = ks
        kbb_ref[0, :, sl] = ks.astype(BF16)
    vb_ref[0] = mm(5)
    vbt_ref[0, 0] = mm_t(1).astype(BF16)


def _proj_ab(x, w6, wvt, wf, bf, tabs):
    nb, t, _ = x.shape
    tm = min(ROW_TILE, t)
    assert t % tm == 0
    w = A_WIDTH
    tok = lambda width: pl.BlockSpec((1, tm, width), lambda b, i: (b, i, 0))
    tr = pl.BlockSpec((1, 1, w, tm), lambda b, i: (b, i, 0, 0))
    tab = pl.BlockSpec((tm, LANES), lambda b, i: (i, 0))
    full = lambda a: pl.BlockSpec(a.shape, lambda b, i: (0,) * a.ndim)
    sds = lambda width, dt: jax.ShapeDtypeStruct((nb, t, width), dt)
    sds_t = jax.ShapeDtypeStruct((nb, t // tm, w, tm), BF16)
    return pl.pallas_call(
        _proj_ab_kernel,
        grid=(nb, t // tm),
        in_specs=[tok(D_MODEL), full(w6), full(wvt), full(wf), full(bf), tab, tab, tab],
        out_specs=[tok(w), tok(w), tok(w), tok(w), tr, tok(H_A), tok(LANES), tok(w), tok(w), tok(w), tok(w), tr],
        out_shape=[sds(w, BF16), sds(w, F32), sds(w, BF16), sds(w, F32), sds_t, sds(H_A, F32), sds(LANES, F32),
                   sds(w, BF16), sds(w, F32), sds(w, BF16), sds(w, F32), sds_t],
        compiler_params=_cparams(("parallel", "parallel")),
        name="proj_ab",
    )(x, w6, wvt, wf, bf, *tabs)


def _decay_bias_kernel(lf_ref, spread_ref, lower_ref, o_ref, carry_ref):
    @pl.when(pl.program_id(1) == 0)
    def _():
        carry_ref[...] = jnp.zeros_like(carry_ref)

    spread = spread_ref[...]
    lower = lower_ref[...]
    tc = lower.shape[0]
    lane = lax.broadcasted_iota(jnp.int32, (1, LANES), 1).astype(F32)
    piece = lane - BIAS_PIECES * jnp.floor((lane + 0.5) * (1.0 / BIAS_PIECES))
    carry = carry_ref[...]
    for r in range(lf_ref.shape[1] // tc):
        rows = slice(r * tc, (r + 1) * tc)
        x = lf_ref[0, rows, :]
        xr = sum(jnp.dot(p, spread, preferred_element_type=F32) for p in _split3(x))
        c = sum(jnp.dot(lower, p, preferred_element_type=F32) for p in _split3(xr)) + carry
        carry = c[tc - 1:tc, :]
        hi, mid, lo = (p.astype(F32) for p in _split3(c * (-LOG2E)))
        o_ref[0, rows, :] = jnp.where(piece == 0.0, hi, jnp.where(piece == 1.0, mid, lo)).astype(BF16)
    carry_ref[...] = carry


def _decay_bias(lf_wide):
    nb, t, _ = lf_wide.shape
    tc = min(ATTN_BLOCK, t)
    tb = min(BIAS_ROWS_PER_STEP, t)
    assert t % tb == 0 and tb % tc == 0
    spec = pl.BlockSpec((1, tb, LANES), lambda b, i: (b, i, 0))
    src = jnp.arange(LANES)[:, None]
    dst = jnp.arange(LANES)[None, :]
    spread = ((dst // BIAS_PIECES == src) & (src < H_A)).astype(BF16)
    lower = jnp.tril(jnp.ones((tc, tc), BF16))
    const = lambda a: pl.BlockSpec(a.shape, lambda b, i: (0, 0))
    return pl.pallas_call(
        _decay_bias_kernel,
        grid=(nb, t // tb),
        in_specs=[spec, const(spread), const(lower)],
        out_specs=spec,
        out_shape=jax.ShapeDtypeStruct((nb, t, LANES), BF16),
        scratch_shapes=[pltpu.VMEM((1, LANES), F32)],
        compiler_params=_cparams(("parallel", "arbitrary")),
        name="cumsum",
    )(lf_wide, spread, lower)


def _attn_kernel(*refs, mode, tq, tk, mask_shift, lam_init):
    if mode == "diff":
        q_ref, k_ref, vt_ref, lq1_ref, lk1_ref, lq2_ref, lk2_ref, sub_ref, o_ref = refs[:9]
    elif mode == "fox":
        q_ref, k_ref, vt_ref, b_ref, o_ref = refs[:5]
    else:
        q_ref, k_ref, vt_ref, o_ref = refs[:4]
    m_sc, acc_sc, sa, bma, sb, bmb = refs[-6:]
    v_rows = LANES if mode == "diff" else HEAD_DIM
    sa_sc, sb_sc = (sa, bma), (sb, bmb)

    qi = pl.program_id(2)
    q = q_ref[0]
    lane = lax.broadcasted_iota(jnp.int32, (1, LANES), 1)
    if mode == "mla":
        qs = [q[:, :LANES], q[:, LANES:]]
    else:
        zero = jnp.zeros_like(q)
        qs = [jnp.where(lane < HEAD_DIM, q, zero), jnp.where(lane >= HEAD_DIM, q, zero)]
        if mode == "fox":
            def pick(i):
                lo = BIAS_PIECES * (2 * pl.program_id(1) + i)
                hot = jnp.where((lane >= lo) & (lane < lo + BIAS_PIECES), 1.0, 0.0)
                return jnp.broadcast_to(hot, (tq, LANES)).astype(BF16)

            qs = [jnp.concatenate([qs[i], pick(i)], axis=1) for i in range(2)]

    m_sc[...] = jnp.full(m_sc.shape, NEG_INF, F32)
    acc_sc[...] = jnp.zeros(acc_sc.shape, F32)

    pw = min(PANEL[mode], tq)

    def scores(j, bufs, q0, q1):
        s_sc, bm_sc = bufs
        cs = slice(q0, q1)
        k = k_ref[0, j]
        if mode == "fox":
            k = jnp.concatenate([k, b_ref[0, j]], axis=1)
        for i in range(2):
            ki = k[:, i * LANES:(i + 1) * LANES] if mode == "mla" else k
            st = lax.dot_general(ki, qs[i][cs], (((1,), (1,)), ((), ())), preferred_element_type=F32)
            s_sc[i, :, cs] = st
            bm_sc[i, :, cs] = jnp.max(st, axis=0, keepdims=True)

    def consume(j, bufs, q0, q1, key0=None):
        s_sc, bm_sc = bufs
        cs = slice(q0, q1)
        vt = vt_ref[0, j]
        masked = key0 is not None and ((key0 + tk - 1) >> mask_shift) > (q0 >> mask_shift)
        for i in range(2):
            st = s_sc[i, :, cs]
            if masked:
                key = lax.broadcasted_iota(jnp.int32, (tk, q1 - q0), 0) + key0
                qry = lax.broadcasted_iota(jnp.int32, (tk, q1 - q0), 1) + q0
                vis = lax.shift_right_logical(key, mask_shift) <= lax.shift_right_logical(qry, mask_shift)
                st = jnp.where(vis, st, NEG_INF)
                blk_max = jnp.max(st, axis=0, keepdims=True)
            else:
                blk_max = bm_sc[i, :, cs]
            m_prev = m_sc[i, :, cs]
            m_new = jnp.maximum(m_prev, blk_max)
            alpha = jnp.exp2(m_prev - m_new)
            p = jnp.exp2(st - m_new).astype(BF16)
            vi = vt if mode == "diff" else vt[i * HEAD_DIM:(i + 1) * HEAD_DIM]
            vi = jnp.concatenate([vi, jnp.ones((BF16_ROWS, tk), BF16)], axis=0)
            acc_sc[i, :, cs] = alpha * acc_sc[i, :, cs] + jnp.dot(vi, p, preferred_element_type=F32)
            m_sc[i, :, cs] = m_new

    def stage(nxt, cur):
        for q0 in range(0, tq, pw):
            scores(nxt[0], nxt[1], q0, q0 + pw)
            consume(cur[0], cur[1], q0, q0 + pw)

    n_diag = tq // tk
    nfull = n_diag * qi
    bufs = (sa_sc, sb_sc)

    def trip(jj, carry):
        for d in range(n_diag):
            j = n_diag * jj + d
            stage((j + 1, bufs[(d + 1) % 2]), (j, bufs[d % 2]))
        return carry

    for q0 in range(0, tq, pw):
        scores(0, sa_sc, q0, q0 + pw)
    lax.fori_loop(0, qi, trip, 0)

    pt = min(pw, tk)
    sees = lambda d, q1: ((d * tk) >> mask_shift) <= ((q1 - 1) >> mask_shift)
    for d in range(n_diag):
        for q0 in range(0, tq, pt):
            if d + 1 < n_diag and sees(d + 1, q0 + pt):
                scores(nfull + d + 1, bufs[(d + 1) % 2], q0, q0 + pt)
            if sees(d, q0 + pt):
                consume(nfull + d, bufs[d % 2], q0, q0 + pt, key0=d * tk)

    if mode == "diff":
        lam = (jnp.exp(jnp.sum(lq1_ref[...] * lk1_ref[...], axis=1, keepdims=True))
               - jnp.exp(jnp.sum(lq2_ref[...] * lk2_ref[...], axis=1, keepdims=True)) + lam_init)
    for q0 in range(0, tq, pw):
        cs = slice(q0, q0 + pw)
        o0 = acc_sc[0, :v_rows, cs] / acc_sc[0, v_rows:v_rows + 1, cs]
        o1 = acc_sc[1, :v_rows, cs] / acc_sc[1, v_rows:v_rows + 1, cs]
        if mode == "diff":
            o = o0 - lam * o1
            ms = jnp.mean(o * o, axis=0, keepdims=True)
            o = (o * lax.rsqrt(ms + RMS_EPS)).T * sub_ref[...] * (1.0 - lam_init)
        else:
            o = jnp.concatenate([o0, o1], axis=0).T
        o_ref[0, cs, :] = o.astype(o_ref.dtype)


def _attention(mode, q, k, vt, extra, *, n_pairs, mask_shift, lam_init=0.0):
    nb, t_q, _ = q.shape
    _, nkb, tk, _ = k.shape
    tq = Q_KEY_BLOCKS * tk
    assert t_q % tq == 0 and nkb == t_q // tk
    qw = 2 * LANES if mode == "mla" else LANES
    in_specs = [
        pl.BlockSpec((1, tq, qw), lambda b, p, i: (b, i, p)),
        pl.BlockSpec((1, nkb, tk, qw), lambda b, p, i: (b, 0, 0, p)),
        pl.BlockSpec((1, nkb, LANES, tk), lambda b, p, i: (b, 0, p, 0)),
    ]
    if mode == "fox":
        in_specs.append(pl.BlockSpec((1, nkb, tk, LANES), lambda b, p, i: (b, 0, 0, 0)))
    elif mode == "diff":
        in_specs += [pl.BlockSpec(a.shape, lambda b, p, i: (0, 0)) for a in extra]
    kern = functools.partial(_attn_kernel, mode=mode, tq=tq, tk=tk, mask_shift=mask_shift, lam_init=lam_init)
    return pl.pallas_call(
        kern,
        grid=(nb, n_pairs, t_q // tq),
        in_specs=in_specs,
        out_specs=pl.BlockSpec((1, tq, LANES), lambda b, p, i: (b, i, p)),
        out_shape=jax.ShapeDtypeStruct((nb, t_q, n_pairs * LANES), BF16),
        scratch_shapes=[pltpu.VMEM((2, 1, tq), F32),
                        pltpu.VMEM((2, (LANES if mode == "diff" else HEAD_DIM) + BF16_ROWS, tq), F32),
                        pltpu.VMEM((2, tk, tq), F32), pltpu.VMEM((2, 1, tq), F32),
                        pltpu.VMEM((2, tk, tq), F32), pltpu.VMEM((2, 1, tq), F32)],
        compiler_params=_cparams(("parallel", "parallel", "arbitrary"), ATTN_VMEM_LIMIT),
        name="attn_" + mode,
    )(q, k, vt, *extra)


def _decode_attn_kernel(*refs, mode, lam_init):
    if mode == "fox":
        q_ref, kc_ref, vc_ref, kn_ref, vn_ref, b_ref, o_ref = refs
    else:
        q_ref, kc_ref, vc_ref, kn_ref, vn_ref, lq1_ref, lk1_ref, lq2_ref, lk2_ref, sub_ref, o_ref = refs
        lam = (jnp.exp(jnp.sum(lq1_ref[...] * lk1_ref[...], axis=1, keepdims=True))
               - jnp.exp(jnp.sum(lq2_ref[...] * lk2_ref[...], axis=1, keepdims=True)) + lam_init)
    ts = q_ref.shape[1]
    past = kc_ref.shape[1]
    lane = lax.broadcasted_iota(jnp.int32, (1, LANES), 1)
    row = lax.broadcasted_iota(jnp.int32, (ts, ts), 0)
    col = lax.broadcasted_iota(jnp.int32, (ts, ts), 1)
    nt = (((1,), (1,)), ((), ()))
    for p in range(q_ref.shape[2] // LANES):
        sl = slice(p * LANES, (p + 1) * LANES)
        q = q_ref[0, :, sl]
        kc = kc_ref[0, :, sl].astype(BF16)
        kn = kn_ref[0, :, sl]
        vc = vc_ref[0, :, sl].astype(BF16)
        vn = vn_ref[0, :, sl].astype(BF16)
        if mode == "fox":
            kc = jnp.concatenate([kc, b_ref[0, :past, :]], axis=1)
            kn = jnp.concatenate([kn, b_ref[0, past:past + ts, :]], axis=1)
        zero = jnp.zeros_like(q)
        outs = []
        for i in range(2):
            qi = jnp.where(lane < HEAD_DIM, q, zero) if i == 0 else jnp.where(lane >= HEAD_DIM, q, zero)
            if mode == "fox":
                lo = BIAS_PIECES * (2 * p + i)
                hot = jnp.where((lane >= lo) & (lane < lo + BIAS_PIECES), 1.0, 0.0)
                qi = jnp.concatenate([qi, jnp.broadcast_to(hot, (ts, LANES)).astype(BF16)], axis=1)
            sc = lax.dot_general(qi, kc, nt, preferred_element_type=F32)
            sn = lax.dot_general(qi, kn, nt, preferred_element_type=F32)
            if mode == "fox":
                sn = jnp.where(col <= row, sn, NEG_INF)
            m = jnp.maximum(jnp.max(sc, axis=1, keepdims=True), jnp.max(sn, axis=1, keepdims=True))
            pc = jnp.exp2(sc - m)
            pn = jnp.exp2(sn - m)
            l = jnp.sum(pc, axis=1, keepdims=True) + jnp.sum(pn, axis=1, keepdims=True)
            outs.append((jnp.dot(pc.astype(BF16), vc, preferred_element_type=F32)
                         + jnp.dot(pn.astype(BF16), vn, preferred_element_type=F32)) / l)
        if mode == "fox":
            o = jnp.where(lane < HEAD_DIM, outs[0], outs[1])
        else:
            o = outs[0] - lam * outs[1]
            ms = jnp.mean(o * o, axis=1, keepdims=True)
            o = o * lax.rsqrt(ms + RMS_EPS) * sub_ref[...] * (1.0 - lam_init)
        o_ref[0, :, sl] = o.astype(o_ref.dtype)


def _decode_attention(mode, q, k_cache, v_cache, k_new, v_new, extra, lam_init=0.0):
    nb, ts, w = q.shape
    assert k_cache.shape[1] % CHUNK == 0 and ts <= CHUNK
    per_b = lambda a: pl.BlockSpec((1,) + a.shape[1:], lambda b: (b, 0, 0))
    if mode == "fox":
        extra_specs = [per_b(extra[0])]
    else:
        extra_specs = [pl.BlockSpec(a.shape, lambda b: (0, 0)) for a in extra]
    return pl.pallas_call(
        functools.partial(_decode_attn_kernel, mode=mode, lam_init=lam_init),
        grid=(nb,),
        in_specs=[per_b(q), per_b(k_cache), per_b(v_cache), per_b(k_new), per_b(v_new)] + extra_specs,
        out_specs=pl.BlockSpec((1, ts, w), lambda b: (b, 0, 0)),
        out_shape=jax.ShapeDtypeStruct((nb, ts, w), BF16),
        compiler_params=_cparams(("parallel",)),
        name="decode_" + mode,
    )(q, k_cache, v_cache, k_new, v_new, *extra)


def _outproj_ln_kernel(*refs, n_in, alpha):
    x_ref = refs[0]
    o_refs = refs[1:1 + n_in]
    w_refs = refs[1 + n_in:1 + 2 * n_in]
    g_ref, b_ref, y_ref = refs[1 + 2 * n_in:]
    mix = jnp.dot(o_refs[0][...], w_refs[0][...], preferred_element_type=F32)
    for o_r, w_r in zip(o_refs[1:], w_refs[1:]):
        mix = mix + jnp.dot(o_r[...], w_r[...], preferred_element_type=F32)
    y_ref[...] = _layer_norm(alpha * x_ref[...] + mix, g_ref[...], b_ref[...])


def _outproj_ln(x, outs, ws, g, b, alpha):
    n, d = x.shape
    tm = min(MOE_TILE, n)
    assert n % tm == 0
    row = lambda width: pl.BlockSpec((tm, width), lambda i: (i, 0))
    full = lambda a: pl.BlockSpec(a.shape, lambda i: (0, 0))
    return pl.pallas_call(
        functools.partial(_outproj_ln_kernel, n_in=len(outs), alpha=alpha),
        grid=(n // tm,),
        in_specs=[row(d)] + [row(o.shape[1]) for o in outs] + [full(w) for w in ws] + [full(g), full(b)],
        out_specs=row(d),
        out_shape=jax.ShapeDtypeStruct((n, d), F32),
        compiler_params=_cparams(("parallel",)),
        name="outproj_ln",
    )(x, *outs, *ws, g, b)


def _route(logits):
    lane = lax.broadcasted_iota(jnp.int32, logits.shape, 1).astype(F32)
    big = float(1 << 20)
    is_g = lane < N_GROUPS
    lg = jnp.where(is_g, logits, NEG_INF)
    eg = jnp.where(is_g, jnp.exp(lg - jnp.max(lg, axis=1, keepdims=True)), 0.0)
    pg = eg / jnp.sum(eg, axis=1, keepdims=True)
    p_g = jnp.max(pg, axis=1, keepdims=True)
    gidx = jnp.min(jnp.where(is_g & (pg == p_g), lane, big), axis=1, keepdims=True)
    lo = GATE_LANE0 + EXPERTS_PER_GROUP * gidx
    sel = (lane >= lo) & (lane < lo + EXPERTS_PER_GROUP)
    le = jnp.where(sel, logits, NEG_INF)
    ee = jnp.where(sel, jnp.exp(le - jnp.max(le, axis=1, keepdims=True)), 0.0)
    pe = ee / jnp.sum(ee, axis=1, keepdims=True)
    v1 = jnp.max(jnp.where(sel, pe, -1.0), axis=1, keepdims=True)
    i1 = jnp.min(jnp.where(sel & (pe == v1), lane, big), axis=1, keepdims=True)
    rest = sel & (lane != i1)
    v2 = jnp.max(jnp.where(rest, pe, -1.0), axis=1, keepdims=True)
    i2 = jnp.min(jnp.where(rest & (pe == v2), lane, big), axis=1, keepdims=True)
    tot = v1 + v2
    w1 = v1 / tot * p_g
    w2 = v2 / tot * p_g
    return jnp.where(lane == i1, w1, jnp.where(lane == i2, w2, 0.0))


def _moe_ln_kernel(x_ref, wrh_ref, wrl_ref, br_ref, w1_ref, w3_ref, w2_ref, g_ref, b_ref, y_ref,
                   xb_sc, gate_sc, acc_sc, *, alpha):
    e = pl.program_id(1)

    @pl.when(e == 0)
    def _():
        x = x_ref[...]
        xh = x.astype(BF16)
        xl = (x - xh.astype(F32)).astype(BF16)
        xb_sc[...] = xh
        logits = (jnp.dot(xh, wrh_ref[...], preferred_element_type=F32)
                  + jnp.dot(xl, wrh_ref[...], preferred_element_type=F32)
                  + jnp.dot(xh, wrl_ref[...], preferred_element_type=F32) + br_ref[...])
        gate_sc[...] = _route(logits)
        acc_sc[...] = jnp.zeros_like(acc_sc)

    xb = xb_sc[...]
    lane = lax.broadcasted_iota(jnp.int32, (1, LANES), 1)
    for s in range(w1_ref.shape[0]):
        h1 = jnp.dot(xb, w1_ref[s].astype(BF16), preferred_element_type=F32)
        h3 = jnp.dot(xb, w3_ref[s].astype(BF16), preferred_element_type=F32)
        hdn = (h1 * jax.nn.sigmoid(h1)) * h3
        y = jnp.dot(hdn.astype(BF16), w2_ref[s].astype(BF16), preferred_element_type=F32)
        expert_lane = e * w1_ref.shape[0] + s + GATE_LANE0
        ge = jnp.sum(jnp.where(lane == expert_lane, gate_sc[...], 0.0), axis=1, keepdims=True)
        acc_sc[...] += ge * y

    @pl.when(e == pl.num_programs(1) - 1)
    def _():
        y_ref[...] = _layer_norm(alpha * x_ref[...] + acc_sc[...], g_ref[...], b_ref[...])


def _moe_ln(x, wrh, wrl, br, w1, w3, w2, g, b, alpha):
    n, d = x.shape
    tm = min(MOE_TILE, n)
    assert n % tm == 0
    ne = w1.shape[0]
    assert ne % EXPERTS_PER_STEP == 0
    per_expert = lambda a: pl.BlockSpec((EXPERTS_PER_STEP,) + a.shape[1:], lambda i, e: (e, 0, 0))
    full = lambda a: pl.BlockSpec(a.shape, lambda i, e: (0, 0))
    return pl.pallas_call(
        functools.partial(_moe_ln_kernel, alpha=alpha),
        grid=(n // tm, ne // EXPERTS_PER_STEP),
        in_specs=[pl.BlockSpec((tm, d), lambda i, e: (i, 0)), full(wrh), full(wrl), full(br),
                  per_expert(w1), per_expert(w3), per_expert(w2),
                  full(g), full(b)],
        out_specs=pl.BlockSpec((tm, d), lambda i, e: (i, 0)),
        out_shape=jax.ShapeDtypeStruct((n, d), F32),
        scratch_shapes=[pltpu.VMEM((tm, d), BF16), pltpu.VMEM((tm, LANES), F32), pltpu.VMEM((tm, d), F32)],
        compiler_params=_cparams(("parallel", "arbitrary"), MOE_VMEM_LIMIT),
        name="moe_ln",
    )(x, wrh, wrl, br, w1, w3, w2, g, b)


def _proj_c_kernel(x_ref, win_ref, gq_ref, gkv_ref, wuq_ref, wrot_ref, cq_ref, sq_ref,
                   ck_ref, s1k_ref, s2k_ref, q_ref, ckv_ref, kr_ref):
    xb = x_ref[0].astype(BF16)
    h = jnp.dot(xb, win_ref[...], preferred_element_type=F32)
    qa = h[:, :Q_RANK]
    kva = h[:, Q_RANK:Q_RANK + KV_RANK]
    krw = h[:, Q_RANK + KV_RANK:]
    qn = qa * lax.rsqrt(jnp.mean(qa * qa, axis=1, keepdims=True) + RMS_EPS) * gq_ref[...]
    ckv_ref[0] = kva * lax.rsqrt(jnp.mean(kva * kva, axis=1, keepdims=True) + RMS_EPS) * gkv_ref[...]
    half = ROPE_DIM // 2
    kr = _rope3(krw, ck_ref[...], s1k_ref[...], s2k_ref[...], LANES - half, half)
    kr_ref[0] = kr[:, :ROPE_DIM]
    qnb = qn.astype(BF16)
    q = jnp.dot(qnb, wuq_ref[...], preferred_element_type=F32)
    q_rot = jnp.dot(qnb, wrot_ref[...], preferred_element_type=F32)
    cq, sq = cq_ref[...], sq_ref[...]
    scale = (NOPE_DIM + ROPE_DIM) ** -0.5 * LOG2E
    for hd in range(H_C):
        sl = slice(hd * LANES, (hd + 1) * LANES)
        q_ref[0, :, sl] = ((q[:, sl] * cq + q_rot[:, sl] * sq) * scale).astype(BF16)


def _proj_c(x, win, gq, gkv, wuq, wrot, tabs_q, tabs_k):
    nb, t, _ = x.shape
    tm = min(MOE_TILE, t)
    assert t % tm == 0
    tok = lambda width: pl.BlockSpec((1, tm, width), lambda b, i: (b, i, 0))
    tab = pl.BlockSpec((tm, LANES), lambda b, i: (i, 0))
    full = lambda a: pl.BlockSpec(a.shape, lambda b, i: (0, 0))
    return pl.pallas_call(
        _proj_c_kernel,
        grid=(nb, t // tm),
        in_specs=[tok(D_MODEL), full(win), full(gq), full(gkv), full(wuq), full(wrot)] + [tab] * 5,
        out_specs=[tok(H_C * LANES), tok(KV_RANK), tok(ROPE_DIM)],
        out_shape=[jax.ShapeDtypeStruct((nb, t, H_C * LANES), BF16),
                   jax.ShapeDtypeStruct((nb, t, KV_RANK), F32),
                   jax.ShapeDtypeStruct((nb, t, ROPE_DIM), F32)],
        compiler_params=_cparams(("parallel", "parallel")),
        name="proj_c",
    )(x, win, gq, gkv, wuq, wrot, *tabs_q, *tabs_k)


def _kv_up_kernel(ckv_ref, kr_ref, wk_ref, place_ref, wvt_ref, k_ref, vt_ref):
    cb = ckv_ref[...].astype(BF16)
    k = (jnp.dot(cb, wk_ref[...], preferred_element_type=F32)
         + jnp.dot(kr_ref[...].astype(BF16), place_ref[...], preferred_element_type=F32))
    k_ref[0] = k.astype(BF16)
    vt = lax.dot_general(wvt_ref[...], cb, (((1,), (1,)), ((), ())), preferred_element_type=F32)
    vt_ref[0] = vt.astype(BF16)


def _kv_up(ckv, kr, wk, place, wvt):
    n = ckv.shape[0]
    tm = ATTN_BLOCK
    assert n % tm == 0
    row = lambda width: pl.BlockSpec((tm, width), lambda i: (i, 0))
    full = lambda a: pl.BlockSpec(a.shape, lambda i: (0, 0))
    return pl.pallas_call(
        _kv_up_kernel,
        grid=(n // tm,),
        in_specs=[row(KV_RANK), row(ROPE_DIM), full(wk), full(place), full(wvt)],
        out_specs=[pl.BlockSpec((1, tm, H_C * LANES), lambda i: (i, 0, 0)),
                   pl.BlockSpec((1, H_C * V_DIM_C, tm), lambda i: (i, 0, 0))],
        out_shape=[jax.ShapeDtypeStruct((n // tm, tm, H_C * LANES), BF16),
                   jax.ShapeDtypeStruct((n // tm, H_C * V_DIM_C, tm), BF16)],
        compiler_params=_cparams(("parallel",)),
        name="kv_up",
    )(ckv, kr, wk, place, wvt)


def _mla_decode_kernel(q_ref, ckv_ref, kr_ref, ckvn_ref, krn_ref, wabs_ref, wv_ref, o_ref):
    q = q_ref[0]
    ts = q.shape[0]
    qs = jnp.concatenate(
        [jnp.dot(q[:, h * LANES:(h + 1) * LANES], wabs_ref[h], preferred_element_type=F32).astype(BF16)
         for h in range(H_C)], axis=0)
    kc = jnp.concatenate([ckv_ref[0].astype(BF16), kr_ref[0].astype(BF16)], axis=1)
    kn = jnp.concatenate([ckvn_ref[0].astype(BF16), krn_ref[0].astype(BF16)], axis=1)
    nt = (((1,), (1,)), ((), ()))
    sc = lax.dot_general(qs, kc, nt, preferred_element_type=F32)
    sn = lax.dot_general(qs, kn, nt, preferred_element_type=F32)
    m = jnp.maximum(jnp.max(sc, axis=1, keepdims=True), jnp.max(sn, axis=1, keepdims=True))
    pc = jnp.exp2(sc - m)
    pn = jnp.exp2(sn - m)
    l = jnp.sum(pc, axis=1, keepdims=True) + jnp.sum(pn, axis=1, keepdims=True)
    ol = (jnp.dot(pc.astype(BF16), kc[:, :KV_RANK], preferred_element_type=F32)
          + jnp.dot(pn.astype(BF16), kn[:, :KV_RANK], preferred_element_type=F32)) / l
    olb = ol.astype(BF16)
    o = jnp.dot(olb[:ts], wv_ref[0], preferred_element_type=F32)
    for h in range(1, H_C):
        o = o + jnp.dot(olb[h * ts:(h + 1) * ts], wv_ref[h], preferred_element_type=F32)
    o_ref[0] = o.astype(o_ref.dtype)


def _mla_decode(q, ckv_c, kr_c, ckv_n, kr_n, w_abs, w_vout):
    nb, ts, _ = q.shape
    assert ckv_c.shape[1] % CHUNK == 0 and ts <= CHUNK
    per_b = lambda a: pl.BlockSpec((1,) + a.shape[1:], lambda b: (b, 0, 0))
    full = lambda a: pl.BlockSpec(a.shape, lambda b: (0, 0, 0))
    return pl.pallas_call(
        _mla_decode_kernel,
        grid=(nb,),
        in_specs=[per_b(q), per_b(ckv_c), per_b(kr_c), per_b(ckv_n), per_b(kr_n), full(w_abs), full(w_vout)],
        out_specs=pl.BlockSpec((1, ts, H_C * V_DIM_C), lambda b: (b, 0, 0)),
        out_shape=jax.ShapeDtypeStruct((nb, ts, H_C * V_DIM_C), BF16),
        compiler_params=_cparams(("parallel",)),
        name="mla_decode",
    )(q, ckv_c, kr_c, ckv_n, kr_n, w_abs, w_vout)


def _rope_tables(pos, dim, lane0):
    half = dim // 2
    inv = ROPE_THETA ** (-jnp.arange(0, dim, 2, dtype=F32) / dim)
    ang = pos.astype(F32)[:, None] * inv[None, :]
    cos, sin = jnp.cos(ang), jnp.sin(ang)
    zero = jnp.zeros_like(sin)
    c = jnp.concatenate([cos, cos], axis=1)
    s1 = jnp.concatenate([-sin, zero], axis=1)
    s2 = jnp.concatenate([zero, sin], axis=1)
    if lane0 < 0:
        reps = LANES // dim
        return tuple(jnp.tile(a, (1, reps)) for a in (c, s1, s2))
    t = pos.shape[0]
    pad = lambda a, fill: jnp.concatenate(
        [jnp.full((t, lane0), fill, F32), a, jnp.full((t, LANES - lane0 - dim), fill, F32)], axis=1)
    return pad(c, 1.0), pad(s1, 0.0), pad(s2, 0.0)


def _pad_cols(a, width):
    return jnp.pad(a, ((0, 0), (0, width - a.shape[1])))


def _blocks(a, tk):
    nb, t, l = a.shape
    return a.reshape(nb, t // tk, tk, l)


def _cat_pad_time(cache, new, t_pad):
    nb, t0, l = cache.shape
    t1 = new.shape[1]
    return jnp.concatenate([cache, new, jnp.zeros((nb, t_pad - t0 - t1, l), cache.dtype)], axis=1)


def kernel(x_prompt, x_sample, cache_fox_k, cache_fox_v, cache_fox_logf, cache_diff_k, cache_diff_v, cache_mla_ckv, cache_mla_krope, w_in_ab, b_fgate, diff_lq1, diff_lk1, diff_lq2, diff_lk2, diff_subln, w_out_ab, w_in_c, mla_q_norm, mla_kv_norm, mla_w_uq, mla_w_ukv, w_out_c, ln1_g, ln1_b, ln2_g, ln2_b, moe_wg, moe_bg, moe_we, moe_be, moe_w1, moe_w3, moe_w2):
    bp, tp, d = x_prompt.shape
    bs, ts, _ = x_sample.shape
    past = cache_fox_k.shape[2]
    depth = ln1_g.shape[0]
    alpha = (2 * depth) ** 0.25
    tk = ATTN_BLOCK
    assert past % tk == 0
    ns = bs * ts
    t_dec = past + tk

    pos_p = jnp.arange(tp)
    pos_s = jnp.tile(past + jnp.arange(ts), bs)

    xp = x_prompt
    xs = x_sample.reshape(1, ns, d)
    out_ab_p, out_ab_s, out_c_p, out_c_s = [], [], [], []

    for i in range(depth):
        j = i // 2
        if i % 2 == 0:
            lam_init = 0.8 - 0.6 * math.exp(-0.3 * i)
            cuts = [0, A_WIDTH, 2 * A_WIDTH, 3 * A_WIDTH, 3 * A_WIDTH + H_A,
                    3 * A_WIDTH + H_A + B_QK_WIDTH, 3 * A_WIDTH + H_A + 2 * B_QK_WIDTH,
                    3 * A_WIDTH + H_A + 2 * B_QK_WIDTH + B_V_WIDTH]
            w = w_in_ab[j]
            piece = lambda a: w[:, cuts[a]:cuts[a + 1]]
            w6 = jnp.stack([piece(0), piece(1), piece(2), piece(4), piece(5), piece(6)]).astype(BF16)
            wvt = jnp.stack([piece(2).T, piece(6).T]).astype(BF16)
            wf = _pad_cols(piece(3), LANES).astype(BF16)
            bf = _pad_cols(b_fgate[j][None, :], LANES)
            wout = w_out_ab[j].astype(BF16)
            diff_extra = (diff_lq1[j][None, :], diff_lk1[j][None, :], diff_lq2[j][None, :],
                          diff_lk2[j][None, :], diff_subln[j][None, :])

            tabs = _rope_tables(pos_p, HEAD_DIM, -1)
            (qa, ka, kab, va, vat, lf, lfw, qb, kb, kbb, vb, vbt) = _proj_ab(xp, w6, wvt, wf, bf, tabs)
            bias = _blocks(_decay_bias(lfw), tk)
            oa = _attention("fox", qa, _blocks(kab, tk), vat, (bias,), n_pairs=H_A // 2, mask_shift=0)
            ob = _attention("diff", qb, _blocks(kbb, tk), vbt, diff_extra,
                            n_pairs=H_B, mask_shift=int(math.log2(CHUNK)), lam_init=lam_init)
            out_ab_p.append((ka.reshape(bp, tp, H_A, HEAD_DIM), va.reshape(bp, tp, H_A, HEAD_DIM), lf,
                             kb.reshape(bp, tp, H_B, 2, HEAD_DIM), vb.reshape(bp, tp, H_B, 2 * HEAD_DIM)))
            xp2 = _outproj_ln(xp.reshape(bp * tp, d), [oa.reshape(bp * tp, -1), ob.reshape(bp * tp, -1)],
                              [wout[:A_WIDTH], wout[A_WIDTH:]], ln1_g[i][None, :], ln1_b[i][None, :], alpha)

            tabs = _rope_tables(pos_s, HEAD_DIM, -1)
            (qa, ka, kab, va, _, lf, lfw, qb, kb, kbb, vb, _) = _proj_ab(xs, w6, wvt, wf, bf, tabs)
            rs = lambda a: a.reshape(bs, ts, a.shape[-1])
            cache_lfw = jnp.pad(cache_fox_logf[j].astype(F32), ((0, 0), (0, 0), (0, LANES - H_A)))
            bias = _decay_bias(_cat_pad_time(cache_lfw, rs(lfw), t_dec))
            flat = lambda c: c.reshape(bs, past, -1)
            oa = _decode_attention("fox", rs(qa), flat(cache_fox_k[j]), flat(cache_fox_v[j]), rs(kab), rs(va),
                                   (bias,))
            ob = _decode_attention("diff", rs(qb), flat(cache_diff_k[j]), flat(cache_diff_v[j]), rs(kbb),
                                   rs(vb), diff_extra, lam_init=lam_init)
            out_ab_s.append((ka.reshape(bs, ts, H_A, HEAD_DIM), va.reshape(bs, ts, H_A, HEAD_DIM),
                             lf.reshape(bs, ts, H_A), kb.reshape(bs, ts, H_B, 2, HEAD_DIM),
                             vb.reshape(bs, ts, H_B, 2 * HEAD_DIM)))
            xs2 = _outproj_ln(xs.reshape(ns, d), [oa.reshape(ns, -1), ob.reshape(ns, -1)],
                              [wout[:A_WIDTH], wout[A_WIDTH:]], ln1_g[i][None, :], ln1_b[i][None, :], alpha)
        else:
            wc = w_in_c[j]
            kr_cols = _pad_cols(wc[:, Q_RANK + KV_RANK:], LANES)
            win = jnp.concatenate([wc[:, :Q_RANK + KV_RANK], kr_cols], axis=1).astype(BF16)
            wuq3 = jnp.pad(mla_w_uq[j].reshape(Q_RANK, H_C, NOPE_DIM + ROPE_DIM),
                           ((0, 0), (0, 0), (0, LANES - NOPE_DIM - ROPE_DIM)))
            wuq = wuq3.reshape(Q_RANK, H_C * LANES).astype(BF16)
            r0, r1, r2 = NOPE_DIM, NOPE_DIM + ROPE_DIM // 2, NOPE_DIM + ROPE_DIM
            wrot = jnp.zeros_like(wuq3).at[:, :, r0:r1].set(-wuq3[:, :, r1:r2]).at[:, :, r1:r2].set(wuq3[:, :, r0:r1])
            wrot = wrot.reshape(Q_RANK, H_C * LANES).astype(BF16)
            q_tabs = lambda pos: (lambda c, s1, s2: (c, s2 - s1))(*_rope_tables(pos, ROPE_DIM, NOPE_DIM))
            wukv = mla_w_ukv[j].reshape(KV_RANK, H_C, NOPE_DIM + V_DIM_C)
            wk = jnp.pad(wukv[:, :, :NOPE_DIM], ((0, 0), (0, 0), (0, LANES - NOPE_DIM)))
            wk = wk.reshape(KV_RANK, H_C * LANES).astype(BF16)
            wvt = wukv[:, :, NOPE_DIM:].reshape(KV_RANK, H_C * V_DIM_C).T.astype(BF16)
            place = jnp.tile(_pad_cols(jnp.concatenate(
                [jnp.zeros((ROPE_DIM, NOPE_DIM), F32), jnp.eye(ROPE_DIM, dtype=F32)], axis=1), LANES),
                (1, H_C)).astype(BF16)
            gq = mla_q_norm[j][None, :]
            gkv = mla_kv_norm[j][None, :]
            wout = w_out_c[j].astype(BF16)

            q, ckv, kr = _proj_c(xp, win, gq, gkv, wuq, wrot, q_tabs(pos_p), _rope_tables(pos_p, ROPE_DIM, 0))
            kc, vct = _kv_up(ckv.reshape(bp * tp, KV_RANK), kr.reshape(bp * tp, ROPE_DIM), wk, place, wvt)
            per_seq = lambda a, nb: a.reshape((nb, a.shape[0] // nb) + a.shape[1:])
            oc = _attention("mla", q, per_seq(kc, bp), per_seq(vct, bp), (), n_pairs=H_C // 2,
                            mask_shift=int(math.log2(CHUNK)))
            out_c_p.append((ckv, kr))
            xp2 = _outproj_ln(xp.reshape(bp * tp, d), [oc.reshape(bp * tp, -1)], [wout],
                              ln1_g[i][None, :], ln1_b[i][None, :], alpha)

            q, ckv, kr = _proj_c(xs, win, gq, gkv, wuq, wrot, q_tabs(pos_s), _rope_tables(pos_s, ROPE_DIM, 0))
            w_abs = jnp.zeros((H_C, LANES, 2 * LANES), F32)
            w_abs = w_abs.at[:, :NOPE_DIM, :KV_RANK].set(jnp.transpose(wukv[:, :, :NOPE_DIM], (1, 2, 0)))
            w_abs = w_abs.at[:, NOPE_DIM:NOPE_DIM + ROPE_DIM, KV_RANK:KV_RANK + ROPE_DIM].set(
                jnp.eye(ROPE_DIM, dtype=F32))
            w_vout = jnp.einsum("khd,hg->hkgd", wukv[:, :, NOPE_DIM:], jnp.eye(H_C, dtype=F32))
            w_vout = w_vout.reshape(H_C, KV_RANK, H_C * V_DIM_C)
            wide = lambda a: jnp.pad(a.astype(F32), ((0, 0), (0, 0), (0, LANES - ROPE_DIM)))
            oc = _mla_decode(q.reshape(bs, ts, -1), cache_mla_ckv[j].astype(F32), wide(cache_mla_krope[j]),
                             ckv.reshape(bs, ts, KV_RANK), wide(kr.reshape(bs, ts, ROPE_DIM)),
                             w_abs.astype(BF16), w_vout.astype(BF16))
            out_c_s.append((ckv.reshape(bs, ts, KV_RANK), kr.reshape(bs, ts, ROPE_DIM)))
            xs2 = _outproj_ln(xs.reshape(ns, d), [oc.reshape(ns, -1)], [wout],
                              ln1_g[i][None, :], ln1_b[i][None, :], alpha)

        wr = _pad_cols(jnp.concatenate(
            [moe_wg[i]] + [moe_we[i][gi] for gi in range(N_GROUPS)], axis=1), LANES)
        wrh = wr.astype(BF16)
        wrl = (wr - wrh.astype(F32)).astype(BF16)
        br = _pad_cols(jnp.concatenate([moe_bg[i], moe_be[i].reshape(-1)])[None, :], LANES)
        moe_w = (moe_w1[i], moe_w3[i], moe_w2[i])
        g2, b2 = ln2_g[i][None, :], ln2_b[i][None, :]
        xp = _moe_ln(xp2, wrh, wrl, br, *moe_w, g2, b2, alpha).reshape(bp, tp, d)
        xs = _moe_ln(xs2, wrh, wrl, br, *moe_w, g2, b2, alpha).reshape(1, ns, d)

    stack = lambda rows, n: jnp.stack([r[n] for r in rows])
    return (xp, xs.reshape(bs, ts, d),
            stack(out_ab_p, 0), stack(out_ab_p, 1), stack(out_ab_p, 2), stack(out_ab_p, 3), stack(out_ab_p, 4),
            stack(out_c_p, 0), stack(out_c_p, 1),
            stack(out_ab_s, 0), stack(out_ab_s, 1), stack(out_ab_s, 2), stack(out_ab_s, 3), stack(out_ab_s, 4),
            stack(out_c_s, 0), stack(out_c_s, 1))
```
